```python
import math
import jax, jax.numpy as jnp
from jax import lax
import numpy as np

D_MODEL = 1024
BATCH = 4
SEQ = 4096
DEPTH = 2
DEC_BATCH = 128
DEC_SEQ = 4
PAST_LEN = 8192
PAGE_SIZE = 128

N_MIXERS = 2
N_ATTN_LAYERS = (DEPTH + N_MIXERS - 1) // N_MIXERS
N_RWKV_LAYERS = DEPTH // N_MIXERS
N_META = 16
RMS_EPS = 1e-6

A_HEADS = 16
A_KV_HEADS = 4
A_GROUP = A_HEADS // A_KV_HEADS
A_HEAD_DIM = 64
A_WIDTH = A_HEADS * A_HEAD_DIM
A_KV_WIDTH = A_KV_HEADS * A_HEAD_DIM
WINDOW = 128
BLOCK = 128
N_BUCKETS = 32
MAX_DISTANCE = 128

R_HEAD_DIM = 64
R_HEADS = D_MODEL // R_HEAD_DIM
R_WIDTH = R_HEADS * R_HEAD_DIM
DECAY_LORA = 64
ICLR_LORA = 64
GN_EPS = 64e-5
N_SHIFT_MIX = 6

kernel_name = "hybrid_swa_sink_rwkv7_step"


def rmsnorm(x, g):
    xf = x.astype(jnp.float32)
    y = xf * lax.rsqrt(jnp.mean(xf * xf, axis=-1, keepdims=True) + RMS_EPS)
    return (y * g.astype(jnp.float32)).astype(x.dtype)


def t5_bucket(rel):
    n = jnp.maximum(rel, 0)
    max_exact = N_BUCKETS // 2
    nf = jnp.maximum(n, max_exact).astype(jnp.float32)
    large = max_exact + (jnp.log(nf / max_exact) / math.log(MAX_DISTANCE / max_exact)
                         * (N_BUCKETS - max_exact)).astype(jnp.int32)
    large = jnp.minimum(large, N_BUCKETS - 1)
    return jnp.where(n < max_exact, n, large)


def rel_bias(table, rel):
    b = table[t5_bucket(rel)].astype(jnp.float32)
    return jnp.moveaxis(b, -1, 0).reshape(A_KV_HEADS, A_GROUP, *rel.shape)


def sink_attend(q, k, v, bias, valid, sinks):
    s = jnp.einsum('...qkgd,...skd->...kgqs', q, k).astype(jnp.float32) * (A_HEAD_DIM ** -0.5) + bias
    s = jnp.where(valid[..., None, None, :, :], s, -jnp.inf)
    sink = sinks.astype(jnp.float32).reshape(A_KV_HEADS, A_GROUP, 1, 1)
    m = jnp.maximum(jnp.max(s, axis=-1, keepdims=True), sink)
    p = jnp.exp(s - m)
    denom = jnp.sum(p, axis=-1, keepdims=True) + jnp.exp(sink - m)
    o = jnp.einsum('...kgqs,...skd->...qkgd', (p / denom).astype(v.dtype), v)
    return o.reshape(*o.shape[:-3], A_WIDTH)


def attn_project(xn, w_in):
    B, T = xn.shape[:2]
    proj = jnp.einsum('btd,de->bte', xn, w_in)
    q, k, v, g = jnp.split(proj, [A_WIDTH, A_WIDTH + A_KV_WIDTH, A_WIDTH + 2 * A_KV_WIDTH], axis=-1)
    return (q.reshape(B, T, A_KV_HEADS, A_GROUP, A_HEAD_DIM),
            k.reshape(B, T, A_KV_HEADS, A_HEAD_DIM),
            v.reshape(B, T, A_KV_HEADS, A_HEAD_DIM), g)


def attn_prompt(xn, w_in, sinks, w_out, table):
    q, k, v, g = attn_project(xn, w_in)
    B, L = xn.shape[:2]
    pad = BLOCK - N_META
    nb = (L + pad) // BLOCK

    def blocks(t):
        t = jnp.pad(t, [(0, 0), (pad, 0)] + [(0, 0)] * (t.ndim - 2))
        return t.reshape(B, nb, BLOCK, *t.shape[2:])

    def with_prev(t):
        prev = jnp.pad(t, [(0, 0), (1, 0)] + [(0, 0)] * (t.ndim - 2))[:, :-1]
        return jnp.concatenate([prev, t], axis=2)

    qb = blocks(q)
    kc = with_prev(blocks(k))
    vc = with_prev(blocks(v))
    qi = jnp.arange(BLOCK)[:, None]
    kj = jnp.arange(2 * BLOCK)[None, :]
    rel = BLOCK + qi - kj
    kpos = (jnp.arange(nb)[:, None, None] - 1) * BLOCK + kj[None] - pad
    valid = (rel >= 0) & (rel < WINDOW) & (kpos >= 0)
    o = sink_attend(qb, kc, vc, rel_bias(table, rel), valid, sinks)
    o = o.reshape(B, nb * BLOCK, A_WIDTH)[:, pad:]
    y = jnp.einsum('bte,ed->btd', o * jax.nn.silu(g), w_out)
    return y, k[:, -WINDOW:], v[:, -WINDOW:]


def attn_sample(xn, cache_k, cache_v, w_in, sinks, w_out, table):
    q, k, v, g = attn_project(xn, w_in)
    T = xn.shape[1]
    keep = cache_k.shape[1]
    kc = jnp.concatenate([cache_k.astype(k.dtype), k], axis=1)
    vc = jnp.concatenate([cache_v.astype(v.dtype), v], axis=1)
    rel = keep + jnp.arange(T)[:, None] - jnp.arange(keep + T)[None, :]
    valid = (rel >= 0) & (rel < WINDOW)
    o = sink_attend(q, kc, vc, rel_bias(table, rel), valid, sinks)
    y = jnp.einsum('bte,ed->btd', o * jax.nn.silu(g), w_out)
    return y, kc[:, -keep:], vc[:, -keep:]


def wkv_scan(S0, r, w, k, v, a, b):
    xs = tuple(jnp.moveaxis(t.astype(jnp.float32), 1, 0) for t in (r, w, k, v, a, b))

    def step(S, inp):
        r_t, w_t, k_t, v_t, a_t, b_t = inp
        sa = jnp.einsum('bhvk,bhk->bhv', S, a_t)
        S = S * w_t[:, :, None, :] + sa[..., None] * b_t[:, :, None, :] + v_t[..., None] * k_t[:, :, None, :]
        return S, jnp.einsum('bhvk,bhk->bhv', S, r_t)

    S, ys = lax.scan(step, S0, xs)
    return S, jnp.moveaxis(ys, 0, 1)


def rwkv_time_mix(xn, wkv0, shift0, mu, w_in, w0, w1, w2, a0, a1, a2, k_k, k_a, r_k,
                  ln_g, ln_b, w_out):
    B, T, _ = xn.shape
    f32 = jnp.float32
    xprev = jnp.concatenate([shift0[:, None].astype(xn.dtype), xn[:, :-1]], axis=1)
    xmix = xn[None] + (xprev - xn)[None] * mu[:, None, None, :].astype(xn.dtype)
    proj = jnp.einsum('cbtd,cdw->cbtw', xmix[:4], w_in)
    r, k, v, g = proj[0], proj[1], proj[2], proj[3]
    xw, xa = xmix[4], xmix[5]
    z = (w0 + jnp.tanh(xw @ w1) @ w2).astype(f32)
    log_w = -jax.nn.softplus(-z) - 0.5
    decay = jnp.exp(-jnp.exp(log_w))
    a = jax.nn.sigmoid((a0 + (xa @ a1) @ a2).astype(f32))

    def hd(t):
        return t.reshape(B, T, R_HEADS, R_HEAD_DIM)

    kk = hd((k * k_k).astype(f32))
    kk = kk / jnp.maximum(jnp.sqrt(jnp.sum(kk * kk, axis=-1, keepdims=True)), 1e-12)
    k_h = hd((k.astype(f32) * (1.0 + (a - 1.0) * k_a.astype(f32))))
    r_h, v_h, a_h, w_h = hd(r.astype(f32)), hd(v.astype(f32)), hd(a), hd(decay)
    S, y = wkv_scan(wkv0.astype(f32), r_h, w_h, k_h, v_h, -kk, kk * a_h)
    mean = jnp.mean(y, axis=-1, keepdims=True)
    var = jnp.mean(jnp.square(y - mean), axis=-1, keepdims=True)
    y = ((y - mean) * lax.rsqrt(var + GN_EPS)).reshape(B, T, R_WIDTH)
    y = y * ln_g.astype(f32) + ln_b.astype(f32)
    bonus = jnp.sum(r_h * k_h * r_k.astype(f32), axis=-1, keepdims=True) * v_h
    y = (y + bonus.reshape(B, T, R_WIDTH)) * jax.nn.silu(g.astype(f32))
    out = jnp.einsum('btw,wd->btd', y.astype(xn.dtype), w_out)
    return out, S, xn[:, -1]


def run_trunk(h, p, win_k, win_v, wkv, shift, is_prompt):
    new_k, new_v, new_wkv, new_shift = [], [], [], []
    for i in range(DEPTH):
        j = i // N_MIXERS
        xn = rmsnorm(h, p['norm_gain'][i])
        if i % N_MIXERS == 0:
            if is_prompt:
                y, ck, cv = attn_prompt(xn, p['attn_w_in'][j], p['attn_sinks'][j], p['attn_w_out'][j], p['table'])
            else:
                y, ck, cv = attn_sample(xn, win_k[j], win_v[j], p['attn_w_in'][j], p['attn_sinks'][j],
                                        p['attn_w_out'][j], p['table'])
            new_k.append(ck)
            new_v.append(cv)
        else:
            y, s, sh = rwkv_time_mix(xn, wkv[j], shift[j], p['mu'][j], p['r_w_in'][j], p['w0'][j], p['w1'][j],
                                     p['w2'][j], p['a0'][j], p['a1'][j], p['a2'][j], p['k_k'][j], p['k_a'][j],
                                     p['r_k'][j], p['ln_g'][j], p['ln_b'][j], p['r_w_out'][j])
            new_wkv.append(s)
            new_shift.append(sh)
        h = h + y
    return (rmsnorm(h, p['final_gain']), jnp.stack(new_k), jnp.stack(new_v),
            jnp.stack(new_wkv), jnp.stack(new_shift))


def setup_inputs(seed: int = 0) -> dict:
    key = jax.random.key(seed)
    ks = jax.random.split(key, 32)
    nrm = jax.random.normal
    f32 = jnp.float32
    NA, NR = N_ATTN_LAYERS, N_RWKV_LAYERS
    return {
        "x_prompt": nrm(ks[0], (BATCH, SEQ, D_MODEL), f32),
        "x_sample": nrm(ks[1], (DEC_BATCH, DEC_SEQ, D_MODEL), f32),
        "cache_win_k": nrm(ks[2], (NA, DEC_BATCH, min(WINDOW, PAST_LEN), A_KV_HEADS, A_HEAD_DIM), f32),
        "cache_win_v": nrm(ks[3], (NA, DEC_BATCH, min(WINDOW, PAST_LEN), A_KV_HEADS, A_HEAD_DIM), f32),
        "state_wkv": 0.3 * nrm(ks[4], (NR, DEC_BATCH, R_HEADS, R_HEAD_DIM, R_HEAD_DIM), f32),
        "state_shift": nrm(ks[5], (NR, DEC_BATCH, D_MODEL), f32),
        "meta_tokens": nrm(ks[6], (N_META, D_MODEL), f32),
        "rel_bias_table": 0.5 * nrm(ks[7], (N_BUCKETS, A_HEADS), f32),
        "norm_gain": 1.0 + 0.05 * nrm(ks[8], (DEPTH, D_MODEL), f32),
        "final_gain": 1.0 + 0.05 * nrm(ks[9], (D_MODEL,), f32),
        "attn_w_in": nrm(ks[10], (NA, D_MODEL, 2 * A_WIDTH + 2 * A_KV_WIDTH), f32) * D_MODEL ** -0.5,
        "attn_sinks": 0.5 * nrm(ks[11], (NA, A_HEADS), f32),
        "attn_w_out": nrm(ks[12], (NA, A_WIDTH, D_MODEL), f32) * A_WIDTH ** -0.5,
        "rwkv_mu": jax.random.uniform(ks[13], (NR, N_SHIFT_MIX, D_MODEL), f32),
        "rwkv_w_in": nrm(ks[14], (NR, 4, D_MODEL, R_WIDTH), f32) * D_MODEL ** -0.5,
        "rwkv_w0": -1.0 + 0.5 * nrm(ks[15], (NR, R_WIDTH), f32),
        "rwkv_w1": nrm(ks[16], (NR, D_MODEL, DECAY_LORA), f32) * D_MODEL ** -0.5,
        "rwkv_w2": 0.5 * nrm(ks[17], (NR, DECAY_LORA, R_WIDTH), f32) * DECAY_LORA ** -0.5,
        "rwkv_a0": 0.5 * nrm(ks[18], (NR, R_WIDTH), f32),
        "rwkv_a1": nrm(ks[19], (NR, D_MODEL, ICLR_LORA), f32) * D_MODEL ** -0.5,
        "rwkv_a2": 0.5 * nrm(ks[20], (NR, ICLR_LORA, R_WIDTH), f32) * ICLR_LORA ** -0.5,
        "rwkv_k_k": 0.85 + 0.05 * nrm(ks[21], (NR, R_WIDTH), f32),
        "rwkv_k_a": 1.0 + 0.05 * nrm(ks[22], (NR, R_WIDTH), f32),
        "rwkv_r_k": 0.1 * nrm(ks[23], (NR, R_HEADS, R_HEAD_DIM), f32),
        "rwkv_ln_gamma": 1.0 + 0.05 * nrm(ks[24], (NR, R_WIDTH), f32),
        "rwkv_ln_beta": 0.02 * nrm(ks[25], (NR, R_WIDTH), f32),
        "rwkv_w_out": nrm(ks[26], (NR, R_WIDTH, D_MODEL), f32) * R_WIDTH ** -0.5,
    }


def reference(x_prompt, x_sample, cache_win_k, cache_win_v, state_wkv, state_shift,
              meta_tokens, rel_bias_table, norm_gain, final_gain,
              attn_w_in, attn_sinks, attn_w_out,
              rwkv_mu, rwkv_w_in, rwkv_w0, rwkv_w1, rwkv_w2, rwkv_a0, rwkv_a1, rwkv_a2,
              rwkv_k_k, rwkv_k_a, rwkv_r_k, rwkv_ln_gamma, rwkv_ln_beta, rwkv_w_out):
    p = dict(table=rel_bias_table, norm_gain=norm_gain, final_gain=final_gain,
             attn_w_in=attn_w_in, attn_sinks=attn_sinks, attn_w_out=attn_w_out,
             mu=rwkv_mu, r_w_in=rwkv_w_in, w0=rwkv_w0, w1=rwkv_w1, w2=rwkv_w2,
             a0=rwkv_a0, a1=rwkv_a1, a2=rwkv_a2, k_k=rwkv_k_k, k_a=rwkv_k_a, r_k=rwkv_r_k,
             ln_g=rwkv_ln_gamma, ln_b=rwkv_ln_beta, r_w_out=rwkv_w_out)
    B = x_prompt.shape[0]
    meta = jnp.broadcast_to(meta_tokens[None].astype(x_prompt.dtype), (B, N_META, D_MODEL))
    h_prompt = jnp.concatenate([meta, x_prompt], axis=1)
    wkv_zero = jnp.zeros((N_RWKV_LAYERS, B, R_HEADS, R_HEAD_DIM, R_HEAD_DIM), jnp.float32)
    shift_zero = jnp.zeros((N_RWKV_LAYERS, B, D_MODEL), x_prompt.dtype)
    out_p, new_win_k_prompt, new_win_v_prompt, new_wkv_prompt, new_shift_prompt = run_trunk(
        h_prompt, p, None, None, wkv_zero, shift_zero, True)
    y_prompt = out_p[:, N_META:]
    y_sample, new_win_k_sample, new_win_v_sample, new_wkv_sample, new_shift_sample = run_trunk(
        x_sample, p, cache_win_k, cache_win_v, state_wkv, state_shift, False)
    return (y_prompt, y_sample,
            new_win_k_prompt, new_win_v_prompt, new_wkv_prompt, new_shift_prompt,
            new_win_k_sample, new_win_v_sample, new_wkv_sample, new_shift_sample)
```

```python
import functools
import math

import numpy as np
import jax
import jax.numpy as jnp
from jax import lax
from jax.experimental import pallas as pl
from jax.experimental.pallas import tpu as pltpu

F32 = jnp.float32
BF16 = jnp.bfloat16
HI = lax.Precision.HIGHEST

D_MODEL = 1024
N_META = 16
RMS_EPS = 1e-6
A_HEADS = 16
A_KV_HEADS = 4
A_HEAD_DIM = 64
A_WIDTH = A_HEADS * A_HEAD_DIM
A_KV_WIDTH = A_KV_HEADS * A_HEAD_DIM
WINDOW = 128
BLOCK = 128
N_BUCKETS = 32
MAX_DISTANCE = 128
R_HEAD_DIM = 64
R_HEADS = D_MODEL // R_HEAD_DIM
GN_EPS = 64e-5

LANES = 128
SUBLANES = 8
PAD = BLOCK - N_META
NEG = -1e30
WKV_CHUNK = 64
VMEM_LIMIT = 56 * 1024 * 1024


def _cparams(n_axes):
    return pltpu.CompilerParams(dimension_semantics=("arbitrary",) * n_axes,
                                vmem_limit_bytes=VMEM_LIMIT)


def _rmsnorm(x, gain):
    return x * lax.rsqrt(jnp.mean(x * x, axis=-1, keepdims=True) + RMS_EPS) * gain


def _sigmoid(x):
    return 1.0 / (1.0 + jnp.exp(-x))


def _t5_bucket_np(rel):
    n = np.maximum(rel, 0)
    max_exact = N_BUCKETS // 2
    nf = np.maximum(n, max_exact).astype(np.float32)
    scale = np.float32(math.log(MAX_DISTANCE / max_exact))
    large = max_exact + (np.log(nf / np.float32(max_exact)) / scale
                         * np.float32(N_BUCKETS - max_exact)).astype(np.int32)
    large = np.minimum(large, N_BUCKETS - 1)
    return np.where(n < max_exact, n, large).astype(np.int32)


def _bias_kernel(table_ref, bucket_ref, out_ref):
    h = pl.program_id(0)
    bk = bucket_ref[...]
    acc = jnp.full(bk.shape, NEG, F32)
    for b in range(N_BUCKETS):
        acc = jnp.where(bk == b, table_ref[b, h], acc)
    out_ref[0] = acc


def _bias_call(table, bucket_np):
    r, c = bucket_np.shape
    return pl.pallas_call(
        _bias_kernel,
        grid=(A_HEADS,),
        in_specs=[pl.BlockSpec(memory_space=pltpu.SMEM),
                  pl.BlockSpec((r, c), lambda h: (0, 0))],
        out_specs=pl.BlockSpec((1, r, c), lambda h: (h, 0, 0)),
        out_shape=jax.ShapeDtypeStruct((A_HEADS, r, c), F32),
        compiler_params=_cparams(1),
        name="bias_expand",
    )(table, jnp.asarray(bucket_np))


def _attn_proj_kernel(x_ref, gain_ref, w_ref, q_ref, kv_ref, g_ref):
    xn = _rmsnorm(x_ref[...], gain_ref[...])
    proj = jnp.dot(xn.astype(BF16), w_ref[...], preferred_element_type=F32)
    q_ref[...] = (proj[:, :A_WIDTH] * (A_HEAD_DIM ** -0.5)).astype(q_ref.dtype)
    kv_ref[...] = proj[:, A_WIDTH:A_WIDTH + 2 * A_KV_WIDTH]
    g_ref[...] = proj[:, A_WIDTH + 2 * A_KV_WIDTH:]


def _attn_proj_call(x2d, gain, w_bf16, tm, q_dtype):
    n = x2d.shape[0]
    wcols = w_bf16.shape[1]
    return pl.pallas_call(
        _attn_proj_kernel,
        grid=(n // tm,),
        in_specs=[pl.BlockSpec((tm, D_MODEL), lambda i: (i, 0)),
                  pl.BlockSpec((1, D_MODEL), lambda i: (0, 0)),
                  pl.BlockSpec((D_MODEL, wcols), lambda i: (0, 0))],
        out_specs=[pl.BlockSpec((tm, A_WIDTH), lambda i: (i, 0)),
                   pl.BlockSpec((tm, 2 * A_KV_WIDTH), lambda i: (i, 0)),
                   pl.BlockSpec((tm, A_WIDTH), lambda i: (i, 0))],
        out_shape=[jax.ShapeDtypeStruct((n, A_WIDTH), q_dtype),
                   jax.ShapeDtypeStruct((n, 2 * A_KV_WIDTH), F32),
                   jax.ShapeDtypeStruct((n, A_WIDTH), F32)],
        compiler_params=_cparams(1),
        name="attn_proj",
    )(x2d, gain, w_bf16)


def _padded_kv_tiles(kv, c):
    lo = lax.broadcasted_iota(jnp.int32, (1, LANES), 1) < A_HEAD_DIM
    j = c // 2
    out = []
    for base in (0, A_KV_WIDTH):
        t = kv[:, base + j * LANES: base + (j + 1) * LANES]
        tr = pltpu.roll(t, A_HEAD_DIM, 1)
        if c % 2 == 0:
            even, odd = jnp.where(lo, t, 0.0), jnp.where(lo, 0.0, tr)
        else:
            even, odd = jnp.where(lo, tr, 0.0), jnp.where(lo, 0.0, t)
        out += [even.astype(BF16), odd.astype(BF16)]
    return out


def _attend_pair(q_tile, ke, ko, ve, vo, bias_e, bias_o, sink_e, sink_o, kvalid):
    dn = (((1,), (1,)), ((), ()))
    total = None
    for kk, vv, bias, sink in ((ke, ve, bias_e, sink_e), (ko, vo, bias_o, sink_o)):
        s = lax.dot_general(q_tile, kk, dn, preferred_element_type=F32) + bias
        if kvalid is not None:
            s = jnp.where(kvalid, s, NEG)
        m = jnp.maximum(jnp.max(s, axis=1, keepdims=True), sink)
        p = jnp.exp(s - m)
        den = jnp.sum(p, axis=1, keepdims=True) + jnp.exp(sink - m)
        o = jnp.dot(p.astype(BF16), vv, preferred_element_type=F32) * (1.0 / den)
        total = o if total is None else total + o
    return total


def _attn_prompt_kernel(sinks_ref, q_ref, kvc_ref, kvp_ref, g_ref, h_ref, bias_ref, wout_ref,
                        out_ref, og_ref):
    i = pl.program_id(1)
    kv = jnp.concatenate([kvp_ref[...], kvc_ref[...]], axis=0)
    col = lax.broadcasted_iota(jnp.int32, (1, 2 * BLOCK), 1)
    kvalid = (i - 1) * BLOCK + col >= PAD
    for c in range(A_KV_HEADS):
        ke, ko, ve, vo = _padded_kv_tiles(kv, c)
        for mm in (2 * c, 2 * c + 1):
            sl = slice(mm * LANES, (mm + 1) * LANES)
            he, ho = 2 * mm, 2 * mm + 1
            o = _attend_pair(q_ref[:, sl], ke, ko, ve, vo, bias_ref[he], bias_ref[ho],
                             sinks_ref[he], sinks_ref[ho], kvalid)
            gt = g_ref[:, sl]
            og_ref[:, sl] = (o * (gt * _sigmoid(gt))).astype(BF16)
    out_ref[...] = h_ref[...] + jnp.dot(og_ref[...], wout_ref[...], preferred_element_type=F32)


def _attn_prompt_call(sinks, q, kv, g, h, bias, wout_bf16, nbatch, nblk):
    n = q.shape[0]
    row = lambda b, i: (b * nblk + i, 0)
    prev = lambda b, i: (b * nblk + jnp.maximum(i - 1, 0), 0)
    return pl.pallas_call(
        _attn_prompt_kernel,
        grid=(nbatch, nblk),
        in_specs=[pl.BlockSpec(memory_space=pltpu.SMEM),
                  pl.BlockSpec((BLOCK, A_WIDTH), row),
                  pl.BlockSpec((BLOCK, 2 * A_KV_WIDTH), row),
                  pl.BlockSpec((BLOCK, 2 * A_KV_WIDTH), prev),
                  pl.BlockSpec((BLOCK, A_WIDTH), row),
                  pl.BlockSpec((BLOCK, D_MODEL), row),
                  pl.BlockSpec((A_HEADS, BLOCK, 2 * BLOCK), lambda b, i: (0, 0, 0)),
                  pl.BlockSpec((A_WIDTH, D_MODEL), lambda b, i: (0, 0))],
        out_specs=pl.BlockSpec((BLOCK, D_MODEL), row),
        out_shape=jax.ShapeDtypeStruct((n, D_MODEL), F32),
        scratch_shapes=[pltpu.VMEM((BLOCK, A_WIDTH), BF16)],
        compiler_params=_cparams(2),
        name="attn_prompt",
    )(sinks, q, kv, kv, g, h, bias, wout_bf16)


SAMPLE_SB = 8
SAMPLE_KEYS = 2 * BLOCK


def _attn_sample_kernel(sinks_ref, q_ref, kvn_ref, g_ref, h_ref, ck_ref, cv_ref, bias_ref, wout_ref,
                        out_ref, nk_ref, nv_ref, og_ref, *, t_new):
    keep = ck_ref.shape[1]
    fill = SAMPLE_KEYS - keep - 8
    for s in range(SAMPLE_SB):
        rows = slice(s * t_new, (s + 1) * t_new)
        kvn = kvn_ref[rows, :]
        new8 = jnp.concatenate([kvn, jnp.zeros((8 - t_new, 2 * A_KV_WIDTH), F32)], axis=0)
        cache = jnp.concatenate([ck_ref[s], cv_ref[s]], axis=1)
        kv = jnp.concatenate([cache, new8, jnp.zeros((fill, 2 * A_KV_WIDTH), F32)], axis=0)
        for c in range(A_KV_HEADS):
            ke, ko, ve, vo = _padded_kv_tiles(kv, c)
            for mm in (2 * c, 2 * c + 1):
                sl = slice(mm * LANES, (mm + 1) * LANES)
                he, ho = 2 * mm, 2 * mm + 1
                o = _attend_pair(q_ref[rows, sl].astype(BF16), ke, ko, ve, vo,
                                 bias_ref[he, :t_new, :], bias_ref[ho, :t_new, :],
                                 sinks_ref[he], sinks_ref[ho], None)
                gt = g_ref[rows, sl]
                og_ref[rows, sl] = o * (gt * _sigmoid(gt))
        nk_ref[s, :keep - t_new, :] = ck_ref[s, t_new:, :]
        nk_ref[s, keep - t_new:, :] = kvn[:, :A_KV_WIDTH]
        nv_ref[s, :keep - t_new, :] = cv_ref[s, t_new:, :]
        nv_ref[s, keep - t_new:, :] = kvn[:, A_KV_WIDTH:]
    out_ref[...] = h_ref[...] + jnp.dot(og_ref[...].astype(BF16), wout_ref[...],
                                        preferred_element_type=F32)


def _attn_sample_call(sinks, q, kv, g, h, cache_k, cache_v, bias, wout_bf16, t_new):
    nseq, keep = cache_k.shape[0], cache_k.shape[1]
    rows = SAMPLE_SB * t_new
    row = lambda i: (i, 0)
    cspec = pl.BlockSpec((SAMPLE_SB, keep, A_KV_WIDTH), lambda i: (i, 0, 0))
    return pl.pallas_call(
        functools.partial(_attn_sample_kernel, t_new=t_new),
        grid=(nseq // SAMPLE_SB,),
        in_specs=[pl.BlockSpec(memory_space=pltpu.SMEM),
                  pl.BlockSpec((rows, A_WIDTH), row),
                  pl.BlockSpec((rows, 2 * A_KV_WIDTH), row),
                  pl.BlockSpec((rows, A_WIDTH), row),
                  pl.BlockSpec((rows, D_MODEL), row),
                  cspec, cspec,
                  pl.BlockSpec((A_HEADS, 8, SAMPLE_KEYS), lambda i: (0, 0, 0)),
                  pl.BlockSpec((A_WIDTH, D_MODEL), lambda i: (0, 0))],
        out_specs=[pl.BlockSpec((rows, D_MODEL), row), cspec, cspec],
        out_shape=[jax.ShapeDtypeStruct((nseq * t_new, D_MODEL), F32),
                   jax.ShapeDtypeStruct(cache_k.shape, F32),
                   jax.ShapeDtypeStruct(cache_v.shape, F32)],
        scratch_shapes=[pltpu.VMEM((rows, A_WIDTH), F32)],
        compiler_params=_cparams(1),
        name="attn_sample",
    )(sinks, q, kv, g, h, cache_k, cache_v, bias, wout_bf16)


def _rwkv_proj_kernel(h_ref, shift_ref, gain_ref, mu_ref, win_ref, w0_ref, w1_ref, w2_ref,
                      a0_ref, a1_ref, a2_ref,
                      r_ref, k_ref, v_ref, g_ref, ld_ref, a_ref, xn_ref, *scratch, seq_len):
    xn = _rmsnorm(h_ref[...], gain_ref[...])
    tm = xn.shape[0]
    rolled = pltpu.roll(xn, 1, 0)
    row = lax.broadcasted_iota(jnp.int32, (tm, 1), 0)
    if seq_len is None:
        carry_ref, = scratch

        @pl.when(pl.program_id(1) == 0)
        def _():
            carry_ref[...] = shift_ref[0]

        xprev = jnp.where(row == 0, carry_ref[...], rolled)
        carry_ref[...] = xn[tm - 1:tm, :]
        xn_ref[0] = xn[tm - 1:tm, :]
    else:
        xprev = jnp.where(row % seq_len == 0, shift_ref[...], rolled)
        xn_ref[...] = xn
    dx = xprev - xn

    def mix(c):
        return (xn + dx * mu_ref[c:c + 1, :]).astype(BF16)

    for c, o_ref in enumerate((r_ref, k_ref, v_ref, g_ref)):
        o_ref[...] = jnp.dot(mix(c), win_ref[c], preferred_element_type=F32)
    lw = jnp.tanh(jnp.dot(mix(4), w1_ref[...], preferred_element_type=F32))
    z = w0_ref[...] + jnp.dot(lw.astype(BF16), w2_ref[...], preferred_element_type=F32)
    u = -z
    softplus = jnp.maximum(u, 0.0) + jnp.log(1.0 + jnp.exp(-jnp.abs(u)))
    ld_ref[...] = -jnp.exp(-softplus - 0.5)
    la = jnp.dot(mix(5), a1_ref[...], preferred_element_type=F32)
    a_ref[...] = _sigmoid(a0_ref[...] + jnp.dot(la.astype(BF16), a2_ref[...],
                                                preferred_element_type=F32))


def _rwkv_proj_call(h, shift, p, nbatch, ntile, tm, seq_len):
    n = h.shape[0]
    row = lambda b, i: (b * ntile + i, 0)
    full2 = lambda b, i: (0, 0)
    if seq_len is None:
        shift_spec = pl.BlockSpec((1, 1, D_MODEL), lambda b, i: (b, 0, 0))
        xn_spec = pl.BlockSpec((1, 1, D_MODEL), lambda b, i: (b, 0, 0))
        xn_shape = jax.ShapeDtypeStruct((nbatch, 1, D_MODEL), F32)
        scratch = [pltpu.VMEM((1, D_MODEL), F32)]
    else:
        shift_spec = pl.BlockSpec((tm, D_MODEL), row)
        xn_spec = pl.BlockSpec((tm, D_MODEL), row)
        xn_shape = jax.ShapeDtypeStruct((n, D_MODEL), F32)
        scratch = []
    lora = p["w1"].shape[1]
    big = jax.ShapeDtypeStruct((n, D_MODEL), F32)
    return pl.pallas_call(
        functools.partial(_rwkv_proj_kernel, seq_len=seq_len),
        grid=(nbatch, ntile),
        in_specs=[pl.BlockSpec((tm, D_MODEL), row),
                  shift_spec,
                  pl.BlockSpec((1, D_MODEL), full2),
                  pl.BlockSpec(p["mu"].shape, full2),
                  pl.BlockSpec(p["w_in"].shape, lambda b, i: (0, 0, 0)),
                  pl.BlockSpec((1, D_MODEL), full2),
                  pl.BlockSpec((D_MODEL, lora), full2),
                  pl.BlockSpec((lora, D_MODEL), full2),
                  pl.BlockSpec((1, D_MODEL), full2),
                  pl.BlockSpec((D_MODEL, lora), full2),
                  pl.BlockSpec((lora, D_MODEL), full2)],
        out_specs=[pl.BlockSpec((tm, D_MODEL), row)] * 6 + [xn_spec],
        out_shape=[big] * 6 + [xn_shape],
        scratch_shapes=scratch,
        compiler_params=_cparams(2),
        name="rwkv_proj",
    )(h, shift, p["gain"], p["mu"], p["w_in"], p["w0"], p["w1"], p["w2"], p["a0"], p["a1"], p["a2"])


def _dot_hi(a, b):
    return jnp.dot(a, b, precision=HI, preferred_element_type=F32)


def _dot_nt(a, b):
    return lax.dot_general(a, b, (((1,), (1,)), ((), ())), precision=HI, preferred_element_type=F32)


def _dot_tn(a, b):
    return lax.dot_general(a, b, (((0,), (0,)), ((), ())), precision=HI, preferred_element_type=F32)


def _wkv_head_inputs(k, a, kk_gain, ka_gain):
    kkx = k * kk_gain
    nrm = jnp.sqrt(jnp.sum(kkx * kkx, axis=1, keepdims=True))
    kk = kkx / jnp.maximum(nrm, 1e-12)
    kh = k * (1.0 + (a - 1.0) * ka_gain)
    return kh, -kk, kk * a


def _wkv_chunk_kernel(r_ref, k_ref, v_ref, a_ref, ld_ref, kkg_ref, kag_ref, y_ref, sout_ref, st_ref):
    ci = pl.program_id(2)
    c = r_ref.shape[0]
    n = R_HEAD_DIM

    @pl.when(ci == 0)
    def _():
        st_ref[...] = jnp.zeros_like(st_ref)

    rr = lax.broadcasted_iota(jnp.int32, (c, c), 0)
    cc = lax.broadcasted_iota(jnp.int32, (c, c), 1)
    incl = rr >= cc
    strict = rr > cc
    tri = incl.astype(F32)
    eye_c = (rr == cc).astype(F32)
    eye_n = (lax.broadcasted_iota(jnp.int32, (n, n), 0) == lax.broadcasted_iota(jnp.int32, (n, n), 1))
    n_double = int(math.log2(c)) - 1
    for hh in range(2):
        sl = slice(hh * n, (hh + 1) * n)
        r, k, v, a, ld = (ref[:, sl] for ref in (r_ref, k_ref, v_ref, a_ref, ld_ref))
        kh, av, bv = _wkv_head_inputs(k, a, kkg_ref[:, sl], kag_ref[:, sl])
        cs = _dot_hi(tri, ld)
        cs_end = cs[c - 1:c, :]
        e_neg = jnp.exp(-cs)
        at = av * jnp.exp(cs - ld)
        bt = bv * e_neg
        kt = kh * e_neg
        rt = r * jnp.exp(cs)
        e_end = jnp.exp(cs_end - cs)
        bh = bv * e_end
        khh = kh * e_end
        lab = jnp.where(strict, _dot_nt(at, bt), 0.0)
        lak = jnp.where(strict, _dot_nt(at, kt), 0.0)
        mrb = jnp.where(incl, _dot_nt(rt, bt), 0.0)
        mrk = jnp.where(incl, _dot_nt(rt, kt), 0.0)
        pw = lab
        tinv = eye_c + lab
        for _ in range(n_double):
            pw = _dot_hi(pw, pw)
            tinv = tinv + _dot_hi(tinv, pw)
        st = st_ref[hh]
        u = _dot_hi(tinv, _dot_hi(at, st) + _dot_hi(lak, v))
        y_ref[:, sl] = _dot_hi(rt, st) + _dot_hi(mrb, u) + _dot_hi(mrk, v)
        w_end_col = jnp.sum(jnp.where(eye_n, jnp.exp(cs_end), 0.0), axis=1, keepdims=True)
        st_new = w_end_col * st + _dot_tn(bh, u) + _dot_tn(khh, v)
        st_ref[hh] = st_new
        sout_ref[0, hh] = st_new


def _wkv_chunk_call(r, k, v, a, ld, kk_gain, ka_gain, nbatch, seq):
    n = r.shape[0]
    c = WKV_CHUNK
    nchunk = seq // c
    npair = D_MODEL // LANES
    tile = pl.BlockSpec((c, LANES), lambda b, j, t: (b * nchunk + t, j))
    par = pl.BlockSpec((1, LANES), lambda b, j, t: (0, j))
    return pl.pallas_call(
        _wkv_chunk_kernel,
        grid=(nbatch, npair, nchunk),
        in_specs=[tile] * 5 + [par, par],
        out_specs=[tile, pl.BlockSpec((1, 2, R_HEAD_DIM, R_HEAD_DIM), lambda b, j, t: (b, j, 0, 0))],
        out_shape=[jax.ShapeDtypeStruct((n, D_MODEL), F32),
                   jax.ShapeDtypeStruct((nbatch, R_HEADS, R_HEAD_DIM, R_HEAD_DIM), F32)],
        scratch_shapes=[pltpu.VMEM((2, R_HEAD_DIM, R_HEAD_DIM), F32)],
        compiler_params=_cparams(3),
        name="wkv_chunk",
    )(r, k, v, a, ld, kk_gain, ka_gain)


WKV_SB = 8


def _wkv_seq_kernel(r_ref, k_ref, v_ref, a_ref, ld_ref, kkg_ref, kag_ref, s_ref, y_ref, sout_ref,
                    *, seq_len):
    n = R_HEAD_DIM
    eye = (lax.broadcasted_iota(jnp.int32, (n, n), 0) == lax.broadcasted_iota(jnp.int32, (n, n), 1))

    per_group = SUBLANES // seq_len

    def one_group(gi, carry):
        rows = pl.ds(pl.multiple_of(gi * SUBLANES, SUBLANES), SUBLANES)
        for h in range(R_HEADS):
            sl = slice(h * n, (h + 1) * n)
            r, k, v, a, ld = (ref[rows, sl] for ref in (r_ref, k_ref, v_ref, a_ref, ld_ref))
            kh, av, bv = _wkv_head_inputs(k, a, kkg_ref[:, sl], kag_ref[:, sl])
            w = jnp.exp(ld)
            ys = []
            for q in range(per_group):
                s = gi * per_group + q
                state = s_ref[s, h]
                for t in range(q * seq_len, (q + 1) * seq_len):
                    tt = slice(t, t + 1)
                    sa = jnp.sum(state * av[tt], axis=1, keepdims=True)
                    vcol = jnp.sum(jnp.where(eye, v[tt], 0.0), axis=1, keepdims=True)
                    state = state * w[tt] + sa * bv[tt] + vcol * kh[tt]
                    ycol = jnp.sum(state * r[tt], axis=1, keepdims=True)
                    ys.append(jnp.sum(jnp.where(eye, ycol, 0.0), axis=0, keepdims=True))
                sout_ref[s, h] = state
            y_ref[rows, sl] = jnp.concatenate(ys, axis=0)
        return carry

    lax.fori_loop(0, WKV_SB // per_group, one_group, 0)


def _wkv_seq_call(r, k, v, a, ld, kk_gain, ka_gain, state, seq_len):
    n = r.shape[0]
    nseq = state.shape[0]
    rows = WKV_SB * seq_len
    tile = pl.BlockSpec((rows, D_MODEL), lambda i: (i, 0))
    par = pl.BlockSpec((1, D_MODEL), lambda i: (0, 0))
    sspec = pl.BlockSpec((WKV_SB, R_HEADS, R_HEAD_DIM, R_HEAD_DIM), lambda i: (i, 0, 0, 0))
    return pl.pallas_call(
        functools.partial(_wkv_seq_kernel, seq_len=seq_len),
        grid=(nseq // WKV_SB,),
        in_specs=[tile] * 5 + [par, par, sspec],
        out_specs=[tile, sspec],
        out_shape=[jax.ShapeDtypeStruct((n, D_MODEL), F32),
                   jax.ShapeDtypeStruct(state.shape, F32)],
        compiler_params=_cparams(1),
        name="wkv_seq",
    )(r, k, v, a, ld, kk_gain, ka_gain, state)


def _seg_sum(x, ones_bf16):
    hi = x.astype(BF16)
    lo = (x - hi.astype(F32)).astype(BF16)
    return (jnp.dot(hi, ones_bf16, preferred_element_type=F32)
            + jnp.dot(lo, ones_bf16, preferred_element_type=F32))


def _rwkv_out_kernel(y_ref, r_ref, k_ref, v_ref, a_ref, g_ref, h_ref, kag_ref, rk_ref, lng_ref,
                     lnb_ref, wout_ref, fg_ref, out_ref, z_ref):
    half_r = lax.broadcasted_iota(jnp.int32, (LANES, LANES), 0) // R_HEAD_DIM
    half_c = lax.broadcasted_iota(jnp.int32, (LANES, LANES), 1) // R_HEAD_DIM
    ones = (half_r == half_c).astype(BF16)
    inv_n = 1.0 / R_HEAD_DIM
    for j in range(D_MODEL // LANES):
        sl = slice(j * LANES, (j + 1) * LANES)
        y = y_ref[:, sl]
        d = y - _seg_sum(y, ones) * inv_n
        var = _seg_sum(d * d, ones) * inv_n
        yn = d * lax.rsqrt(var + GN_EPS) * lng_ref[:, sl] + lnb_ref[:, sl]
        a = a_ref[:, sl]
        r = r_ref[:, sl]
        kh = k_ref[:, sl] * (1.0 + (a - 1.0) * kag_ref[:, sl])
        bonus = _seg_sum(r * kh * rk_ref[:, sl], ones) * v_ref[:, sl]
        g = g_ref[:, sl]
        z_ref[:, sl] = ((yn + bonus) * (g * _sigmoid(g))).astype(BF16)
    h2 = h_ref[...] + jnp.dot(z_ref[...], wout_ref[...], preferred_element_type=F32)
    out_ref[...] = _rmsnorm(h2, fg_ref[...])


def _rwkv_out_call(y, r, k, v, a, g, h, p, nbatch, ntile, tm, in_tiles_per_batch, skip):
    src = lambda b, i: (b * in_tiles_per_batch + i + skip, 0)
    dst = lambda b, i: (b * ntile + i, 0)
    par = pl.BlockSpec((1, D_MODEL), lambda b, i: (0, 0))
    tile = pl.BlockSpec((tm, D_MODEL), src)
    return pl.pallas_call(
        _rwkv_out_kernel,
        grid=(nbatch, ntile),
        in_specs=[tile] * 7 + [par] * 4 + [pl.BlockSpec((D_MODEL, D_MODEL), lambda b, i: (0, 0)), par],
        out_specs=pl.BlockSpec((tm, D_MODEL), dst),
        out_shape=jax.ShapeDtypeStruct((nbatch * ntile * tm, D_MODEL), F32),
        scratch_shapes=[pltpu.VMEM((tm, D_MODEL), BF16)],
        compiler_params=_cparams(2),
        name="rwkv_out",
    )(y, r, k, v, a, g, h, p["k_a"], p["r_k"], p["ln_g"], p["ln_b"], p["w_out"], p["final_gain"])


def _prompt_bucket():
    qi = np.arange(BLOCK)[:, None]
    kj = np.arange(2 * BLOCK)[None, :]
    rel = BLOCK + qi - kj
    return np.where((rel >= 0) & (rel < WINDOW), _t5_bucket_np(rel), -1).astype(np.int32)


def _sample_bucket(keep, t_new):
    t = np.arange(8)[:, None]
    j = np.arange(SAMPLE_KEYS)[None, :]
    rel = keep + t - j
    ok = (rel >= 0) & (rel < WINDOW) & (j < keep + t_new) & (t < t_new)
    return np.where(ok, _t5_bucket_np(rel), -1).astype(np.int32)


def kernel(x_prompt, x_sample, cache_win_k, cache_win_v, state_wkv, state_shift, meta_tokens, rel_bias_table, norm_gain, final_gain, attn_w_in, attn_sinks, attn_w_out, rwkv_mu, rwkv_w_in, rwkv_w0, rwkv_w1, rwkv_w2, rwkv_a0, rwkv_a1, rwkv_a2, rwkv_k_k, rwkv_k_a, rwkv_r_k, rwkv_ln_gamma, rwkv_ln_beta, rwkv_w_out):
    nb, seq, _ = x_prompt.shape
    ns, t_new, _ = x_sample.shape
    keep = cache_win_k.shape[2]
    lp = seq + BLOCK
    nblk = lp // BLOCK
    row = lambda x: x.reshape(1, D_MODEL)

    w_in0 = attn_w_in[0].astype(BF16)
    w_out0 = attn_w_out[0].astype(BF16)
    gain0 = row(norm_gain[0])
    sinks = attn_sinks[0]
    rp = dict(gain=row(norm_gain[1]), mu=rwkv_mu[0], w_in=rwkv_w_in[0].astype(BF16),
              w0=row(rwkv_w0[0]), w1=rwkv_w1[0].astype(BF16), w2=rwkv_w2[0].astype(BF16),
              a0=row(rwkv_a0[0]), a1=rwkv_a1[0].astype(BF16), a2=rwkv_a2[0].astype(BF16),
              k_a=row(rwkv_k_a[0]), r_k=row(rwkv_r_k[0]), ln_g=row(rwkv_ln_gamma[0]),
              ln_b=row(rwkv_ln_beta[0]), w_out=rwkv_w_out[0].astype(BF16),
              final_gain=row(final_gain))
    kk_gain = row(rwkv_k_k[0])

    bias_p = _bias_call(rel_bias_table, _prompt_bucket())
    bias_s = _bias_call(rel_bias_table, _sample_bucket(keep, t_new))

    meta = jnp.broadcast_to(meta_tokens[None].astype(F32), (nb, N_META, D_MODEL))
    h0 = jnp.concatenate([jnp.zeros((nb, PAD, D_MODEL), F32), meta, x_prompt], axis=1)
    h0 = h0.reshape(nb * lp, D_MODEL)
    q, kv, g = _attn_proj_call(h0, gain0, w_in0, 256, BF16)
    h1 = _attn_prompt_call(sinks, q, kv, g, h0, bias_p, w_out0, nb, nblk)
    kv3 = kv.reshape(nb, lp, 2 * A_KV_WIDTH)[:, lp - WINDOW:, :]
    win_k_p = kv3[:, :, :A_KV_WIDTH].reshape(1, nb, WINDOW, A_KV_HEADS, A_HEAD_DIM)
    win_v_p = kv3[:, :, A_KV_WIDTH:].reshape(1, nb, WINDOW, A_KV_HEADS, A_HEAD_DIM)

    shift0 = jnp.zeros((nb, 1, D_MODEL), F32)
    r, k, v, g1, ld, a, xlast = _rwkv_proj_call(h1, shift0, rp, nb, nblk, BLOCK, None)
    y, st = _wkv_chunk_call(r, k, v, a, ld, kk_gain, rp["k_a"], nb, lp)
    y_prompt = _rwkv_out_call(y, r, k, v, a, g1, h1, rp, nb, nblk - 1, BLOCK, nblk, 1)
    y_prompt = y_prompt.reshape(nb, seq, D_MODEL)
    wkv_p = jnp.swapaxes(st, -1, -2)[None]
    shift_p = xlast.reshape(1, nb, D_MODEL)

    xs = x_sample.reshape(ns * t_new, D_MODEL)
    qs, kvs, gs = _attn_proj_call(xs, gain0, w_in0, 256, F32)
    ck = cache_win_k[0].reshape(ns, keep, A_KV_WIDTH)
    cv = cache_win_v[0].reshape(ns, keep, A_KV_WIDTH)
    h1s, nk, nv = _attn_sample_call(sinks, qs, kvs, gs, xs, ck, cv, bias_s, w_out0, t_new)
    win_k_s = nk.reshape(1, ns, keep, A_KV_HEADS, A_HEAD_DIM)
    win_v_s = nv.reshape(1, ns, keep, A_KV_HEADS, A_HEAD_DIM)

    shift_rows = jnp.repeat(state_shift[0], t_new, axis=0)
    tms = 256
    rs, ks, vs, g1s, lds, as_, xns = _rwkv_proj_call(h1s, shift_rows, rp, 1, ns * t_new // tms, tms, t_new)
    ys, st_s = _wkv_seq_call(rs, ks, vs, as_, lds, kk_gain, rp["k_a"], state_wkv[0], t_new)
    y_sample = _rwkv_out_call(ys, rs, ks, vs, as_, g1s, h1s, rp, 1, ns * t_new // tms, tms, 0, 0)
    y_sample = y_sample.reshape(ns, t_new, D_MODEL)
    wkv_s = st_s[None]
    shift_s = xns.reshape(ns, t_new, D_MODEL)[:, t_new - 1][None]

    return (y_prompt, y_sample, win_k_p, win_v_p, wkv_p, shift_p, win_k_s, win_v_s, wkv_s, shift_s)
```

```python
import functools
import math

import numpy as np
import jax
import jax.numpy as jnp
from jax import lax
from jax.experimental import pallas as pl
from jax.experimental.pallas import tpu as pltpu

F32 = jnp.float32
BF16 = jnp.bfloat16

D_MODEL = 1024
N_META = 16
RMS_EPS = 1e-6
A_HEADS = 16
A_KV_HEADS = 4
A_HEAD_DIM = 64
A_WIDTH = A_HEADS * A_HEAD_DIM
A_KV_WIDTH = A_KV_HEADS * A_HEAD_DIM
WINDOW = 128
BLOCK = 128
N_BUCKETS = 32
MAX_DISTANCE = 128
R_HEAD_DIM = 64
R_HEADS = D_MODEL // R_HEAD_DIM
GN_EPS = 64e-5

LANES = 128
SUBLANES = 8
PAD = BLOCK - N_META
NEG = -1e30
WKV_CHUNK = 64
WKV_PAIRS_PER_STEP = 8
VMEM_LIMIT = 56 * 1024 * 1024


def _cparams(n_axes):
    return pltpu.CompilerParams(dimension_semantics=("arbitrary",) * n_axes,
                                vmem_limit_bytes=VMEM_LIMIT)


def _rmsnorm(x, gain):
    return x * lax.rsqrt(jnp.mean(x * x, axis=-1, keepdims=True) + RMS_EPS) * gain


def _sigmoid(x):
    return 1.0 / (1.0 + jnp.exp(-x))


def _t5_bucket_np(rel):
    n = np.maximum(rel, 0)
    max_exact = N_BUCKETS // 2
    nf = np.maximum(n, max_exact).astype(np.float32)
    scale = np.float32(math.log(MAX_DISTANCE / max_exact))
    large = max_exact + (np.log(nf / np.float32(max_exact)) / scale
                         * np.float32(N_BUCKETS - max_exact)).astype(np.int32)
    large = np.minimum(large, N_BUCKETS - 1)
    return np.where(n < max_exact, n, large).astype(np.int32)


def _bias_kernel(table_ref, bucket_ref, out_ref):
    h = pl.program_id(0)
    bk = bucket_ref[...]
    acc = jnp.full(bk.shape, NEG, F32)
    for b in range(N_BUCKETS):
        acc = jnp.where(bk == b, table_ref[b, h], acc)
    out_ref[0] = acc


def _bias_call(table, bucket_np):
    r, c = bucket_np.shape
    return pl.pallas_call(
        _bias_kernel,
        grid=(A_HEADS,),
        in_specs=[pl.BlockSpec(memory_space=pltpu.SMEM),
                  pl.BlockSpec((r, c), lambda h: (0, 0))],
        out_specs=pl.BlockSpec((1, r, c), lambda h: (h, 0, 0)),
        out_shape=jax.ShapeDtypeStruct((A_HEADS, r, c), F32),
        compiler_params=_cparams(1),
        name="bias_expand",
    )(table, jnp.asarray(bucket_np))


def _attn_proj_kernel(x_ref, gain_ref, w_ref, q_ref, kv_ref, g_ref):
    xn = _rmsnorm(x_ref[...], gain_ref[...])
    proj = jnp.dot(xn.astype(BF16), w_ref[...], preferred_element_type=F32)
    q_ref[...] = (proj[:, :A_WIDTH] * (A_HEAD_DIM ** -0.5)).astype(q_ref.dtype)
    kv_ref[...] = proj[:, A_WIDTH:A_WIDTH + 2 * A_KV_WIDTH]
    g_ref[...] = proj[:, A_WIDTH + 2 * A_KV_WIDTH:]


def _attn_proj_call(x2d, gain, w_bf16, tm, q_dtype):
    n = x2d.shape[0]
    wcols = w_bf16.shape[1]
    return pl.pallas_call(
        _attn_proj_kernel,
        grid=(n // tm,),
        in_specs=[pl.BlockSpec((tm, D_MODEL), lambda i: (i, 0)),
                  pl.BlockSpec((1, D_MODEL), lambda i: (0, 0)),
                  pl.BlockSpec((D_MODEL, wcols), lambda i: (0, 0))],
        out_specs=[pl.BlockSpec((tm, A_WIDTH), lambda i: (i, 0)),
                   pl.BlockSpec((tm, 2 * A_KV_WIDTH), lambda i: (i, 0)),
                   pl.BlockSpec((tm, A_WIDTH), lambda i: (i, 0))],
        out_shape=[jax.ShapeDtypeStruct((n, A_WIDTH), q_dtype),
                   jax.ShapeDtypeStruct((n, 2 * A_KV_WIDTH), F32),
                   jax.ShapeDtypeStruct((n, A_WIDTH), F32)],
        compiler_params=_cparams(1),
        name="attn_proj",
    )(x2d, gain, w_bf16)


def _padded_kv_tiles(kv, c):
    lo = lax.broadcasted_iota(jnp.int32, (1, LANES), 1) < A_HEAD_DIM
    j = c // 2
    out = []
    for base in (0, A_KV_WIDTH):
        t = kv[:, base + j * LANES: base + (j + 1) * LANES]
        tr = pltpu.roll(t, A_HEAD_DIM, 1)
        if c % 2 == 0:
            even, odd = jnp.where(lo, t, 0.0), jnp.where(lo, 0.0, tr)
        else:
            even, odd = jnp.where(lo, tr, 0.0), jnp.where(lo, 0.0, t)
        out += [even.astype(BF16), odd.astype(BF16)]
    return out


def _attend_pair(q_tile, ke, ko, ve, vo, bias_e, bias_o, sink_e, sink_o, kvalid):
    dn = (((1,), (1,)), ((), ()))
    total = None
    for kk, vv, bias, sink in ((ke, ve, bias_e, sink_e), (ko, vo, bias_o, sink_o)):
        s = lax.dot_general(q_tile, kk, dn, preferred_element_type=F32) + bias
        if kvalid is not None:
            s = jnp.where(kvalid, s, NEG)
        m = jnp.maximum(jnp.max(s, axis=1, keepdims=True), sink)
        p = jnp.exp(s - m)
        den = jnp.sum(p, axis=1, keepdims=True) + jnp.exp(sink - m)
        o = jnp.dot(p.astype(BF16), vv, preferred_element_type=F32) * (1.0 / den)
        total = o if total is None else total + o
    return total


def _attn_prompt_kernel(sinks_ref, q_ref, kvc_ref, kvp_ref, g_ref, h_ref, bias_ref, wout_ref,
                        out_ref, og_ref):
    i = pl.program_id(1)
    kv = jnp.concatenate([kvp_ref[...], kvc_ref[...]], axis=0)
    col = lax.broadcasted_iota(jnp.int32, (1, 2 * BLOCK), 1)
    kvalid = (i - 1) * BLOCK + col >= PAD
    for c in range(A_KV_HEADS):
        ke, ko, ve, vo = _padded_kv_tiles(kv, c)
        for mm in (2 * c, 2 * c + 1):
            sl = slice(mm * LANES, (mm + 1) * LANES)
            he, ho = 2 * mm, 2 * mm + 1
            o = _attend_pair(q_ref[:, sl], ke, ko, ve, vo, bias_ref[he], bias_ref[ho],
                             sinks_ref[he], sinks_ref[ho], kvalid)
            gt = g_ref[:, sl]
            og_ref[:, sl] = (o * (gt * _sigmoid(gt))).astype(BF16)
    out_ref[...] = h_ref[...] + jnp.dot(og_ref[...], wout_ref[...], preferred_element_type=F32)


def _attn_prompt_call(sinks, q, kv, g, h, bias, wout_bf16, nbatch, nblk):
    n = q.shape[0]
    row = lambda b, i: (b * nblk + i, 0)
    prev = lambda b, i: (b * nblk + jnp.maximum(i - 1, 0), 0)
    return pl.pallas_call(
        _attn_prompt_kernel,
        grid=(nbatch, nblk),
        in_specs=[pl.BlockSpec(memory_space=pltpu.SMEM),
                  pl.BlockSpec((BLOCK, A_WIDTH), row),
                  pl.BlockSpec((BLOCK, 2 * A_KV_WIDTH), row),
                  pl.BlockSpec((BLOCK, 2 * A_KV_WIDTH), prev),
                  pl.BlockSpec((BLOCK, A_WIDTH), row),
                  pl.BlockSpec((BLOCK, D_MODEL), row),
                  pl.BlockSpec((A_HEADS, BLOCK, 2 * BLOCK), lambda b, i: (0, 0, 0)),
                  pl.BlockSpec((A_WIDTH, D_MODEL), lambda b, i: (0, 0))],
        out_specs=pl.BlockSpec((BLOCK, D_MODEL), row),
        out_shape=jax.ShapeDtypeStruct((n, D_MODEL), F32),
        scratch_shapes=[pltpu.VMEM((BLOCK, A_WIDTH), BF16)],
        compiler_params=_cparams(2),
        name="attn_prompt",
    )(sinks, q, kv, kv, g, h, bias, wout_bf16)


SAMPLE_SB = 8
SAMPLE_KEYS = 2 * BLOCK


def _attn_sample_kernel(sinks_ref, q_ref, kvn_ref, g_ref, h_ref, ck_ref, cv_ref, bias_ref, wout_ref,
                        out_ref, nk_ref, nv_ref, og_ref, *, t_new):
    keep = ck_ref.shape[1]
    fill = SAMPLE_KEYS - keep - 8
    for s in range(SAMPLE_SB):
        rows = slice(s * t_new, (s + 1) * t_new)
        kvn = kvn_ref[rows, :]
        new8 = jnp.concatenate([kvn, jnp.zeros((8 - t_new, 2 * A_KV_WIDTH), F32)], axis=0)
        cache = jnp.concatenate([ck_ref[s], cv_ref[s]], axis=1)
        kv = jnp.concatenate([cache, new8, jnp.zeros((fill, 2 * A_KV_WIDTH), F32)], axis=0)
        for c in range(A_KV_HEADS):
            ke, ko, ve, vo = _padded_kv_tiles(kv, c)
            for mm in (2 * c, 2 * c + 1):
                sl = slice(mm * LANES, (mm + 1) * LANES)
                he, ho = 2 * mm, 2 * mm + 1
                o = _attend_pair(q_ref[rows, sl].astype(BF16), ke, ko, ve, vo,
                                 bias_ref[he, :t_new, :], bias_ref[ho, :t_new, :],
                                 sinks_ref[he], sinks_ref[ho], None)
                gt = g_ref[rows, sl]
                og_ref[rows, sl] = o * (gt * _sigmoid(gt))
        nk_ref[s, :keep - t_new, :] = ck_ref[s, t_new:, :]
        nk_ref[s, keep - t_new:, :] = kvn[:, :A_KV_WIDTH]
        nv_ref[s, :keep - t_new, :] = cv_ref[s, t_new:, :]
        nv_ref[s, keep - t_new:, :] = kvn[:, A_KV_WIDTH:]
    out_ref[...] = h_ref[...] + jnp.dot(og_ref[...].astype(BF16), wout_ref[...],
                                        preferred_element_type=F32)


def _attn_sample_call(sinks, q, kv, g, h, cache_k, cache_v, bias, wout_bf16, t_new):
    nseq, keep = cache_k.shape[0], cache_k.shape[1]
    rows = SAMPLE_SB * t_new
    row = lambda i: (i, 0)
    cspec = pl.BlockSpec((SAMPLE_SB, keep, A_KV_WIDTH), lambda i: (i, 0, 0))
    return pl.pallas_call(
        functools.partial(_attn_sample_kernel, t_new=t_new),
        grid=(nseq // SAMPLE_SB,),
        in_specs=[pl.BlockSpec(memory_space=pltpu.SMEM),
                  pl.BlockSpec((rows, A_WIDTH), row),
                  pl.BlockSpec((rows, 2 * A_KV_WIDTH), row),
                  pl.BlockSpec((rows, A_WIDTH), row),
                  pl.BlockSpec((rows, D_MODEL), row),
                  cspec, cspec,
                  pl.BlockSpec((A_HEADS, 8, SAMPLE_KEYS), lambda i: (0, 0, 0)),
                  pl.BlockSpec((A_WIDTH, D_MODEL), lambda i: (0, 0))],
        out_specs=[pl.BlockSpec((rows, D_MODEL), row), cspec, cspec],
        out_shape=[jax.ShapeDtypeStruct((nseq * t_new, D_MODEL), F32),
                   jax.ShapeDtypeStruct(cache_k.shape, F32),
                   jax.ShapeDtypeStruct(cache_v.shape, F32)],
        scratch_shapes=[pltpu.VMEM((rows, A_WIDTH), F32)],
        compiler_params=_cparams(1),
        name="attn_sample",
    )(sinks, q, kv, g, h, cache_k, cache_v, bias, wout_bf16)


def _rwkv_proj_kernel(h_ref, shift_ref, gain_ref, mu_ref, win_ref, w0_ref, w1_ref, w2_ref,
                      a0_ref, a1_ref, a2_ref,
                      r_ref, k_ref, v_ref, g_ref, ld_ref, a_ref, xn_ref, *scratch, seq_len):
    xn = _rmsnorm(h_ref[...], gain_ref[...])
    tm = xn.shape[0]
    rolled = pltpu.roll(xn, 1, 0)
    row = lax.broadcasted_iota(jnp.int32, (tm, 1), 0)
    if seq_len is None:
        carry_ref, = scratch

        @pl.when(pl.program_id(1) == 0)
        def _():
            carry_ref[...] = shift_ref[0]

        xprev = jnp.where(row == 0, carry_ref[...], rolled)
        carry_ref[...] = xn[tm - 1:tm, :]
        xn_ref[0] = xn[tm - 1:tm, :]
    else:
        xprev = jnp.where(row % seq_len == 0, shift_ref[...], rolled)
        xn_ref[...] = xn
    dx = xprev - xn

    def mix(c):
        return (xn + dx * mu_ref[c:c + 1, :]).astype(BF16)

    for c, o_ref in enumerate((r_ref, k_ref, v_ref, g_ref)):
        o_ref[...] = jnp.dot(mix(c), win_ref[c], preferred_element_type=F32)
    lw = jnp.tanh(jnp.dot(mix(4), w1_ref[...], preferred_element_type=F32))
    z = w0_ref[...] + jnp.dot(lw.astype(BF16), w2_ref[...], preferred_element_type=F32)
    u = -z
    softplus = jnp.maximum(u, 0.0) + jnp.log(1.0 + jnp.exp(-jnp.abs(u)))
    ld_ref[...] = -jnp.exp(-softplus - 0.5)
    la = jnp.dot(mix(5), a1_ref[...], preferred_element_type=F32)
    a_ref[...] = _sigmoid(a0_ref[...] + jnp.dot(la.astype(BF16), a2_ref[...],
                                                preferred_element_type=F32))


def _rwkv_proj_call(h, shift, p, nbatch, ntile, tm, seq_len):
    n = h.shape[0]
    row = lambda b, i: (b * ntile + i, 0)
    full2 = lambda b, i: (0, 0)
    if seq_len is None:
        shift_spec = pl.BlockSpec((1, 1, D_MODEL), lambda b, i: (b, 0, 0))
        xn_spec = pl.BlockSpec((1, 1, D_MODEL), lambda b, i: (b, 0, 0))
        xn_shape = jax.ShapeDtypeStruct((nbatch, 1, D_MODEL), F32)
        scratch = [pltpu.VMEM((1, D_MODEL), F32)]
    else:
        shift_spec = pl.BlockSpec((tm, D_MODEL), row)
        xn_spec = pl.BlockSpec((tm, D_MODEL), row)
        xn_shape = jax.ShapeDtypeStruct((n, D_MODEL), F32)
        scratch = []
    lora = p["w1"].shape[1]
    big = jax.ShapeDtypeStruct((n, D_MODEL), F32)
    return pl.pallas_call(
        functools.partial(_rwkv_proj_kernel, seq_len=seq_len),
        grid=(nbatch, ntile),
        in_specs=[pl.BlockSpec((tm, D_MODEL), row),
                  shift_spec,
                  pl.BlockSpec((1, D_MODEL), full2),
                  pl.BlockSpec(p["mu"].shape, full2),
                  pl.BlockSpec(p["w_in"].shape, lambda b, i: (0, 0, 0)),
                  pl.BlockSpec((1, D_MODEL), full2),
                  pl.BlockSpec((D_MODEL, lora), full2),
                  pl.BlockSpec((lora, D_MODEL), full2),
                  pl.BlockSpec((1, D_MODEL), full2),
                  pl.BlockSpec((D_MODEL, lora), full2),
                  pl.BlockSpec((lora, D_MODEL), full2)],
        out_specs=[pl.BlockSpec((tm, D_MODEL), row)] * 6 + [xn_spec],
        out_shape=[big] * 6 + [xn_shape],
        scratch_shapes=scratch,
        compiler_params=_cparams(2),
        name="rwkv_proj",
    )(h, shift, p["gain"], p["mu"], p["w_in"], p["w0"], p["w1"], p["w2"], p["a0"], p["a1"], p["a2"])


NN = (((1,), (0,)), ((), ()))
NT = (((1,), (1,)), ((), ()))
TN = (((0,), (0,)), ((), ()))


def _split(x):
    hi = x.astype(BF16)
    return hi, (x - hi.astype(F32)).astype(BF16)


def _mm(a, b, dn):
    return lax.dot_general(a, b, dn, preferred_element_type=F32)


def _mm3(a, b, dn):
    ah, al = _split(a)
    bh, bl = _split(b)
    return _mm(al, bh, dn) + _mm(ah, bl, dn) + _mm(ah, bh, dn)


def _seg_sum(x, ones_bf16):
    hi, lo = _split(x)
    return _mm(lo, ones_bf16, NN) + _mm(hi, ones_bf16, NN)


def _mm3s(a, b, dn):
    (ah, al), (bh, bl) = a, b
    return _mm(al, bh, dn) + _mm(ah, bl, dn) + _mm(ah, bh, dn)


def _wkv_head_inputs(k, a, kk_gain, ka_gain):
    kkx = k * kk_gain
    nrm = jnp.sqrt(jnp.sum(kkx * kkx, axis=1, keepdims=True))
    kk = kkx / jnp.maximum(nrm, 1e-12)
    kh = k * (1.0 + (a - 1.0) * ka_gain)
    return kh, -kk, kk * a


def _wkv_chunk_kernel(r_ref, k_ref, v_ref, a_ref, ld_ref, kkg_ref, kag_ref, y_ref, sout_ref, st_ref):
    ci = pl.program_id(2)
    c = r_ref.shape[0]
    pairs = range(r_ref.shape[1] // LANES)
    half = R_HEAD_DIM

    @pl.when(ci == 0)
    def _():
        st_ref[...] = jnp.zeros_like(st_ref)

    def iota(shape, dim):
        return lax.broadcasted_iota(jnp.int32, shape, dim)

    lo = iota((1, LANES), 1) < half
    row2, lane2 = iota((2 * c, LANES), 0), iota((2 * c, LANES), 1)
    t_row, t_col = row2 & (c - 1), lane2 & (half - 1)
    causal = t_row + (row2 >= c).astype(jnp.int32) > t_col
    same_half = (row2 < half) == (lane2 < half)
    ones_blk = same_half.astype(BF16)
    eye_full = row2 == lane2
    eye_pair = (iota((c, LANES), 0) == (iota((c, LANES), 1) & (half - 1))).astype(F32)
    tri = (iota((c, c), 0) >= iota((c, c), 1)).astype(BF16)

    def bd(z):
        return jnp.concatenate([jnp.where(lo, z, 0.0), jnp.where(lo, 0.0, z)], axis=0)

    def bd_swap(z):
        return jnp.concatenate([jnp.where(lo, 0.0, z), jnp.where(lo, z, 0.0)], axis=0)

    def split_map(z, f):
        hi, lw = _split(z)
        return f(hi), f(lw)

    tiles = [slice(j * LANES, (j + 1) * LANES) for j in pairs]
    r, k, v, a, ld = ([ref[:, sl] for sl in tiles] for ref in (r_ref, k_ref, v_ref, a_ref, ld_ref))
    kkx = [k[j] * kkg_ref[:, tiles[j]] for j in pairs]
    ssq = [_seg_sum(kkx[j] * kkx[j], ones_blk) for j in pairs]
    ld_s = [_split(ld[j]) for j in pairs]
    cs = [_mm(tri, ld_s[j][1], NN) + _mm(tri, ld_s[j][0], NN) for j in pairs]
    x_s, yn_s, ys_s, bke_s, vbd_s, w_end = [], [], [], [], [], []
    for j in pairs:
        kk = kkx[j] / jnp.maximum(jnp.sqrt(ssq[j]), 1e-12)
        kh = k[j] * (1.0 + (a[j] - 1.0) * kag_ref[:, tiles[j]])
        bv = kk * a[j]
        cs_end = cs[j][c - 1:c, :]
        e_pos = jnp.exp(cs[j])
        e_neg = jnp.exp(-cs[j])
        e_end = jnp.exp(cs_end - cs[j])
        at = -kk * jnp.exp(cs[j] - ld[j])
        x_s.append(_split(jnp.concatenate([at, r[j] * e_pos], axis=0)))
        bt, kt = _split(bv * e_neg), _split(kh * e_neg)
        yn_s.append(tuple(jnp.concatenate([p, q], axis=0) for p, q in zip(bt, kt)))
        ys_s.append(tuple(jnp.concatenate([q, p], axis=0) for p, q in zip(bt, kt)))
        bke_s.append(_split(jnp.concatenate([bv * e_end, kh * e_end], axis=0)))
        vbd_s.append(split_map(v[j], bd))
        w_end.append(jnp.exp(cs_end))
    ga = [_mm3s(tuple(jnp.where(lo, p, 0.0) for p in x_s[j]), ys_s[j], NT) for j in pairs]
    gb = [_mm3s(tuple(jnp.where(lo, 0.0, p) for p in x_s[j]), yn_s[j], NT) for j in pairs]
    lk = [jnp.where(causal, jnp.where(lo, ga[j], gb[j]), 0.0) for j in pairs]
    lb = [jnp.where(causal, jnp.where(lo, gb[j], ga[j]), 0.0) for j in pairs]
    pw = [lb[j][:c] for j in pairs]
    acc = [eye_pair + pw[j] for j in pairs]
    pw = [_mm(pw[j].astype(BF16), bd(pw[j].astype(BF16)), NN) for j in pairs]
    for _ in range(int(math.log2(c)) - 2):
        both = [_mm(jnp.concatenate([pw[j], acc[j]], axis=0).astype(BF16), bd(pw[j].astype(BF16)), NN)
                for j in pairs]
        pw = [both[j][:c] for j in pairs]
        acc = [acc[j] + both[j][c:] for j in pairs]
    tinv = [acc[j] + _mm(acc[j].astype(BF16), bd(pw[j].astype(BF16)), NN) for j in pairs]
    st = [st_ref[j] for j in pairs]
    from_state = [_mm3s(x_s[j], _split(st[j]), NN) for j in pairs]
    from_v = [_mm3s(_split(lk[j]), vbd_s[j], NN) for j in pairs]
    u = [_mm3s(_split(tinv[j]), split_map(from_state[j][:c] + from_v[j][:c], bd_swap), NN) for j in pairs]
    for j in pairs:
        y_ref[:, tiles[j]] = (from_state[j][c:] + from_v[j][c:]
                              + _mm3s(_split(lb[j][c:]), split_map(u[j], bd_swap), NN))
    for j in pairs:
        upd = _mm3s(bke_s[j], _split(jnp.concatenate([u[j], v[j]], axis=0)), TN)
        w_col = jnp.sum(jnp.where(eye_full, w_end[j], 0.0), axis=1, keepdims=True)
        st_new = w_col * st[j] + jnp.where(same_half, upd, 0.0)
        st_ref[j] = st_new
        sout_ref[0, j] = st_new


def _wkv_chunk_call(r, k, v, a, ld, kk_gain, ka_gain, nbatch, seq):
    n = r.shape[0]
    c = WKV_CHUNK
    nchunk = seq // c
    width = WKV_PAIRS_PER_STEP * LANES
    ngroup = D_MODEL // width
    tile = pl.BlockSpec((c, width), lambda b, j, t: (b * nchunk + t, j))
    par = pl.BlockSpec((1, width), lambda b, j, t: (0, j))
    return pl.pallas_call(
        _wkv_chunk_kernel,
        grid=(nbatch, ngroup, nchunk),
        in_specs=[tile] * 5 + [par, par],
        out_specs=[tile, pl.BlockSpec((1, WKV_PAIRS_PER_STEP, LANES, LANES), lambda b, j, t: (b, j, 0, 0))],
        out_shape=[jax.ShapeDtypeStruct((n, D_MODEL), F32),
                   jax.ShapeDtypeStruct((nbatch, D_MODEL // LANES, LANES, LANES), F32)],
        scratch_shapes=[pltpu.VMEM((WKV_PAIRS_PER_STEP, LANES, LANES), F32)],
        compiler_params=_cparams(3),
        name="wkv_chunk",
    )(r, k, v, a, ld, kk_gain, ka_gain)


WKV_SB = 8


def _wkv_seq_kernel(r_ref, k_ref, v_ref, a_ref, ld_ref, kkg_ref, kag_ref, s_ref, y_ref, sout_ref,
                    *, seq_len):
    n = R_HEAD_DIM
    eye = (lax.broadcasted_iota(jnp.int32, (n, n), 0) == lax.broadcasted_iota(jnp.int32, (n, n), 1))

    per_group = SUBLANES // seq_len

    def one_group(gi, carry):
        rows = pl.ds(pl.multiple_of(gi * SUBLANES, SUBLANES), SUBLANES)
        for h in range(R_HEADS):
            sl = slice(h * n, (h + 1) * n)
            r, k, v, a, ld = (ref[rows, sl] for ref in (r_ref, k_ref, v_ref, a_ref, ld_ref))
            kh, av, bv = _wkv_head_inputs(k, a, kkg_ref[:, sl], kag_ref[:, sl])
            w = jnp.exp(ld)
            ys = []
            for q in range(per_group):
                s = gi * per_group + q
                state = s_ref[s, h]
                for t in range(q * seq_len, (q + 1) * seq_len):
                    tt = slice(t, t + 1)
                    sa = jnp.sum(state * av[tt], axis=1, keepdims=True)
                    vcol = jnp.sum(jnp.where(eye, v[tt], 0.0), axis=1, keepdims=True)
                    state = state * w[tt] + sa * bv[tt] + vcol * kh[tt]
                    ycol = jnp.sum(state * r[tt], axis=1, keepdims=True)
                    ys.append(jnp.sum(jnp.where(eye, ycol, 0.0), axis=0, keepdims=True))
                sout_ref[s, h] = state
            y_ref[rows, sl] = jnp.concatenate(ys, axis=0)
        return carry

    lax.fori_loop(0, WKV_SB // per_group, one_group, 0)


def _wkv_seq_call(r, k, v, a, ld, kk_gain, ka_gain, state, seq_len):
    n = r.shape[0]
    nseq = state.shape[0]
    rows = WKV_SB * seq_len
    tile = pl.BlockSpec((rows, D_MODEL), lambda i: (i, 0))
    par = pl.BlockSpec((1, D_MODEL), lambda i: (0, 0))
    sspec = pl.BlockSpec((WKV_SB, R_HEADS, R_HEAD_DIM, R_HEAD_DIM), lambda i: (i, 0, 0, 0))
    return pl.pallas_call(
        functools.partial(_wkv_seq_kernel, seq_len=seq_len),
        grid=(nseq // WKV_SB,),
        in_specs=[tile] * 5 + [par, par, sspec],
        out_specs=[tile, sspec],
        out_shape=[jax.ShapeDtypeStruct((n, D_MODEL), F32),
                   jax.ShapeDtypeStruct(state.shape, F32)],
        compiler_params=_cparams(1),
        name="wkv_seq",
    )(r, k, v, a, ld, kk_gain, ka_gain, state)


def _rwkv_out_kernel(y_ref, r_ref, k_ref, v_ref, a_ref, g_ref, h_ref, kag_ref, rk_ref, lng_ref,
                     lnb_ref, wout_ref, fg_ref, out_ref, z_ref):
    half_r = lax.broadcasted_iota(jnp.int32, (LANES, LANES), 0) // R_HEAD_DIM
    half_c = lax.broadcasted_iota(jnp.int32, (LANES, LANES), 1) // R_HEAD_DIM
    ones = (half_r == half_c).astype(BF16)
    inv_n = 1.0 / R_HEAD_DIM
    for j in range(D_MODEL // LANES):
        sl = slice(j * LANES, (j + 1) * LANES)
        y = y_ref[:, sl]
        d = y - _seg_sum(y, ones) * inv_n
        var = _seg_sum(d * d, ones) * inv_n
        yn = d * lax.rsqrt(var + GN_EPS) * lng_ref[:, sl] + lnb_ref[:, sl]
        a = a_ref[:, sl]
        r = r_ref[:, sl]
        kh = k_ref[:, sl] * (1.0 + (a - 1.0) * kag_ref[:, sl])
        bonus = _seg_sum(r * kh * rk_ref[:, sl], ones) * v_ref[:, sl]
        g = g_ref[:, sl]
        z_ref[:, sl] = ((yn + bonus) * (g * _sigmoid(g))).astype(BF16)
    h2 = h_ref[...] + jnp.dot(z_ref[...], wout_ref[...], preferred_element_type=F32)
    out_ref[...] = _rmsnorm(h2, fg_ref[...])


def _rwkv_out_call(y, r, k, v, a, g, h, p, nbatch, ntile, tm, in_tiles_per_batch, skip):
    src = lambda b, i: (b * in_tiles_per_batch + i + skip, 0)
    dst = lambda b, i: (b * ntile + i, 0)
    par = pl.BlockSpec((1, D_MODEL), lambda b, i: (0, 0))
    tile = pl.BlockSpec((tm, D_MODEL), src)
    return pl.pallas_call(
        _rwkv_out_kernel,
        grid=(nbatch, ntile),
        in_specs=[tile] * 7 + [par] * 4 + [pl.BlockSpec((D_MODEL, D_MODEL), lambda b, i: (0, 0)), par],
        out_specs=pl.BlockSpec((tm, D_MODEL), dst),
        out_shape=jax.ShapeDtypeStruct((nbatch * ntile * tm, D_MODEL), F32),
        scratch_shapes=[pltpu.VMEM((tm, D_MODEL), BF16)],
        compiler_params=_cparams(2),
        name="rwkv_out",
    )(y, r, k, v, a, g, h, p["k_a"], p["r_k"], p["ln_g"], p["ln_b"], p["w_out"], p["final_gain"])


def _prompt_bucket():
    qi = np.arange(BLOCK)[:, None]
    kj = np.arange(2 * BLOCK)[None, :]
    rel = BLOCK + qi - kj
    return np.where((rel >= 0) & (rel < WINDOW), _t5_bucket_np(rel), -1).astype(np.int32)


def _sample_bucket(keep, t_new):
    t = np.arange(8)[:, None]
    j = np.arange(SAMPLE_KEYS)[None, :]
    rel = keep + t - j
    ok = (rel >= 0) & (rel < WINDOW) & (j < keep + t_new) & (t < t_new)
    return np.where(ok, _t5_bucket_np(rel), -1).astype(np.int32)


def kernel(x_prompt, x_sample, cache_win_k, cache_win_v, state_wkv, state_shift, meta_tokens, rel_bias_table, norm_gain, final_gain, attn_w_in, attn_sinks, attn_w_out, rwkv_mu, rwkv_w_in, rwkv_w0, rwkv_w1, rwkv_w2, rwkv_a0, rwkv_a1, rwkv_a2, rwkv_k_k, rwkv_k_a, rwkv_r_k, rwkv_ln_gamma, rwkv_ln_beta, rwkv_w_out):
    nb, seq, _ = x_prompt.shape
    ns, t_new, _ = x_sample.shape
    keep = cache_win_k.shape[2]
    lp = seq + BLOCK
    nblk = lp // BLOCK
    row = lambda x: x.reshape(1, D_MODEL)

    w_in0 = attn_w_in[0].astype(BF16)
    w_out0 = attn_w_out[0].astype(BF16)
    gain0 = row(norm_gain[0])
    sinks = attn_sinks[0]
    rp = dict(gain=row(norm_gain[1]), mu=rwkv_mu[0], w_in=rwkv_w_in[0].astype(BF16),
              w0=row(rwkv_w0[0]), w1=rwkv_w1[0].astype(BF16), w2=rwkv_w2[0].astype(BF16),
              a0=row(rwkv_a0[0]), a1=rwkv_a1[0].astype(BF16), a2=rwkv_a2[0].astype(BF16),
              k_a=row(rwkv_k_a[0]), r_k=row(rwkv_r_k[0]), ln_g=row(rwkv_ln_gamma[0]),
              ln_b=row(rwkv_ln_beta[0]), w_out=rwkv_w_out[0].astype(BF16),
              final_gain=row(final_gain))
    kk_gain = row(rwkv_k_k[0])

    bias_p = _bias_call(rel_bias_table, _prompt_bucket())
    bias_s = _bias_call(rel_bias_table, _sample_bucket(keep, t_new))

    meta = jnp.broadcast_to(meta_tokens[None].astype(F32), (nb, N_META, D_MODEL))
    h0 = jnp.concatenate([jnp.zeros((nb, PAD, D_MODEL), F32), meta, x_prompt], axis=1)
    h0 = h0.reshape(nb * lp, D_MODEL)
    q, kv, g = _attn_proj_call(h0, gain0, w_in0, 256, BF16)
    h1 = _attn_prompt_call(sinks, q, kv, g, h0, bias_p, w_out0, nb, nblk)
    kv3 = kv.reshape(nb, lp, 2 * A_KV_WIDTH)[:, lp - WINDOW:, :]
    win_k_p = kv3[:, :, :A_KV_WIDTH].reshape(1, nb, WINDOW, A_KV_HEADS, A_HEAD_DIM)
    win_v_p = kv3[:, :, A_KV_WIDTH:].reshape(1, nb, WINDOW, A_KV_HEADS, A_HEAD_DIM)

    shift0 = jnp.zeros((nb, 1, D_MODEL), F32)
    r, k, v, g1, ld, a, xlast = _rwkv_proj_call(h1, shift0, rp, nb, nblk, BLOCK, None)
    y, st = _wkv_chunk_call(r, k, v, a, ld, kk_gain, rp["k_a"], nb, lp)
    y_prompt = _rwkv_out_call(y, r, k, v, a, g1, h1, rp, nb, nblk - 1, BLOCK, nblk, 1)
    y_prompt = y_prompt.reshape(nb, seq, D_MODEL)
    st = st.reshape(nb, D_MODEL // LANES, 2, R_HEAD_DIM, 2, R_HEAD_DIM)
    st = jnp.stack([st[:, :, 0, :, 0, :], st[:, :, 1, :, 1, :]], axis=2)
    wkv_p = jnp.swapaxes(st, -1, -2).reshape(1, nb, R_HEADS, R_HEAD_DIM, R_HEAD_DIM)
    shift_p = xlast.reshape(1, nb, D_MODEL)

    xs = x_sample.reshape(ns * t_new, D_MODEL)
    qs, kvs, gs = _attn_proj_call(xs, gain0, w_in0, 256, F32)
    ck = cache_win_k[0].reshape(ns, keep, A_KV_WIDTH)
    cv = cache_win_v[0].reshape(ns, keep, A_KV_WIDTH)
    h1s, nk, nv = _attn_sample_call(sinks, qs, kvs, gs, xs, ck, cv, bias_s, w_out0, t_new)
    win_k_s = nk.reshape(1, ns, keep, A_KV_HEADS, A_HEAD_DIM)
    win_v_s = nv.reshape(1, ns, keep, A_KV_HEADS, A_HEAD_DIM)

    shift_rows = jnp.repeat(state_shift[0], t_new, axis=0)
    tms = 256
    rs, ks, vs, g1s, lds, as_, xns = _rwkv_proj_call(h1s, shift_rows, rp, 1, ns * t_new // tms, tms, t_new)
    ys, st_s = _wkv_seq_call(rs, ks, vs, as_, lds, kk_gain, rp["k_a"], state_wkv[0], t_new)
    y_sample = _rwkv_out_call(ys, rs, ks, vs, as_, g1s, h1s, rp, 1, ns * t_new // tms, tms, 0, 0)
    y_sample = y_sample.reshape(ns, t_new, D_MODEL)
    wkv_s = st_s[None]
    shift_s = xns.reshape(ns, t_new, D_MODEL)[:, t_new - 1][None]

    return (y_prompt, y_sample, win_k_p, win_v_p, wkv_p, shift_p, win_k_s, win_v_s, wkv_s, shift_s)
```

```python
import functools
import math

import numpy as np
import jax
import jax.numpy as jnp
from jax import lax
from jax.experimental import pallas as pl
from jax.experimental.pallas import tpu as pltpu

F32 = jnp.float32
BF16 = jnp.bfloat16

D_MODEL = 1024
N_META = 16
RMS_EPS = 1e-6
A_HEADS = 16
A_KV_HEADS = 4
A_HEAD_DIM = 64
A_WIDTH = A_HEADS * A_HEAD_DIM
A_KV_WIDTH = A_KV_HEADS * A_HEAD_DIM
WINDOW = 128
BLOCK = 128
N_BUCKETS = 32
MAX_DISTANCE = 128
R_HEAD_DIM = 64
R_HEADS = D_MODEL // R_HEAD_DIM
GN_EPS = 64e-5

LANES = 128
SUBLANES = 8
PAD = BLOCK - N_META
NEG = -1e30
WKV_CHUNK = 64
WKV_PAIRS_PER_STEP = 8
WKV_SAMPLE_PAIRS = 4
VMEM_LIMIT = 56 * 1024 * 1024


def _cparams(n_axes):
    return pltpu.CompilerParams(dimension_semantics=("arbitrary",) * n_axes,
                                vmem_limit_bytes=VMEM_LIMIT)


def _rmsnorm(x, gain):
    return x * lax.rsqrt(jnp.mean(x * x, axis=-1, keepdims=True) + RMS_EPS) * gain


def _sigmoid(x):
    return 1.0 / (1.0 + jnp.exp(-x))


def _iota(shape, dim):
    return lax.broadcasted_iota(jnp.int32, shape, dim)


def _t5_bucket_np(rel):
    n = np.maximum(rel, 0)
    max_exact = N_BUCKETS // 2
    nf = np.maximum(n, max_exact).astype(np.float32)
    scale = np.float32(math.log(MAX_DISTANCE / max_exact))
    large = max_exact + (np.log(nf / np.float32(max_exact)) / scale
                         * np.float32(N_BUCKETS - max_exact)).astype(np.int32)
    large = np.minimum(large, N_BUCKETS - 1)
    return np.where(n < max_exact, n, large).astype(np.int32)


def _bias_kernel(table_ref, bucket_ref, out_ref):
    h = pl.program_id(0)
    bk = bucket_ref[...]
    acc = jnp.full(bk.shape, NEG, F32)
    for b in range(N_BUCKETS):
        acc = jnp.where(bk == b, table_ref[b, h], acc)
    out_ref[0] = acc


def _bias_call(table, bucket_np):
    r, c = bucket_np.shape
    return pl.pallas_call(
        _bias_kernel,
        grid=(A_HEADS,),
        in_specs=[pl.BlockSpec(memory_space=pltpu.SMEM),
                  pl.BlockSpec((r, c), lambda h: (0, 0))],
        out_specs=pl.BlockSpec((1, r, c), lambda h: (h, 0, 0)),
        out_shape=jax.ShapeDtypeStruct((A_HEADS, r, c), F32),
        compiler_params=_cparams(1),
        name="bias_expand",
    )(table, jnp.asarray(bucket_np))


def _attn_proj_kernel(x_ref, gain_ref, w_ref, q_ref, kv_ref, g_ref):
    xn = _rmsnorm(x_ref[...], gain_ref[...])
    proj = jnp.dot(xn.astype(BF16), w_ref[...], preferred_element_type=F32)
    q_ref[...] = (proj[:, :A_WIDTH] * (A_HEAD_DIM ** -0.5)).astype(q_ref.dtype)
    kv_ref[...] = proj[:, A_WIDTH:A_WIDTH + 2 * A_KV_WIDTH]
    g_ref[...] = proj[:, A_WIDTH + 2 * A_KV_WIDTH:]


def _attn_proj_call(x2d, gain, w_bf16, tm, q_dtype):
    n = x2d.shape[0]
    wcols = w_bf16.shape[1]
    return pl.pallas_call(
        _attn_proj_kernel,
        grid=(n // tm,),
        in_specs=[pl.BlockSpec((tm, D_MODEL), lambda i: (i, 0)),
                  pl.BlockSpec((1, D_MODEL), lambda i: (0, 0)),
                  pl.BlockSpec((D_MODEL, wcols), lambda i: (0, 0))],
        out_specs=[pl.BlockSpec((tm, A_WIDTH), lambda i: (i, 0)),
                   pl.BlockSpec((tm, 2 * A_KV_WIDTH), lambda i: (i, 0)),
                   pl.BlockSpec((tm, A_WIDTH), lambda i: (i, 0))],
        out_shape=[jax.ShapeDtypeStruct((n, A_WIDTH), q_dtype),
                   jax.ShapeDtypeStruct((n, 2 * A_KV_WIDTH), F32),
                   jax.ShapeDtypeStruct((n, A_WIDTH), F32)],
        compiler_params=_cparams(1),
        name="attn_proj",
    )(x2d, gain, w_bf16)


def _padded_kv_tiles(kv, c):
    lo = _iota((1, LANES), 1) < A_HEAD_DIM
    j = c // 2
    out = []
    for base in (0, A_KV_WIDTH):
        t = kv[:, base + j * LANES: base + (j + 1) * LANES]
        tr = pltpu.roll(t, A_HEAD_DIM, 1)
        if c % 2 == 0:
            even, odd = jnp.where(lo, t, 0.0), jnp.where(lo, 0.0, tr)
        else:
            even, odd = jnp.where(lo, tr, 0.0), jnp.where(lo, 0.0, t)
        out += [even.astype(BF16), odd.astype(BF16)]
    return out


def _mm_nt(a, b):
    return lax.dot_general(a, b, (((1,), (1,)), ((), ())), preferred_element_type=F32)


def _attend_pair(q_tile, ke, ko, ve, vo, bias_e, bias_o, sink_e, sink_o, kvalid):
    total = None
    for kk, vv, bias, sink in ((ke, ve, bias_e, sink_e), (ko, vo, bias_o, sink_o)):
        s = _mm_nt(q_tile, kk) + bias
        if kvalid is not None:
            s = jnp.where(kvalid, s, NEG)
        m = jnp.maximum(jnp.max(s, axis=1, keepdims=True), sink)
        p = jnp.exp(s - m)
        den = jnp.sum(p, axis=1, keepdims=True) + jnp.exp(sink - m)
        o = jnp.dot(p.astype(BF16), vv, preferred_element_type=F32) * (1.0 / den)
        total = o if total is None else total + o
    return total


def _attn_prompt_kernel(sinks_ref, q_ref, kvc_ref, kvp_ref, g_ref, h_ref, bias_ref, wout_ref,
                        out_ref, og_ref):
    i = pl.program_id(1)
    kv = jnp.concatenate([kvp_ref[...], kvc_ref[...]], axis=0)
    col = _iota((1, 2 * BLOCK), 1)
    kvalid = (i - 1) * BLOCK + col >= PAD
    for c in range(A_KV_HEADS):
        ke, ko, ve, vo = _padded_kv_tiles(kv, c)
        for mm in (2 * c, 2 * c + 1):
            sl = slice(mm * LANES, (mm + 1) * LANES)
            he, ho = 2 * mm, 2 * mm + 1
            o = _attend_pair(q_ref[:, sl], ke, ko, ve, vo, bias_ref[he], bias_ref[ho],
                             sinks_ref[he], sinks_ref[ho], kvalid)
            gt = g_ref[:, sl]
            og_ref[:, sl] = (o * (gt * _sigmoid(gt))).astype(BF16)
    out_ref[...] = h_ref[...] + jnp.dot(og_ref[...], wout_ref[...], preferred_element_type=F32)


def _attn_prompt_call(sinks, q, kv, g, h, bias, wout_bf16, nbatch, nblk):
    n = q.shape[0]
    row = lambda b, i: (b * nblk + i, 0)
    prev = lambda b, i: (b * nblk + jnp.maximum(i - 1, 0), 0)
    return pl.pallas_call(
        _attn_prompt_kernel,
        grid=(nbatch, nblk),
        in_specs=[pl.BlockSpec(memory_space=pltpu.SMEM),
                  pl.BlockSpec((BLOCK, A_WIDTH), row),
                  pl.BlockSpec((BLOCK, 2 * A_KV_WIDTH), row),
                  pl.BlockSpec((BLOCK, 2 * A_KV_WIDTH), prev),
                  pl.BlockSpec((BLOCK, A_WIDTH), row),
                  pl.BlockSpec((BLOCK, D_MODEL), row),
                  pl.BlockSpec((A_HEADS, BLOCK, 2 * BLOCK), lambda b, i: (0, 0, 0)),
                  pl.BlockSpec((A_WIDTH, D_MODEL), lambda b, i: (0, 0))],
        out_specs=pl.BlockSpec((BLOCK, D_MODEL), row),
        out_shape=jax.ShapeDtypeStruct((n, D_MODEL), F32),
        scratch_shapes=[pltpu.VMEM((BLOCK, A_WIDTH), BF16)],
        compiler_params=_cparams(2),
        name="attn_prompt",
    )(sinks, q, kv, kv, g, h, bias, wout_bf16)


SAMPLE_SB = 8
SAMPLE_KEYS = 2 * BLOCK


def _attn_sample_kernel(sinks_ref, q_ref, kvn_ref, g_ref, h_ref, ck_ref, cv_ref, bias0_ref, bias1_ref,
                        wout_ref, out_ref, nk_ref, nv_ref, og_ref, *, t_new):
    keep = ck_ref.shape[1]
    lo = _iota((1, LANES), 1) < A_HEAD_DIM
    stack = 8 * SUBLANES
    own = (_iota((stack, 1), 0) & (SUBLANES - 1)) // t_new
    piece = _iota((stack, 1), 0) // SUBLANES
    bias_refs = (bias0_ref, bias1_ref)
    n_tile = A_KV_WIDTH // LANES
    pair_ids = range(SAMPLE_SB * t_new // SUBLANES)

    bias_c, bias_n, sink = [], [], []
    for j in range(n_tile):
        heads = slice(8 * j, 8 * j + 8)
        bias_c.append(bias0_ref[heads, :, :keep].reshape(stack, keep))
        bias_n.append([ref[heads, :, keep:keep + SUBLANES].reshape(stack, SUBLANES) for ref in bias_refs])
        col = jnp.zeros((stack, 1), F32)
        for gq in range(8):
            col = jnp.where(piece == gq, sinks_ref[8 * j + gq], col)
        sink.append(col)

    chains = [(p, j, s) for p in pair_ids for j in range(n_tile) for s in range(SUBLANES // t_new)]
    qs = {}
    for p in pair_ids:
        rows = slice(p * SUBLANES, (p + 1) * SUBLANES)
        for j in range(n_tile):
            parts = []
            for gq in range(8):
                t = q_ref[rows, (4 * j + gq // 2) * LANES:(4 * j + gq // 2 + 1) * LANES]
                want_lo = gq < 4
                if (gq % 2 == 0) != want_lo:
                    t = pltpu.roll(t, A_HEAD_DIM, 1)
                parts.append(jnp.where(lo, t, 0.0) if want_lo else jnp.where(lo, 0.0, t))
            qs[p, j] = jnp.concatenate(parts, axis=0).astype(BF16)

    def kv_tiles(p, j, s, base):
        seq = p * (SUBLANES // t_new) + s
        cache = (ck_ref if base == 0 else cv_ref)[seq][:, j * LANES:(j + 1) * LANES].astype(BF16)
        new = kvn_ref[p * SUBLANES:(p + 1) * SUBLANES, base + j * LANES:base + (j + 1) * LANES].astype(BF16)
        return cache, new

    sc, sn = [], []
    for p, j, s in chains:
        kc, kn = kv_tiles(p, j, s, 0)
        sc.append(_mm_nt(qs[p, j], kc) + bias_c[j])
        sn.append(_mm_nt(qs[p, j], kn) + bias_n[j][s])
    mx = [jnp.maximum(jnp.maximum(jnp.max(sc[i], axis=1, keepdims=True),
                                  jnp.max(sn[i], axis=1, keepdims=True)), sink[chains[i][1]])
          for i in range(len(chains))]
    pc = [jnp.exp(sc[i] - mx[i]) for i in range(len(chains))]
    pn = [jnp.exp(sn[i] - mx[i]) for i in range(len(chains))]
    den = [jnp.sum(pc[i], axis=1, keepdims=True) + jnp.sum(pn[i], axis=1, keepdims=True)
           + jnp.exp(sink[chains[i][1]] - mx[i]) for i in range(len(chains))]
    outs = {}
    for i, (p, j, s) in enumerate(chains):
        vc, vn = kv_tiles(p, j, s, A_KV_WIDTH)
        o = (jnp.dot(pc[i].astype(BF16), vc, preferred_element_type=F32)
             + jnp.dot(pn[i].astype(BF16), vn, preferred_element_type=F32)) * (1.0 / den[i])
        outs[p, j] = o if s == 0 else jnp.where(own == s, o, outs[p, j])
    for p in pair_ids:
        rows = slice(p * SUBLANES, (p + 1) * SUBLANES)
        for j in range(n_tile):
            o = outs[p, j]
            for gg in range(4):
                even = o[2 * gg * SUBLANES:(2 * gg + 1) * SUBLANES]
                odd = o[(2 * gg + 1) * SUBLANES:(2 * gg + 2) * SUBLANES]
                if gg < 2:
                    tile = jnp.where(lo, even, pltpu.roll(odd, A_HEAD_DIM, 1))
                else:
                    tile = jnp.where(lo, pltpu.roll(even, A_HEAD_DIM, 1), odd)
                sl = slice((4 * j + gg) * LANES, (4 * j + gg + 1) * LANES)
                gt = g_ref[rows, sl]
                og_ref[rows, sl] = tile * (gt * _sigmoid(gt))
    for seq in range(SAMPLE_SB):
        new = kvn_ref[seq * t_new:(seq + 1) * t_new, :]
        nk_ref[seq, :keep - t_new, :] = ck_ref[seq, t_new:, :]
        nk_ref[seq, keep - t_new:, :] = new[:, :A_KV_WIDTH]
        nv_ref[seq, :keep - t_new, :] = cv_ref[seq, t_new:, :]
        nv_ref[seq, keep - t_new:, :] = new[:, A_KV_WIDTH:]
    out_ref[...] = h_ref[...] + jnp.dot(og_ref[...].astype(BF16), wout_ref[...],
                                        preferred_element_type=F32)


def _attn_sample_call(sinks, q, kv, g, h, cache_k, cache_v, bias0, bias1, wout_bf16, t_new):
    nseq, keep = cache_k.shape[0], cache_k.shape[1]
    rows = SAMPLE_SB * t_new
    row = lambda i: (i, 0)
    cspec = pl.BlockSpec((SAMPLE_SB, keep, A_KV_WIDTH), lambda i: (i, 0, 0))
    bspec = pl.BlockSpec((A_HEADS, SUBLANES, SAMPLE_KEYS), lambda i: (0, 0, 0))
    return pl.pallas_call(
        functools.partial(_attn_sample_kernel, t_new=t_new),
        grid=(nseq // SAMPLE_SB,),
        in_specs=[pl.BlockSpec(memory_space=pltpu.SMEM),
                  pl.BlockSpec((rows, A_WIDTH), row),
                  pl.BlockSpec((rows, 2 * A_KV_WIDTH), row),
                  pl.BlockSpec((rows, A_WIDTH), row),
                  pl.BlockSpec((rows, D_MODEL), row),
                  cspec, cspec, bspec, bspec,
                  pl.BlockSpec((A_WIDTH, D_MODEL), lambda i: (0, 0))],
        out_specs=[pl.BlockSpec((rows, D_MODEL), row), cspec, cspec],
        out_shape=[jax.ShapeDtypeStruct((nseq * t_new, D_MODEL), F32),
                   jax.ShapeDtypeStruct(cache_k.shape, F32),
                   jax.ShapeDtypeStruct(cache_v.shape, F32)],
        scratch_shapes=[pltpu.VMEM((rows, A_WIDTH), F32)],
        compiler_params=_cparams(1),
        name="attn_sample",
    )(sinks, q, kv, g, h, cache_k, cache_v, bias0, bias1, wout_bf16)


def _rwkv_proj_kernel(h_ref, shift_ref, gain_ref, mu_ref, win_ref, w0_ref, w1_ref, w2_ref,
                      a0_ref, a1_ref, a2_ref,
                      r_ref, k_ref, v_ref, g_ref, ld_ref, a_ref, xn_ref, *scratch, seq_len):
    xn = _rmsnorm(h_ref[...], gain_ref[...])
    tm = xn.shape[0]
    rolled = pltpu.roll(xn, 1, 0)
    row = _iota((tm, 1), 0)
    if seq_len is None:
        carry_ref, = scratch

        @pl.when(pl.program_id(1) == 0)
        def _():
            carry_ref[...] = shift_ref[0]

        xprev = jnp.where(row == 0, carry_ref[...], rolled)
        carry_ref[...] = xn[tm - 1:tm, :]
        xn_ref[0] = xn[tm - 1:tm, :]
    else:
        xprev = jnp.where(row % seq_len == 0, shift_ref[...], rolled)
        xn_ref[...] = xn
    dx = xprev - xn

    def mix(c):
        return (xn + dx * mu_ref[c:c + 1, :]).astype(BF16)

    for c, o_ref in enumerate((r_ref, k_ref, v_ref, g_ref)):
        o_ref[...] = jnp.dot(mix(c), win_ref[c], preferred_element_type=F32)
    lw = jnp.tanh(jnp.dot(mix(4), w1_ref[...], preferred_element_type=F32))
    z = w0_ref[...] + jnp.dot(lw.astype(BF16), w2_ref[...], preferred_element_type=F32)
    u = -z
    softplus = jnp.maximum(u, 0.0) + jnp.log(1.0 + jnp.exp(-jnp.abs(u)))
    ld_ref[...] = -jnp.exp(-softplus - 0.5)
    la = jnp.dot(mix(5), a1_ref[...], preferred_element_type=F32)
    a_ref[...] = _sigmoid(a0_ref[...] + jnp.dot(la.astype(BF16), a2_ref[...],
                                                preferred_element_type=F32))


def _rwkv_proj_call(h, shift, p, nbatch, ntile, tm, seq_len):
    n = h.shape[0]
    row = lambda b, i: (b * ntile + i, 0)
    full2 = lambda b, i: (0, 0)
    if seq_len is None:
        shift_spec = pl.BlockSpec((1, 1, D_MODEL), lambda b, i: (b, 0, 0))
        xn_spec = pl.BlockSpec((1, 1, D_MODEL), lambda b, i: (b, 0, 0))
        xn_shape = jax.ShapeDtypeStruct((nbatch, 1, D_MODEL), F32)
        scratch = [pltpu.VMEM((1, D_MODEL), F32)]
    else:
        shift_spec = pl.BlockSpec((tm, D_MODEL), row)
        xn_spec = pl.BlockSpec((tm, D_MODEL), row)
        xn_shape = jax.ShapeDtypeStruct((n, D_MODEL), F32)
        scratch = []
    lora = p["w1"].shape[1]
    big = jax.ShapeDtypeStruct((n, D_MODEL), F32)
    return pl.pallas_call(
        functools.partial(_rwkv_proj_kernel, seq_len=seq_len),
        grid=(nbatch, ntile),
        in_specs=[pl.BlockSpec((tm, D_MODEL), row),
                  shift_spec,
                  pl.BlockSpec((1, D_MODEL), full2),
                  pl.BlockSpec(p["mu"].shape, full2),
                  pl.BlockSpec(p["w_in"].shape, lambda b, i: (0, 0, 0)),
                  pl.BlockSpec((1, D_MODEL), full2),
                  pl.BlockSpec((D_MODEL, lora), full2),
                  pl.BlockSpec((lora, D_MODEL), full2),
                  pl.BlockSpec((1, D_MODEL), full2),
                  pl.BlockSpec((D_MODEL, lora), full2),
                  pl.BlockSpec((lora, D_MODEL), full2)],
        out_specs=[pl.BlockSpec((tm, D_MODEL), row)] * 6 + [xn_spec],
        out_shape=[big] * 6 + [xn_shape],
        scratch_shapes=scratch,
        compiler_params=_cparams(2),
        name="rwkv_proj",
    )(h, shift, p["gain"], p["mu"], p["w_in"], p["w0"], p["w1"], p["w2"], p["a0"], p["a1"], p["a2"])


NN = (((1,), (0,)), ((), ()))
NT = (((1,), (1,)), ((), ()))
TN = (((0,), (0,)), ((), ()))


def _split(x):
    hi = x.astype(BF16)
    return hi, (x - hi.astype(F32)).astype(BF16)


def _mm(a, b, dn):
    return lax.dot_general(a, b, dn, preferred_element_type=F32)


def _mm3s(a, b, dn):
    (ah, al), (bh, bl) = a, b
    return _mm(al, bh, dn) + _mm(ah, bl, dn) + _mm(ah, bh, dn)


def _mm3(a, b, dn):
    return _mm3s(_split(a), _split(b), dn)


def _seg_sum(x, ones_bf16):
    hi, lo = _split(x)
    return _mm(lo, ones_bf16, NN) + _mm(hi, ones_bf16, NN)


def _lane_lo():
    return _iota((1, LANES), 1) < R_HEAD_DIM


def _bd(z):
    lo = _lane_lo()
    return jnp.concatenate([jnp.where(lo, z, 0.0), jnp.where(lo, 0.0, z)], axis=0)


def _bd_swap(z):
    lo = _lane_lo()
    return jnp.concatenate([jnp.where(lo, 0.0, z), jnp.where(lo, z, 0.0)], axis=0)


def _split_map(z, f):
    hi, lw = _split(z)
    return f(hi), f(lw)


def _wkv_batch_stage(r, k, v, a, ld, kkg, kag, seq_len):
    c = r[0].shape[0]
    pairs = range(len(r))
    half = R_HEAD_DIM
    lo = _lane_lo()
    row2, lane2 = _iota((2 * c, LANES), 0), _iota((2 * c, LANES), 1)
    t_row, t_col = row2 & (c - 1), lane2 & (half - 1)
    causal = t_row + (row2 >= c).astype(jnp.int32) > t_col
    rr, cc = _iota((c, c), 0), _iota((c, c), 1)
    tri = rr >= cc
    span = c
    if seq_len is not None:
        shift = int(math.log2(seq_len))
        causal = causal & ((t_row >> shift) == (t_col >> shift))
        same_seq = (rr >> shift) == (cc >> shift)
        tri = tri & same_seq
        span = seq_len
    tri = tri.astype(BF16)
    ones_blk = ((row2 < half) == (lane2 < half)).astype(BF16)
    eye_pair = (_iota((c, LANES), 0) == (_iota((c, LANES), 1) & (half - 1))).astype(F32)

    kkx = [k[j] * kkg[j] for j in pairs]
    ssq = [_seg_sum(kkx[j] * kkx[j], ones_blk) for j in pairs]
    ld_s = [_split(ld[j]) for j in pairs]
    cs = [_mm(tri, ld_s[j][1], NN) + _mm(tri, ld_s[j][0], NN) for j in pairs]
    if seq_len is None:
        tot = [cs[j][c - 1:c, :] for j in pairs]
    else:
        seg = same_seq.astype(BF16)
        tot = [_mm(seg, ld_s[j][1], NN) + _mm(seg, ld_s[j][0], NN) for j in pairs]
    x, x_s, yn_s, ys_s, bke, vbd_s = [], [], [], [], [], []
    for j in pairs:
        kk = kkx[j] / jnp.maximum(jnp.sqrt(ssq[j]), 1e-12)
        kh = k[j] * (1.0 + (a[j] - 1.0) * kag[j])
        bv = kk * a[j]
        e_neg = jnp.exp(-cs[j])
        e_end = jnp.exp(tot[j] - cs[j])
        x.append(jnp.concatenate([-kk * jnp.exp(cs[j] - ld[j]), r[j] * jnp.exp(cs[j])], axis=0))
        x_s.append(_split(x[j]))
        bt, kt = _split(bv * e_neg), _split(kh * e_neg)
        yn_s.append(tuple(jnp.concatenate([p, q], axis=0) for p, q in zip(bt, kt)))
        ys_s.append(tuple(jnp.concatenate([q, p], axis=0) for p, q in zip(bt, kt)))
        bke.append(jnp.concatenate([bv * e_end, kh * e_end], axis=0))
        vbd_s.append(_split_map(v[j], _bd))
    ga = [_mm3s(tuple(jnp.where(lo, p, 0.0) for p in x_s[j]), ys_s[j], NT) for j in pairs]
    gb = [_mm3s(tuple(jnp.where(lo, 0.0, p) for p in x_s[j]), yn_s[j], NT) for j in pairs]
    lk = [jnp.where(causal, jnp.where(lo, ga[j], gb[j]), 0.0) for j in pairs]
    lb = [jnp.where(causal, jnp.where(lo, gb[j], ga[j]), 0.0) for j in pairs]
    pw = [lb[j][:c] for j in pairs]
    acc = [eye_pair + pw[j] for j in pairs]
    pw = [_mm(pw[j].astype(BF16), _bd(pw[j].astype(BF16)), NN) for j in pairs]
    for _ in range(int(math.log2(span)) - 2):
        both = [_mm(jnp.concatenate([pw[j], acc[j]], axis=0).astype(BF16), _bd(pw[j].astype(BF16)), NN)
                for j in pairs]
        pw = [both[j][:c] for j in pairs]
        acc = [acc[j] + both[j][c:] for j in pairs]
    tinv = [acc[j] + _mm(acc[j].astype(BF16), _bd(pw[j].astype(BF16)), NN) for j in pairs]
    from_v = [_mm3s(_split(lk[j]), vbd_s[j], NN) for j in pairs]
    return dict(x=x, x_s=x_s, lb=lb, tinv=tinv, from_v=from_v, bke=bke, tot=tot)


def _wkv_finish(stage, j, from_state_a, from_state_r):
    c = from_state_a.shape[0]
    u = _mm3s(_split(stage["tinv"][j]), _split_map(from_state_a + stage["from_v"][j][:c], _bd_swap), NN)
    y = (from_state_r + stage["from_v"][j][c:]
         + _mm3s(_split(stage["lb"][j][c:]), _split_map(u, _bd_swap), NN))
    return u, y


def _wkv_chunk_kernel(r_ref, k_ref, v_ref, a_ref, ld_ref, kkg_ref, kag_ref, y_ref, sout_ref, st_ref):
    c = r_ref.shape[0]
    pairs = range(r_ref.shape[1] // LANES)
    tiles = [slice(j * LANES, (j + 1) * LANES) for j in pairs]

    @pl.when(pl.program_id(2) == 0)
    def _():
        st_ref[...] = jnp.zeros_like(st_ref)

    r, k, v, a, ld = ([ref[:, sl] for sl in tiles] for ref in (r_ref, k_ref, v_ref, a_ref, ld_ref))
    stage = _wkv_batch_stage(r, k, v, a, ld, [kkg_ref[:, sl] for sl in tiles],
                             [kag_ref[:, sl] for sl in tiles], None)
    row2, lane2 = _iota((2 * c, LANES), 0), _iota((2 * c, LANES), 1)
    same_half = (row2 < R_HEAD_DIM) == (lane2 < R_HEAD_DIM)
    eye_full = row2 == lane2
    st = [st_ref[j] for j in pairs]
    from_state = [_mm3s(stage["x_s"][j], _split(st[j]), NN) for j in pairs]
    us = []
    for j in pairs:
        u, y = _wkv_finish(stage, j, from_state[j][:c], from_state[j][c:])
        y_ref[:, tiles[j]] = y
        us.append(u)
    for j in pairs:
        upd = _mm3(stage["bke"][j], jnp.concatenate([us[j], v[j]], axis=0), TN)
        w_col = jnp.sum(jnp.where(eye_full, jnp.exp(stage["tot"][j]), 0.0), axis=1, keepdims=True)
        st_new = w_col * st[j] + jnp.where(same_half, upd, 0.0)
        st_ref[j] = st_new
        sout_ref[0, j] = st_new


def _wkv_chunk_call(r, k, v, a, ld, kk_gain, ka_gain, nbatch, seq):
    n = r.shape[0]
    c = WKV_CHUNK
    nchunk = seq // c
    width = WKV_PAIRS_PER_STEP * LANES
    ngroup = D_MODEL // width
    tile = pl.BlockSpec((c, width), lambda b, j, t: (b * nchunk + t, j))
    par = pl.BlockSpec((1, width), lambda b, j, t: (0, j))
    return pl.pallas_call(
        _wkv_chunk_kernel,
        grid=(nbatch, ngroup, nchunk),
        in_specs=[tile] * 5 + [par, par],
        out_specs=[tile, pl.BlockSpec((1, WKV_PAIRS_PER_STEP, LANES, LANES), lambda b, j, t: (b, j, 0, 0))],
        out_shape=[jax.ShapeDtypeStruct((n, D_MODEL), F32),
                   jax.ShapeDtypeStruct((nbatch, D_MODEL // LANES, LANES, LANES), F32)],
        scratch_shapes=[pltpu.VMEM((WKV_PAIRS_PER_STEP, LANES, LANES), F32)],
        compiler_params=_cparams(3),
        name="wkv_chunk",
    )(r, k, v, a, ld, kk_gain, ka_gain)


def _wkv_sample_kernel(r_ref, k_ref, v_ref, a_ref, ld_ref, kkg_ref, kag_ref, s_ref, y_ref, sout_ref,
                       *, seq_len):
    c = r_ref.shape[0]
    half = R_HEAD_DIM
    pairs = range(r_ref.shape[1] // LANES)
    tiles = [slice(j * LANES, (j + 1) * LANES) for j in pairs]
    r, k, v, a, ld = ([ref[:, sl] for sl in tiles] for ref in (r_ref, k_ref, v_ref, a_ref, ld_ref))
    stage = _wkv_batch_stage(r, k, v, a, ld, [kkg_ref[:, sl] for sl in tiles],
                             [kag_ref[:, sl] for sl in tiles], seq_len)
    row2, lane2 = _iota((2 * c, LANES), 0), _iota((2 * c, LANES), 1)
    same_half = (row2 < half) == (lane2 < half)
    per_group = SUBLANES // seq_len
    slab_seq = (_iota((2 * SUBLANES, LANES), 0) & (SUBLANES - 1)) // seq_len
    zeros = jnp.zeros((half, half), F32)

    def slab(z, g):
        return jnp.concatenate([z[g * SUBLANES:(g + 1) * SUBLANES],
                                z[c + g * SUBLANES:c + (g + 1) * SUBLANES]], axis=0)

    states = {}
    from_state = []
    for j in pairs:
        fa, fr = [], []
        for g in range(c // SUBLANES):
            xg = slab(stage["x"][j], g)
            acc = None
            for q in range(per_group):
                s = g * per_group + q
                sa, sb = s_ref[s, 2 * j], s_ref[s, 2 * j + 1]
                st = jnp.concatenate([jnp.concatenate([sa, zeros], axis=1),
                                      jnp.concatenate([zeros, sb], axis=1)], axis=0)
                states[j, s] = st
                part = _mm3(jnp.where(slab_seq == q, xg, 0.0), st, NT)
                acc = part if acc is None else acc + part
            fa.append(acc[:SUBLANES])
            fr.append(acc[SUBLANES:])
        from_state.append((jnp.concatenate(fa, axis=0), jnp.concatenate(fr, axis=0)))
    for j in pairs:
        u, y = _wkv_finish(stage, j, *from_state[j])
        y_ref[:, tiles[j]] = y
        uv = jnp.concatenate([u, v[j]], axis=0)
        w_end = jnp.exp(stage["tot"][j])
        for g in range(c // SUBLANES):
            uvg, bkg = slab(uv, g), slab(stage["bke"][j], g)
            for q in range(per_group):
                s = g * per_group + q
                upd = _mm3(jnp.where(slab_seq == q, uvg, 0.0), bkg, TN)
                row = g * SUBLANES + q * seq_len
                st_new = states[j, s] * w_end[row:row + 1, :] + jnp.where(same_half, upd, 0.0)
                sout_ref[s, 2 * j] = st_new[:half, :half]
                sout_ref[s, 2 * j + 1] = st_new[half:, half:]


def _wkv_sample_call(r, k, v, a, ld, kk_gain, ka_gain, state, seq_len):
    n = r.shape[0]
    nseq = state.shape[0]
    c = WKV_CHUNK
    per_step = c // seq_len
    width = WKV_SAMPLE_PAIRS * LANES
    tile = pl.BlockSpec((c, width), lambda i, j: (i, j))
    par = pl.BlockSpec((1, width), lambda i, j: (0, j))
    sspec = pl.BlockSpec((per_step, 2 * WKV_SAMPLE_PAIRS, R_HEAD_DIM, R_HEAD_DIM), lambda i, j: (i, j, 0, 0))
    return pl.pallas_call(
        functools.partial(_wkv_sample_kernel, seq_len=seq_len),
        grid=(nseq // per_step, D_MODEL // width),
        in_specs=[tile] * 5 + [par, par, sspec],
        out_specs=[tile, sspec],
        out_shape=[jax.ShapeDtypeStruct((n, D_MODEL), F32),
                   jax.ShapeDtypeStruct(state.shape, F32)],
        compiler_params=_cparams(2),
        name="wkv_sample",
    )(r, k, v, a, ld, kk_gain, ka_gain, state)


def _rwkv_out_kernel(y_ref, r_ref, k_ref, v_ref, a_ref, g_ref, h_ref, kag_ref, rk_ref, lng_ref,
                     lnb_ref, wout_ref, fg_ref, out_ref, z_ref):
    half_r = _iota((LANES, LANES), 0) // R_HEAD_DIM
    half_c = _iota((LANES, LANES), 1) // R_HEAD_DIM
    ones = (half_r == half_c).astype(BF16)
    inv_n = 1.0 / R_HEAD_DIM
    for j in range(D_MODEL // LANES):
        sl = slice(j * LANES, (j + 1) * LANES)
        y = y_ref[:, sl]
        d = y - _seg_sum(y, ones) * inv_n
        var = _seg_sum(d * d, ones) * inv_n
        yn = d * lax.rsqrt(var + GN_EPS) * lng_ref[:, sl] + lnb_ref[:, sl]
        a = a_ref[:, sl]
        r = r_ref[:, sl]
        kh = k_ref[:, sl] * (1.0 + (a - 1.0) * kag_ref[:, sl])
        bonus = _seg_sum(r * kh * rk_ref[:, sl], ones) * v_ref[:, sl]
        g = g_ref[:, sl]
        z_ref[:, sl] = ((yn + bonus) * (g * _sigmoid(g))).astype(BF16)
    h2 = h_ref[...] + jnp.dot(z_ref[...], wout_ref[...], preferred_element_type=F32)
    out_ref[...] = _rmsnorm(h2, fg_ref[...])


def _rwkv_out_call(y, r, k, v, a, g, h, p, nbatch, ntile, tm, in_tiles_per_batch, skip):
    src = lambda b, i: (b * in_tiles_per_batch + i + skip, 0)
    dst = lambda b, i: (b * ntile + i, 0)
    par = pl.BlockSpec((1, D_MODEL), lambda b, i: (0, 0))
    tile = pl.BlockSpec((tm, D_MODEL), src)
    return pl.pallas_call(
        _rwkv_out_kernel,
        grid=(nbatch, ntile),
        in_specs=[tile] * 7 + [par] * 4 + [pl.BlockSpec((D_MODEL, D_MODEL), lambda b, i: (0, 0)), par],
        out_specs=pl.BlockSpec((tm, D_MODEL), dst),
        out_shape=jax.ShapeDtypeStruct((nbatch * ntile * tm, D_MODEL), F32),
        scratch_shapes=[pltpu.VMEM((tm, D_MODEL), BF16)],
        compiler_params=_cparams(2),
        name="rwkv_out",
    )(y, r, k, v, a, g, h, p["k_a"], p["r_k"], p["ln_g"], p["ln_b"], p["w_out"], p["final_gain"])


def _prompt_bucket():
    qi = np.arange(BLOCK)[:, None]
    kj = np.arange(2 * BLOCK)[None, :]
    rel = BLOCK + qi - kj
    return np.where((rel >= 0) & (rel < WINDOW), _t5_bucket_np(rel), -1).astype(np.int32)


def _sample_bucket(keep, t_new, slot):
    t = (np.arange(SUBLANES) % t_new)[:, None]
    j = np.arange(SAMPLE_KEYS)[None, :]
    own = j - keep - slot * t_new
    rel = np.where(j < keep, keep + t - j, t - own)
    ok = (rel >= 0) & (rel < WINDOW) & ((j < keep) | ((own >= 0) & (own < t_new)))
    return np.where(ok, _t5_bucket_np(rel), -1).astype(np.int32)


def kernel(x_prompt, x_sample, cache_win_k, cache_win_v, state_wkv, state_shift, meta_tokens, rel_bias_table, norm_gain, final_gain, attn_w_in, attn_sinks, attn_w_out, rwkv_mu, rwkv_w_in, rwkv_w0, rwkv_w1, rwkv_w2, rwkv_a0, rwkv_a1, rwkv_a2, rwkv_k_k, rwkv_k_a, rwkv_r_k, rwkv_ln_gamma, rwkv_ln_beta, rwkv_w_out):
    nb, seq, _ = x_prompt.shape
    ns, t_new, _ = x_sample.shape
    keep = cache_win_k.shape[2]
    lp = seq + BLOCK
    nblk = lp // BLOCK
    row = lambda x: x.reshape(1, D_MODEL)

    w_in0 = attn_w_in[0].astype(BF16)
    w_out0 = attn_w_out[0].astype(BF16)
    gain0 = row(norm_gain[0])
    sinks = attn_sinks[0]
    rp = dict(gain=row(norm_gain[1]), mu=rwkv_mu[0], w_in=rwkv_w_in[0].astype(BF16),
              w0=row(rwkv_w0[0]), w1=rwkv_w1[0].astype(BF16), w2=rwkv_w2[0].astype(BF16),
              a0=row(rwkv_a0[0]), a1=rwkv_a1[0].astype(BF16), a2=rwkv_a2[0].astype(BF16),
              k_a=row(rwkv_k_a[0]), r_k=row(rwkv_r_k[0]), ln_g=row(rwkv_ln_gamma[0]),
              ln_b=row(rwkv_ln_beta[0]), w_out=rwkv_w_out[0].astype(BF16),
              final_gain=row(final_gain))
    kk_gain = row(rwkv_k_k[0])

    bias_p = _bias_call(rel_bias_table, _prompt_bucket())
    bias_s = [_bias_call(rel_bias_table, _sample_bucket(keep, t_new, slot)) for slot in range(2)]

    meta = jnp.broadcast_to(meta_tokens[None].astype(F32), (nb, N_META, D_MODEL))
    h0 = jnp.concatenate([jnp.zeros((nb, PAD, D_MODEL), F32), meta, x_prompt], axis=1)
    h0 = h0.reshape(nb * lp, D_MODEL)
    q, kv, g = _attn_proj_call(h0, gain0, w_in0, 256, BF16)
    h1 = _attn_prompt_call(sinks, q, kv, g, h0, bias_p, w_out0, nb, nblk)
    kv3 = kv.reshape(nb, lp, 2 * A_KV_WIDTH)[:, lp - WINDOW:, :]
    win_k_p = kv3[:, :, :A_KV_WIDTH].reshape(1, nb, WINDOW, A_KV_HEADS, A_HEAD_DIM)
    win_v_p = kv3[:, :, A_KV_WIDTH:].reshape(1, nb, WINDOW, A_KV_HEADS, A_HEAD_DIM)

    shift0 = jnp.zeros((nb, 1, D_MODEL), F32)
    r, k, v, g1, ld, a, xlast = _rwkv_proj_call(h1, shift0, rp, nb, nblk, BLOCK, None)
    y, st = _wkv_chunk_call(r, k, v, a, ld, kk_gain, rp["k_a"], nb, lp)
    y_prompt = _rwkv_out_call(y, r, k, v, a, g1, h1, rp, nb, nblk - 1, BLOCK, nblk, 1)
    y_prompt = y_prompt.reshape(nb, seq, D_MODEL)
    st = st.reshape(nb, D_MODEL // LANES, 2, R_HEAD_DIM, 2, R_HEAD_DIM)
    st = jnp.stack([st[:, :, 0, :, 0, :], st[:, :, 1, :, 1, :]], axis=2)
    wkv_p = jnp.swapaxes(st, -1, -2).reshape(1, nb, R_HEADS, R_HEAD_DIM, R_HEAD_DIM)
    shift_p = xlast.reshape(1, nb, D_MODEL)

    xs = x_sample.reshape(ns * t_new, D_MODEL)
    qs, kvs, gs = _attn_proj_call(xs, gain0, w_in0, 256, F32)
    ck = cache_win_k[0].reshape(ns, keep, A_KV_WIDTH)
    cv = cache_win_v[0].reshape(ns, keep, A_KV_WIDTH)
    h1s, nk, nv = _attn_sample_call(sinks, qs, kvs, gs, xs, ck, cv, bias_s[0], bias_s[1], w_out0, t_new)
    win_k_s = nk.reshape(1, ns, keep, A_KV_HEADS, A_HEAD_DIM)
    win_v_s = nv.reshape(1, ns, keep, A_KV_HEADS, A_HEAD_DIM)

    shift_rows = jnp.repeat(state_shift[0], t_new, axis=0)
    tms = 256
    rs, ks, vs, g1s, lds, as_, xns = _rwkv_proj_call(h1s, shift_rows, rp, 1, ns * t_new // tms, tms, t_new)
    ys, st_s = _wkv_sample_call(rs, ks, vs, as_, lds, kk_gain, rp["k_a"], state_wkv[0], t_new)
    y_sample = _rwkv_out_call(ys, rs, ks, vs, as_, g1s, h1s, rp, 1, ns * t_new // tms, tms, 0, 0)
    y_sample = y_sample.reshape(ns, t_new, D_MODEL)
    wkv_s = st_s[None]
    shift_s = xns.reshape(ns, t_new, D_MODEL)[:, t_new - 1][None]

    return (y_prompt, y_sample, win_k_p, win_v_p, wkv_p, shift_p, win_k_s, win_v_s, wkv_s, shift_s)
```

```python
import functools
import math

import numpy as np
import jax
import jax.numpy as jnp
from jax import lax
from jax.experimental import pallas as pl
from jax.experimental.pallas import tpu as pltpu

F32 = jnp.float32
BF16 = jnp.bfloat16

D_MODEL = 1024
N_META = 16
RMS_EPS = 1e-6
A_HEADS = 16
A_KV_HEADS = 4
A_HEAD_DIM = 64
A_WIDTH = A_HEADS * A_HEAD_DIM
A_KV_WIDTH = A_KV_HEADS * A_HEAD_DIM
WINDOW = 128
BLOCK = 128
N_BUCKETS = 32
MAX_DISTANCE = 128
R_HEAD_DIM = 64
R_HEADS = D_MODEL // R_HEAD_DIM
GN_EPS = 64e-5

LANES = 128
SUBLANES = 8
PAD = BLOCK - N_META
NEG = -1e30
WKV_CHUNK = 64
WKV_PAIRS_PER_STEP = 8
WKV_SAMPLE_PAIRS = 4
RWKV_PROJ_ROWS = 384
RWKV_OUT_ROWS = 512
VMEM_LIMIT = 56 * 1024 * 1024


def _cparams(n_axes):
    return pltpu.CompilerParams(dimension_semantics=("arbitrary",) * n_axes,
                                vmem_limit_bytes=VMEM_LIMIT)


def _rmsnorm(x, gain):
    return x * lax.rsqrt(jnp.mean(x * x, axis=-1, keepdims=True) + RMS_EPS) * gain


def _sigmoid(x):
    return 1.0 / (1.0 + jnp.exp(-x))


def _iota(shape, dim):
    return lax.broadcasted_iota(jnp.int32, shape, dim)


def _t5_bucket_np(rel):
    n = np.maximum(rel, 0)
    max_exact = N_BUCKETS // 2
    nf = np.maximum(n, max_exact).astype(np.float32)
    scale = np.float32(math.log(MAX_DISTANCE / max_exact))
    large = max_exact + (np.log(nf / np.float32(max_exact)) / scale
                         * np.float32(N_BUCKETS - max_exact)).astype(np.int32)
    large = np.minimum(large, N_BUCKETS - 1)
    return np.where(n < max_exact, n, large).astype(np.int32)


def _bias_kernel(table_ref, bucket_ref, out_ref):
    h = pl.program_id(0)
    bk = bucket_ref[...]
    acc = jnp.full(bk.shape, NEG, F32)
    for b in range(N_BUCKETS):
        acc = jnp.where(bk == b, table_ref[b, h], acc)
    out_ref[0] = acc


def _bias_call(table, bucket_np):
    r, c = bucket_np.shape
    return pl.pallas_call(
        _bias_kernel,
        grid=(A_HEADS,),
        in_specs=[pl.BlockSpec(memory_space=pltpu.SMEM),
                  pl.BlockSpec((r, c), lambda h: (0, 0))],
        out_specs=pl.BlockSpec((1, r, c), lambda h: (h, 0, 0)),
        out_shape=jax.ShapeDtypeStruct((A_HEADS, r, c), F32),
        compiler_params=_cparams(1),
        name="bias_expand",
    )(table, jnp.asarray(bucket_np))


def _attn_proj_kernel(x_ref, gain_ref, w_ref, q_ref, kv_ref, g_ref):
    xn = _rmsnorm(x_ref[...], gain_ref[...])
    proj = jnp.dot(xn.astype(BF16), w_ref[...], preferred_element_type=F32)
    q_ref[...] = (proj[:, :A_WIDTH] * (A_HEAD_DIM ** -0.5)).astype(q_ref.dtype)
    kv_ref[...] = proj[:, A_WIDTH:A_WIDTH + 2 * A_KV_WIDTH]
    g_ref[...] = proj[:, A_WIDTH + 2 * A_KV_WIDTH:]


def _attn_proj_call(x2d, gain, w_bf16, tm, q_dtype):
    n = x2d.shape[0]
    wcols = w_bf16.shape[1]
    return pl.pallas_call(
        _attn_proj_kernel,
        grid=(n // tm,),
        in_specs=[pl.BlockSpec((tm, D_MODEL), lambda i: (i, 0)),
                  pl.BlockSpec((1, D_MODEL), lambda i: (0, 0)),
                  pl.BlockSpec((D_MODEL, wcols), lambda i: (0, 0))],
        out_specs=[pl.BlockSpec((tm, A_WIDTH), lambda i: (i, 0)),
                   pl.BlockSpec((tm, 2 * A_KV_WIDTH), lambda i: (i, 0)),
                   pl.BlockSpec((tm, A_WIDTH), lambda i: (i, 0))],
        out_shape=[jax.ShapeDtypeStruct((n, A_WIDTH), q_dtype),
                   jax.ShapeDtypeStruct((n, 2 * A_KV_WIDTH), F32),
                   jax.ShapeDtypeStruct((n, A_WIDTH), F32)],
        compiler_params=_cparams(1),
        name="attn_proj",
    )(x2d, gain, w_bf16)


def _padded_kv_tiles(kv, c):
    lo = _iota((1, LANES), 1) < A_HEAD_DIM
    j = c // 2
    out = []
    for base in (0, A_KV_WIDTH):
        t = kv[:, base + j * LANES: base + (j + 1) * LANES]
        tr = pltpu.roll(t, A_HEAD_DIM, 1)
        if c % 2 == 0:
            even, odd = jnp.where(lo, t, 0.0), jnp.where(lo, 0.0, tr)
        else:
            even, odd = jnp.where(lo, tr, 0.0), jnp.where(lo, 0.0, t)
        out += [even.astype(BF16), odd.astype(BF16)]
    return out


def _mm_nt(a, b):
    return lax.dot_general(a, b, (((1,), (1,)), ((), ())), preferred_element_type=F32)


def _attend_pair(q_tile, ke, ko, ve, vo, bias_e, bias_o, sink_e, sink_o, kvalid):
    total = None
    for kk, vv, bias, sink in ((ke, ve, bias_e, sink_e), (ko, vo, bias_o, sink_o)):
        s = _mm_nt(q_tile, kk) + bias
        if kvalid is not None:
            s = jnp.where(kvalid, s, NEG)
        m = jnp.maximum(jnp.max(s, axis=1, keepdims=True), sink)
        p = jnp.exp(s - m)
        den = jnp.sum(p, axis=1, keepdims=True) + jnp.exp(sink - m)
        o = jnp.dot(p.astype(BF16), vv, preferred_element_type=F32) * (1.0 / den)
        total = o if total is None else total + o
    return total


def _attn_prompt_kernel(sinks_ref, q_ref, kvc_ref, kvp_ref, g_ref, h_ref, bias_ref, wout_ref,
                        out_ref, og_ref):
    i = pl.program_id(1)
    kv = jnp.concatenate([kvp_ref[...], kvc_ref[...]], axis=0)
    col = _iota((1, 2 * BLOCK), 1)
    kvalid = (i - 1) * BLOCK + col >= PAD
    for c in range(A_KV_HEADS):
        ke, ko, ve, vo = _padded_kv_tiles(kv, c)
        for mm in (2 * c, 2 * c + 1):
            sl = slice(mm * LANES, (mm + 1) * LANES)
            he, ho = 2 * mm, 2 * mm + 1
            o = _attend_pair(q_ref[:, sl], ke, ko, ve, vo, bias_ref[he], bias_ref[ho],
                             sinks_ref[he], sinks_ref[ho], kvalid)
            gt = g_ref[:, sl]
            og_ref[:, sl] = (o * (gt * _sigmoid(gt))).astype(BF16)
    out_ref[...] = h_ref[...] + jnp.dot(og_ref[...], wout_ref[...], preferred_element_type=F32)


def _attn_prompt_call(sinks, q, kv, g, h, bias, wout_bf16, nbatch, nblk):
    n = q.shape[0]
    row = lambda b, i: (b * nblk + i, 0)
    prev = lambda b, i: (b * nblk + jnp.maximum(i - 1, 0), 0)
    return pl.pallas_call(
        _attn_prompt_kernel,
        grid=(nbatch, nblk),
        in_specs=[pl.BlockSpec(memory_space=pltpu.SMEM),
                  pl.BlockSpec((BLOCK, A_WIDTH), row),
                  pl.BlockSpec((BLOCK, 2 * A_KV_WIDTH), row),
                  pl.BlockSpec((BLOCK, 2 * A_KV_WIDTH), prev),
                  pl.BlockSpec((BLOCK, A_WIDTH), row),
                  pl.BlockSpec((BLOCK, D_MODEL), row),
                  pl.BlockSpec((A_HEADS, BLOCK, 2 * BLOCK), lambda b, i: (0, 0, 0)),
                  pl.BlockSpec((A_WIDTH, D_MODEL), lambda b, i: (0, 0))],
        out_specs=pl.BlockSpec((BLOCK, D_MODEL), row),
        out_shape=jax.ShapeDtypeStruct((n, D_MODEL), F32),
        scratch_shapes=[pltpu.VMEM((BLOCK, A_WIDTH), BF16)],
        compiler_params=_cparams(2),
        name="attn_prompt",
    )(sinks, q, kv, kv, g, h, bias, wout_bf16)


SAMPLE_SB = 8
SAMPLE_KEYS = 2 * BLOCK


def _attn_sample_kernel(sinks_ref, q_ref, kvn_ref, g_ref, h_ref, ck_ref, cv_ref, bias0_ref, bias1_ref,
                        wout_ref, out_ref, nk_ref, nv_ref, og_ref, *, t_new):
    keep = ck_ref.shape[1]
    lo = _iota((1, LANES), 1) < A_HEAD_DIM
    stack = 8 * SUBLANES
    own = (_iota((stack, 1), 0) & (SUBLANES - 1)) // t_new
    piece = _iota((stack, 1), 0) // SUBLANES
    bias_refs = (bias0_ref, bias1_ref)
    n_tile = A_KV_WIDTH // LANES
    pair_ids = range(SAMPLE_SB * t_new // SUBLANES)

    bias_c, bias_n, sink = [], [], []
    for j in range(n_tile):
        heads = slice(8 * j, 8 * j + 8)
        bias_c.append(bias0_ref[heads, :, :keep].reshape(stack, keep))
        bias_n.append([ref[heads, :, keep:keep + SUBLANES].reshape(stack, SUBLANES) for ref in bias_refs])
        col = jnp.zeros((stack, 1), F32)
        for gq in range(8):
            col = jnp.where(piece == gq, sinks_ref[8 * j + gq], col)
        sink.append(col)

    chains = [(p, j, s) for p in pair_ids for j in range(n_tile) for s in range(SUBLANES // t_new)]
    qs = {}
    for p in pair_ids:
        rows = slice(p * SUBLANES, (p + 1) * SUBLANES)
        for j in range(n_tile):
            parts = []
            for gq in range(8):
                t = q_ref[rows, (4 * j + gq // 2) * LANES:(4 * j + gq // 2 + 1) * LANES]
                want_lo = gq < 4
                if (gq % 2 == 0) != want_lo:
                    t = pltpu.roll(t, A_HEAD_DIM, 1)
                parts.append(jnp.where(lo, t, 0.0) if want_lo else jnp.where(lo, 0.0, t))
            qs[p, j] = jnp.concatenate(parts, axis=0).astype(BF16)

    def kv_tiles(p, j, s, base):
        seq = p * (SUBLANES // t_new) + s
        cache = (ck_ref if base == 0 else cv_ref)[seq][:, j * LANES:(j + 1) * LANES].astype(BF16)
        new = kvn_ref[p * SUBLANES:(p + 1) * SUBLANES, base + j * LANES:base + (j + 1) * LANES].astype(BF16)
        return cache, new

    sc, sn = [], []
    for p, j, s in chains:
        kc, kn = kv_tiles(p, j, s, 0)
        sc.append(_mm_nt(qs[p, j], kc) + bias_c[j])
        sn.append(_mm_nt(qs[p, j], kn) + bias_n[j][s])
    mx = [jnp.maximum(jnp.maximum(jnp.max(sc[i], axis=1, keepdims=True),
                                  jnp.max(sn[i], axis=1, keepdims=True)), sink[chains[i][1]])
          for i in range(len(chains))]
    pc = [jnp.exp(sc[i] - mx[i]) for i in range(len(chains))]
    pn = [jnp.exp(sn[i] - mx[i]) for i in range(len(chains))]
    den = [jnp.sum(pc[i], axis=1, keepdims=True) + jnp.sum(pn[i], axis=1, keepdims=True)
           + jnp.exp(sink[chains[i][1]] - mx[i]) for i in range(len(chains))]
    outs = {}
    for i, (p, j, s) in enumerate(chains):
        vc, vn = kv_tiles(p, j, s, A_KV_WIDTH)
        o = (jnp.dot(pc[i].astype(BF16), vc, preferred_element_type=F32)
             + jnp.dot(pn[i].astype(BF16), vn, preferred_element_type=F32)) * (1.0 / den[i])
        outs[p, j] = o if s == 0 else jnp.where(own == s, o, outs[p, j])
    for p in pair_ids:
        rows = slice(p * SUBLANES, (p + 1) * SUBLANES)
        for j in range(n_tile):
            o = outs[p, j]
            for gg in range(4):
                even = o[2 * gg * SUBLANES:(2 * gg + 1) * SUBLANES]
                odd = o[(2 * gg + 1) * SUBLANES:(2 * gg + 2) * SUBLANES]
                if gg < 2:
                    tile = jnp.where(lo, even, pltpu.roll(odd, A_HEAD_DIM, 1))
                else:
                    tile = jnp.where(lo, pltpu.roll(even, A_HEAD_DIM, 1), odd)
                sl = slice((4 * j + gg) * LANES, (4 * j + gg + 1) * LANES)
                gt = g_ref[rows, sl]
                og_ref[rows, sl] = tile * (gt * _sigmoid(gt))
    for seq in range(SAMPLE_SB):
        new = kvn_ref[seq * t_new:(seq + 1) * t_new, :]
        nk_ref[seq, :keep - t_new, :] = ck_ref[seq, t_new:, :]
        nk_ref[seq, keep - t_new:, :] = new[:, :A_KV_WIDTH]
        nv_ref[seq, :keep - t_new, :] = cv_ref[seq, t_new:, :]
        nv_ref[seq, keep - t_new:, :] = new[:, A_KV_WIDTH:]
    out_ref[...] = h_ref[...] + jnp.dot(og_ref[...].astype(BF16), wout_ref[...],
                                        preferred_element_type=F32)


def _attn_sample_call(sinks, q, kv, g, h, cache_k, cache_v, bias0, bias1, wout_bf16, t_new):
    nseq, keep = cache_k.shape[0], cache_k.shape[1]
    rows = SAMPLE_SB * t_new
    row = lambda i: (i, 0)
    cspec = pl.BlockSpec((SAMPLE_SB, keep, A_KV_WIDTH), lambda i: (i, 0, 0))
    bspec = pl.BlockSpec((A_HEADS, SUBLANES, SAMPLE_KEYS), lambda i: (0, 0, 0))
    return pl.pallas_call(
        functools.partial(_attn_sample_kernel, t_new=t_new),
        grid=(nseq // SAMPLE_SB,),
        in_specs=[pl.BlockSpec(memory_space=pltpu.SMEM),
                  pl.BlockSpec((rows, A_WIDTH), row),
                  pl.BlockSpec((rows, 2 * A_KV_WIDTH), row),
                  pl.BlockSpec((rows, A_WIDTH), row),
                  pl.BlockSpec((rows, D_MODEL), row),
                  cspec, cspec, bspec, bspec,
                  pl.BlockSpec((A_WIDTH, D_MODEL), lambda i: (0, 0))],
        out_specs=[pl.BlockSpec((rows, D_MODEL), row), cspec, cspec],
        out_shape=[jax.ShapeDtypeStruct((nseq * t_new, D_MODEL), F32),
                   jax.ShapeDtypeStruct(cache_k.shape, F32),
                   jax.ShapeDtypeStruct(cache_v.shape, F32)],
        scratch_shapes=[pltpu.VMEM((rows, A_WIDTH), F32)],
        compiler_params=_cparams(1),
        name="attn_sample",
    )(sinks, q, kv, g, h, cache_k, cache_v, bias0, bias1, wout_bf16)


def _rwkv_proj_kernel(h_ref, shift_ref, gain_ref, mu_ref, win_ref, w0_ref, w1_ref, w2_ref,
                      a0_ref, a1_ref, a2_ref,
                      r_ref, k_ref, v_ref, g_ref, ld_ref, a_ref, xn_ref, *scratch, seq_len):
    xn = _rmsnorm(h_ref[...], gain_ref[...])
    tm = xn.shape[0]
    rolled = pltpu.roll(xn, 1, 0)
    row = _iota((tm, 1), 0)
    if seq_len is None:
        carry_ref, = scratch

        @pl.when(pl.program_id(1) == 0)
        def _():
            carry_ref[...] = shift_ref[0]

        xprev = jnp.where(row == 0, carry_ref[...], rolled)
        carry_ref[...] = xn[tm - 1:tm, :]
        xn_ref[0] = xn[tm - 1:tm, :]
    else:
        xprev = jnp.where(row % seq_len == 0, shift_ref[...], rolled)
        xn_ref[...] = xn
    dx = xprev - xn

    def mix(c):
        return (xn + dx * mu_ref[c:c + 1, :]).astype(BF16)

    for c, o_ref in enumerate((r_ref, k_ref, v_ref, g_ref)):
        o_ref[...] = jnp.dot(mix(c), win_ref[c], preferred_element_type=F32)
    lw = jnp.tanh(jnp.dot(mix(4), w1_ref[...], preferred_element_type=F32))
    z = w0_ref[...] + jnp.dot(lw.astype(BF16), w2_ref[...], preferred_element_type=F32)
    u = -z
    softplus = jnp.maximum(u, 0.0) + jnp.log(1.0 + jnp.exp(-jnp.abs(u)))
    ld_ref[...] = -jnp.exp(-softplus - 0.5)
    la = jnp.dot(mix(5), a1_ref[...], preferred_element_type=F32)
    a_ref[...] = _sigmoid(a0_ref[...] + jnp.dot(la.astype(BF16), a2_ref[...],
                                                preferred_element_type=F32))


def _rwkv_proj_call(h, shift, p, nbatch, ntile, tm, seq_len):
    n = h.shape[0]
    row = lambda b, i: (b * ntile + i, 0)
    full2 = lambda b, i: (0, 0)
    if seq_len is None:
        shift_spec = pl.BlockSpec((1, 1, D_MODEL), lambda b, i: (b, 0, 0))
        xn_spec = pl.BlockSpec((1, 1, D_MODEL), lambda b, i: (b, 0, 0))
        xn_shape = jax.ShapeDtypeStruct((nbatch, 1, D_MODEL), F32)
        scratch = [pltpu.VMEM((1, D_MODEL), F32)]
    else:
        shift_spec = pl.BlockSpec((tm, D_MODEL), row)
        xn_spec = pl.BlockSpec((tm, D_MODEL), row)
        xn_shape = jax.ShapeDtypeStruct((n, D_MODEL), F32)
        scratch = []
    lora = p["w1"].shape[1]
    big = jax.ShapeDtypeStruct((n, D_MODEL), F32)
    return pl.pallas_call(
        functools.partial(_rwkv_proj_kernel, seq_len=seq_len),
        grid=(nbatch, ntile),
        in_specs=[pl.BlockSpec((tm, D_MODEL), row),
                  shift_spec,
                  pl.BlockSpec((1, D_MODEL), full2),
                  pl.BlockSpec(p["mu"].shape, full2),
                  pl.BlockSpec(p["w_in"].shape, lambda b, i: (0, 0, 0)),
                  pl.BlockSpec((1, D_MODEL), full2),
                  pl.BlockSpec((D_MODEL, lora), full2),
                  pl.BlockSpec((lora, D_MODEL), full2),
                  pl.BlockSpec((1, D_MODEL), full2),
                  pl.BlockSpec((D_MODEL, lora), full2),
                  pl.BlockSpec((lora, D_MODEL), full2)],
        out_specs=[pl.BlockSpec((tm, D_MODEL), row)] * 6 + [xn_spec],
        out_shape=[big] * 6 + [xn_shape],
        scratch_shapes=scratch,
        compiler_params=_cparams(2),
        name="rwkv_proj",
    )(h, shift, p["gain"], p["mu"], p["w_in"], p["w0"], p["w1"], p["w2"], p["a0"], p["a1"], p["a2"])


NN = (((1,), (0,)), ((), ()))
NT = (((1,), (1,)), ((), ()))
TN = (((0,), (0,)), ((), ()))


def _split(x):
    hi = x.astype(BF16)
    return hi, (x - hi.astype(F32)).astype(BF16)


def _mm(a, b, dn):
    return lax.dot_general(a, b, dn, preferred_element_type=F32)


def _mm3s(a, b, dn):
    (ah, al), (bh, bl) = a, b
    return _mm(al, bh, dn) + _mm(ah, bl, dn) + _mm(ah, bh, dn)


def _mm3(a, b, dn):
    return _mm3s(_split(a), _split(b), dn)


def _mm3_top(x_hi, x_lo_top, b, dn):
    bh, bl = b
    c = x_lo_top.shape[0]
    full = _mm(x_hi, bh, dn)
    return full[:c] + _mm(x_lo_top, bh, dn) + _mm(x_hi[:c], bl, dn), full[c:]


def _seg_sum(x, ones_bf16):
    hi, lo = _split(x)
    return _mm(lo, ones_bf16, NN) + _mm(hi, ones_bf16, NN)


def _lane_lo():
    return _iota((1, LANES), 1) < R_HEAD_DIM


def _same_half():
    return (_iota((LANES, LANES), 0) < R_HEAD_DIM) == (_iota((LANES, LANES), 1) < R_HEAD_DIM)


def _ones_blk():
    return _same_half().astype(BF16)


def _bd(z):
    lo = _lane_lo()
    return jnp.concatenate([jnp.where(lo, z, 0.0), jnp.where(lo, 0.0, z)], axis=0)


def _bd_swap(z):
    lo = _lane_lo()
    return jnp.concatenate([jnp.where(lo, 0.0, z), jnp.where(lo, z, 0.0)], axis=0)


def _split_map(z, f):
    hi, lw = _split(z)
    return f(hi), f(lw)


def _wkv_batch_stage(r, k, v, a, ld, kkg, kag, seq_len, fillers=(), early=None):
    c = r[0].shape[0]
    pairs = range(len(r))
    half = R_HEAD_DIM
    lo = _lane_lo()
    t_row, t_col = _iota((c, LANES), 0), _iota((c, LANES), 1) & (half - 1)
    strict, incl = t_row > t_col, t_row >= t_col
    rr, cc = _iota((c, c), 0), _iota((c, c), 1)
    tri = rr >= cc
    span = c
    if seq_len is not None:
        shift = int(math.log2(seq_len))
        same = (t_row >> shift) == (t_col >> shift)
        strict, incl = strict & same, incl & same
        same_seq = (rr >> shift) == (cc >> shift)
        tri = tri & same_seq
        span = seq_len
    tri = tri.astype(BF16)
    ones_blk = _ones_blk()
    eye_pair = (_iota((c, LANES), 0) == (_iota((c, LANES), 1) & (half - 1))).astype(F32)

    kkx = [k[j] * kkg[j] for j in pairs]
    ssq = [_seg_sum(kkx[j] * kkx[j], ones_blk) for j in pairs]
    ld_s = [_split(ld[j]) for j in pairs]
    cs = [_mm(tri, ld_s[j][1], NN) + _mm(tri, ld_s[j][0], NN) for j in pairs]
    if seq_len is None:
        tot = [cs[j][c - 1:c, :] for j in pairs]
    else:
        seg = same_seq.astype(BF16)
        tot = [_mm(seg, ld_s[j][1], NN) + _mm(seg, ld_s[j][0], NN) for j in pairs]
    x, x_hi, at_lo, kh_all, bke, vbd_s, lk_a, lk_r, lb_a, lb_r = ([] for _ in range(10))
    for j in pairs:
        kk = kkx[j] / jnp.maximum(jnp.sqrt(ssq[j]), 1e-12)
        kh = k[j] * (1.0 + (a[j] - 1.0) * kag[j])
        bv = kk * a[j]
        e_neg = jnp.exp(-cs[j])
        e_end = jnp.exp(tot[j] - cs[j])
        at, rt = -kk * jnp.exp(cs[j] - ld[j]), r[j] * jnp.exp(cs[j])
        at_hi, at_l = _split(at)
        x.append(jnp.concatenate([at, rt], axis=0))
        x_hi.append(jnp.concatenate([at_hi, rt.astype(BF16)], axis=0))
        at_lo.append(at_l)
        kh_all.append(kh)
        bt, kt = _split(bv * e_neg), _split(kh * e_neg)
        yn_s = tuple(jnp.concatenate([p, q], axis=0) for p, q in zip(bt, kt))
        ys_s = tuple(jnp.concatenate([q, p], axis=0) for p, q in zip(bt, kt))
        bke.append(jnp.concatenate([bv * e_end, kh * e_end], axis=0))
        vbd_s.append(_split_map(v[j], _bd))
        ga_a, ga_r = _mm3_top(jnp.where(lo, x_hi[j], 0.0), jnp.where(lo, at_l, 0.0), ys_s, NT)
        gb_a, gb_r = _mm3_top(jnp.where(lo, 0.0, x_hi[j]), jnp.where(lo, 0.0, at_l), yn_s, NT)
        lk_a.append(jnp.where(strict, jnp.where(lo, ga_a, gb_a), 0.0))
        lk_r.append(jnp.where(incl, jnp.where(lo, ga_r, gb_r), 0.0))
        lb_a.append(jnp.where(strict, jnp.where(lo, gb_a, ga_a), 0.0))
        lb_r.append(jnp.where(incl, jnp.where(lo, gb_r, ga_r), 0.0))
    fillers = list(fillers)
    if early is not None:
        early.update(x_hi=x_hi, at_lo=at_lo)
    pw = lb_a
    acc = [eye_pair + pw[j] for j in pairs]
    pw = [_mm(pw[j].astype(BF16), _bd(pw[j].astype(BF16)), NN) for j in pairs]
    from_v = []
    for j in pairs:
        lk_hi, lk_lo = _split(lk_a[j])
        from_v.append(_mm3_top(jnp.concatenate([lk_hi, lk_r[j].astype(BF16)], axis=0), lk_lo, vbd_s[j], NN))
    for _ in range(int(math.log2(span)) - 2):
        both = [_mm(jnp.concatenate([pw[j], acc[j]], axis=0).astype(BF16), _bd(pw[j].astype(BF16)), NN)
                for j in pairs]
        if fillers:
            fillers.pop(0)()
        pw = [both[j][:c] for j in pairs]
        acc = [acc[j] + both[j][c:] for j in pairs]
    tinv = [acc[j] + _mm(acc[j].astype(BF16), _bd(pw[j].astype(BF16)), NN) for j in pairs]
    for fill in fillers:
        fill()
    return dict(x=x, x_hi=x_hi, at_lo=at_lo, kh=kh_all, lb_r=lb_r, tinv=tinv, from_v=from_v, bke=bke,
                tot=tot, ones_blk=ones_blk)


def _wkv_finish(stage, from_state_a, from_state_r):
    pairs = range(len(from_state_a))
    u = [_mm3s(_split(stage["tinv"][j]),
               _split_map(from_state_a[j] + stage["from_v"][j][0], _bd_swap), NN) for j in pairs]
    y = [from_state_r[j] + stage["from_v"][j][1]
         + _mm(stage["lb_r"][j].astype(BF16), _bd_swap(u[j].astype(BF16)), NN) for j in pairs]
    return u, y


def _wkv_gate(y, r, kh, v, g, rk, lng, lnb, ones_blk):
    pairs = range(len(y))
    inv_n = 1.0 / R_HEAD_DIM

    def seg(t):
        return _mm(t.astype(BF16), ones_blk, NN)

    rkk = [seg(r[j] * kh[j] * rk[j]) for j in pairs]
    mean = [seg(y[j]) * inv_n for j in pairs]
    d = [y[j] - mean[j] for j in pairs]
    var = [seg(d[j] * d[j]) * inv_n for j in pairs]
    return [((d[j] * lax.rsqrt(var[j] + GN_EPS) * lng[j] + lnb[j] + rkk[j] * v[j])
             * (g[j] * _sigmoid(g[j]))).astype(BF16) for j in pairs]


def _wkv_chunk_kernel(r_ref, k_ref, v_ref, a_ref, ld_ref, g_ref, kkg_ref, kag_ref, rk_ref, lng_ref, lnb_ref,
                      z_ref, sout_ref, st_ref):
    pairs = range(r_ref.shape[1] // LANES)
    tiles = [slice(j * LANES, (j + 1) * LANES) for j in pairs]

    @pl.when(pl.program_id(2) == 0)
    def _():
        st_ref[...] = jnp.zeros_like(st_ref)

    r, k, v, a, ld = ([ref[:, sl] for sl in tiles] for ref in (r_ref, k_ref, v_ref, a_ref, ld_ref))
    st = [st_ref[j] for j in pairs]
    from_state = {}
    stage = {}

    def state_products(group):
        def run():
            for j in group:
                from_state[j] = _mm3_top(stage["x_hi"][j], stage["at_lo"][j], _split(st[j]), NN)
        return run

    n_fill = 4
    groups = [list(pairs)[i::n_fill] for i in range(n_fill)]
    stage.update(_wkv_batch_stage(r, k, v, a, ld, [kkg_ref[:, sl] for sl in tiles],
                                  [kag_ref[:, sl] for sl in tiles], None,
                                  fillers=[state_products(grp) for grp in groups if grp], early=stage))
    same_half = _same_half()
    eye_full = _iota((LANES, LANES), 0) == _iota((LANES, LANES), 1)
    us, ys = _wkv_finish(stage, [from_state[j][0] for j in pairs], [from_state[j][1] for j in pairs])
    upd = [_mm3(stage["bke"][j], jnp.concatenate([us[j], v[j]], axis=0), TN) for j in pairs]
    zs = _wkv_gate(ys, r, stage["kh"], v, *([ref[:, sl] for sl in tiles]
                                          for ref in (g_ref, rk_ref, lng_ref, lnb_ref)), stage["ones_blk"])
    for j in pairs:
        z_ref[:, tiles[j]] = zs[j]
    for j in pairs:
        w_col = jnp.sum(jnp.where(eye_full, jnp.exp(stage["tot"][j]), 0.0), axis=1, keepdims=True)
        st_new = w_col * st[j] + jnp.where(same_half, upd[j], 0.0)
        st_ref[j] = st_new
        sout_ref[0, j] = st_new


def _wkv_chunk_call(r, k, v, a, ld, g, p, nbatch, seq, skip_rows):
    c = WKV_CHUNK
    nchunk = seq // c
    skip = skip_rows // c
    out_chunks = nchunk - skip
    width = WKV_PAIRS_PER_STEP * LANES
    ngroup = D_MODEL // width
    tile = pl.BlockSpec((c, width), lambda b, j, t: (b * nchunk + t, j))
    ztile = pl.BlockSpec((c, width), lambda b, j, t: (b * out_chunks + jnp.maximum(t - skip, 0), j))
    par = pl.BlockSpec((1, width), lambda b, j, t: (0, j))
    return pl.pallas_call(
        _wkv_chunk_kernel,
        grid=(nbatch, ngroup, nchunk),
        in_specs=[tile] * 6 + [par] * 5,
        out_specs=[ztile, pl.BlockSpec((1, WKV_PAIRS_PER_STEP, LANES, LANES), lambda b, j, t: (b, j, 0, 0))],
        out_shape=[jax.ShapeDtypeStruct((nbatch * out_chunks * c, D_MODEL), BF16),
                   jax.ShapeDtypeStruct((nbatch, D_MODEL // LANES, LANES, LANES), F32)],
        scratch_shapes=[pltpu.VMEM((WKV_PAIRS_PER_STEP, LANES, LANES), F32)],
        compiler_params=_cparams(3),
        name="wkv_chunk",
    )(r, k, v, a, ld, g, p["k_k"], p["k_a"], p["r_k"], p["ln_g"], p["ln_b"])


def _wkv_sample_kernel(r_ref, k_ref, v_ref, a_ref, ld_ref, g_ref, kkg_ref, kag_ref, rk_ref, lng_ref, lnb_ref,
                       s_ref, z_ref, sout_ref, *, seq_len):
    c = r_ref.shape[0]
    half = R_HEAD_DIM
    pairs = range(r_ref.shape[1] // LANES)
    tiles = [slice(j * LANES, (j + 1) * LANES) for j in pairs]
    r, k, v, a, ld = ([ref[:, sl] for sl in tiles] for ref in (r_ref, k_ref, v_ref, a_ref, ld_ref))
    stage = _wkv_batch_stage(r, k, v, a, ld, [kkg_ref[:, sl] for sl in tiles],
                             [kag_ref[:, sl] for sl in tiles], seq_len)
    same_half = _same_half()
    per_group = SUBLANES // seq_len
    slab_seq = (_iota((2 * SUBLANES, LANES), 0) & (SUBLANES - 1)) // seq_len
    zeros = jnp.zeros((half, half), F32)

    def slab(z, g):
        return jnp.concatenate([z[g * SUBLANES:(g + 1) * SUBLANES],
                                z[c + g * SUBLANES:c + (g + 1) * SUBLANES]], axis=0)

    states = {}
    from_state = []
    for j in pairs:
        fa, fr = [], []
        for g in range(c // SUBLANES):
            xg = slab(stage["x"][j], g)
            acc = None
            for q in range(per_group):
                s = g * per_group + q
                sa, sb = s_ref[s, 2 * j], s_ref[s, 2 * j + 1]
                st = jnp.concatenate([jnp.concatenate([sa, zeros], axis=1),
                                      jnp.concatenate([zeros, sb], axis=1)], axis=0)
                states[j, s] = st
                part = _mm3(jnp.where(slab_seq == q, xg, 0.0), st, NT)
                acc = part if acc is None else acc + part
            fa.append(acc[:SUBLANES])
            fr.append(acc[SUBLANES:])
        from_state.append((jnp.concatenate(fa, axis=0), jnp.concatenate(fr, axis=0)))
    us, ys = _wkv_finish(stage, [fs[0] for fs in from_state], [fs[1] for fs in from_state])
    zs = _wkv_gate(ys, r, stage["kh"], v, *([ref[:, sl] for sl in tiles]
                                          for ref in (g_ref, rk_ref, lng_ref, lnb_ref)), stage["ones_blk"])
    for j in pairs:
        z_ref[:, tiles[j]] = zs[j]
        uv = jnp.concatenate([us[j], v[j]], axis=0)
        w_end = jnp.exp(stage["tot"][j])
        for g in range(c // SUBLANES):
            uvg, bkg = slab(uv, g), slab(stage["bke"][j], g)
            for q in range(per_group):
                s = g * per_group + q
                upd = _mm3(jnp.where(slab_seq == q, uvg, 0.0), bkg, TN)
                row = g * SUBLANES + q * seq_len
                st_new = states[j, s] * w_end[row:row + 1, :] + jnp.where(same_half, upd, 0.0)
                sout_ref[s, 2 * j] = st_new[:half, :half]
                sout_ref[s, 2 * j + 1] = st_new[half:, half:]


def _wkv_sample_call(r, k, v, a, ld, g, p, state, seq_len):
    n = r.shape[0]
    nseq = state.shape[0]
    c = WKV_CHUNK
    per_step = c // seq_len
    width = WKV_SAMPLE_PAIRS * LANES
    tile = pl.BlockSpec((c, width), lambda i, j: (i, j))
    par = pl.BlockSpec((1, width), lambda i, j: (0, j))
    sspec = pl.BlockSpec((per_step, 2 * WKV_SAMPLE_PAIRS, R_HEAD_DIM, R_HEAD_DIM), lambda i, j: (i, j, 0, 0))
    return pl.pallas_call(
        functools.partial(_wkv_sample_kernel, seq_len=seq_len),
        grid=(nseq // per_step, D_MODEL // width),
        in_specs=[tile] * 6 + [par] * 5 + [sspec],
        out_specs=[tile, sspec],
        out_shape=[jax.ShapeDtypeStruct((n, D_MODEL), BF16),
                   jax.ShapeDtypeStruct(state.shape, F32)],
        compiler_params=_cparams(2),
        name="wkv_sample",
    )(r, k, v, a, ld, g, p["k_k"], p["k_a"], p["r_k"], p["ln_g"], p["ln_b"], state)


RWKV_OUT_PIECES = 4


def _rwkv_out_kernel(z_ref, *refs):
    h_refs, (wout_ref, fg_ref, out_ref) = refs[:RWKV_OUT_PIECES], refs[RWKV_OUT_PIECES:]
    h = jnp.concatenate([ref[...] for ref in h_refs], axis=0)
    h2 = h + jnp.dot(z_ref[...], wout_ref[...], preferred_element_type=F32)
    out_ref[...] = _rmsnorm(h2, fg_ref[...])


def _rwkv_out_call(z, h, p, nbatch, ntile, tm, h_pieces_per_batch, skip):
    piece = tm // RWKV_OUT_PIECES
    dst = lambda b, i: (b * ntile + i, 0)
    hspec = [pl.BlockSpec((piece, D_MODEL),
                          functools.partial(lambda b, i, kk: (b * h_pieces_per_batch + i * RWKV_OUT_PIECES + skip + kk, 0),
                                            kk=kk))
             for kk in range(RWKV_OUT_PIECES)]
    return pl.pallas_call(
        _rwkv_out_kernel,
        grid=(nbatch, ntile),
        in_specs=[pl.BlockSpec((tm, D_MODEL), dst)] + hspec
                 + [pl.BlockSpec((D_MODEL, D_MODEL), lambda b, i: (0, 0)),
                    pl.BlockSpec((1, D_MODEL), lambda b, i: (0, 0))],
        out_specs=pl.BlockSpec((tm, D_MODEL), dst),
        out_shape=jax.ShapeDtypeStruct((nbatch * ntile * tm, D_MODEL), F32),
        compiler_params=_cparams(2),
        name="rwkv_out",
    )(z, *([h] * RWKV_OUT_PIECES), p["w_out"], p["final_gain"])


def _prompt_bucket():
    qi = np.arange(BLOCK)[:, None]
    kj = np.arange(2 * BLOCK)[None, :]
    rel = BLOCK + qi - kj
    return np.where((rel >= 0) & (rel < WINDOW), _t5_bucket_np(rel), -1).astype(np.int32)


def _sample_bucket(keep, t_new, slot):
    t = (np.arange(SUBLANES) % t_new)[:, None]
    j = np.arange(SAMPLE_KEYS)[None, :]
    own = j - keep - slot * t_new
    rel = np.where(j < keep, keep + t - j, t - own)
    ok = (rel >= 0) & (rel < WINDOW) & ((j < keep) | ((own >= 0) & (own < t_new)))
    return np.where(ok, _t5_bucket_np(rel), -1).astype(np.int32)


def kernel(x_prompt, x_sample, cache_win_k, cache_win_v, state_wkv, state_shift, meta_tokens, rel_bias_table, norm_gain, final_gain, attn_w_in, attn_sinks, attn_w_out, rwkv_mu, rwkv_w_in, rwkv_w0, rwkv_w1, rwkv_w2, rwkv_a0, rwkv_a1, rwkv_a2, rwkv_k_k, rwkv_k_a, rwkv_r_k, rwkv_ln_gamma, rwkv_ln_beta, rwkv_w_out):
    nb, seq, _ = x_prompt.shape
    ns, t_new, _ = x_sample.shape
    keep = cache_win_k.shape[2]
    lp = seq + BLOCK
    nblk = lp // BLOCK
    row = lambda x: x.reshape(1, D_MODEL)

    w_in0 = attn_w_in[0].astype(BF16)
    w_out0 = attn_w_out[0].astype(BF16)
    gain0 = row(norm_gain[0])
    sinks = attn_sinks[0]
    rp = dict(gain=row(norm_gain[1]), mu=rwkv_mu[0], w_in=rwkv_w_in[0].astype(BF16),
              w0=row(rwkv_w0[0]), w1=rwkv_w1[0].astype(BF16), w2=rwkv_w2[0].astype(BF16),
              a0=row(rwkv_a0[0]), a1=rwkv_a1[0].astype(BF16), a2=rwkv_a2[0].astype(BF16),
              k_k=row(rwkv_k_k[0]), k_a=row(rwkv_k_a[0]), r_k=row(rwkv_r_k[0]), ln_g=row(rwkv_ln_gamma[0]),
              ln_b=row(rwkv_ln_beta[0]), w_out=rwkv_w_out[0].astype(BF16),
              final_gain=row(final_gain))

    bias_p = _bias_call(rel_bias_table, _prompt_bucket())
    bias_s = [_bias_call(rel_bias_table, _sample_bucket(keep, t_new, slot)) for slot in range(2)]

    meta = jnp.broadcast_to(meta_tokens[None].astype(F32), (nb, N_META, D_MODEL))
    h0 = jnp.concatenate([jnp.zeros((nb, PAD, D_MODEL), F32), meta, x_prompt], axis=1)
    h0 = h0.reshape(nb * lp, D_MODEL)
    q, kv, g = _attn_proj_call(h0, gain0, w_in0, 256, BF16)
    h1 = _attn_prompt_call(sinks, q, kv, g, h0, bias_p, w_out0, nb, nblk)
    kv3 = kv.reshape(nb, lp, 2 * A_KV_WIDTH)[:, lp - WINDOW:, :]
    win_k_p = kv3[:, :, :A_KV_WIDTH].reshape(1, nb, WINDOW, A_KV_HEADS, A_HEAD_DIM)
    win_v_p = kv3[:, :, A_KV_WIDTH:].reshape(1, nb, WINDOW, A_KV_HEADS, A_HEAD_DIM)

    shift0 = jnp.zeros((nb, 1, D_MODEL), F32)
    r, k, v, g1, ld, a, xlast = _rwkv_proj_call(h1, shift0, rp, nb, lp // RWKV_PROJ_ROWS, RWKV_PROJ_ROWS, None)
    z, st = _wkv_chunk_call(r, k, v, a, ld, g1, rp, nb, lp, BLOCK)
    y_prompt = _rwkv_out_call(z, h1, rp, nb, seq // RWKV_OUT_ROWS, RWKV_OUT_ROWS,
                              lp * RWKV_OUT_PIECES // RWKV_OUT_ROWS, BLOCK * RWKV_OUT_PIECES // RWKV_OUT_ROWS)
    y_prompt = y_prompt.reshape(nb, seq, D_MODEL)
    st = st.reshape(nb, D_MODEL // LANES, 2, R_HEAD_DIM, 2, R_HEAD_DIM)
    st = jnp.stack([st[:, :, 0, :, 0, :], st[:, :, 1, :, 1, :]], axis=2)
    wkv_p = jnp.swapaxes(st, -1, -2).reshape(1, nb, R_HEADS, R_HEAD_DIM, R_HEAD_DIM)
    shift_p = xlast.reshape(1, nb, D_MODEL)

    xs = x_sample.reshape(ns * t_new, D_MODEL)
    qs, kvs, gs = _attn_proj_call(xs, gain0, w_in0, 256, F32)
    ck = cache_win_k[0].reshape(ns, keep, A_KV_WIDTH)
    cv = cache_win_v[0].reshape(ns, keep, A_KV_WIDTH)
    h1s, nk, nv = _attn_sample_call(sinks, qs, kvs, gs, xs, ck, cv, bias_s[0], bias_s[1], w_out0, t_new)
    win_k_s = nk.reshape(1, ns, keep, A_KV_HEADS, A_HEAD_DIM)
    win_v_s = nv.reshape(1, ns, keep, A_KV_HEADS, A_HEAD_DIM)

    shift_rows = jnp.repeat(state_shift[0], t_new, axis=0)
    tms = 256
    rs, ks, vs, g1s, lds, as_, xns = _rwkv_proj_call(h1s, shift_rows, rp, 1, ns * t_new // tms, tms, t_new)
    zs, st_s = _wkv_sample_call(rs, ks, vs, as_, lds, g1s, rp, state_wkv[0], t_new)
    y_sample = _rwkv_out_call(zs, h1s, rp, 1, ns * t_new // RWKV_OUT_ROWS, RWKV_OUT_ROWS,
                              ns * t_new * RWKV_OUT_PIECES // RWKV_OUT_ROWS, 0)
    y_sample = y_sample.reshape(ns, t_new, D_MODEL)
    wkv_s = st_s[None]
    shift_s = xns.reshape(ns, t_new, D_MODEL)[:, t_new - 1][None]

    return (y_prompt, y_sample, win_k_p, win_v_p, wkv_p, shift_p, win_k_s, win_v_s, wkv_s, shift_s)
```

```python
import functools
import math

import numpy as np
import jax
import jax.numpy as jnp
from jax import lax
from jax.experimental import pallas as pl
from jax.experimental.pallas import tpu as pltpu

F32 = jnp.float32
BF16 = jnp.bfloat16

D_MODEL = 1024
N_META = 16
RMS_EPS = 1e-6
A_HEADS = 16
A_KV_HEADS = 4
A_HEAD_DIM = 64
A_WIDTH = A_HEADS * A_HEAD_DIM
A_KV_WIDTH = A_KV_HEADS * A_HEAD_DIM
WINDOW = 128
BLOCK = 128
N_BUCKETS = 32
MAX_DISTANCE = 128
R_HEAD_DIM = 64
R_HEADS = D_MODEL // R_HEAD_DIM
GN_EPS = 64e-5

LANES = 128
SUBLANES = 8
PAD = BLOCK - N_META
NEG = -1e30
WKV_CHUNK = 64
WKV_PAIRS_PER_STEP = 8
WKV_SAMPLE_PAIRS = 4
ATTN_PROJ_ROWS = 384
RWKV_PROJ_ROWS = 384
RWKV_OUT_ROWS = 512
VMEM_LIMIT = 56 * 1024 * 1024


def _cparams(n_axes):
    return pltpu.CompilerParams(dimension_semantics=("arbitrary",) * n_axes,
                                vmem_limit_bytes=VMEM_LIMIT)


def _rmsnorm(x, gain):
    return x * lax.rsqrt(jnp.mean(x * x, axis=-1, keepdims=True) + RMS_EPS) * gain


def _sigmoid(x):
    return 1.0 / (1.0 + jnp.exp(-x))


def _iota(shape, dim):
    return lax.broadcasted_iota(jnp.int32, shape, dim)


def _t5_bucket_np(rel):
    n = np.maximum(rel, 0)
    max_exact = N_BUCKETS // 2
    nf = np.maximum(n, max_exact).astype(np.float32)
    scale = np.float32(math.log(MAX_DISTANCE / max_exact))
    large = max_exact + (np.log(nf / np.float32(max_exact)) / scale
                         * np.float32(N_BUCKETS - max_exact)).astype(np.int32)
    large = np.minimum(large, N_BUCKETS - 1)
    return np.where(n < max_exact, n, large).astype(np.int32)


def _bias_kernel(table_ref, bucket_ref, out_ref):
    h = pl.program_id(0)
    bk = bucket_ref[...]
    acc = jnp.full(bk.shape, NEG, F32)
    for b in range(N_BUCKETS):
        acc = jnp.where(bk == b, table_ref[b, h], acc)
    out_ref[0] = acc


def _bias_call(table, bucket_np):
    r, c = bucket_np.shape
    return pl.pallas_call(
        _bias_kernel,
        grid=(A_HEADS,),
        in_specs=[pl.BlockSpec(memory_space=pltpu.SMEM),
                  pl.BlockSpec((r, c), lambda h: (0, 0))],
        out_specs=pl.BlockSpec((1, r, c), lambda h: (h, 0, 0)),
        out_shape=jax.ShapeDtypeStruct((A_HEADS, r, c), F32),
        compiler_params=_cparams(1),
        name="bias_expand",
    )(table, jnp.asarray(bucket_np))


def _attn_proj_kernel(head_ref, *refs, n_piece):
    x_refs, (gain_ref, w_ref, q_ref, kv_ref, g_ref) = refs[:n_piece], refs[n_piece:]
    first = x_refs[0][...]
    if head_ref is not None:
        first = jnp.where(pl.program_id(1) == 0, head_ref[...], first)
    x = jnp.concatenate([first] + [ref[...] for ref in x_refs[1:]], axis=0)
    xn = _rmsnorm(x, gain_ref[...])
    proj = jnp.dot(xn.astype(BF16), w_ref[...], preferred_element_type=F32)
    q_ref[...] = (proj[:, :A_WIDTH] * (A_HEAD_DIM ** -0.5)).astype(q_ref.dtype)
    kv_ref[...] = proj[:, A_WIDTH:A_WIDTH + 2 * A_KV_WIDTH]
    g_ref[...] = proj[:, A_WIDTH + 2 * A_KV_WIDTH:]


def _attn_proj_call(x2d, head, gain, w_bf16, q_dtype, nbatch, ntile, n_piece, piece):
    tm = n_piece * piece
    wcols = w_bf16.shape[1]
    per_seq = x2d.shape[0] // (nbatch * piece)
    lead = 0 if head is None else 1
    dst = lambda b, i: (b * ntile + i, 0)
    xspec = [pl.BlockSpec((piece, D_MODEL),
                          functools.partial(lambda b, i, kk: (b * per_seq + jnp.maximum(i * n_piece + kk - lead, 0), 0),
                                            kk=kk))
             for kk in range(n_piece)]
    kern = functools.partial(_attn_proj_kernel, n_piece=n_piece)
    operands = [x2d] * n_piece + [gain, w_bf16]
    if head is None:
        kern = functools.partial(kern, None)
        head_spec = []
    else:
        head_spec = [pl.BlockSpec((piece, D_MODEL), lambda b, i: (0, 0))]
        operands = [head] + operands
    n = nbatch * ntile * tm
    return pl.pallas_call(
        kern,
        grid=(nbatch, ntile),
        in_specs=head_spec + xspec + [pl.BlockSpec((1, D_MODEL), lambda b, i: (0, 0)),
                                      pl.BlockSpec((D_MODEL, wcols), lambda b, i: (0, 0))],
        out_specs=[pl.BlockSpec((tm, A_WIDTH), dst),
                   pl.BlockSpec((tm, 2 * A_KV_WIDTH), dst),
                   pl.BlockSpec((tm, A_WIDTH), dst)],
        out_shape=[jax.ShapeDtypeStruct((n, A_WIDTH), q_dtype),
                   jax.ShapeDtypeStruct((n, 2 * A_KV_WIDTH), F32),
                   jax.ShapeDtypeStruct((n, A_WIDTH), F32)],
        compiler_params=_cparams(2),
        name="attn_proj",
    )(*operands)


def _padded_kv_tiles(kv, c):
    lo = _iota((1, LANES), 1) < A_HEAD_DIM
    j = c // 2
    out = []
    for base in (0, A_KV_WIDTH):
        t = kv[:, base + j * LANES: base + (j + 1) * LANES]
        tr = pltpu.roll(t, A_HEAD_DIM, 1)
        if c % 2 == 0:
            even, odd = jnp.where(lo, t, 0.0), jnp.where(lo, 0.0, tr)
        else:
            even, odd = jnp.where(lo, tr, 0.0), jnp.where(lo, 0.0, t)
        out += [even.astype(BF16), odd.astype(BF16)]
    return out


def _mm_nt(a, b):
    return lax.dot_general(a, b, (((1,), (1,)), ((), ())), preferred_element_type=F32)


def _attend_pair(q_tile, ke, ko, ve, vo, bias_e, bias_o, sink_e, sink_o, kvalid):
    total = None
    for kk, vv, bias, sink in ((ke, ve, bias_e, sink_e), (ko, vo, bias_o, sink_o)):
        s = _mm_nt(q_tile, kk) + bias
        if kvalid is not None:
            s = jnp.where(kvalid, s, NEG)
        m = jnp.maximum(jnp.max(s, axis=1, keepdims=True), sink)
        p = jnp.exp(s - m)
        den = jnp.sum(p, axis=1, keepdims=True) + jnp.exp(sink - m)
        o = jnp.dot(p.astype(BF16), vv, preferred_element_type=F32) * (1.0 / den)
        total = o if total is None else total + o
    return total


def _attn_prompt_kernel(sinks_ref, q_ref, kvc_ref, kvp_ref, g_ref, head_ref, x_ref, bias_ref, wout_ref,
                        out_ref, og_ref):
    i = pl.program_id(1)
    resid = jnp.where(i == 0, head_ref[...], x_ref[...])
    kv = jnp.concatenate([kvp_ref[...], kvc_ref[...]], axis=0)
    col = _iota((1, 2 * BLOCK), 1)
    kvalid = (i - 1) * BLOCK + col >= PAD
    for c in range(A_KV_HEADS):
        ke, ko, ve, vo = _padded_kv_tiles(kv, c)
        for mm in (2 * c, 2 * c + 1):
            sl = slice(mm * LANES, (mm + 1) * LANES)
            he, ho = 2 * mm, 2 * mm + 1
            o = _attend_pair(q_ref[:, sl], ke, ko, ve, vo, bias_ref[he], bias_ref[ho],
                             sinks_ref[he], sinks_ref[ho], kvalid)
            gt = g_ref[:, sl]
            og_ref[:, sl] = (o * (gt * _sigmoid(gt))).astype(BF16)
    out_ref[...] = resid + jnp.dot(og_ref[...], wout_ref[...], preferred_element_type=F32)


def _attn_prompt_call(sinks, q, kv, g, head, x2d, bias, wout_bf16, nbatch, nblk):
    n = q.shape[0]
    row = lambda b, i: (b * nblk + i, 0)
    prev = lambda b, i: (b * nblk + jnp.maximum(i - 1, 0), 0)
    xrow = lambda b, i: (b * (nblk - 1) + jnp.maximum(i - 1, 0), 0)
    return pl.pallas_call(
        _attn_prompt_kernel,
        grid=(nbatch, nblk),
        in_specs=[pl.BlockSpec(memory_space=pltpu.SMEM),
                  pl.BlockSpec((BLOCK, A_WIDTH), row),
                  pl.BlockSpec((BLOCK, 2 * A_KV_WIDTH), row),
                  pl.BlockSpec((BLOCK, 2 * A_KV_WIDTH), prev),
                  pl.BlockSpec((BLOCK, A_WIDTH), row),
                  pl.BlockSpec((BLOCK, D_MODEL), lambda b, i: (0, 0)),
                  pl.BlockSpec((BLOCK, D_MODEL), xrow),
                  pl.BlockSpec((A_HEADS, BLOCK, 2 * BLOCK), lambda b, i: (0, 0, 0)),
                  pl.BlockSpec((A_WIDTH, D_MODEL), lambda b, i: (0, 0))],
        out_specs=pl.BlockSpec((BLOCK, D_MODEL), row),
        out_shape=jax.ShapeDtypeStruct((n, D_MODEL), F32),
        scratch_shapes=[pltpu.VMEM((BLOCK, A_WIDTH), BF16)],
        compiler_params=_cparams(2),
        name="attn_prompt",
    )(sinks, q, kv, kv, g, head, x2d, bias, wout_bf16)


SAMPLE_SB = 8
SAMPLE_KEYS = 2 * BLOCK


def _attn_sample_kernel(sinks_ref, q_ref, kvn_ref, g_ref, h_ref, ck_ref, cv_ref, bias0_ref, bias1_ref,
                        wout_ref, out_ref, nk_ref, nv_ref, og_ref, *, t_new):
    keep = ck_ref.shape[1]
    lo = _iota((1, LANES), 1) < A_HEAD_DIM
    stack = 8 * SUBLANES
    own = (_iota((stack, 1), 0) & (SUBLANES - 1)) // t_new
    piece = _iota((stack, 1), 0) // SUBLANES
    bias_refs = (bias0_ref, bias1_ref)
    n_tile = A_KV_WIDTH // LANES
    pair_ids = range(SAMPLE_SB * t_new // SUBLANES)

    bias_c, bias_n, sink = [], [], []
    for j in range(n_tile):
        heads = slice(8 * j, 8 * j + 8)
        bias_c.append(bias0_ref[heads, :, :keep].reshape(stack, keep))
        bias_n.append([ref[heads, :, keep:keep + SUBLANES].reshape(stack, SUBLANES) for ref in bias_refs])
        col = jnp.zeros((stack, 1), F32)
        for gq in range(8):
            col = jnp.where(piece == gq, sinks_ref[8 * j + gq], col)
        sink.append(col)

    chains = [(p, j, s) for p in pair_ids for j in range(n_tile) for s in range(SUBLANES // t_new)]
    qs = {}
    for p in pair_ids:
        rows = slice(p * SUBLANES, (p + 1) * SUBLANES)
        for j in range(n_tile):
            parts = []
            for gq in range(8):
                t = q_ref[rows, (4 * j + gq // 2) * LANES:(4 * j + gq // 2 + 1) * LANES]
                want_lo = gq < 4
                if (gq % 2 == 0) != want_lo:
                    t = pltpu.roll(t, A_HEAD_DIM, 1)
                parts.append(jnp.where(lo, t, 0.0) if want_lo else jnp.where(lo, 0.0, t))
            qs[p, j] = jnp.concatenate(parts, axis=0).astype(BF16)

    def kv_tiles(p, j, s, base):
        seq = p * (SUBLANES // t_new) + s
        cache = (ck_ref if base == 0 else cv_ref)[seq][:, j * LANES:(j + 1) * LANES].astype(BF16)
        new = kvn_ref[p * SUBLANES:(p + 1) * SUBLANES, base + j * LANES:base + (j + 1) * LANES].astype(BF16)
        return cache, new

    sc, sn = [], []
    for p, j, s in chains:
        kc, kn = kv_tiles(p, j, s, 0)
        sc.append(_mm_nt(qs[p, j], kc) + bias_c[j])
        sn.append(_mm_nt(qs[p, j], kn) + bias_n[j][s])
    mx = [jnp.maximum(jnp.maximum(jnp.max(sc[i], axis=1, keepdims=True),
                                  jnp.max(sn[i], axis=1, keepdims=True)), sink[chains[i][1]])
          for i in range(len(chains))]
    pc = [jnp.exp(sc[i] - mx[i]) for i in range(len(chains))]
    pn = [jnp.exp(sn[i] - mx[i]) for i in range(len(chains))]
    den = [jnp.sum(pc[i], axis=1, keepdims=True) + jnp.sum(pn[i], axis=1, keepdims=True)
           + jnp.exp(sink[chains[i][1]] - mx[i]) for i in range(len(chains))]
    outs = {}
    for i, (p, j, s) in enumerate(chains):
        vc, vn = kv_tiles(p, j, s, A_KV_WIDTH)
        o = (jnp.dot(pc[i].astype(BF16), vc, preferred_element_type=F32)
             + jnp.dot(pn[i].astype(BF16), vn, preferred_element_type=F32)) * (1.0 / den[i])
        outs[p, j] = o if s == 0 else jnp.where(own == s, o, outs[p, j])
    for p in pair_ids:
        rows = slice(p * SUBLANES, (p + 1) * SUBLANES)
        for j in range(n_tile):
            o = outs[p, j]
            for gg in range(4):
                even = o[2 * gg * SUBLANES:(2 * gg + 1) * SUBLANES]
                odd = o[(2 * gg + 1) * SUBLANES:(2 * gg + 2) * SUBLANES]
                if gg < 2:
                    tile = jnp.where(lo, even, pltpu.roll(odd, A_HEAD_DIM, 1))
                else:
                    tile = jnp.where(lo, pltpu.roll(even, A_HEAD_DIM, 1), odd)
                sl = slice((4 * j + gg) * LANES, (4 * j + gg + 1) * LANES)
                gt = g_ref[rows, sl]
                og_ref[rows, sl] = tile * (gt * _sigmoid(gt))
    for seq in range(SAMPLE_SB):
        new = kvn_ref[seq * t_new:(seq + 1) * t_new, :]
        nk_ref[seq, :keep - t_new, :] = ck_ref[seq, t_new:, :]
        nk_ref[seq, keep - t_new:, :] = new[:, :A_KV_WIDTH]
        nv_ref[seq, :keep - t_new, :] = cv_ref[seq, t_new:, :]
        nv_ref[seq, keep - t_new:, :] = new[:, A_KV_WIDTH:]
    out_ref[...] = h_ref[...] + jnp.dot(og_ref[...].astype(BF16), wout_ref[...],
                                        preferred_element_type=F32)


def _attn_sample_call(sinks, q, kv, g, h, cache_k, cache_v, bias0, bias1, wout_bf16, t_new):
    nseq, keep = cache_k.shape[0], cache_k.shape[1]
    rows = SAMPLE_SB * t_new
    row = lambda i: (i, 0)
    cspec = pl.BlockSpec((SAMPLE_SB, keep, A_KV_WIDTH), lambda i: (i, 0, 0))
    bspec = pl.BlockSpec((A_HEADS, SUBLANES, SAMPLE_KEYS), lambda i: (0, 0, 0))
    return pl.pallas_call(
        functools.partial(_attn_sample_kernel, t_new=t_new),
        grid=(nseq // SAMPLE_SB,),
        in_specs=[pl.BlockSpec(memory_space=pltpu.SMEM),
                  pl.BlockSpec((rows, A_WIDTH), row),
                  pl.BlockSpec((rows, 2 * A_KV_WIDTH), row),
                  pl.BlockSpec((rows, A_WIDTH), row),
                  pl.BlockSpec((rows, D_MODEL), row),
                  cspec, cspec, bspec, bspec,
                  pl.BlockSpec((A_WIDTH, D_MODEL), lambda i: (0, 0))],
        out_specs=[pl.BlockSpec((rows, D_MODEL), row), cspec, cspec],
        out_shape=[jax.ShapeDtypeStruct((nseq * t_new, D_MODEL), F32),
                   jax.ShapeDtypeStruct(cache_k.shape, F32),
                   jax.ShapeDtypeStruct(cache_v.shape, F32)],
        scratch_shapes=[pltpu.VMEM((rows, A_WIDTH), F32)],
        compiler_params=_cparams(1),
        name="attn_sample",
    )(sinks, q, kv, g, h, cache_k, cache_v, bias0, bias1, wout_bf16)


def _rwkv_proj_kernel(h_ref, shift_ref, gain_ref, mu_ref, win_ref, w0_ref, w1_ref, w2_ref,
                      a0_ref, a1_ref, a2_ref,
                      r_ref, k_ref, v_ref, g_ref, ld_ref, a_ref, xn_ref, *scratch, seq_len):
    xn = _rmsnorm(h_ref[...], gain_ref[...])
    tm = xn.shape[0]
    rolled = pltpu.roll(xn, 1, 0)
    row = _iota((tm, 1), 0)
    if seq_len is None:
        carry_ref, = scratch

        @pl.when(pl.program_id(1) == 0)
        def _():
            carry_ref[...] = shift_ref[0]

        xprev = jnp.where(row == 0, carry_ref[...], rolled)
        carry_ref[...] = xn[tm - 1:tm, :]
        xn_ref[0] = xn[tm - 1:tm, :]
    else:
        xprev = jnp.where(row % seq_len == 0, shift_ref[...], rolled)
        xn_ref[...] = xn
    dx = xprev - xn

    def mix(c):
        return (xn + dx * mu_ref[c:c + 1, :]).astype(BF16)

    for c, o_ref in enumerate((r_ref, k_ref, v_ref, g_ref)):
        o_ref[...] = jnp.dot(mix(c), win_ref[c], preferred_element_type=F32)
    lw = jnp.tanh(jnp.dot(mix(4), w1_ref[...], preferred_element_type=F32))
    z = w0_ref[...] + jnp.dot(lw.astype(BF16), w2_ref[...], preferred_element_type=F32)
    u = -z
    softplus = jnp.maximum(u, 0.0) + jnp.log(1.0 + jnp.exp(-jnp.abs(u)))
    ld_ref[...] = -jnp.exp(-softplus - 0.5)
    la = jnp.dot(mix(5), a1_ref[...], preferred_element_type=F32)
    a_ref[...] = _sigmoid(a0_ref[...] + jnp.dot(la.astype(BF16), a2_ref[...],
                                                preferred_element_type=F32))


def _rwkv_proj_call(h, shift, p, nbatch, ntile, tm, seq_len):
    n = h.shape[0]
    row = lambda b, i: (b * ntile + i, 0)
    full2 = lambda b, i: (0, 0)
    if seq_len is None:
        shift_spec = pl.BlockSpec((1, 1, D_MODEL), lambda b, i: (b, 0, 0))
        xn_spec = pl.BlockSpec((1, 1, D_MODEL), lambda b, i: (b, 0, 0))
        xn_shape = jax.ShapeDtypeStruct((nbatch, 1, D_MODEL), F32)
        scratch = [pltpu.VMEM((1, D_MODEL), F32)]
    else:
        shift_spec = pl.BlockSpec((tm, D_MODEL), row)
        xn_spec = pl.BlockSpec((tm, D_MODEL), row)
        xn_shape = jax.ShapeDtypeStruct((n, D_MODEL), F32)
        scratch = []
    lora = p["w1"].shape[1]
    big = jax.ShapeDtypeStruct((n, D_MODEL), F32)
    return pl.pallas_call(
        functools.partial(_rwkv_proj_kernel, seq_len=seq_len),
        grid=(nbatch, ntile),
        in_specs=[pl.BlockSpec((tm, D_MODEL), row),
                  shift_spec,
                  pl.BlockSpec((1, D_MODEL), full2),
                  pl.BlockSpec(p["mu"].shape, full2),
                  pl.BlockSpec(p["w_in"].shape, lambda b, i: (0, 0, 0)),
                  pl.BlockSpec((1, D_MODEL), full2),
                  pl.BlockSpec((D_MODEL, lora), full2),
                  pl.BlockSpec((lora, D_MODEL), full2),
                  pl.BlockSpec((1, D_MODEL), full2),
                  pl.BlockSpec((D_MODEL, lora), full2),
                  pl.BlockSpec((lora, D_MODEL), full2)],
        out_specs=[pl.BlockSpec((tm, D_MODEL), row)] * 6 + [xn_spec],
        out_shape=[big] * 6 + [xn_shape],
        scratch_shapes=scratch,
        compiler_params=_cparams(2),
        name="rwkv_proj",
    )(h, shift, p["gain"], p["mu"], p["w_in"], p["w0"], p["w1"], p["w2"], p["a0"], p["a1"], p["a2"])


NN = (((1,), (0,)), ((), ()))
NT = (((1,), (1,)), ((), ()))
TN = (((0,), (0,)), ((), ()))


def _split(x):
    hi = x.astype(BF16)
    return hi, (x - hi.astype(F32)).astype(BF16)


def _mm(a, b, dn):
    return lax.dot_general(a, b, dn, preferred_element_type=F32)


def _mm3s(a, b, dn):
    (ah, al), (bh, bl) = a, b
    return _mm(al, bh, dn) + _mm(ah, bl, dn) + _mm(ah, bh, dn)


def _mm3(a, b, dn):
    return _mm3s(_split(a), _split(b), dn)


def _mm3_top(x_hi, x_lo_top, b, dn):
    bh, bl = b
    c = x_lo_top.shape[0]
    full = _mm(x_hi, bh, dn)
    return full[:c] + _mm(x_lo_top, bh, dn) + _mm(x_hi[:c], bl, dn), full[c:]


def _seg_sum(x, ones_bf16):
    hi, lo = _split(x)
    return _mm(lo, ones_bf16, NN) + _mm(hi, ones_bf16, NN)


def _lane_lo():
    return _iota((1, LANES), 1) < R_HEAD_DIM


def _same_half():
    return (_iota((LANES, LANES), 0) < R_HEAD_DIM) == (_iota((LANES, LANES), 1) < R_HEAD_DIM)


def _ones_blk():
    return _same_half().astype(BF16)


def _bd(z):
    lo = _lane_lo()
    return jnp.concatenate([jnp.where(lo, z, 0.0), jnp.where(lo, 0.0, z)], axis=0)


def _bd_swap(z):
    lo = _lane_lo()
    return jnp.concatenate([jnp.where(lo, 0.0, z), jnp.where(lo, z, 0.0)], axis=0)


def _split_map(z, f):
    hi, lw = _split(z)
    return f(hi), f(lw)


def _wkv_batch_stage(r, k, v, a, ld, kkg, kag, seq_len, fillers=(), early=None):
    c = r[0].shape[0]
    pairs = range(len(r))
    half = R_HEAD_DIM
    lo = _lane_lo()
    t_row, t_col = _iota((c, LANES), 0), _iota((c, LANES), 1) & (half - 1)
    strict, incl = t_row > t_col, t_row >= t_col
    rr, cc = _iota((c, c), 0), _iota((c, c), 1)
    tri = rr >= cc
    span = c
    if seq_len is not None:
        shift = int(math.log2(seq_len))
        same = (t_row >> shift) == (t_col >> shift)
        strict, incl = strict & same, incl & same
        same_seq = (rr >> shift) == (cc >> shift)
        tri = tri & same_seq
        span = seq_len
    tri = tri.astype(BF16)
    ones_blk = _ones_blk()
    eye_pair = (_iota((c, LANES), 0) == (_iota((c, LANES), 1) & (half - 1))).astype(F32)

    kkx = [k[j] * kkg[j] for j in pairs]
    ssq = [_seg_sum(kkx[j] * kkx[j], ones_blk) for j in pairs]
    ld_s = [_split(ld[j]) for j in pairs]
    cs = [_mm(tri, ld_s[j][1], NN) + _mm(tri, ld_s[j][0], NN) for j in pairs]
    if seq_len is None:
        tot = [cs[j][c - 1:c, :] for j in pairs]
    else:
        seg = same_seq.astype(BF16)
        tot = [_mm(seg, ld_s[j][1], NN) + _mm(seg, ld_s[j][0], NN) for j in pairs]
    x, x_hi, at_lo, kh_all, bke, vbd_s, lk_a, lk_r, lb_a, lb_r = ([] for _ in range(10))
    for j in pairs:
        kk = kkx[j] / jnp.maximum(jnp.sqrt(ssq[j]), 1e-12)
        kh = k[j] * (1.0 + (a[j] - 1.0) * kag[j])
        bv = kk * a[j]
        e_neg = jnp.exp(-cs[j])
        e_end = jnp.exp(tot[j] - cs[j])
        at, rt = -kk * jnp.exp(cs[j] - ld[j]), r[j] * jnp.exp(cs[j])
        at_hi, at_l = _split(at)
        x.append(jnp.concatenate([at, rt], axis=0))
        x_hi.append(jnp.concatenate([at_hi, rt.astype(BF16)], axis=0))
        at_lo.append(at_l)
        kh_all.append(kh)
        bt, kt = _split(bv * e_neg), _split(kh * e_neg)
        yn_s = tuple(jnp.concatenate([p, q], axis=0) for p, q in zip(bt, kt))
        ys_s = tuple(jnp.concatenate([q, p], axis=0) for p, q in zip(bt, kt))
        bke.append(jnp.concatenate([bv * e_end, kh * e_end], axis=0))
        vbd_s.append(_split_map(v[j], _bd))
        ga_a, ga_r = _mm3_top(jnp.where(lo, x_hi[j], 0.0), jnp.where(lo, at_l, 0.0), ys_s, NT)
        gb_a, gb_r = _mm3_top(jnp.where(lo, 0.0, x_hi[j]), jnp.where(lo, 0.0, at_l), yn_s, NT)
        lk_a.append(jnp.where(strict, jnp.where(lo, ga_a, gb_a), 0.0))
        lk_r.append(jnp.where(incl, jnp.where(lo, ga_r, gb_r), 0.0))
        lb_a.append(jnp.where(strict, jnp.where(lo, gb_a, ga_a), 0.0))
        lb_r.append(jnp.where(incl, jnp.where(lo, gb_r, ga_r), 0.0))
    fillers = list(fillers)
    if early is not None:
        early.update(x_hi=x_hi, at_lo=at_lo)
    pw = lb_a
    acc = [eye_pair + pw[j] for j in pairs]
    pw = [_mm(pw[j].astype(BF16), _bd(pw[j].astype(BF16)), NN) for j in pairs]
    from_v = []
    for j in pairs:
        lk_hi, lk_lo = _split(lk_a[j])
        from_v.append(_mm3_top(jnp.concatenate([lk_hi, lk_r[j].astype(BF16)], axis=0), lk_lo, vbd_s[j], NN))
    for _ in range(int(math.log2(span)) - 2):
        both = [_mm(jnp.concatenate([pw[j], acc[j]], axis=0).astype(BF16), _bd(pw[j].astype(BF16)), NN)
                for j in pairs]
        if fillers:
            fillers.pop(0)()
        pw = [both[j][:c] for j in pairs]
        acc = [acc[j] + both[j][c:] for j in pairs]
    tinv = [acc[j] + _mm(acc[j].astype(BF16), _bd(pw[j].astype(BF16)), NN) for j in pairs]
    for fill in fillers:
        fill()
    return dict(x=x, x_hi=x_hi, at_lo=at_lo, kh=kh_all, lb_r=lb_r, tinv=tinv, from_v=from_v, bke=bke,
                tot=tot, ones_blk=ones_blk)


def _wkv_finish(stage, from_state_a, from_state_r):
    pairs = range(len(from_state_a))
    u = [_mm3s(_split(stage["tinv"][j]),
               _split_map(from_state_a[j] + stage["from_v"][j][0], _bd_swap), NN) for j in pairs]
    y = [from_state_r[j] + stage["from_v"][j][1]
         + _mm(stage["lb_r"][j].astype(BF16), _bd_swap(u[j].astype(BF16)), NN) for j in pairs]
    return u, y


def _wkv_gate(y, r, kh, v, g, rk, lng, lnb, ones_blk):
    pairs = range(len(y))
    inv_n = 1.0 / R_HEAD_DIM

    def seg(t):
        return _mm(t.astype(BF16), ones_blk, NN)

    rkk = [seg(r[j] * kh[j] * rk[j]) for j in pairs]
    mean = [seg(y[j]) * inv_n for j in pairs]
    d = [y[j] - mean[j] for j in pairs]
    var = [seg(d[j] * d[j]) * inv_n for j in pairs]
    return [((d[j] * lax.rsqrt(var[j] + GN_EPS) * lng[j] + lnb[j] + rkk[j] * v[j])
             * (g[j] * _sigmoid(g[j]))).astype(BF16) for j in pairs]


def _wkv_chunk_kernel(r_ref, k_ref, v_ref, a_ref, ld_ref, g_ref, kkg_ref, kag_ref, rk_ref, lng_ref, lnb_ref,
                      z_ref, sout_ref, st_ref):
    pairs = range(r_ref.shape[1] // LANES)
    tiles = [slice(j * LANES, (j + 1) * LANES) for j in pairs]

    @pl.when(pl.program_id(2) == 0)
    def _():
        st_ref[...] = jnp.zeros_like(st_ref)

    r, k, v, a, ld = ([ref[:, sl] for sl in tiles] for ref in (r_ref, k_ref, v_ref, a_ref, ld_ref))
    st = [st_ref[j] for j in pairs]
    from_state = {}
    stage = {}

    def state_products(group):
        def run():
            for j in group:
                from_state[j] = _mm3_top(stage["x_hi"][j], stage["at_lo"][j], _split(st[j]), NN)
        return run

    n_fill = 4
    groups = [list(pairs)[i::n_fill] for i in range(n_fill)]
    stage.update(_wkv_batch_stage(r, k, v, a, ld, [kkg_ref[:, sl] for sl in tiles],
                                  [kag_ref[:, sl] for sl in tiles], None,
                                  fillers=[state_products(grp) for grp in groups if grp], early=stage))
    same_half = _same_half()
    eye_full = _iota((LANES, LANES), 0) == _iota((LANES, LANES), 1)
    us, ys = _wkv_finish(stage, [from_state[j][0] for j in pairs], [from_state[j][1] for j in pairs])
    upd = [_mm3(stage["bke"][j], jnp.concatenate([us[j], v[j]], axis=0), TN) for j in pairs]
    zs = _wkv_gate(ys, r, stage["kh"], v, *([ref[:, sl] for sl in tiles]
                                          for ref in (g_ref, rk_ref, lng_ref, lnb_ref)), stage["ones_blk"])
    for j in pairs:
        z_ref[:, tiles[j]] = zs[j]
    for j in pairs:
        w_col = jnp.sum(jnp.where(eye_full, jnp.exp(stage["tot"][j]), 0.0), axis=1, keepdims=True)
        st_new = w_col * st[j] + jnp.where(same_half, upd[j], 0.0)
        st_ref[j] = st_new
        sout_ref[0, j] = st_new


def _wkv_chunk_call(r, k, v, a, ld, g, p, nbatch, seq, skip_rows):
    c = WKV_CHUNK
    nchunk = seq // c
    skip = skip_rows // c
    out_chunks = nchunk - skip
    width = WKV_PAIRS_PER_STEP * LANES
    ngroup = D_MODEL // width
    tile = pl.BlockSpec((c, width), lambda b, j, t: (b * nchunk + t, j))
    ztile = pl.BlockSpec((c, width), lambda b, j, t: (b * out_chunks + jnp.maximum(t - skip, 0), j))
    par = pl.BlockSpec((1, width), lambda b, j, t: (0, j))
    return pl.pallas_call(
        _wkv_chunk_kernel,
        grid=(nbatch, ngroup, nchunk),
        in_specs=[tile] * 6 + [par] * 5,
        out_specs=[ztile, pl.BlockSpec((1, WKV_PAIRS_PER_STEP, LANES, LANES), lambda b, j, t: (b, j, 0, 0))],
        out_shape=[jax.ShapeDtypeStruct((nbatch * out_chunks * c, D_MODEL), BF16),
                   jax.ShapeDtypeStruct((nbatch, D_MODEL // LANES, LANES, LANES), F32)],
        scratch_shapes=[pltpu.VMEM((WKV_PAIRS_PER_STEP, LANES, LANES), F32)],
        compiler_params=_cparams(3),
        name="wkv_chunk",
    )(r, k, v, a, ld, g, p["k_k"], p["k_a"], p["r_k"], p["ln_g"], p["ln_b"])


def _wkv_sample_kernel(r_ref, k_ref, v_ref, a_ref, ld_ref, g_ref, kkg_ref, kag_ref, rk_ref, lng_ref, lnb_ref,
                       s_ref, z_ref, sout_ref, *, seq_len):
    c = r_ref.shape[0]
    half = R_HEAD_DIM
    pairs = range(r_ref.shape[1] // LANES)
    tiles = [slice(j * LANES, (j + 1) * LANES) for j in pairs]
    r, k, v, a, ld = ([ref[:, sl] for sl in tiles] for ref in (r_ref, k_ref, v_ref, a_ref, ld_ref))
    stage = _wkv_batch_stage(r, k, v, a, ld, [kkg_ref[:, sl] for sl in tiles],
                             [kag_ref[:, sl] for sl in tiles], seq_len)
    same_half = _same_half()
    per_group = SUBLANES // seq_len
    slab_seq = (_iota((2 * SUBLANES, LANES), 0) & (SUBLANES - 1)) // seq_len
    zeros = jnp.zeros((half, half), F32)

    def slab(z, g):
        return jnp.concatenate([z[g * SUBLANES:(g + 1) * SUBLANES],
                                z[c + g * SUBLANES:c + (g + 1) * SUBLANES]], axis=0)

    states = {}
    from_state = []
    for j in pairs:
        fa, fr = [], []
        for g in range(c // SUBLANES):
            xg = slab(stage["x"][j], g)
            acc = None
            for q in range(per_group):
                s = g * per_group + q
                sa, sb = s_ref[s, 2 * j], s_ref[s, 2 * j + 1]
                st = jnp.concatenate([jnp.concatenate([sa, zeros], axis=1),
                                      jnp.concatenate([zeros, sb], axis=1)], axis=0)
                states[j, s] = st
                part = _mm3(jnp.where(slab_seq == q, xg, 0.0), st, NT)
                acc = part if acc is None else acc + part
            fa.append(acc[:SUBLANES])
            fr.append(acc[SUBLANES:])
        from_state.append((jnp.concatenate(fa, axis=0), jnp.concatenate(fr, axis=0)))
    us, ys = _wkv_finish(stage, [fs[0] for fs in from_state], [fs[1] for fs in from_state])
    zs = _wkv_gate(ys, r, stage["kh"], v, *([ref[:, sl] for sl in tiles]
                                          for ref in (g_ref, rk_ref, lng_ref, lnb_ref)), stage["ones_blk"])
    for j in pairs:
        z_ref[:, tiles[j]] = zs[j]
        uv = jnp.concatenate([us[j], v[j]], axis=0)
        w_end = jnp.exp(stage["tot"][j])
        for g in range(c // SUBLANES):
            uvg, bkg = slab(uv, g), slab(stage["bke"][j], g)
            for q in range(per_group):
                s = g * per_group + q
                upd = _mm3(jnp.where(slab_seq == q, uvg, 0.0), bkg, TN)
                row = g * SUBLANES + q * seq_len
                st_new = states[j, s] * w_end[row:row + 1, :] + jnp.where(same_half, upd, 0.0)
                sout_ref[s, 2 * j] = st_new[:half, :half]
                sout_ref[s, 2 * j + 1] = st_new[half:, half:]


def _wkv_sample_call(r, k, v, a, ld, g, p, state, seq_len):
    n = r.shape[0]
    nseq = state.shape[0]
    c = WKV_CHUNK
    per_step = c // seq_len
    width = WKV_SAMPLE_PAIRS * LANES
    tile = pl.BlockSpec((c, width), lambda i, j: (i, j))
    par = pl.BlockSpec((1, width), lambda i, j: (0, j))
    sspec = pl.BlockSpec((per_step, 2 * WKV_SAMPLE_PAIRS, R_HEAD_DIM, R_HEAD_DIM), lambda i, j: (i, j, 0, 0))
    return pl.pallas_call(
        functools.partial(_wkv_sample_kernel, seq_len=seq_len),
        grid=(nseq // per_step, D_MODEL // width),
        in_specs=[tile] * 6 + [par] * 5 + [sspec],
        out_specs=[tile, sspec],
        out_shape=[jax.ShapeDtypeStruct((n, D_MODEL), BF16),
                   jax.ShapeDtypeStruct(state.shape, F32)],
        compiler_params=_cparams(2),
        name="wkv_sample",
    )(r, k, v, a, ld, g, p["k_k"], p["k_a"], p["r_k"], p["ln_g"], p["ln_b"], state)


WKV_LANES_UNROLL = 4


def _wkv_lanes_kernel(r_ref, k_ref, v_ref, a_ref, ld_ref, g_ref, kkg_ref, kag_ref, rk_ref, lng_ref, lnb_ref,
                      s_ref, z_ref, sout_ref, prep_ref, y_ref, *, seq_len):
    n = R_HEAD_DIM
    nseq = s_ref.shape[3]
    eye = _iota((LANES, LANES), 0) == _iota((LANES, LANES), 1)

    def column(ref):
        return jnp.sum(jnp.where(eye, ref[...], 0.0), axis=1, keepdims=True)

    kkg, kag, rk, lng, lnb = (column(ref) for ref in (kkg_ref, kag_ref, rk_ref, lng_ref, lnb_ref))

    def token(ref, t):
        return ref[pl.ds(t, nseq, stride=seq_len), :].T

    bonus = []
    for t in range(seq_len):
        r, k, v, a = (token(ref, t) for ref in (r_ref, k_ref, v_ref, a_ref))
        w = jnp.exp(token(ld_ref, t))
        kkx = k * kkg
        kh = k * (1.0 + (a - 1.0) * kag)
        rkk = r * kh * rk
        tiles = []
        for hh in range(2):
            rows = slice(hh * n, (hh + 1) * n)
            nrm = jnp.sqrt(jnp.sum(kkx[rows] * kkx[rows], axis=0, keepdims=True))
            kk = kkx[rows] / jnp.maximum(nrm, 1e-12)
            for q, val in enumerate((w[rows], -kk, kk * a[rows], kh[rows], r[rows], v[rows])):
                prep_ref[q, t, hh] = val
            tiles.append(jnp.sum(rkk[rows], axis=0, keepdims=True) * v[rows])
        bonus.append(tiles)

    for hh in range(2):
        def advance(i, carry, hh=hh):
            for u in range(WKV_LANES_UNROLL):
                vi = i * WKV_LANES_UNROLL + u
                slab = s_ref[hh, vi]
                for t in range(seq_len):
                    w, av, bv, kh, r = (prep_ref[q, t, hh] for q in range(5))
                    vrow = prep_ref[5, t, hh, pl.ds(vi, 1), :]
                    sa = jnp.sum(slab * av, axis=0, keepdims=True)
                    slab = slab * w + sa * bv + vrow * kh
                    y_ref[t, hh, pl.ds(vi, 1), :] = jnp.sum(slab * r, axis=0, keepdims=True)
                sout_ref[hh, vi] = slab
            return carry

        lax.fori_loop(0, n // WKV_LANES_UNROLL, advance, 0)

    for t in range(seq_len):
        parts = []
        for hh in range(2):
            rows = slice(hh * n, (hh + 1) * n)
            y = y_ref[t, hh]
            d = y - jnp.mean(y, axis=0, keepdims=True)
            var = jnp.mean(d * d, axis=0, keepdims=True)
            parts.append(d * lax.rsqrt(var + GN_EPS) * lng[rows] + lnb[rows] + bonus[t][hh])
        g = token(g_ref, t)
        z = jnp.concatenate(parts, axis=0) * (g * _sigmoid(g))
        z_ref[pl.ds(t, nseq, stride=seq_len), :] = z.T


def _wkv_lanes_call(r, k, v, a, ld, g, p, state_hvkb, seq_len):
    n = r.shape[0]
    nseq = state_hvkb.shape[3]
    tile = pl.BlockSpec((n, LANES), lambda j: (0, j))
    par = pl.BlockSpec((1, LANES), lambda j: (0, j))
    sspec = pl.BlockSpec((2, R_HEAD_DIM, R_HEAD_DIM, nseq), lambda j: (j, 0, 0, 0))
    return pl.pallas_call(
        functools.partial(_wkv_lanes_kernel, seq_len=seq_len),
        grid=(D_MODEL // LANES,),
        in_specs=[tile] * 6 + [par] * 5 + [sspec],
        out_specs=[tile, sspec],
        out_shape=[jax.ShapeDtypeStruct((n, D_MODEL), F32),
                   jax.ShapeDtypeStruct(state_hvkb.shape, F32)],
        scratch_shapes=[pltpu.VMEM((6, seq_len, 2, R_HEAD_DIM, nseq), F32),
                        pltpu.VMEM((seq_len, 2, R_HEAD_DIM, nseq), F32)],
        compiler_params=_cparams(1),
        name="wkv_lanes",
    )(r, k, v, a, ld, g, p["k_k"], p["k_a"], p["r_k"], p["ln_g"], p["ln_b"], state_hvkb)


RWKV_OUT_PIECES = 4


def _rwkv_out_kernel(z_ref, *refs):
    h_refs, (wout_ref, fg_ref, out_ref) = refs[:RWKV_OUT_PIECES], refs[RWKV_OUT_PIECES:]
    h = jnp.concatenate([ref[...] for ref in h_refs], axis=0)
    h2 = h + jnp.dot(z_ref[...].astype(BF16), wout_ref[...], preferred_element_type=F32)
    out_ref[...] = _rmsnorm(h2, fg_ref[...])


def _rwkv_out_call(z, h, p, nbatch, ntile, tm, h_pieces_per_batch, skip):
    piece = tm // RWKV_OUT_PIECES
    dst = lambda b, i: (b * ntile + i, 0)
    hspec = [pl.BlockSpec((piece, D_MODEL),
                          functools.partial(lambda b, i, kk: (b * h_pieces_per_batch + i * RWKV_OUT_PIECES + skip + kk, 0),
                                            kk=kk))
             for kk in range(RWKV_OUT_PIECES)]
    return pl.pallas_call(
        _rwkv_out_kernel,
        grid=(nbatch, ntile),
        in_specs=[pl.BlockSpec((tm, D_MODEL), dst)] + hspec
                 + [pl.BlockSpec((D_MODEL, D_MODEL), lambda b, i: (0, 0)),
                    pl.BlockSpec((1, D_MODEL), lambda b, i: (0, 0))],
        out_specs=pl.BlockSpec((tm, D_MODEL), dst),
        out_shape=jax.ShapeDtypeStruct((nbatch * ntile * tm, D_MODEL), F32),
        compiler_params=_cparams(2),
        name="rwkv_out",
    )(z, *([h] * RWKV_OUT_PIECES), p["w_out"], p["final_gain"])


def _prompt_bucket():
    qi = np.arange(BLOCK)[:, None]
    kj = np.arange(2 * BLOCK)[None, :]
    rel = BLOCK + qi - kj
    return np.where((rel >= 0) & (rel < WINDOW), _t5_bucket_np(rel), -1).astype(np.int32)


def _sample_bucket(keep, t_new, slot):
    t = (np.arange(SUBLANES) % t_new)[:, None]
    j = np.arange(SAMPLE_KEYS)[None, :]
    own = j - keep - slot * t_new
    rel = np.where(j < keep, keep + t - j, t - own)
    ok = (rel >= 0) & (rel < WINDOW) & ((j < keep) | ((own >= 0) & (own < t_new)))
    return np.where(ok, _t5_bucket_np(rel), -1).astype(np.int32)


def kernel(x_prompt, x_sample, cache_win_k, cache_win_v, state_wkv, state_shift, meta_tokens, rel_bias_table, norm_gain, final_gain, attn_w_in, attn_sinks, attn_w_out, rwkv_mu, rwkv_w_in, rwkv_w0, rwkv_w1, rwkv_w2, rwkv_a0, rwkv_a1, rwkv_a2, rwkv_k_k, rwkv_k_a, rwkv_r_k, rwkv_ln_gamma, rwkv_ln_beta, rwkv_w_out):
    nb, seq, _ = x_prompt.shape
    ns, t_new, _ = x_sample.shape
    keep = cache_win_k.shape[2]
    lp = seq + BLOCK
    nblk = lp // BLOCK
    row = lambda x: x.reshape(1, D_MODEL)

    w_in0 = attn_w_in[0].astype(BF16)
    w_out0 = attn_w_out[0].astype(BF16)
    gain0 = row(norm_gain[0])
    sinks = attn_sinks[0]
    rp = dict(gain=row(norm_gain[1]), mu=rwkv_mu[0], w_in=rwkv_w_in[0].astype(BF16),
              w0=row(rwkv_w0[0]), w1=rwkv_w1[0].astype(BF16), w2=rwkv_w2[0].astype(BF16),
              a0=row(rwkv_a0[0]), a1=rwkv_a1[0].astype(BF16), a2=rwkv_a2[0].astype(BF16),
              k_k=row(rwkv_k_k[0]), k_a=row(rwkv_k_a[0]), r_k=row(rwkv_r_k[0]), ln_g=row(rwkv_ln_gamma[0]),
              ln_b=row(rwkv_ln_beta[0]), w_out=rwkv_w_out[0].astype(BF16),
              final_gain=row(final_gain))

    bias_p = _bias_call(rel_bias_table, _prompt_bucket())
    bias_s = [_bias_call(rel_bias_table, _sample_bucket(keep, t_new, slot)) for slot in range(2)]

    head = jnp.concatenate([jnp.zeros((PAD, D_MODEL), F32), meta_tokens.astype(F32)], axis=0)
    xp = x_prompt.reshape(nb * seq, D_MODEL)
    q, kv, g = _attn_proj_call(xp, head, gain0, w_in0, BF16, nb, lp // ATTN_PROJ_ROWS,
                               ATTN_PROJ_ROWS // BLOCK, BLOCK)
    h1 = _attn_prompt_call(sinks, q, kv, g, head, xp, bias_p, w_out0, nb, nblk)
    kv3 = kv.reshape(nb, lp, 2 * A_KV_WIDTH)[:, lp - WINDOW:, :]
    win_k_p = kv3[:, :, :A_KV_WIDTH].reshape(1, nb, WINDOW, A_KV_HEADS, A_HEAD_DIM)
    win_v_p = kv3[:, :, A_KV_WIDTH:].reshape(1, nb, WINDOW, A_KV_HEADS, A_HEAD_DIM)

    shift0 = jnp.zeros((nb, 1, D_MODEL), F32)
    r, k, v, g1, ld, a, xlast = _rwkv_proj_call(h1, shift0, rp, nb, lp // RWKV_PROJ_ROWS, RWKV_PROJ_ROWS, None)
    z, st = _wkv_chunk_call(r, k, v, a, ld, g1, rp, nb, lp, BLOCK)
    y_prompt = _rwkv_out_call(z, h1, rp, nb, seq // RWKV_OUT_ROWS, RWKV_OUT_ROWS,
                              lp * RWKV_OUT_PIECES // RWKV_OUT_ROWS, BLOCK * RWKV_OUT_PIECES // RWKV_OUT_ROWS)
    y_prompt = y_prompt.reshape(nb, seq, D_MODEL)
    st = st.reshape(nb, D_MODEL // LANES, 2, R_HEAD_DIM, 2, R_HEAD_DIM)
    st = jnp.stack([st[:, :, 0, :, 0, :], st[:, :, 1, :, 1, :]], axis=2)
    wkv_p = jnp.swapaxes(st, -1, -2).reshape(1, nb, R_HEADS, R_HEAD_DIM, R_HEAD_DIM)
    shift_p = xlast.reshape(1, nb, D_MODEL)

    xs = x_sample.reshape(ns * t_new, D_MODEL)
    qs, kvs, gs = _attn_proj_call(xs, None, gain0, w_in0, F32, 1, 1, 1, ns * t_new)
    ck = cache_win_k[0].reshape(ns, keep, A_KV_WIDTH)
    cv = cache_win_v[0].reshape(ns, keep, A_KV_WIDTH)
    h1s, nk, nv = _attn_sample_call(sinks, qs, kvs, gs, xs, ck, cv, bias_s[0], bias_s[1], w_out0, t_new)
    win_k_s = nk.reshape(1, ns, keep, A_KV_HEADS, A_HEAD_DIM)
    win_v_s = nv.reshape(1, ns, keep, A_KV_HEADS, A_HEAD_DIM)

    shift_rows = jnp.repeat(state_shift[0], t_new, axis=0)
    tms = 256
    rs, ks, vs, g1s, lds, as_, xns = _rwkv_proj_call(h1s, shift_rows, rp, 1, ns * t_new // tms, tms, t_new)
    zs, st_s = _wkv_lanes_call(rs, ks, vs, as_, lds, g1s, rp, jnp.transpose(state_wkv[0], (1, 2, 3, 0)), t_new)
    y_sample = _rwkv_out_call(zs, h1s, rp, 1, ns * t_new // RWKV_OUT_ROWS, RWKV_OUT_ROWS,
                              ns * t_new * RWKV_OUT_PIECES // RWKV_OUT_ROWS, 0)
    y_sample = y_sample.reshape(ns, t_new, D_MODEL)
    wkv_s = jnp.transpose(st_s, (3, 0, 1, 2))[None]
    shift_s = xns.reshape(ns, t_new, D_MODEL)[:, t_new - 1][None]

    return (y_prompt, y_sample, win_k_p, win_v_p, wkv_p, shift_p, win_k_s, win_v_s, wkv_s, shift_s)
```

```python
import functools
import math

import numpy as np
import jax
import jax.numpy as jnp
from jax import lax
from jax.experimental import pallas as pl
from jax.experimental.pallas import tpu as pltpu

F32 = jnp.float32
BF16 = jnp.bfloat16

D_MODEL = 1024
N_META = 16
RMS_EPS = 1e-6
A_HEADS = 16
A_KV_HEADS = 4
A_HEAD_DIM = 64
A_WIDTH = A_HEADS * A_HEAD_DIM
A_KV_WIDTH = A_KV_HEADS * A_HEAD_DIM
WINDOW = 128
BLOCK = 128
N_BUCKETS = 32
MAX_DISTANCE = 128
R_HEAD_DIM = 64
R_HEADS = D_MODEL // R_HEAD_DIM
GN_EPS = 64e-5

LANES = 128
SUBLANES = 8
PAD = BLOCK - N_META
NEG = -1e30
WKV_CHUNK = 64
WKV_PAIRS_PER_STEP = 8
ATTN_PROJ_ROWS = 384
RWKV_PROJ_ROWS = 384
RWKV_OUT_ROWS = 512
VMEM_LIMIT = 56 * 1024 * 1024


def _cparams(n_axes):
    return pltpu.CompilerParams(dimension_semantics=("arbitrary",) * n_axes,
                                vmem_limit_bytes=VMEM_LIMIT)


def _rmsnorm(x, gain):
    return x * lax.rsqrt(jnp.mean(x * x, axis=-1, keepdims=True) + RMS_EPS) * gain


def _sigmoid(x):
    return 1.0 / (1.0 + jnp.exp(-x))


def _iota(shape, dim):
    return lax.broadcasted_iota(jnp.int32, shape, dim)


def _t5_bucket_np(rel):
    n = np.maximum(rel, 0)
    max_exact = N_BUCKETS // 2
    nf = np.maximum(n, max_exact).astype(np.float32)
    scale = np.float32(math.log(MAX_DISTANCE / max_exact))
    large = max_exact + (np.log(nf / np.float32(max_exact)) / scale
                         * np.float32(N_BUCKETS - max_exact)).astype(np.int32)
    large = np.minimum(large, N_BUCKETS - 1)
    return np.where(n < max_exact, n, large).astype(np.int32)


def _bias_kernel(table_ref, *refs):
    h = pl.program_id(0)
    n = len(refs) // 2
    for bucket_ref, out_ref in zip(refs[:n], refs[n:]):
        bk = bucket_ref[...]
        acc = jnp.full(bk.shape, NEG, F32)
        for b in range(N_BUCKETS):
            acc = jnp.where(bk == b, table_ref[b, h], acc)
        out_ref[0] = acc


def _bias_call(table, buckets_np):
    return pl.pallas_call(
        _bias_kernel,
        grid=(A_HEADS,),
        in_specs=[pl.BlockSpec(memory_space=pltpu.SMEM)]
                 + [pl.BlockSpec(bk.shape, lambda h: (0, 0)) for bk in buckets_np],
        out_specs=[pl.BlockSpec((1,) + bk.shape, lambda h: (h, 0, 0)) for bk in buckets_np],
        out_shape=[jax.ShapeDtypeStruct((A_HEADS,) + bk.shape, F32) for bk in buckets_np],
        compiler_params=_cparams(1),
        name="bias_expand",
    )(table, *(jnp.asarray(bk) for bk in buckets_np))


def _attn_proj_kernel(head_ref, *refs, n_piece):
    x_refs, (gain_ref, w_ref, q_ref, kv_ref, g_ref) = refs[:n_piece], refs[n_piece:]
    first = x_refs[0][...]
    if head_ref is not None:
        first = jnp.where(pl.program_id(1) == 0, head_ref[...], first)
    x = jnp.concatenate([first] + [ref[...] for ref in x_refs[1:]], axis=0)
    xn = _rmsnorm(x, gain_ref[...])
    proj = jnp.dot(xn.astype(BF16), w_ref[...], preferred_element_type=F32)
    q_ref[...] = (proj[:, :A_WIDTH] * (A_HEAD_DIM ** -0.5)).astype(q_ref.dtype)
    kv_ref[...] = proj[:, A_WIDTH:A_WIDTH + 2 * A_KV_WIDTH]
    g_ref[...] = proj[:, A_WIDTH + 2 * A_KV_WIDTH:]


def _attn_proj_call(x2d, head, gain, w_bf16, q_dtype, nbatch, ntile, n_piece, piece):
    tm = n_piece * piece
    wcols = w_bf16.shape[1]
    per_seq = x2d.shape[0] // (nbatch * piece)
    lead = 0 if head is None else 1
    dst = lambda b, i: (b * ntile + i, 0)
    xspec = [pl.BlockSpec((piece, D_MODEL),
                          functools.partial(lambda b, i, kk: (b * per_seq + jnp.maximum(i * n_piece + kk - lead, 0), 0),
                                            kk=kk))
             for kk in range(n_piece)]
    kern = functools.partial(_attn_proj_kernel, n_piece=n_piece)
    operands = [x2d] * n_piece + [gain, w_bf16]
    if head is None:
        kern = functools.partial(kern, None)
        head_spec = []
    else:
        head_spec = [pl.BlockSpec((piece, D_MODEL), lambda b, i: (0, 0))]
        operands = [head] + operands
    n = nbatch * ntile * tm
    return pl.pallas_call(
        kern,
        grid=(nbatch, ntile),
        in_specs=head_spec + xspec + [pl.BlockSpec((1, D_MODEL), lambda b, i: (0, 0)),
                                      pl.BlockSpec((D_MODEL, wcols), lambda b, i: (0, 0))],
        out_specs=[pl.BlockSpec((tm, A_WIDTH), dst),
                   pl.BlockSpec((tm, 2 * A_KV_WIDTH), dst),
                   pl.BlockSpec((tm, A_WIDTH), dst)],
        out_shape=[jax.ShapeDtypeStruct((n, A_WIDTH), q_dtype),
                   jax.ShapeDtypeStruct((n, 2 * A_KV_WIDTH), F32),
                   jax.ShapeDtypeStruct((n, A_WIDTH), F32)],
        compiler_params=_cparams(2),
        name="attn_proj",
    )(*operands)


def _padded_kv_tiles(kv, c):
    lo = _iota((1, LANES), 1) < A_HEAD_DIM
    j = c // 2
    out = []
    for base in (0, A_KV_WIDTH):
        t = kv[:, base + j * LANES: base + (j + 1) * LANES]
        tr = pltpu.roll(t, A_HEAD_DIM, 1)
        if c % 2 == 0:
            even, odd = jnp.where(lo, t, 0.0), jnp.where(lo, 0.0, tr)
        else:
            even, odd = jnp.where(lo, tr, 0.0), jnp.where(lo, 0.0, t)
        out += [even.astype(BF16), odd.astype(BF16)]
    return out


def _mm_nt(a, b):
    return lax.dot_general(a, b, (((1,), (1,)), ((), ())), preferred_element_type=F32)


def _attn_prompt_kernel(sinks_ref, q_ref, kvc_ref, kvp_ref, g_ref, head_ref, x_ref, bias_ref, wout_ref,
                        out_ref, og_ref):
    i = pl.program_id(1)
    resid = jnp.where(i == 0, head_ref[...], x_ref[...])
    row, col = _iota((BLOCK, BLOCK), 0), _iota((BLOCK, BLOCK), 1)
    own = col <= row
    kvalid = (i - 1 + own.astype(jnp.int32)) * BLOCK + col >= PAD
    kvp, kvc = kvp_ref[...], kvc_ref[...]
    heads = range(A_HEADS)
    prev = [_padded_kv_tiles(kvp, c) for c in range(A_KV_HEADS)]
    cur = [_padded_kv_tiles(kvc, c) for c in range(A_KV_HEADS)]
    s = []
    for h in heads:
        mm, idx = h // 2, h % 2
        q_tile = q_ref[:, mm * LANES:(mm + 1) * LANES]
        sc = jnp.where(own, _mm_nt(q_tile, cur[mm // 2][idx]), _mm_nt(q_tile, prev[mm // 2][idx]))
        s.append(jnp.where(kvalid, sc + bias_ref[h], NEG))
    m = [jnp.maximum(jnp.max(s[h], axis=1, keepdims=True), sinks_ref[h]) for h in heads]
    p = [jnp.exp(s[h] - m[h]) for h in heads]
    den = [jnp.sum(p[h], axis=1, keepdims=True) + jnp.exp(sinks_ref[h] - m[h]) for h in heads]
    o = []
    for h in heads:
        c, idx = h // 4, h % 2
        pv = (jnp.dot(jnp.where(own, p[h], 0.0).astype(BF16), cur[c][2 + idx], preferred_element_type=F32)
              + jnp.dot(jnp.where(own, 0.0, p[h]).astype(BF16), prev[c][2 + idx], preferred_element_type=F32))
        o.append(pv * (1.0 / den[h]))
    for mm in range(A_HEADS // 2):
        sl = slice(mm * LANES, (mm + 1) * LANES)
        gt = g_ref[:, sl]
        og_ref[:, sl] = ((o[2 * mm] + o[2 * mm + 1]) * (gt * _sigmoid(gt))).astype(BF16)
    out_ref[...] = resid + jnp.dot(og_ref[...], wout_ref[...], preferred_element_type=F32)


def _attn_prompt_call(sinks, q, kv, g, head, x2d, bias, wout_bf16, nbatch, nblk):
    n = q.shape[0]
    row = lambda b, i: (b * nblk + i, 0)
    prev = lambda b, i: (b * nblk + jnp.maximum(i - 1, 0), 0)
    xrow = lambda b, i: (b * (nblk - 1) + jnp.maximum(i - 1, 0), 0)
    return pl.pallas_call(
        _attn_prompt_kernel,
        grid=(nbatch, nblk),
        in_specs=[pl.BlockSpec(memory_space=pltpu.SMEM),
                  pl.BlockSpec((BLOCK, A_WIDTH), row),
                  pl.BlockSpec((BLOCK, 2 * A_KV_WIDTH), row),
                  pl.BlockSpec((BLOCK, 2 * A_KV_WIDTH), prev),
                  pl.BlockSpec((BLOCK, A_WIDTH), row),
                  pl.BlockSpec((BLOCK, D_MODEL), lambda b, i: (0, 0)),
                  pl.BlockSpec((BLOCK, D_MODEL), xrow),
                  pl.BlockSpec((A_HEADS, BLOCK, BLOCK), lambda b, i: (0, 0, 0)),
                  pl.BlockSpec((A_WIDTH, D_MODEL), lambda b, i: (0, 0))],
        out_specs=pl.BlockSpec((BLOCK, D_MODEL), row),
        out_shape=jax.ShapeDtypeStruct((n, D_MODEL), F32),
        scratch_shapes=[pltpu.VMEM((BLOCK, A_WIDTH), BF16)],
        compiler_params=_cparams(2),
        name="attn_prompt",
    )(sinks, q, kv, kv, g, head, x2d, bias, wout_bf16)


SAMPLE_SB = 8
SAMPLE_KEYS = 2 * BLOCK


def _attn_sample_kernel(sinks_ref, q_ref, kvn_ref, g_ref, h_ref, ck_ref, cv_ref, bias0_ref, bias1_ref,
                        wout_ref, out_ref, nk_ref, nv_ref, og_ref, *, t_new):
    keep = ck_ref.shape[1]
    lo = _iota((1, LANES), 1) < A_HEAD_DIM
    stack = 8 * SUBLANES
    own = (_iota((stack, 1), 0) & (SUBLANES - 1)) // t_new
    piece = _iota((stack, 1), 0) // SUBLANES
    bias_refs = (bias0_ref, bias1_ref)
    n_tile = A_KV_WIDTH // LANES
    pair_ids = range(SAMPLE_SB * t_new // SUBLANES)

    bias_c, bias_n, sink = [], [], []
    for j in range(n_tile):
        heads = slice(8 * j, 8 * j + 8)
        bias_c.append(bias0_ref[heads, :, :keep].reshape(stack, keep))
        bias_n.append([ref[heads, :, keep:keep + SUBLANES].reshape(stack, SUBLANES) for ref in bias_refs])
        col = jnp.zeros((stack, 1), F32)
        for gq in range(8):
            col = jnp.where(piece == gq, sinks_ref[8 * j + gq], col)
        sink.append(col)

    chains = [(p, j, s) for p in pair_ids for j in range(n_tile) for s in range(SUBLANES // t_new)]
    qs = {}
    for p in pair_ids:
        rows = slice(p * SUBLANES, (p + 1) * SUBLANES)
        for j in range(n_tile):
            parts = []
            for gq in range(8):
                t = q_ref[rows, (4 * j + gq // 2) * LANES:(4 * j + gq // 2 + 1) * LANES]
                want_lo = gq < 4
                if (gq % 2 == 0) != want_lo:
                    t = pltpu.roll(t, A_HEAD_DIM, 1)
                parts.append(jnp.where(lo, t, 0.0) if want_lo else jnp.where(lo, 0.0, t))
            qs[p, j] = jnp.concatenate(parts, axis=0).astype(BF16)

    def kv_tiles(p, j, s, base):
        seq = p * (SUBLANES // t_new) + s
        cache = (ck_ref if base == 0 else cv_ref)[seq][:, j * LANES:(j + 1) * LANES].astype(BF16)
        new = kvn_ref[p * SUBLANES:(p + 1) * SUBLANES, base + j * LANES:base + (j + 1) * LANES].astype(BF16)
        return cache, new

    sc, sn = [], []
    for p, j, s in chains:
        kc, kn = kv_tiles(p, j, s, 0)
        sc.append(_mm_nt(qs[p, j], kc) + bias_c[j])
        sn.append(_mm_nt(qs[p, j], kn) + bias_n[j][s])
    mx = [jnp.maximum(jnp.maximum(jnp.max(sc[i], axis=1, keepdims=True),
                                  jnp.max(sn[i], axis=1, keepdims=True)), sink[chains[i][1]])
          for i in range(len(chains))]
    pc = [jnp.exp(sc[i] - mx[i]) for i in range(len(chains))]
    pn = [jnp.exp(sn[i] - mx[i]) for i in range(len(chains))]
    den = [jnp.sum(pc[i], axis=1, keepdims=True) + jnp.sum(pn[i], axis=1, keepdims=True)
           + jnp.exp(sink[chains[i][1]] - mx[i]) for i in range(len(chains))]
    outs = {}
    for i, (p, j, s) in enumerate(chains):
        vc, vn = kv_tiles(p, j, s, A_KV_WIDTH)
        o = (jnp.dot(pc[i].astype(BF16), vc, preferred_element_type=F32)
             + jnp.dot(pn[i].astype(BF16), vn, preferred_element_type=F32)) * (1.0 / den[i])
        outs[p, j] = o if s == 0 else jnp.where(own == s, o, outs[p, j])
    for p in pair_ids:
        rows = slice(p * SUBLANES, (p + 1) * SUBLANES)
        for j in range(n_tile):
            o = outs[p, j]
            for gg in range(4):
                even = o[2 * gg * SUBLANES:(2 * gg + 1) * SUBLANES]
                odd = o[(2 * gg + 1) * SUBLANES:(2 * gg + 2) * SUBLANES]
                if gg < 2:
                    tile = jnp.where(lo, even, pltpu.roll(odd, A_HEAD_DIM, 1))
                else:
                    tile = jnp.where(lo, pltpu.roll(even, A_HEAD_DIM, 1), odd)
                sl = slice((4 * j + gg) * LANES, (4 * j + gg + 1) * LANES)
                gt = g_ref[rows, sl]
                og_ref[rows, sl] = tile * (gt * _sigmoid(gt))
    for seq in range(SAMPLE_SB):
        new = kvn_ref[seq * t_new:(seq + 1) * t_new, :]
        nk_ref[seq, :keep - t_new, :] = ck_ref[seq, t_new:, :]
        nk_ref[seq, keep - t_new:, :] = new[:, :A_KV_WIDTH]
        nv_ref[seq, :keep - t_new, :] = cv_ref[seq, t_new:, :]
        nv_ref[seq, keep - t_new:, :] = new[:, A_KV_WIDTH:]
    out_ref[...] = h_ref[...] + jnp.dot(og_ref[...].astype(BF16), wout_ref[...],
                                        preferred_element_type=F32)


def _attn_sample_call(sinks, q, kv, g, h, cache_k, cache_v, bias0, bias1, wout_bf16, t_new):
    nseq, keep = cache_k.shape[0], cache_k.shape[1]
    rows = SAMPLE_SB * t_new
    row = lambda i: (i, 0)
    cspec = pl.BlockSpec((SAMPLE_SB, keep, A_KV_WIDTH), lambda i: (i, 0, 0))
    bspec = pl.BlockSpec((A_HEADS, SUBLANES, SAMPLE_KEYS), lambda i: (0, 0, 0))
    return pl.pallas_call(
        functools.partial(_attn_sample_kernel, t_new=t_new),
        grid=(nseq // SAMPLE_SB,),
        in_specs=[pl.BlockSpec(memory_space=pltpu.SMEM),
                  pl.BlockSpec((rows, A_WIDTH), row),
                  pl.BlockSpec((rows, 2 * A_KV_WIDTH), row),
                  pl.BlockSpec((rows, A_WIDTH), row),
                  pl.BlockSpec((rows, D_MODEL), row),
                  cspec, cspec, bspec, bspec,
                  pl.BlockSpec((A_WIDTH, D_MODEL), lambda i: (0, 0))],
        out_specs=[pl.BlockSpec((rows, D_MODEL), row), cspec, cspec],
        out_shape=[jax.ShapeDtypeStruct((nseq * t_new, D_MODEL), F32),
                   jax.ShapeDtypeStruct(cache_k.shape, F32),
                   jax.ShapeDtypeStruct(cache_v.shape, F32)],
        scratch_shapes=[pltpu.VMEM((rows, A_WIDTH), F32)],
        compiler_params=_cparams(1),
        name="attn_sample",
    )(sinks, q, kv, g, h, cache_k, cache_v, bias0, bias1, wout_bf16)


def _rwkv_proj_kernel(h_ref, shift_ref, gain_ref, mu_ref, win_ref, w0_ref, w1_ref, w2_ref,
                      a0_ref, a1_ref, a2_ref,
                      r_ref, k_ref, v_ref, g_ref, ld_ref, a_ref, xn_ref, *scratch, seq_len):
    xn = _rmsnorm(h_ref[...], gain_ref[...])
    tm = xn.shape[0]
    rolled = pltpu.roll(xn, 1, 0)
    row = _iota((tm, 1), 0)
    if seq_len is None:
        carry_ref, = scratch

        @pl.when(pl.program_id(1) == 0)
        def _():
            carry_ref[...] = shift_ref[0]

        xprev = jnp.where(row == 0, carry_ref[...], rolled)
        carry_ref[...] = xn[tm - 1:tm, :]
        xn_ref[0] = xn[tm - 1:tm, :]
    else:
        xprev = jnp.where(row % seq_len == 0, shift_ref[...], rolled)
        xn_ref[...] = xn
    dx = xprev - xn

    def mix(c):
        return (xn + dx * mu_ref[c:c + 1, :]).astype(BF16)

    for c, o_ref in enumerate((r_ref, k_ref, v_ref, g_ref)):
        o_ref[...] = jnp.dot(mix(c), win_ref[c], preferred_element_type=F32)
    lw = jnp.tanh(jnp.dot(mix(4), w1_ref[...], preferred_element_type=F32))
    z = w0_ref[...] + jnp.dot(lw.astype(BF16), w2_ref[...], preferred_element_type=F32)
    ld_ref[...] = -math.exp(-0.5) * _sigmoid(z)
    la = jnp.dot(mix(5), a1_ref[...], preferred_element_type=F32)
    a_ref[...] = _sigmoid(a0_ref[...] + jnp.dot(la.astype(BF16), a2_ref[...],
                                                preferred_element_type=F32))


def _rwkv_proj_call(h, shift, p, nbatch, ntile, tm, seq_len):
    n = h.shape[0]
    row = lambda b, i: (b * ntile + i, 0)
    full2 = lambda b, i: (0, 0)
    if seq_len is None:
        shift_spec = pl.BlockSpec((1, 1, D_MODEL), lambda b, i: (b, 0, 0))
        xn_spec = pl.BlockSpec((1, 1, D_MODEL), lambda b, i: (b, 0, 0))
        xn_shape = jax.ShapeDtypeStruct((nbatch, 1, D_MODEL), F32)
        scratch = [pltpu.VMEM((1, D_MODEL), F32)]
    else:
        shift_spec = pl.BlockSpec((tm, D_MODEL), row)
        xn_spec = pl.BlockSpec((tm, D_MODEL), row)
        xn_shape = jax.ShapeDtypeStruct((n, D_MODEL), F32)
        scratch = []
    lora = p["w1"].shape[1]
    big = jax.ShapeDtypeStruct((n, D_MODEL), F32)
    return pl.pallas_call(
        functools.partial(_rwkv_proj_kernel, seq_len=seq_len),
        grid=(nbatch, ntile),
        in_specs=[pl.BlockSpec((tm, D_MODEL), row),
                  shift_spec,
                  pl.BlockSpec((1, D_MODEL), full2),
                  pl.BlockSpec(p["mu"].shape, full2),
                  pl.BlockSpec(p["w_in"].shape, lambda b, i: (0, 0, 0)),
                  pl.BlockSpec((1, D_MODEL), full2),
                  pl.BlockSpec((D_MODEL, lora), full2),
                  pl.BlockSpec((lora, D_MODEL), full2),
                  pl.BlockSpec((1, D_MODEL), full2),
                  pl.BlockSpec((D_MODEL, lora), full2),
                  pl.BlockSpec((lora, D_MODEL), full2)],
        out_specs=[pl.BlockSpec((tm, D_MODEL), row)] * 6 + [xn_spec],
        out_shape=[big] * 6 + [xn_shape],
        scratch_shapes=scratch,
        compiler_params=_cparams(2),
        name="rwkv_proj",
    )(h, shift, p["gain"], p["mu"], p["w_in"], p["w0"], p["w1"], p["w2"], p["a0"], p["a1"], p["a2"])


NN = (((1,), (0,)), ((), ()))
NT = (((1,), (1,)), ((), ()))
TN = (((0,), (0,)), ((), ()))


def _split(x):
    hi = x.astype(BF16)
    return hi, (x - hi.astype(F32)).astype(BF16)


def _mm(a, b, dn):
    return lax.dot_general(a, b, dn, preferred_element_type=F32)


def _mm3s(a, b, dn):
    (ah, al), (bh, bl) = a, b
    return _mm(al, bh, dn) + _mm(ah, bl, dn) + _mm(ah, bh, dn)


def _mm3(a, b, dn):
    return _mm3s(_split(a), _split(b), dn)


def _mm3_top(x_hi, x_lo_top, b, dn):
    bh, bl = b
    c = x_lo_top.shape[0]
    full = _mm(x_hi, bh, dn)
    return full[:c] + _mm(x_lo_top, bh, dn) + _mm(x_hi[:c], bl, dn), full[c:]


def _seg_sum(x, ones_bf16):
    hi, lo = _split(x)
    return _mm(lo, ones_bf16, NN) + _mm(hi, ones_bf16, NN)


def _lane_lo():
    return _iota((1, LANES), 1) < R_HEAD_DIM


def _same_half():
    return (_iota((LANES, LANES), 0) < R_HEAD_DIM) == (_iota((LANES, LANES), 1) < R_HEAD_DIM)


def _ones_blk():
    return _same_half().astype(BF16)


def _bd(z):
    lo = _lane_lo()
    return jnp.concatenate([jnp.where(lo, z, 0.0), jnp.where(lo, 0.0, z)], axis=0)


def _bd_swap(z):
    lo = _lane_lo()
    return jnp.concatenate([jnp.where(lo, 0.0, z), jnp.where(lo, z, 0.0)], axis=0)


def _split_map(z, f):
    hi, lw = _split(z)
    return f(hi), f(lw)


def _wkv_batch_stage(r, k, v, a, ld, kkg, kag, fillers=(), early=None):
    c = r[0].shape[0]
    pairs = range(len(r))
    half = R_HEAD_DIM
    lo = _lane_lo()
    t_row, t_col = _iota((c, LANES), 0), _iota((c, LANES), 1) & (half - 1)
    strict, incl = t_row > t_col, t_row >= t_col
    tri = (_iota((c, c), 0) >= _iota((c, c), 1)).astype(BF16)
    ones_blk = _ones_blk()
    eye_pair = (_iota((c, LANES), 0) == (_iota((c, LANES), 1) & (half - 1))).astype(F32)

    kkx = [k[j] * kkg[j] for j in pairs]
    ssq = [_seg_sum(kkx[j] * kkx[j], ones_blk) for j in pairs]
    ld_s = [_split(ld[j]) for j in pairs]
    cs = [_mm(tri, ld_s[j][1], NN) + _mm(tri, ld_s[j][0], NN) for j in pairs]
    tot = [cs[j][c - 1:c, :] for j in pairs]
    x_hi, at_lo, kh_all, bke, vbd_s, lk_a, lk_r, lb_a, lb_r = ([] for _ in range(9))
    for j in pairs:
        kk = kkx[j] / jnp.maximum(jnp.sqrt(ssq[j]), 1e-12)
        kh = k[j] * (1.0 + (a[j] - 1.0) * kag[j])
        bv = kk * a[j]
        e_neg = jnp.exp(-cs[j])
        e_end = jnp.exp(tot[j] - cs[j])
        at, rt = -kk * jnp.exp(cs[j] - ld[j]), r[j] * jnp.exp(cs[j])
        at_hi, at_l = _split(at)
        x_hi.append(jnp.concatenate([at_hi, rt.astype(BF16)], axis=0))
        at_lo.append(at_l)
        kh_all.append(kh)
        bt, kt = _split(bv * e_neg), _split(kh * e_neg)
        yn_s = tuple(jnp.concatenate([p, q], axis=0) for p, q in zip(bt, kt))
        ys_s = tuple(jnp.concatenate([q, p], axis=0) for p, q in zip(bt, kt))
        bke.append(jnp.concatenate([bv * e_end, kh * e_end], axis=0))
        vbd_s.append(_split_map(v[j], _bd))
        ga_a, ga_r = _mm3_top(jnp.where(lo, x_hi[j], 0.0), jnp.where(lo, at_l, 0.0), ys_s, NT)
        gb_a, gb_r = _mm3_top(jnp.where(lo, 0.0, x_hi[j]), jnp.where(lo, 0.0, at_l), yn_s, NT)
        lk_a.append(jnp.where(strict, jnp.where(lo, ga_a, gb_a), 0.0))
        lk_r.append(jnp.where(incl, jnp.where(lo, ga_r, gb_r), 0.0))
        lb_a.append(jnp.where(strict, jnp.where(lo, gb_a, ga_a), 0.0))
        lb_r.append(jnp.where(incl, jnp.where(lo, gb_r, ga_r), 0.0))
    fillers = list(fillers)
    if early is not None:
        early.update(x_hi=x_hi, at_lo=at_lo)
    pw = lb_a
    acc = [eye_pair + pw[j] for j in pairs]
    pw = [_mm(pw[j].astype(BF16), _bd(pw[j].astype(BF16)), NN) for j in pairs]
    from_v = []
    for j in pairs:
        lk_hi, lk_lo = _split(lk_a[j])
        from_v.append(_mm3_top(jnp.concatenate([lk_hi, lk_r[j].astype(BF16)], axis=0), lk_lo, vbd_s[j], NN))
    for _ in range(int(math.log2(c)) - 2):
        both = [_mm(jnp.concatenate([pw[j], acc[j]], axis=0).astype(BF16), _bd(pw[j].astype(BF16)), NN)
                for j in pairs]
        if fillers:
            fillers.pop(0)()
        pw = [both[j][:c] for j in pairs]
        acc = [acc[j] + both[j][c:] for j in pairs]
    tinv = [acc[j] + _mm(acc[j].astype(BF16), _bd(pw[j].astype(BF16)), NN) for j in pairs]
    for fill in fillers:
        fill()
    return dict(x_hi=x_hi, at_lo=at_lo, kh=kh_all, lb_r=lb_r, tinv=tinv, from_v=from_v, bke=bke,
                tot=tot, ones_blk=ones_blk)


def _wkv_finish(stage, from_state_a, from_state_r):
    pairs = range(len(from_state_a))
    u = [_mm3s(_split(stage["tinv"][j]),
               _split_map(from_state_a[j] + stage["from_v"][j][0], _bd_swap), NN) for j in pairs]
    y = [from_state_r[j] + stage["from_v"][j][1]
         + _mm(stage["lb_r"][j].astype(BF16), _bd_swap(u[j].astype(BF16)), NN) for j in pairs]
    return u, y


def _wkv_gate(y, r, kh, v, g, rk, lng, lnb, ones_blk):
    pairs = range(len(y))
    inv_n = 1.0 / R_HEAD_DIM

    def seg(t):
        return _mm(t.astype(BF16), ones_blk, NN)

    rkk = [seg(r[j] * kh[j] * rk[j]) for j in pairs]
    mean = [seg(y[j]) * inv_n for j in pairs]
    d = [y[j] - mean[j] for j in pairs]
    var = [seg(d[j] * d[j]) * inv_n for j in pairs]
    return [((d[j] * lax.rsqrt(var[j] + GN_EPS) * lng[j] + lnb[j] + rkk[j] * v[j])
             * (g[j] * _sigmoid(g[j]))).astype(BF16) for j in pairs]


def _wkv_chunk_kernel(r_ref, k_ref, v_ref, a_ref, ld_ref, g_ref, kkg_ref, kag_ref, rk_ref, lng_ref, lnb_ref,
                      z_ref, sout_ref, st_ref):
    pairs = range(r_ref.shape[1] // LANES)
    tiles = [slice(j * LANES, (j + 1) * LANES) for j in pairs]

    @pl.when(pl.program_id(2) == 0)
    def _():
        st_ref[...] = jnp.zeros_like(st_ref)

    r, k, v, a, ld = ([ref[:, sl] for sl in tiles] for ref in (r_ref, k_ref, v_ref, a_ref, ld_ref))
    st = [st_ref[j] for j in pairs]
    from_state = {}
    stage = {}

    def state_products(group):
        def run():
            for j in group:
                from_state[j] = _mm3_top(stage["x_hi"][j], stage["at_lo"][j], _split(st[j]), NN)
        return run

    n_fill = 4
    groups = [list(pairs)[i::n_fill] for i in range(n_fill)]
    stage.update(_wkv_batch_stage(r, k, v, a, ld, [kkg_ref[:, sl] for sl in tiles],
                                  [kag_ref[:, sl] for sl in tiles],
                                  fillers=[state_products(grp) for grp in groups if grp], early=stage))
    same_half = _same_half()
    eye_full = _iota((LANES, LANES), 0) == _iota((LANES, LANES), 1)
    us, ys = _wkv_finish(stage, [from_state[j][0] for j in pairs], [from_state[j][1] for j in pairs])
    upd = [_mm3(stage["bke"][j], jnp.concatenate([us[j], v[j]], axis=0), TN) for j in pairs]
    zs = _wkv_gate(ys, r, stage["kh"], v, *([ref[:, sl] for sl in tiles]
                                          for ref in (g_ref, rk_ref, lng_ref, lnb_ref)), stage["ones_blk"])
    for j in pairs:
        z_ref[:, tiles[j]] = zs[j]
    for j in pairs:
        w_col = jnp.sum(jnp.where(eye_full, jnp.exp(stage["tot"][j]), 0.0), axis=1, keepdims=True)
        st_new = w_col * st[j] + jnp.where(same_half, upd[j], 0.0)
        st_ref[j] = st_new
        sout_ref[0, j] = st_new


def _wkv_chunk_call(r, k, v, a, ld, g, p, nbatch, seq, skip_rows):
    c = WKV_CHUNK
    nchunk = seq // c
    skip = skip_rows // c
    out_chunks = nchunk - skip
    width = WKV_PAIRS_PER_STEP * LANES
    ngroup = D_MODEL // width
    tile = pl.BlockSpec((c, width), lambda b, j, t: (b * nchunk + t, j))
    ztile = pl.BlockSpec((c, width), lambda b, j, t: (b * out_chunks + jnp.maximum(t - skip, 0), j))
    par = pl.BlockSpec((1, width), lambda b, j, t: (0, j))
    return pl.pallas_call(
        _wkv_chunk_kernel,
        grid=(nbatch, ngroup, nchunk),
        in_specs=[tile] * 6 + [par] * 5,
        out_specs=[ztile, pl.BlockSpec((1, WKV_PAIRS_PER_STEP, LANES, LANES), lambda b, j, t: (b, j, 0, 0))],
        out_shape=[jax.ShapeDtypeStruct((nbatch * out_chunks * c, D_MODEL), BF16),
                   jax.ShapeDtypeStruct((nbatch, D_MODEL // LANES, LANES, LANES), F32)],
        scratch_shapes=[pltpu.VMEM((WKV_PAIRS_PER_STEP, LANES, LANES), F32)],
        compiler_params=_cparams(3),
        name="wkv_chunk",
    )(r, k, v, a, ld, g, p["k_k"], p["k_a"], p["r_k"], p["ln_g"], p["ln_b"])


WKV_LANES_UNROLL = 4


def _wkv_lanes_kernel(r_ref, k_ref, v_ref, a_ref, ld_ref, g_ref, kkg_ref, kag_ref, rk_ref, lng_ref, lnb_ref,
                      s_ref, z_ref, sout_ref, prep_ref, y_ref, *, seq_len):
    n = R_HEAD_DIM
    nseq = s_ref.shape[3]
    eye = _iota((LANES, LANES), 0) == _iota((LANES, LANES), 1)

    def column(ref):
        return jnp.sum(jnp.where(eye, ref[...], 0.0), axis=1, keepdims=True)

    kkg, kag, rk, lng, lnb = (column(ref) for ref in (kkg_ref, kag_ref, rk_ref, lng_ref, lnb_ref))

    def token(ref, t):
        return ref[pl.ds(t, nseq, stride=seq_len), :].T

    bonus = []
    for t in range(seq_len):
        r, k, v, a = (token(ref, t) for ref in (r_ref, k_ref, v_ref, a_ref))
        w = jnp.exp(token(ld_ref, t))
        kkx = k * kkg
        kh = k * (1.0 + (a - 1.0) * kag)
        rkk = r * kh * rk
        tiles = []
        for hh in range(2):
            rows = slice(hh * n, (hh + 1) * n)
            nrm = jnp.sqrt(jnp.sum(kkx[rows] * kkx[rows], axis=0, keepdims=True))
            kk = kkx[rows] / jnp.maximum(nrm, 1e-12)
            for q, val in enumerate((w[rows], -kk, kk * a[rows], kh[rows], r[rows], v[rows])):
                prep_ref[q, t, hh] = val
            tiles.append(jnp.sum(rkk[rows], axis=0, keepdims=True) * v[rows])
        bonus.append(tiles)

    for hh in range(2):
        def advance(i, carry, hh=hh):
            for u in range(WKV_LANES_UNROLL):
                vi = i * WKV_LANES_UNROLL + u
                slab = s_ref[hh, vi]
                for t in range(seq_len):
                    w, av, bv, kh, r = (prep_ref[q, t, hh] for q in range(5))
                    vrow = prep_ref[5, t, hh, pl.ds(vi, 1), :]
                    sa = jnp.sum(slab * av, axis=0, keepdims=True)
                    slab = slab * w + sa * bv + vrow * kh
                    y_ref[t, hh, pl.ds(vi, 1), :] = jnp.sum(slab * r, axis=0, keepdims=True)
                sout_ref[hh, vi] = slab
            return carry

        lax.fori_loop(0, n // WKV_LANES_UNROLL, advance, 0)

    for t in range(seq_len):
        parts = []
        for hh in range(2):
            rows = slice(hh * n, (hh + 1) * n)
            y = y_ref[t, hh]
            d = y - jnp.mean(y, axis=0, keepdims=True)
            var = jnp.mean(d * d, axis=0, keepdims=True)
            parts.append(d * lax.rsqrt(var + GN_EPS) * lng[rows] + lnb[rows] + bonus[t][hh])
        g = token(g_ref, t)
        z = jnp.concatenate(parts, axis=0) * (g * _sigmoid(g))
        z_ref[pl.ds(t, nseq, stride=seq_len), :] = z.T


def _wkv_lanes_call(r, k, v, a, ld, g, p, state_hvkb, seq_len):
    n = r.shape[0]
    nseq = state_hvkb.shape[3]
    tile = pl.BlockSpec((n, LANES), lambda j: (0, j))
    par = pl.BlockSpec((1, LANES), lambda j: (0, j))
    sspec = pl.BlockSpec((2, R_HEAD_DIM, R_HEAD_DIM, nseq), lambda j: (j, 0, 0, 0))
    return pl.pallas_call(
        functools.partial(_wkv_lanes_kernel, seq_len=seq_len),
        grid=(D_MODEL // LANES,),
        in_specs=[tile] * 6 + [par] * 5 + [sspec],
        out_specs=[tile, sspec],
        out_shape=[jax.ShapeDtypeStruct((n, D_MODEL), F32),
                   jax.ShapeDtypeStruct(state_hvkb.shape, F32)],
        scratch_shapes=[pltpu.VMEM((6, seq_len, 2, R_HEAD_DIM, nseq), F32),
                        pltpu.VMEM((seq_len, 2, R_HEAD_DIM, nseq), F32)],
        compiler_params=_cparams(1),
        name="wkv_lanes",
    )(r, k, v, a, ld, g, p["k_k"], p["k_a"], p["r_k"], p["ln_g"], p["ln_b"], state_hvkb)


RWKV_OUT_PIECES = 4


def _rwkv_out_kernel(z_ref, *refs):
    h_refs, (wout_ref, fg_ref, out_ref) = refs[:RWKV_OUT_PIECES], refs[RWKV_OUT_PIECES:]
    h = jnp.concatenate([ref[...] for ref in h_refs], axis=0)
    h2 = h + jnp.dot(z_ref[...].astype(BF16), wout_ref[...], preferred_element_type=F32)
    out_ref[...] = _rmsnorm(h2, fg_ref[...])


def _rwkv_out_call(z, h, p, nbatch, ntile, tm, h_pieces_per_batch, skip):
    piece = tm // RWKV_OUT_PIECES
    dst = lambda b, i: (b * ntile + i, 0)
    hspec = [pl.BlockSpec((piece, D_MODEL),
                          functools.partial(lambda b, i, kk: (b * h_pieces_per_batch + i * RWKV_OUT_PIECES + skip + kk, 0),
                                            kk=kk))
             for kk in range(RWKV_OUT_PIECES)]
    return pl.pallas_call(
        _rwkv_out_kernel,
        grid=(nbatch, ntile),
        in_specs=[pl.BlockSpec((tm, D_MODEL), dst)] + hspec
                 + [pl.BlockSpec((D_MODEL, D_MODEL), lambda b, i: (0, 0)),
                    pl.BlockSpec((1, D_MODEL), lambda b, i: (0, 0))],
        out_specs=pl.BlockSpec((tm, D_MODEL), dst),
        out_shape=jax.ShapeDtypeStruct((nbatch * ntile * tm, D_MODEL), F32),
        compiler_params=_cparams(2),
        name="rwkv_out",
    )(z, *([h] * RWKV_OUT_PIECES), p["w_out"], p["final_gain"])


def _prompt_bucket():
    assert WINDOW == BLOCK
    rel = (np.arange(BLOCK)[:, None] - np.arange(BLOCK)[None, :]) % BLOCK
    return _t5_bucket_np(rel)


def _sample_bucket(keep, t_new, slot):
    t = (np.arange(SUBLANES) % t_new)[:, None]
    j = np.arange(SAMPLE_KEYS)[None, :]
    own = j - keep - slot * t_new
    rel = np.where(j < keep, keep + t - j, t - own)
    ok = (rel >= 0) & (rel < WINDOW) & ((j < keep) | ((own >= 0) & (own < t_new)))
    return np.where(ok, _t5_bucket_np(rel), -1).astype(np.int32)


def kernel(x_prompt, x_sample, cache_win_k, cache_win_v, state_wkv, state_shift, meta_tokens, rel_bias_table, norm_gain, final_gain, attn_w_in, attn_sinks, attn_w_out, rwkv_mu, rwkv_w_in, rwkv_w0, rwkv_w1, rwkv_w2, rwkv_a0, rwkv_a1, rwkv_a2, rwkv_k_k, rwkv_k_a, rwkv_r_k, rwkv_ln_gamma, rwkv_ln_beta, rwkv_w_out):
    nb, seq, _ = x_prompt.shape
    ns, t_new, _ = x_sample.shape
    keep = cache_win_k.shape[2]
    lp = seq + BLOCK
    nblk = lp // BLOCK
    row = lambda x: x.reshape(1, D_MODEL)

    w_in0 = attn_w_in[0].astype(BF16)
    w_out0 = attn_w_out[0].astype(BF16)
    gain0 = row(norm_gain[0])
    sinks = attn_sinks[0]
    rp = dict(gain=row(norm_gain[1]), mu=rwkv_mu[0], w_in=rwkv_w_in[0].astype(BF16),
              w0=row(rwkv_w0[0]), w1=rwkv_w1[0].astype(BF16), w2=rwkv_w2[0].astype(BF16),
              a0=row(rwkv_a0[0]), a1=rwkv_a1[0].astype(BF16), a2=rwkv_a2[0].astype(BF16),
              k_k=row(rwkv_k_k[0]), k_a=row(rwkv_k_a[0]), r_k=row(rwkv_r_k[0]), ln_g=row(rwkv_ln_gamma[0]),
              ln_b=row(rwkv_ln_beta[0]), w_out=rwkv_w_out[0].astype(BF16),
              final_gain=row(final_gain))

    bias_p, *bias_s = _bias_call(rel_bias_table, [_prompt_bucket()]
                                 + [_sample_bucket(keep, t_new, slot) for slot in range(2)])

    head = jnp.concatenate([jnp.zeros((PAD, D_MODEL), F32), meta_tokens.astype(F32)], axis=0)
    xp = x_prompt.reshape(nb * seq, D_MODEL)
    q, kv, g = _attn_proj_call(xp, head, gain0, w_in0, BF16, nb, lp // ATTN_PROJ_ROWS,
                               ATTN_PROJ_ROWS // BLOCK, BLOCK)
    h1 = _attn_prompt_call(sinks, q, kv, g, head, xp, bias_p, w_out0, nb, nblk)
    kv3 = kv.reshape(nb, lp, 2 * A_KV_WIDTH)[:, lp - WINDOW:, :]
    win_k_p = kv3[:, :, :A_KV_WIDTH].reshape(1, nb, WINDOW, A_KV_HEADS, A_HEAD_DIM)
    win_v_p = kv3[:, :, A_KV_WIDTH:].reshape(1, nb, WINDOW, A_KV_HEADS, A_HEAD_DIM)

    shift0 = jnp.zeros((nb, 1, D_MODEL), F32)
    r, k, v, g1, ld, a, xlast = _rwkv_proj_call(h1, shift0, rp, nb, lp // RWKV_PROJ_ROWS, RWKV_PROJ_ROWS, None)
    z, st = _wkv_chunk_call(r, k, v, a, ld, g1, rp, nb, lp, BLOCK)
    y_prompt = _rwkv_out_call(z, h1, rp, nb, seq // RWKV_OUT_ROWS, RWKV_OUT_ROWS,
                              lp * RWKV_OUT_PIECES // RWKV_OUT_ROWS, BLOCK * RWKV_OUT_PIECES // RWKV_OUT_ROWS)
    y_prompt = y_prompt.reshape(nb, seq, D_MODEL)
    st = st.reshape(nb, D_MODEL // LANES, 2, R_HEAD_DIM, 2, R_HEAD_DIM)
    st = jnp.stack([st[:, :, 0, :, 0, :], st[:, :, 1, :, 1, :]], axis=2)
    wkv_p = jnp.swapaxes(st, -1, -2).reshape(1, nb, R_HEADS, R_HEAD_DIM, R_HEAD_DIM)
    shift_p = xlast.reshape(1, nb, D_MODEL)

    xs = x_sample.reshape(ns * t_new, D_MODEL)
    qs, kvs, gs = _attn_proj_call(xs, None, gain0, w_in0, F32, 1, 1, 1, ns * t_new)
    ck = cache_win_k[0].reshape(ns, keep, A_KV_WIDTH)
    cv = cache_win_v[0].reshape(ns, keep, A_KV_WIDTH)
    h1s, nk, nv = _attn_sample_call(sinks, qs, kvs, gs, xs, ck, cv, bias_s[0], bias_s[1], w_out0, t_new)
    win_k_s = nk.reshape(1, ns, keep, A_KV_HEADS, A_HEAD_DIM)
    win_v_s = nv.reshape(1, ns, keep, A_KV_HEADS, A_HEAD_DIM)

    shift_rows = jnp.repeat(state_shift[0], t_new, axis=0)
    tms = 256
    rs, ks, vs, g1s, lds, as_, xns = _rwkv_proj_call(h1s, shift_rows, rp, 1, ns * t_new // tms, tms, t_new)
    zs, st_s = _wkv_lanes_call(rs, ks, vs, as_, lds, g1s, rp, jnp.transpose(state_wkv[0], (1, 2, 3, 0)), t_new)
    y_sample = _rwkv_out_call(zs, h1s, rp, 1, ns * t_new // RWKV_OUT_ROWS, RWKV_OUT_ROWS,
                              ns * t_new * RWKV_OUT_PIECES // RWKV_OUT_ROWS, 0)
    y_sample = y_sample.reshape(ns, t_new, D_MODEL)
    wkv_s = jnp.transpose(st_s, (3, 0, 1, 2))[None]
    shift_s = xns.reshape(ns, t_new, D_MODEL)[:, t_new - 1][None]

    return (y_prompt, y_sample, win_k_p, win_v_p, wkv_p, shift_p, win_k_s, win_v_s, wkv_s, shift_s)
```

```python
import functools
import math

import numpy as np
import jax
import jax.numpy as jnp
from jax import lax
from jax.experimental import pallas as pl
from jax.experimental.pallas import tpu as pltpu

F32 = jnp.float32
BF16 = jnp.bfloat16

D_MODEL = 1024
N_META = 16
RMS_EPS = 1e-6
A_HEADS = 16
A_KV_HEADS = 4
A_HEAD_DIM = 64
A_WIDTH = A_HEADS * A_HEAD_DIM
A_KV_WIDTH = A_KV_HEADS * A_HEAD_DIM
WINDOW = 128
BLOCK = 128
N_BUCKETS = 32
MAX_DISTANCE = 128
R_HEAD_DIM = 64
R_HEADS = D_MODEL // R_HEAD_DIM
GN_EPS = 64e-5

LANES = 128
SUBLANES = 8
PAD = BLOCK - N_META
NEG = -1e30
WKV_CHUNK = 64
WKV_CHUNKS_PER_STEP = 3
ATTN_PROJ_ROWS = 384
RWKV_PROJ_ROWS = 384
RWKV_OUT_ROWS = 512
VMEM_LIMIT = 56 * 1024 * 1024


def _cparams(n_axes):
    return pltpu.CompilerParams(dimension_semantics=("arbitrary",) * n_axes,
                                vmem_limit_bytes=VMEM_LIMIT)


def _rmsnorm(x, gain):
    return x * lax.rsqrt(jnp.mean(x * x, axis=-1, keepdims=True) + RMS_EPS) * gain


def _sigmoid(x):
    return 1.0 / (1.0 + jnp.exp(-x))


def _iota(shape, dim):
    return lax.broadcasted_iota(jnp.int32, shape, dim)


def _t5_bucket_np(rel):
    n = np.maximum(rel, 0)
    max_exact = N_BUCKETS // 2
    nf = np.maximum(n, max_exact).astype(np.float32)
    scale = np.float32(math.log(MAX_DISTANCE / max_exact))
    large = max_exact + (np.log(nf / np.float32(max_exact)) / scale
                         * np.float32(N_BUCKETS - max_exact)).astype(np.int32)
    large = np.minimum(large, N_BUCKETS - 1)
    return np.where(n < max_exact, n, large).astype(np.int32)


def _bias_kernel(table_ref, *refs):
    h = pl.program_id(0)
    n = len(refs) // 2
    for bucket_ref, out_ref in zip(refs[:n], refs[n:]):
        bk = bucket_ref[...]
        acc = jnp.full(bk.shape, NEG, F32)
        for b in range(N_BUCKETS):
            acc = jnp.where(bk == b, table_ref[b, h], acc)
        out_ref[0] = acc


def _bias_call(table, buckets_np):
    return pl.pallas_call(
        _bias_kernel,
        grid=(A_HEADS,),
        in_specs=[pl.BlockSpec(memory_space=pltpu.SMEM)]
                 + [pl.BlockSpec(bk.shape, lambda h: (0, 0)) for bk in buckets_np],
        out_specs=[pl.BlockSpec((1,) + bk.shape, lambda h: (h, 0, 0)) for bk in buckets_np],
        out_shape=[jax.ShapeDtypeStruct((A_HEADS,) + bk.shape, F32) for bk in buckets_np],
        compiler_params=_cparams(1),
        name="bias_expand",
    )(table, *(jnp.asarray(bk) for bk in buckets_np))


def _attn_proj_kernel(head_ref, *refs, n_piece):
    x_refs, (gain_ref, w_ref, q_ref, kv_ref, g_ref) = refs[:n_piece], refs[n_piece:]
    first = x_refs[0][...]
    if head_ref is not None:
        first = jnp.where(pl.program_id(1) == 0, head_ref[...], first)
    x = jnp.concatenate([first] + [ref[...] for ref in x_refs[1:]], axis=0)
    xn = _rmsnorm(x, gain_ref[...])
    proj = jnp.dot(xn.astype(BF16), w_ref[...], preferred_element_type=F32)
    q_ref[...] = (proj[:, :A_WIDTH] * (A_HEAD_DIM ** -0.5)).astype(q_ref.dtype)
    kv_ref[...] = proj[:, A_WIDTH:A_WIDTH + 2 * A_KV_WIDTH]
    g_ref[...] = proj[:, A_WIDTH + 2 * A_KV_WIDTH:]


def _attn_proj_call(x2d, head, gain, w_bf16, q_dtype, nbatch, ntile, n_piece, piece):
    tm = n_piece * piece
    wcols = w_bf16.shape[1]
    per_seq = x2d.shape[0] // (nbatch * piece)
    lead = 0 if head is None else 1
    dst = lambda b, i: (b * ntile + i, 0)
    xspec = [pl.BlockSpec((piece, D_MODEL),
                          functools.partial(lambda b, i, kk: (b * per_seq + jnp.maximum(i * n_piece + kk - lead, 0), 0),
                                            kk=kk))
             for kk in range(n_piece)]
    kern = functools.partial(_attn_proj_kernel, n_piece=n_piece)
    operands = [x2d] * n_piece + [gain, w_bf16]
    if head is None:
        kern = functools.partial(kern, None)
        head_spec = []
    else:
        head_spec = [pl.BlockSpec((piece, D_MODEL), lambda b, i: (0, 0))]
        operands = [head] + operands
    n = nbatch * ntile * tm
    return pl.pallas_call(
        kern,
        grid=(nbatch, ntile),
        in_specs=head_spec + xspec + [pl.BlockSpec((1, D_MODEL), lambda b, i: (0, 0)),
                                      pl.BlockSpec((D_MODEL, wcols), lambda b, i: (0, 0))],
        out_specs=[pl.BlockSpec((tm, A_WIDTH), dst),
                   pl.BlockSpec((tm, 2 * A_KV_WIDTH), dst),
                   pl.BlockSpec((tm, A_WIDTH), dst)],
        out_shape=[jax.ShapeDtypeStruct((n, A_WIDTH), q_dtype),
                   jax.ShapeDtypeStruct((n, 2 * A_KV_WIDTH), F32),
                   jax.ShapeDtypeStruct((n, A_WIDTH), F32)],
        compiler_params=_cparams(2),
        name="attn_proj",
    )(*operands)


def _padded_kv_tiles(kv, c):
    lo = _iota((1, LANES), 1) < A_HEAD_DIM
    j = c // 2
    out = []
    for base in (0, A_KV_WIDTH):
        t = kv[:, base + j * LANES: base + (j + 1) * LANES]
        tr = pltpu.roll(t, A_HEAD_DIM, 1)
        if c % 2 == 0:
            even, odd = jnp.where(lo, t, 0.0), jnp.where(lo, 0.0, tr)
        else:
            even, odd = jnp.where(lo, tr, 0.0), jnp.where(lo, 0.0, t)
        out += [even.astype(BF16), odd.astype(BF16)]
    return out


def _mm_nt(a, b):
    return lax.dot_general(a, b, (((1,), (1,)), ((), ())), preferred_element_type=F32)


def _attn_prompt_kernel(sinks_ref, q_ref, kvc_ref, kvp_ref, g_ref, head_ref, x_ref, bias_ref, wout_ref,
                        out_ref, og_ref):
    i = pl.program_id(1)
    resid = jnp.where(i == 0, head_ref[...], x_ref[...])
    row, col = _iota((BLOCK, BLOCK), 0), _iota((BLOCK, BLOCK), 1)
    own = col <= row
    kvalid = (i - 1 + own.astype(jnp.int32)) * BLOCK + col >= PAD
    kvp, kvc = kvp_ref[...], kvc_ref[...]
    heads = range(A_HEADS)
    prev = [_padded_kv_tiles(kvp, c) for c in range(A_KV_HEADS)]
    cur = [_padded_kv_tiles(kvc, c) for c in range(A_KV_HEADS)]
    s = []
    for h in heads:
        mm, idx = h // 2, h % 2
        q_tile = q_ref[:, mm * LANES:(mm + 1) * LANES]
        sc = jnp.where(own, _mm_nt(q_tile, cur[mm // 2][idx]), _mm_nt(q_tile, prev[mm // 2][idx]))
        s.append(jnp.where(kvalid, sc + bias_ref[h], NEG))
    m = [jnp.maximum(jnp.max(s[h], axis=1, keepdims=True), sinks_ref[h]) for h in heads]
    p = [jnp.exp(s[h] - m[h]) for h in heads]
    den = [jnp.sum(p[h], axis=1, keepdims=True) + jnp.exp(sinks_ref[h] - m[h]) for h in heads]
    o = []
    for h in heads:
        c, idx = h // 4, h % 2
        pv = (jnp.dot(jnp.where(own, p[h], 0.0).astype(BF16), cur[c][2 + idx], preferred_element_type=F32)
              + jnp.dot(jnp.where(own, 0.0, p[h]).astype(BF16), prev[c][2 + idx], preferred_element_type=F32))
        o.append(pv * (1.0 / den[h]))
    for mm in range(A_HEADS // 2):
        sl = slice(mm * LANES, (mm + 1) * LANES)
        gt = g_ref[:, sl]
        og_ref[:, sl] = ((o[2 * mm] + o[2 * mm + 1]) * (gt * _sigmoid(gt))).astype(BF16)
    out_ref[...] = resid + jnp.dot(og_ref[...], wout_ref[...], preferred_element_type=F32)


def _attn_prompt_call(sinks, q, kv, g, head, x2d, bias, wout_bf16, nbatch, nblk):
    n = q.shape[0]
    row = lambda b, i: (b * nblk + i, 0)
    prev = lambda b, i: (b * nblk + jnp.maximum(i - 1, 0), 0)
    xrow = lambda b, i: (b * (nblk - 1) + jnp.maximum(i - 1, 0), 0)
    return pl.pallas_call(
        _attn_prompt_kernel,
        grid=(nbatch, nblk),
        in_specs=[pl.BlockSpec(memory_space=pltpu.SMEM),
                  pl.BlockSpec((BLOCK, A_WIDTH), row),
                  pl.BlockSpec((BLOCK, 2 * A_KV_WIDTH), row),
                  pl.BlockSpec((BLOCK, 2 * A_KV_WIDTH), prev),
                  pl.BlockSpec((BLOCK, A_WIDTH), row),
                  pl.BlockSpec((BLOCK, D_MODEL), lambda b, i: (0, 0)),
                  pl.BlockSpec((BLOCK, D_MODEL), xrow),
                  pl.BlockSpec((A_HEADS, BLOCK, BLOCK), lambda b, i: (0, 0, 0)),
                  pl.BlockSpec((A_WIDTH, D_MODEL), lambda b, i: (0, 0))],
        out_specs=pl.BlockSpec((BLOCK, D_MODEL), row),
        out_shape=jax.ShapeDtypeStruct((n, D_MODEL), F32),
        scratch_shapes=[pltpu.VMEM((BLOCK, A_WIDTH), BF16)],
        compiler_params=_cparams(2),
        name="attn_prompt",
    )(sinks, q, kv, kv, g, head, x2d, bias, wout_bf16)


SAMPLE_SB = 8
SAMPLE_KEYS = 2 * BLOCK


def _attn_sample_kernel(sinks_ref, q_ref, kvn_ref, g_ref, h_ref, ck_ref, cv_ref, bias0_ref, bias1_ref,
                        wout_ref, out_ref, nk_ref, nv_ref, og_ref, *, t_new):
    keep = ck_ref.shape[1]
    lo = _iota((1, LANES), 1) < A_HEAD_DIM
    stack = 8 * SUBLANES
    own = (_iota((stack, 1), 0) & (SUBLANES - 1)) // t_new
    piece = _iota((stack, 1), 0) // SUBLANES
    bias_refs = (bias0_ref, bias1_ref)
    n_tile = A_KV_WIDTH // LANES
    pair_ids = range(SAMPLE_SB * t_new // SUBLANES)

    bias_c, bias_n, sink = [], [], []
    for j in range(n_tile):
        heads = slice(8 * j, 8 * j + 8)
        bias_c.append(bias0_ref[heads, :, :keep].reshape(stack, keep))
        bias_n.append([ref[heads, :, keep:keep + SUBLANES].reshape(stack, SUBLANES) for ref in bias_refs])
        col = jnp.zeros((stack, 1), F32)
        for gq in range(8):
            col = jnp.where(piece == gq, sinks_ref[8 * j + gq], col)
        sink.append(col)

    chains = [(p, j, s) for p in pair_ids for j in range(n_tile) for s in range(SUBLANES // t_new)]
    qs = {}
    for p in pair_ids:
        rows = slice(p * SUBLANES, (p + 1) * SUBLANES)
        for j in range(n_tile):
            parts = []
            for gq in range(8):
                t = q_ref[rows, (4 * j + gq // 2) * LANES:(4 * j + gq // 2 + 1) * LANES]
                want_lo = gq < 4
                if (gq % 2 == 0) != want_lo:
                    t = pltpu.roll(t, A_HEAD_DIM, 1)
                parts.append(jnp.where(lo, t, 0.0) if want_lo else jnp.where(lo, 0.0, t))
            qs[p, j] = jnp.concatenate(parts, axis=0).astype(BF16)

    def kv_tiles(p, j, s, base):
        seq = p * (SUBLANES // t_new) + s
        cache = (ck_ref if base == 0 else cv_ref)[seq][:, j * LANES:(j + 1) * LANES].astype(BF16)
        new = kvn_ref[p * SUBLANES:(p + 1) * SUBLANES, base + j * LANES:base + (j + 1) * LANES].astype(BF16)
        return cache, new

    sc, sn = [], []
    for p, j, s in chains:
        kc, kn = kv_tiles(p, j, s, 0)
        sc.append(_mm_nt(qs[p, j], kc) + bias_c[j])
        sn.append(_mm_nt(qs[p, j], kn) + bias_n[j][s])
    mx = [jnp.maximum(jnp.maximum(jnp.max(sc[i], axis=1, keepdims=True),
                                  jnp.max(sn[i], axis=1, keepdims=True)), sink[chains[i][1]])
          for i in range(len(chains))]
    pc = [jnp.exp(sc[i] - mx[i]) for i in range(len(chains))]
    pn = [jnp.exp(sn[i] - mx[i]) for i in range(len(chains))]
    den = [jnp.sum(pc[i], axis=1, keepdims=True) + jnp.sum(pn[i], axis=1, keepdims=True)
           + jnp.exp(sink[chains[i][1]] - mx[i]) for i in range(len(chains))]
    outs = {}
    for i, (p, j, s) in enumerate(chains):
        vc, vn = kv_tiles(p, j, s, A_KV_WIDTH)
        o = (jnp.dot(pc[i].astype(BF16), vc, preferred_element_type=F32)
             + jnp.dot(pn[i].astype(BF16), vn, preferred_element_type=F32)) * (1.0 / den[i])
        outs[p, j] = o if s == 0 else jnp.where(own == s, o, outs[p, j])
    for p in pair_ids:
        rows = slice(p * SUBLANES, (p + 1) * SUBLANES)
        for j in range(n_tile):
            o = outs[p, j]
            for gg in range(4):
                even = o[2 * gg * SUBLANES:(2 * gg + 1) * SUBLANES]
                odd = o[(2 * gg + 1) * SUBLANES:(2 * gg + 2) * SUBLANES]
                if gg < 2:
                    tile = jnp.where(lo, even, pltpu.roll(odd, A_HEAD_DIM, 1))
                else:
                    tile = jnp.where(lo, pltpu.roll(even, A_HEAD_DIM, 1), odd)
                sl = slice((4 * j + gg) * LANES, (4 * j + gg + 1) * LANES)
                gt = g_ref[rows, sl]
                og_ref[rows, sl] = tile * (gt * _sigmoid(gt))
    for seq in range(SAMPLE_SB):
        new = kvn_ref[seq * t_new:(seq + 1) * t_new, :]
        nk_ref[seq, :keep - t_new, :] = ck_ref[seq, t_new:, :]
        nk_ref[seq, keep - t_new:, :] = new[:, :A_KV_WIDTH]
        nv_ref[seq, :keep - t_new, :] = cv_ref[seq, t_new:, :]
        nv_ref[seq, keep - t_new:, :] = new[:, A_KV_WIDTH:]
    out_ref[...] = h_ref[...] + jnp.dot(og_ref[...].astype(BF16), wout_ref[...],
                                        preferred_element_type=F32)


def _attn_sample_call(sinks, q, kv, g, h, cache_k, cache_v, bias0, bias1, wout_bf16, t_new):
    nseq, keep = cache_k.shape[0], cache_k.shape[1]
    rows = SAMPLE_SB * t_new
    row = lambda i: (i, 0)
    cspec = pl.BlockSpec((SAMPLE_SB, keep, A_KV_WIDTH), lambda i: (i, 0, 0))
    bspec = pl.BlockSpec((A_HEADS, SUBLANES, SAMPLE_KEYS), lambda i: (0, 0, 0))
    return pl.pallas_call(
        functools.partial(_attn_sample_kernel, t_new=t_new),
        grid=(nseq // SAMPLE_SB,),
        in_specs=[pl.BlockSpec(memory_space=pltpu.SMEM),
                  pl.BlockSpec((rows, A_WIDTH), row),
                  pl.BlockSpec((rows, 2 * A_KV_WIDTH), row),
                  pl.BlockSpec((rows, A_WIDTH), row),
                  pl.BlockSpec((rows, D_MODEL), row),
                  cspec, cspec, bspec, bspec,
                  pl.BlockSpec((A_WIDTH, D_MODEL), lambda i: (0, 0))],
        out_specs=[pl.BlockSpec((rows, D_MODEL), row), cspec, cspec],
        out_shape=[jax.ShapeDtypeStruct((nseq * t_new, D_MODEL), F32),
                   jax.ShapeDtypeStruct(cache_k.shape, F32),
                   jax.ShapeDtypeStruct(cache_v.shape, F32)],
        scratch_shapes=[pltpu.VMEM((rows, A_WIDTH), F32)],
        compiler_params=_cparams(1),
        name="attn_sample",
    )(sinks, q, kv, g, h, cache_k, cache_v, bias0, bias1, wout_bf16)


def _rwkv_proj_kernel(h_ref, shift_ref, gain_ref, mu_ref, win_ref, w0_ref, w1_ref, w2_ref,
                      a0_ref, a1_ref, a2_ref,
                      r_ref, k_ref, v_ref, g_ref, ld_ref, a_ref, xn_ref, *scratch, seq_len):
    xn = _rmsnorm(h_ref[...], gain_ref[...])
    tm = xn.shape[0]
    rolled = pltpu.roll(xn, 1, 0)
    row = _iota((tm, 1), 0)
    if seq_len is None:
        carry_ref, = scratch

        @pl.when(pl.program_id(1) == 0)
        def _():
            carry_ref[...] = shift_ref[0]

        xprev = jnp.where(row == 0, carry_ref[...], rolled)
        carry_ref[...] = xn[tm - 1:tm, :]
        xn_ref[0] = xn[tm - 1:tm, :]
    else:
        xprev = jnp.where(row % seq_len == 0, shift_ref[...], rolled)
        xn_ref[...] = xn
    dx = xprev - xn

    def mix(c):
        return (xn + dx * mu_ref[c:c + 1, :]).astype(BF16)

    for c, o_ref in enumerate((r_ref, k_ref, v_ref, g_ref)):
        o_ref[...] = jnp.dot(mix(c), win_ref[c], preferred_element_type=F32)
    lw = jnp.tanh(jnp.dot(mix(4), w1_ref[...], preferred_element_type=F32))
    z = w0_ref[...] + jnp.dot(lw.astype(BF16), w2_ref[...], preferred_element_type=F32)
    ld_ref[...] = -math.exp(-0.5) * _sigmoid(z)
    la = jnp.dot(mix(5), a1_ref[...], preferred_element_type=F32)
    a_ref[...] = _sigmoid(a0_ref[...] + jnp.dot(la.astype(BF16), a2_ref[...],
                                                preferred_element_type=F32))


def _rwkv_proj_call(h, shift, p, nbatch, ntile, tm, seq_len):
    n = h.shape[0]
    row = lambda b, i: (b * ntile + i, 0)
    full2 = lambda b, i: (0, 0)
    if seq_len is None:
        shift_spec = pl.BlockSpec((1, 1, D_MODEL), lambda b, i: (b, 0, 0))
        xn_spec = pl.BlockSpec((1, 1, D_MODEL), lambda b, i: (b, 0, 0))
        xn_shape = jax.ShapeDtypeStruct((nbatch, 1, D_MODEL), F32)
        scratch = [pltpu.VMEM((1, D_MODEL), F32)]
    else:
        shift_spec = pl.BlockSpec((tm, D_MODEL), row)
        xn_spec = pl.BlockSpec((tm, D_MODEL), row)
        xn_shape = jax.ShapeDtypeStruct((n, D_MODEL), F32)
        scratch = []
    lora = p["w1"].shape[1]
    big = jax.ShapeDtypeStruct((n, D_MODEL), F32)
    return pl.pallas_call(
        functools.partial(_rwkv_proj_kernel, seq_len=seq_len),
        grid=(nbatch, ntile),
        in_specs=[pl.BlockSpec((tm, D_MODEL), row),
                  shift_spec,
                  pl.BlockSpec((1, D_MODEL), full2),
                  pl.BlockSpec(p["mu"].shape, full2),
                  pl.BlockSpec(p["w_in"].shape, lambda b, i: (0, 0, 0)),
                  pl.BlockSpec((1, D_MODEL), full2),
                  pl.BlockSpec((D_MODEL, lora), full2),
                  pl.BlockSpec((lora, D_MODEL), full2),
                  pl.BlockSpec((1, D_MODEL), full2),
                  pl.BlockSpec((D_MODEL, lora), full2),
                  pl.BlockSpec((lora, D_MODEL), full2)],
        out_specs=[pl.BlockSpec((tm, D_MODEL), row)] * 6 + [xn_spec],
        out_shape=[big] * 6 + [xn_shape],
        scratch_shapes=scratch,
        compiler_params=_cparams(2),
        name="rwkv_proj",
    )(h, shift, p["gain"], p["mu"], p["w_in"], p["w0"], p["w1"], p["w2"], p["a0"], p["a1"], p["a2"])


NN = (((1,), (0,)), ((), ()))
NT = (((1,), (1,)), ((), ()))
TN = (((0,), (0,)), ((), ()))


def _split(x):
    hi = x.astype(BF16)
    return hi, (x - hi.astype(F32)).astype(BF16)


def _mm(a, b, dn):
    return lax.dot_general(a, b, dn, preferred_element_type=F32)


def _mm3s(a, b, dn):
    (ah, al), (bh, bl) = a, b
    return _mm(al, bh, dn) + _mm(ah, bl, dn) + _mm(ah, bh, dn)


def _mm3(a, b, dn):
    return _mm3s(_split(a), _split(b), dn)


def _mm3_top(x_hi, x_lo_top, b, dn):
    bh, bl = b
    c = x_lo_top.shape[0]
    full = _mm(x_hi, bh, dn)
    return full[:c] + _mm(x_lo_top, bh, dn) + _mm(x_hi[:c], bl, dn), full[c:]


def _seg_sum(x, ones_bf16):
    hi, lo = _split(x)
    return _mm(lo, ones_bf16, NN) + _mm(hi, ones_bf16, NN)


def _lane_lo():
    return _iota((1, LANES), 1) < R_HEAD_DIM


def _same_half():
    return (_iota((LANES, LANES), 0) < R_HEAD_DIM) == (_iota((LANES, LANES), 1) < R_HEAD_DIM)


def _ones_blk():
    return _same_half().astype(BF16)


def _bd(z):
    lo = _lane_lo()
    return jnp.concatenate([jnp.where(lo, z, 0.0), jnp.where(lo, 0.0, z)], axis=0)


def _bd_swap(z):
    lo = _lane_lo()
    return jnp.concatenate([jnp.where(lo, 0.0, z), jnp.where(lo, z, 0.0)], axis=0)


def _split_map(z, f):
    hi, lw = _split(z)
    return f(hi), f(lw)


def _wkv_batch_stage(r, k, v, a, ld, kkg, kag, fillers=(), early=None):
    c = r[0].shape[0]
    pairs = range(len(r))
    half = R_HEAD_DIM
    lo = _lane_lo()
    t_row, t_col = _iota((c, LANES), 0), _iota((c, LANES), 1) & (half - 1)
    strict, incl = t_row > t_col, t_row >= t_col
    tri = (_iota((c, c), 0) >= _iota((c, c), 1)).astype(BF16)
    ones_blk = _ones_blk()
    eye_pair = (_iota((c, LANES), 0) == (_iota((c, LANES), 1) & (half - 1))).astype(F32)

    kkx = [k[j] * kkg[j] for j in pairs]
    ssq = [_seg_sum(kkx[j] * kkx[j], ones_blk) for j in pairs]
    ld_s = [_split(ld[j]) for j in pairs]
    cs = [_mm(tri, ld_s[j][1], NN) + _mm(tri, ld_s[j][0], NN) for j in pairs]
    tot = [cs[j][c - 1:c, :] for j in pairs]
    x_hi, at_lo, kh_all, bke, vbd_s, lk_a, lk_r, lb_a, lb_r = ([] for _ in range(9))
    for j in pairs:
        kk = kkx[j] / jnp.maximum(jnp.sqrt(ssq[j]), 1e-12)
        kh = k[j] * (1.0 + (a[j] - 1.0) * kag[j])
        bv = kk * a[j]
        e_neg = jnp.exp(-cs[j])
        e_end = jnp.exp(tot[j] - cs[j])
        at, rt = -kk * jnp.exp(cs[j] - ld[j]), r[j] * jnp.exp(cs[j])
        at_hi, at_l = _split(at)
        x_hi.append(jnp.concatenate([at_hi, rt.astype(BF16)], axis=0))
        at_lo.append(at_l)
        kh_all.append(kh)
        bt, kt = _split(bv * e_neg), _split(kh * e_neg)
        yn_s = tuple(jnp.concatenate([p, q], axis=0) for p, q in zip(bt, kt))
        ys_s = tuple(jnp.concatenate([q, p], axis=0) for p, q in zip(bt, kt))
        bke.append(jnp.concatenate([bv * e_end, kh * e_end], axis=0))
        vbd_s.append(_split_map(v[j], _bd))
        ga_a, ga_r = _mm3_top(jnp.where(lo, x_hi[j], 0.0), jnp.where(lo, at_l, 0.0), ys_s, NT)
        gb_a, gb_r = _mm3_top(jnp.where(lo, 0.0, x_hi[j]), jnp.where(lo, 0.0, at_l), yn_s, NT)
        lk_a.append(jnp.where(strict, jnp.where(lo, ga_a, gb_a), 0.0))
        lk_r.append(jnp.where(incl, jnp.where(lo, ga_r, gb_r), 0.0))
        lb_a.append(jnp.where(strict, jnp.where(lo, gb_a, ga_a), 0.0))
        lb_r.append(jnp.where(incl, jnp.where(lo, gb_r, ga_r), 0.0))
    fillers = list(fillers)
    if early is not None:
        early.update(x_hi=x_hi, at_lo=at_lo)
    pw = lb_a
    acc = [eye_pair + pw[j] for j in pairs]
    pw = [_mm(pw[j].astype(BF16), _bd(pw[j].astype(BF16)), NN) for j in pairs]
    from_v = []
    for j in pairs:
        lk_hi, lk_lo = _split(lk_a[j])
        from_v.append(_mm3_top(jnp.concatenate([lk_hi, lk_r[j].astype(BF16)], axis=0), lk_lo, vbd_s[j], NN))
    for _ in range(int(math.log2(c)) - 2):
        both = [_mm(jnp.concatenate([pw[j], acc[j]], axis=0).astype(BF16), _bd(pw[j].astype(BF16)), NN)
                for j in pairs]
        if fillers:
            fillers.pop(0)()
        pw = [both[j][:c] for j in pairs]
        acc = [acc[j] + both[j][c:] for j in pairs]
    tinv = [acc[j] + _mm(acc[j].astype(BF16), _bd(pw[j].astype(BF16)), NN) for j in pairs]
    for fill in fillers:
        fill()
    return dict(x_hi=x_hi, at_lo=at_lo, kh=kh_all, lb_r=lb_r, tinv=tinv, from_v=from_v, bke=bke,
                tot=tot, ones_blk=ones_blk)


def _wkv_finish(stage, from_state_a, from_state_r):
    pairs = range(len(from_state_a))
    u = [_mm3s(_split(stage["tinv"][j]),
               _split_map(from_state_a[j] + stage["from_v"][j][0], _bd_swap), NN) for j in pairs]
    y = [from_state_r[j] + stage["from_v"][j][1]
         + _mm(stage["lb_r"][j].astype(BF16), _bd_swap(u[j].astype(BF16)), NN) for j in pairs]
    return u, y


def _wkv_gate(y, r, kh, v, g, rk, lng, lnb, ones_blk):
    pairs = range(len(y))
    inv_n = 1.0 / R_HEAD_DIM

    def seg(t):
        return _mm(t.astype(BF16), ones_blk, NN)

    rkk = [seg(r[j] * kh[j] * rk[j]) for j in pairs]
    mean = [seg(y[j]) * inv_n for j in pairs]
    d = [y[j] - mean[j] for j in pairs]
    var = [seg(d[j] * d[j]) * inv_n for j in pairs]
    return [((d[j] * lax.rsqrt(var[j] + GN_EPS) * lng[j] + lnb[j] + rkk[j] * v[j])
             * (g[j] * _sigmoid(g[j]))).astype(BF16) for j in pairs]


def _wkv_chunk_kernel(r_ref, k_ref, v_ref, a_ref, ld_ref, g_ref, kkg_ref, kag_ref, rk_ref, lng_ref, lnb_ref,
                      z_ref, sout_ref, st_ref):
    c = WKV_CHUNK
    n_sub = r_ref.shape[0] // c
    npair = r_ref.shape[1] // LANES
    pairs = range(npair)
    items = [(s, j) for s in range(n_sub) for j in pairs]

    @pl.when(pl.program_id(2) == 0)
    def _():
        st_ref[...] = jnp.zeros_like(st_ref)

    def tile(ref, it):
        s, j = it
        return ref[s * c:(s + 1) * c, j * LANES:(j + 1) * LANES]

    def par(ref):
        return [ref[:, j * LANES:(j + 1) * LANES] for _, j in items]

    r, k, v, a, ld, g = ([tile(ref, it) for it in items] for ref in (r_ref, k_ref, v_ref, a_ref, ld_ref, g_ref))
    st = [st_ref[j] for j in pairs]
    from_state = {}
    stage = {}

    def state_products(group):
        def run():
            for j in group:
                from_state[j] = _mm3_top(stage["x_hi"][j], stage["at_lo"][j], _split(st[j]), NN)
        return run

    n_fill = 4
    groups = [list(pairs)[i::n_fill] for i in range(n_fill)]
    stage.update(_wkv_batch_stage(r, k, v, a, ld, par(kkg_ref), par(kag_ref),
                                  fillers=[state_products(grp) for grp in groups if grp], early=stage))
    same_half = _same_half()
    eye_full = _iota((LANES, LANES), 0) == _iota((LANES, LANES), 1)
    ys = []
    for s in range(n_sub):
        idx = [s * npair + j for j in pairs]
        if s > 0:
            for j in pairs:
                from_state[j] = _mm3_top(stage["x_hi"][idx[j]], stage["at_lo"][idx[j]], _split(st[j]), NN)
        sub = {key: [stage[key][i] for i in idx] for key in ("tinv", "from_v", "lb_r")}
        us, ys_s = _wkv_finish(sub, [from_state[j][0] for j in pairs], [from_state[j][1] for j in pairs])
        upd = [_mm3(stage["bke"][idx[j]], jnp.concatenate([us[j], v[idx[j]]], axis=0), TN) for j in pairs]
        ys += ys_s
        for j in pairs:
            w_col = jnp.sum(jnp.where(eye_full, jnp.exp(stage["tot"][idx[j]]), 0.0), axis=1, keepdims=True)
            st[j] = w_col * st[j] + jnp.where(same_half, upd[j], 0.0)
    zs = _wkv_gate(ys, r, stage["kh"], v, g, par(rk_ref), par(lng_ref), par(lnb_ref), stage["ones_blk"])
    for i, (s, j) in enumerate(items):
        z_ref[s * c:(s + 1) * c, j * LANES:(j + 1) * LANES] = zs[i]
    for j in pairs:
        st_ref[j] = st[j]
        sout_ref[0, j] = st[j]


def _wkv_chunk_call(r, k, v, a, ld, g, p, nbatch, seq):
    rows = WKV_CHUNK * WKV_CHUNKS_PER_STEP
    nstep = seq // rows
    npair = D_MODEL // LANES
    tile = pl.BlockSpec((rows, D_MODEL), lambda b, j, t: (b * nstep + t, 0))
    par = pl.BlockSpec((1, D_MODEL), lambda b, j, t: (0, 0))
    return pl.pallas_call(
        _wkv_chunk_kernel,
        grid=(nbatch, 1, nstep),
        in_specs=[tile] * 6 + [par] * 5,
        out_specs=[tile, pl.BlockSpec((1, npair, LANES, LANES), lambda b, j, t: (b, 0, 0, 0))],
        out_shape=[jax.ShapeDtypeStruct((nbatch * seq, D_MODEL), BF16),
                   jax.ShapeDtypeStruct((nbatch, npair, LANES, LANES), F32)],
        scratch_shapes=[pltpu.VMEM((npair, LANES, LANES), F32)],
        compiler_params=_cparams(3),
        name="wkv_chunk",
    )(r, k, v, a, ld, g, p["k_k"], p["k_a"], p["r_k"], p["ln_g"], p["ln_b"])


WKV_LANES_UNROLL = 4


def _wkv_lanes_kernel(r_ref, k_ref, v_ref, a_ref, ld_ref, g_ref, kkg_ref, kag_ref, rk_ref, lng_ref, lnb_ref,
                      s_ref, z_ref, sout_ref, prep_ref, y_ref, *, seq_len):
    n = R_HEAD_DIM
    nseq = s_ref.shape[3]
    eye = _iota((LANES, LANES), 0) == _iota((LANES, LANES), 1)

    def column(ref):
        return jnp.sum(jnp.where(eye, ref[...], 0.0), axis=1, keepdims=True)

    kkg, kag, rk, lng, lnb = (column(ref) for ref in (kkg_ref, kag_ref, rk_ref, lng_ref, lnb_ref))

    def token(ref, t):
        return ref[pl.ds(t, nseq, stride=seq_len), :].T

    bonus = []
    for t in range(seq_len):
        r, k, v, a = (token(ref, t) for ref in (r_ref, k_ref, v_ref, a_ref))
        w = jnp.exp(token(ld_ref, t))
        kkx = k * kkg
        kh = k * (1.0 + (a - 1.0) * kag)
        rkk = r * kh * rk
        tiles = []
        for hh in range(2):
            rows = slice(hh * n, (hh + 1) * n)
            nrm = jnp.sqrt(jnp.sum(kkx[rows] * kkx[rows], axis=0, keepdims=True))
            kk = kkx[rows] / jnp.maximum(nrm, 1e-12)
            for q, val in enumerate((w[rows], -kk, kk * a[rows], kh[rows], r[rows], v[rows])):
                prep_ref[q, t, hh] = val
            tiles.append(jnp.sum(rkk[rows], axis=0, keepdims=True) * v[rows])
        bonus.append(tiles)

    for hh in range(2):
        def advance(i, carry, hh=hh):
            for u in range(WKV_LANES_UNROLL):
                vi = i * WKV_LANES_UNROLL + u
                slab = s_ref[hh, vi]
                for t in range(seq_len):
                    w, av, bv, kh, r = (prep_ref[q, t, hh] for q in range(5))
                    vrow = prep_ref[5, t, hh, pl.ds(vi, 1), :]
                    sa = jnp.sum(slab * av, axis=0, keepdims=True)
                    slab = slab * w + sa * bv + vrow * kh
                    y_ref[t, hh, pl.ds(vi, 1), :] = jnp.sum(slab * r, axis=0, keepdims=True)
                sout_ref[hh, vi] = slab
            return carry

        lax.fori_loop(0, n // WKV_LANES_UNROLL, advance, 0)

    for t in range(seq_len):
        parts = []
        for hh in range(2):
            rows = slice(hh * n, (hh + 1) * n)
            y = y_ref[t, hh]
            d = y - jnp.mean(y, axis=0, keepdims=True)
            var = jnp.mean(d * d, axis=0, keepdims=True)
            parts.append(d * lax.rsqrt(var + GN_EPS) * lng[rows] + lnb[rows] + bonus[t][hh])
        g = token(g_ref, t)
        z = jnp.concatenate(parts, axis=0) * (g * _sigmoid(g))
        z_ref[pl.ds(t, nseq, stride=seq_len), :] = z.T


def _wkv_lanes_call(r, k, v, a, ld, g, p, state_hvkb, seq_len):
    n = r.shape[0]
    nseq = state_hvkb.shape[3]
    tile = pl.BlockSpec((n, LANES), lambda j: (0, j))
    par = pl.BlockSpec((1, LANES), lambda j: (0, j))
    sspec = pl.BlockSpec((2, R_HEAD_DIM, R_HEAD_DIM, nseq), lambda j: (j, 0, 0, 0))
    return pl.pallas_call(
        functools.partial(_wkv_lanes_kernel, seq_len=seq_len),
        grid=(D_MODEL // LANES,),
        in_specs=[tile] * 6 + [par] * 5 + [sspec],
        out_specs=[tile, sspec],
        out_shape=[jax.ShapeDtypeStruct((n, D_MODEL), F32),
                   jax.ShapeDtypeStruct(state_hvkb.shape, F32)],
        scratch_shapes=[pltpu.VMEM((6, seq_len, 2, R_HEAD_DIM, nseq), F32),
                        pltpu.VMEM((seq_len, 2, R_HEAD_DIM, nseq), F32)],
        compiler_params=_cparams(1),
        name="wkv_lanes",
    )(r, k, v, a, ld, g, p["k_k"], p["k_a"], p["r_k"], p["ln_g"], p["ln_b"], state_hvkb)


RWKV_OUT_PIECES = 4


def _rwkv_out_kernel(*refs):
    n = RWKV_OUT_PIECES
    z_refs, h_refs, (wout_ref, fg_ref, out_ref) = refs[:n], refs[n:2 * n], refs[2 * n:]
    z = jnp.concatenate([ref[...].astype(BF16) for ref in z_refs], axis=0)
    h = jnp.concatenate([ref[...] for ref in h_refs], axis=0)
    h2 = h + jnp.dot(z, wout_ref[...], preferred_element_type=F32)
    out_ref[...] = _rmsnorm(h2, fg_ref[...])


def _rwkv_out_call(z, h, p, nbatch, ntile, tm, pieces_per_batch, skip):
    piece = tm // RWKV_OUT_PIECES
    dst = lambda b, i: (b * ntile + i, 0)
    pspec = [pl.BlockSpec((piece, D_MODEL),
                          functools.partial(lambda b, i, kk: (b * pieces_per_batch + i * RWKV_OUT_PIECES + skip + kk, 0),
                                            kk=kk))
             for kk in range(RWKV_OUT_PIECES)]
    return pl.pallas_call(
        _rwkv_out_kernel,
        grid=(nbatch, ntile),
        in_specs=pspec + pspec + [pl.BlockSpec((D_MODEL, D_MODEL), lambda b, i: (0, 0)),
                                  pl.BlockSpec((1, D_MODEL), lambda b, i: (0, 0))],
        out_specs=pl.BlockSpec((tm, D_MODEL), dst),
        out_shape=jax.ShapeDtypeStruct((nbatch * ntile * tm, D_MODEL), F32),
        compiler_params=_cparams(2),
        name="rwkv_out",
    )(*([z] * RWKV_OUT_PIECES), *([h] * RWKV_OUT_PIECES), p["w_out"], p["final_gain"])


def _prompt_bucket():
    assert WINDOW == BLOCK
    rel = (np.arange(BLOCK)[:, None] - np.arange(BLOCK)[None, :]) % BLOCK
    return _t5_bucket_np(rel)


def _sample_bucket(keep, t_new, slot):
    t = (np.arange(SUBLANES) % t_new)[:, None]
    j = np.arange(SAMPLE_KEYS)[None, :]
    own = j - keep - slot * t_new
    rel = np.where(j < keep, keep + t - j, t - own)
    ok = (rel >= 0) & (rel < WINDOW) & ((j < keep) | ((own >= 0) & (own < t_new)))
    return np.where(ok, _t5_bucket_np(rel), -1).astype(np.int32)


def kernel(x_prompt, x_sample, cache_win_k, cache_win_v, state_wkv, state_shift, meta_tokens, rel_bias_table, norm_gain, final_gain, attn_w_in, attn_sinks, attn_w_out, rwkv_mu, rwkv_w_in, rwkv_w0, rwkv_w1, rwkv_w2, rwkv_a0, rwkv_a1, rwkv_a2, rwkv_k_k, rwkv_k_a, rwkv_r_k, rwkv_ln_gamma, rwkv_ln_beta, rwkv_w_out):
    nb, seq, _ = x_prompt.shape
    ns, t_new, _ = x_sample.shape
    keep = cache_win_k.shape[2]
    lp = seq + BLOCK
    nblk = lp // BLOCK
    row = lambda x: x.reshape(1, D_MODEL)

    w_in0 = attn_w_in[0].astype(BF16)
    w_out0 = attn_w_out[0].astype(BF16)
    gain0 = row(norm_gain[0])
    sinks = attn_sinks[0]
    rp = dict(gain=row(norm_gain[1]), mu=rwkv_mu[0], w_in=rwkv_w_in[0].astype(BF16),
              w0=row(rwkv_w0[0]), w1=rwkv_w1[0].astype(BF16), w2=rwkv_w2[0].astype(BF16),
              a0=row(rwkv_a0[0]), a1=rwkv_a1[0].astype(BF16), a2=rwkv_a2[0].astype(BF16),
              k_k=row(rwkv_k_k[0]), k_a=row(rwkv_k_a[0]), r_k=row(rwkv_r_k[0]), ln_g=row(rwkv_ln_gamma[0]),
              ln_b=row(rwkv_ln_beta[0]), w_out=rwkv_w_out[0].astype(BF16),
              final_gain=row(final_gain))

    bias_p, *bias_s = _bias_call(rel_bias_table, [_prompt_bucket()]
                                 + [_sample_bucket(keep, t_new, slot) for slot in range(2)])

    head = jnp.concatenate([jnp.zeros((PAD, D_MODEL), F32), meta_tokens.astype(F32)], axis=0)
    xp = x_prompt.reshape(nb * seq, D_MODEL)
    q, kv, g = _attn_proj_call(xp, head, gain0, w_in0, BF16, nb, lp // ATTN_PROJ_ROWS,
                               ATTN_PROJ_ROWS // BLOCK, BLOCK)
    h1 = _attn_prompt_call(sinks, q, kv, g, head, xp, bias_p, w_out0, nb, nblk)
    kv3 = kv.reshape(nb, lp, 2 * A_KV_WIDTH)[:, lp - WINDOW:, :]
    win_k_p = kv3[:, :, :A_KV_WIDTH].reshape(1, nb, WINDOW, A_KV_HEADS, A_HEAD_DIM)
    win_v_p = kv3[:, :, A_KV_WIDTH:].reshape(1, nb, WINDOW, A_KV_HEADS, A_HEAD_DIM)

    shift0 = jnp.zeros((nb, 1, D_MODEL), F32)
    r, k, v, g1, ld, a, xlast = _rwkv_proj_call(h1, shift0, rp, nb, lp // RWKV_PROJ_ROWS, RWKV_PROJ_ROWS, None)
    z, st = _wkv_chunk_call(r, k, v, a, ld, g1, rp, nb, lp)
    y_prompt = _rwkv_out_call(z, h1, rp, nb, seq // RWKV_OUT_ROWS, RWKV_OUT_ROWS,
                              lp * RWKV_OUT_PIECES // RWKV_OUT_ROWS, BLOCK * RWKV_OUT_PIECES // RWKV_OUT_ROWS)
    y_prompt = y_prompt.reshape(nb, seq, D_MODEL)
    st = st.reshape(nb, D_MODEL // LANES, 2, R_HEAD_DIM, 2, R_HEAD_DIM)
    st = jnp.stack([st[:, :, 0, :, 0, :], st[:, :, 1, :, 1, :]], axis=2)
    wkv_p = jnp.swapaxes(st, -1, -2).reshape(1, nb, R_HEADS, R_HEAD_DIM, R_HEAD_DIM)
    shift_p = xlast.reshape(1, nb, D_MODEL)

    xs = x_sample.reshape(ns * t_new, D_MODEL)
    qs, kvs, gs = _attn_proj_call(xs, None, gain0, w_in0, F32, 1, 1, 1, ns * t_new)
    ck = cache_win_k[0].reshape(ns, keep, A_KV_WIDTH)
    cv = cache_win_v[0].reshape(ns, keep, A_KV_WIDTH)
    h1s, nk, nv = _attn_sample_call(sinks, qs, kvs, gs, xs, ck, cv, bias_s[0], bias_s[1], w_out0, t_new)
    win_k_s = nk.reshape(1, ns, keep, A_KV_HEADS, A_HEAD_DIM)
    win_v_s = nv.reshape(1, ns, keep, A_KV_HEADS, A_HEAD_DIM)

    shift_rows = jnp.repeat(state_shift[0], t_new, axis=0)
    tms = 256
    rs, ks, vs, g1s, lds, as_, xns = _rwkv_proj_call(h1s, shift_rows, rp, 1, ns * t_new // tms, tms, t_new)
    zs, st_s = _wkv_lanes_call(rs, ks, vs, as_, lds, g1s, rp, jnp.transpose(state_wkv[0], (1, 2, 3, 0)), t_new)
    y_sample = _rwkv_out_call(zs, h1s, rp, 1, ns * t_new // RWKV_OUT_ROWS, RWKV_OUT_ROWS,
                              ns * t_new * RWKV_OUT_PIECES // RWKV_OUT_ROWS, 0)
    y_sample = y_sample.reshape(ns, t_new, D_MODEL)
    wkv_s = jnp.transpose(st_s, (3, 0, 1, 2))[None]
    shift_s = xns.reshape(ns, t_new, D_MODEL)[:, t_new - 1][None]

    return (y_prompt, y_sample, win_k_p, win_v_p, wkv_p, shift_p, win_k_s, win_v_s, wkv_s, shift_s)
```

```python
import functools
import math

import numpy as np
import jax
import jax.numpy as jnp
from jax import lax
from jax.experimental import pallas as pl
from jax.experimental.pallas import tpu as pltpu

F32 = jnp.float32
BF16 = jnp.bfloat16

D_MODEL = 1024
N_META = 16
RMS_EPS = 1e-6
A_HEADS = 16
A_KV_HEADS = 4
A_HEAD_DIM = 64
A_WIDTH = A_HEADS * A_HEAD_DIM
A_KV_WIDTH = A_KV_HEADS * A_HEAD_DIM
WINDOW = 128
BLOCK = 128
N_BUCKETS = 32
MAX_DISTANCE = 128
R_HEAD_DIM = 64
R_HEADS = D_MODEL // R_HEAD_DIM
GN_EPS = 64e-5

LANES = 128
SUBLANES = 8
PAD = BLOCK - N_META
NEG = -1e30
WKV_CHUNK = 64
WKV_CHUNKS_PER_STEP = 3
ATTN_PROJ_ROWS = 384
RWKV_PROJ_ROWS = 384
RWKV_OUT_ROWS = 512
VMEM_LIMIT = 56 * 1024 * 1024


def _cparams(n_axes):
    return pltpu.CompilerParams(dimension_semantics=("arbitrary",) * n_axes,
                                vmem_limit_bytes=VMEM_LIMIT)


def _rmsnorm(x, gain):
    return x * lax.rsqrt(jnp.mean(x * x, axis=-1, keepdims=True) + RMS_EPS) * gain


def _sigmoid(x):
    return 1.0 / (1.0 + jnp.exp(-x))


def _iota(shape, dim):
    return lax.broadcasted_iota(jnp.int32, shape, dim)


def _t5_bucket_np(rel):
    n = np.maximum(rel, 0)
    max_exact = N_BUCKETS // 2
    nf = np.maximum(n, max_exact).astype(np.float32)
    scale = np.float32(math.log(MAX_DISTANCE / max_exact))
    large = max_exact + (np.log(nf / np.float32(max_exact)) / scale
                         * np.float32(N_BUCKETS - max_exact)).astype(np.int32)
    large = np.minimum(large, N_BUCKETS - 1)
    return np.where(n < max_exact, n, large).astype(np.int32)


def _bias_kernel(table_ref, *refs):
    h = pl.program_id(0)
    n = len(refs) // 2
    for bucket_ref, out_ref in zip(refs[:n], refs[n:]):
        bk = bucket_ref[...]
        acc = jnp.full(bk.shape, NEG, F32)
        for b in range(N_BUCKETS):
            acc = jnp.where(bk == b, table_ref[b, h], acc)
        out_ref[0] = acc


def _bias_call(table, buckets_np):
    return pl.pallas_call(
        _bias_kernel,
        grid=(A_HEADS,),
        in_specs=[pl.BlockSpec(memory_space=pltpu.SMEM)]
                 + [pl.BlockSpec(bk.shape, lambda h: (0, 0)) for bk in buckets_np],
        out_specs=[pl.BlockSpec((1,) + bk.shape, lambda h: (h, 0, 0)) for bk in buckets_np],
        out_shape=[jax.ShapeDtypeStruct((A_HEADS,) + bk.shape, F32) for bk in buckets_np],
        compiler_params=_cparams(1),
        name="bias_expand",
    )(table, *(jnp.asarray(bk) for bk in buckets_np))


def _attn_proj_kernel(head_ref, *refs, n_piece):
    x_refs, (gain_ref, w_ref, q_ref, kv_ref, g_ref) = refs[:n_piece], refs[n_piece:]
    first = x_refs[0][...]
    if head_ref is not None:
        first = jnp.where(pl.program_id(1) == 0, head_ref[...], first)
    x = jnp.concatenate([first] + [ref[...] for ref in x_refs[1:]], axis=0)
    xn = _rmsnorm(x, gain_ref[...])
    proj = jnp.dot(xn.astype(BF16), w_ref[...], preferred_element_type=F32)
    q_ref[...] = (proj[:, :A_WIDTH] * (A_HEAD_DIM ** -0.5)).astype(q_ref.dtype)
    kv_ref[...] = proj[:, A_WIDTH:A_WIDTH + 2 * A_KV_WIDTH]
    g_ref[...] = proj[:, A_WIDTH + 2 * A_KV_WIDTH:]


def _attn_proj_call(x2d, head, gain, w_bf16, q_dtype, nbatch, ntile, n_piece, piece):
    tm = n_piece * piece
    wcols = w_bf16.shape[1]
    per_seq = x2d.shape[0] // (nbatch * piece)
    lead = 0 if head is None else 1
    dst = lambda b, i: (b * ntile + i, 0)
    xspec = [pl.BlockSpec((piece, D_MODEL),
                          functools.partial(lambda b, i, kk: (b * per_seq + jnp.maximum(i * n_piece + kk - lead, 0), 0),
                                            kk=kk))
             for kk in range(n_piece)]
    kern = functools.partial(_attn_proj_kernel, n_piece=n_piece)
    operands = [x2d] * n_piece + [gain, w_bf16]
    if head is None:
        kern = functools.partial(kern, None)
        head_spec = []
    else:
        head_spec = [pl.BlockSpec((piece, D_MODEL), lambda b, i: (0, 0))]
        operands = [head] + operands
    n = nbatch * ntile * tm
    return pl.pallas_call(
        kern,
        grid=(nbatch, ntile),
        in_specs=head_spec + xspec + [pl.BlockSpec((1, D_MODEL), lambda b, i: (0, 0)),
                                      pl.BlockSpec((D_MODEL, wcols), lambda b, i: (0, 0))],
        out_specs=[pl.BlockSpec((tm, A_WIDTH), dst),
                   pl.BlockSpec((tm, 2 * A_KV_WIDTH), dst),
                   pl.BlockSpec((tm, A_WIDTH), dst)],
        out_shape=[jax.ShapeDtypeStruct((n, A_WIDTH), q_dtype),
                   jax.ShapeDtypeStruct((n, 2 * A_KV_WIDTH), F32),
                   jax.ShapeDtypeStruct((n, A_WIDTH), F32)],
        compiler_params=_cparams(2),
        name="attn_proj",
    )(*operands)


def _padded_kv_tiles(kv, c):
    lo = _iota((1, LANES), 1) < A_HEAD_DIM
    j = c // 2
    out = []
    for base in (0, A_KV_WIDTH):
        t = kv[:, base + j * LANES: base + (j + 1) * LANES]
        tr = pltpu.roll(t, A_HEAD_DIM, 1)
        if c % 2 == 0:
            even, odd = jnp.where(lo, t, 0.0), jnp.where(lo, 0.0, tr)
        else:
            even, odd = jnp.where(lo, tr, 0.0), jnp.where(lo, 0.0, t)
        out += [even.astype(BF16), odd.astype(BF16)]
    return out


def _mm_nt(a, b):
    return lax.dot_general(a, b, (((1,), (1,)), ((), ())), preferred_element_type=F32)


def _attn_prompt_kernel(sinks_ref, q_ref, kvc_ref, kvp_ref, g_ref, head_ref, x_ref, bias_ref, wout_ref,
                        out_ref, og_ref):
    i = pl.program_id(1)
    resid = jnp.where(i == 0, head_ref[...], x_ref[...])
    row, col = _iota((BLOCK, BLOCK), 0), _iota((BLOCK, BLOCK), 1)
    own = col <= row
    kvalid = (i - 1 + own.astype(jnp.int32)) * BLOCK + col >= PAD
    kvp, kvc = kvp_ref[...], kvc_ref[...]
    heads = range(A_HEADS)
    prev = [_padded_kv_tiles(kvp, c) for c in range(A_KV_HEADS)]
    cur = [_padded_kv_tiles(kvc, c) for c in range(A_KV_HEADS)]
    s = []
    for h in heads:
        mm, idx = h // 2, h % 2
        q_tile = q_ref[:, mm * LANES:(mm + 1) * LANES]
        sc = jnp.where(own, _mm_nt(q_tile, cur[mm // 2][idx]), _mm_nt(q_tile, prev[mm // 2][idx]))
        s.append(jnp.where(kvalid, sc + bias_ref[h], NEG))
    m = [jnp.maximum(jnp.max(s[h], axis=1, keepdims=True), sinks_ref[h]) for h in heads]
    p = [jnp.exp(s[h] - m[h]) for h in heads]
    den = [jnp.sum(p[h], axis=1, keepdims=True) + jnp.exp(sinks_ref[h] - m[h]) for h in heads]
    o = []
    for h in heads:
        c, idx = h // 4, h % 2
        pv = (jnp.dot(jnp.where(own, p[h], 0.0).astype(BF16), cur[c][2 + idx], preferred_element_type=F32)
              + jnp.dot(jnp.where(own, 0.0, p[h]).astype(BF16), prev[c][2 + idx], preferred_element_type=F32))
        o.append(pv * (1.0 / den[h]))
    for mm in range(A_HEADS // 2):
        sl = slice(mm * LANES, (mm + 1) * LANES)
        gt = g_ref[:, sl]
        og_ref[:, sl] = ((o[2 * mm] + o[2 * mm + 1]) * (gt * _sigmoid(gt))).astype(BF16)
    out_ref[...] = resid + jnp.dot(og_ref[...], wout_ref[...], preferred_element_type=F32)


def _attn_prompt_call(sinks, q, kv, g, head, x2d, bias, wout_bf16, nbatch, nblk):
    n = q.shape[0]
    row = lambda b, i: (b * nblk + i, 0)
    prev = lambda b, i: (b * nblk + jnp.maximum(i - 1, 0), 0)
    xrow = lambda b, i: (b * (nblk - 1) + jnp.maximum(i - 1, 0), 0)
    return pl.pallas_call(
        _attn_prompt_kernel,
        grid=(nbatch, nblk),
        in_specs=[pl.BlockSpec(memory_space=pltpu.SMEM),
                  pl.BlockSpec((BLOCK, A_WIDTH), row),
                  pl.BlockSpec((BLOCK, 2 * A_KV_WIDTH), row),
                  pl.BlockSpec((BLOCK, 2 * A_KV_WIDTH), prev),
                  pl.BlockSpec((BLOCK, A_WIDTH), row),
                  pl.BlockSpec((BLOCK, D_MODEL), lambda b, i: (0, 0)),
                  pl.BlockSpec((BLOCK, D_MODEL), xrow),
                  pl.BlockSpec((A_HEADS, BLOCK, BLOCK), lambda b, i: (0, 0, 0)),
                  pl.BlockSpec((A_WIDTH, D_MODEL), lambda b, i: (0, 0))],
        out_specs=pl.BlockSpec((BLOCK, D_MODEL), row),
        out_shape=jax.ShapeDtypeStruct((n, D_MODEL), F32),
        scratch_shapes=[pltpu.VMEM((BLOCK, A_WIDTH), BF16)],
        compiler_params=_cparams(2),
        name="attn_prompt",
    )(sinks, q, kv, kv, g, head, x2d, bias, wout_bf16)


SAMPLE_SB = 8
SAMPLE_KEYS = 2 * BLOCK


def _attn_sample_kernel(sinks_ref, q_ref, kvn_ref, g_ref, h_ref, ck_ref, cv_ref, bias0_ref, bias1_ref,
                        wout_ref, out_ref, nk_ref, nv_ref, og_ref, *, t_new):
    keep = ck_ref.shape[1]
    lo = _iota((1, LANES), 1) < A_HEAD_DIM
    stack = 8 * SUBLANES
    own = (_iota((stack, 1), 0) & (SUBLANES - 1)) // t_new
    piece = _iota((stack, 1), 0) // SUBLANES
    bias_refs = (bias0_ref, bias1_ref)
    n_tile = A_KV_WIDTH // LANES
    pair_ids = range(SAMPLE_SB * t_new // SUBLANES)

    bias_c, bias_n, sink = [], [], []
    for j in range(n_tile):
        heads = slice(8 * j, 8 * j + 8)
        bias_c.append(bias0_ref[heads, :, :keep].reshape(stack, keep))
        bias_n.append([ref[heads, :, keep:keep + SUBLANES].reshape(stack, SUBLANES) for ref in bias_refs])
        col = jnp.zeros((stack, 1), F32)
        for gq in range(8):
            col = jnp.where(piece == gq, sinks_ref[8 * j + gq], col)
        sink.append(col)

    chains = [(p, j, s) for p in pair_ids for j in range(n_tile) for s in range(SUBLANES // t_new)]
    qs = {}
    for p in pair_ids:
        rows = slice(p * SUBLANES, (p + 1) * SUBLANES)
        for j in range(n_tile):
            parts = []
            for gq in range(8):
                t = q_ref[rows, (4 * j + gq // 2) * LANES:(4 * j + gq // 2 + 1) * LANES]
                want_lo = gq < 4
                if (gq % 2 == 0) != want_lo:
                    t = pltpu.roll(t, A_HEAD_DIM, 1)
                parts.append(jnp.where(lo, t, 0.0) if want_lo else jnp.where(lo, 0.0, t))
            qs[p, j] = jnp.concatenate(parts, axis=0).astype(BF16)

    def kv_tiles(p, j, s, base):
        seq = p * (SUBLANES // t_new) + s
        cache = (ck_ref if base == 0 else cv_ref)[seq][:, j * LANES:(j + 1) * LANES].astype(BF16)
        new = kvn_ref[p * SUBLANES:(p + 1) * SUBLANES, base + j * LANES:base + (j + 1) * LANES].astype(BF16)
        return cache, new

    sc, sn = [], []
    for p, j, s in chains:
        kc, kn = kv_tiles(p, j, s, 0)
        sc.append(_mm_nt(qs[p, j], kc) + bias_c[j])
        sn.append(_mm_nt(qs[p, j], kn) + bias_n[j][s])
    mx = [jnp.maximum(jnp.maximum(jnp.max(sc[i], axis=1, keepdims=True),
                                  jnp.max(sn[i], axis=1, keepdims=True)), sink[chains[i][1]])
          for i in range(len(chains))]
    pc = [jnp.exp(sc[i] - mx[i]) for i in range(len(chains))]
    pn = [jnp.exp(sn[i] - mx[i]) for i in range(len(chains))]
    den = [jnp.sum(pc[i], axis=1, keepdims=True) + jnp.sum(pn[i], axis=1, keepdims=True)
           + jnp.exp(sink[chains[i][1]] - mx[i]) for i in range(len(chains))]
    outs = {}
    for i, (p, j, s) in enumerate(chains):
        vc, vn = kv_tiles(p, j, s, A_KV_WIDTH)
        o = (jnp.dot(pc[i].astype(BF16), vc, preferred_element_type=F32)
             + jnp.dot(pn[i].astype(BF16), vn, preferred_element_type=F32)) * (1.0 / den[i])
        outs[p, j] = o if s == 0 else jnp.where(own == s, o, outs[p, j])
    for p in pair_ids:
        rows = slice(p * SUBLANES, (p + 1) * SUBLANES)
        for j in range(n_tile):
            o = outs[p, j]
            for gg in range(4):
                even = o[2 * gg * SUBLANES:(2 * gg + 1) * SUBLANES]
                odd = o[(2 * gg + 1) * SUBLANES:(2 * gg + 2) * SUBLANES]
                if gg < 2:
                    tile = jnp.where(lo, even, pltpu.roll(odd, A_HEAD_DIM, 1))
                else:
                    tile = jnp.where(lo, pltpu.roll(even, A_HEAD_DIM, 1), odd)
                sl = slice((4 * j + gg) * LANES, (4 * j + gg + 1) * LANES)
                gt = g_ref[rows, sl]
                og_ref[rows, sl] = tile * (gt * _sigmoid(gt))
    for seq in range(SAMPLE_SB):
        new = kvn_ref[seq * t_new:(seq + 1) * t_new, :]
        nk_ref[seq, :keep - t_new, :] = ck_ref[seq, t_new:, :]
        nk_ref[seq, keep - t_new:, :] = new[:, :A_KV_WIDTH]
        nv_ref[seq, :keep - t_new, :] = cv_ref[seq, t_new:, :]
        nv_ref[seq, keep - t_new:, :] = new[:, A_KV_WIDTH:]
    out_ref[...] = h_ref[...] + jnp.dot(og_ref[...].astype(BF16), wout_ref[...],
                                        preferred_element_type=F32)


def _attn_sample_call(sinks, q, kv, g, h, cache_k, cache_v, bias0, bias1, wout_bf16, t_new):
    nseq, keep = cache_k.shape[0], cache_k.shape[1]
    rows = SAMPLE_SB * t_new
    row = lambda i: (i, 0)
    cspec = pl.BlockSpec((SAMPLE_SB, keep, A_KV_WIDTH), lambda i: (i, 0, 0))
    bspec = pl.BlockSpec((A_HEADS, SUBLANES, SAMPLE_KEYS), lambda i: (0, 0, 0))
    return pl.pallas_call(
        functools.partial(_attn_sample_kernel, t_new=t_new),
        grid=(nseq // SAMPLE_SB,),
        in_specs=[pl.BlockSpec(memory_space=pltpu.SMEM),
                  pl.BlockSpec((rows, A_WIDTH), row),
                  pl.BlockSpec((rows, 2 * A_KV_WIDTH), row),
                  pl.BlockSpec((rows, A_WIDTH), row),
                  pl.BlockSpec((rows, D_MODEL), row),
                  cspec, cspec, bspec, bspec,
                  pl.BlockSpec((A_WIDTH, D_MODEL), lambda i: (0, 0))],
        out_specs=[pl.BlockSpec((rows, D_MODEL), row), cspec, cspec],
        out_shape=[jax.ShapeDtypeStruct((nseq * t_new, D_MODEL), F32),
                   jax.ShapeDtypeStruct(cache_k.shape, F32),
                   jax.ShapeDtypeStruct(cache_v.shape, F32)],
        scratch_shapes=[pltpu.VMEM((rows, A_WIDTH), F32)],
        compiler_params=_cparams(1),
        name="attn_sample",
    )(sinks, q, kv, g, h, cache_k, cache_v, bias0, bias1, wout_bf16)


def _rwkv_proj_kernel(h_ref, shift_ref, gain_ref, mu_ref, win_ref, w0_ref, w1_ref, w2_ref,
                      a0_ref, a1_ref, a2_ref,
                      r_ref, k_ref, v_ref, g_ref, ld_ref, a_ref, xn_ref, *scratch, seq_len):
    xn = _rmsnorm(h_ref[...], gain_ref[...])
    tm = xn.shape[0]
    rolled = pltpu.roll(xn, 1, 0)
    row = _iota((tm, 1), 0)
    if seq_len is None:
        carry_ref, = scratch

        @pl.when(pl.program_id(1) == 0)
        def _():
            carry_ref[...] = shift_ref[0]

        xprev = jnp.where(row == 0, carry_ref[...], rolled)
        carry_ref[...] = xn[tm - 1:tm, :]
        xn_ref[0] = xn[tm - 1:tm, :]
    else:
        xprev = jnp.where(row % seq_len == 0, shift_ref[...], rolled)
        xn_ref[...] = xn
    dx = xprev - xn

    def mix(c):
        return (xn + dx * mu_ref[c:c + 1, :]).astype(BF16)

    for c, o_ref in enumerate((r_ref, k_ref, v_ref, g_ref)):
        o_ref[...] = jnp.dot(mix(c), win_ref[c], preferred_element_type=F32)
    lw = jnp.tanh(jnp.dot(mix(4), w1_ref[...], preferred_element_type=F32))
    z = w0_ref[...] + jnp.dot(lw.astype(BF16), w2_ref[...], preferred_element_type=F32)
    ld_ref[...] = -math.exp(-0.5) * _sigmoid(z)
    la = jnp.dot(mix(5), a1_ref[...], preferred_element_type=F32)
    a_ref[...] = _sigmoid(a0_ref[...] + jnp.dot(la.astype(BF16), a2_ref[...],
                                                preferred_element_type=F32))


def _rwkv_proj_call(h, shift, p, nbatch, ntile, tm, seq_len):
    n = h.shape[0]
    row = lambda b, i: (b * ntile + i, 0)
    full2 = lambda b, i: (0, 0)
    if seq_len is None:
        shift_spec = pl.BlockSpec((1, 1, D_MODEL), lambda b, i: (b, 0, 0))
        xn_spec = pl.BlockSpec((1, 1, D_MODEL), lambda b, i: (b, 0, 0))
        xn_shape = jax.ShapeDtypeStruct((nbatch, 1, D_MODEL), F32)
        scratch = [pltpu.VMEM((1, D_MODEL), F32)]
    else:
        shift_spec = pl.BlockSpec((tm, D_MODEL), row)
        xn_spec = pl.BlockSpec((tm, D_MODEL), row)
        xn_shape = jax.ShapeDtypeStruct((n, D_MODEL), F32)
        scratch = []
    lora = p["w1"].shape[1]
    big = jax.ShapeDtypeStruct((n, D_MODEL), F32)
    return pl.pallas_call(
        functools.partial(_rwkv_proj_kernel, seq_len=seq_len),
        grid=(nbatch, ntile),
        in_specs=[pl.BlockSpec((tm, D_MODEL), row),
                  shift_spec,
                  pl.BlockSpec((1, D_MODEL), full2),
                  pl.BlockSpec(p["mu"].shape, full2),
                  pl.BlockSpec(p["w_in"].shape, lambda b, i: (0, 0, 0)),
                  pl.BlockSpec((1, D_MODEL), full2),
                  pl.BlockSpec((D_MODEL, lora), full2),
                  pl.BlockSpec((lora, D_MODEL), full2),
                  pl.BlockSpec((1, D_MODEL), full2),
                  pl.BlockSpec((D_MODEL, lora), full2),
                  pl.BlockSpec((lora, D_MODEL), full2)],
        out_specs=[pl.BlockSpec((tm, D_MODEL), row)] * 6 + [xn_spec],
        out_shape=[big] * 6 + [xn_shape],
        scratch_shapes=scratch,
        compiler_params=_cparams(2),
        name="rwkv_proj",
    )(h, shift, p["gain"], p["mu"], p["w_in"], p["w0"], p["w1"], p["w2"], p["a0"], p["a1"], p["a2"])


NN = (((1,), (0,)), ((), ()))
NT = (((1,), (1,)), ((), ()))
TN = (((0,), (0,)), ((), ()))


def _split(x):
    hi = x.astype(BF16)
    return hi, (x - hi.astype(F32)).astype(BF16)


def _mm(a, b, dn):
    return lax.dot_general(a, b, dn, preferred_element_type=F32)


def _mm3s(a, b, dn):
    (ah, al), (bh, bl) = a, b
    if dn == TN:
        both = _mm(jnp.concatenate([ah, al], axis=1), bh, dn)
        m = ah.shape[1]
    else:
        both = _mm(jnp.concatenate([ah, al], axis=0), bh, dn)
        m = ah.shape[0]
    return both[:m] + both[m:] + _mm(ah, bl, dn)


def _mm3(a, b, dn):
    return _mm3s(_split(a), _split(b), dn)


def _mm3_top(x_hi, x_lo_top, b, dn):
    bh, bl = b
    c = x_lo_top.shape[0]
    both = _mm(jnp.concatenate([x_hi, x_lo_top], axis=0), bh, dn)
    return both[:c] + both[2 * c:] + _mm(x_hi[:c], bl, dn), both[c:2 * c]


def _seg_sums(xs, ones_bf16, exact):
    rows = xs[0].shape[0]
    stacked = jnp.concatenate(xs, axis=0)
    if exact:
        hi, lo = _split(stacked)
        out = _mm(lo, ones_bf16, NN) + _mm(hi, ones_bf16, NN)
    else:
        out = _mm(stacked.astype(BF16), ones_bf16, NN)
    return [out[i * rows:(i + 1) * rows] for i in range(len(xs))]


def _lane_lo():
    return _iota((1, LANES), 1) < R_HEAD_DIM


def _same_half():
    return (_iota((LANES, LANES), 0) < R_HEAD_DIM) == (_iota((LANES, LANES), 1) < R_HEAD_DIM)


def _ones_blk():
    return _same_half().astype(BF16)


def _bd(z):
    lo = _lane_lo()
    return jnp.concatenate([jnp.where(lo, z, 0.0), jnp.where(lo, 0.0, z)], axis=0)


def _bd_swap(z):
    lo = _lane_lo()
    return jnp.concatenate([jnp.where(lo, 0.0, z), jnp.where(lo, z, 0.0)], axis=0)


def _split_map(z, f):
    hi, lw = _split(z)
    return f(hi), f(lw)


def _wkv_batch_stage(r, k, v, a, ld, kkg, kag, fillers=(), early=None):
    c = r[0].shape[0]
    pairs = range(len(r))
    half = R_HEAD_DIM
    lo = _lane_lo()
    t_row, t_col = _iota((c, LANES), 0), _iota((c, LANES), 1) & (half - 1)
    strict, incl = t_row > t_col, t_row >= t_col
    tri = (_iota((c, c), 0) >= _iota((c, c), 1)).astype(BF16)
    ones_blk = _ones_blk()
    eye_pair = (_iota((c, LANES), 0) == (_iota((c, LANES), 1) & (half - 1))).astype(F32)

    kkx = [k[j] * kkg[j] for j in pairs]
    ssq = _seg_sums([kkx[j] * kkx[j] for j in pairs], ones_blk, exact=True)
    ld_hi, ld_lo = _split(jnp.concatenate(ld, axis=1))
    cs_all = _mm(tri, ld_lo, NN) + _mm(tri, ld_hi, NN)
    cs = [cs_all[:, j * LANES:(j + 1) * LANES] for j in pairs]
    tot = [cs[j][c - 1:c, :] for j in pairs]
    x_hi, at_lo, kh_all, bke, vbd_s, lk_a, lk_r, lb_a, lb_r = ([] for _ in range(9))
    for j in pairs:
        kk = kkx[j] / jnp.maximum(jnp.sqrt(ssq[j]), 1e-12)
        kh = k[j] * (1.0 + (a[j] - 1.0) * kag[j])
        bv = kk * a[j]
        e_neg = jnp.exp(-cs[j])
        e_end = jnp.exp(tot[j] - cs[j])
        at, rt = -kk * jnp.exp(cs[j] - ld[j]), r[j] * jnp.exp(cs[j])
        at_hi, at_l = _split(at)
        x_hi.append(jnp.concatenate([at_hi, rt.astype(BF16)], axis=0))
        at_lo.append(at_l)
        kh_all.append(kh)
        bt, kt = _split(bv * e_neg), _split(kh * e_neg)
        y_hi, y_lo = (jnp.concatenate([p, q], axis=0) for p, q in zip(bt, kt))
        bke.append(jnp.concatenate([bv * e_end, kh * e_end], axis=0))
        vbd_s.append(_split_map(v[j], _bd))
        xa, xb = jnp.where(lo, x_hi[j], 0.0), jnp.where(lo, 0.0, x_hi[j])
        both = _mm(jnp.concatenate([xa, xb, jnp.where(lo, at_l, 0.0), jnp.where(lo, 0.0, at_l)], axis=0),
                   y_hi, NT)
        corr = _mm(jnp.concatenate([xa[:c], xb[:c]], axis=0), y_lo, NT)
        ga_a = pltpu.roll(both[:c] + both[4 * c:5 * c] + corr[:c], half, 1)
        ga_r = pltpu.roll(both[c:2 * c], half, 1)
        gb_a = both[2 * c:3 * c] + both[5 * c:] + corr[c:]
        gb_r = both[3 * c:4 * c]
        lk_a.append(jnp.where(strict, jnp.where(lo, ga_a, gb_a), 0.0))
        lk_r.append(jnp.where(incl, jnp.where(lo, ga_r, gb_r), 0.0))
        lb_a.append(jnp.where(strict, jnp.where(lo, gb_a, ga_a), 0.0))
        lb_r.append(jnp.where(incl, jnp.where(lo, gb_r, ga_r), 0.0))
    fillers = list(fillers)
    if early is not None:
        early.update(x_hi=x_hi, at_lo=at_lo)
    pw = lb_a
    acc = [eye_pair + pw[j] for j in pairs]
    pw = [_mm(pw[j].astype(BF16), _bd(pw[j].astype(BF16)), NN) for j in pairs]
    from_v = []
    for j in pairs:
        lk_hi, lk_lo = _split(lk_a[j])
        from_v.append(_mm3_top(jnp.concatenate([lk_hi, lk_r[j].astype(BF16)], axis=0), lk_lo, vbd_s[j], NN))
    for _ in range(int(math.log2(c)) - 2):
        both = [_mm(jnp.concatenate([pw[j], acc[j]], axis=0).astype(BF16), _bd(pw[j].astype(BF16)), NN)
                for j in pairs]
        if fillers:
            fillers.pop(0)()
        pw = [both[j][:c] for j in pairs]
        acc = [acc[j] + both[j][c:] for j in pairs]
    tinv = [acc[j] + _mm(acc[j].astype(BF16), _bd(pw[j].astype(BF16)), NN) for j in pairs]
    for fill in fillers:
        fill()
    return dict(x_hi=x_hi, at_lo=at_lo, kh=kh_all, lb_r=lb_r, tinv=tinv, from_v=from_v, bke=bke,
                tot=tot, ones_blk=ones_blk)


def _wkv_finish(stage, from_state_a, from_state_r):
    pairs = range(len(from_state_a))
    u = [_mm3s(_split(stage["tinv"][j]),
               _split_map(from_state_a[j] + stage["from_v"][j][0], _bd_swap), NN) for j in pairs]
    y = [from_state_r[j] + stage["from_v"][j][1]
         + _mm(stage["lb_r"][j].astype(BF16), _bd_swap(u[j].astype(BF16)), NN) for j in pairs]
    return u, y


def _wkv_gate(y, r, kh, v, g, rk, lng, lnb, ones_blk):
    n = len(y)
    pairs = range(n)
    inv_n = 1.0 / R_HEAD_DIM
    sums = _seg_sums([r[j] * kh[j] * rk[j] for j in pairs] + list(y), ones_blk, exact=False)
    rkk, mean = sums[:n], [s * inv_n for s in sums[n:]]
    d = [y[j] - mean[j] for j in pairs]
    var = [s * inv_n for s in _seg_sums([d[j] * d[j] for j in pairs], ones_blk, exact=False)]
    return [((d[j] * lax.rsqrt(var[j] + GN_EPS) * lng[j] + lnb[j] + rkk[j] * v[j])
             * (g[j] * _sigmoid(g[j]))).astype(BF16) for j in pairs]


def _wkv_chunk_kernel(r_ref, k_ref, v_ref, a_ref, ld_ref, g_ref, kkg_ref, kag_ref, rk_ref, lng_ref, lnb_ref,
                      z_ref, sout_ref, st_ref):
    c = WKV_CHUNK
    n_sub = r_ref.shape[0] // c
    npair = r_ref.shape[1] // LANES
    pairs = range(npair)
    items = [(s, j) for s in range(n_sub) for j in pairs]

    @pl.when(pl.program_id(2) == 0)
    def _():
        st_ref[...] = jnp.zeros_like(st_ref)

    def tile(ref, it):
        s, j = it
        return ref[s * c:(s + 1) * c, j * LANES:(j + 1) * LANES]

    def par(ref):
        return [ref[:, j * LANES:(j + 1) * LANES] for _, j in items]

    r, k, v, a, ld, g = ([tile(ref, it) for it in items] for ref in (r_ref, k_ref, v_ref, a_ref, ld_ref, g_ref))
    st = [st_ref[j] for j in pairs]
    from_state = {}
    stage = {}

    def state_products(group):
        def run():
            for j in group:
                from_state[j] = _mm3_top(stage["x_hi"][j], stage["at_lo"][j], _split(st[j]), NN)
        return run

    n_fill = 4
    groups = [list(pairs)[i::n_fill] for i in range(n_fill)]
    stage.update(_wkv_batch_stage(r, k, v, a, ld, par(kkg_ref), par(kag_ref),
                                  fillers=[state_products(grp) for grp in groups if grp], early=stage))
    same_half = _same_half()
    eye_full = _iota((LANES, LANES), 0) == _iota((LANES, LANES), 1)
    ys = []
    for s in range(n_sub):
        idx = [s * npair + j for j in pairs]
        if s > 0:
            for j in pairs:
                from_state[j] = _mm3_top(stage["x_hi"][idx[j]], stage["at_lo"][idx[j]], _split(st[j]), NN)
        sub = {key: [stage[key][i] for i in idx] for key in ("tinv", "from_v", "lb_r")}
        us, ys_s = _wkv_finish(sub, [from_state[j][0] for j in pairs], [from_state[j][1] for j in pairs])
        upd = [_mm3(stage["bke"][idx[j]], jnp.concatenate([us[j], v[idx[j]]], axis=0), TN) for j in pairs]
        ys += ys_s
        for j in pairs:
            w_col = jnp.sum(jnp.where(eye_full, jnp.exp(stage["tot"][idx[j]]), 0.0), axis=1, keepdims=True)
            st[j] = w_col * st[j] + jnp.where(same_half, upd[j], 0.0)
    zs = _wkv_gate(ys, r, stage["kh"], v, g, par(rk_ref), par(lng_ref), par(lnb_ref), stage["ones_blk"])
    for i, (s, j) in enumerate(items):
        z_ref[s * c:(s + 1) * c, j * LANES:(j + 1) * LANES] = zs[i]
    for j in pairs:
        st_ref[j] = st[j]
        sout_ref[0, j] = st[j]


def _wkv_chunk_call(r, k, v, a, ld, g, p, nbatch, seq):
    rows = WKV_CHUNK * WKV_CHUNKS_PER_STEP
    nstep = seq // rows
    npair = D_MODEL // LANES
    tile = pl.BlockSpec((rows, D_MODEL), lambda b, j, t: (b * nstep + t, 0))
    par = pl.BlockSpec((1, D_MODEL), lambda b, j, t: (0, 0))
    return pl.pallas_call(
        _wkv_chunk_kernel,
        grid=(nbatch, 1, nstep),
        in_specs=[tile] * 6 + [par] * 5,
        out_specs=[tile, pl.BlockSpec((1, npair, LANES, LANES), lambda b, j, t: (b, 0, 0, 0))],
        out_shape=[jax.ShapeDtypeStruct((nbatch * seq, D_MODEL), BF16),
                   jax.ShapeDtypeStruct((nbatch, npair, LANES, LANES), F32)],
        scratch_shapes=[pltpu.VMEM((npair, LANES, LANES), F32)],
        compiler_params=_cparams(3),
        name="wkv_chunk",
    )(r, k, v, a, ld, g, p["k_k"], p["k_a"], p["r_k"], p["ln_g"], p["ln_b"])


WKV_LANES_UNROLL = 4


def _wkv_lanes_kernel(r_ref, k_ref, v_ref, a_ref, ld_ref, g_ref, kkg_ref, kag_ref, rk_ref, lng_ref, lnb_ref,
                      s_ref, z_ref, sout_ref, prep_ref, y_ref, *, seq_len):
    n = R_HEAD_DIM
    nseq = s_ref.shape[3]
    eye = _iota((LANES, LANES), 0) == _iota((LANES, LANES), 1)

    def column(ref):
        return jnp.sum(jnp.where(eye, ref[...], 0.0), axis=1, keepdims=True)

    kkg, kag, rk, lng, lnb = (column(ref) for ref in (kkg_ref, kag_ref, rk_ref, lng_ref, lnb_ref))

    def token(ref, t):
        return ref[pl.ds(t, nseq, stride=seq_len), :].T

    bonus = []
    for t in range(seq_len):
        r, k, v, a = (token(ref, t) for ref in (r_ref, k_ref, v_ref, a_ref))
        w = jnp.exp(token(ld_ref, t))
        kkx = k * kkg
        kh = k * (1.0 + (a - 1.0) * kag)
        rkk = r * kh * rk
        tiles = []
        for hh in range(2):
            rows = slice(hh * n, (hh + 1) * n)
            nrm = jnp.sqrt(jnp.sum(kkx[rows] * kkx[rows], axis=0, keepdims=True))
            kk = kkx[rows] / jnp.maximum(nrm, 1e-12)
            for q, val in enumerate((w[rows], -kk, kk * a[rows], kh[rows], r[rows], v[rows])):
                prep_ref[q, t, hh] = val
            tiles.append(jnp.sum(rkk[rows], axis=0, keepdims=True) * v[rows])
        bonus.append(tiles)

    for hh in range(2):
        def advance(i, carry, hh=hh):
            for u in range(WKV_LANES_UNROLL):
                vi = i * WKV_LANES_UNROLL + u
                slab = s_ref[hh, vi]
                for t in range(seq_len):
                    w, av, bv, kh, r = (prep_ref[q, t, hh] for q in range(5))
                    vrow = prep_ref[5, t, hh, pl.ds(vi, 1), :]
                    sa = jnp.sum(slab * av, axis=0, keepdims=True)
                    slab = slab * w + sa * bv + vrow * kh
                    y_ref[t, hh, pl.ds(vi, 1), :] = jnp.sum(slab * r, axis=0, keepdims=True)
                sout_ref[hh, vi] = slab
            return carry

        lax.fori_loop(0, n // WKV_LANES_UNROLL, advance, 0)

    for t in range(seq_len):
        parts = []
        for hh in range(2):
            rows = slice(hh * n, (hh + 1) * n)
            y = y_ref[t, hh]
            d = y - jnp.mean(y, axis=0, keepdims=True)
            var = jnp.mean(d * d, axis=0, keepdims=True)
            parts.append(d * lax.rsqrt(var + GN_EPS) * lng[rows] + lnb[rows] + bonus[t][hh])
        g = token(g_ref, t)
        z = jnp.concatenate(parts, axis=0) * (g * _sigmoid(g))
        z_ref[pl.ds(t, nseq, stride=seq_len), :] = z.T


def _wkv_lanes_call(r, k, v, a, ld, g, p, state_hvkb, seq_len):
    n = r.shape[0]
    nseq = state_hvkb.shape[3]
    tile = pl.BlockSpec((n, LANES), lambda j: (0, j))
    par = pl.BlockSpec((1, LANES), lambda j: (0, j))
    sspec = pl.BlockSpec((2, R_HEAD_DIM, R_HEAD_DIM, nseq), lambda j: (j, 0, 0, 0))
    return pl.pallas_call(
        functools.partial(_wkv_lanes_kernel, seq_len=seq_len),
        grid=(D_MODEL // LANES,),
        in_specs=[tile] * 6 + [par] * 5 + [sspec],
        out_specs=[tile, sspec],
        out_shape=[jax.ShapeDtypeStruct((n, D_MODEL), F32),
                   jax.ShapeDtypeStruct(state_hvkb.shape, F32)],
        scratch_shapes=[pltpu.VMEM((6, seq_len, 2, R_HEAD_DIM, nseq), F32),
                        pltpu.VMEM((seq_len, 2, R_HEAD_DIM, nseq), F32)],
        compiler_params=_cparams(1),
        name="wkv_lanes",
    )(r, k, v, a, ld, g, p["k_k"], p["k_a"], p["r_k"], p["ln_g"], p["ln_b"], state_hvkb)


RWKV_OUT_PIECES = 4


def _rwkv_out_kernel(*refs):
    n = RWKV_OUT_PIECES
    z_refs, h_refs, (wout_ref, fg_ref, out_ref) = refs[:n], refs[n:2 * n], refs[2 * n:]
    z = jnp.concatenate([ref[...].astype(BF16) for ref in z_refs], axis=0)
    h = jnp.concatenate([ref[...] for ref in h_refs], axis=0)
    h2 = h + jnp.dot(z, wout_ref[...], preferred_element_type=F32)
    out_ref[...] = _rmsnorm(h2, fg_ref[...])


def _rwkv_out_call(z, h, p, nbatch, ntile, tm, pieces_per_batch, skip):
    piece = tm // RWKV_OUT_PIECES
    dst = lambda b, i: (b * ntile + i, 0)
    pspec = [pl.BlockSpec((piece, D_MODEL),
                          functools.partial(lambda b, i, kk: (b * pieces_per_batch + i * RWKV_OUT_PIECES + skip + kk, 0),
                                            kk=kk))
             for kk in range(RWKV_OUT_PIECES)]
    return pl.pallas_call(
        _rwkv_out_kernel,
        grid=(nbatch, ntile),
        in_specs=pspec + pspec + [pl.BlockSpec((D_MODEL, D_MODEL), lambda b, i: (0, 0)),
                                  pl.BlockSpec((1, D_MODEL), lambda b, i: (0, 0))],
        out_specs=pl.BlockSpec((tm, D_MODEL), dst),
        out_shape=jax.ShapeDtypeStruct((nbatch * ntile * tm, D_MODEL), F32),
        compiler_params=_cparams(2),
        name="rwkv_out",
    )(*([z] * RWKV_OUT_PIECES), *([h] * RWKV_OUT_PIECES), p["w_out"], p["final_gain"])


def _prompt_bucket():
    assert WINDOW == BLOCK
    rel = (np.arange(BLOCK)[:, None] - np.arange(BLOCK)[None, :]) % BLOCK
    return _t5_bucket_np(rel)


def _sample_bucket(keep, t_new, slot):
    t = (np.arange(SUBLANES) % t_new)[:, None]
    j = np.arange(SAMPLE_KEYS)[None, :]
    own = j - keep - slot * t_new
    rel = np.where(j < keep, keep + t - j, t - own)
    ok = (rel >= 0) & (rel < WINDOW) & ((j < keep) | ((own >= 0) & (own < t_new)))
    return np.where(ok, _t5_bucket_np(rel), -1).astype(np.int32)


def kernel(x_prompt, x_sample, cache_win_k, cache_win_v, state_wkv, state_shift, meta_tokens, rel_bias_table, norm_gain, final_gain, attn_w_in, attn_sinks, attn_w_out, rwkv_mu, rwkv_w_in, rwkv_w0, rwkv_w1, rwkv_w2, rwkv_a0, rwkv_a1, rwkv_a2, rwkv_k_k, rwkv_k_a, rwkv_r_k, rwkv_ln_gamma, rwkv_ln_beta, rwkv_w_out):
    nb, seq, _ = x_prompt.shape
    ns, t_new, _ = x_sample.shape
    keep = cache_win_k.shape[2]
    lp = seq + BLOCK
    nblk = lp // BLOCK
    row = lambda x: x.reshape(1, D_MODEL)

    w_in0 = attn_w_in[0].astype(BF16)
    w_out0 = attn_w_out[0].astype(BF16)
    gain0 = row(norm_gain[0])
    sinks = attn_sinks[0]
    rp = dict(gain=row(norm_gain[1]), mu=rwkv_mu[0], w_in=rwkv_w_in[0].astype(BF16),
              w0=row(rwkv_w0[0]), w1=rwkv_w1[0].astype(BF16), w2=rwkv_w2[0].astype(BF16),
              a0=row(rwkv_a0[0]), a1=rwkv_a1[0].astype(BF16), a2=rwkv_a2[0].astype(BF16),
              k_k=row(rwkv_k_k[0]), k_a=row(rwkv_k_a[0]), r_k=row(rwkv_r_k[0]), ln_g=row(rwkv_ln_gamma[0]),
              ln_b=row(rwkv_ln_beta[0]), w_out=rwkv_w_out[0].astype(BF16),
              final_gain=row(final_gain))

    bias_p, *bias_s = _bias_call(rel_bias_table, [_prompt_bucket()]
                                 + [_sample_bucket(keep, t_new, slot) for slot in range(2)])

    head = jnp.concatenate([jnp.zeros((PAD, D_MODEL), F32), meta_tokens.astype(F32)], axis=0)
    xp = x_prompt.reshape(nb * seq, D_MODEL)
    q, kv, g = _attn_proj_call(xp, head, gain0, w_in0, BF16, nb, lp // ATTN_PROJ_ROWS,
                               ATTN_PROJ_ROWS // BLOCK, BLOCK)
    h1 = _attn_prompt_call(sinks, q, kv, g, head, xp, bias_p, w_out0, nb, nblk)
    kv3 = kv.reshape(nb, lp, 2 * A_KV_WIDTH)[:, lp - WINDOW:, :]
    win_k_p = kv3[:, :, :A_KV_WIDTH].reshape(1, nb, WINDOW, A_KV_HEADS, A_HEAD_DIM)
    win_v_p = kv3[:, :, A_KV_WIDTH:].reshape(1, nb, WINDOW, A_KV_HEADS, A_HEAD_DIM)

    shift0 = jnp.zeros((nb, 1, D_MODEL), F32)
    r, k, v, g1, ld, a, xlast = _rwkv_proj_call(h1, shift0, rp, nb, lp // RWKV_PROJ_ROWS, RWKV_PROJ_ROWS, None)
    z, st = _wkv_chunk_call(r, k, v, a, ld, g1, rp, nb, lp)
    y_prompt = _rwkv_out_call(z, h1, rp, nb, seq // RWKV_OUT_ROWS, RWKV_OUT_ROWS,
                              lp * RWKV_OUT_PIECES // RWKV_OUT_ROWS, BLOCK * RWKV_OUT_PIECES // RWKV_OUT_ROWS)
    y_prompt = y_prompt.reshape(nb, seq, D_MODEL)
    st = st.reshape(nb, D_MODEL // LANES, 2, R_HEAD_DIM, 2, R_HEAD_DIM)
    st = jnp.stack([st[:, :, 0, :, 0, :], st[:, :, 1, :, 1, :]], axis=2)
    wkv_p = jnp.swapaxes(st, -1, -2).reshape(1, nb, R_HEADS, R_HEAD_DIM, R_HEAD_DIM)
    shift_p = xlast.reshape(1, nb, D_MODEL)

    xs = x_sample.reshape(ns * t_new, D_MODEL)
    qs, kvs, gs = _attn_proj_call(xs, None, gain0, w_in0, F32, 1, 1, 1, ns * t_new)
    ck = cache_win_k[0].reshape(ns, keep, A_KV_WIDTH)
    cv = cache_win_v[0].reshape(ns, keep, A_KV_WIDTH)
    h1s, nk, nv = _attn_sample_call(sinks, qs, kvs, gs, xs, ck, cv, bias_s[0], bias_s[1], w_out0, t_new)
    win_k_s = nk.reshape(1, ns, keep, A_KV_HEADS, A_HEAD_DIM)
    win_v_s = nv.reshape(1, ns, keep, A_KV_HEADS, A_HEAD_DIM)

    shift_rows = jnp.repeat(state_shift[0], t_new, axis=0)
    tms = 256
    rs, ks, vs, g1s, lds, as_, xns = _rwkv_proj_call(h1s, shift_rows, rp, 1, ns * t_new // tms, tms, t_new)
    zs, st_s = _wkv_lanes_call(rs, ks, vs, as_, lds, g1s, rp, jnp.transpose(state_wkv[0], (1, 2, 3, 0)), t_new)
    y_sample = _rwkv_out_call(zs, h1s, rp, 1, ns * t_new // RWKV_OUT_ROWS, RWKV_OUT_ROWS,
                              ns * t_new * RWKV_OUT_PIECES // RWKV_OUT_ROWS, 0)
    y_sample = y_sample.reshape(ns, t_new, D_MODEL)
    wkv_s = jnp.transpose(st_s, (3, 0, 1, 2))[None]
    shift_s = xns.reshape(ns, t_new, D_MODEL)[:, t_new - 1][None]

    return (y_prompt, y_sample, win_k_p, win_v_p, wkv_p, shift_p, win_k_s, win_v_s, wkv_s, shift_s)
```

```python
import functools
import math

import numpy as np
import jax
import jax.numpy as jnp
from jax import lax
from jax.experimental import pallas as pl
from jax.experimental.pallas import tpu as pltpu

F32 = jnp.float32
BF16 = jnp.bfloat16

D_MODEL = 1024
N_META = 16
RMS_EPS = 1e-6
A_HEADS = 16
A_KV_HEADS = 4
A_HEAD_DIM = 64
A_WIDTH = A_HEADS * A_HEAD_DIM
A_KV_WIDTH = A_KV_HEADS * A_HEAD_DIM
WINDOW = 128
BLOCK = 128
N_BUCKETS = 32
MAX_DISTANCE = 128
R_HEAD_DIM = 64
R_HEADS = D_MODEL // R_HEAD_DIM
GN_EPS = 64e-5

LANES = 128
SUBLANES = 8
PAD = BLOCK - N_META
NEG = -1e30
WKV_CHUNK = 64
WKV_CHUNKS_PER_STEP = 3
ATTN_PROJ_ROWS = 384
RWKV_PROJ_ROWS = 528
RWKV_OUT_ROWS = 512
VMEM_LIMIT = 56 * 1024 * 1024


def _cparams(n_axes):
    return pltpu.CompilerParams(dimension_semantics=("arbitrary",) * n_axes,
                                vmem_limit_bytes=VMEM_LIMIT)


def _rmsnorm(x, gain):
    return x * lax.rsqrt(jnp.mean(x * x, axis=-1, keepdims=True) + RMS_EPS) * gain


def _sigmoid(x):
    return 1.0 / (1.0 + jnp.exp(-x))


def _iota(shape, dim):
    return lax.broadcasted_iota(jnp.int32, shape, dim)


def _t5_bucket_np(rel):
    n = np.maximum(rel, 0)
    max_exact = N_BUCKETS // 2
    nf = np.maximum(n, max_exact).astype(np.float32)
    scale = np.float32(math.log(MAX_DISTANCE / max_exact))
    large = max_exact + (np.log(nf / np.float32(max_exact)) / scale
                         * np.float32(N_BUCKETS - max_exact)).astype(np.int32)
    large = np.minimum(large, N_BUCKETS - 1)
    return np.where(n < max_exact, n, large).astype(np.int32)


def _bias_kernel(table_ref, *refs):
    h = pl.program_id(0)
    n = len(refs) // 2
    for bucket_ref, out_ref in zip(refs[:n], refs[n:]):
        bk = bucket_ref[...]
        acc = jnp.full(bk.shape, NEG, F32)
        for b in range(N_BUCKETS):
            acc = jnp.where(bk == b, table_ref[b, h], acc)
        out_ref[0] = acc


def _bias_call(table, buckets_np):
    return pl.pallas_call(
        _bias_kernel,
        grid=(A_HEADS,),
        in_specs=[pl.BlockSpec(memory_space=pltpu.SMEM)]
                 + [pl.BlockSpec(bk.shape, lambda h: (0, 0)) for bk in buckets_np],
        out_specs=[pl.BlockSpec((1,) + bk.shape, lambda h: (h, 0, 0)) for bk in buckets_np],
        out_shape=[jax.ShapeDtypeStruct((A_HEADS,) + bk.shape, F32) for bk in buckets_np],
        compiler_params=_cparams(1),
        name="bias_expand",
    )(table, *(jnp.asarray(bk) for bk in buckets_np))


def _attn_proj_kernel(head_ref, *refs, n_piece):
    x_refs, (gain_ref, w_ref, q_ref, kv_ref, g_ref) = refs[:n_piece], refs[n_piece:]
    first = x_refs[0][...]
    if head_ref is not None:
        first = jnp.where(pl.program_id(1) == 0, head_ref[...], first)
    x = jnp.concatenate([first] + [ref[...] for ref in x_refs[1:]], axis=0)
    xn = _rmsnorm(x, gain_ref[...])
    proj = jnp.dot(xn.astype(BF16), w_ref[...], preferred_element_type=F32)
    q_ref[...] = (proj[:, :A_WIDTH] * (A_HEAD_DIM ** -0.5)).astype(q_ref.dtype)
    kv_ref[...] = proj[:, A_WIDTH:A_WIDTH + 2 * A_KV_WIDTH]
    g_ref[...] = proj[:, A_WIDTH + 2 * A_KV_WIDTH:]


def _attn_proj_call(x2d, head, gain, w_bf16, q_dtype, nbatch, ntile, n_piece, piece):
    tm = n_piece * piece
    wcols = w_bf16.shape[1]
    per_seq = x2d.shape[0] // (nbatch * piece)
    lead = 0 if head is None else 1
    dst = lambda b, i: (b * ntile + i, 0)
    xspec = [pl.BlockSpec((piece, D_MODEL),
                          functools.partial(lambda b, i, kk: (b * per_seq + jnp.maximum(i * n_piece + kk - lead, 0), 0),
                                            kk=kk))
             for kk in range(n_piece)]
    kern = functools.partial(_attn_proj_kernel, n_piece=n_piece)
    operands = [x2d] * n_piece + [gain, w_bf16]
    if head is None:
        kern = functools.partial(kern, None)
        head_spec = []
    else:
        head_spec = [pl.BlockSpec((piece, D_MODEL), lambda b, i: (0, 0))]
        operands = [head] + operands
    n = nbatch * ntile * tm
    return pl.pallas_call(
        kern,
        grid=(nbatch, ntile),
        in_specs=head_spec + xspec + [pl.BlockSpec((1, D_MODEL), lambda b, i: (0, 0)),
                                      pl.BlockSpec((D_MODEL, wcols), lambda b, i: (0, 0))],
        out_specs=[pl.BlockSpec((tm, A_WIDTH), dst),
                   pl.BlockSpec((tm, 2 * A_KV_WIDTH), dst),
                   pl.BlockSpec((tm, A_WIDTH), dst)],
        out_shape=[jax.ShapeDtypeStruct((n, A_WIDTH), q_dtype),
                   jax.ShapeDtypeStruct((n, 2 * A_KV_WIDTH), F32),
                   jax.ShapeDtypeStruct((n, A_WIDTH), F32)],
        compiler_params=_cparams(2),
        name="attn_proj",
    )(*operands)


def _padded_kv_tiles(kv, c):
    lo = _iota((1, LANES), 1) < A_HEAD_DIM
    j = c // 2
    out = []
    for base in (0, A_KV_WIDTH):
        t = kv[:, base + j * LANES: base + (j + 1) * LANES]
        tr = pltpu.roll(t, A_HEAD_DIM, 1)
        if c % 2 == 0:
            even, odd = jnp.where(lo, t, 0.0), jnp.where(lo, 0.0, tr)
        else:
            even, odd = jnp.where(lo, tr, 0.0), jnp.where(lo, 0.0, t)
        out += [even.astype(BF16), odd.astype(BF16)]
    return out


def _mm_nt(a, b):
    return lax.dot_general(a, b, (((1,), (1,)), ((), ())), preferred_element_type=F32)


def _attn_prompt_kernel(sinks_ref, q_ref, kvc_ref, kvp_ref, g_ref, head_ref, x_ref, bias_ref, wout_ref,
                        out_ref, og_ref):
    i = pl.program_id(1)
    resid = jnp.where(i == 0, head_ref[...], x_ref[...])
    stack = 2 * BLOCK
    row, col = _iota((stack, BLOCK), 0) & (BLOCK - 1), _iota((stack, BLOCK), 1)
    upper = _iota((stack, 1), 0) >= BLOCK
    own = col <= row
    kvalid = (i - 1 + own.astype(jnp.int32)) * BLOCK + col >= PAD
    kvp, kvc = kvp_ref[...], kvc_ref[...]
    prev = [_padded_kv_tiles(kvp, c) for c in range(A_KV_HEADS)]
    cur = [_padded_kv_tiles(kvc, c) for c in range(A_KV_HEADS)]
    chains = [(c, idx) for c in range(A_KV_HEADS) for idx in range(2)]
    n = range(len(chains))
    s, sink = [], []
    for c, idx in chains:
        q2 = q_ref[:, 2 * c * LANES:(2 * c + 2) * LANES]
        q2 = jnp.concatenate([q2[:, :LANES], q2[:, LANES:]], axis=0)
        sc = jnp.where(own, _mm_nt(q2, cur[c][idx]), _mm_nt(q2, prev[c][idx]))
        bias = jnp.concatenate([bias_ref[4 * c + idx], bias_ref[4 * c + 2 + idx]], axis=0)
        s.append(jnp.where(kvalid, sc + bias, NEG))
        sink.append(jnp.where(upper, sinks_ref[4 * c + 2 + idx], sinks_ref[4 * c + idx]))
    m = [jnp.maximum(jnp.max(s[t], axis=1, keepdims=True), sink[t]) for t in n]
    p = [jnp.exp(s[t] - m[t]) for t in n]
    den = [jnp.sum(p[t], axis=1, keepdims=True) + jnp.exp(sink[t] - m[t]) for t in n]
    o = []
    for t, (c, idx) in enumerate(chains):
        pv = (jnp.dot(jnp.where(own, p[t], 0.0).astype(BF16), cur[c][2 + idx], preferred_element_type=F32)
              + jnp.dot(jnp.where(own, 0.0, p[t]).astype(BF16), prev[c][2 + idx], preferred_element_type=F32))
        o.append(pv * (1.0 / den[t]))
    for c in range(A_KV_HEADS):
        both = o[2 * c] + o[2 * c + 1]
        for half in range(2):
            sl = slice((2 * c + half) * LANES, (2 * c + half + 1) * LANES)
            gt = g_ref[:, sl]
            og_ref[:, sl] = (both[half * BLOCK:(half + 1) * BLOCK] * (gt * _sigmoid(gt))).astype(BF16)
    out_ref[...] = resid + jnp.dot(og_ref[...], wout_ref[...], preferred_element_type=F32)


def _attn_prompt_call(sinks, q, kv, g, head, x2d, bias, wout_bf16, nbatch, nblk):
    n = q.shape[0]
    row = lambda b, i: (b * nblk + i, 0)
    prev = lambda b, i: (b * nblk + jnp.maximum(i - 1, 0), 0)
    xrow = lambda b, i: (b * (nblk - 1) + jnp.maximum(i - 1, 0), 0)
    return pl.pallas_call(
        _attn_prompt_kernel,
        grid=(nbatch, nblk),
        in_specs=[pl.BlockSpec(memory_space=pltpu.SMEM),
                  pl.BlockSpec((BLOCK, A_WIDTH), row),
                  pl.BlockSpec((BLOCK, 2 * A_KV_WIDTH), row),
                  pl.BlockSpec((BLOCK, 2 * A_KV_WIDTH), prev),
                  pl.BlockSpec((BLOCK, A_WIDTH), row),
                  pl.BlockSpec((BLOCK, D_MODEL), lambda b, i: (0, 0)),
                  pl.BlockSpec((BLOCK, D_MODEL), xrow),
                  pl.BlockSpec((A_HEADS, BLOCK, BLOCK), lambda b, i: (0, 0, 0)),
                  pl.BlockSpec((A_WIDTH, D_MODEL), lambda b, i: (0, 0))],
        out_specs=pl.BlockSpec((BLOCK, D_MODEL), row),
        out_shape=jax.ShapeDtypeStruct((n, D_MODEL), F32),
        scratch_shapes=[pltpu.VMEM((BLOCK, A_WIDTH), BF16)],
        compiler_params=_cparams(2),
        name="attn_prompt",
    )(sinks, q, kv, kv, g, head, x2d, bias, wout_bf16)


SAMPLE_SB = 8
SAMPLE_KEYS = 2 * BLOCK


def _attn_sample_kernel(sinks_ref, q_ref, kvn_ref, g_ref, h_ref, ck_ref, cv_ref, bias0_ref, bias1_ref,
                        wout_ref, out_ref, nk_ref, nv_ref, og_ref, *, t_new):
    keep = ck_ref.shape[2]
    lo = _iota((1, LANES), 1) < A_HEAD_DIM
    stack = 8 * SUBLANES
    own = (_iota((stack, 1), 0) & (SUBLANES - 1)) // t_new
    piece = _iota((stack, 1), 0) // SUBLANES
    bias_refs = (bias0_ref, bias1_ref)
    n_tile = A_KV_WIDTH // LANES
    pair_ids = range(SAMPLE_SB * t_new // SUBLANES)

    bias_c, bias_n, sink = [], [], []
    for j in range(n_tile):
        heads = slice(8 * j, 8 * j + 8)
        bias_c.append(bias0_ref[heads, :, :keep].reshape(stack, keep))
        bias_n.append([ref[heads, :, keep:keep + SUBLANES].reshape(stack, SUBLANES) for ref in bias_refs])
        col = jnp.zeros((stack, 1), F32)
        for gq in range(8):
            col = jnp.where(piece == gq, sinks_ref[8 * j + gq], col)
        sink.append(col)

    chains = [(p, j, s) for p in pair_ids for j in range(n_tile) for s in range(SUBLANES // t_new)]
    qs = {}
    for p in pair_ids:
        rows = slice(p * SUBLANES, (p + 1) * SUBLANES)
        for j in range(n_tile):
            parts = []
            for gq in range(8):
                t = q_ref[rows, (4 * j + gq // 2) * LANES:(4 * j + gq // 2 + 1) * LANES]
                want_lo = gq < 4
                if (gq % 2 == 0) != want_lo:
                    t = pltpu.roll(t, A_HEAD_DIM, 1)
                parts.append(jnp.where(lo, t, 0.0) if want_lo else jnp.where(lo, 0.0, t))
            qs[p, j] = jnp.concatenate(parts, axis=0).astype(BF16)

    def kv_tiles(p, j, s, base):
        seq = p * (SUBLANES // t_new) + s
        cache = (ck_ref if base == 0 else cv_ref)[seq][j * LANES:(j + 1) * LANES, :].astype(BF16)
        new = kvn_ref[p * SUBLANES:(p + 1) * SUBLANES, base + j * LANES:base + (j + 1) * LANES].astype(BF16)
        return cache, new

    sc, sn = [], []
    for p, j, s in chains:
        kc, kn = kv_tiles(p, j, s, 0)
        sc.append(jnp.dot(qs[p, j], kc, preferred_element_type=F32) + bias_c[j])
        sn.append(_mm_nt(qs[p, j], kn) + bias_n[j][s])
    mx = [jnp.maximum(jnp.maximum(jnp.max(sc[i], axis=1, keepdims=True),
                                  jnp.max(sn[i], axis=1, keepdims=True)), sink[chains[i][1]])
          for i in range(len(chains))]
    pc = [jnp.exp(sc[i] - mx[i]) for i in range(len(chains))]
    pn = [jnp.exp(sn[i] - mx[i]) for i in range(len(chains))]
    den = [jnp.sum(pc[i], axis=1, keepdims=True) + jnp.sum(pn[i], axis=1, keepdims=True)
           + jnp.exp(sink[chains[i][1]] - mx[i]) for i in range(len(chains))]
    outs = {}
    for i, (p, j, s) in enumerate(chains):
        vc, vn = kv_tiles(p, j, s, A_KV_WIDTH)
        o = (_mm_nt(pc[i].astype(BF16), vc)
             + jnp.dot(pn[i].astype(BF16), vn, preferred_element_type=F32)) * (1.0 / den[i])
        outs[p, j] = o if s == 0 else jnp.where(own == s, o, outs[p, j])
    for p in pair_ids:
        rows = slice(p * SUBLANES, (p + 1) * SUBLANES)
        for j in range(n_tile):
            o = outs[p, j]
            for gg in range(4):
                even = o[2 * gg * SUBLANES:(2 * gg + 1) * SUBLANES]
                odd = o[(2 * gg + 1) * SUBLANES:(2 * gg + 2) * SUBLANES]
                if gg < 2:
                    tile = jnp.where(lo, even, pltpu.roll(odd, A_HEAD_DIM, 1))
                else:
                    tile = jnp.where(lo, pltpu.roll(even, A_HEAD_DIM, 1), odd)
                sl = slice((4 * j + gg) * LANES, (4 * j + gg + 1) * LANES)
                gt = g_ref[rows, sl]
                og_ref[rows, sl] = tile * (gt * _sigmoid(gt))
    lane_pos = _iota((SUBLANES, keep), 1)
    old = _iota((1, keep), 1) < keep - t_new
    for p in pair_ids:
        new8 = kvn_ref[p * SUBLANES:(p + 1) * SUBLANES, :]
        for s in range(SUBLANES // t_new):
            seq = p * (SUBLANES // t_new) + s
            sel = (lane_pos == _iota((SUBLANES, keep), 0) + (keep - t_new - s * t_new)).astype(F32)
            placed = _mm3(new8, sel, TN)
            for ref_in, ref_out, base in ((ck_ref, nk_ref, 0), (cv_ref, nv_ref, A_KV_WIDTH)):
                shifted = pltpu.roll(ref_in[seq], keep - t_new, 1)
                ref_out[seq] = jnp.where(old, shifted, placed[base:base + A_KV_WIDTH])
    out_ref[...] = h_ref[...] + jnp.dot(og_ref[...].astype(BF16), wout_ref[...],
                                        preferred_element_type=F32)


def _attn_sample_call(sinks, q, kv, g, h, cache_k, cache_v, bias0, bias1, wout_bf16, t_new):
    nseq, keep = cache_k.shape[0], cache_k.shape[2]
    rows = SAMPLE_SB * t_new
    row = lambda i: (i, 0)
    cspec = pl.BlockSpec((SAMPLE_SB, A_KV_WIDTH, keep), lambda i: (i, 0, 0))
    bspec = pl.BlockSpec((A_HEADS, SUBLANES, SAMPLE_KEYS), lambda i: (0, 0, 0))
    return pl.pallas_call(
        functools.partial(_attn_sample_kernel, t_new=t_new),
        grid=(nseq // SAMPLE_SB,),
        in_specs=[pl.BlockSpec(memory_space=pltpu.SMEM),
                  pl.BlockSpec((rows, A_WIDTH), row),
                  pl.BlockSpec((rows, 2 * A_KV_WIDTH), row),
                  pl.BlockSpec((rows, A_WIDTH), row),
                  pl.BlockSpec((rows, D_MODEL), row),
                  cspec, cspec, bspec, bspec,
                  pl.BlockSpec((A_WIDTH, D_MODEL), lambda i: (0, 0))],
        out_specs=[pl.BlockSpec((rows, D_MODEL), row), cspec, cspec],
        out_shape=[jax.ShapeDtypeStruct((nseq * t_new, D_MODEL), F32),
                   jax.ShapeDtypeStruct(cache_k.shape, F32),
                   jax.ShapeDtypeStruct(cache_v.shape, F32)],
        scratch_shapes=[pltpu.VMEM((rows, A_WIDTH), F32)],
        compiler_params=_cparams(1),
        name="attn_sample",
    )(sinks, q, kv, g, h, cache_k, cache_v, bias0, bias1, wout_bf16)


def _rwkv_proj_kernel(h_ref, shift_ref, gain_ref, mu_ref, win_ref, w0_ref, w1_ref, w2_ref,
                      a0_ref, a1_ref, a2_ref,
                      r_ref, k_ref, v_ref, g_ref, ld_ref, a_ref, xn_ref, *scratch, seq_len):
    xn = _rmsnorm(h_ref[...], gain_ref[...])
    tm = xn.shape[0]
    rolled = pltpu.roll(xn, 1, 0)
    row = _iota((tm, 1), 0)
    if seq_len is None:
        carry_ref, = scratch

        @pl.when(pl.program_id(1) == 0)
        def _():
            carry_ref[...] = shift_ref[0]

        xprev = jnp.where(row == 0, carry_ref[...], rolled)
        carry_ref[...] = xn[tm - 1:tm, :]
        xn_ref[0] = xn[tm - 1:tm, :]
    else:
        xprev = jnp.where(row % seq_len == 0, shift_ref[...], rolled)
        xn_ref[...] = xn
    dx = xprev - xn

    def mix(c):
        return (xn + dx * mu_ref[c:c + 1, :]).astype(BF16)

    for c, o_ref in enumerate((r_ref, k_ref, v_ref, g_ref)):
        o_ref[...] = jnp.dot(mix(c), win_ref[c], preferred_element_type=F32)
    lw = jnp.tanh(jnp.dot(mix(4), w1_ref[...], preferred_element_type=F32))
    z = w0_ref[...] + jnp.dot(lw.astype(BF16), w2_ref[...], preferred_element_type=F32)
    ld_ref[...] = -math.exp(-0.5) * _sigmoid(z)
    la = jnp.dot(mix(5), a1_ref[...], preferred_element_type=F32)
    a_ref[...] = _sigmoid(a0_ref[...] + jnp.dot(la.astype(BF16), a2_ref[...],
                                                preferred_element_type=F32))


def _rwkv_proj_call(h, shift, p, nbatch, ntile, tm, seq_len):
    n = h.shape[0]
    row = lambda b, i: (b * ntile + i, 0)
    full2 = lambda b, i: (0, 0)
    if seq_len is None:
        shift_spec = pl.BlockSpec((1, 1, D_MODEL), lambda b, i: (b, 0, 0))
        xn_spec = pl.BlockSpec((1, 1, D_MODEL), lambda b, i: (b, 0, 0))
        xn_shape = jax.ShapeDtypeStruct((nbatch, 1, D_MODEL), F32)
        scratch = [pltpu.VMEM((1, D_MODEL), F32)]
    else:
        shift_spec = pl.BlockSpec((tm, D_MODEL), row)
        xn_spec = pl.BlockSpec((tm, D_MODEL), row)
        xn_shape = jax.ShapeDtypeStruct((n, D_MODEL), F32)
        scratch = []
    lora = p["w1"].shape[1]
    big = jax.ShapeDtypeStruct((n, D_MODEL), F32)
    return pl.pallas_call(
        functools.partial(_rwkv_proj_kernel, seq_len=seq_len),
        grid=(nbatch, ntile),
        in_specs=[pl.BlockSpec((tm, D_MODEL), row),
                  shift_spec,
                  pl.BlockSpec((1, D_MODEL), full2),
                  pl.BlockSpec(p["mu"].shape, full2),
                  pl.BlockSpec(p["w_in"].shape, lambda b, i: (0, 0, 0)),
                  pl.BlockSpec((1, D_MODEL), full2),
                  pl.BlockSpec((D_MODEL, lora), full2),
                  pl.BlockSpec((lora, D_MODEL), full2),
                  pl.BlockSpec((1, D_MODEL), full2),
                  pl.BlockSpec((D_MODEL, lora), full2),
                  pl.BlockSpec((lora, D_MODEL), full2)],
        out_specs=[pl.BlockSpec((tm, D_MODEL), row)] * 6 + [xn_spec],
        out_shape=[big] * 6 + [xn_shape],
        scratch_shapes=scratch,
        compiler_params=_cparams(2),
        name="rwkv_proj",
    )(h, shift, p["gain"], p["mu"], p["w_in"], p["w0"], p["w1"], p["w2"], p["a0"], p["a1"], p["a2"])


NN = (((1,), (0,)), ((), ()))
NT = (((1,), (1,)), ((), ()))
TN = (((0,), (0,)), ((), ()))


def _split(x):
    hi = x.astype(BF16)
    return hi, (x - hi.astype(F32)).astype(BF16)


def _mm(a, b, dn):
    return lax.dot_general(a, b, dn, preferred_element_type=F32)


def _mm3s(a, b, dn):
    (ah, al), (bh, bl) = a, b
    if dn == TN:
        both = _mm(jnp.concatenate([ah, al], axis=1), bh, dn)
        m = ah.shape[1]
    else:
        both = _mm(jnp.concatenate([ah, al], axis=0), bh, dn)
        m = ah.shape[0]
    return both[:m] + both[m:] + _mm(ah, bl, dn)


def _mm3(a, b, dn):
    return _mm3s(_split(a), _split(b), dn)


def _mm3_top(x_hi, x_lo_top, b, dn):
    bh, bl = b
    c = x_lo_top.shape[0]
    both = _mm(jnp.concatenate([x_hi, x_lo_top], axis=0), bh, dn)
    return both[:c] + both[2 * c:] + _mm(x_hi[:c], bl, dn), both[c:2 * c]


def _seg_sums(xs, ones_bf16, exact):
    rows = xs[0].shape[0]
    stacked = jnp.concatenate(xs, axis=0)
    if exact:
        hi, lo = _split(stacked)
        out = _mm(lo, ones_bf16, NN) + _mm(hi, ones_bf16, NN)
    else:
        out = _mm(stacked.astype(BF16), ones_bf16, NN)
    return [out[i * rows:(i + 1) * rows] for i in range(len(xs))]


def _lane_lo():
    return _iota((1, LANES), 1) < R_HEAD_DIM


def _same_half():
    return (_iota((LANES, LANES), 0) < R_HEAD_DIM) == (_iota((LANES, LANES), 1) < R_HEAD_DIM)


def _ones_blk():
    return _same_half().astype(BF16)


def _bd(z):
    lo = _lane_lo()
    return jnp.concatenate([jnp.where(lo, z, 0.0), jnp.where(lo, 0.0, z)], axis=0)


def _bd_swap(z):
    lo = _lane_lo()
    return jnp.concatenate([jnp.where(lo, 0.0, z), jnp.where(lo, z, 0.0)], axis=0)


def _split_map(z, f):
    hi, lw = _split(z)
    return f(hi), f(lw)


def _wkv_batch_stage(r, k, v, a, ld, kkg, kag, fillers=(), early=None):
    c = r[0].shape[0]
    pairs = range(len(r))
    half = R_HEAD_DIM
    lo = _lane_lo()
    t_row, t_col = _iota((c, LANES), 0), _iota((c, LANES), 1) & (half - 1)
    strict, incl = t_row > t_col, t_row >= t_col
    tri = (_iota((c, c), 0) >= _iota((c, c), 1)).astype(BF16)
    ones_blk = _ones_blk()
    eye_pair = (_iota((c, LANES), 0) == (_iota((c, LANES), 1) & (half - 1))).astype(F32)

    kkx = [k[j] * kkg[j] for j in pairs]
    ssq = _seg_sums([kkx[j] * kkx[j] for j in pairs], ones_blk, exact=True)
    ld_hi, ld_lo = _split(jnp.concatenate(ld, axis=1))
    cs_all = _mm(tri, ld_lo, NN) + _mm(tri, ld_hi, NN)
    cs = [cs_all[:, j * LANES:(j + 1) * LANES] for j in pairs]
    tot = [cs[j][c - 1:c, :] for j in pairs]
    x_hi, at_lo, kh_all, bke, vbd_s, lk_a, lk_r, lb_a, lb_r = ([] for _ in range(9))
    for j in pairs:
        kk = kkx[j] / jnp.maximum(jnp.sqrt(ssq[j]), 1e-12)
        kh = k[j] * (1.0 + (a[j] - 1.0) * kag[j])
        bv = kk * a[j]
        e_neg = jnp.exp(-cs[j])
        e_end = jnp.exp(tot[j] - cs[j])
        at, rt = -kk * jnp.exp(cs[j] - ld[j]), r[j] * jnp.exp(cs[j])
        at_hi, at_l = _split(at)
        x_hi.append(jnp.concatenate([at_hi, rt.astype(BF16)], axis=0))
        at_lo.append(at_l)
        kh_all.append(kh)
        bt, kt = _split(bv * e_neg), _split(kh * e_neg)
        y_hi, y_lo = (jnp.concatenate([p, q], axis=0) for p, q in zip(bt, kt))
        bke.append(jnp.concatenate([bv * e_end, kh * e_end], axis=0))
        vbd_s.append(_split_map(v[j], _bd))
        xa, xb = jnp.where(lo, x_hi[j], 0.0), jnp.where(lo, 0.0, x_hi[j])
        both = _mm(jnp.concatenate([xa, xb, jnp.where(lo, at_l, 0.0), jnp.where(lo, 0.0, at_l)], axis=0),
                   y_hi, NT)
        corr = _mm(jnp.concatenate([xa[:c], xb[:c]], axis=0), y_lo, NT)
        ga_a = pltpu.roll(both[:c] + both[4 * c:5 * c] + corr[:c], half, 1)
        ga_r = pltpu.roll(both[c:2 * c], half, 1)
        gb_a = both[2 * c:3 * c] + both[5 * c:] + corr[c:]
        gb_r = both[3 * c:4 * c]
        lk_a.append(jnp.where(strict, jnp.where(lo, ga_a, gb_a), 0.0))
        lk_r.append(jnp.where(incl, jnp.where(lo, ga_r, gb_r), 0.0))
        lb_a.append(jnp.where(strict, jnp.where(lo, gb_a, ga_a), 0.0))
        lb_r.append(jnp.where(incl, jnp.where(lo, gb_r, ga_r), 0.0))
    fillers = list(fillers)
    if early is not None:
        early.update(x_hi=x_hi, at_lo=at_lo)
    pw = lb_a
    acc = [eye_pair + pw[j] for j in pairs]
    pw = [_mm(pw[j].astype(BF16), _bd(pw[j].astype(BF16)), NN) for j in pairs]
    from_v = []
    for j in pairs:
        lk_hi, lk_lo = _split(lk_a[j])
        from_v.append(_mm3_top(jnp.concatenate([lk_hi, lk_r[j].astype(BF16)], axis=0), lk_lo, vbd_s[j], NN))
    for _ in range(int(math.log2(c)) - 2):
        both = [_mm(jnp.concatenate([pw[j], acc[j]], axis=0).astype(BF16), _bd(pw[j].astype(BF16)), NN)
                for j in pairs]
        if fillers:
            fillers.pop(0)()
        pw = [both[j][:c] for j in pairs]
        acc = [acc[j] + both[j][c:] for j in pairs]
    tinv = [acc[j] + _mm(acc[j].astype(BF16), _bd(pw[j].astype(BF16)), NN) for j in pairs]
    for fill in fillers:
        fill()
    return dict(x_hi=x_hi, at_lo=at_lo, kh=kh_all, lb_r=lb_r, tinv=tinv, from_v=from_v, bke=bke,
                tot=tot, ones_blk=ones_blk)


def _wkv_finish(stage, from_state_a, from_state_r):
    pairs = range(len(from_state_a))
    u = [_mm3s(_split(stage["tinv"][j]),
               _split_map(from_state_a[j] + stage["from_v"][j][0], _bd_swap), NN) for j in pairs]
    y = [from_state_r[j] + stage["from_v"][j][1]
         + _mm(stage["lb_r"][j].astype(BF16), _bd_swap(u[j].astype(BF16)), NN) for j in pairs]
    return u, y


def _wkv_gate(y, r, kh, v, g, rk, lng, lnb, ones_blk):
    n = len(y)
    pairs = range(n)
    inv_n = 1.0 / R_HEAD_DIM
    sums = _seg_sums([r[j] * kh[j] * rk[j] for j in pairs] + list(y), ones_blk, exact=False)
    rkk, mean = sums[:n], [s * inv_n for s in sums[n:]]
    d = [y[j] - mean[j] for j in pairs]
    var = [s * inv_n for s in _seg_sums([d[j] * d[j] for j in pairs], ones_blk, exact=False)]
    return [((d[j] * lax.rsqrt(var[j] + GN_EPS) * lng[j] + lnb[j] + rkk[j] * v[j])
             * (g[j] * _sigmoid(g[j]))).astype(BF16) for j in pairs]


def _wkv_chunk_kernel(r_ref, k_ref, v_ref, a_ref, ld_ref, g_ref, kkg_ref, kag_ref, rk_ref, lng_ref, lnb_ref,
                      z_ref, sout_ref, st_ref):
    c = WKV_CHUNK
    n_sub = r_ref.shape[0] // c
    npair = r_ref.shape[1] // LANES
    pairs = range(npair)
    items = [(s, j) for s in range(n_sub) for j in pairs]

    @pl.when(pl.program_id(2) == 0)
    def _():
        st_ref[...] = jnp.zeros_like(st_ref)

    def tile(ref, it):
        s, j = it
        return ref[s * c:(s + 1) * c, j * LANES:(j + 1) * LANES]

    def par(ref):
        return [ref[:, j * LANES:(j + 1) * LANES] for _, j in items]

    r, k, v, a, ld, g = ([tile(ref, it) for it in items] for ref in (r_ref, k_ref, v_ref, a_ref, ld_ref, g_ref))
    st = [st_ref[j] for j in pairs]
    from_state = {}
    stage = {}

    def state_products(group):
        def run():
            for j in group:
                from_state[j] = _mm3_top(stage["x_hi"][j], stage["at_lo"][j], _split(st[j]), NN)
        return run

    n_fill = 4
    groups = [list(pairs)[i::n_fill] for i in range(n_fill)]
    stage.update(_wkv_batch_stage(r, k, v, a, ld, par(kkg_ref), par(kag_ref),
                                  fillers=[state_products(grp) for grp in groups if grp], early=stage))
    same_half = _same_half()
    eye_full = _iota((LANES, LANES), 0) == _iota((LANES, LANES), 1)
    ys = []
    for s in range(n_sub):
        idx = [s * npair + j for j in pairs]
        if s > 0:
            for j in pairs:
                from_state[j] = _mm3_top(stage["x_hi"][idx[j]], stage["at_lo"][idx[j]], _split(st[j]), NN)
        sub = {key: [stage[key][i] for i in idx] for key in ("tinv", "from_v", "lb_r")}
        us, ys_s = _wkv_finish(sub, [from_state[j][0] for j in pairs], [from_state[j][1] for j in pairs])
        upd = [_mm3(stage["bke"][idx[j]], jnp.concatenate([us[j], v[idx[j]]], axis=0), TN) for j in pairs]
        ys += ys_s
        for j in pairs:
            w_col = jnp.sum(jnp.where(eye_full, jnp.exp(stage["tot"][idx[j]]), 0.0), axis=1, keepdims=True)
            st[j] = w_col * st[j] + jnp.where(same_half, upd[j], 0.0)
    zs = _wkv_gate(ys, r, stage["kh"], v, g, par(rk_ref), par(lng_ref), par(lnb_ref), stage["ones_blk"])
    for i, (s, j) in enumerate(items):
        z_ref[s * c:(s + 1) * c, j * LANES:(j + 1) * LANES] = zs[i]
    for j in pairs:
        st_ref[j] = st[j]
        sout_ref[0, j] = st[j]


def _wkv_chunk_call(r, k, v, a, ld, g, p, nbatch, seq):
    rows = WKV_CHUNK * WKV_CHUNKS_PER_STEP
    nstep = seq // rows
    npair = D_MODEL // LANES
    tile = pl.BlockSpec((rows, D_MODEL), lambda b, j, t: (b * nstep + t, 0))
    par = pl.BlockSpec((1, D_MODEL), lambda b, j, t: (0, 0))
    return pl.pallas_call(
        _wkv_chunk_kernel,
        grid=(nbatch, 1, nstep),
        in_specs=[tile] * 6 + [par] * 5,
        out_specs=[tile, pl.BlockSpec((1, npair, LANES, LANES), lambda b, j, t: (b, 0, 0, 0))],
        out_shape=[jax.ShapeDtypeStruct((nbatch * seq, D_MODEL), BF16),
                   jax.ShapeDtypeStruct((nbatch, npair, LANES, LANES), F32)],
        scratch_shapes=[pltpu.VMEM((npair, LANES, LANES), F32)],
        compiler_params=_cparams(3),
        name="wkv_chunk",
    )(r, k, v, a, ld, g, p["k_k"], p["k_a"], p["r_k"], p["ln_g"], p["ln_b"])


WKV_LANES_UNROLL = 4


def _wkv_lanes_kernel(r_ref, k_ref, v_ref, a_ref, ld_ref, g_ref, kkg_ref, kag_ref, rk_ref, lng_ref, lnb_ref,
                      s_ref, z_ref, sout_ref, prep_ref, y_ref, *, seq_len):
    n = R_HEAD_DIM
    nseq = s_ref.shape[3]
    eye = _iota((LANES, LANES), 0) == _iota((LANES, LANES), 1)

    def column(ref):
        return jnp.sum(jnp.where(eye, ref[...], 0.0), axis=1, keepdims=True)

    kkg, kag, rk, lng, lnb = (column(ref) for ref in (kkg_ref, kag_ref, rk_ref, lng_ref, lnb_ref))

    def token(ref, t):
        return ref[pl.ds(t, nseq, stride=seq_len), :].T

    bonus = []
    for t in range(seq_len):
        r, k, v, a = (token(ref, t) for ref in (r_ref, k_ref, v_ref, a_ref))
        w = jnp.exp(token(ld_ref, t))
        kkx = k * kkg
        kh = k * (1.0 + (a - 1.0) * kag)
        rkk = r * kh * rk
        tiles = []
        for hh in range(2):
            rows = slice(hh * n, (hh + 1) * n)
            nrm = jnp.sqrt(jnp.sum(kkx[rows] * kkx[rows], axis=0, keepdims=True))
            kk = kkx[rows] / jnp.maximum(nrm, 1e-12)
            for q, val in enumerate((w[rows], -kk, kk * a[rows], kh[rows], r[rows], v[rows])):
                prep_ref[q, t, hh] = val
            tiles.append(jnp.sum(rkk[rows], axis=0, keepdims=True) * v[rows])
        bonus.append(tiles)

    for hh in range(2):
        def advance(i, carry, hh=hh):
            for u in range(WKV_LANES_UNROLL):
                vi = i * WKV_LANES_UNROLL + u
                slab = s_ref[hh, vi]
                for t in range(seq_len):
                    w, av, bv, kh, r = (prep_ref[q, t, hh] for q in range(5))
                    vrow = prep_ref[5, t, hh, pl.ds(vi, 1), :]
                    sa = jnp.sum(slab * av, axis=0, keepdims=True)
                    slab = slab * w + sa * bv + vrow * kh
                    y_ref[t, hh, pl.ds(vi, 1), :] = jnp.sum(slab * r, axis=0, keepdims=True)
                sout_ref[hh, vi] = slab
            return carry

        lax.fori_loop(0, n // WKV_LANES_UNROLL, advance, 0)

    for t in range(seq_len):
        parts = []
        for hh in range(2):
            rows = slice(hh * n, (hh + 1) * n)
            y = y_ref[t, hh]
            d = y - jnp.mean(y, axis=0, keepdims=True)
            var = jnp.mean(d * d, axis=0, keepdims=True)
            parts.append(d * lax.rsqrt(var + GN_EPS) * lng[rows] + lnb[rows] + bonus[t][hh])
        g = token(g_ref, t)
        z = jnp.concatenate(parts, axis=0) * (g * _sigmoid(g))
        z_ref[pl.ds(t, nseq, stride=seq_len), :] = z.T


def _wkv_lanes_call(r, k, v, a, ld, g, p, state_hvkb, seq_len):
    n = r.shape[0]
    nseq = state_hvkb.shape[3]
    tile = pl.BlockSpec((n, LANES), lambda j: (0, j))
    par = pl.BlockSpec((1, LANES), lambda j: (0, j))
    sspec = pl.BlockSpec((2, R_HEAD_DIM, R_HEAD_DIM, nseq), lambda j: (j, 0, 0, 0))
    return pl.pallas_call(
        functools.partial(_wkv_lanes_kernel, seq_len=seq_len),
        grid=(D_MODEL // LANES,),
        in_specs=[tile] * 6 + [par] * 5 + [sspec],
        out_specs=[tile, sspec],
        out_shape=[jax.ShapeDtypeStruct((n, D_MODEL), F32),
                   jax.ShapeDtypeStruct(state_hvkb.shape, F32)],
        scratch_shapes=[pltpu.VMEM((6, seq_len, 2, R_HEAD_DIM, nseq), F32),
                        pltpu.VMEM((seq_len, 2, R_HEAD_DIM, nseq), F32)],
        compiler_params=_cparams(1),
        name="wkv_lanes",
    )(r, k, v, a, ld, g, p["k_k"], p["k_a"], p["r_k"], p["ln_g"], p["ln_b"], state_hvkb)


RWKV_OUT_PIECES = 4


def _rwkv_out_kernel(*refs):
    n = RWKV_OUT_PIECES
    z_refs, h_refs, (wout_ref, fg_ref, out_ref) = refs[:n], refs[n:2 * n], refs[2 * n:]
    z = jnp.concatenate([ref[...].astype(BF16) for ref in z_refs], axis=0)
    h = jnp.concatenate([ref[...] for ref in h_refs], axis=0)
    h2 = h + jnp.dot(z, wout_ref[...], preferred_element_type=F32)
    out_ref[...] = _rmsnorm(h2, fg_ref[...])


def _rwkv_out_call(z, h, p, nbatch, ntile, tm, pieces_per_batch, skip):
    piece = tm // RWKV_OUT_PIECES
    dst = lambda b, i: (b * ntile + i, 0)
    pspec = [pl.BlockSpec((piece, D_MODEL),
                          functools.partial(lambda b, i, kk: (b * pieces_per_batch + i * RWKV_OUT_PIECES + skip + kk, 0),
                                            kk=kk))
             for kk in range(RWKV_OUT_PIECES)]
    return pl.pallas_call(
        _rwkv_out_kernel,
        grid=(nbatch, ntile),
        in_specs=pspec + pspec + [pl.BlockSpec((D_MODEL, D_MODEL), lambda b, i: (0, 0)),
                                  pl.BlockSpec((1, D_MODEL), lambda b, i: (0, 0))],
        out_specs=pl.BlockSpec((tm, D_MODEL), dst),
        out_shape=jax.ShapeDtypeStruct((nbatch * ntile * tm, D_MODEL), F32),
        compiler_params=_cparams(2),
        name="rwkv_out",
    )(*([z] * RWKV_OUT_PIECES), *([h] * RWKV_OUT_PIECES), p["w_out"], p["final_gain"])


def _prompt_bucket():
    assert WINDOW == BLOCK
    rel = (np.arange(BLOCK)[:, None] - np.arange(BLOCK)[None, :]) % BLOCK
    return _t5_bucket_np(rel)


def _sample_bucket(keep, t_new, slot):
    t = (np.arange(SUBLANES) % t_new)[:, None]
    j = np.arange(SAMPLE_KEYS)[None, :]
    own = j - keep - slot * t_new
    rel = np.where(j < keep, keep + t - j, t - own)
    ok = (rel >= 0) & (rel < WINDOW) & ((j < keep) | ((own >= 0) & (own < t_new)))
    return np.where(ok, _t5_bucket_np(rel), -1).astype(np.int32)


def kernel(x_prompt, x_sample, cache_win_k, cache_win_v, state_wkv, state_shift, meta_tokens, rel_bias_table, norm_gain, final_gain, attn_w_in, attn_sinks, attn_w_out, rwkv_mu, rwkv_w_in, rwkv_w0, rwkv_w1, rwkv_w2, rwkv_a0, rwkv_a1, rwkv_a2, rwkv_k_k, rwkv_k_a, rwkv_r_k, rwkv_ln_gamma, rwkv_ln_beta, rwkv_w_out):
    nb, seq, _ = x_prompt.shape
    ns, t_new, _ = x_sample.shape
    keep = cache_win_k.shape[2]
    lp = seq + BLOCK
    nblk = lp // BLOCK
    row = lambda x: x.reshape(1, D_MODEL)

    w_in0 = attn_w_in[0].astype(BF16)
    w_out0 = attn_w_out[0].astype(BF16)
    gain0 = row(norm_gain[0])
    sinks = attn_sinks[0]
    rp = dict(gain=row(norm_gain[1]), mu=rwkv_mu[0], w_in=rwkv_w_in[0].astype(BF16),
              w0=row(rwkv_w0[0]), w1=rwkv_w1[0].astype(BF16), w2=rwkv_w2[0].astype(BF16),
              a0=row(rwkv_a0[0]), a1=rwkv_a1[0].astype(BF16), a2=rwkv_a2[0].astype(BF16),
              k_k=row(rwkv_k_k[0]), k_a=row(rwkv_k_a[0]), r_k=row(rwkv_r_k[0]), ln_g=row(rwkv_ln_gamma[0]),
              ln_b=row(rwkv_ln_beta[0]), w_out=rwkv_w_out[0].astype(BF16),
              final_gain=row(final_gain))

    bias_p, *bias_s = _bias_call(rel_bias_table, [_prompt_bucket()]
                                 + [_sample_bucket(keep, t_new, slot) for slot in range(2)])

    head = jnp.concatenate([jnp.zeros((PAD, D_MODEL), F32), meta_tokens.astype(F32)], axis=0)
    xp = x_prompt.reshape(nb * seq, D_MODEL)
    q, kv, g = _attn_proj_call(xp, head, gain0, w_in0, BF16, nb, lp // ATTN_PROJ_ROWS,
                               ATTN_PROJ_ROWS // BLOCK, BLOCK)
    h1 = _attn_prompt_call(sinks, q, kv, g, head, xp, bias_p, w_out0, nb, nblk)
    kv3 = kv.reshape(nb, lp, 2 * A_KV_WIDTH)[:, lp - WINDOW:, :]
    win_k_p = kv3[:, :, :A_KV_WIDTH].reshape(1, nb, WINDOW, A_KV_HEADS, A_HEAD_DIM)
    win_v_p = kv3[:, :, A_KV_WIDTH:].reshape(1, nb, WINDOW, A_KV_HEADS, A_HEAD_DIM)

    shift0 = jnp.zeros((nb, 1, D_MODEL), F32)
    r, k, v, g1, ld, a, xlast = _rwkv_proj_call(h1, shift0, rp, nb, lp // RWKV_PROJ_ROWS, RWKV_PROJ_ROWS, None)
    z, st = _wkv_chunk_call(r, k, v, a, ld, g1, rp, nb, lp)
    y_prompt = _rwkv_out_call(z, h1, rp, nb, seq // RWKV_OUT_ROWS, RWKV_OUT_ROWS,
                              lp * RWKV_OUT_PIECES // RWKV_OUT_ROWS, BLOCK * RWKV_OUT_PIECES // RWKV_OUT_ROWS)
    y_prompt = y_prompt.reshape(nb, seq, D_MODEL)
    st = st.reshape(nb, D_MODEL // LANES, 2, R_HEAD_DIM, 2, R_HEAD_DIM)
    st = jnp.stack([st[:, :, 0, :, 0, :], st[:, :, 1, :, 1, :]], axis=2)
    wkv_p = jnp.swapaxes(st, -1, -2).reshape(1, nb, R_HEADS, R_HEAD_DIM, R_HEAD_DIM)
    shift_p = xlast.reshape(1, nb, D_MODEL)

    xs = x_sample.reshape(ns * t_new, D_MODEL)
    qs, kvs, gs = _attn_proj_call(xs, None, gain0, w_in0, F32, 1, 1, 1, ns * t_new)
    ck = jnp.swapaxes(cache_win_k[0].reshape(ns, keep, A_KV_WIDTH), 1, 2)
    cv = jnp.swapaxes(cache_win_v[0].reshape(ns, keep, A_KV_WIDTH), 1, 2)
    h1s, nk, nv = _attn_sample_call(sinks, qs, kvs, gs, xs, ck, cv, bias_s[0], bias_s[1], w_out0, t_new)
    win_k_s = jnp.swapaxes(nk, 1, 2).reshape(1, ns, keep, A_KV_HEADS, A_HEAD_DIM)
    win_v_s = jnp.swapaxes(nv, 1, 2).reshape(1, ns, keep, A_KV_HEADS, A_HEAD_DIM)

    shift_rows = jnp.repeat(state_shift[0], t_new, axis=0)
    tms = 256
    rs, ks, vs, g1s, lds, as_, xns = _rwkv_proj_call(h1s, shift_rows, rp, 1, ns * t_new // tms, tms, t_new)
    zs, st_s = _wkv_lanes_call(rs, ks, vs, as_, lds, g1s, rp, jnp.transpose(state_wkv[0], (1, 2, 3, 0)), t_new)
    y_sample = _rwkv_out_call(zs, h1s, rp, 1, ns * t_new // RWKV_OUT_ROWS, RWKV_OUT_ROWS,
                              ns * t_new * RWKV_OUT_PIECES // RWKV_OUT_ROWS, 0)
    y_sample = y_sample.reshape(ns, t_new, D_MODEL)
    wkv_s = jnp.transpose(st_s, (3, 0, 1, 2))[None]
    shift_s = xns.reshape(ns, t_new, D_MODEL)[:, t_new - 1][None]

    return (y_prompt, y_sample, win_k_p, win_v_p, wkv_p, shift_p, win_k_s, win_v_s, wkv_s, shift_s)
```

```python
import functools
import math

import numpy as np
import jax
import jax.numpy as jnp
from jax import lax
from jax.experimental import pallas as pl
from jax.experimental.pallas import tpu as pltpu

F32 = jnp.float32
BF16 = jnp.bfloat16

D_MODEL = 1024
N_META = 16
RMS_EPS = 1e-6
A_HEADS = 16
A_KV_HEADS = 4
A_HEAD_DIM = 64
A_WIDTH = A_HEADS * A_HEAD_DIM
A_KV_WIDTH = A_KV_HEADS * A_HEAD_DIM
WINDOW = 128
BLOCK = 128
N_BUCKETS = 32
MAX_DISTANCE = 128
R_HEAD_DIM = 64
R_HEADS = D_MODEL // R_HEAD_DIM
GN_EPS = 64e-5

LANES = 128
SUBLANES = 8
PAD = BLOCK - N_META
NEG = -1e30
WKV_CHUNK = 64
WKV_CHUNKS_PER_STEP = 3
ATTN_BLOCKS_PER_STEP = 3
ATTN_PROJ_ROWS = 384
RWKV_PROJ_ROWS = 528
RWKV_OUT_ROWS = 1024
VMEM_LIMIT = 56 * 1024 * 1024


def _cparams(n_axes):
    return pltpu.CompilerParams(dimension_semantics=("arbitrary",) * n_axes,
                                vmem_limit_bytes=VMEM_LIMIT)


def _rmsnorm(x, gain):
    return x * lax.rsqrt(jnp.mean(x * x, axis=-1, keepdims=True) + RMS_EPS) * gain


def _sigmoid(x):
    return 1.0 / (1.0 + jnp.exp(-x))


def _iota(shape, dim):
    return lax.broadcasted_iota(jnp.int32, shape, dim)


def _t5_bucket_np(rel):
    n = np.maximum(rel, 0)
    max_exact = N_BUCKETS // 2
    nf = np.maximum(n, max_exact).astype(np.float32)
    scale = np.float32(math.log(MAX_DISTANCE / max_exact))
    large = max_exact + (np.log(nf / np.float32(max_exact)) / scale
                         * np.float32(N_BUCKETS - max_exact)).astype(np.int32)
    large = np.minimum(large, N_BUCKETS - 1)
    return np.where(n < max_exact, n, large).astype(np.int32)


def _bias_kernel(table_ref, *refs):
    h = pl.program_id(0)
    n = len(refs) // 2
    for bucket_ref, out_ref in zip(refs[:n], refs[n:]):
        bk = bucket_ref[...]
        acc = jnp.full(bk.shape, NEG, F32)
        for b in range(N_BUCKETS):
            acc = jnp.where(bk == b, table_ref[b, h], acc)
        out_ref[0] = acc


def _bias_call(table, buckets_np):
    return pl.pallas_call(
        _bias_kernel,
        grid=(A_HEADS,),
        in_specs=[pl.BlockSpec(memory_space=pltpu.SMEM)]
                 + [pl.BlockSpec(bk.shape, lambda h: (0, 0)) for bk in buckets_np],
        out_specs=[pl.BlockSpec((1,) + bk.shape, lambda h: (h, 0, 0)) for bk in buckets_np],
        out_shape=[jax.ShapeDtypeStruct((A_HEADS,) + bk.shape, F32) for bk in buckets_np],
        compiler_params=_cparams(1),
        name="bias_expand",
    )(table, *(jnp.asarray(bk) for bk in buckets_np))


def _attn_proj_kernel(head_ref, *refs, n_piece):
    x_refs, (gain_ref, w_ref, q_ref, kv_ref, g_ref) = refs[:n_piece], refs[n_piece:]
    first = x_refs[0][...]
    if head_ref is not None:
        first = jnp.where(pl.program_id(1) == 0, head_ref[...], first)
    x = jnp.concatenate([first] + [ref[...] for ref in x_refs[1:]], axis=0)
    xn = _rmsnorm(x, gain_ref[...])
    proj = jnp.dot(xn.astype(BF16), w_ref[...], preferred_element_type=F32)
    q_ref[...] = (proj[:, :A_WIDTH] * (A_HEAD_DIM ** -0.5)).astype(q_ref.dtype)
    kv_ref[...] = proj[:, A_WIDTH:A_WIDTH + 2 * A_KV_WIDTH]
    g_ref[...] = proj[:, A_WIDTH + 2 * A_KV_WIDTH:]


def _attn_proj_call(x2d, head, gain, w_bf16, q_dtype, nbatch, ntile, n_piece, piece):
    tm = n_piece * piece
    wcols = w_bf16.shape[1]
    per_seq = x2d.shape[0] // (nbatch * piece)
    lead = 0 if head is None else 1
    dst = lambda b, i: (b * ntile + i, 0)
    xspec = [pl.BlockSpec((piece, D_MODEL),
                          functools.partial(lambda b, i, kk: (b * per_seq + jnp.maximum(i * n_piece + kk - lead, 0), 0),
                                            kk=kk))
             for kk in range(n_piece)]
    kern = functools.partial(_attn_proj_kernel, n_piece=n_piece)
    operands = [x2d] * n_piece + [gain, w_bf16]
    if head is None:
        kern = functools.partial(kern, None)
        head_spec = []
    else:
        head_spec = [pl.BlockSpec((piece, D_MODEL), lambda b, i: (0, 0))]
        operands = [head] + operands
    n = nbatch * ntile * tm
    return pl.pallas_call(
        kern,
        grid=(nbatch, ntile),
        in_specs=head_spec + xspec + [pl.BlockSpec((1, D_MODEL), lambda b, i: (0, 0)),
                                      pl.BlockSpec((D_MODEL, wcols), lambda b, i: (0, 0))],
        out_specs=[pl.BlockSpec((tm, A_WIDTH), dst),
                   pl.BlockSpec((tm, 2 * A_KV_WIDTH), dst),
                   pl.BlockSpec((tm, A_WIDTH), dst)],
        out_shape=[jax.ShapeDtypeStruct((n, A_WIDTH), q_dtype),
                   jax.ShapeDtypeStruct((n, 2 * A_KV_WIDTH), F32),
                   jax.ShapeDtypeStruct((n, A_WIDTH), F32)],
        compiler_params=_cparams(2),
        name="attn_proj",
    )(*operands)


def _padded_kv_tiles(kv, c):
    lo = _iota((1, LANES), 1) < A_HEAD_DIM
    j = c // 2
    out = []
    for base in (0, A_KV_WIDTH):
        t = kv[:, base + j * LANES: base + (j + 1) * LANES]
        tr = pltpu.roll(t, A_HEAD_DIM, 1)
        if c % 2 == 0:
            even, odd = jnp.where(lo, t, 0.0), jnp.where(lo, 0.0, tr)
        else:
            even, odd = jnp.where(lo, tr, 0.0), jnp.where(lo, 0.0, t)
        out += [even.astype(BF16), odd.astype(BF16)]
    return out


def _mm_nt(a, b):
    return lax.dot_general(a, b, (((1,), (1,)), ((), ())), preferred_element_type=F32)


def _attn_prompt_kernel(sinks_ref, q_ref, kvc_ref, kvp_ref, g_ref, head_ref, *refs):
    nb = ATTN_BLOCKS_PER_STEP
    x_refs, (bias_ref, wout_ref, out_ref, og_ref) = refs[:nb], refs[nb:]
    i = pl.program_id(1)
    stack = 2 * BLOCK
    row, col = _iota((stack, BLOCK), 0) & (BLOCK - 1), _iota((stack, BLOCK), 1)
    upper = _iota((stack, 1), 0) >= BLOCK
    own = col <= row
    chains = [(c, idx) for c in range(A_KV_HEADS) for idx in range(2)]
    n = range(len(chains))
    cur = [_padded_kv_tiles(kvp_ref[...], c) for c in range(A_KV_HEADS)]
    for j in range(nb):
        rows = slice(j * BLOCK, (j + 1) * BLOCK)
        kvalid = (i * nb + j - 1 + own.astype(jnp.int32)) * BLOCK + col >= PAD
        prev, cur = cur, [_padded_kv_tiles(kvc_ref[rows, :], c) for c in range(A_KV_HEADS)]
        s, sink = [], []
        for c, idx in chains:
            q2 = q_ref[rows, 2 * c * LANES:(2 * c + 2) * LANES]
            q2 = jnp.concatenate([q2[:, :LANES], q2[:, LANES:]], axis=0)
            sc = jnp.where(own, _mm_nt(q2, cur[c][idx]), _mm_nt(q2, prev[c][idx]))
            bias = jnp.concatenate([bias_ref[4 * c + idx], bias_ref[4 * c + 2 + idx]], axis=0)
            s.append(jnp.where(kvalid, sc + bias, NEG))
            sink.append(jnp.where(upper, sinks_ref[4 * c + 2 + idx], sinks_ref[4 * c + idx]))
        m = [jnp.maximum(jnp.max(s[t], axis=1, keepdims=True), sink[t]) for t in n]
        p = [jnp.exp(s[t] - m[t]) for t in n]
        den = [jnp.sum(p[t], axis=1, keepdims=True) + jnp.exp(sink[t] - m[t]) for t in n]
        o = []
        for t, (c, idx) in enumerate(chains):
            pv = (jnp.dot(jnp.where(own, p[t], 0.0).astype(BF16), cur[c][2 + idx], preferred_element_type=F32)
                  + jnp.dot(jnp.where(own, 0.0, p[t]).astype(BF16), prev[c][2 + idx], preferred_element_type=F32))
            o.append(pv * (1.0 / den[t]))
        for c in range(A_KV_HEADS):
            both = o[2 * c] + o[2 * c + 1]
            for half in range(2):
                sl = slice((2 * c + half) * LANES, (2 * c + half + 1) * LANES)
                gt = g_ref[rows, sl]
                og_ref[rows, sl] = (both[half * BLOCK:(half + 1) * BLOCK] * (gt * _sigmoid(gt))).astype(BF16)
    resid = jnp.concatenate([jnp.where(i == 0, head_ref[...], x_refs[0][...])]
                            + [ref[...] for ref in x_refs[1:]], axis=0)
    out_ref[...] = resid + jnp.dot(og_ref[...], wout_ref[...], preferred_element_type=F32)


def _attn_prompt_call(sinks, q, kv, g, head, x2d, bias, wout_bf16, nbatch, nblk):
    n = q.shape[0]
    nb = ATTN_BLOCKS_PER_STEP
    nstep = nblk // nb
    rows = nb * BLOCK
    row = lambda b, i: (b * nstep + i, 0)
    prev = lambda b, i: (b * nblk + jnp.maximum(i * nb - 1, 0), 0)
    xrow = [functools.partial(lambda b, i, j: (b * (nblk - 1) + jnp.maximum(i * nb + j - 1, 0), 0), j=j)
            for j in range(nb)]
    return pl.pallas_call(
        _attn_prompt_kernel,
        grid=(nbatch, nstep),
        in_specs=[pl.BlockSpec(memory_space=pltpu.SMEM),
                  pl.BlockSpec((rows, A_WIDTH), row),
                  pl.BlockSpec((rows, 2 * A_KV_WIDTH), row),
                  pl.BlockSpec((BLOCK, 2 * A_KV_WIDTH), prev),
                  pl.BlockSpec((rows, A_WIDTH), row),
                  pl.BlockSpec((BLOCK, D_MODEL), lambda b, i: (0, 0))]
                 + [pl.BlockSpec((BLOCK, D_MODEL), xrow[j]) for j in range(nb)]
                 + [pl.BlockSpec((A_HEADS, BLOCK, BLOCK), lambda b, i: (0, 0, 0)),
                    pl.BlockSpec((A_WIDTH, D_MODEL), lambda b, i: (0, 0))],
        out_specs=pl.BlockSpec((rows, D_MODEL), row),
        out_shape=jax.ShapeDtypeStruct((n, D_MODEL), F32),
        scratch_shapes=[pltpu.VMEM((rows, A_WIDTH), BF16)],
        compiler_params=_cparams(2),
        name="attn_prompt",
    )(sinks, q, kv, kv, g, head, *([x2d] * nb), bias, wout_bf16)


SAMPLE_SB = 8
SAMPLE_KEYS = 2 * BLOCK


def _attn_sample_kernel(sinks_ref, q_ref, kvn_ref, g_ref, h_ref, ck_ref, cv_ref, bias0_ref, bias1_ref,
                        wout_ref, out_ref, nk_ref, nv_ref, og_ref, *, t_new):
    keep = ck_ref.shape[2]
    lo = _iota((1, LANES), 1) < A_HEAD_DIM
    stack = 8 * SUBLANES
    own = (_iota((stack, 1), 0) & (SUBLANES - 1)) // t_new
    piece = _iota((stack, 1), 0) // SUBLANES
    bias_refs = (bias0_ref, bias1_ref)
    n_tile = A_KV_WIDTH // LANES
    pair_ids = range(SAMPLE_SB * t_new // SUBLANES)

    bias_c, bias_n, sink = [], [], []
    for j in range(n_tile):
        heads = slice(8 * j, 8 * j + 8)
        bias_c.append(bias0_ref[heads, :, :keep].reshape(stack, keep))
        bias_n.append([ref[heads, :, keep:keep + SUBLANES].reshape(stack, SUBLANES) for ref in bias_refs])
        col = jnp.zeros((stack, 1), F32)
        for gq in range(8):
            col = jnp.where(piece == gq, sinks_ref[8 * j + gq], col)
        sink.append(col)

    chains = [(p, j, s) for p in pair_ids for j in range(n_tile) for s in range(SUBLANES // t_new)]
    qs = {}
    for p in pair_ids:
        rows = slice(p * SUBLANES, (p + 1) * SUBLANES)
        for j in range(n_tile):
            parts = []
            for gq in range(8):
                t = q_ref[rows, (4 * j + gq // 2) * LANES:(4 * j + gq // 2 + 1) * LANES]
                want_lo = gq < 4
                if (gq % 2 == 0) != want_lo:
                    t = pltpu.roll(t, A_HEAD_DIM, 1)
                parts.append(jnp.where(lo, t, 0.0) if want_lo else jnp.where(lo, 0.0, t))
            qs[p, j] = jnp.concatenate(parts, axis=0).astype(BF16)

    def kv_tiles(p, j, s, base):
        seq = p * (SUBLANES // t_new) + s
        cache = (ck_ref if base == 0 else cv_ref)[seq][j * LANES:(j + 1) * LANES, :].astype(BF16)
        new = kvn_ref[p * SUBLANES:(p + 1) * SUBLANES, base + j * LANES:base + (j + 1) * LANES].astype(BF16)
        return cache, new

    sc, sn = [], []
    for p, j, s in chains:
        kc, kn = kv_tiles(p, j, s, 0)
        sc.append(jnp.dot(qs[p, j], kc, preferred_element_type=F32) + bias_c[j])
        sn.append(_mm_nt(qs[p, j], kn) + bias_n[j][s])
    mx = [jnp.maximum(jnp.maximum(jnp.max(sc[i], axis=1, keepdims=True),
                                  jnp.max(sn[i], axis=1, keepdims=True)), sink[chains[i][1]])
          for i in range(len(chains))]
    pc = [jnp.exp(sc[i] - mx[i]) for i in range(len(chains))]
    pn = [jnp.exp(sn[i] - mx[i]) for i in range(len(chains))]
    den = [jnp.sum(pc[i], axis=1, keepdims=True) + jnp.sum(pn[i], axis=1, keepdims=True)
           + jnp.exp(sink[chains[i][1]] - mx[i]) for i in range(len(chains))]
    outs = {}
    for i, (p, j, s) in enumerate(chains):
        vc, vn = kv_tiles(p, j, s, A_KV_WIDTH)
        o = (_mm_nt(pc[i].astype(BF16), vc)
             + jnp.dot(pn[i].astype(BF16), vn, preferred_element_type=F32)) * (1.0 / den[i])
        outs[p, j] = o if s == 0 else jnp.where(own == s, o, outs[p, j])
    for p in pair_ids:
        rows = slice(p * SUBLANES, (p + 1) * SUBLANES)
        for j in range(n_tile):
            o = outs[p, j]
            for gg in range(4):
                even = o[2 * gg * SUBLANES:(2 * gg + 1) * SUBLANES]
                odd = o[(2 * gg + 1) * SUBLANES:(2 * gg + 2) * SUBLANES]
                if gg < 2:
                    tile = jnp.where(lo, even, pltpu.roll(odd, A_HEAD_DIM, 1))
                else:
                    tile = jnp.where(lo, pltpu.roll(even, A_HEAD_DIM, 1), odd)
                sl = slice((4 * j + gg) * LANES, (4 * j + gg + 1) * LANES)
                gt = g_ref[rows, sl]
                og_ref[rows, sl] = tile * (gt * _sigmoid(gt))
    lane_pos = _iota((SUBLANES, keep), 1)
    old = _iota((1, keep), 1) < keep - t_new
    for p in pair_ids:
        new8 = kvn_ref[p * SUBLANES:(p + 1) * SUBLANES, :]
        for s in range(SUBLANES // t_new):
            seq = p * (SUBLANES // t_new) + s
            sel = (lane_pos == _iota((SUBLANES, keep), 0) + (keep - t_new - s * t_new)).astype(F32)
            placed = _mm3(new8, sel, TN)
            for ref_in, ref_out, base in ((ck_ref, nk_ref, 0), (cv_ref, nv_ref, A_KV_WIDTH)):
                shifted = pltpu.roll(ref_in[seq], keep - t_new, 1)
                ref_out[seq] = jnp.where(old, shifted, placed[base:base + A_KV_WIDTH])
    out_ref[...] = h_ref[...] + jnp.dot(og_ref[...].astype(BF16), wout_ref[...],
                                        preferred_element_type=F32)


def _attn_sample_call(sinks, q, kv, g, h, cache_k, cache_v, bias0, bias1, wout_bf16, t_new):
    nseq, keep = cache_k.shape[0], cache_k.shape[2]
    rows = SAMPLE_SB * t_new
    row = lambda i: (i, 0)
    cspec = pl.BlockSpec((SAMPLE_SB, A_KV_WIDTH, keep), lambda i: (i, 0, 0))
    bspec = pl.BlockSpec((A_HEADS, SUBLANES, SAMPLE_KEYS), lambda i: (0, 0, 0))
    return pl.pallas_call(
        functools.partial(_attn_sample_kernel, t_new=t_new),
        grid=(nseq // SAMPLE_SB,),
        in_specs=[pl.BlockSpec(memory_space=pltpu.SMEM),
                  pl.BlockSpec((rows, A_WIDTH), row),
                  pl.BlockSpec((rows, 2 * A_KV_WIDTH), row),
                  pl.BlockSpec((rows, A_WIDTH), row),
                  pl.BlockSpec((rows, D_MODEL), row),
                  cspec, cspec, bspec, bspec,
                  pl.BlockSpec((A_WIDTH, D_MODEL), lambda i: (0, 0))],
        out_specs=[pl.BlockSpec((rows, D_MODEL), row), cspec, cspec],
        out_shape=[jax.ShapeDtypeStruct((nseq * t_new, D_MODEL), F32),
                   jax.ShapeDtypeStruct(cache_k.shape, F32),
                   jax.ShapeDtypeStruct(cache_v.shape, F32)],
        scratch_shapes=[pltpu.VMEM((rows, A_WIDTH), F32)],
        compiler_params=_cparams(1),
        name="attn_sample",
    )(sinks, q, kv, g, h, cache_k, cache_v, bias0, bias1, wout_bf16)


def _rwkv_proj_kernel(h_ref, shift_ref, gain_ref, mu_ref, win_ref, w0_ref, w1_ref, w2_ref,
                      a0_ref, a1_ref, a2_ref,
                      r_ref, k_ref, v_ref, g_ref, ld_ref, a_ref, xn_ref, *scratch, seq_len):
    xn = _rmsnorm(h_ref[...], gain_ref[...])
    tm = xn.shape[0]
    rolled = pltpu.roll(xn, 1, 0)
    row = _iota((tm, 1), 0)
    if seq_len is None:
        carry_ref, = scratch

        @pl.when(pl.program_id(1) == 0)
        def _():
            carry_ref[...] = shift_ref[0]

        xprev = jnp.where(row == 0, carry_ref[...], rolled)
        carry_ref[...] = xn[tm - 1:tm, :]
        xn_ref[0] = xn[tm - 1:tm, :]
    else:
        xprev = jnp.where(row % seq_len == 0, shift_ref[...], rolled)
        xn_ref[...] = xn
    dx = xprev - xn

    def mix(c):
        return (xn + dx * mu_ref[c:c + 1, :]).astype(BF16)

    for c, o_ref in enumerate((r_ref, k_ref, v_ref, g_ref)):
        o_ref[...] = jnp.dot(mix(c), win_ref[c], preferred_element_type=F32)
    lw = jnp.tanh(jnp.dot(mix(4), w1_ref[...], preferred_element_type=F32))
    z = w0_ref[...] + jnp.dot(lw.astype(BF16), w2_ref[...], preferred_element_type=F32)
    ld_ref[...] = -math.exp(-0.5) * _sigmoid(z)
    la = jnp.dot(mix(5), a1_ref[...], preferred_element_type=F32)
    a_ref[...] = _sigmoid(a0_ref[...] + jnp.dot(la.astype(BF16), a2_ref[...],
                                                preferred_element_type=F32))


def _rwkv_proj_call(h, shift, p, nbatch, ntile, tm, seq_len):
    n = h.shape[0]
    row = lambda b, i: (b * ntile + i, 0)
    full2 = lambda b, i: (0, 0)
    if seq_len is None:
        shift_spec = pl.BlockSpec((1, 1, D_MODEL), lambda b, i: (b, 0, 0))
        xn_spec = pl.BlockSpec((1, 1, D_MODEL), lambda b, i: (b, 0, 0))
        xn_shape = jax.ShapeDtypeStruct((nbatch, 1, D_MODEL), F32)
        scratch = [pltpu.VMEM((1, D_MODEL), F32)]
    else:
        shift_spec = pl.BlockSpec((tm, D_MODEL), row)
        xn_spec = pl.BlockSpec((tm, D_MODEL), row)
        xn_shape = jax.ShapeDtypeStruct((n, D_MODEL), F32)
        scratch = []
    lora = p["w1"].shape[1]
    big = jax.ShapeDtypeStruct((n, D_MODEL), F32)
    return pl.pallas_call(
        functools.partial(_rwkv_proj_kernel, seq_len=seq_len),
        grid=(nbatch, ntile),
        in_specs=[pl.BlockSpec((tm, D_MODEL), row),
                  shift_spec,
                  pl.BlockSpec((1, D_MODEL), full2),
                  pl.BlockSpec(p["mu"].shape, full2),
                  pl.BlockSpec(p["w_in"].shape, lambda b, i: (0, 0, 0)),
                  pl.BlockSpec((1, D_MODEL), full2),
                  pl.BlockSpec((D_MODEL, lora), full2),
                  pl.BlockSpec((lora, D_MODEL), full2),
                  pl.BlockSpec((1, D_MODEL), full2),
                  pl.BlockSpec((D_MODEL, lora), full2),
                  pl.BlockSpec((lora, D_MODEL), full2)],
        out_specs=[pl.BlockSpec((tm, D_MODEL), row)] * 6 + [xn_spec],
        out_shape=[big] * 6 + [xn_shape],
        scratch_shapes=scratch,
        compiler_params=_cparams(2),
        name="rwkv_proj",
    )(h, shift, p["gain"], p["mu"], p["w_in"], p["w0"], p["w1"], p["w2"], p["a0"], p["a1"], p["a2"])


NN = (((1,), (0,)), ((), ()))
NT = (((1,), (1,)), ((), ()))
TN = (((0,), (0,)), ((), ()))


def _split(x):
    hi = x.astype(BF16)
    return hi, (x - hi.astype(F32)).astype(BF16)


def _mm(a, b, dn):
    return lax.dot_general(a, b, dn, preferred_element_type=F32)


def _mm3s(a, b, dn):
    (ah, al), (bh, bl) = a, b
    if dn == TN:
        both = _mm(jnp.concatenate([ah, al], axis=1), bh, dn)
        m = ah.shape[1]
    else:
        both = _mm(jnp.concatenate([ah, al], axis=0), bh, dn)
        m = ah.shape[0]
    return both[:m] + both[m:] + _mm(ah, bl, dn)


def _mm3(a, b, dn):
    return _mm3s(_split(a), _split(b), dn)


def _mm3_top(x_hi, x_lo_top, b, dn):
    bh, bl = b
    c = x_lo_top.shape[0]
    both = _mm(jnp.concatenate([x_hi, x_lo_top], axis=0), bh, dn)
    return both[:c] + both[2 * c:] + _mm(x_hi[:c], bl, dn), both[c:2 * c]


def _seg_sums(xs, ones_bf16, exact):
    rows = xs[0].shape[0]
    stacked = jnp.concatenate(xs, axis=0)
    if exact:
        hi, lo = _split(stacked)
        out = _mm(lo, ones_bf16, NN) + _mm(hi, ones_bf16, NN)
    else:
        out = _mm(stacked.astype(BF16), ones_bf16, NN)
    return [out[i * rows:(i + 1) * rows] for i in range(len(xs))]


def _lane_lo():
    return _iota((1, LANES), 1) < R_HEAD_DIM


def _same_half():
    return (_iota((LANES, LANES), 0) < R_HEAD_DIM) == (_iota((LANES, LANES), 1) < R_HEAD_DIM)


def _ones_blk():
    return _same_half().astype(BF16)


def _bd(z):
    lo = _lane_lo()
    return jnp.concatenate([jnp.where(lo, z, 0.0), jnp.where(lo, 0.0, z)], axis=0)


def _bd_swap(z):
    lo = _lane_lo()
    return jnp.concatenate([jnp.where(lo, 0.0, z), jnp.where(lo, z, 0.0)], axis=0)


def _split_map(z, f):
    hi, lw = _split(z)
    return f(hi), f(lw)


def _wkv_batch_stage(r, k, v, a, ld, kkg, kag, fillers=(), early=None):
    c = r[0].shape[0]
    pairs = range(len(r))
    half = R_HEAD_DIM
    lo = _lane_lo()
    t_row, t_col = _iota((c, LANES), 0), _iota((c, LANES), 1) & (half - 1)
    strict, incl = t_row > t_col, t_row >= t_col
    tri = (_iota((c, c), 0) >= _iota((c, c), 1)).astype(BF16)
    ones_blk = _ones_blk()
    eye_pair = (_iota((c, LANES), 0) == (_iota((c, LANES), 1) & (half - 1))).astype(F32)

    kkx = [k[j] * kkg[j] for j in pairs]
    ssq = _seg_sums([kkx[j] * kkx[j] for j in pairs], ones_blk, exact=True)
    ld_hi, ld_lo = _split(jnp.concatenate(ld, axis=1))
    cs_all = _mm(tri, ld_lo, NN) + _mm(tri, ld_hi, NN)
    cs = [cs_all[:, j * LANES:(j + 1) * LANES] for j in pairs]
    tot = [cs[j][c - 1:c, :] for j in pairs]
    x_hi, at_lo, kh_all, bke, vbd_s, lk_a, lk_r, lb_a, lb_r = ([] for _ in range(9))
    for j in pairs:
        kk = kkx[j] / jnp.maximum(jnp.sqrt(ssq[j]), 1e-12)
        kh = k[j] * (1.0 + (a[j] - 1.0) * kag[j])
        bv = kk * a[j]
        e_neg = jnp.exp(-cs[j])
        e_end = jnp.exp(tot[j] - cs[j])
        at, rt = -kk * jnp.exp(cs[j] - ld[j]), r[j] * jnp.exp(cs[j])
        at_hi, at_l = _split(at)
        x_hi.append(jnp.concatenate([at_hi, rt.astype(BF16)], axis=0))
        at_lo.append(at_l)
        kh_all.append(kh)
        bt, kt = _split(bv * e_neg), _split(kh * e_neg)
        y_hi, y_lo = (jnp.concatenate([p, q], axis=0) for p, q in zip(bt, kt))
        bke.append(jnp.concatenate([bv * e_end, kh * e_end], axis=0))
        vbd_s.append(_split_map(v[j], _bd))
        xa, xb = jnp.where(lo, x_hi[j], 0.0), jnp.where(lo, 0.0, x_hi[j])
        both = _mm(jnp.concatenate([xa, xb, jnp.where(lo, at_l, 0.0), jnp.where(lo, 0.0, at_l)], axis=0),
                   y_hi, NT)
        corr = _mm(jnp.concatenate([xa[:c], xb[:c]], axis=0), y_lo, NT)
        ga_a = pltpu.roll(both[:c] + both[4 * c:5 * c] + corr[:c], half, 1)
        ga_r = pltpu.roll(both[c:2 * c], half, 1)
        gb_a = both[2 * c:3 * c] + both[5 * c:] + corr[c:]
        gb_r = both[3 * c:4 * c]
        lk_a.append(jnp.where(strict, jnp.where(lo, ga_a, gb_a), 0.0))
        lk_r.append(jnp.where(incl, jnp.where(lo, ga_r, gb_r), 0.0))
        lb_a.append(jnp.where(strict, jnp.where(lo, gb_a, ga_a), 0.0))
        lb_r.append(jnp.where(incl, jnp.where(lo, gb_r, ga_r), 0.0))
    fillers = list(fillers)
    if early is not None:
        early.update(x_hi=x_hi, at_lo=at_lo)
    pw = lb_a
    acc = [eye_pair + pw[j] for j in pairs]
    pw = [_mm(pw[j].astype(BF16), _bd(pw[j].astype(BF16)), NN) for j in pairs]
    from_v = []
    for j in pairs:
        lk_hi, lk_lo = _split(lk_a[j])
        from_v.append(_mm3_top(jnp.concatenate([lk_hi, lk_r[j].astype(BF16)], axis=0), lk_lo, vbd_s[j], NN))
    for _ in range(int(math.log2(c)) - 2):
        both = [_mm(jnp.concatenate([pw[j], acc[j]], axis=0).astype(BF16), _bd(pw[j].astype(BF16)), NN)
                for j in pairs]
        if fillers:
            fillers.pop(0)()
        pw = [both[j][:c] for j in pairs]
        acc = [acc[j] + both[j][c:] for j in pairs]
    tinv = [acc[j] + _mm(acc[j].astype(BF16), _bd(pw[j].astype(BF16)), NN) for j in pairs]
    for fill in fillers:
        fill()
    return dict(x_hi=x_hi, at_lo=at_lo, kh=kh_all, lb_r=lb_r, tinv=tinv, from_v=from_v, bke=bke,
                tot=tot, ones_blk=ones_blk)


def _wkv_finish(stage, from_state_a, from_state_r):
    pairs = range(len(from_state_a))
    u = [_mm3s(_split(stage["tinv"][j]),
               _split_map(from_state_a[j] + stage["from_v"][j][0], _bd_swap), NN) for j in pairs]
    y = [from_state_r[j] + stage["from_v"][j][1]
         + _mm(stage["lb_r"][j].astype(BF16), _bd_swap(u[j].astype(BF16)), NN) for j in pairs]
    return u, y


def _wkv_gate(y, r, kh, v, g, rk, lng, lnb, ones_blk):
    n = len(y)
    pairs = range(n)
    inv_n = 1.0 / R_HEAD_DIM
    sums = _seg_sums([r[j] * kh[j] * rk[j] for j in pairs] + list(y), ones_blk, exact=False)
    rkk, mean = sums[:n], [s * inv_n for s in sums[n:]]
    d = [y[j] - mean[j] for j in pairs]
    var = [s * inv_n for s in _seg_sums([d[j] * d[j] for j in pairs], ones_blk, exact=False)]
    return [((d[j] * lax.rsqrt(var[j] + GN_EPS) * lng[j] + lnb[j] + rkk[j] * v[j])
             * (g[j] * _sigmoid(g[j]))).astype(BF16) for j in pairs]


def _wkv_chunk_kernel(r_ref, k_ref, v_ref, a_ref, ld_ref, g_ref, kkg_ref, kag_ref, rk_ref, lng_ref, lnb_ref,
                      z_ref, sout_ref, st_ref):
    c = WKV_CHUNK
    n_sub = r_ref.shape[0] // c
    npair = r_ref.shape[1] // LANES
    pairs = range(npair)
    items = [(s, j) for s in range(n_sub) for j in pairs]

    @pl.when(pl.program_id(2) == 0)
    def _():
        st_ref[...] = jnp.zeros_like(st_ref)

    def tile(ref, it):
        s, j = it
        return ref[s * c:(s + 1) * c, j * LANES:(j + 1) * LANES]

    def par(ref):
        return [ref[:, j * LANES:(j + 1) * LANES] for _, j in items]

    r, k, v, a, ld, g = ([tile(ref, it) for it in items] for ref in (r_ref, k_ref, v_ref, a_ref, ld_ref, g_ref))
    st = [st_ref[j] for j in pairs]
    from_state = {}
    stage = {}

    def state_products(group):
        def run():
            for j in group:
                from_state[j] = _mm3_top(stage["x_hi"][j], stage["at_lo"][j], _split(st[j]), NN)
        return run

    n_fill = 4
    groups = [list(pairs)[i::n_fill] for i in range(n_fill)]
    stage.update(_wkv_batch_stage(r, k, v, a, ld, par(kkg_ref), par(kag_ref),
                                  fillers=[state_products(grp) for grp in groups if grp], early=stage))
    same_half = _same_half()
    eye_full = _iota((LANES, LANES), 0) == _iota((LANES, LANES), 1)
    ys = []
    for s in range(n_sub):
        idx = [s * npair + j for j in pairs]
        if s > 0:
            for j in pairs:
                from_state[j] = _mm3_top(stage["x_hi"][idx[j]], stage["at_lo"][idx[j]], _split(st[j]), NN)
        sub = {key: [stage[key][i] for i in idx] for key in ("tinv", "from_v", "lb_r")}
        us, ys_s = _wkv_finish(sub, [from_state[j][0] for j in pairs], [from_state[j][1] for j in pairs])
        upd = [_mm3(stage["bke"][idx[j]], jnp.concatenate([us[j], v[idx[j]]], axis=0), TN) for j in pairs]
        ys += ys_s
        for j in pairs:
            w_col = jnp.sum(jnp.where(eye_full, jnp.exp(stage["tot"][idx[j]]), 0.0), axis=1, keepdims=True)
            st[j] = w_col * st[j] + jnp.where(same_half, upd[j], 0.0)
    zs = _wkv_gate(ys, r, stage["kh"], v, g, par(rk_ref), par(lng_ref), par(lnb_ref), stage["ones_blk"])
    for i, (s, j) in enumerate(items):
        z_ref[s * c:(s + 1) * c, j * LANES:(j + 1) * LANES] = zs[i]
    for j in pairs:
        st_ref[j] = st[j]
        sout_ref[0, j] = st[j]


def _wkv_chunk_call(r, k, v, a, ld, g, p, nbatch, seq):
    rows = WKV_CHUNK * WKV_CHUNKS_PER_STEP
    nstep = seq // rows
    npair = D_MODEL // LANES
    tile = pl.BlockSpec((rows, D_MODEL), lambda b, j, t: (b * nstep + t, 0))
    par = pl.BlockSpec((1, D_MODEL), lambda b, j, t: (0, 0))
    return pl.pallas_call(
        _wkv_chunk_kernel,
        grid=(nbatch, 1, nstep),
        in_specs=[tile] * 6 + [par] * 5,
        out_specs=[tile, pl.BlockSpec((1, npair, LANES, LANES), lambda b, j, t: (b, 0, 0, 0))],
        out_shape=[jax.ShapeDtypeStruct((nbatch * seq, D_MODEL), BF16),
                   jax.ShapeDtypeStruct((nbatch, npair, LANES, LANES), F32)],
        scratch_shapes=[pltpu.VMEM((npair, LANES, LANES), F32)],
        compiler_params=_cparams(3),
        name="wkv_chunk",
    )(r, k, v, a, ld, g, p["k_k"], p["k_a"], p["r_k"], p["ln_g"], p["ln_b"])


WKV_LANES_UNROLL = 4


def _wkv_lanes_kernel(r_ref, k_ref, v_ref, a_ref, ld_ref, g_ref, kkg_ref, kag_ref, rk_ref, lng_ref, lnb_ref,
                      s_ref, z_ref, sout_ref, prep_ref, y_ref, *, seq_len):
    n = R_HEAD_DIM
    nseq = s_ref.shape[3]
    eye = _iota((LANES, LANES), 0) == _iota((LANES, LANES), 1)

    def column(ref):
        return jnp.sum(jnp.where(eye, ref[...], 0.0), axis=1, keepdims=True)

    kkg, kag, rk, lng, lnb = (column(ref) for ref in (kkg_ref, kag_ref, rk_ref, lng_ref, lnb_ref))

    def token(ref, t):
        return ref[pl.ds(t, nseq, stride=seq_len), :].T

    bonus = []
    for t in range(seq_len):
        r, k, v, a = (token(ref, t) for ref in (r_ref, k_ref, v_ref, a_ref))
        w = jnp.exp(token(ld_ref, t))
        kkx = k * kkg
        kh = k * (1.0 + (a - 1.0) * kag)
        rkk = r * kh * rk
        tiles = []
        for hh in range(2):
            rows = slice(hh * n, (hh + 1) * n)
            nrm = jnp.sqrt(jnp.sum(kkx[rows] * kkx[rows], axis=0, keepdims=True))
            kk = kkx[rows] / jnp.maximum(nrm, 1e-12)
            for q, val in enumerate((w[rows], -kk, kk * a[rows], kh[rows], r[rows], v[rows])):
                prep_ref[q, t, hh] = val
            tiles.append(jnp.sum(rkk[rows], axis=0, keepdims=True) * v[rows])
        bonus.append(tiles)

    for hh in range(2):
        def advance(i, carry, hh=hh):
            for u in range(WKV_LANES_UNROLL):
                vi = i * WKV_LANES_UNROLL + u
                slab = s_ref[hh, vi]
                for t in range(seq_len):
                    w, av, bv, kh, r = (prep_ref[q, t, hh] for q in range(5))
                    vrow = prep_ref[5, t, hh, pl.ds(vi, 1), :]
                    sa = jnp.sum(slab * av, axis=0, keepdims=True)
                    slab = slab * w + sa * bv + vrow * kh
                    y_ref[t, hh, pl.ds(vi, 1), :] = jnp.sum(slab * r, axis=0, keepdims=True)
                sout_ref[hh, vi] = slab
            return carry

        lax.fori_loop(0, n // WKV_LANES_UNROLL, advance, 0)

    for t in range(seq_len):
        parts = []
        for hh in range(2):
            rows = slice(hh * n, (hh + 1) * n)
            y = y_ref[t, hh]
            d = y - jnp.mean(y, axis=0, keepdims=True)
            var = jnp.mean(d * d, axis=0, keepdims=True)
            parts.append(d * lax.rsqrt(var + GN_EPS) * lng[rows] + lnb[rows] + bonus[t][hh])
        g = token(g_ref, t)
        z = jnp.concatenate(parts, axis=0) * (g * _sigmoid(g))
        z_ref[pl.ds(t, nseq, stride=seq_len), :] = z.T


def _wkv_lanes_call(r, k, v, a, ld, g, p, state_hvkb, seq_len):
    n = r.shape[0]
    nseq = state_hvkb.shape[3]
    tile = pl.BlockSpec((n, LANES), lambda j: (0, j))
    par = pl.BlockSpec((1, LANES), lambda j: (0, j))
    sspec = pl.BlockSpec((2, R_HEAD_DIM, R_HEAD_DIM, nseq), lambda j: (j, 0, 0, 0))
    return pl.pallas_call(
        functools.partial(_wkv_lanes_kernel, seq_len=seq_len),
        grid=(D_MODEL // LANES,),
        in_specs=[tile] * 6 + [par] * 5 + [sspec],
        out_specs=[tile, sspec],
        out_shape=[jax.ShapeDtypeStruct((n, D_MODEL), F32),
                   jax.ShapeDtypeStruct(state_hvkb.shape, F32)],
        scratch_shapes=[pltpu.VMEM((6, seq_len, 2, R_HEAD_DIM, nseq), F32),
                        pltpu.VMEM((seq_len, 2, R_HEAD_DIM, nseq), F32)],
        compiler_params=_cparams(1),
        name="wkv_lanes",
    )(r, k, v, a, ld, g, p["k_k"], p["k_a"], p["r_k"], p["ln_g"], p["ln_b"], state_hvkb)


RWKV_OUT_PIECES = 8


def _rwkv_out_kernel(*refs):
    n = RWKV_OUT_PIECES
    z_refs, h_refs, (wout_ref, fg_ref, out_ref) = refs[:n], refs[n:2 * n], refs[2 * n:]
    z = jnp.concatenate([ref[...].astype(BF16) for ref in z_refs], axis=0)
    h = jnp.concatenate([ref[...] for ref in h_refs], axis=0)
    h2 = h + jnp.dot(z, wout_ref[...], preferred_element_type=F32)
    out_ref[...] = _rmsnorm(h2, fg_ref[...])


def _rwkv_out_call(z, h, p, nbatch, ntile, tm, pieces_per_batch, skip):
    piece = tm // RWKV_OUT_PIECES
    dst = lambda b, i: (b * ntile + i, 0)
    pspec = [pl.BlockSpec((piece, D_MODEL),
                          functools.partial(lambda b, i, kk: (b * pieces_per_batch + i * RWKV_OUT_PIECES + skip + kk, 0),
                                            kk=kk))
             for kk in range(RWKV_OUT_PIECES)]
    return pl.pallas_call(
        _rwkv_out_kernel,
        grid=(nbatch, ntile),
        in_specs=pspec + pspec + [pl.BlockSpec((D_MODEL, D_MODEL), lambda b, i: (0, 0)),
                                  pl.BlockSpec((1, D_MODEL), lambda b, i: (0, 0))],
        out_specs=pl.BlockSpec((tm, D_MODEL), dst),
        out_shape=jax.ShapeDtypeStruct((nbatch * ntile * tm, D_MODEL), F32),
        compiler_params=_cparams(2),
        name="rwkv_out",
    )(*([z] * RWKV_OUT_PIECES), *([h] * RWKV_OUT_PIECES), p["w_out"], p["final_gain"])


def _prompt_bucket():
    assert WINDOW == BLOCK
    rel = (np.arange(BLOCK)[:, None] - np.arange(BLOCK)[None, :]) % BLOCK
    return _t5_bucket_np(rel)


def _sample_bucket(keep, t_new, slot):
    t = (np.arange(SUBLANES) % t_new)[:, None]
    j = np.arange(SAMPLE_KEYS)[None, :]
    own = j - keep - slot * t_new
    rel = np.where(j < keep, keep + t - j, t - own)
    ok = (rel >= 0) & (rel < WINDOW) & ((j < keep) | ((own >= 0) & (own < t_new)))
    return np.where(ok, _t5_bucket_np(rel), -1).astype(np.int32)


def kernel(x_prompt, x_sample, cache_win_k, cache_win_v, state_wkv, state_shift, meta_tokens, rel_bias_table, norm_gain, final_gain, attn_w_in, attn_sinks, attn_w_out, rwkv_mu, rwkv_w_in, rwkv_w0, rwkv_w1, rwkv_w2, rwkv_a0, rwkv_a1, rwkv_a2, rwkv_k_k, rwkv_k_a, rwkv_r_k, rwkv_ln_gamma, rwkv_ln_beta, rwkv_w_out):
    nb, seq, _ = x_prompt.shape
    ns, t_new, _ = x_sample.shape
    keep = cache_win_k.shape[2]
    lp = seq + BLOCK
    nblk = lp // BLOCK
    row = lambda x: x.reshape(1, D_MODEL)

    w_in0 = attn_w_in[0].astype(BF16)
    w_out0 = attn_w_out[0].astype(BF16)
    gain0 = row(norm_gain[0])
    sinks = attn_sinks[0]
    rp = dict(gain=row(norm_gain[1]), mu=rwkv_mu[0], w_in=rwkv_w_in[0].astype(BF16),
              w0=row(rwkv_w0[0]), w1=rwkv_w1[0].astype(BF16), w2=rwkv_w2[0].astype(BF16),
              a0=row(rwkv_a0[0]), a1=rwkv_a1[0].astype(BF16), a2=rwkv_a2[0].astype(BF16),
              k_k=row(rwkv_k_k[0]), k_a=row(rwkv_k_a[0]), r_k=row(rwkv_r_k[0]), ln_g=row(rwkv_ln_gamma[0]),
              ln_b=row(rwkv_ln_beta[0]), w_out=rwkv_w_out[0].astype(BF16),
              final_gain=row(final_gain))

    bias_p, *bias_s = _bias_call(rel_bias_table, [_prompt_bucket()]
                                 + [_sample_bucket(keep, t_new, slot) for slot in range(2)])

    head = jnp.concatenate([jnp.zeros((PAD, D_MODEL), F32), meta_tokens.astype(F32)], axis=0)
    xp = x_prompt.reshape(nb * seq, D_MODEL)
    q, kv, g = _attn_proj_call(xp, head, gain0, w_in0, BF16, nb, lp // ATTN_PROJ_ROWS,
                               ATTN_PROJ_ROWS // BLOCK, BLOCK)
    h1 = _attn_prompt_call(sinks, q, kv, g, head, xp, bias_p, w_out0, nb, nblk)
    kv3 = kv.reshape(nb, lp, 2 * A_KV_WIDTH)[:, lp - WINDOW:, :]
    win_k_p = kv3[:, :, :A_KV_WIDTH].reshape(1, nb, WINDOW, A_KV_HEADS, A_HEAD_DIM)
    win_v_p = kv3[:, :, A_KV_WIDTH:].reshape(1, nb, WINDOW, A_KV_HEADS, A_HEAD_DIM)

    shift0 = jnp.zeros((nb, 1, D_MODEL), F32)
    r, k, v, g1, ld, a, xlast = _rwkv_proj_call(h1, shift0, rp, nb, lp // RWKV_PROJ_ROWS, RWKV_PROJ_ROWS, None)
    z, st = _wkv_chunk_call(r, k, v, a, ld, g1, rp, nb, lp)
    y_prompt = _rwkv_out_call(z, h1, rp, nb, seq // RWKV_OUT_ROWS, RWKV_OUT_ROWS,
                              lp * RWKV_OUT_PIECES // RWKV_OUT_ROWS, BLOCK * RWKV_OUT_PIECES // RWKV_OUT_ROWS)
    y_prompt = y_prompt.reshape(nb, seq, D_MODEL)
    st = st.reshape(nb, D_MODEL // LANES, 2, R_HEAD_DIM, 2, R_HEAD_DIM)
    st = jnp.stack([st[:, :, 0, :, 0, :], st[:, :, 1, :, 1, :]], axis=2)
    wkv_p = jnp.swapaxes(st, -1, -2).reshape(1, nb, R_HEADS, R_HEAD_DIM, R_HEAD_DIM)
    shift_p = xlast.reshape(1, nb, D_MODEL)

    xs = x_sample.reshape(ns * t_new, D_MODEL)
    qs, kvs, gs = _attn_proj_call(xs, None, gain0, w_in0, F32, 1, 1, 1, ns * t_new)
    ck = jnp.swapaxes(cache_win_k[0].reshape(ns, keep, A_KV_WIDTH), 1, 2)
    cv = jnp.swapaxes(cache_win_v[0].reshape(ns, keep, A_KV_WIDTH), 1, 2)
    h1s, nk, nv = _attn_sample_call(sinks, qs, kvs, gs, xs, ck, cv, bias_s[0], bias_s[1], w_out0, t_new)
    win_k_s = jnp.swapaxes(nk, 1, 2).reshape(1, ns, keep, A_KV_HEADS, A_HEAD_DIM)
    win_v_s = jnp.swapaxes(nv, 1, 2).reshape(1, ns, keep, A_KV_HEADS, A_HEAD_DIM)

    shift_rows = jnp.repeat(state_shift[0], t_new, axis=0)
    tms = 256
    rs, ks, vs, g1s, lds, as_, xns = _rwkv_proj_call(h1s, shift_rows, rp, 1, ns * t_new // tms, tms, t_new)
    zs, st_s = _wkv_lanes_call(rs, ks, vs, as_, lds, g1s, rp, jnp.transpose(state_wkv[0], (1, 2, 3, 0)), t_new)
    y_sample = _rwkv_out_call(zs, h1s, rp, 1, 1, ns * t_new, RWKV_OUT_PIECES, 0)
    y_sample = y_sample.reshape(ns, t_new, D_MODEL)
    wkv_s = jnp.transpose(st_s, (3, 0, 1, 2))[None]
    shift_s = xns.reshape(ns, t_new, D_MODEL)[:, t_new - 1][None]

    return (y_prompt, y_sample, win_k_p, win_v_p, wkv_p, shift_p, win_k_s, win_v_s, wkv_s, shift_s)
```

```python
import functools
import math

import numpy as np
import jax
import jax.numpy as jnp
from jax import lax
from jax.experimental import pallas as pl
from jax.experimental.pallas import tpu as pltpu

F32 = jnp.float32
BF16 = jnp.bfloat16

D_MODEL = 1024
N_META = 16
RMS_EPS = 1e-6
A_HEADS = 16
A_KV_HEADS = 4
A_HEAD_DIM = 64
A_WIDTH = A_HEADS * A_HEAD_DIM
A_KV_WIDTH = A_KV_HEADS * A_HEAD_DIM
WINDOW = 128
BLOCK = 128
N_BUCKETS = 32
MAX_DISTANCE = 128
R_HEAD_DIM = 64
R_HEADS = D_MODEL // R_HEAD_DIM
GN_EPS = 64e-5

LANES = 128
SUBLANES = 8
PAD = BLOCK - N_META
NEG = -1e30
WKV_CHUNK = 64
WKV_CHUNKS_PER_STEP = 3
ATTN_BLOCKS_PER_STEP = 3
ATTN_PROJ_ROWS = 384
RWKV_PROJ_ROWS = 528
RWKV_OUT_ROWS = 1024
VMEM_LIMIT = 56 * 1024 * 1024


def _cparams(n_axes):
    return pltpu.CompilerParams(dimension_semantics=("arbitrary",) * n_axes,
                                vmem_limit_bytes=VMEM_LIMIT)


def _rmsnorm(x, gain):
    return x * lax.rsqrt(jnp.mean(x * x, axis=-1, keepdims=True) + RMS_EPS) * gain


def _sigmoid(x):
    return 1.0 / (1.0 + jnp.exp(-x))


def _iota(shape, dim):
    return lax.broadcasted_iota(jnp.int32, shape, dim)


def _t5_bucket_np(rel):
    n = np.maximum(rel, 0)
    max_exact = N_BUCKETS // 2
    nf = np.maximum(n, max_exact).astype(np.float32)
    scale = np.float32(math.log(MAX_DISTANCE / max_exact))
    large = max_exact + (np.log(nf / np.float32(max_exact)) / scale
                         * np.float32(N_BUCKETS - max_exact)).astype(np.int32)
    large = np.minimum(large, N_BUCKETS - 1)
    return np.where(n < max_exact, n, large).astype(np.int32)


def _bias_kernel(table_ref, *refs):
    h = pl.program_id(0)
    n = len(refs) // 2
    for bucket_ref, out_ref in zip(refs[:n], refs[n:]):
        bk = bucket_ref[...]
        acc = jnp.full(bk.shape, NEG, F32)
        for b in range(N_BUCKETS):
            acc = jnp.where(bk == b, table_ref[b, h], acc)
        out_ref[0] = acc


def _bias_call(table, buckets_np):
    return pl.pallas_call(
        _bias_kernel,
        grid=(A_HEADS,),
        in_specs=[pl.BlockSpec(memory_space=pltpu.SMEM)]
                 + [pl.BlockSpec(bk.shape, lambda h: (0, 0)) for bk in buckets_np],
        out_specs=[pl.BlockSpec((1,) + bk.shape, lambda h: (h, 0, 0)) for bk in buckets_np],
        out_shape=[jax.ShapeDtypeStruct((A_HEADS,) + bk.shape, F32) for bk in buckets_np],
        compiler_params=_cparams(1),
        name="bias_expand",
    )(table, *(jnp.asarray(bk) for bk in buckets_np))


def _attn_proj_kernel(head_ref, *refs, n_piece):
    x_refs, (gain_ref, w_ref, q_ref, kv_ref, g_ref) = refs[:n_piece], refs[n_piece:]
    first = x_refs[0][...]
    if head_ref is not None:
        first = jnp.where(pl.program_id(1) == 0, head_ref[...], first)
    x = jnp.concatenate([first] + [ref[...] for ref in x_refs[1:]], axis=0)
    xn = _rmsnorm(x, gain_ref[...])
    proj = jnp.dot(xn.astype(BF16), w_ref[...], preferred_element_type=F32)
    q_ref[...] = (proj[:, :A_WIDTH] * (A_HEAD_DIM ** -0.5)).astype(q_ref.dtype)
    kv_ref[...] = proj[:, A_WIDTH:A_WIDTH + 2 * A_KV_WIDTH]
    g_ref[...] = proj[:, A_WIDTH + 2 * A_KV_WIDTH:]


def _attn_proj_call(x2d, head, gain, w_bf16, q_dtype, nbatch, ntile, n_piece, piece):
    tm = n_piece * piece
    wcols = w_bf16.shape[1]
    per_seq = x2d.shape[0] // (nbatch * piece)
    lead = 0 if head is None else 1
    dst = lambda b, i: (b * ntile + i, 0)
    xspec = [pl.BlockSpec((piece, D_MODEL),
                          functools.partial(lambda b, i, kk: (b * per_seq + jnp.maximum(i * n_piece + kk - lead, 0), 0),
                                            kk=kk))
             for kk in range(n_piece)]
    kern = functools.partial(_attn_proj_kernel, n_piece=n_piece)
    operands = [x2d] * n_piece + [gain, w_bf16]
    if head is None:
        kern = functools.partial(kern, None)
        head_spec = []
    else:
        head_spec = [pl.BlockSpec((piece, D_MODEL), lambda b, i: (0, 0))]
        operands = [head] + operands
    n = nbatch * ntile * tm
    return pl.pallas_call(
        kern,
        grid=(nbatch, ntile),
        in_specs=head_spec + xspec + [pl.BlockSpec((1, D_MODEL), lambda b, i: (0, 0)),
                                      pl.BlockSpec((D_MODEL, wcols), lambda b, i: (0, 0))],
        out_specs=[pl.BlockSpec((tm, A_WIDTH), dst),
                   pl.BlockSpec((tm, 2 * A_KV_WIDTH), dst),
                   pl.BlockSpec((tm, A_WIDTH), dst)],
        out_shape=[jax.ShapeDtypeStruct((n, A_WIDTH), q_dtype),
                   jax.ShapeDtypeStruct((n, 2 * A_KV_WIDTH), F32),
                   jax.ShapeDtypeStruct((n, A_WIDTH), F32)],
        compiler_params=_cparams(2),
        name="attn_proj",
    )(*operands)


def _padded_kv_tiles(kv, c):
    lo = _iota((1, LANES), 1) < A_HEAD_DIM
    j = c // 2
    out = []
    for base in (0, A_KV_WIDTH):
        t = kv[:, base + j * LANES: base + (j + 1) * LANES]
        tr = pltpu.roll(t, A_HEAD_DIM, 1)
        if c % 2 == 0:
            even, odd = jnp.where(lo, t, 0.0), jnp.where(lo, 0.0, tr)
        else:
            even, odd = jnp.where(lo, tr, 0.0), jnp.where(lo, 0.0, t)
        out += [even.astype(BF16), odd.astype(BF16)]
    return out


def _mm_nt(a, b):
    return lax.dot_general(a, b, (((1,), (1,)), ((), ())), preferred_element_type=F32)


def _attn_prompt_kernel(sinks_ref, q_ref, kvc_ref, kvp_ref, g_ref, head_ref, *refs):
    nb = ATTN_BLOCKS_PER_STEP
    x_refs, (bias_ref, wout_ref, out_ref, og_ref) = refs[:nb], refs[nb:]
    i = pl.program_id(1)
    stack = 2 * BLOCK
    row, col = _iota((stack, BLOCK), 0) & (BLOCK - 1), _iota((stack, BLOCK), 1)
    upper = _iota((stack, 1), 0) >= BLOCK
    own = col <= row
    chains = [(c, idx) for c in range(A_KV_HEADS) for idx in range(2)]
    n = range(len(chains))
    cur = [_padded_kv_tiles(kvp_ref[...], c) for c in range(A_KV_HEADS)]
    for j in range(nb):
        rows = slice(j * BLOCK, (j + 1) * BLOCK)
        kvalid = (i * nb + j - 1 + own.astype(jnp.int32)) * BLOCK + col >= PAD
        prev, cur = cur, [_padded_kv_tiles(kvc_ref[rows, :], c) for c in range(A_KV_HEADS)]
        s, sink = [], []
        for c, idx in chains:
            q2 = q_ref[rows, 2 * c * LANES:(2 * c + 2) * LANES]
            q2 = jnp.concatenate([q2[:, :LANES], q2[:, LANES:]], axis=0)
            sc = jnp.where(own, _mm_nt(q2, cur[c][idx]), _mm_nt(q2, prev[c][idx]))
            bias = jnp.concatenate([bias_ref[4 * c + idx], bias_ref[4 * c + 2 + idx]], axis=0)
            s.append(jnp.where(kvalid, sc + bias, NEG))
            sink.append(jnp.where(upper, sinks_ref[4 * c + 2 + idx], sinks_ref[4 * c + idx]))
        m = [jnp.maximum(jnp.max(s[t], axis=1, keepdims=True), sink[t]) for t in n]
        p = [jnp.exp(s[t] - m[t]) for t in n]
        den = [jnp.sum(p[t], axis=1, keepdims=True) + jnp.exp(sink[t] - m[t]) for t in n]
        o = []
        for t, (c, idx) in enumerate(chains):
            pv = (jnp.dot(jnp.where(own, p[t], 0.0).astype(BF16), cur[c][2 + idx], preferred_element_type=F32)
                  + jnp.dot(jnp.where(own, 0.0, p[t]).astype(BF16), prev[c][2 + idx], preferred_element_type=F32))
            o.append(pv * (1.0 / den[t]))
        for c in range(A_KV_HEADS):
            both = o[2 * c] + o[2 * c + 1]
            for half in range(2):
                sl = slice((2 * c + half) * LANES, (2 * c + half + 1) * LANES)
                gt = g_ref[rows, sl]
                og_ref[rows, sl] = (both[half * BLOCK:(half + 1) * BLOCK] * (gt * _sigmoid(gt))).astype(BF16)
    resid = jnp.concatenate([jnp.where(i == 0, head_ref[...], x_refs[0][...])]
                            + [ref[...] for ref in x_refs[1:]], axis=0)
    out_ref[...] = resid + jnp.dot(og_ref[...], wout_ref[...], preferred_element_type=F32)


def _attn_prompt_call(sinks, q, kv, g, head, x2d, bias, wout_bf16, nbatch, nblk):
    n = q.shape[0]
    nb = ATTN_BLOCKS_PER_STEP
    nstep = nblk // nb
    rows = nb * BLOCK
    row = lambda b, i: (b * nstep + i, 0)
    prev = lambda b, i: (b * nblk + jnp.maximum(i * nb - 1, 0), 0)
    xrow = [functools.partial(lambda b, i, j: (b * (nblk - 1) + jnp.maximum(i * nb + j - 1, 0), 0), j=j)
            for j in range(nb)]
    return pl.pallas_call(
        _attn_prompt_kernel,
        grid=(nbatch, nstep),
        in_specs=[pl.BlockSpec(memory_space=pltpu.SMEM),
                  pl.BlockSpec((rows, A_WIDTH), row),
                  pl.BlockSpec((rows, 2 * A_KV_WIDTH), row),
                  pl.BlockSpec((BLOCK, 2 * A_KV_WIDTH), prev),
                  pl.BlockSpec((rows, A_WIDTH), row),
                  pl.BlockSpec((BLOCK, D_MODEL), lambda b, i: (0, 0))]
                 + [pl.BlockSpec((BLOCK, D_MODEL), xrow[j]) for j in range(nb)]
                 + [pl.BlockSpec((A_HEADS, BLOCK, BLOCK), lambda b, i: (0, 0, 0)),
                    pl.BlockSpec((A_WIDTH, D_MODEL), lambda b, i: (0, 0))],
        out_specs=pl.BlockSpec((rows, D_MODEL), row),
        out_shape=jax.ShapeDtypeStruct((n, D_MODEL), F32),
        scratch_shapes=[pltpu.VMEM((rows, A_WIDTH), BF16)],
        compiler_params=_cparams(2),
        name="attn_prompt",
    )(sinks, q, kv, kv, g, head, *([x2d] * nb), bias, wout_bf16)


SAMPLE_SB = 8
SAMPLE_KEYS = 2 * BLOCK


def _attn_sample_kernel(sinks_ref, q_ref, kvn_ref, g_ref, h_ref, ck_ref, cv_ref, bias0_ref, bias1_ref,
                        wout_ref, out_ref, nk_ref, nv_ref, og_ref, *, t_new):
    keep = ck_ref.shape[2]
    lo = _iota((1, LANES), 1) < A_HEAD_DIM
    stack = 8 * SUBLANES
    own = (_iota((stack, 1), 0) & (SUBLANES - 1)) // t_new
    piece = _iota((stack, 1), 0) // SUBLANES
    bias_refs = (bias0_ref, bias1_ref)
    n_tile = A_KV_WIDTH // LANES
    pair_ids = range(SAMPLE_SB * t_new // SUBLANES)

    bias_c, bias_n, sink = [], [], []
    for j in range(n_tile):
        heads = slice(8 * j, 8 * j + 8)
        bias_c.append(bias0_ref[heads, :, :keep].reshape(stack, keep))
        bias_n.append([ref[heads, :, keep:keep + SUBLANES].reshape(stack, SUBLANES) for ref in bias_refs])
        col = jnp.zeros((stack, 1), F32)
        for gq in range(8):
            col = jnp.where(piece == gq, sinks_ref[8 * j + gq], col)
        sink.append(col)

    chains = [(p, j, s) for p in pair_ids for j in range(n_tile) for s in range(SUBLANES // t_new)]
    qs = {}
    for p in pair_ids:
        rows = slice(p * SUBLANES, (p + 1) * SUBLANES)
        for j in range(n_tile):
            parts = []
            for gq in range(8):
                t = q_ref[rows, (4 * j + gq // 2) * LANES:(4 * j + gq // 2 + 1) * LANES]
                want_lo = gq < 4
                if (gq % 2 == 0) != want_lo:
                    t = pltpu.roll(t, A_HEAD_DIM, 1)
                parts.append(jnp.where(lo, t, 0.0) if want_lo else jnp.where(lo, 0.0, t))
            qs[p, j] = jnp.concatenate(parts, axis=0).astype(BF16)

    def kv_tiles(p, j, s, base):
        seq = p * (SUBLANES // t_new) + s
        cache = (ck_ref if base == 0 else cv_ref)[seq][j * LANES:(j + 1) * LANES, :].astype(BF16)
        new = kvn_ref[p * SUBLANES:(p + 1) * SUBLANES, base + j * LANES:base + (j + 1) * LANES].astype(BF16)
        return cache, new

    sc, sn = [], []
    for p, j, s in chains:
        kc, kn = kv_tiles(p, j, s, 0)
        sc.append(jnp.dot(qs[p, j], kc, preferred_element_type=F32) + bias_c[j])
        sn.append(_mm_nt(qs[p, j], kn) + bias_n[j][s])
    mx = [jnp.maximum(jnp.maximum(jnp.max(sc[i], axis=1, keepdims=True),
                                  jnp.max(sn[i], axis=1, keepdims=True)), sink[chains[i][1]])
          for i in range(len(chains))]
    pc = [jnp.exp(sc[i] - mx[i]) for i in range(len(chains))]
    pn = [jnp.exp(sn[i] - mx[i]) for i in range(len(chains))]
    den = [jnp.sum(pc[i], axis=1, keepdims=True) + jnp.sum(pn[i], axis=1, keepdims=True)
           + jnp.exp(sink[chains[i][1]] - mx[i]) for i in range(len(chains))]
    outs = {}
    for i, (p, j, s) in enumerate(chains):
        vc, vn = kv_tiles(p, j, s, A_KV_WIDTH)
        o = (_mm_nt(pc[i].astype(BF16), vc)
             + jnp.dot(pn[i].astype(BF16), vn, preferred_element_type=F32)) * (1.0 / den[i])
        outs[p, j] = o if s == 0 else jnp.where(own == s, o, outs[p, j])
    for p in pair_ids:
        rows = slice(p * SUBLANES, (p + 1) * SUBLANES)
        for j in range(n_tile):
            o = outs[p, j]
            for gg in range(4):
                even = o[2 * gg * SUBLANES:(2 * gg + 1) * SUBLANES]
                odd = o[(2 * gg + 1) * SUBLANES:(2 * gg + 2) * SUBLANES]
                if gg < 2:
                    tile = jnp.where(lo, even, pltpu.roll(odd, A_HEAD_DIM, 1))
                else:
                    tile = jnp.where(lo, pltpu.roll(even, A_HEAD_DIM, 1), odd)
                sl = slice((4 * j + gg) * LANES, (4 * j + gg + 1) * LANES)
                gt = g_ref[rows, sl]
                og_ref[rows, sl] = tile * (gt * _sigmoid(gt))
    lane_pos = _iota((SUBLANES, keep), 1)
    old = _iota((1, keep), 1) < keep - t_new
    for p in pair_ids:
        new8 = kvn_ref[p * SUBLANES:(p + 1) * SUBLANES, :]
        for s in range(SUBLANES // t_new):
            seq = p * (SUBLANES // t_new) + s
            sel = (lane_pos == _iota((SUBLANES, keep), 0) + (keep - t_new - s * t_new)).astype(F32)
            placed = _mm3(new8, sel, TN)
            for ref_in, ref_out, base in ((ck_ref, nk_ref, 0), (cv_ref, nv_ref, A_KV_WIDTH)):
                shifted = pltpu.roll(ref_in[seq], keep - t_new, 1)
                ref_out[seq] = jnp.where(old, shifted, placed[base:base + A_KV_WIDTH])
    out_ref[...] = h_ref[...] + jnp.dot(og_ref[...].astype(BF16), wout_ref[...],
                                        preferred_element_type=F32)


def _attn_sample_call(sinks, q, kv, g, h, cache_k, cache_v, bias0, bias1, wout_bf16, t_new):
    nseq, keep = cache_k.shape[0], cache_k.shape[2]
    rows = SAMPLE_SB * t_new
    row = lambda i: (i, 0)
    cspec = pl.BlockSpec((SAMPLE_SB, A_KV_WIDTH, keep), lambda i: (i, 0, 0))
    bspec = pl.BlockSpec((A_HEADS, SUBLANES, SAMPLE_KEYS), lambda i: (0, 0, 0))
    return pl.pallas_call(
        functools.partial(_attn_sample_kernel, t_new=t_new),
        grid=(nseq // SAMPLE_SB,),
        in_specs=[pl.BlockSpec(memory_space=pltpu.SMEM),
                  pl.BlockSpec((rows, A_WIDTH), row),
                  pl.BlockSpec((rows, 2 * A_KV_WIDTH), row),
                  pl.BlockSpec((rows, A_WIDTH), row),
                  pl.BlockSpec((rows, D_MODEL), row),
                  cspec, cspec, bspec, bspec,
                  pl.BlockSpec((A_WIDTH, D_MODEL), lambda i: (0, 0))],
        out_specs=[pl.BlockSpec((rows, D_MODEL), row), cspec, cspec],
        out_shape=[jax.ShapeDtypeStruct((nseq * t_new, D_MODEL), F32),
                   jax.ShapeDtypeStruct(cache_k.shape, F32),
                   jax.ShapeDtypeStruct(cache_v.shape, F32)],
        scratch_shapes=[pltpu.VMEM((rows, A_WIDTH), F32)],
        compiler_params=_cparams(1),
        name="attn_sample",
    )(sinks, q, kv, g, h, cache_k, cache_v, bias0, bias1, wout_bf16)


def _rwkv_proj_kernel(h_ref, shift_ref, gain_ref, mu_ref, win_ref, w0_ref, w1_ref, w2_ref,
                      a0_ref, a1_ref, a2_ref,
                      r_ref, k_ref, v_ref, g_ref, ld_ref, a_ref, xn_ref, *scratch, seq_len):
    xn = _rmsnorm(h_ref[...], gain_ref[...])
    tm = xn.shape[0]
    rolled = pltpu.roll(xn, 1, 0)
    row = _iota((tm, 1), 0)
    if seq_len is None:
        carry_ref, = scratch

        @pl.when(pl.program_id(1) == 0)
        def _():
            carry_ref[...] = shift_ref[0]

        xprev = jnp.where(row == 0, carry_ref[...], rolled)
        carry_ref[...] = xn[tm - 1:tm, :]
        xn_ref[0] = xn[tm - 1:tm, :]
    else:
        xprev = jnp.where(row % seq_len == 0, shift_ref[...], rolled)
        xn_ref[...] = xn
    dx = xprev - xn

    def mix(c):
        return (xn + dx * mu_ref[c:c + 1, :]).astype(BF16)

    for c, o_ref in enumerate((r_ref, k_ref, v_ref, g_ref)):
        o_ref[...] = jnp.dot(mix(c), win_ref[c], preferred_element_type=F32)
    lw = jnp.tanh(jnp.dot(mix(4), w1_ref[...], preferred_element_type=F32))
    z = w0_ref[...] + jnp.dot(lw.astype(BF16), w2_ref[...], preferred_element_type=F32)
    ld_ref[...] = -math.exp(-0.5) * _sigmoid(z)
    la = jnp.dot(mix(5), a1_ref[...], preferred_element_type=F32)
    a_ref[...] = _sigmoid(a0_ref[...] + jnp.dot(la.astype(BF16), a2_ref[...],
                                                preferred_element_type=F32))


def _rwkv_proj_call(h, shift, p, nbatch, ntile, tm, seq_len):
    n = h.shape[0]
    row = lambda b, i: (b * ntile + i, 0)
    full2 = lambda b, i: (0, 0)
    if seq_len is None:
        shift_spec = pl.BlockSpec((1, 1, D_MODEL), lambda b, i: (b, 0, 0))
        xn_spec = pl.BlockSpec((1, 1, D_MODEL), lambda b, i: (b, 0, 0))
        xn_shape = jax.ShapeDtypeStruct((nbatch, 1, D_MODEL), F32)
        scratch = [pltpu.VMEM((1, D_MODEL), F32)]
    else:
        shift_spec = pl.BlockSpec((tm, D_MODEL), row)
        xn_spec = pl.BlockSpec((tm, D_MODEL), row)
        xn_shape = jax.ShapeDtypeStruct((n, D_MODEL), F32)
        scratch = []
    lora = p["w1"].shape[1]
    big = jax.ShapeDtypeStruct((n, D_MODEL), F32)
    return pl.pallas_call(
        functools.partial(_rwkv_proj_kernel, seq_len=seq_len),
        grid=(nbatch, ntile),
        in_specs=[pl.BlockSpec((tm, D_MODEL), row),
                  shift_spec,
                  pl.BlockSpec((1, D_MODEL), full2),
                  pl.BlockSpec(p["mu"].shape, full2),
                  pl.BlockSpec(p["w_in"].shape, lambda b, i: (0, 0, 0)),
                  pl.BlockSpec((1, D_MODEL), full2),
                  pl.BlockSpec((D_MODEL, lora), full2),
                  pl.BlockSpec((lora, D_MODEL), full2),
                  pl.BlockSpec((1, D_MODEL), full2),
                  pl.BlockSpec((D_MODEL, lora), full2),
                  pl.BlockSpec((lora, D_MODEL), full2)],
        out_specs=[pl.BlockSpec((tm, D_MODEL), row)] * 6 + [xn_spec],
        out_shape=[big] * 6 + [xn_shape],
        scratch_shapes=scratch,
        compiler_params=_cparams(2),
        name="rwkv_proj",
    )(h, shift, p["gain"], p["mu"], p["w_in"], p["w0"], p["w1"], p["w2"], p["a0"], p["a1"], p["a2"])


NN = (((1,), (0,)), ((), ()))
NT = (((1,), (1,)), ((), ()))
TN = (((0,), (0,)), ((), ()))


def _split(x):
    hi = x.astype(BF16)
    return hi, (x - hi.astype(F32)).astype(BF16)


def _mm(a, b, dn):
    return lax.dot_general(a, b, dn, preferred_element_type=F32)


def _mm3s(a, b, dn):
    (ah, al), (bh, bl) = a, b
    if dn == TN:
        both = _mm(jnp.concatenate([ah, al], axis=1), bh, dn)
        m = ah.shape[1]
    else:
        both = _mm(jnp.concatenate([ah, al], axis=0), bh, dn)
        m = ah.shape[0]
    return both[:m] + both[m:] + _mm(ah, bl, dn)


def _mm3(a, b, dn):
    return _mm3s(_split(a), _split(b), dn)


def _mm3_top(x_hi, x_lo_top, b, dn):
    bh, bl = b
    c = x_lo_top.shape[0]
    both = _mm(jnp.concatenate([x_hi, x_lo_top], axis=0), bh, dn)
    return both[:c] + both[2 * c:] + _mm(x_hi[:c], bl, dn), both[c:2 * c]


def _lane_lo():
    return _iota((1, LANES), 1) < R_HEAD_DIM


def _half_sums(x):
    lo = _lane_lo()
    return jnp.where(lo, jnp.sum(jnp.where(lo, x, 0.0), axis=1, keepdims=True),
                     jnp.sum(jnp.where(lo, 0.0, x), axis=1, keepdims=True))


def _same_half():
    return (_iota((LANES, LANES), 0) < R_HEAD_DIM) == (_iota((LANES, LANES), 1) < R_HEAD_DIM)


def _bd(z):
    lo = _lane_lo()
    return jnp.concatenate([jnp.where(lo, z, 0.0), jnp.where(lo, 0.0, z)], axis=0)


def _bd_swap(z):
    lo = _lane_lo()
    return jnp.concatenate([jnp.where(lo, 0.0, z), jnp.where(lo, z, 0.0)], axis=0)


def _split_map(z, f):
    hi, lw = _split(z)
    return f(hi), f(lw)


def _wkv_batch_stage(r, k, v, a, ld, kkg, kag, fillers=(), early=None):
    c = r[0].shape[0]
    pairs = range(len(r))
    half = R_HEAD_DIM
    lo = _lane_lo()
    t_row, t_col = _iota((c, LANES), 0), _iota((c, LANES), 1) & (half - 1)
    strict, incl = t_row > t_col, t_row >= t_col
    tri = (_iota((c, c), 0) >= _iota((c, c), 1)).astype(BF16)
    eye_pair = (_iota((c, LANES), 0) == (_iota((c, LANES), 1) & (half - 1))).astype(F32)

    kkx = [k[j] * kkg[j] for j in pairs]
    ssq = [_half_sums(kkx[j] * kkx[j]) for j in pairs]
    ld_hi, ld_lo = _split(jnp.concatenate(ld, axis=1))
    cs_all = _mm(tri, ld_lo, NN) + _mm(tri, ld_hi, NN)
    cs = [cs_all[:, j * LANES:(j + 1) * LANES] for j in pairs]
    tot = [cs[j][c - 1:c, :] for j in pairs]
    x_hi, at_lo, kh_all, bke, v_split, vbd_s, lk_a, lk_r, lb_a, lb_r = ([] for _ in range(10))
    for j in pairs:
        kk = kkx[j] / jnp.maximum(jnp.sqrt(ssq[j]), 1e-12)
        kh = k[j] * (1.0 + (a[j] - 1.0) * kag[j])
        bv = kk * a[j]
        e_neg = jnp.exp(-cs[j])
        e_end = jnp.exp(tot[j] - cs[j])
        at, rt = -kk * jnp.exp(cs[j] - ld[j]), r[j] * jnp.exp(cs[j])
        at_hi, at_l = _split(at)
        x_hi.append(jnp.concatenate([at_hi, rt.astype(BF16)], axis=0))
        at_lo.append(at_l)
        kh_all.append(kh)
        bt, kt = _split(bv * e_neg), _split(kh * e_neg)
        y_hi, y_lo = (jnp.concatenate([p, q], axis=0) for p, q in zip(bt, kt))
        bke.append(jnp.concatenate([bv * e_end, kh * e_end], axis=0))
        v_split.append(_split(v[j]))
        vbd_s.append(tuple(_bd(part) for part in v_split[j]))
        xa, xb = jnp.where(lo, x_hi[j], 0.0), jnp.where(lo, 0.0, x_hi[j])
        both = _mm(jnp.concatenate([xa, xb, jnp.where(lo, at_l, 0.0), jnp.where(lo, 0.0, at_l)], axis=0),
                   y_hi, NT)
        corr = _mm(jnp.concatenate([xa[:c], xb[:c]], axis=0), y_lo, NT)
        ga_a = pltpu.roll(both[:c] + both[4 * c:5 * c] + corr[:c], half, 1)
        ga_r = pltpu.roll(both[c:2 * c], half, 1)
        gb_a = both[2 * c:3 * c] + both[5 * c:] + corr[c:]
        gb_r = both[3 * c:4 * c]
        lk_a.append(jnp.where(strict, jnp.where(lo, ga_a, gb_a), 0.0))
        lk_r.append(jnp.where(incl, jnp.where(lo, ga_r, gb_r), 0.0))
        lb_a.append(jnp.where(strict, jnp.where(lo, gb_a, ga_a), 0.0))
        lb_r.append(jnp.where(incl, jnp.where(lo, gb_r, ga_r), 0.0))
    fillers = list(fillers)
    if early is not None:
        early.update(x_hi=x_hi, at_lo=at_lo)
    pw = lb_a
    acc = [eye_pair + pw[j] for j in pairs]
    pw = [_mm(pw[j].astype(BF16), _bd(pw[j].astype(BF16)), NN) for j in pairs]
    from_v = []
    for j in pairs:
        lk_hi, lk_lo = _split(lk_a[j])
        from_v.append(_mm3_top(jnp.concatenate([lk_hi, lk_r[j].astype(BF16)], axis=0), lk_lo, vbd_s[j], NN))
    for _ in range(int(math.log2(c)) - 2):
        both = [_mm(jnp.concatenate([pw[j], acc[j]], axis=0).astype(BF16), _bd(pw[j].astype(BF16)), NN)
                for j in pairs]
        if fillers:
            fillers.pop(0)()
        pw = [both[j][:c] for j in pairs]
        acc = [acc[j] + both[j][c:] for j in pairs]
    tinv = [acc[j] + _mm(acc[j].astype(BF16), _bd(pw[j].astype(BF16)), NN) for j in pairs]
    for fill in fillers:
        fill()
    return dict(x_hi=x_hi, at_lo=at_lo, kh=kh_all, lb_r=lb_r, tinv=tinv, from_v=from_v, bke=bke,
                v_split=v_split, tot=tot)


def _wkv_finish(stage, from_state_a, from_state_r):
    pairs = range(len(from_state_a))
    u = [_split(_mm3s(_split(stage["tinv"][j]),
                      _split_map(from_state_a[j] + stage["from_v"][j][0], _bd_swap), NN)) for j in pairs]
    y = [from_state_r[j] + stage["from_v"][j][1]
         + _mm(stage["lb_r"][j].astype(BF16), _bd_swap(u[j][0]), NN) for j in pairs]
    return u, y


def _wkv_gate(y, r, kh, v, g, rk, lng, lnb):
    n = len(y)
    pairs = range(n)
    inv_n = 1.0 / R_HEAD_DIM
    rkk = [_half_sums(r[j] * kh[j] * rk[j]) for j in pairs]
    mean = [_half_sums(y[j]) * inv_n for j in pairs]
    d = [y[j] - mean[j] for j in pairs]
    var = [_half_sums(d[j] * d[j]) * inv_n for j in pairs]
    return [((d[j] * lax.rsqrt(var[j] + GN_EPS) * lng[j] + lnb[j] + rkk[j] * v[j])
             * (g[j] * _sigmoid(g[j]))).astype(BF16) for j in pairs]


def _wkv_chunk_kernel(r_ref, k_ref, v_ref, a_ref, ld_ref, g_ref, kkg_ref, kag_ref, rk_ref, lng_ref, lnb_ref,
                      z_ref, sout_ref, st_ref):
    c = WKV_CHUNK
    n_sub = r_ref.shape[0] // c
    npair = r_ref.shape[1] // LANES
    pairs = range(npair)
    items = [(s, j) for s in range(n_sub) for j in pairs]

    @pl.when(pl.program_id(2) == 0)
    def _():
        st_ref[...] = jnp.zeros_like(st_ref)

    def tile(ref, it):
        s, j = it
        return ref[s * c:(s + 1) * c, j * LANES:(j + 1) * LANES]

    def par(ref):
        return [ref[:, j * LANES:(j + 1) * LANES] for _, j in items]

    r, k, v, a, ld, g = ([tile(ref, it) for it in items] for ref in (r_ref, k_ref, v_ref, a_ref, ld_ref, g_ref))
    st = [st_ref[j] for j in pairs]
    from_state = {}
    stage = {}

    def state_products(group):
        def run():
            for j in group:
                from_state[j] = _mm3_top(stage["x_hi"][j], stage["at_lo"][j], _split(st[j]), NN)
        return run

    n_fill = 4
    groups = [list(pairs)[i::n_fill] for i in range(n_fill)]
    stage.update(_wkv_batch_stage(r, k, v, a, ld, par(kkg_ref), par(kag_ref),
                                  fillers=[state_products(grp) for grp in groups if grp], early=stage))
    same_half = _same_half()
    eye_full = _iota((LANES, LANES), 0) == _iota((LANES, LANES), 1)
    ys = []
    for s in range(n_sub):
        idx = [s * npair + j for j in pairs]
        if s > 0:
            for j in pairs:
                from_state[j] = _mm3_top(stage["x_hi"][idx[j]], stage["at_lo"][idx[j]], _split(st[j]), NN)
        sub = {key: [stage[key][i] for i in idx] for key in ("tinv", "from_v", "lb_r")}
        us, ys_s = _wkv_finish(sub, [from_state[j][0] for j in pairs], [from_state[j][1] for j in pairs])
        upd = [_mm3s(_split(stage["bke"][idx[j]]),
                     tuple(jnp.concatenate([up, vp], axis=0) for up, vp in zip(us[j], stage["v_split"][idx[j]])),
                     TN) for j in pairs]
        ys += ys_s
        for j in pairs:
            w_col = jnp.sum(jnp.where(eye_full, jnp.exp(stage["tot"][idx[j]]), 0.0), axis=1, keepdims=True)
            st[j] = w_col * st[j] + jnp.where(same_half, upd[j], 0.0)
    zs = _wkv_gate(ys, r, stage["kh"], v, g, par(rk_ref), par(lng_ref), par(lnb_ref))
    for i, (s, j) in enumerate(items):
        z_ref[s * c:(s + 1) * c, j * LANES:(j + 1) * LANES] = zs[i]
    for j in pairs:
        st_ref[j] = st[j]
        sout_ref[0, j] = st[j]


def _wkv_chunk_call(r, k, v, a, ld, g, p, nbatch, seq):
    rows = WKV_CHUNK * WKV_CHUNKS_PER_STEP
    nstep = seq // rows
    npair = D_MODEL // LANES
    tile = pl.BlockSpec((rows, D_MODEL), lambda b, j, t: (b * nstep + t, 0))
    par = pl.BlockSpec((1, D_MODEL), lambda b, j, t: (0, 0))
    return pl.pallas_call(
        _wkv_chunk_kernel,
        grid=(nbatch, 1, nstep),
        in_specs=[tile] * 6 + [par] * 5,
        out_specs=[tile, pl.BlockSpec((1, npair, LANES, LANES), lambda b, j, t: (b, 0, 0, 0))],
        out_shape=[jax.ShapeDtypeStruct((nbatch * seq, D_MODEL), BF16),
                   jax.ShapeDtypeStruct((nbatch, npair, LANES, LANES), F32)],
        scratch_shapes=[pltpu.VMEM((npair, LANES, LANES), F32)],
        compiler_params=_cparams(3),
        name="wkv_chunk",
    )(r, k, v, a, ld, g, p["k_k"], p["k_a"], p["r_k"], p["ln_g"], p["ln_b"])


WKV_LANES_UNROLL = 4


def _wkv_lanes_kernel(r_ref, k_ref, v_ref, a_ref, ld_ref, g_ref, kkg_ref, kag_ref, rk_ref, lng_ref, lnb_ref,
                      s_ref, z_ref, sout_ref, prep_ref, y_ref, *, seq_len):
    n = R_HEAD_DIM
    nseq = s_ref.shape[3]
    eye = _iota((LANES, LANES), 0) == _iota((LANES, LANES), 1)

    def column(ref):
        return jnp.sum(jnp.where(eye, ref[...], 0.0), axis=1, keepdims=True)

    kkg, kag, rk, lng, lnb = (column(ref) for ref in (kkg_ref, kag_ref, rk_ref, lng_ref, lnb_ref))

    def token(ref, t):
        return ref[pl.ds(t, nseq, stride=seq_len), :].T

    bonus = []
    for t in range(seq_len):
        r, k, v, a = (token(ref, t) for ref in (r_ref, k_ref, v_ref, a_ref))
        w = jnp.exp(token(ld_ref, t))
        kkx = k * kkg
        kh = k * (1.0 + (a - 1.0) * kag)
        rkk = r * kh * rk
        tiles = []
        for hh in range(2):
            rows = slice(hh * n, (hh + 1) * n)
            nrm = jnp.sqrt(jnp.sum(kkx[rows] * kkx[rows], axis=0, keepdims=True))
            kk = kkx[rows] / jnp.maximum(nrm, 1e-12)
            for q, val in enumerate((w[rows], -kk, kk * a[rows], kh[rows], r[rows], v[rows])):
                prep_ref[q, t, hh] = val
            tiles.append(jnp.sum(rkk[rows], axis=0, keepdims=True) * v[rows])
        bonus.append(tiles)

    for hh in range(2):
        def advance(i, carry, hh=hh):
            for u in range(WKV_LANES_UNROLL):
                vi = i * WKV_LANES_UNROLL + u
                slab = s_ref[hh, vi]
                for t in range(seq_len):
                    w, av, bv, kh, r = (prep_ref[q, t, hh] for q in range(5))
                    vrow = prep_ref[5, t, hh, pl.ds(vi, 1), :]
                    sa = jnp.sum(slab * av, axis=0, keepdims=True)
                    slab = slab * w + sa * bv + vrow * kh
                    y_ref[t, hh, pl.ds(vi, 1), :] = jnp.sum(slab * r, axis=0, keepdims=True)
                sout_ref[hh, vi] = slab
            return carry

        lax.fori_loop(0, n // WKV_LANES_UNROLL, advance, 0)

    for t in range(seq_len):
        parts = []
        for hh in range(2):
            rows = slice(hh * n, (hh + 1) * n)
            y = y_ref[t, hh]
            d = y - jnp.mean(y, axis=0, keepdims=True)
            var = jnp.mean(d * d, axis=0, keepdims=True)
            parts.append(d * lax.rsqrt(var + GN_EPS) * lng[rows] + lnb[rows] + bonus[t][hh])
        g = token(g_ref, t)
        z = jnp.concatenate(parts, axis=0) * (g * _sigmoid(g))
        z_ref[pl.ds(t, nseq, stride=seq_len), :] = z.T


def _wkv_lanes_call(r, k, v, a, ld, g, p, state_hvkb, seq_len):
    n = r.shape[0]
    nseq = state_hvkb.shape[3]
    tile = pl.BlockSpec((n, LANES), lambda j: (0, j))
    par = pl.BlockSpec((1, LANES), lambda j: (0, j))
    sspec = pl.BlockSpec((2, R_HEAD_DIM, R_HEAD_DIM, nseq), lambda j: (j, 0, 0, 0))
    return pl.pallas_call(
        functools.partial(_wkv_lanes_kernel, seq_len=seq_len),
        grid=(D_MODEL // LANES,),
        in_specs=[tile] * 6 + [par] * 5 + [sspec],
        out_specs=[tile, sspec],
        out_shape=[jax.ShapeDtypeStruct((n, D_MODEL), F32),
                   jax.ShapeDtypeStruct(state_hvkb.shape, F32)],
        scratch_shapes=[pltpu.VMEM((6, seq_len, 2, R_HEAD_DIM, nseq), F32),
                        pltpu.VMEM((seq_len, 2, R_HEAD_DIM, nseq), F32)],
        compiler_params=_cparams(1),
        name="wkv_lanes",
    )(r, k, v, a, ld, g, p["k_k"], p["k_a"], p["r_k"], p["ln_g"], p["ln_b"], state_hvkb)


RWKV_OUT_PIECES = 8


def _rwkv_out_kernel(*refs):
    n = RWKV_OUT_PIECES
    z_refs, h_refs, (wout_ref, fg_ref, out_ref) = refs[:n], refs[n:2 * n], refs[2 * n:]
    z = jnp.concatenate([ref[...].astype(BF16) for ref in z_refs], axis=0)
    h = jnp.concatenate([ref[...] for ref in h_refs], axis=0)
    h2 = h + jnp.dot(z, wout_ref[...], preferred_element_type=F32)
    out_ref[...] = _rmsnorm(h2, fg_ref[...])


def _rwkv_out_call(z, h, p, nbatch, ntile, tm, pieces_per_batch, skip):
    piece = tm // RWKV_OUT_PIECES
    dst = lambda b, i: (b * ntile + i, 0)
    pspec = [pl.BlockSpec((piece, D_MODEL),
                          functools.partial(lambda b, i, kk: (b * pieces_per_batch + i * RWKV_OUT_PIECES + skip + kk, 0),
                                            kk=kk))
             for kk in range(RWKV_OUT_PIECES)]
    return pl.pallas_call(
        _rwkv_out_kernel,
        grid=(nbatch, ntile),
        in_specs=pspec + pspec + [pl.BlockSpec((D_MODEL, D_MODEL), lambda b, i: (0, 0)),
                                  pl.BlockSpec((1, D_MODEL), lambda b, i: (0, 0))],
        out_specs=pl.BlockSpec((tm, D_MODEL), dst),
        out_shape=jax.ShapeDtypeStruct((nbatch * ntile * tm, D_MODEL), F32),
        compiler_params=_cparams(2),
        name="rwkv_out",
    )(*([z] * RWKV_OUT_PIECES), *([h] * RWKV_OUT_PIECES), p["w_out"], p["final_gain"])


def _prompt_bucket():
    assert WINDOW == BLOCK
    rel = (np.arange(BLOCK)[:, None] - np.arange(BLOCK)[None, :]) % BLOCK
    return _t5_bucket_np(rel)


def _sample_bucket(keep, t_new, slot):
    t = (np.arange(SUBLANES) % t_new)[:, None]
    j = np.arange(SAMPLE_KEYS)[None, :]
    own = j - keep - slot * t_new
    rel = np.where(j < keep, keep + t - j, t - own)
    ok = (rel >= 0) & (rel < WINDOW) & ((j < keep) | ((own >= 0) & (own < t_new)))
    return np.where(ok, _t5_bucket_np(rel), -1).astype(np.int32)


def kernel(x_prompt, x_sample, cache_win_k, cache_win_v, state_wkv, state_shift, meta_tokens, rel_bias_table, norm_gain, final_gain, attn_w_in, attn_sinks, attn_w_out, rwkv_mu, rwkv_w_in, rwkv_w0, rwkv_w1, rwkv_w2, rwkv_a0, rwkv_a1, rwkv_a2, rwkv_k_k, rwkv_k_a, rwkv_r_k, rwkv_ln_gamma, rwkv_ln_beta, rwkv_w_out):
    nb, seq, _ = x_prompt.shape
    ns, t_new, _ = x_sample.shape
    keep = cache_win_k.shape[2]
    lp = seq + BLOCK
    nblk = lp // BLOCK
    row = lambda x: x.reshape(1, D_MODEL)

    w_in0 = attn_w_in[0].astype(BF16)
    w_out0 = attn_w_out[0].astype(BF16)
    gain0 = row(norm_gain[0])
    sinks = attn_sinks[0]
    rp = dict(gain=row(norm_gain[1]), mu=rwkv_mu[0], w_in=rwkv_w_in[0].astype(BF16),
              w0=row(rwkv_w0[0]), w1=rwkv_w1[0].astype(BF16), w2=rwkv_w2[0].astype(BF16),
              a0=row(rwkv_a0[0]), a1=rwkv_a1[0].astype(BF16), a2=rwkv_a2[0].astype(BF16),
              k_k=row(rwkv_k_k[0]), k_a=row(rwkv_k_a[0]), r_k=row(rwkv_r_k[0]), ln_g=row(rwkv_ln_gamma[0]),
              ln_b=row(rwkv_ln_beta[0]), w_out=rwkv_w_out[0].astype(BF16),
              final_gain=row(final_gain))

    bias_p, *bias_s = _bias_call(rel_bias_table, [_prompt_bucket()]
                                 + [_sample_bucket(keep, t_new, slot) for slot in range(2)])

    head = jnp.concatenate([jnp.zeros((PAD, D_MODEL), F32), meta_tokens.astype(F32)], axis=0)
    xp = x_prompt.reshape(nb * seq, D_MODEL)
    q, kv, g = _attn_proj_call(xp, head, gain0, w_in0, BF16, nb, lp // ATTN_PROJ_ROWS,
                               ATTN_PROJ_ROWS // BLOCK, BLOCK)
    h1 = _attn_prompt_call(sinks, q, kv, g, head, xp, bias_p, w_out0, nb, nblk)
    kv3 = kv.reshape(nb, lp, 2 * A_KV_WIDTH)[:, lp - WINDOW:, :]
    win_k_p = kv3[:, :, :A_KV_WIDTH].reshape(1, nb, WINDOW, A_KV_HEADS, A_HEAD_DIM)
    win_v_p = kv3[:, :, A_KV_WIDTH:].reshape(1, nb, WINDOW, A_KV_HEADS, A_HEAD_DIM)

    shift0 = jnp.zeros((nb, 1, D_MODEL), F32)
    r, k, v, g1, ld, a, xlast = _rwkv_proj_call(h1, shift0, rp, nb, lp // RWKV_PROJ_ROWS, RWKV_PROJ_ROWS, None)
    z, st = _wkv_chunk_call(r, k, v, a, ld, g1, rp, nb, lp)
    y_prompt = _rwkv_out_call(z, h1, rp, nb, seq // RWKV_OUT_ROWS, RWKV_OUT_ROWS,
                              lp * RWKV_OUT_PIECES // RWKV_OUT_ROWS, BLOCK * RWKV_OUT_PIECES // RWKV_OUT_ROWS)
    y_prompt = y_prompt.reshape(nb, seq, D_MODEL)
    st = st.reshape(nb, D_MODEL // LANES, 2, R_HEAD_DIM, 2, R_HEAD_DIM)
    st = jnp.stack([st[:, :, 0, :, 0, :], st[:, :, 1, :, 1, :]], axis=2)
    wkv_p = jnp.swapaxes(st, -1, -2).reshape(1, nb, R_HEADS, R_HEAD_DIM, R_HEAD_DIM)
    shift_p = xlast.reshape(1, nb, D_MODEL)

    xs = x_sample.reshape(ns * t_new, D_MODEL)
    qs, kvs, gs = _attn_proj_call(xs, None, gain0, w_in0, F32, 1, 1, 1, ns * t_new)
    ck = jnp.swapaxes(cache_win_k[0].reshape(ns, keep, A_KV_WIDTH), 1, 2)
    cv = jnp.swapaxes(cache_win_v[0].reshape(ns, keep, A_KV_WIDTH), 1, 2)
    h1s, nk, nv = _attn_sample_call(sinks, qs, kvs, gs, xs, ck, cv, bias_s[0], bias_s[1], w_out0, t_new)
    win_k_s = jnp.swapaxes(nk, 1, 2).reshape(1, ns, keep, A_KV_HEADS, A_HEAD_DIM)
    win_v_s = jnp.swapaxes(nv, 1, 2).reshape(1, ns, keep, A_KV_HEADS, A_HEAD_DIM)

    shift_rows = jnp.repeat(state_shift[0], t_new, axis=0)
    tms = 256
    rs, ks, vs, g1s, lds, as_, xns = _rwkv_proj_call(h1s, shift_rows, rp, 1, ns * t_new // tms, tms, t_new)
    zs, st_s = _wkv_lanes_call(rs, ks, vs, as_, lds, g1s, rp, jnp.transpose(state_wkv[0], (1, 2, 3, 0)), t_new)
    y_sample = _rwkv_out_call(zs, h1s, rp, 1, 1, ns * t_new, RWKV_OUT_PIECES, 0)
    y_sample = y_sample.reshape(ns, t_new, D_MODEL)
    wkv_s = jnp.transpose(st_s, (3, 0, 1, 2))[None]
    shift_s = xns.reshape(ns, t_new, D_MODEL)[:, t_new - 1][None]

    return (y_prompt, y_sample, win_k_p, win_v_p, wkv_p, shift_p, win_k_s, win_v_s, wkv_s, shift_s)
```

```python
import functools
import math

import numpy as np
import jax
import jax.numpy as jnp
from jax import lax
from jax.experimental import pallas as pl
from jax.experimental.pallas import tpu as pltpu

F32 = jnp.float32
BF16 = jnp.bfloat16

D_MODEL = 1024
N_META = 16
RMS_EPS = 1e-6
A_HEADS = 16
A_KV_HEADS = 4
A_HEAD_DIM = 64
A_WIDTH = A_HEADS * A_HEAD_DIM
A_KV_WIDTH = A_KV_HEADS * A_HEAD_DIM
WINDOW = 128
BLOCK = 128
N_BUCKETS = 32
MAX_DISTANCE = 128
R_HEAD_DIM = 64
R_HEADS = D_MODEL // R_HEAD_DIM
GN_EPS = 64e-5

LANES = 128
SUBLANES = 8
PAD = BLOCK - N_META
NEG = -1e30
WKV_CHUNK = 64
WKV_CHUNKS_PER_STEP = 3
ATTN_BLOCKS_PER_STEP = 3
ATTN_PROJ_ROWS = 384
RWKV_PROJ_ROWS = 528
RWKV_OUT_ROWS = 1024
VMEM_LIMIT = 56 * 1024 * 1024


def _cparams(n_axes):
    return pltpu.CompilerParams(dimension_semantics=("arbitrary",) * n_axes,
                                vmem_limit_bytes=VMEM_LIMIT)


def _rmsnorm(x, gain):
    return x * lax.rsqrt(jnp.mean(x * x, axis=-1, keepdims=True) + RMS_EPS) * gain


def _sigmoid(x):
    return 1.0 / (1.0 + jnp.exp(-x))


def _iota(shape, dim):
    return lax.broadcasted_iota(jnp.int32, shape, dim)


def _t5_bucket_np(rel):
    n = np.maximum(rel, 0)
    max_exact = N_BUCKETS // 2
    nf = np.maximum(n, max_exact).astype(np.float32)
    scale = np.float32(math.log(MAX_DISTANCE / max_exact))
    large = max_exact + (np.log(nf / np.float32(max_exact)) / scale
                         * np.float32(N_BUCKETS - max_exact)).astype(np.int32)
    large = np.minimum(large, N_BUCKETS - 1)
    return np.where(n < max_exact, n, large).astype(np.int32)


def _bias_kernel(table_ref, *refs):
    h = pl.program_id(0)
    n = len(refs) // 2
    for bucket_ref, out_ref in zip(refs[:n], refs[n:]):
        bk = bucket_ref[...]
        acc = jnp.full(bk.shape, NEG, F32)
        for b in range(N_BUCKETS):
            acc = jnp.where(bk == b, table_ref[b, h], acc)
        out_ref[0] = acc


def _bias_call(table, buckets_np):
    return pl.pallas_call(
        _bias_kernel,
        grid=(A_HEADS,),
        in_specs=[pl.BlockSpec(memory_space=pltpu.SMEM)]
                 + [pl.BlockSpec(bk.shape, lambda h: (0, 0)) for bk in buckets_np],
        out_specs=[pl.BlockSpec((1,) + bk.shape, lambda h: (h, 0, 0)) for bk in buckets_np],
        out_shape=[jax.ShapeDtypeStruct((A_HEADS,) + bk.shape, F32) for bk in buckets_np],
        compiler_params=_cparams(1),
        name="bias_expand",
    )(table, *(jnp.asarray(bk) for bk in buckets_np))


def _attn_proj_kernel(head_ref, *refs, n_piece):
    x_refs, (gain_ref, w_ref, q_ref, kv_ref, g_ref) = refs[:n_piece], refs[n_piece:]
    first = x_refs[0][...]
    if head_ref is not None:
        first = jnp.where(pl.program_id(1) == 0, head_ref[...], first)
    x = jnp.concatenate([first] + [ref[...] for ref in x_refs[1:]], axis=0)
    xn = _rmsnorm(x, gain_ref[...])
    proj = jnp.dot(xn.astype(BF16), w_ref[...], preferred_element_type=F32)
    q_ref[...] = (proj[:, :A_WIDTH] * (A_HEAD_DIM ** -0.5)).astype(q_ref.dtype)
    kv_ref[...] = proj[:, A_WIDTH:A_WIDTH + 2 * A_KV_WIDTH]
    g_ref[...] = proj[:, A_WIDTH + 2 * A_KV_WIDTH:]


def _attn_proj_call(x2d, head, gain, w_bf16, q_dtype, nbatch, ntile, n_piece, piece):
    tm = n_piece * piece
    wcols = w_bf16.shape[1]
    per_seq = x2d.shape[0] // (nbatch * piece)
    lead = 0 if head is None else 1
    dst = lambda b, i: (b * ntile + i, 0)
    xspec = [pl.BlockSpec((piece, D_MODEL),
                          functools.partial(lambda b, i, kk: (b * per_seq + jnp.maximum(i * n_piece + kk - lead, 0), 0),
                                            kk=kk))
             for kk in range(n_piece)]
    kern = functools.partial(_attn_proj_kernel, n_piece=n_piece)
    operands = [x2d] * n_piece + [gain, w_bf16]
    if head is None:
        kern = functools.partial(kern, None)
        head_spec = []
    else:
        head_spec = [pl.BlockSpec((piece, D_MODEL), lambda b, i: (0, 0))]
        operands = [head] + operands
    n = nbatch * ntile * tm
    return pl.pallas_call(
        kern,
        grid=(nbatch, ntile),
        in_specs=head_spec + xspec + [pl.BlockSpec((1, D_MODEL), lambda b, i: (0, 0)),
                                      pl.BlockSpec((D_MODEL, wcols), lambda b, i: (0, 0))],
        out_specs=[pl.BlockSpec((tm, A_WIDTH), dst),
                   pl.BlockSpec((tm, 2 * A_KV_WIDTH), dst),
                   pl.BlockSpec((tm, A_WIDTH), dst)],
        out_shape=[jax.ShapeDtypeStruct((n, A_WIDTH), q_dtype),
                   jax.ShapeDtypeStruct((n, 2 * A_KV_WIDTH), F32),
                   jax.ShapeDtypeStruct((n, A_WIDTH), F32)],
        compiler_params=_cparams(2),
        name="attn_proj",
    )(*operands)


def _padded_kv_tiles(kv, c):
    lo = _iota((1, LANES), 1) < A_HEAD_DIM
    j = c // 2
    out = []
    for base in (0, A_KV_WIDTH):
        t = kv[:, base + j * LANES: base + (j + 1) * LANES]
        tr = pltpu.roll(t, A_HEAD_DIM, 1)
        if c % 2 == 0:
            even, odd = jnp.where(lo, t, 0.0), jnp.where(lo, 0.0, tr)
        else:
            even, odd = jnp.where(lo, tr, 0.0), jnp.where(lo, 0.0, t)
        out += [even.astype(BF16), odd.astype(BF16)]
    return out


def _mm_nt(a, b):
    return lax.dot_general(a, b, (((1,), (1,)), ((), ())), preferred_element_type=F32)


def _attn_prompt_kernel(sinks_ref, q_ref, kvc_ref, kvp_ref, g_ref, head_ref, *refs):
    nb = ATTN_BLOCKS_PER_STEP
    x_refs, (bias_ref, wout_ref, out_ref, og_ref) = refs[:nb], refs[nb:]
    i = pl.program_id(1)
    stack = 2 * BLOCK
    row, col = _iota((stack, BLOCK), 0) & (BLOCK - 1), _iota((stack, BLOCK), 1)
    upper = _iota((stack, 1), 0) >= BLOCK
    own = col <= row
    chains = [(c, idx) for c in range(A_KV_HEADS) for idx in range(2)]
    n = range(len(chains))
    cur = [_padded_kv_tiles(kvp_ref[...], c) for c in range(A_KV_HEADS)]
    for j in range(nb):
        rows = slice(j * BLOCK, (j + 1) * BLOCK)
        kvalid = (i * nb + j - 1 + own.astype(jnp.int32)) * BLOCK + col >= PAD
        prev, cur = cur, [_padded_kv_tiles(kvc_ref[rows, :], c) for c in range(A_KV_HEADS)]
        s, sink = [], []
        for c, idx in chains:
            q2 = q_ref[rows, 2 * c * LANES:(2 * c + 2) * LANES]
            q2 = jnp.concatenate([q2[:, :LANES], q2[:, LANES:]], axis=0)
            sc = jnp.where(own, _mm_nt(q2, cur[c][idx]), _mm_nt(q2, prev[c][idx]))
            bias = jnp.concatenate([bias_ref[4 * c + idx], bias_ref[4 * c + 2 + idx]], axis=0)
            s.append(jnp.where(kvalid, sc + bias, NEG))
            sink.append(jnp.where(upper, sinks_ref[4 * c + 2 + idx], sinks_ref[4 * c + idx]))
        m = [jnp.maximum(jnp.max(s[t], axis=1, keepdims=True), sink[t]) for t in n]
        p = [jnp.exp(s[t] - m[t]) for t in n]
        den = [jnp.sum(p[t], axis=1, keepdims=True) + jnp.exp(sink[t] - m[t]) for t in n]
        o = []
        for t, (c, idx) in enumerate(chains):
            pv = (jnp.dot(jnp.where(own, p[t], 0.0).astype(BF16), cur[c][2 + idx], preferred_element_type=F32)
                  + jnp.dot(jnp.where(own, 0.0, p[t]).astype(BF16), prev[c][2 + idx], preferred_element_type=F32))
            o.append(pv * (1.0 / den[t]))
        for c in range(A_KV_HEADS):
            both = o[2 * c] + o[2 * c + 1]
            for half in range(2):
                sl = slice((2 * c + half) * LANES, (2 * c + half + 1) * LANES)
                gt = g_ref[rows, sl]
                og_ref[rows, sl] = (both[half * BLOCK:(half + 1) * BLOCK] * (gt * _sigmoid(gt))).astype(BF16)
    resid = jnp.concatenate([jnp.where(i == 0, head_ref[...], x_refs[0][...])]
                            + [ref[...] for ref in x_refs[1:]], axis=0)
    out_ref[...] = resid + jnp.dot(og_ref[...], wout_ref[...], preferred_element_type=F32)


def _attn_prompt_call(sinks, q, kv, g, head, x2d, bias, wout_bf16, nbatch, nblk):
    n = q.shape[0]
    nb = ATTN_BLOCKS_PER_STEP
    nstep = nblk // nb
    rows = nb * BLOCK
    row = lambda b, i: (b * nstep + i, 0)
    prev = lambda b, i: (b * nblk + jnp.maximum(i * nb - 1, 0), 0)
    xrow = [functools.partial(lambda b, i, j: (b * (nblk - 1) + jnp.maximum(i * nb + j - 1, 0), 0), j=j)
            for j in range(nb)]
    return pl.pallas_call(
        _attn_prompt_kernel,
        grid=(nbatch, nstep),
        in_specs=[pl.BlockSpec(memory_space=pltpu.SMEM),
                  pl.BlockSpec((rows, A_WIDTH), row),
                  pl.BlockSpec((rows, 2 * A_KV_WIDTH), row),
                  pl.BlockSpec((BLOCK, 2 * A_KV_WIDTH), prev),
                  pl.BlockSpec((rows, A_WIDTH), row),
                  pl.BlockSpec((BLOCK, D_MODEL), lambda b, i: (0, 0))]
                 + [pl.BlockSpec((BLOCK, D_MODEL), xrow[j]) for j in range(nb)]
                 + [pl.BlockSpec((A_HEADS, BLOCK, BLOCK), lambda b, i: (0, 0, 0)),
                    pl.BlockSpec((A_WIDTH, D_MODEL), lambda b, i: (0, 0))],
        out_specs=pl.BlockSpec((rows, D_MODEL), row),
        out_shape=jax.ShapeDtypeStruct((n, D_MODEL), F32),
        scratch_shapes=[pltpu.VMEM((rows, A_WIDTH), BF16)],
        compiler_params=_cparams(2),
        name="attn_prompt",
    )(sinks, q, kv, kv, g, head, *([x2d] * nb), bias, wout_bf16)


SAMPLE_SB = 8
SAMPLE_KEYS = 2 * BLOCK


def _attn_sample_kernel(sinks_ref, q_ref, kvn_ref, g_ref, h_ref, ck_ref, cv_ref, bias0_ref, bias1_ref,
                        wout_ref, out_ref, nk_ref, nv_ref, og_ref, *, t_new):
    keep = ck_ref.shape[2]
    lo = _iota((1, LANES), 1) < A_HEAD_DIM
    stack = 8 * SUBLANES
    own = (_iota((stack, 1), 0) & (SUBLANES - 1)) // t_new
    piece = _iota((stack, 1), 0) // SUBLANES
    bias_refs = (bias0_ref, bias1_ref)
    n_tile = A_KV_WIDTH // LANES
    pair_ids = range(SAMPLE_SB * t_new // SUBLANES)

    bias_c, bias_n, sink = [], [], []
    for j in range(n_tile):
        heads = slice(8 * j, 8 * j + 8)
        bias_c.append(bias0_ref[heads, :, :keep].reshape(stack, keep))
        bias_n.append([ref[heads, :, keep:keep + SUBLANES].reshape(stack, SUBLANES) for ref in bias_refs])
        col = jnp.zeros((stack, 1), F32)
        for gq in range(8):
            col = jnp.where(piece == gq, sinks_ref[8 * j + gq], col)
        sink.append(col)

    chains = [(p, j, s) for p in pair_ids for j in range(n_tile) for s in range(SUBLANES // t_new)]
    qs = {}
    for p in pair_ids:
        rows = slice(p * SUBLANES, (p + 1) * SUBLANES)
        for j in range(n_tile):
            parts = []
            for gq in range(8):
                t = q_ref[rows, (4 * j + gq // 2) * LANES:(4 * j + gq // 2 + 1) * LANES]
                want_lo = gq < 4
                if (gq % 2 == 0) != want_lo:
                    t = pltpu.roll(t, A_HEAD_DIM, 1)
                parts.append(jnp.where(lo, t, 0.0) if want_lo else jnp.where(lo, 0.0, t))
            qs[p, j] = jnp.concatenate(parts, axis=0).astype(BF16)

    def kv_tiles(p, j, s, base):
        seq = p * (SUBLANES // t_new) + s
        cache = (ck_ref if base == 0 else cv_ref)[seq][j * LANES:(j + 1) * LANES, :].astype(BF16)
        new = kvn_ref[p * SUBLANES:(p + 1) * SUBLANES, base + j * LANES:base + (j + 1) * LANES].astype(BF16)
        return cache, new

    sc, sn = [], []
    for p, j, s in chains:
        kc, kn = kv_tiles(p, j, s, 0)
        sc.append(jnp.dot(qs[p, j], kc, preferred_element_type=F32) + bias_c[j])
        sn.append(_mm_nt(qs[p, j], kn) + bias_n[j][s])
    mx = [jnp.maximum(jnp.maximum(jnp.max(sc[i], axis=1, keepdims=True),
                                  jnp.max(sn[i], axis=1, keepdims=True)), sink[chains[i][1]])
          for i in range(len(chains))]
    pc = [jnp.exp(sc[i] - mx[i]) for i in range(len(chains))]
    pn = [jnp.exp(sn[i] - mx[i]) for i in range(len(chains))]
    den = [jnp.sum(pc[i], axis=1, keepdims=True) + jnp.sum(pn[i], axis=1, keepdims=True)
           + jnp.exp(sink[chains[i][1]] - mx[i]) for i in range(len(chains))]
    outs = {}
    for i, (p, j, s) in enumerate(chains):
        vc, vn = kv_tiles(p, j, s, A_KV_WIDTH)
        o = (_mm_nt(pc[i].astype(BF16), vc)
             + jnp.dot(pn[i].astype(BF16), vn, preferred_element_type=F32)) * (1.0 / den[i])
        outs[p, j] = o if s == 0 else jnp.where(own == s, o, outs[p, j])
    for p in pair_ids:
        rows = slice(p * SUBLANES, (p + 1) * SUBLANES)
        for j in range(n_tile):
            o = outs[p, j]
            for gg in range(4):
                even = o[2 * gg * SUBLANES:(2 * gg + 1) * SUBLANES]
                odd = o[(2 * gg + 1) * SUBLANES:(2 * gg + 2) * SUBLANES]
                if gg < 2:
                    tile = jnp.where(lo, even, pltpu.roll(odd, A_HEAD_DIM, 1))
                else:
                    tile = jnp.where(lo, pltpu.roll(even, A_HEAD_DIM, 1), odd)
                sl = slice((4 * j + gg) * LANES, (4 * j + gg + 1) * LANES)
                gt = g_ref[rows, sl]
                og_ref[rows, sl] = tile * (gt * _sigmoid(gt))
    lane_pos = _iota((SUBLANES, keep), 1)
    old = _iota((1, keep), 1) < keep - t_new
    for p in pair_ids:
        new8 = kvn_ref[p * SUBLANES:(p + 1) * SUBLANES, :]
        for s in range(SUBLANES // t_new):
            seq = p * (SUBLANES // t_new) + s
            sel = (lane_pos == _iota((SUBLANES, keep), 0) + (keep - t_new - s * t_new)).astype(F32)
            placed = _mm3(new8, sel, TN)
            for ref_in, ref_out, base in ((ck_ref, nk_ref, 0), (cv_ref, nv_ref, A_KV_WIDTH)):
                shifted = pltpu.roll(ref_in[seq], keep - t_new, 1)
                ref_out[seq] = jnp.where(old, shifted, placed[base:base + A_KV_WIDTH])
    out_ref[...] = h_ref[...] + jnp.dot(og_ref[...].astype(BF16), wout_ref[...],
                                        preferred_element_type=F32)


def _attn_sample_call(sinks, q, kv, g, h, cache_k, cache_v, bias0, bias1, wout_bf16, t_new):
    nseq, keep = cache_k.shape[0], cache_k.shape[2]
    rows = SAMPLE_SB * t_new
    row = lambda i: (i, 0)
    cspec = pl.BlockSpec((SAMPLE_SB, A_KV_WIDTH, keep), lambda i: (i, 0, 0))
    bspec = pl.BlockSpec((A_HEADS, SUBLANES, SAMPLE_KEYS), lambda i: (0, 0, 0))
    return pl.pallas_call(
        functools.partial(_attn_sample_kernel, t_new=t_new),
        grid=(nseq // SAMPLE_SB,),
        in_specs=[pl.BlockSpec(memory_space=pltpu.SMEM),
                  pl.BlockSpec((rows, A_WIDTH), row),
                  pl.BlockSpec((rows, 2 * A_KV_WIDTH), row),
                  pl.BlockSpec((rows, A_WIDTH), row),
                  pl.BlockSpec((rows, D_MODEL), row),
                  cspec, cspec, bspec, bspec,
                  pl.BlockSpec((A_WIDTH, D_MODEL), lambda i: (0, 0))],
        out_specs=[pl.BlockSpec((rows, D_MODEL), row), cspec, cspec],
        out_shape=[jax.ShapeDtypeStruct((nseq * t_new, D_MODEL), F32),
                   jax.ShapeDtypeStruct(cache_k.shape, F32),
                   jax.ShapeDtypeStruct(cache_v.shape, F32)],
        scratch_shapes=[pltpu.VMEM((rows, A_WIDTH), F32)],
        compiler_params=_cparams(1),
        name="attn_sample",
    )(sinks, q, kv, g, h, cache_k, cache_v, bias0, bias1, wout_bf16)


def _rwkv_proj_kernel(h_ref, shift_ref, gain_ref, mu_ref, win_ref, w0_ref, w1_ref, w2_ref,
                      a0_ref, a1_ref, a2_ref,
                      r_ref, k_ref, v_ref, g_ref, ld_ref, a_ref, xn_ref, *scratch, seq_len):
    xn = _rmsnorm(h_ref[...], gain_ref[...])
    tm = xn.shape[0]
    rolled = pltpu.roll(xn, 1, 0)
    row = _iota((tm, 1), 0)
    if seq_len is None:
        carry_ref, = scratch

        @pl.when(pl.program_id(1) == 0)
        def _():
            carry_ref[...] = shift_ref[0]

        xprev = jnp.where(row == 0, carry_ref[...], rolled)
        carry_ref[...] = xn[tm - 1:tm, :]
        xn_ref[0] = xn[tm - 1:tm, :]
    else:
        xprev = jnp.where(row % seq_len == 0, shift_ref[...], rolled)
        xn_ref[...] = xn
    dx = xprev - xn

    def mix(c):
        return (xn + dx * mu_ref[c:c + 1, :]).astype(BF16)

    for c, o_ref in enumerate((r_ref, k_ref, v_ref, g_ref)):
        o_ref[...] = jnp.dot(mix(c), win_ref[c], preferred_element_type=F32)
    lw = jnp.tanh(jnp.dot(mix(4), w1_ref[...], preferred_element_type=F32))
    z = w0_ref[...] + jnp.dot(lw.astype(BF16), w2_ref[...], preferred_element_type=F32)
    ld_ref[...] = -math.exp(-0.5) * _sigmoid(z)
    la = jnp.dot(mix(5), a1_ref[...], preferred_element_type=F32)
    a_ref[...] = _sigmoid(a0_ref[...] + jnp.dot(la.astype(BF16), a2_ref[...],
                                                preferred_element_type=F32))


def _rwkv_proj_call(h, shift, p, nbatch, ntile, tm, seq_len):
    n = h.shape[0]
    row = lambda b, i: (b * ntile + i, 0)
    full2 = lambda b, i: (0, 0)
    if seq_len is None:
        shift_spec = pl.BlockSpec((1, 1, D_MODEL), lambda b, i: (b, 0, 0))
        xn_spec = pl.BlockSpec((1, 1, D_MODEL), lambda b, i: (b, 0, 0))
        xn_shape = jax.ShapeDtypeStruct((nbatch, 1, D_MODEL), F32)
        scratch = [pltpu.VMEM((1, D_MODEL), F32)]
    else:
        shift_spec = pl.BlockSpec((tm, D_MODEL), row)
        xn_spec = pl.BlockSpec((tm, D_MODEL), row)
        xn_shape = jax.ShapeDtypeStruct((n, D_MODEL), F32)
        scratch = []
    lora = p["w1"].shape[1]
    big = jax.ShapeDtypeStruct((n, D_MODEL), F32)
    return pl.pallas_call(
        functools.partial(_rwkv_proj_kernel, seq_len=seq_len),
        grid=(nbatch, ntile),
        in_specs=[pl.BlockSpec((tm, D_MODEL), row),
                  shift_spec,
                  pl.BlockSpec((1, D_MODEL), full2),
                  pl.BlockSpec(p["mu"].shape, full2),
                  pl.BlockSpec(p["w_in"].shape, lambda b, i: (0, 0, 0)),
                  pl.BlockSpec((1, D_MODEL), full2),
                  pl.BlockSpec((D_MODEL, lora), full2),
                  pl.BlockSpec((lora, D_MODEL), full2),
                  pl.BlockSpec((1, D_MODEL), full2),
                  pl.BlockSpec((D_MODEL, lora), full2),
                  pl.BlockSpec((lora, D_MODEL), full2)],
        out_specs=[pl.BlockSpec((tm, D_MODEL), row)] * 6 + [xn_spec],
        out_shape=[big] * 6 + [xn_shape],
        scratch_shapes=scratch,
        compiler_params=_cparams(2),
        name="rwkv_proj",
    )(h, shift, p["gain"], p["mu"], p["w_in"], p["w0"], p["w1"], p["w2"], p["a0"], p["a1"], p["a2"])


NN = (((1,), (0,)), ((), ()))
NT = (((1,), (1,)), ((), ()))
TN = (((0,), (0,)), ((), ()))


def _split(x):
    hi = x.astype(BF16)
    return hi, (x - hi.astype(F32)).astype(BF16)


def _mm(a, b, dn):
    return lax.dot_general(a, b, dn, preferred_element_type=F32)


def _mm3s(a, b, dn):
    (ah, al), (bh, bl) = a, b
    if dn == TN:
        both = _mm(jnp.concatenate([ah, al], axis=1), bh, dn)
        m = ah.shape[1]
    else:
        both = _mm(jnp.concatenate([ah, al], axis=0), bh, dn)
        m = ah.shape[0]
    return both[:m] + both[m:] + _mm(ah, bl, dn)


def _mm3(a, b, dn):
    return _mm3s(_split(a), _split(b), dn)


def _mm4k(a, b, dn):
    (ah, al), (bh, bl) = a, b
    row = jnp.concatenate([bh, bl], axis=1)
    out = _mm(jnp.concatenate([ah, al], axis=0 if dn == TN else 1), jnp.concatenate([row, row], axis=0), dn)
    n = bh.shape[1]
    return out[:, :n] + out[:, n:]


def _mm3_top(x_hi, x_lo_top, b, dn):
    bh, bl = b
    c = x_lo_top.shape[0]
    both = _mm(jnp.concatenate([x_hi, x_lo_top], axis=0), bh, dn)
    return both[:c] + both[2 * c:] + _mm(x_hi[:c], bl, dn), both[c:2 * c]


def _seg_sums(xs, ones_bf16, exact):
    rows = xs[0].shape[0]
    stacked = jnp.concatenate(xs, axis=0)
    if exact:
        hi, lo = _split(stacked)
        out = _mm(lo, ones_bf16, NN) + _mm(hi, ones_bf16, NN)
    else:
        out = _mm(stacked.astype(BF16), ones_bf16, NN)
    return [out[i * rows:(i + 1) * rows] for i in range(len(xs))]


def _lane_lo():
    return _iota((1, LANES), 1) < R_HEAD_DIM


def _same_half():
    return (_iota((LANES, LANES), 0) < R_HEAD_DIM) == (_iota((LANES, LANES), 1) < R_HEAD_DIM)


def _ones_blk():
    return _same_half().astype(BF16)


def _bd(z):
    lo = _lane_lo()
    return jnp.concatenate([jnp.where(lo, z, 0.0), jnp.where(lo, 0.0, z)], axis=0)


def _bd_swap(z):
    lo = _lane_lo()
    return jnp.concatenate([jnp.where(lo, 0.0, z), jnp.where(lo, z, 0.0)], axis=0)


def _split_map(z, f):
    hi, lw = _split(z)
    return f(hi), f(lw)


def _wkv_batch_stage(r, k, v, a, ld, kkg, kag, fillers=(), early=None):
    c = r[0].shape[0]
    pairs = range(len(r))
    half = R_HEAD_DIM
    lo = _lane_lo()
    t_row, t_col = _iota((c, LANES), 0), _iota((c, LANES), 1) & (half - 1)
    strict, incl = t_row > t_col, t_row >= t_col
    tri = (_iota((c, c), 0) >= _iota((c, c), 1)).astype(BF16)
    ones_blk = _ones_blk()
    eye_pair = (_iota((c, LANES), 0) == (_iota((c, LANES), 1) & (half - 1))).astype(F32)

    kkx = [k[j] * kkg[j] for j in pairs]
    ssq = _seg_sums([kkx[j] * kkx[j] for j in pairs], ones_blk, exact=True)
    ld_hi, ld_lo = _split(jnp.concatenate(ld, axis=1))
    cs_all = _mm(tri, ld_lo, NN) + _mm(tri, ld_hi, NN)
    cs = [cs_all[:, j * LANES:(j + 1) * LANES] for j in pairs]
    tot = [cs[j][c - 1:c, :] for j in pairs]
    x_hi, at_lo, kh_all, bke, vbd_s, lk_a, lk_r, lb_a, lb_r = ([] for _ in range(9))
    for j in pairs:
        kk = kkx[j] / jnp.maximum(jnp.sqrt(ssq[j]), 1e-12)
        kh = k[j] * (1.0 + (a[j] - 1.0) * kag[j])
        bv = kk * a[j]
        e_neg = jnp.exp(-cs[j])
        e_end = jnp.exp(tot[j] - cs[j])
        at, rt = -kk * jnp.exp(cs[j] - ld[j]), r[j] * jnp.exp(cs[j])
        at_hi, at_l = _split(at)
        x_hi.append(jnp.concatenate([at_hi, rt.astype(BF16)], axis=0))
        at_lo.append(at_l)
        kh_all.append(kh)
        bt, kt = _split(bv * e_neg), _split(kh * e_neg)
        y_hi, y_lo = (jnp.concatenate([p, q], axis=0) for p, q in zip(bt, kt))
        bke.append(jnp.concatenate([bv * e_end, kh * e_end], axis=0))
        vbd_s.append(_split_map(v[j], _bd))
        xa, xb = jnp.where(lo, x_hi[j], 0.0), jnp.where(lo, 0.0, x_hi[j])
        both = _mm(jnp.concatenate([xa, xb, jnp.where(lo, at_l, 0.0), jnp.where(lo, 0.0, at_l)], axis=0),
                   y_hi, NT)
        corr = _mm(jnp.concatenate([xa[:c], xb[:c]], axis=0), y_lo, NT)
        ga_a = pltpu.roll(both[:c] + both[4 * c:5 * c] + corr[:c], half, 1)
        ga_r = pltpu.roll(both[c:2 * c], half, 1)
        gb_a = both[2 * c:3 * c] + both[5 * c:] + corr[c:]
        gb_r = both[3 * c:4 * c]
        lk_a.append(jnp.where(strict, jnp.where(lo, ga_a, gb_a), 0.0))
        lk_r.append(jnp.where(incl, jnp.where(lo, ga_r, gb_r), 0.0))
        lb_a.append(jnp.where(strict, jnp.where(lo, gb_a, ga_a), 0.0))
        lb_r.append(jnp.where(incl, jnp.where(lo, gb_r, ga_r), 0.0))
    fillers = list(fillers)
    if early is not None:
        early.update(x_hi=x_hi, at_lo=at_lo)
    pw = lb_a
    acc = [eye_pair + pw[j] for j in pairs]
    pw = [_mm(pw[j].astype(BF16), _bd(pw[j].astype(BF16)), NN) for j in pairs]
    from_v = []
    for j in pairs:
        lk_hi, lk_lo = _split(lk_a[j])
        from_v.append(_mm3_top(jnp.concatenate([lk_hi, lk_r[j].astype(BF16)], axis=0), lk_lo, vbd_s[j], NN))
    for _ in range(int(math.log2(c)) - 2):
        both = [_mm(jnp.concatenate([pw[j], acc[j]], axis=0).astype(BF16), _bd(pw[j].astype(BF16)), NN)
                for j in pairs]
        if fillers:
            fillers.pop(0)()
        pw = [both[j][:c] for j in pairs]
        acc = [acc[j] + both[j][c:] for j in pairs]
    tinv = [acc[j] + _mm(acc[j].astype(BF16), _bd(pw[j].astype(BF16)), NN) for j in pairs]
    for fill in fillers:
        fill()
    return dict(x_hi=x_hi, at_lo=at_lo, kh=kh_all, lb_r=lb_r, tinv=tinv, from_v=from_v, bke=bke,
                tot=tot, ones_blk=ones_blk)


def _wkv_finish(stage, from_state_a, from_state_r):
    pairs = range(len(from_state_a))
    u = [_mm4k(_split(stage["tinv"][j]),
               _split_map(from_state_a[j] + stage["from_v"][j][0], _bd_swap), NN) for j in pairs]
    y = [from_state_r[j] + stage["from_v"][j][1]
         + _mm(stage["lb_r"][j].astype(BF16), _bd_swap(u[j].astype(BF16)), NN) for j in pairs]
    return u, y


def _wkv_gate(y, r, kh, v, g, rk, lng, lnb, ones_blk):
    n = len(y)
    pairs = range(n)
    inv_n = 1.0 / R_HEAD_DIM
    sums = _seg_sums([r[j] * kh[j] * rk[j] for j in pairs] + list(y), ones_blk, exact=False)
    rkk, mean = sums[:n], [s * inv_n for s in sums[n:]]
    d = [y[j] - mean[j] for j in pairs]
    var = [s * inv_n for s in _seg_sums([d[j] * d[j] for j in pairs], ones_blk, exact=False)]
    return [((d[j] * lax.rsqrt(var[j] + GN_EPS) * lng[j] + lnb[j] + rkk[j] * v[j])
             * (g[j] * _sigmoid(g[j]))).astype(BF16) for j in pairs]


def _wkv_chunk_kernel(r_ref, k_ref, v_ref, a_ref, ld_ref, g_ref, kkg_ref, kag_ref, rk_ref, lng_ref, lnb_ref,
                      z_ref, sout_ref, st_ref):
    c = WKV_CHUNK
    n_sub = r_ref.shape[0] // c
    npair = r_ref.shape[1] // LANES
    pairs = range(npair)
    items = [(s, j) for s in range(n_sub) for j in pairs]

    @pl.when(pl.program_id(2) == 0)
    def _():
        st_ref[...] = jnp.zeros_like(st_ref)

    def tile(ref, it):
        s, j = it
        return ref[s * c:(s + 1) * c, j * LANES:(j + 1) * LANES]

    def par(ref):
        return [ref[:, j * LANES:(j + 1) * LANES] for _, j in items]

    r, k, v, a, ld, g = ([tile(ref, it) for it in items] for ref in (r_ref, k_ref, v_ref, a_ref, ld_ref, g_ref))
    st = [st_ref[j] for j in pairs]
    from_state = {}
    stage = {}

    def state_products(group):
        def run():
            for j in group:
                from_state[j] = _mm3_top(stage["x_hi"][j], stage["at_lo"][j], _split(st[j]), NN)
        return run

    n_fill = 4
    groups = [list(pairs)[i::n_fill] for i in range(n_fill)]
    stage.update(_wkv_batch_stage(r, k, v, a, ld, par(kkg_ref), par(kag_ref),
                                  fillers=[state_products(grp) for grp in groups if grp], early=stage))
    same_half = _same_half()
    eye_full = _iota((LANES, LANES), 0) == _iota((LANES, LANES), 1)
    rk, lng, lnb = par(rk_ref), par(lng_ref), par(lnb_ref)

    def gate(s, y):
        sel = [s * npair + j for j in pairs]
        return _wkv_gate(y, *([xs[i] for i in sel] for xs in (r, stage["kh"], v, g, rk, lng, lnb)),
                         stage["ones_blk"])

    ys, zs = [], []
    for s in range(n_sub):
        idx = [s * npair + j for j in pairs]
        if s > 0:
            for j in pairs:
                from_state[j] = _mm3_top(stage["x_hi"][idx[j]], stage["at_lo"][idx[j]], _split(st[j]), NN)
            zs += gate(s - 1, ys)
        sub = {key: [stage[key][i] for i in idx] for key in ("tinv", "from_v", "lb_r")}
        us, ys = _wkv_finish(sub, [from_state[j][0] for j in pairs], [from_state[j][1] for j in pairs])
        upd = [_mm4k(_split(stage["bke"][idx[j]]), _split(jnp.concatenate([us[j], v[idx[j]]], axis=0)), TN)
               for j in pairs]
        for j in pairs:
            w_col = jnp.sum(jnp.where(eye_full, jnp.exp(stage["tot"][idx[j]]), 0.0), axis=1, keepdims=True)
            st[j] = w_col * st[j] + jnp.where(same_half, upd[j], 0.0)
    zs += gate(n_sub - 1, ys)
    for i, (s, j) in enumerate(items):
        z_ref[s * c:(s + 1) * c, j * LANES:(j + 1) * LANES] = zs[i]
    for j in pairs:
        st_ref[j] = st[j]
        sout_ref[0, j] = st[j]


def _wkv_chunk_call(r, k, v, a, ld, g, p, nbatch, seq):
    rows = WKV_CHUNK * WKV_CHUNKS_PER_STEP
    nstep = seq // rows
    npair = D_MODEL // LANES
    tile = pl.BlockSpec((rows, D_MODEL), lambda b, j, t: (b * nstep + t, 0))
    par = pl.BlockSpec((1, D_MODEL), lambda b, j, t: (0, 0))
    return pl.pallas_call(
        _wkv_chunk_kernel,
        grid=(nbatch, 1, nstep),
        in_specs=[tile] * 6 + [par] * 5,
        out_specs=[tile, pl.BlockSpec((1, npair, LANES, LANES), lambda b, j, t: (b, 0, 0, 0))],
        out_shape=[jax.ShapeDtypeStruct((nbatch * seq, D_MODEL), BF16),
                   jax.ShapeDtypeStruct((nbatch, npair, LANES, LANES), F32)],
        scratch_shapes=[pltpu.VMEM((npair, LANES, LANES), F32)],
        compiler_params=_cparams(3),
        name="wkv_chunk",
    )(r, k, v, a, ld, g, p["k_k"], p["k_a"], p["r_k"], p["ln_g"], p["ln_b"])


WKV_LANES_UNROLL = 4


def _wkv_lanes_kernel(r_ref, k_ref, v_ref, a_ref, ld_ref, g_ref, kkg_ref, kag_ref, rk_ref, lng_ref, lnb_ref,
                      s_ref, z_ref, sout_ref, prep_ref, y_ref, *, seq_len):
    n = R_HEAD_DIM
    nseq = s_ref.shape[3]
    eye = _iota((LANES, LANES), 0) == _iota((LANES, LANES), 1)

    def column(ref):
        return jnp.sum(jnp.where(eye, ref[...], 0.0), axis=1, keepdims=True)

    kkg, kag, rk, lng, lnb = (column(ref) for ref in (kkg_ref, kag_ref, rk_ref, lng_ref, lnb_ref))

    def token(ref, t):
        return ref[pl.ds(t, nseq, stride=seq_len), :].T

    bonus = []
    for t in range(seq_len):
        r, k, v, a = (token(ref, t) for ref in (r_ref, k_ref, v_ref, a_ref))
        w = jnp.exp(token(ld_ref, t))
        kkx = k * kkg
        kh = k * (1.0 + (a - 1.0) * kag)
        rkk = r * kh * rk
        tiles = []
        for hh in range(2):
            rows = slice(hh * n, (hh + 1) * n)
            nrm = jnp.sqrt(jnp.sum(kkx[rows] * kkx[rows], axis=0, keepdims=True))
            kk = kkx[rows] / jnp.maximum(nrm, 1e-12)
            for q, val in enumerate((w[rows], -kk, kk * a[rows], kh[rows], r[rows], v[rows])):
                prep_ref[q, t, hh] = val
            tiles.append(jnp.sum(rkk[rows], axis=0, keepdims=True) * v[rows])
        bonus.append(tiles)

    for hh in range(2):
        def advance(i, carry, hh=hh):
            for u in range(WKV_LANES_UNROLL):
                vi = i * WKV_LANES_UNROLL + u
                slab = s_ref[hh, vi]
                for t in range(seq_len):
                    w, av, bv, kh, r = (prep_ref[q, t, hh] for q in range(5))
                    vrow = prep_ref[5, t, hh, pl.ds(vi, 1), :]
                    sa = jnp.sum(slab * av, axis=0, keepdims=True)
                    slab = slab * w + sa * bv + vrow * kh
                    y_ref[t, hh, pl.ds(vi, 1), :] = jnp.sum(slab * r, axis=0, keepdims=True)
                sout_ref[hh, vi] = slab
            return carry

        lax.fori_loop(0, n // WKV_LANES_UNROLL, advance, 0)

    for t in range(seq_len):
        parts = []
        for hh in range(2):
            rows = slice(hh * n, (hh + 1) * n)
            y = y_ref[t, hh]
            d = y - jnp.mean(y, axis=0, keepdims=True)
            var = jnp.mean(d * d, axis=0, keepdims=True)
            parts.append(d * lax.rsqrt(var + GN_EPS) * lng[rows] + lnb[rows] + bonus[t][hh])
        g = token(g_ref, t)
        z = jnp.concatenate(parts, axis=0) * (g * _sigmoid(g))
        z_ref[pl.ds(t, nseq, stride=seq_len), :] = z.T


def _wkv_lanes_call(r, k, v, a, ld, g, p, state_hvkb, seq_len):
    n = r.shape[0]
    nseq = state_hvkb.shape[3]
    tile = pl.BlockSpec((n, LANES), lambda j: (0, j))
    par = pl.BlockSpec((1, LANES), lambda j: (0, j))
    sspec = pl.BlockSpec((2, R_HEAD_DIM, R_HEAD_DIM, nseq), lambda j: (j, 0, 0, 0))
    return pl.pallas_call(
        functools.partial(_wkv_lanes_kernel, seq_len=seq_len),
        grid=(D_MODEL // LANES,),
        in_specs=[tile] * 6 + [par] * 5 + [sspec],
        out_specs=[tile, sspec],
        out_shape=[jax.ShapeDtypeStruct((n, D_MODEL), F32),
                   jax.ShapeDtypeStruct(state_hvkb.shape, F32)],
        scratch_shapes=[pltpu.VMEM((6, seq_len, 2, R_HEAD_DIM, nseq), F32),
                        pltpu.VMEM((seq_len, 2, R_HEAD_DIM, nseq), F32)],
        compiler_params=_cparams(1),
        name="wkv_lanes",
    )(r, k, v, a, ld, g, p["k_k"], p["k_a"], p["r_k"], p["ln_g"], p["ln_b"], state_hvkb)


RWKV_OUT_PIECES = 8


def _rwkv_out_kernel(*refs):
    n = RWKV_OUT_PIECES
    z_refs, h_refs, (wout_ref, fg_ref, out_ref) = refs[:n], refs[n:2 * n], refs[2 * n:]
    z = jnp.concatenate([ref[...].astype(BF16) for ref in z_refs], axis=0)
    h = jnp.concatenate([ref[...] for ref in h_refs], axis=0)
    h2 = h + jnp.dot(z, wout_ref[...], preferred_element_type=F32)
    out_ref[...] = _rmsnorm(h2, fg_ref[...])


def _rwkv_out_call(z, h, p, nbatch, ntile, tm, pieces_per_batch, skip):
    piece = tm // RWKV_OUT_PIECES
    dst = lambda b, i: (b * ntile + i, 0)
    pspec = [pl.BlockSpec((piece, D_MODEL),
                          functools.partial(lambda b, i, kk: (b * pieces_per_batch + i * RWKV_OUT_PIECES + skip + kk, 0),
                                            kk=kk))
             for kk in range(RWKV_OUT_PIECES)]
    return pl.pallas_call(
        _rwkv_out_kernel,
        grid=(nbatch, ntile),
        in_specs=pspec + pspec + [pl.BlockSpec((D_MODEL, D_MODEL), lambda b, i: (0, 0)),
                                  pl.BlockSpec((1, D_MODEL), lambda b, i: (0, 0))],
        out_specs=pl.BlockSpec((tm, D_MODEL), dst),
        out_shape=jax.ShapeDtypeStruct((nbatch * ntile * tm, D_MODEL), F32),
        compiler_params=_cparams(2),
        name="rwkv_out",
    )(*([z] * RWKV_OUT_PIECES), *([h] * RWKV_OUT_PIECES), p["w_out"], p["final_gain"])


def _prompt_bucket():
    assert WINDOW == BLOCK
    rel = (np.arange(BLOCK)[:, None] - np.arange(BLOCK)[None, :]) % BLOCK
    return _t5_bucket_np(rel)


def _sample_bucket(keep, t_new, slot):
    t = (np.arange(SUBLANES) % t_new)[:, None]
    j = np.arange(SAMPLE_KEYS)[None, :]
    own = j - keep - slot * t_new
    rel = np.where(j < keep, keep + t - j, t - own)
    ok = (rel >= 0) & (rel < WINDOW) & ((j < keep) | ((own >= 0) & (own < t_new)))
    return np.where(ok, _t5_bucket_np(rel), -1).astype(np.int32)


def kernel(x_prompt, x_sample, cache_win_k, cache_win_v, state_wkv, state_shift, meta_tokens, rel_bias_table, norm_gain, final_gain, attn_w_in, attn_sinks, attn_w_out, rwkv_mu, rwkv_w_in, rwkv_w0, rwkv_w1, rwkv_w2, rwkv_a0, rwkv_a1, rwkv_a2, rwkv_k_k, rwkv_k_a, rwkv_r_k, rwkv_ln_gamma, rwkv_ln_beta, rwkv_w_out):
    nb, seq, _ = x_prompt.shape
    ns, t_new, _ = x_sample.shape
    keep = cache_win_k.shape[2]
    lp = seq + BLOCK
    nblk = lp // BLOCK
    row = lambda x: x.reshape(1, D_MODEL)

    w_in0 = attn_w_in[0].astype(BF16)
    w_out0 = attn_w_out[0].astype(BF16)
    gain0 = row(norm_gain[0])
    sinks = attn_sinks[0]
    rp = dict(gain=row(norm_gain[1]), mu=rwkv_mu[0], w_in=rwkv_w_in[0].astype(BF16),
              w0=row(rwkv_w0[0]), w1=rwkv_w1[0].astype(BF16), w2=rwkv_w2[0].astype(BF16),
              a0=row(rwkv_a0[0]), a1=rwkv_a1[0].astype(BF16), a2=rwkv_a2[0].astype(BF16),
              k_k=row(rwkv_k_k[0]), k_a=row(rwkv_k_a[0]), r_k=row(rwkv_r_k[0]), ln_g=row(rwkv_ln_gamma[0]),
              ln_b=row(rwkv_ln_beta[0]), w_out=rwkv_w_out[0].astype(BF16),
              final_gain=row(final_gain))

    bias_p, *bias_s = _bias_call(rel_bias_table, [_prompt_bucket()]
                                 + [_sample_bucket(keep, t_new, slot) for slot in range(2)])

    head = jnp.concatenate([jnp.zeros((PAD, D_MODEL), F32), meta_tokens.astype(F32)], axis=0)
    xp = x_prompt.reshape(nb * seq, D_MODEL)
    q, kv, g = _attn_proj_call(xp, head, gain0, w_in0, BF16, nb, lp // ATTN_PROJ_ROWS,
                               ATTN_PROJ_ROWS // BLOCK, BLOCK)
    h1 = _attn_prompt_call(sinks, q, kv, g, head, xp, bias_p, w_out0, nb, nblk)
    kv3 = kv.reshape(nb, lp, 2 * A_KV_WIDTH)[:, lp - WINDOW:, :]
    win_k_p = kv3[:, :, :A_KV_WIDTH].reshape(1, nb, WINDOW, A_KV_HEADS, A_HEAD_DIM)
    win_v_p = kv3[:, :, A_KV_WIDTH:].reshape(1, nb, WINDOW, A_KV_HEADS, A_HEAD_DIM)

    shift0 = jnp.zeros((nb, 1, D_MODEL), F32)
    r, k, v, g1, ld, a, xlast = _rwkv_proj_call(h1, shift0, rp, nb, lp // RWKV_PROJ_ROWS, RWKV_PROJ_ROWS, None)
    z, st = _wkv_chunk_call(r, k, v, a, ld, g1, rp, nb, lp)
    y_prompt = _rwkv_out_call(z, h1, rp, nb, seq // RWKV_OUT_ROWS, RWKV_OUT_ROWS,
                              lp * RWKV_OUT_PIECES // RWKV_OUT_ROWS, BLOCK * RWKV_OUT_PIECES // RWKV_OUT_ROWS)
    y_prompt = y_prompt.reshape(nb, seq, D_MODEL)
    st = st.reshape(nb, D_MODEL // LANES, 2, R_HEAD_DIM, 2, R_HEAD_DIM)
    st = jnp.stack([st[:, :, 0, :, 0, :], st[:, :, 1, :, 1, :]], axis=2)
    wkv_p = jnp.swapaxes(st, -1, -2).reshape(1, nb, R_HEADS, R_HEAD_DIM, R_HEAD_DIM)
    shift_p = xlast.reshape(1, nb, D_MODEL)

    xs = x_sample.reshape(ns * t_new, D_MODEL)
    qs, kvs, gs = _attn_proj_call(xs, None, gain0, w_in0, F32, 1, 1, 1, ns * t_new)
    ck = jnp.swapaxes(cache_win_k[0].reshape(ns, keep, A_KV_WIDTH), 1, 2)
    cv = jnp.swapaxes(cache_win_v[0].reshape(ns, keep, A_KV_WIDTH), 1, 2)
    h1s, nk, nv = _attn_sample_call(sinks, qs, kvs, gs, xs, ck, cv, bias_s[0], bias_s[1], w_out0, t_new)
    win_k_s = jnp.swapaxes(nk, 1, 2).reshape(1, ns, keep, A_KV_HEADS, A_HEAD_DIM)
    win_v_s = jnp.swapaxes(nv, 1, 2).reshape(1, ns, keep, A_KV_HEADS, A_HEAD_DIM)

    shift_rows = jnp.repeat(state_shift[0], t_new, axis=0)
    tms = 256
    rs, ks, vs, g1s, lds, as_, xns = _rwkv_proj_call(h1s, shift_rows, rp, 1, ns * t_new // tms, tms, t_new)
    zs, st_s = _wkv_lanes_call(rs, ks, vs, as_, lds, g1s, rp, jnp.transpose(state_wkv[0], (1, 2, 3, 0)), t_new)
    y_sample = _rwkv_out_call(zs, h1s, rp, 1, 1, ns * t_new, RWKV_OUT_PIECES, 0)
    y_sample = y_sample.reshape(ns, t_new, D_MODEL)
    wkv_s = jnp.transpose(st_s, (3, 0, 1, 2))[None]
    shift_s = xns.reshape(ns, t_new, D_MODEL)[:, t_new - 1][None]

    return (y_prompt, y_sample, win_k_p, win_v_p, wkv_p, shift_p, win_k_s, win_v_s, wkv_s, shift_s)
```

```python
import functools
import math

import numpy as np
import jax
import jax.numpy as jnp
from jax import lax
from jax.experimental import pallas as pl
from jax.experimental.pallas import tpu as pltpu

F32 = jnp.float32
BF16 = jnp.bfloat16

D_MODEL = 1024
N_META = 16
RMS_EPS = 1e-6
A_HEADS = 16
A_KV_HEADS = 4
A_HEAD_DIM = 64
A_WIDTH = A_HEADS * A_HEAD_DIM
A_KV_WIDTH = A_KV_HEADS * A_HEAD_DIM
WINDOW = 128
BLOCK = 128
N_BUCKETS = 32
MAX_DISTANCE = 128
R_HEAD_DIM = 64
R_HEADS = D_MODEL // R_HEAD_DIM
GN_EPS = 64e-5

LANES = 128
SUBLANES = 8
PAD = BLOCK - N_META
NEG = -1e30
WKV_CHUNK = 64
WKV_CHUNKS_PER_STEP = 6
ATTN_BLOCKS_PER_STEP = 3
ATTN_PROJ_ROWS = 1408
RWKV_PROJ_ROWS = 528
RWKV_OUT_ROWS = 1024
VMEM_LIMIT = 56 * 1024 * 1024


def _cparams(n_axes):
    return pltpu.CompilerParams(dimension_semantics=("arbitrary",) * n_axes,
                                vmem_limit_bytes=VMEM_LIMIT)


def _rmsnorm(x, gain):
    return x * lax.rsqrt(jnp.mean(x * x, axis=-1, keepdims=True) + RMS_EPS) * gain


def _sigmoid(x):
    return 1.0 / (1.0 + jnp.exp(-x))


def _iota(shape, dim):
    return lax.broadcasted_iota(jnp.int32, shape, dim)


def _t5_bucket_np(rel):
    n = np.maximum(rel, 0)
    max_exact = N_BUCKETS // 2
    nf = np.maximum(n, max_exact).astype(np.float32)
    scale = np.float32(math.log(MAX_DISTANCE / max_exact))
    large = max_exact + (np.log(nf / np.float32(max_exact)) / scale
                         * np.float32(N_BUCKETS - max_exact)).astype(np.int32)
    large = np.minimum(large, N_BUCKETS - 1)
    return np.where(n < max_exact, n, large).astype(np.int32)


def _bias_kernel(table_ref, *refs):
    h = pl.program_id(0)
    n = len(refs) // 2
    for bucket_ref, out_ref in zip(refs[:n], refs[n:]):
        bk = bucket_ref[...]
        acc = jnp.full(bk.shape, NEG, F32)
        for b in range(N_BUCKETS):
            acc = jnp.where(bk == b, table_ref[b, h], acc)
        out_ref[0] = acc


def _bias_call(table, buckets_np):
    return pl.pallas_call(
        _bias_kernel,
        grid=(A_HEADS,),
        in_specs=[pl.BlockSpec(memory_space=pltpu.SMEM)]
                 + [pl.BlockSpec(bk.shape, lambda h: (0, 0)) for bk in buckets_np],
        out_specs=[pl.BlockSpec((1,) + bk.shape, lambda h: (h, 0, 0)) for bk in buckets_np],
        out_shape=[jax.ShapeDtypeStruct((A_HEADS,) + bk.shape, F32) for bk in buckets_np],
        compiler_params=_cparams(1),
        name="bias_expand",
    )(table, *(jnp.asarray(bk) for bk in buckets_np))


def _attn_proj_kernel(head_ref, *refs, n_piece):
    x_refs, (gain_ref, w_ref, q_ref, kv_ref, g_ref) = refs[:n_piece], refs[n_piece:]
    first = x_refs[0][...]
    if head_ref is not None:
        first = jnp.where(pl.program_id(1) == 0, head_ref[...], first)
    x = jnp.concatenate([first] + [ref[...] for ref in x_refs[1:]], axis=0)
    xn = _rmsnorm(x, gain_ref[...])
    proj = jnp.dot(xn.astype(BF16), w_ref[...], preferred_element_type=F32)
    q_ref[...] = (proj[:, :A_WIDTH] * (A_HEAD_DIM ** -0.5)).astype(q_ref.dtype)
    kv_ref[...] = proj[:, A_WIDTH:A_WIDTH + 2 * A_KV_WIDTH]
    g_ref[...] = proj[:, A_WIDTH + 2 * A_KV_WIDTH:]


def _attn_proj_call(x2d, head, gain, w_bf16, q_dtype, nbatch, ntile, n_piece, piece):
    tm = n_piece * piece
    wcols = w_bf16.shape[1]
    per_seq = x2d.shape[0] // (nbatch * piece)
    lead = 0 if head is None else 1
    dst = lambda b, i: (b * ntile + i, 0)
    xspec = [pl.BlockSpec((piece, D_MODEL),
                          functools.partial(lambda b, i, kk: (b * per_seq + jnp.maximum(i * n_piece + kk - lead, 0), 0),
                                            kk=kk))
             for kk in range(n_piece)]
    kern = functools.partial(_attn_proj_kernel, n_piece=n_piece)
    operands = [x2d] * n_piece + [gain, w_bf16]
    if head is None:
        kern = functools.partial(kern, None)
        head_spec = []
    else:
        head_spec = [pl.BlockSpec((piece, D_MODEL), lambda b, i: (0, 0))]
        operands = [head] + operands
    n = nbatch * ntile * tm
    return pl.pallas_call(
        kern,
        grid=(nbatch, ntile),
        in_specs=head_spec + xspec + [pl.BlockSpec((1, D_MODEL), lambda b, i: (0, 0)),
                                      pl.BlockSpec((D_MODEL, wcols), lambda b, i: (0, 0))],
        out_specs=[pl.BlockSpec((tm, A_WIDTH), dst),
                   pl.BlockSpec((tm, 2 * A_KV_WIDTH), dst),
                   pl.BlockSpec((tm, A_WIDTH), dst)],
        out_shape=[jax.ShapeDtypeStruct((n, A_WIDTH), q_dtype),
                   jax.ShapeDtypeStruct((n, 2 * A_KV_WIDTH), F32),
                   jax.ShapeDtypeStruct((n, A_WIDTH), F32)],
        compiler_params=_cparams(2),
        name="attn_proj",
    )(*operands)


def _padded_kv_tiles(kv, c):
    lo = _iota((1, LANES), 1) < A_HEAD_DIM
    j = c // 2
    out = []
    for base in (0, A_KV_WIDTH):
        t = kv[:, base + j * LANES: base + (j + 1) * LANES]
        tr = pltpu.roll(t, A_HEAD_DIM, 1)
        if c % 2 == 0:
            even, odd = jnp.where(lo, t, 0.0), jnp.where(lo, 0.0, tr)
        else:
            even, odd = jnp.where(lo, tr, 0.0), jnp.where(lo, 0.0, t)
        out += [even.astype(BF16), odd.astype(BF16)]
    return out


def _mm_nt(a, b):
    return lax.dot_general(a, b, (((1,), (1,)), ((), ())), preferred_element_type=F32)


def _attn_prompt_kernel(sinks_ref, q_ref, kvc_ref, kvp_ref, g_ref, head_ref, *refs):
    nb = ATTN_BLOCKS_PER_STEP
    x_refs, (bias_ref, wout_ref, out_ref, og_ref) = refs[:nb], refs[nb:]
    i = pl.program_id(1)
    stack = 2 * BLOCK
    row, col = _iota((stack, BLOCK), 0) & (BLOCK - 1), _iota((stack, BLOCK), 1)
    upper = _iota((stack, 1), 0) >= BLOCK
    own = col <= row
    chains = [(c, idx) for c in range(A_KV_HEADS) for idx in range(2)]
    n = range(len(chains))
    cur = [_padded_kv_tiles(kvp_ref[...], c) for c in range(A_KV_HEADS)]
    for j in range(nb):
        rows = slice(j * BLOCK, (j + 1) * BLOCK)
        kvalid = (i * nb + j - 1 + own.astype(jnp.int32)) * BLOCK + col >= PAD
        prev, cur = cur, [_padded_kv_tiles(kvc_ref[rows, :], c) for c in range(A_KV_HEADS)]
        s, sink = [], []
        for c, idx in chains:
            q2 = q_ref[rows, 2 * c * LANES:(2 * c + 2) * LANES]
            q2 = jnp.concatenate([q2[:, :LANES], q2[:, LANES:]], axis=0)
            sc = jnp.where(own, _mm_nt(q2, cur[c][idx]), _mm_nt(q2, prev[c][idx]))
            bias = jnp.concatenate([bias_ref[4 * c + idx], bias_ref[4 * c + 2 + idx]], axis=0)
            s.append(jnp.where(kvalid, sc + bias, NEG))
            sink.append(jnp.where(upper, sinks_ref[4 * c + 2 + idx], sinks_ref[4 * c + idx]))
        m = [jnp.maximum(jnp.max(s[t], axis=1, keepdims=True), sink[t]) for t in n]
        p = [jnp.exp(s[t] - m[t]) for t in n]
        den = [jnp.sum(p[t], axis=1, keepdims=True) + jnp.exp(sink[t] - m[t]) for t in n]
        o = []
        for t, (c, idx) in enumerate(chains):
            pv = (jnp.dot(jnp.where(own, p[t], 0.0).astype(BF16), cur[c][2 + idx], preferred_element_type=F32)
                  + jnp.dot(jnp.where(own, 0.0, p[t]).astype(BF16), prev[c][2 + idx], preferred_element_type=F32))
            o.append(pv * (1.0 / den[t]))
        for c in range(A_KV_HEADS):
            both = o[2 * c] + o[2 * c + 1]
            for half in range(2):
                sl = slice((2 * c + half) * LANES, (2 * c + half + 1) * LANES)
                gt = g_ref[rows, sl]
                og_ref[rows, sl] = (both[half * BLOCK:(half + 1) * BLOCK] * (gt * _sigmoid(gt))).astype(BF16)
    resid = jnp.concatenate([jnp.where(i == 0, head_ref[...], x_refs[0][...])]
                            + [ref[...] for ref in x_refs[1:]], axis=0)
    out_ref[...] = resid + jnp.dot(og_ref[...], wout_ref[...], preferred_element_type=F32)


def _attn_prompt_call(sinks, q, kv, g, head, x2d, bias, wout_bf16, nbatch, nblk):
    n = q.shape[0]
    nb = ATTN_BLOCKS_PER_STEP
    nstep = nblk // nb
    rows = nb * BLOCK
    row = lambda b, i: (b * nstep + i, 0)
    prev = lambda b, i: (b * nblk + jnp.maximum(i * nb - 1, 0), 0)
    xrow = [functools.partial(lambda b, i, j: (b * (nblk - 1) + jnp.maximum(i * nb + j - 1, 0), 0), j=j)
            for j in range(nb)]
    return pl.pallas_call(
        _attn_prompt_kernel,
        grid=(nbatch, nstep),
        in_specs=[pl.BlockSpec(memory_space=pltpu.SMEM),
                  pl.BlockSpec((rows, A_WIDTH), row),
                  pl.BlockSpec((rows, 2 * A_KV_WIDTH), row),
                  pl.BlockSpec((BLOCK, 2 * A_KV_WIDTH), prev),
                  pl.BlockSpec((rows, A_WIDTH), row),
                  pl.BlockSpec((BLOCK, D_MODEL), lambda b, i: (0, 0))]
                 + [pl.BlockSpec((BLOCK, D_MODEL), xrow[j]) for j in range(nb)]
                 + [pl.BlockSpec((A_HEADS, BLOCK, BLOCK), lambda b, i: (0, 0, 0)),
                    pl.BlockSpec((A_WIDTH, D_MODEL), lambda b, i: (0, 0))],
        out_specs=pl.BlockSpec((rows, D_MODEL), row),
        out_shape=jax.ShapeDtypeStruct((n, D_MODEL), F32),
        scratch_shapes=[pltpu.VMEM((rows, A_WIDTH), BF16)],
        compiler_params=_cparams(2),
        name="attn_prompt",
    )(sinks, q, kv, kv, g, head, *([x2d] * nb), bias, wout_bf16)


SAMPLE_SB = 8
SAMPLE_KEYS = 2 * BLOCK


def _attn_sample_kernel(sinks_ref, q_ref, kvn_ref, g_ref, h_ref, ck_ref, cv_ref, bias0_ref, bias1_ref,
                        wout_ref, out_ref, nk_ref, nv_ref, og_ref, *, t_new):
    keep = ck_ref.shape[2]
    lo = _iota((1, LANES), 1) < A_HEAD_DIM
    stack = 8 * SUBLANES
    own = (_iota((stack, 1), 0) & (SUBLANES - 1)) // t_new
    piece = _iota((stack, 1), 0) // SUBLANES
    bias_refs = (bias0_ref, bias1_ref)
    n_tile = A_KV_WIDTH // LANES
    pair_ids = range(SAMPLE_SB * t_new // SUBLANES)

    bias_c, bias_n, sink = [], [], []
    for j in range(n_tile):
        heads = slice(8 * j, 8 * j + 8)
        bias_c.append(bias0_ref[heads, :, :keep].reshape(stack, keep))
        bias_n.append([ref[heads, :, keep:keep + SUBLANES].reshape(stack, SUBLANES) for ref in bias_refs])
        col = jnp.zeros((stack, 1), F32)
        for gq in range(8):
            col = jnp.where(piece == gq, sinks_ref[8 * j + gq], col)
        sink.append(col)

    chains = [(p, j, s) for p in pair_ids for j in range(n_tile) for s in range(SUBLANES // t_new)]
    qs = {}
    for p in pair_ids:
        rows = slice(p * SUBLANES, (p + 1) * SUBLANES)
        for j in range(n_tile):
            parts = []
            for gq in range(8):
                t = q_ref[rows, (4 * j + gq // 2) * LANES:(4 * j + gq // 2 + 1) * LANES]
                want_lo = gq < 4
                if (gq % 2 == 0) != want_lo:
                    t = pltpu.roll(t, A_HEAD_DIM, 1)
                parts.append(jnp.where(lo, t, 0.0) if want_lo else jnp.where(lo, 0.0, t))
            qs[p, j] = jnp.concatenate(parts, axis=0).astype(BF16)

    def kv_tiles(p, j, s, base):
        seq = p * (SUBLANES // t_new) + s
        cache = (ck_ref if base == 0 else cv_ref)[seq][j * LANES:(j + 1) * LANES, :].astype(BF16)
        new = kvn_ref[p * SUBLANES:(p + 1) * SUBLANES, base + j * LANES:base + (j + 1) * LANES].astype(BF16)
        return cache, new

    sc, sn = [], []
    for p, j, s in chains:
        kc, kn = kv_tiles(p, j, s, 0)
        sc.append(jnp.dot(qs[p, j], kc, preferred_element_type=F32) + bias_c[j])
        sn.append(_mm_nt(qs[p, j], kn) + bias_n[j][s])
    mx = [jnp.maximum(jnp.maximum(jnp.max(sc[i], axis=1, keepdims=True),
                                  jnp.max(sn[i], axis=1, keepdims=True)), sink[chains[i][1]])
          for i in range(len(chains))]
    pc = [jnp.exp(sc[i] - mx[i]) for i in range(len(chains))]
    pn = [jnp.exp(sn[i] - mx[i]) for i in range(len(chains))]
    den = [jnp.sum(pc[i], axis=1, keepdims=True) + jnp.sum(pn[i], axis=1, keepdims=True)
           + jnp.exp(sink[chains[i][1]] - mx[i]) for i in range(len(chains))]
    outs = {}
    for i, (p, j, s) in enumerate(chains):
        vc, vn = kv_tiles(p, j, s, A_KV_WIDTH)
        o = (_mm_nt(pc[i].astype(BF16), vc)
             + jnp.dot(pn[i].astype(BF16), vn, preferred_element_type=F32)) * (1.0 / den[i])
        outs[p, j] = o if s == 0 else jnp.where(own == s, o, outs[p, j])
    for p in pair_ids:
        rows = slice(p * SUBLANES, (p + 1) * SUBLANES)
        for j in range(n_tile):
            o = outs[p, j]
            for gg in range(4):
                even = o[2 * gg * SUBLANES:(2 * gg + 1) * SUBLANES]
                odd = o[(2 * gg + 1) * SUBLANES:(2 * gg + 2) * SUBLANES]
                if gg < 2:
                    tile = jnp.where(lo, even, pltpu.roll(odd, A_HEAD_DIM, 1))
                else:
                    tile = jnp.where(lo, pltpu.roll(even, A_HEAD_DIM, 1), odd)
                sl = slice((4 * j + gg) * LANES, (4 * j + gg + 1) * LANES)
                gt = g_ref[rows, sl]
                og_ref[rows, sl] = tile * (gt * _sigmoid(gt))
    lane_pos = _iota((SUBLANES, keep), 1)
    old = _iota((1, keep), 1) < keep - t_new
    for p in pair_ids:
        new8 = kvn_ref[p * SUBLANES:(p + 1) * SUBLANES, :]
        for s in range(SUBLANES // t_new):
            seq = p * (SUBLANES // t_new) + s
            sel = (lane_pos == _iota((SUBLANES, keep), 0) + (keep - t_new - s * t_new)).astype(F32)
            placed = _mm3(new8, sel, TN)
            for ref_in, ref_out, base in ((ck_ref, nk_ref, 0), (cv_ref, nv_ref, A_KV_WIDTH)):
                shifted = pltpu.roll(ref_in[seq], keep - t_new, 1)
                ref_out[seq] = jnp.where(old, shifted, placed[base:base + A_KV_WIDTH])
    out_ref[...] = h_ref[...] + jnp.dot(og_ref[...].astype(BF16), wout_ref[...],
                                        preferred_element_type=F32)


def _attn_sample_call(sinks, q, kv, g, h, cache_k, cache_v, bias0, bias1, wout_bf16, t_new):
    nseq, keep = cache_k.shape[0], cache_k.shape[2]
    rows = SAMPLE_SB * t_new
    row = lambda i: (i, 0)
    cspec = pl.BlockSpec((SAMPLE_SB, A_KV_WIDTH, keep), lambda i: (i, 0, 0))
    bspec = pl.BlockSpec((A_HEADS, SUBLANES, SAMPLE_KEYS), lambda i: (0, 0, 0))
    return pl.pallas_call(
        functools.partial(_attn_sample_kernel, t_new=t_new),
        grid=(nseq // SAMPLE_SB,),
        in_specs=[pl.BlockSpec(memory_space=pltpu.SMEM),
                  pl.BlockSpec((rows, A_WIDTH), row),
                  pl.BlockSpec((rows, 2 * A_KV_WIDTH), row),
                  pl.BlockSpec((rows, A_WIDTH), row),
                  pl.BlockSpec((rows, D_MODEL), row),
                  cspec, cspec, bspec, bspec,
                  pl.BlockSpec((A_WIDTH, D_MODEL), lambda i: (0, 0))],
        out_specs=[pl.BlockSpec((rows, D_MODEL), row), cspec, cspec],
        out_shape=[jax.ShapeDtypeStruct((nseq * t_new, D_MODEL), F32),
                   jax.ShapeDtypeStruct(cache_k.shape, F32),
                   jax.ShapeDtypeStruct(cache_v.shape, F32)],
        scratch_shapes=[pltpu.VMEM((rows, A_WIDTH), F32)],
        compiler_params=_cparams(1),
        name="attn_sample",
    )(sinks, q, kv, g, h, cache_k, cache_v, bias0, bias1, wout_bf16)


def _rwkv_proj_kernel(h_ref, shift_ref, gain_ref, mu_ref, win_ref, w0_ref, w1_ref, w2_ref,
                      a0_ref, a1_ref, a2_ref,
                      r_ref, k_ref, v_ref, g_ref, ld_ref, a_ref, xn_ref, *scratch, seq_len):
    xn = _rmsnorm(h_ref[...], gain_ref[...])
    tm = xn.shape[0]
    rolled = pltpu.roll(xn, 1, 0)
    row = _iota((tm, 1), 0)
    if seq_len is None:
        carry_ref, = scratch

        @pl.when(pl.program_id(1) == 0)
        def _():
            carry_ref[...] = shift_ref[0]

        xprev = jnp.where(row == 0, carry_ref[...], rolled)
        carry_ref[...] = xn[tm - 1:tm, :]
        xn_ref[0] = xn[tm - 1:tm, :]
    else:
        xprev = jnp.where(row % seq_len == 0, shift_ref[...], rolled)
        xn_ref[...] = xn
    dx = xprev - xn

    def mix(c):
        return (xn + dx * mu_ref[c:c + 1, :]).astype(BF16)

    for c, o_ref in enumerate((r_ref, k_ref, v_ref, g_ref)):
        o_ref[...] = jnp.dot(mix(c), win_ref[c], preferred_element_type=F32)
    lw = jnp.tanh(jnp.dot(mix(4), w1_ref[...], preferred_element_type=F32))
    z = w0_ref[...] + jnp.dot(lw.astype(BF16), w2_ref[...], preferred_element_type=F32)
    ld_ref[...] = -math.exp(-0.5) * _sigmoid(z)
    la = jnp.dot(mix(5), a1_ref[...], preferred_element_type=F32)
    a_ref[...] = _sigmoid(a0_ref[...] + jnp.dot(la.astype(BF16), a2_ref[...],
                                                preferred_element_type=F32))


def _rwkv_proj_call(h, shift, p, nbatch, ntile, tm, seq_len):
    n = h.shape[0]
    row = lambda b, i: (b * ntile + i, 0)
    full2 = lambda b, i: (0, 0)
    if seq_len is None:
        shift_spec = pl.BlockSpec((1, 1, D_MODEL), lambda b, i: (b, 0, 0))
        xn_spec = pl.BlockSpec((1, 1, D_MODEL), lambda b, i: (b, 0, 0))
        xn_shape = jax.ShapeDtypeStruct((nbatch, 1, D_MODEL), F32)
        scratch = [pltpu.VMEM((1, D_MODEL), F32)]
    else:
        shift_spec = pl.BlockSpec((tm, D_MODEL), row)
        xn_spec = pl.BlockSpec((tm, D_MODEL), row)
        xn_shape = jax.ShapeDtypeStruct((n, D_MODEL), F32)
        scratch = []
    lora = p["w1"].shape[1]
    big = jax.ShapeDtypeStruct((n, D_MODEL), F32)
    return pl.pallas_call(
        functools.partial(_rwkv_proj_kernel, seq_len=seq_len),
        grid=(nbatch, ntile),
        in_specs=[pl.BlockSpec((tm, D_MODEL), row),
                  shift_spec,
                  pl.BlockSpec((1, D_MODEL), full2),
                  pl.BlockSpec(p["mu"].shape, full2),
                  pl.BlockSpec(p["w_in"].shape, lambda b, i: (0, 0, 0)),
                  pl.BlockSpec((1, D_MODEL), full2),
                  pl.BlockSpec((D_MODEL, lora), full2),
                  pl.BlockSpec((lora, D_MODEL), full2),
                  pl.BlockSpec((1, D_MODEL), full2),
                  pl.BlockSpec((D_MODEL, lora), full2),
                  pl.BlockSpec((lora, D_MODEL), full2)],
        out_specs=[pl.BlockSpec((tm, D_MODEL), row)] * 6 + [xn_spec],
        out_shape=[big] * 6 + [xn_shape],
        scratch_shapes=scratch,
        compiler_params=_cparams(2),
        name="rwkv_proj",
    )(h, shift, p["gain"], p["mu"], p["w_in"], p["w0"], p["w1"], p["w2"], p["a0"], p["a1"], p["a2"])


NN = (((1,), (0,)), ((), ()))
NT = (((1,), (1,)), ((), ()))
TN = (((0,), (0,)), ((), ()))


def _split(x):
    hi = x.astype(BF16)
    return hi, (x - hi.astype(F32)).astype(BF16)


def _mm(a, b, dn):
    return lax.dot_general(a, b, dn, preferred_element_type=F32)


def _mm3s(a, b, dn):
    (ah, al), (bh, bl) = a, b
    if dn == TN:
        both = _mm(jnp.concatenate([ah, al], axis=1), bh, dn)
        m = ah.shape[1]
    else:
        both = _mm(jnp.concatenate([ah, al], axis=0), bh, dn)
        m = ah.shape[0]
    return both[:m] + both[m:] + _mm(ah, bl, dn)


def _mm3(a, b, dn):
    return _mm3s(_split(a), _split(b), dn)


def _mm3_top(x_hi, x_lo_top, b, dn):
    bh, bl = b
    c = x_lo_top.shape[0]
    both = _mm(jnp.concatenate([x_hi, x_lo_top], axis=0), bh, dn)
    return both[:c] + both[2 * c:] + _mm(x_hi[:c], bl, dn), both[c:2 * c]


def _seg_sums(xs, ones_bf16, exact):
    rows = xs[0].shape[0]
    stacked = jnp.concatenate(xs, axis=0)
    if exact:
        hi, lo = _split(stacked)
        out = _mm(lo, ones_bf16, NN) + _mm(hi, ones_bf16, NN)
    else:
        out = _mm(stacked.astype(BF16), ones_bf16, NN)
    return [out[i * rows:(i + 1) * rows] for i in range(len(xs))]


def _lane_lo():
    return _iota((1, LANES), 1) < R_HEAD_DIM


def _same_half():
    return (_iota((LANES, LANES), 0) < R_HEAD_DIM) == (_iota((LANES, LANES), 1) < R_HEAD_DIM)


def _ones_blk():
    return _same_half().astype(BF16)


def _bd(z):
    lo = _lane_lo()
    return jnp.concatenate([jnp.where(lo, z, 0.0), jnp.where(lo, 0.0, z)], axis=0)


def _bd_swap(z):
    lo = _lane_lo()
    return jnp.concatenate([jnp.where(lo, 0.0, z), jnp.where(lo, z, 0.0)], axis=0)


def _split_map(z, f):
    hi, lw = _split(z)
    return f(hi), f(lw)


def _wkv_batch_stage(r, k, v, a, ld, kkg, kag, fillers=(), early=None):
    c = r[0].shape[0]
    pairs = range(len(r))
    half = R_HEAD_DIM
    lo = _lane_lo()
    t_row, t_col = _iota((c, LANES), 0), _iota((c, LANES), 1) & (half - 1)
    strict, incl = t_row > t_col, t_row >= t_col
    tri = (_iota((c, c), 0) >= _iota((c, c), 1)).astype(BF16)
    ones_blk = _ones_blk()
    eye_pair = (_iota((c, LANES), 0) == (_iota((c, LANES), 1) & (half - 1))).astype(F32)

    kkx = [k[j] * kkg[j] for j in pairs]
    ssq = _seg_sums([kkx[j] * kkx[j] for j in pairs], ones_blk, exact=True)
    ld_hi, ld_lo = _split(jnp.concatenate(ld, axis=1))
    cs_all = _mm(tri, ld_lo, NN) + _mm(tri, ld_hi, NN)
    cs = [cs_all[:, j * LANES:(j + 1) * LANES] for j in pairs]
    tot = [cs[j][c - 1:c, :] for j in pairs]
    x_hi, at_lo, kh_all, bke, vbd_s, lk_a, lk_r, lb_a, lb_r = ([] for _ in range(9))
    for j in pairs:
        kk = kkx[j] / jnp.maximum(jnp.sqrt(ssq[j]), 1e-12)
        kh = k[j] * (1.0 + (a[j] - 1.0) * kag[j])
        bv = kk * a[j]
        e_neg = jnp.exp(-cs[j])
        e_end = jnp.exp(tot[j] - cs[j])
        at, rt = -kk * jnp.exp(cs[j] - ld[j]), r[j] * jnp.exp(cs[j])
        at_hi, at_l = _split(at)
        x_hi.append(jnp.concatenate([at_hi, rt.astype(BF16)], axis=0))
        at_lo.append(at_l)
        kh_all.append(kh)
        bt, kt = _split(bv * e_neg), _split(kh * e_neg)
        y_hi, y_lo = (jnp.concatenate([p, q], axis=0) for p, q in zip(bt, kt))
        bke.append(jnp.concatenate([bv * e_end, kh * e_end], axis=0))
        vbd_s.append(_split_map(v[j], _bd))
        xa, xb = jnp.where(lo, x_hi[j], 0.0), jnp.where(lo, 0.0, x_hi[j])
        both = _mm(jnp.concatenate([xa, xb, jnp.where(lo, at_l, 0.0), jnp.where(lo, 0.0, at_l)], axis=0),
                   y_hi, NT)
        corr = _mm(jnp.concatenate([xa[:c], xb[:c]], axis=0), y_lo, NT)
        ga_a = pltpu.roll(both[:c] + both[4 * c:5 * c] + corr[:c], half, 1)
        ga_r = pltpu.roll(both[c:2 * c], half, 1)
        gb_a = both[2 * c:3 * c] + both[5 * c:] + corr[c:]
        gb_r = both[3 * c:4 * c]
        lk_a.append(jnp.where(strict, jnp.where(lo, ga_a, gb_a), 0.0))
        lk_r.append(jnp.where(incl, jnp.where(lo, ga_r, gb_r), 0.0))
        lb_a.append(jnp.where(strict, jnp.where(lo, gb_a, ga_a), 0.0))
        lb_r.append(jnp.where(incl, jnp.where(lo, gb_r, ga_r), 0.0))
    fillers = list(fillers)
    if early is not None:
        early.update(x_hi=x_hi, at_lo=at_lo)
    pw = lb_a
    acc = [eye_pair + pw[j] for j in pairs]
    pw = [_mm(pw[j].astype(BF16), _bd(pw[j].astype(BF16)), NN) for j in pairs]
    from_v = []
    for j in pairs:
        lk_hi, lk_lo = _split(lk_a[j])
        from_v.append(_mm3_top(jnp.concatenate([lk_hi, lk_r[j].astype(BF16)], axis=0), lk_lo, vbd_s[j], NN))
    for _ in range(int(math.log2(c)) - 2):
        both = [_mm(jnp.concatenate([pw[j], acc[j]], axis=0).astype(BF16), _bd(pw[j].astype(BF16)), NN)
                for j in pairs]
        if fillers:
            fillers.pop(0)()
        pw = [both[j][:c] for j in pairs]
        acc = [acc[j] + both[j][c:] for j in pairs]
    tinv = [acc[j] + _mm(acc[j].astype(BF16), _bd(pw[j].astype(BF16)), NN) for j in pairs]
    for fill in fillers:
        fill()
    return dict(x_hi=x_hi, at_lo=at_lo, kh=kh_all, lb_r=lb_r, tinv=tinv, from_v=from_v, bke=bke,
                tot=tot, ones_blk=ones_blk)


def _wkv_finish(stage, from_state_a, from_state_r):
    pairs = range(len(from_state_a))
    u = [_mm3s(_split(stage["tinv"][j]),
               _split_map(from_state_a[j] + stage["from_v"][j][0], _bd_swap), NN) for j in pairs]
    y = [from_state_r[j] + stage["from_v"][j][1]
         + _mm(stage["lb_r"][j].astype(BF16), _bd_swap(u[j].astype(BF16)), NN) for j in pairs]
    return u, y


def _wkv_gate(y, r, kh, v, g, rk, lng, lnb, ones_blk):
    n = len(y)
    pairs = range(n)
    inv_n = 1.0 / R_HEAD_DIM
    sums = _seg_sums([r[j] * kh[j] * rk[j] for j in pairs] + list(y), ones_blk, exact=False)
    rkk, mean = sums[:n], [s * inv_n for s in sums[n:]]
    d = [y[j] - mean[j] for j in pairs]
    var = [s * inv_n for s in _seg_sums([d[j] * d[j] for j in pairs], ones_blk, exact=False)]
    return [((d[j] * lax.rsqrt(var[j] + GN_EPS) * lng[j] + lnb[j] + rkk[j] * v[j])
             * (g[j] * _sigmoid(g[j]))).astype(BF16) for j in pairs]


def _wkv_chunk_kernel(r_ref, k_ref, v_ref, a_ref, ld_ref, g_ref, kkg_ref, kag_ref, rk_ref, lng_ref, lnb_ref,
                      z_ref, sout_ref, st_ref):
    c = WKV_CHUNK
    n_sub = r_ref.shape[0] // c
    npair = r_ref.shape[1] // LANES
    pairs = range(npair)
    items = [(s, j) for s in range(n_sub) for j in pairs]

    @pl.when(pl.program_id(2) == 0)
    def _():
        st_ref[...] = jnp.zeros_like(st_ref)

    def tile(ref, it):
        s, j = it
        return ref[s * c:(s + 1) * c, j * LANES:(j + 1) * LANES]

    def par(ref):
        return [ref[:, j * LANES:(j + 1) * LANES] for _, j in items]

    r, k, v, a, ld, g = ([tile(ref, it) for it in items] for ref in (r_ref, k_ref, v_ref, a_ref, ld_ref, g_ref))
    st = [st_ref[j] for j in pairs]
    from_state = {}
    stage = {}

    def state_products(group):
        def run():
            for j in group:
                from_state[j] = _mm3_top(stage["x_hi"][j], stage["at_lo"][j], _split(st[j]), NN)
        return run

    n_fill = 4
    groups = [list(pairs)[i::n_fill] for i in range(n_fill)]
    stage.update(_wkv_batch_stage(r, k, v, a, ld, par(kkg_ref), par(kag_ref),
                                  fillers=[state_products(grp) for grp in groups if grp], early=stage))
    same_half = _same_half()
    eye_full = _iota((LANES, LANES), 0) == _iota((LANES, LANES), 1)
    ys = []
    for s in range(n_sub):
        idx = [s * npair + j for j in pairs]
        if s > 0:
            for j in pairs:
                from_state[j] = _mm3_top(stage["x_hi"][idx[j]], stage["at_lo"][idx[j]], _split(st[j]), NN)
        sub = {key: [stage[key][i] for i in idx] for key in ("tinv", "from_v", "lb_r")}
        us, ys_s = _wkv_finish(sub, [from_state[j][0] for j in pairs], [from_state[j][1] for j in pairs])
        upd = [_mm3(stage["bke"][idx[j]], jnp.concatenate([us[j], v[idx[j]]], axis=0), TN) for j in pairs]
        ys += ys_s
        for j in pairs:
            w_col = jnp.sum(jnp.where(eye_full, jnp.exp(stage["tot"][idx[j]]), 0.0), axis=1, keepdims=True)
            st[j] = w_col * st[j] + jnp.where(same_half, upd[j], 0.0)
    zs = _wkv_gate(ys, r, stage["kh"], v, g, par(rk_ref), par(lng_ref), par(lnb_ref), stage["ones_blk"])
    for i, (s, j) in enumerate(items):
        z_ref[s * c:(s + 1) * c, j * LANES:(j + 1) * LANES] = zs[i]
    for j in pairs:
        st_ref[j] = st[j]
        sout_ref[0, j] = st[j]


def _wkv_chunk_call(r, k, v, a, ld, g, p, nbatch, seq):
    rows = WKV_CHUNK * WKV_CHUNKS_PER_STEP
    nstep = seq // rows
    npair = D_MODEL // LANES
    tile = pl.BlockSpec((rows, D_MODEL), lambda b, j, t: (b * nstep + t, 0))
    par = pl.BlockSpec((1, D_MODEL), lambda b, j, t: (0, 0))
    return pl.pallas_call(
        _wkv_chunk_kernel,
        grid=(nbatch, 1, nstep),
        in_specs=[tile] * 6 + [par] * 5,
        out_specs=[tile, pl.BlockSpec((1, npair, LANES, LANES), lambda b, j, t: (b, 0, 0, 0))],
        out_shape=[jax.ShapeDtypeStruct((nbatch * seq, D_MODEL), BF16),
                   jax.ShapeDtypeStruct((nbatch, npair, LANES, LANES), F32)],
        scratch_shapes=[pltpu.VMEM((npair, LANES, LANES), F32)],
        compiler_params=_cparams(3),
        name="wkv_chunk",
    )(r, k, v, a, ld, g, p["k_k"], p["k_a"], p["r_k"], p["ln_g"], p["ln_b"])


WKV_LANES_UNROLL = 4


def _wkv_lanes_kernel(r_ref, k_ref, v_ref, a_ref, ld_ref, g_ref, kkg_ref, kag_ref, rk_ref, lng_ref, lnb_ref,
                      s_ref, z_ref, sout_ref, prep_ref, y_ref, *, seq_len):
    n = R_HEAD_DIM
    nseq = s_ref.shape[3]
    eye = _iota((LANES, LANES), 0) == _iota((LANES, LANES), 1)

    def column(ref):
        return jnp.sum(jnp.where(eye, ref[...], 0.0), axis=1, keepdims=True)

    kkg, kag, rk, lng, lnb = (column(ref) for ref in (kkg_ref, kag_ref, rk_ref, lng_ref, lnb_ref))

    def token(ref, t):
        return ref[pl.ds(t, nseq, stride=seq_len), :].T

    bonus = []
    for t in range(seq_len):
        r, k, v, a = (token(ref, t) for ref in (r_ref, k_ref, v_ref, a_ref))
        w = jnp.exp(token(ld_ref, t))
        kkx = k * kkg
        kh = k * (1.0 + (a - 1.0) * kag)
        rkk = r * kh * rk
        tiles = []
        for hh in range(2):
            rows = slice(hh * n, (hh + 1) * n)
            nrm = jnp.sqrt(jnp.sum(kkx[rows] * kkx[rows], axis=0, keepdims=True))
            kk = kkx[rows] / jnp.maximum(nrm, 1e-12)
            for q, val in enumerate((w[rows], -kk, kk * a[rows], kh[rows], r[rows], v[rows])):
                prep_ref[q, t, hh] = val
            tiles.append(jnp.sum(rkk[rows], axis=0, keepdims=True) * v[rows])
        bonus.append(tiles)

    for hh in range(2):
        def advance(i, carry, hh=hh):
            for u in range(WKV_LANES_UNROLL):
                vi = i * WKV_LANES_UNROLL + u
                slab = s_ref[hh, vi]
                for t in range(seq_len):
                    w, av, bv, kh, r = (prep_ref[q, t, hh] for q in range(5))
                    vrow = prep_ref[5, t, hh, pl.ds(vi, 1), :]
                    sa = jnp.sum(slab * av, axis=0, keepdims=True)
                    slab = slab * w + sa * bv + vrow * kh
                    y_ref[t, hh, pl.ds(vi, 1), :] = jnp.sum(slab * r, axis=0, keepdims=True)
                sout_ref[hh, vi] = slab
            return carry

        lax.fori_loop(0, n // WKV_LANES_UNROLL, advance, 0)

    for t in range(seq_len):
        parts = []
        for hh in range(2):
            rows = slice(hh * n, (hh + 1) * n)
            y = y_ref[t, hh]
            d = y - jnp.mean(y, axis=0, keepdims=True)
            var = jnp.mean(d * d, axis=0, keepdims=True)
            parts.append(d * lax.rsqrt(var + GN_EPS) * lng[rows] + lnb[rows] + bonus[t][hh])
        g = token(g_ref, t)
        z = jnp.concatenate(parts, axis=0) * (g * _sigmoid(g))
        z_ref[pl.ds(t, nseq, stride=seq_len), :] = z.T


def _wkv_lanes_call(r, k, v, a, ld, g, p, state_hvkb, seq_len):
    n = r.shape[0]
    nseq = state_hvkb.shape[3]
    tile = pl.BlockSpec((n, LANES), lambda j: (0, j))
    par = pl.BlockSpec((1, LANES), lambda j: (0, j))
    sspec = pl.BlockSpec((2, R_HEAD_DIM, R_HEAD_DIM, nseq), lambda j: (j, 0, 0, 0))
    return pl.pallas_call(
        functools.partial(_wkv_lanes_kernel, seq_len=seq_len),
        grid=(D_MODEL // LANES,),
        in_specs=[tile] * 6 + [par] * 5 + [sspec],
        out_specs=[tile, sspec],
        out_shape=[jax.ShapeDtypeStruct((n, D_MODEL), F32),
                   jax.ShapeDtypeStruct(state_hvkb.shape, F32)],
        scratch_shapes=[pltpu.VMEM((6, seq_len, 2, R_HEAD_DIM, nseq), F32),
                        pltpu.VMEM((seq_len, 2, R_HEAD_DIM, nseq), F32)],
        compiler_params=_cparams(1),
        name="wkv_lanes",
    )(r, k, v, a, ld, g, p["k_k"], p["k_a"], p["r_k"], p["ln_g"], p["ln_b"], state_hvkb)


RWKV_OUT_PIECES = 8


def _rwkv_out_kernel(*refs):
    n = RWKV_OUT_PIECES
    z_refs, h_refs, (wout_ref, fg_ref, out_ref) = refs[:n], refs[n:2 * n], refs[2 * n:]
    z = jnp.concatenate([ref[...].astype(BF16) for ref in z_refs], axis=0)
    h = jnp.concatenate([ref[...] for ref in h_refs], axis=0)
    h2 = h + jnp.dot(z, wout_ref[...], preferred_element_type=F32)
    out_ref[...] = _rmsnorm(h2, fg_ref[...])


def _rwkv_out_call(z, h, p, nbatch, ntile, tm, pieces_per_batch, skip):
    piece = tm // RWKV_OUT_PIECES
    dst = lambda b, i: (b * ntile + i, 0)
    pspec = [pl.BlockSpec((piece, D_MODEL),
                          functools.partial(lambda b, i, kk: (b * pieces_per_batch + i * RWKV_OUT_PIECES + skip + kk, 0),
                                            kk=kk))
             for kk in range(RWKV_OUT_PIECES)]
    return pl.pallas_call(
        _rwkv_out_kernel,
        grid=(nbatch, ntile),
        in_specs=pspec + pspec + [pl.BlockSpec((D_MODEL, D_MODEL), lambda b, i: (0, 0)),
                                  pl.BlockSpec((1, D_MODEL), lambda b, i: (0, 0))],
        out_specs=pl.BlockSpec((tm, D_MODEL), dst),
        out_shape=jax.ShapeDtypeStruct((nbatch * ntile * tm, D_MODEL), F32),
        compiler_params=_cparams(2),
        name="rwkv_out",
    )(*([z] * RWKV_OUT_PIECES), *([h] * RWKV_OUT_PIECES), p["w_out"], p["final_gain"])


def _prompt_bucket():
    assert WINDOW == BLOCK
    rel = (np.arange(BLOCK)[:, None] - np.arange(BLOCK)[None, :]) % BLOCK
    return _t5_bucket_np(rel)


def _sample_bucket(keep, t_new, slot):
    t = (np.arange(SUBLANES) % t_new)[:, None]
    j = np.arange(SAMPLE_KEYS)[None, :]
    own = j - keep - slot * t_new
    rel = np.where(j < keep, keep + t - j, t - own)
    ok = (rel >= 0) & (rel < WINDOW) & ((j < keep) | ((own >= 0) & (own < t_new)))
    return np.where(ok, _t5_bucket_np(rel), -1).astype(np.int32)


def kernel(x_prompt, x_sample, cache_win_k, cache_win_v, state_wkv, state_shift, meta_tokens, rel_bias_table, norm_gain, final_gain, attn_w_in, attn_sinks, attn_w_out, rwkv_mu, rwkv_w_in, rwkv_w0, rwkv_w1, rwkv_w2, rwkv_a0, rwkv_a1, rwkv_a2, rwkv_k_k, rwkv_k_a, rwkv_r_k, rwkv_ln_gamma, rwkv_ln_beta, rwkv_w_out):
    nb, seq, _ = x_prompt.shape
    ns, t_new, _ = x_sample.shape
    keep = cache_win_k.shape[2]
    lp = seq + BLOCK
    nblk = lp // BLOCK
    row = lambda x: x.reshape(1, D_MODEL)

    w_in0 = attn_w_in[0].astype(BF16)
    w_out0 = attn_w_out[0].astype(BF16)
    gain0 = row(norm_gain[0])
    sinks = attn_sinks[0]
    rp = dict(gain=row(norm_gain[1]), mu=rwkv_mu[0], w_in=rwkv_w_in[0].astype(BF16),
              w0=row(rwkv_w0[0]), w1=rwkv_w1[0].astype(BF16), w2=rwkv_w2[0].astype(BF16),
              a0=row(rwkv_a0[0]), a1=rwkv_a1[0].astype(BF16), a2=rwkv_a2[0].astype(BF16),
              k_k=row(rwkv_k_k[0]), k_a=row(rwkv_k_a[0]), r_k=row(rwkv_r_k[0]), ln_g=row(rwkv_ln_gamma[0]),
              ln_b=row(rwkv_ln_beta[0]), w_out=rwkv_w_out[0].astype(BF16),
              final_gain=row(final_gain))

    bias_p, *bias_s = _bias_call(rel_bias_table, [_prompt_bucket()]
                                 + [_sample_bucket(keep, t_new, slot) for slot in range(2)])

    head = jnp.concatenate([jnp.zeros((PAD, D_MODEL), F32), meta_tokens.astype(F32)], axis=0)
    xp = x_prompt.reshape(nb * seq, D_MODEL)
    q, kv, g = _attn_proj_call(xp, head, gain0, w_in0, BF16, nb, lp // ATTN_PROJ_ROWS,
                               ATTN_PROJ_ROWS // BLOCK, BLOCK)
    h1 = _attn_prompt_call(sinks, q, kv, g, head, xp, bias_p, w_out0, nb, nblk)
    kv3 = kv.reshape(nb, lp, 2 * A_KV_WIDTH)[:, lp - WINDOW:, :]
    win_k_p = kv3[:, :, :A_KV_WIDTH].reshape(1, nb, WINDOW, A_KV_HEADS, A_HEAD_DIM)
    win_v_p = kv3[:, :, A_KV_WIDTH:].reshape(1, nb, WINDOW, A_KV_HEADS, A_HEAD_DIM)

    shift0 = jnp.zeros((nb, 1, D_MODEL), F32)
    r, k, v, g1, ld, a, xlast = _rwkv_proj_call(h1, shift0, rp, nb, lp // RWKV_PROJ_ROWS, RWKV_PROJ_ROWS, None)
    z, st = _wkv_chunk_call(r, k, v, a, ld, g1, rp, nb, lp)
    y_prompt = _rwkv_out_call(z, h1, rp, nb, seq // RWKV_OUT_ROWS, RWKV_OUT_ROWS,
                              lp * RWKV_OUT_PIECES // RWKV_OUT_ROWS, BLOCK * RWKV_OUT_PIECES // RWKV_OUT_ROWS)
    y_prompt = y_prompt.reshape(nb, seq, D_MODEL)
    st = st.reshape(nb, D_MODEL // LANES, 2, R_HEAD_DIM, 2, R_HEAD_DIM)
    st = jnp.stack([st[:, :, 0, :, 0, :], st[:, :, 1, :, 1, :]], axis=2)
    wkv_p = jnp.swapaxes(st, -1, -2).reshape(1, nb, R_HEADS, R_HEAD_DIM, R_HEAD_DIM)
    shift_p = xlast.reshape(1, nb, D_MODEL)

    xs = x_sample.reshape(ns * t_new, D_MODEL)
    qs, kvs, gs = _attn_proj_call(xs, None, gain0, w_in0, F32, 1, 1, 1, ns * t_new)
    ck = jnp.swapaxes(cache_win_k[0].reshape(ns, keep, A_KV_WIDTH), 1, 2)
    cv = jnp.swapaxes(cache_win_v[0].reshape(ns, keep, A_KV_WIDTH), 1, 2)
    h1s, nk, nv = _attn_sample_call(sinks, qs, kvs, gs, xs, ck, cv, bias_s[0], bias_s[1], w_out0, t_new)
    win_k_s = jnp.swapaxes(nk, 1, 2).reshape(1, ns, keep, A_KV_HEADS, A_HEAD_DIM)
    win_v_s = jnp.swapaxes(nv, 1, 2).reshape(1, ns, keep, A_KV_HEADS, A_HEAD_DIM)

    shift_rows = jnp.repeat(state_shift[0], t_new, axis=0)
    tms = 256
    rs, ks, vs, g1s, lds, as_, xns = _rwkv_proj_call(h1s, shift_rows, rp, 1, ns * t_new // tms, tms, t_new)
    zs, st_s = _wkv_lanes_call(rs, ks, vs, as_, lds, g1s, rp, jnp.transpose(state_wkv[0], (1, 2, 3, 0)), t_new)
    y_sample = _rwkv_out_call(zs, h1s, rp, 1, 1, ns * t_new, RWKV_OUT_PIECES, 0)
    y_sample = y_sample.reshape(ns, t_new, D_MODEL)
    wkv_s = jnp.transpose(st_s, (3, 0, 1, 2))[None]
    shift_s = xns.reshape(ns, t_new, D_MODEL)[:, t_new - 1][None]

    return (y_prompt, y_sample, win_k_p, win_v_p, wkv_p, shift_p, win_k_s, win_v_s, wkv_s, shift_s)
```

```python
import functools
import math

import numpy as np
import jax
import jax.numpy as jnp
from jax import lax
from jax.experimental import pallas as pl
from jax.experimental.pallas import tpu as pltpu

F32 = jnp.float32
BF16 = jnp.bfloat16

D_MODEL = 1024
N_META = 16
RMS_EPS = 1e-6
A_HEADS = 16
A_KV_HEADS = 4
A_HEAD_DIM = 64
A_WIDTH = A_HEADS * A_HEAD_DIM
A_KV_WIDTH = A_KV_HEADS * A_HEAD_DIM
WINDOW = 128
BLOCK = 128
N_BUCKETS = 32
MAX_DISTANCE = 128
R_HEAD_DIM = 64
R_HEADS = D_MODEL // R_HEAD_DIM
GN_EPS = 64e-5

LANES = 128
SUBLANES = 8
PAD = BLOCK - N_META
NEG = -1e30
WKV_CHUNK = 64
WKV_CHUNKS_PER_STEP = 3
ATTN_BLOCKS_PER_STEP = 3
ATTN_PROJ_ROWS = 1408
RWKV_PROJ_ROWS = 528
RWKV_OUT_ROWS = 1024
VMEM_LIMIT = 56 * 1024 * 1024


def _cparams(n_axes):
    return pltpu.CompilerParams(dimension_semantics=("arbitrary",) * n_axes,
                                vmem_limit_bytes=VMEM_LIMIT)


def _rmsnorm(x, gain):
    return x * lax.rsqrt(jnp.mean(x * x, axis=-1, keepdims=True) + RMS_EPS) * gain


def _sigmoid(x):
    return 1.0 / (1.0 + jnp.exp(-x))


def _iota(shape, dim):
    return lax.broadcasted_iota(jnp.int32, shape, dim)


def _t5_bucket_np(rel):
    n = np.maximum(rel, 0)
    max_exact = N_BUCKETS // 2
    nf = np.maximum(n, max_exact).astype(np.float32)
    scale = np.float32(math.log(MAX_DISTANCE / max_exact))
    large = max_exact + (np.log(nf / np.float32(max_exact)) / scale
                         * np.float32(N_BUCKETS - max_exact)).astype(np.int32)
    large = np.minimum(large, N_BUCKETS - 1)
    return np.where(n < max_exact, n, large).astype(np.int32)


def _bias_kernel(table_ref, *refs):
    h = pl.program_id(0)
    n = len(refs) // 2
    for bucket_ref, out_ref in zip(refs[:n], refs[n:]):
        bk = bucket_ref[...]
        acc = jnp.full(bk.shape, NEG, F32)
        for b in range(N_BUCKETS):
            acc = jnp.where(bk == b, table_ref[b, h], acc)
        out_ref[0] = acc


def _bias_call(table, buckets_np):
    return pl.pallas_call(
        _bias_kernel,
        grid=(A_HEADS,),
        in_specs=[pl.BlockSpec(memory_space=pltpu.SMEM)]
                 + [pl.BlockSpec(bk.shape, lambda h: (0, 0)) for bk in buckets_np],
        out_specs=[pl.BlockSpec((1,) + bk.shape, lambda h: (h, 0, 0)) for bk in buckets_np],
        out_shape=[jax.ShapeDtypeStruct((A_HEADS,) + bk.shape, F32) for bk in buckets_np],
        compiler_params=_cparams(1),
        name="bias_expand",
    )(table, *(jnp.asarray(bk) for bk in buckets_np))


def _attn_proj_kernel(head_ref, *refs, n_piece):
    x_refs, (gain_ref, w_ref, q_ref, kv_ref, g_ref) = refs[:n_piece], refs[n_piece:]
    first = x_refs[0][...]
    if head_ref is not None:
        first = jnp.where(pl.program_id(1) == 0, head_ref[...], first)
    x = jnp.concatenate([first] + [ref[...] for ref in x_refs[1:]], axis=0)
    xn = _rmsnorm(x, gain_ref[...])
    proj = jnp.dot(xn.astype(BF16), w_ref[...], preferred_element_type=F32)
    q_ref[...] = (proj[:, :A_WIDTH] * (A_HEAD_DIM ** -0.5)).astype(q_ref.dtype)
    kv_ref[...] = proj[:, A_WIDTH:A_WIDTH + 2 * A_KV_WIDTH]
    g_ref[...] = proj[:, A_WIDTH + 2 * A_KV_WIDTH:]


def _attn_proj_call(x2d, head, gain, w_bf16, q_dtype, nbatch, ntile, n_piece, piece):
    tm = n_piece * piece
    wcols = w_bf16.shape[1]
    per_seq = x2d.shape[0] // (nbatch * piece)
    lead = 0 if head is None else 1
    dst = lambda b, i: (b * ntile + i, 0)
    xspec = [pl.BlockSpec((piece, D_MODEL),
                          functools.partial(lambda b, i, kk: (b * per_seq + jnp.maximum(i * n_piece + kk - lead, 0), 0),
                                            kk=kk))
             for kk in range(n_piece)]
    kern = functools.partial(_attn_proj_kernel, n_piece=n_piece)
    operands = [x2d] * n_piece + [gain, w_bf16]
    if head is None:
        kern = functools.partial(kern, None)
        head_spec = []
    else:
        head_spec = [pl.BlockSpec((piece, D_MODEL), lambda b, i: (0, 0))]
        operands = [head] + operands
    n = nbatch * ntile * tm
    return pl.pallas_call(
        kern,
        grid=(nbatch, ntile),
        in_specs=head_spec + xspec + [pl.BlockSpec((1, D_MODEL), lambda b, i: (0, 0)),
                                      pl.BlockSpec((D_MODEL, wcols), lambda b, i: (0, 0))],
        out_specs=[pl.BlockSpec((tm, A_WIDTH), dst),
                   pl.BlockSpec((tm, 2 * A_KV_WIDTH), dst),
                   pl.BlockSpec((tm, A_WIDTH), dst)],
        out_shape=[jax.ShapeDtypeStruct((n, A_WIDTH), q_dtype),
                   jax.ShapeDtypeStruct((n, 2 * A_KV_WIDTH), F32),
                   jax.ShapeDtypeStruct((n, A_WIDTH), F32)],
        compiler_params=_cparams(2),
        name="attn_proj",
    )(*operands)


def _padded_kv_tiles(kv, c):
    lo = _iota((1, LANES), 1) < A_HEAD_DIM
    j = c // 2
    out = []
    for base in (0, A_KV_WIDTH):
        t = kv[:, base + j * LANES: base + (j + 1) * LANES]
        tr = pltpu.roll(t, A_HEAD_DIM, 1)
        if c % 2 == 0:
            even, odd = jnp.where(lo, t, 0.0), jnp.where(lo, 0.0, tr)
        else:
            even, odd = jnp.where(lo, tr, 0.0), jnp.where(lo, 0.0, t)
        out += [even.astype(BF16), odd.astype(BF16)]
    return out


def _mm_nt(a, b):
    return lax.dot_general(a, b, (((1,), (1,)), ((), ())), preferred_element_type=F32)


def _attn_prompt_kernel(sinks_ref, q_ref, kvc_ref, kvp_ref, g_ref, head_ref, *refs):
    nb = ATTN_BLOCKS_PER_STEP
    x_refs, (bias_ref, wout_ref, out_ref, og_ref) = refs[:nb], refs[nb:]
    i = pl.program_id(1)
    stack = 2 * BLOCK
    row, col = _iota((stack, BLOCK), 0) & (BLOCK - 1), _iota((stack, BLOCK), 1)
    upper = _iota((stack, 1), 0) >= BLOCK
    own = col <= row
    chains = [(c, idx) for c in range(A_KV_HEADS) for idx in range(2)]
    n = range(len(chains))
    cur = [_padded_kv_tiles(kvp_ref[...], c) for c in range(A_KV_HEADS)]
    for j in range(nb):
        rows = slice(j * BLOCK, (j + 1) * BLOCK)
        kvalid = (i * nb + j - 1 + own.astype(jnp.int32)) * BLOCK + col >= PAD
        prev, cur = cur, [_padded_kv_tiles(kvc_ref[rows, :], c) for c in range(A_KV_HEADS)]
        s, sink = [], []
        for c, idx in chains:
            q2 = q_ref[rows, 2 * c * LANES:(2 * c + 2) * LANES]
            q2 = jnp.concatenate([q2[:, :LANES], q2[:, LANES:]], axis=0)
            sc = jnp.where(own, _mm_nt(q2, cur[c][idx]), _mm_nt(q2, prev[c][idx]))
            bias = jnp.concatenate([bias_ref[4 * c + idx], bias_ref[4 * c + 2 + idx]], axis=0)
            s.append(jnp.where(kvalid, sc + bias, NEG))
            sink.append(jnp.where(upper, sinks_ref[4 * c + 2 + idx], sinks_ref[4 * c + idx]))
        m = [jnp.maximum(jnp.max(s[t], axis=1, keepdims=True), sink[t]) for t in n]
        p = [jnp.exp(s[t] - m[t]) for t in n]
        den = [jnp.sum(p[t], axis=1, keepdims=True) + jnp.exp(sink[t] - m[t]) for t in n]
        o = []
        for t, (c, idx) in enumerate(chains):
            pv = (jnp.dot(jnp.where(own, p[t], 0.0).astype(BF16), cur[c][2 + idx], preferred_element_type=F32)
                  + jnp.dot(jnp.where(own, 0.0, p[t]).astype(BF16), prev[c][2 + idx], preferred_element_type=F32))
            o.append(pv * (1.0 / den[t]))
        for c in range(A_KV_HEADS):
            both = o[2 * c] + o[2 * c + 1]
            for half in range(2):
                sl = slice((2 * c + half) * LANES, (2 * c + half + 1) * LANES)
                gt = g_ref[rows, sl]
                og_ref[rows, sl] = (both[half * BLOCK:(half + 1) * BLOCK] * (gt * _sigmoid(gt))).astype(BF16)
    resid = jnp.concatenate([jnp.where(i == 0, head_ref[...], x_refs[0][...])]
                            + [ref[...] for ref in x_refs[1:]], axis=0)
    out_ref[...] = resid + jnp.dot(og_ref[...], wout_ref[...], preferred_element_type=F32)


def _attn_prompt_call(sinks, q, kv, g, head, x2d, bias, wout_bf16, nbatch, nblk):
    n = q.shape[0]
    nb = ATTN_BLOCKS_PER_STEP
    nstep = nblk // nb
    rows = nb * BLOCK
    row = lambda b, i: (b * nstep + i, 0)
    prev = lambda b, i: (b * nblk + jnp.maximum(i * nb - 1, 0), 0)
    xrow = [functools.partial(lambda b, i, j: (b * (nblk - 1) + jnp.maximum(i * nb + j - 1, 0), 0), j=j)
            for j in range(nb)]
    return pl.pallas_call(
        _attn_prompt_kernel,
        grid=(nbatch, nstep),
        in_specs=[pl.BlockSpec(memory_space=pltpu.SMEM),
                  pl.BlockSpec((rows, A_WIDTH), row),
                  pl.BlockSpec((rows, 2 * A_KV_WIDTH), row),
                  pl.BlockSpec((BLOCK, 2 * A_KV_WIDTH), prev),
                  pl.BlockSpec((rows, A_WIDTH), row),
                  pl.BlockSpec((BLOCK, D_MODEL), lambda b, i: (0, 0))]
                 + [pl.BlockSpec((BLOCK, D_MODEL), xrow[j]) for j in range(nb)]
                 + [pl.BlockSpec((A_HEADS, BLOCK, BLOCK), lambda b, i: (0, 0, 0)),
                    pl.BlockSpec((A_WIDTH, D_MODEL), lambda b, i: (0, 0))],
        out_specs=pl.BlockSpec((rows, D_MODEL), row),
        out_shape=jax.ShapeDtypeStruct((n, D_MODEL), F32),
        scratch_shapes=[pltpu.VMEM((rows, A_WIDTH), BF16)],
        compiler_params=_cparams(2),
        name="attn_prompt",
    )(sinks, q, kv, kv, g, head, *([x2d] * nb), bias, wout_bf16)


SAMPLE_SB = 8
SAMPLE_KEYS = 2 * BLOCK


def _attn_sample_kernel(sinks_ref, q_ref, kvn_ref, g_ref, h_ref, ck_ref, cv_ref, bias0_ref, bias1_ref,
                        wout_ref, out_ref, nk_ref, nv_ref, og_ref, *, t_new):
    keep = ck_ref.shape[2]
    lo = _iota((1, LANES), 1) < A_HEAD_DIM
    stack = 8 * SUBLANES
    own = (_iota((stack, 1), 0) & (SUBLANES - 1)) // t_new
    piece = _iota((stack, 1), 0) // SUBLANES
    bias_refs = (bias0_ref, bias1_ref)
    n_tile = A_KV_WIDTH // LANES
    pair_ids = range(SAMPLE_SB * t_new // SUBLANES)

    bias_c, bias_n, sink = [], [], []
    for j in range(n_tile):
        heads = slice(8 * j, 8 * j + 8)
        bias_c.append(bias0_ref[heads, :, :keep].reshape(stack, keep))
        bias_n.append([ref[heads, :, keep:keep + SUBLANES].reshape(stack, SUBLANES) for ref in bias_refs])
        col = jnp.zeros((stack, 1), F32)
        for gq in range(8):
            col = jnp.where(piece == gq, sinks_ref[8 * j + gq], col)
        sink.append(col)

    chains = [(p, j, s) for p in pair_ids for j in range(n_tile) for s in range(SUBLANES // t_new)]
    qs = {}
    for p in pair_ids:
        rows = slice(p * SUBLANES, (p + 1) * SUBLANES)
        for j in range(n_tile):
            parts = []
            for gq in range(8):
                t = q_ref[rows, (4 * j + gq // 2) * LANES:(4 * j + gq // 2 + 1) * LANES]
                want_lo = gq < 4
                if (gq % 2 == 0) != want_lo:
                    t = pltpu.roll(t, A_HEAD_DIM, 1)
                parts.append(jnp.where(lo, t, 0.0) if want_lo else jnp.where(lo, 0.0, t))
            qs[p, j] = jnp.concatenate(parts, axis=0).astype(BF16)

    def kv_tiles(p, j, s, base):
        seq = p * (SUBLANES // t_new) + s
        cache = (ck_ref if base == 0 else cv_ref)[seq][j * LANES:(j + 1) * LANES, :].astype(BF16)
        new = kvn_ref[p * SUBLANES:(p + 1) * SUBLANES, base + j * LANES:base + (j + 1) * LANES].astype(BF16)
        return cache, new

    sc, sn = [], []
    for p, j, s in chains:
        kc, kn = kv_tiles(p, j, s, 0)
        sc.append(jnp.dot(qs[p, j], kc, preferred_element_type=F32) + bias_c[j])
        sn.append(_mm_nt(qs[p, j], kn) + bias_n[j][s])
    mx = [jnp.maximum(jnp.maximum(jnp.max(sc[i], axis=1, keepdims=True),
                                  jnp.max(sn[i], axis=1, keepdims=True)), sink[chains[i][1]])
          for i in range(len(chains))]
    pc = [jnp.exp(sc[i] - mx[i]) for i in range(len(chains))]
    pn = [jnp.exp(sn[i] - mx[i]) for i in range(len(chains))]
    den = [jnp.sum(pc[i], axis=1, keepdims=True) + jnp.sum(pn[i], axis=1, keepdims=True)
           + jnp.exp(sink[chains[i][1]] - mx[i]) for i in range(len(chains))]
    outs = {}
    for i, (p, j, s) in enumerate(chains):
        vc, vn = kv_tiles(p, j, s, A_KV_WIDTH)
        o = (_mm_nt(pc[i].astype(BF16), vc)
             + jnp.dot(pn[i].astype(BF16), vn, preferred_element_type=F32)) * (1.0 / den[i])
        outs[p, j] = o if s == 0 else jnp.where(own == s, o, outs[p, j])
    for p in pair_ids:
        rows = slice(p * SUBLANES, (p + 1) * SUBLANES)
        for j in range(n_tile):
            o = outs[p, j]
            for gg in range(4):
                even = o[2 * gg * SUBLANES:(2 * gg + 1) * SUBLANES]
                odd = o[(2 * gg + 1) * SUBLANES:(2 * gg + 2) * SUBLANES]
                if gg < 2:
                    tile = jnp.where(lo, even, pltpu.roll(odd, A_HEAD_DIM, 1))
                else:
                    tile = jnp.where(lo, pltpu.roll(even, A_HEAD_DIM, 1), odd)
                sl = slice((4 * j + gg) * LANES, (4 * j + gg + 1) * LANES)
                gt = g_ref[rows, sl]
                og_ref[rows, sl] = tile * (gt * _sigmoid(gt))
    lane_pos = _iota((SUBLANES, keep), 1)
    old = _iota((1, keep), 1) < keep - t_new
    for p in pair_ids:
        new8 = kvn_ref[p * SUBLANES:(p + 1) * SUBLANES, :]
        for s in range(SUBLANES // t_new):
            seq = p * (SUBLANES // t_new) + s
            sel = (lane_pos == _iota((SUBLANES, keep), 0) + (keep - t_new - s * t_new)).astype(F32)
            placed = _mm3(new8, sel, TN)
            for ref_in, ref_out, base in ((ck_ref, nk_ref, 0), (cv_ref, nv_ref, A_KV_WIDTH)):
                shifted = pltpu.roll(ref_in[seq], keep - t_new, 1)
                ref_out[seq] = jnp.where(old, shifted, placed[base:base + A_KV_WIDTH])
    out_ref[...] = h_ref[...] + jnp.dot(og_ref[...].astype(BF16), wout_ref[...],
                                        preferred_element_type=F32)


def _attn_sample_call(sinks, q, kv, g, h, cache_k, cache_v, bias0, bias1, wout_bf16, t_new):
    nseq, keep = cache_k.shape[0], cache_k.shape[2]
    rows = SAMPLE_SB * t_new
    row = lambda i: (i, 0)
    cspec = pl.BlockSpec((SAMPLE_SB, A_KV_WIDTH, keep), lambda i: (i, 0, 0))
    bspec = pl.BlockSpec((A_HEADS, SUBLANES, SAMPLE_KEYS), lambda i: (0, 0, 0))
    return pl.pallas_call(
        functools.partial(_attn_sample_kernel, t_new=t_new),
        grid=(nseq // SAMPLE_SB,),
        in_specs=[pl.BlockSpec(memory_space=pltpu.SMEM),
                  pl.BlockSpec((rows, A_WIDTH), row),
                  pl.BlockSpec((rows, 2 * A_KV_WIDTH), row),
                  pl.BlockSpec((rows, A_WIDTH), row),
                  pl.BlockSpec((rows, D_MODEL), row),
                  cspec, cspec, bspec, bspec,
                  pl.BlockSpec((A_WIDTH, D_MODEL), lambda i: (0, 0))],
        out_specs=[pl.BlockSpec((rows, D_MODEL), row), cspec, cspec],
        out_shape=[jax.ShapeDtypeStruct((nseq * t_new, D_MODEL), F32),
                   jax.ShapeDtypeStruct(cache_k.shape, F32),
                   jax.ShapeDtypeStruct(cache_v.shape, F32)],
        scratch_shapes=[pltpu.VMEM((rows, A_WIDTH), F32)],
        compiler_params=_cparams(1),
        name="attn_sample",
    )(sinks, q, kv, g, h, cache_k, cache_v, bias0, bias1, wout_bf16)


def _rwkv_proj_kernel(h_ref, shift_ref, gain_ref, mu_ref, win_ref, w0_ref, w1_ref, w2_ref,
                      a0_ref, a1_ref, a2_ref,
                      r_ref, k_ref, v_ref, g_ref, ld_ref, a_ref, xn_ref, *scratch, seq_len):
    xn = _rmsnorm(h_ref[...], gain_ref[...])
    tm = xn.shape[0]
    rolled = pltpu.roll(xn, 1, 0)
    row = _iota((tm, 1), 0)
    if seq_len is None:
        carry_ref, = scratch

        @pl.when(pl.program_id(1) == 0)
        def _():
            carry_ref[...] = shift_ref[0]

        xprev = jnp.where(row == 0, carry_ref[...], rolled)
        carry_ref[...] = xn[tm - 1:tm, :]
        xn_ref[0] = xn[tm - 1:tm, :]
    else:
        xprev = jnp.where(row % seq_len == 0, shift_ref[...], rolled)
        xn_ref[...] = xn
    dx = xprev - xn

    def mix(c):
        return (xn + dx * mu_ref[c:c + 1, :]).astype(BF16)

    for c, o_ref in enumerate((r_ref, k_ref, v_ref, g_ref)):
        o_ref[...] = jnp.dot(mix(c), win_ref[c], preferred_element_type=F32)
    lw = jnp.tanh(jnp.dot(mix(4), w1_ref[...], preferred_element_type=F32))
    z = w0_ref[...] + jnp.dot(lw.astype(BF16), w2_ref[...], preferred_element_type=F32)
    ld_ref[...] = -math.exp(-0.5) * _sigmoid(z)
    la = jnp.dot(mix(5), a1_ref[...], preferred_element_type=F32)
    a_ref[...] = _sigmoid(a0_ref[...] + jnp.dot(la.astype(BF16), a2_ref[...],
                                                preferred_element_type=F32))


def _rwkv_proj_call(h, shift, p, nbatch, ntile, tm, seq_len):
    n = h.shape[0]
    row = lambda b, i: (b * ntile + i, 0)
    full2 = lambda b, i: (0, 0)
    if seq_len is None:
        shift_spec = pl.BlockSpec((1, 1, D_MODEL), lambda b, i: (b, 0, 0))
        xn_spec = pl.BlockSpec((1, 1, D_MODEL), lambda b, i: (b, 0, 0))
        xn_shape = jax.ShapeDtypeStruct((nbatch, 1, D_MODEL), F32)
        scratch = [pltpu.VMEM((1, D_MODEL), F32)]
    else:
        shift_spec = pl.BlockSpec((tm, D_MODEL), row)
        xn_spec = pl.BlockSpec((tm, D_MODEL), row)
        xn_shape = jax.ShapeDtypeStruct((n, D_MODEL), F32)
        scratch = []
    lora = p["w1"].shape[1]
    big = jax.ShapeDtypeStruct((n, D_MODEL), F32)
    return pl.pallas_call(
        functools.partial(_rwkv_proj_kernel, seq_len=seq_len),
        grid=(nbatch, ntile),
        in_specs=[pl.BlockSpec((tm, D_MODEL), row),
                  shift_spec,
                  pl.BlockSpec((1, D_MODEL), full2),
                  pl.BlockSpec(p["mu"].shape, full2),
                  pl.BlockSpec(p["w_in"].shape, lambda b, i: (0, 0, 0)),
                  pl.BlockSpec((1, D_MODEL), full2),
                  pl.BlockSpec((D_MODEL, lora), full2),
                  pl.BlockSpec((lora, D_MODEL), full2),
                  pl.BlockSpec((1, D_MODEL), full2),
                  pl.BlockSpec((D_MODEL, lora), full2),
                  pl.BlockSpec((lora, D_MODEL), full2)],
        out_specs=[pl.BlockSpec((tm, D_MODEL), row)] * 6 + [xn_spec],
        out_shape=[big] * 6 + [xn_shape],
        scratch_shapes=scratch,
        compiler_params=_cparams(2),
        name="rwkv_proj",
    )(h, shift, p["gain"], p["mu"], p["w_in"], p["w0"], p["w1"], p["w2"], p["a0"], p["a1"], p["a2"])


NN = (((1,), (0,)), ((), ()))
NT = (((1,), (1,)), ((), ()))
TN = (((0,), (0,)), ((), ()))


def _split(x):
    hi = x.astype(BF16)
    return hi, (x - hi.astype(F32)).astype(BF16)


def _mm(a, b, dn):
    return lax.dot_general(a, b, dn, preferred_element_type=F32)


def _mm3s(a, b, dn):
    (ah, al), (bh, bl) = a, b
    if dn == TN:
        both = _mm(jnp.concatenate([ah, al], axis=1), bh, dn)
        m = ah.shape[1]
    else:
        both = _mm(jnp.concatenate([ah, al], axis=0), bh, dn)
        m = ah.shape[0]
    return both[:m] + both[m:] + _mm(ah, bl, dn)


def _mm3(a, b, dn):
    return _mm3s(_split(a), _split(b), dn)


def _mm3_top(x_hi, x_lo_top, b, dn):
    bh, bl = b
    c = x_lo_top.shape[0]
    both = _mm(jnp.concatenate([x_hi, x_lo_top], axis=0), bh, dn)
    return both[:c] + both[2 * c:] + _mm(x_hi[:c], bl, dn), both[c:2 * c]


def _seg_sums(xs, ones_bf16, exact):
    rows = xs[0].shape[0]
    stacked = jnp.concatenate(xs, axis=0)
    if exact:
        hi, lo = _split(stacked)
        out = _mm(lo, ones_bf16, NN) + _mm(hi, ones_bf16, NN)
    else:
        out = _mm(stacked.astype(BF16), ones_bf16, NN)
    return [out[i * rows:(i + 1) * rows] for i in range(len(xs))]


def _lane_lo():
    return _iota((1, LANES), 1) < R_HEAD_DIM


def _same_half():
    return (_iota((LANES, LANES), 0) < R_HEAD_DIM) == (_iota((LANES, LANES), 1) < R_HEAD_DIM)


def _ones_blk():
    return _same_half().astype(BF16)


def _bd(z):
    lo = _lane_lo()
    return jnp.concatenate([jnp.where(lo, z, 0.0), jnp.where(lo, 0.0, z)], axis=0)


def _bd_swap(z):
    lo = _lane_lo()
    return jnp.concatenate([jnp.where(lo, 0.0, z), jnp.where(lo, z, 0.0)], axis=0)


def _split_map(z, f):
    hi, lw = _split(z)
    return f(hi), f(lw)


def _wkv_batch_stage(r, k, v, a, ld, kkg, kag, fillers=(), early=None):
    c = r[0].shape[0]
    pairs = range(len(r))
    half = R_HEAD_DIM
    lo = _lane_lo()
    t_row, t_col = _iota((c, LANES), 0), _iota((c, LANES), 1) & (half - 1)
    strict, incl = t_row > t_col, t_row >= t_col
    tri = (_iota((c, c), 0) >= _iota((c, c), 1)).astype(BF16)
    ones_blk = _ones_blk()
    eye_pair = (_iota((c, LANES), 0) == (_iota((c, LANES), 1) & (half - 1))).astype(F32)

    kkx = [k[j] * kkg[j] for j in pairs]
    ssq = _seg_sums([kkx[j] * kkx[j] for j in pairs], ones_blk, exact=True)
    ld_hi, ld_lo = _split(jnp.concatenate(ld, axis=1))
    cs_all = _mm(tri, ld_lo, NN) + _mm(tri, ld_hi, NN)
    cs = [cs_all[:, j * LANES:(j + 1) * LANES] for j in pairs]
    tot = [cs[j][c - 1:c, :] for j in pairs]
    x_hi, at_lo, kh_all, bke, vbd_s, lk_a, lk_r, lb_a, lb_r = ([] for _ in range(9))
    for j in pairs:
        kk = kkx[j] / jnp.maximum(jnp.sqrt(ssq[j]), 1e-12)
        kh = k[j] * (1.0 + (a[j] - 1.0) * kag[j])
        bv = kk * a[j]
        e_neg = jnp.exp(-cs[j])
        e_end = jnp.exp(tot[j] - cs[j])
        at, rt = -kk * jnp.exp(cs[j] - ld[j]), r[j] * jnp.exp(cs[j])
        at_hi, at_l = _split(at)
        x_hi.append(jnp.concatenate([at_hi, rt.astype(BF16)], axis=0))
        at_lo.append(at_l)
        kh_all.append(kh)
        bt, kt = _split(bv * e_neg), _split(kh * e_neg)
        y_hi, y_lo = (jnp.concatenate([p, q], axis=0) for p, q in zip(bt, kt))
        bke.append(jnp.concatenate([bv * e_end, kh * e_end], axis=0))
        vbd_s.append(_split_map(v[j], _bd))
        xa, xb = jnp.where(lo, x_hi[j], 0.0), jnp.where(lo, 0.0, x_hi[j])
        both = _mm(jnp.concatenate([xa, xb, jnp.where(lo, at_l, 0.0), jnp.where(lo, 0.0, at_l)], axis=0),
                   y_hi, NT)
        corr = _mm(jnp.concatenate([xa[:c], xb[:c]], axis=0), y_lo, NT)
        ga_a = pltpu.roll(both[:c] + both[4 * c:5 * c] + corr[:c], half, 1)
        ga_r = pltpu.roll(both[c:2 * c], half, 1)
        gb_a = both[2 * c:3 * c] + both[5 * c:] + corr[c:]
        gb_r = both[3 * c:4 * c]
        lk_a.append(jnp.where(strict, jnp.where(lo, ga_a, gb_a), 0.0))
        lk_r.append(jnp.where(incl, jnp.where(lo, ga_r, gb_r), 0.0))
        lb_a.append(jnp.where(strict, jnp.where(lo, gb_a, ga_a), 0.0))
        lb_r.append(jnp.where(incl, jnp.where(lo, gb_r, ga_r), 0.0))
    fillers = list(fillers)
    if early is not None:
        early.update(x_hi=x_hi, at_lo=at_lo)
    pw = lb_a
    acc = [eye_pair + pw[j] for j in pairs]
    pw = [_mm(pw[j].astype(BF16), _bd(pw[j].astype(BF16)), NN) for j in pairs]
    from_v = []
    for j in pairs:
        lk_hi, lk_lo = _split(lk_a[j])
        from_v.append(_mm3_top(jnp.concatenate([lk_hi, lk_r[j].astype(BF16)], axis=0), lk_lo, vbd_s[j], NN))
    for _ in range(int(math.log2(c)) - 2):
        both = [_mm(jnp.concatenate([pw[j], acc[j]], axis=0).astype(BF16), _bd(pw[j].astype(BF16)), NN)
                for j in pairs]
        if fillers:
            fillers.pop(0)()
        pw = [both[j][:c] for j in pairs]
        acc = [acc[j] + both[j][c:] for j in pairs]
    tinv = [acc[j] + _mm(acc[j].astype(BF16), _bd(pw[j].astype(BF16)), NN) for j in pairs]
    for fill in fillers:
        fill()
    return dict(x_hi=x_hi, at_lo=at_lo, kh=kh_all, lb_r=lb_r, tinv=tinv, from_v=from_v, bke=bke,
                tot=tot, ones_blk=ones_blk)


def _wkv_finish(stage, from_state_a, from_state_r):
    pairs = range(len(from_state_a))
    u = [_mm3s(_split(stage["tinv"][j]),
               _split_map(from_state_a[j] + stage["from_v"][j][0], _bd_swap), NN) for j in pairs]
    y = [from_state_r[j] + stage["from_v"][j][1]
         + _mm(stage["lb_r"][j].astype(BF16), _bd_swap(u[j].astype(BF16)), NN) for j in pairs]
    return u, y


def _wkv_gate(y, r, kh, v, g, rk, lng, lnb, ones_blk):
    n = len(y)
    pairs = range(n)
    inv_n = 1.0 / R_HEAD_DIM
    sums = _seg_sums([r[j] * kh[j] * rk[j] for j in pairs] + list(y), ones_blk, exact=False)
    rkk, mean = sums[:n], [s * inv_n for s in sums[n:]]
    d = [y[j] - mean[j] for j in pairs]
    var = [s * inv_n for s in _seg_sums([d[j] * d[j] for j in pairs], ones_blk, exact=False)]
    return [((d[j] * lax.rsqrt(var[j] + GN_EPS) * lng[j] + lnb[j] + rkk[j] * v[j])
             * (g[j] * _sigmoid(g[j]))).astype(BF16) for j in pairs]


def _wkv_chunk_kernel(r_ref, k_ref, v_ref, a_ref, ld_ref, g_ref, kkg_ref, kag_ref, rk_ref, lng_ref, lnb_ref,
                      z_ref, sout_ref, st_ref):
    c = WKV_CHUNK
    n_sub = r_ref.shape[0] // c
    npair = r_ref.shape[1] // LANES
    pairs = range(npair)
    items = [(s, j) for s in range(n_sub) for j in pairs]

    @pl.when(pl.program_id(2) == 0)
    def _():
        st_ref[...] = jnp.zeros_like(st_ref)

    def tile(ref, it):
        s, j = it
        return ref[s * c:(s + 1) * c, j * LANES:(j + 1) * LANES]

    def par(ref):
        return [ref[:, j * LANES:(j + 1) * LANES] for _, j in items]

    r, k, v, a, ld, g = ([tile(ref, it) for it in items] for ref in (r_ref, k_ref, v_ref, a_ref, ld_ref, g_ref))
    st = [st_ref[j] for j in pairs]
    from_state = {}
    stage = {}

    def state_products(group):
        def run():
            for j in group:
                from_state[j] = _mm3_top(stage["x_hi"][j], stage["at_lo"][j], _split(st[j]), NN)
        return run

    n_fill = 4
    groups = [list(pairs)[i::n_fill] for i in range(n_fill)]
    stage.update(_wkv_batch_stage(r, k, v, a, ld, par(kkg_ref), par(kag_ref),
                                  fillers=[state_products(grp) for grp in groups if grp], early=stage))
    same_half = _same_half()
    eye_full = _iota((LANES, LANES), 0) == _iota((LANES, LANES), 1)
    ys = []
    for s in range(n_sub):
        idx = [s * npair + j for j in pairs]
        if s > 0:
            for j in pairs:
                from_state[j] = _mm3_top(stage["x_hi"][idx[j]], stage["at_lo"][idx[j]], _split(st[j]), NN)
        sub = {key: [stage[key][i] for i in idx] for key in ("tinv", "from_v", "lb_r")}
        us, ys_s = _wkv_finish(sub, [from_state[j][0] for j in pairs], [from_state[j][1] for j in pairs])
        upd = [_mm3(stage["bke"][idx[j]], jnp.concatenate([us[j], v[idx[j]]], axis=0), TN) for j in pairs]
        ys += ys_s
        for j in pairs:
            w_col = jnp.sum(jnp.where(eye_full, jnp.exp(stage["tot"][idx[j]]), 0.0), axis=1, keepdims=True)
            st[j] = w_col * st[j] + jnp.where(same_half, upd[j], 0.0)
    zs = _wkv_gate(ys, r, stage["kh"], v, g, par(rk_ref), par(lng_ref), par(lnb_ref), stage["ones_blk"])
    for i, (s, j) in enumerate(items):
        z_ref[s * c:(s + 1) * c, j * LANES:(j + 1) * LANES] = zs[i]
    for j in pairs:
        st_ref[j] = st[j]
        sout_ref[0, j] = st[j]


def _wkv_chunk_call(r, k, v, a, ld, g, p, nbatch, seq):
    rows = WKV_CHUNK * WKV_CHUNKS_PER_STEP
    nstep = seq // rows
    npair = D_MODEL // LANES
    tile = pl.BlockSpec((rows, D_MODEL), lambda b, j, t: (b * nstep + t, 0))
    par = pl.BlockSpec((1, D_MODEL), lambda b, j, t: (0, 0))
    return pl.pallas_call(
        _wkv_chunk_kernel,
        grid=(nbatch, 1, nstep),
        in_specs=[tile] * 6 + [par] * 5,
        out_specs=[tile, pl.BlockSpec((1, npair, LANES, LANES), lambda b, j, t: (b, 0, 0, 0))],
        out_shape=[jax.ShapeDtypeStruct((nbatch * seq, D_MODEL), BF16),
                   jax.ShapeDtypeStruct((nbatch, npair, LANES, LANES), F32)],
        scratch_shapes=[pltpu.VMEM((npair, LANES, LANES), F32)],
        compiler_params=_cparams(3),
        name="wkv_chunk",
    )(r, k, v, a, ld, g, p["k_k"], p["k_a"], p["r_k"], p["ln_g"], p["ln_b"])


WKV_LANES_UNROLL = 4


def _wkv_lanes_kernel(r_ref, k_ref, v_ref, a_ref, ld_ref, g_ref, kkg_ref, kag_ref, rk_ref, lng_ref, lnb_ref,
                      s_ref, z_ref, sout_ref, prep_ref, y_ref, *, seq_len):
    n = R_HEAD_DIM
    nseq = s_ref.shape[3]
    eye = _iota((LANES, LANES), 0) == _iota((LANES, LANES), 1)

    def column(ref):
        return jnp.sum(jnp.where(eye, ref[...], 0.0), axis=1, keepdims=True)

    kkg, kag, rk, lng, lnb = (column(ref) for ref in (kkg_ref, kag_ref, rk_ref, lng_ref, lnb_ref))

    def token(ref, t):
        return ref[pl.ds(t, nseq, stride=seq_len), :].T

    bonus = []
    for t in range(seq_len):
        r, k, v, a = (token(ref, t) for ref in (r_ref, k_ref, v_ref, a_ref))
        w = jnp.exp(token(ld_ref, t))
        kkx = k * kkg
        kh = k * (1.0 + (a - 1.0) * kag)
        rkk = r * kh * rk
        tiles = []
        for hh in range(2):
            rows = slice(hh * n, (hh + 1) * n)
            nrm = jnp.sqrt(jnp.sum(kkx[rows] * kkx[rows], axis=0, keepdims=True))
            kk = kkx[rows] / jnp.maximum(nrm, 1e-12)
            for q, val in enumerate((w[rows], -kk, kk * a[rows], kh[rows], r[rows], v[rows])):
                prep_ref[q, t, hh] = val
            tiles.append(jnp.sum(rkk[rows], axis=0, keepdims=True) * v[rows])
        bonus.append(tiles)

    for hh in range(2):
        def advance(i, carry, hh=hh):
            for u in range(WKV_LANES_UNROLL):
                vi = i * WKV_LANES_UNROLL + u
                slab = s_ref[hh, vi]
                for t in range(seq_len):
                    w, av, bv, kh, r = (prep_ref[q, t, hh] for q in range(5))
                    vrow = prep_ref[5, t, hh, pl.ds(vi, 1), :]
                    sa = jnp.sum(slab * av, axis=0, keepdims=True)
                    slab = slab * w + sa * bv + vrow * kh
                    y_ref[t, hh, pl.ds(vi, 1), :] = jnp.sum(slab * r, axis=0, keepdims=True)
                sout_ref[hh, vi] = slab
            return carry

        lax.fori_loop(0, n // WKV_LANES_UNROLL, advance, 0)

    for t in range(seq_len):
        parts = []
        for hh in range(2):
            rows = slice(hh * n, (hh + 1) * n)
            y = y_ref[t, hh]
            d = y - jnp.mean(y, axis=0, keepdims=True)
            var = jnp.mean(d * d, axis=0, keepdims=True)
            parts.append(d * lax.rsqrt(var + GN_EPS) * lng[rows] + lnb[rows] + bonus[t][hh])
        g = token(g_ref, t)
        z = jnp.concatenate(parts, axis=0) * (g * _sigmoid(g))
        z_ref[pl.ds(t, nseq, stride=seq_len), :] = z.T


def _wkv_lanes_call(r, k, v, a, ld, g, p, state_hvkb, seq_len):
    n = r.shape[0]
    nseq = state_hvkb.shape[3]
    tile = pl.BlockSpec((n, LANES), lambda j: (0, j))
    par = pl.BlockSpec((1, LANES), lambda j: (0, j))
    sspec = pl.BlockSpec((2, R_HEAD_DIM, R_HEAD_DIM, nseq), lambda j: (j, 0, 0, 0))
    return pl.pallas_call(
        functools.partial(_wkv_lanes_kernel, seq_len=seq_len),
        grid=(D_MODEL // LANES,),
        in_specs=[tile] * 6 + [par] * 5 + [sspec],
        out_specs=[tile, sspec],
        out_shape=[jax.ShapeDtypeStruct((n, D_MODEL), F32),
                   jax.ShapeDtypeStruct(state_hvkb.shape, F32)],
        scratch_shapes=[pltpu.VMEM((6, seq_len, 2, R_HEAD_DIM, nseq), F32),
                        pltpu.VMEM((seq_len, 2, R_HEAD_DIM, nseq), F32)],
        compiler_params=_cparams(1),
        name="wkv_lanes",
    )(r, k, v, a, ld, g, p["k_k"], p["k_a"], p["r_k"], p["ln_g"], p["ln_b"], state_hvkb)


RWKV_OUT_PIECES = 8


def _rwkv_out_kernel(*refs):
    n = RWKV_OUT_PIECES
    z_refs, h_refs, (wout_ref, fg_ref, out_ref) = refs[:n], refs[n:2 * n], refs[2 * n:]
    z = jnp.concatenate([ref[...].astype(BF16) for ref in z_refs], axis=0)
    h = jnp.concatenate([ref[...] for ref in h_refs], axis=0)
    h2 = h + jnp.dot(z, wout_ref[...], preferred_element_type=F32)
    out_ref[...] = _rmsnorm(h2, fg_ref[...])


def _rwkv_out_call(z, h, p, nbatch, ntile, tm, pieces_per_batch, skip):
    piece = tm // RWKV_OUT_PIECES
    dst = lambda b, i: (b * ntile + i, 0)
    pspec = [pl.BlockSpec((piece, D_MODEL),
                          functools.partial(lambda b, i, kk: (b * pieces_per_batch + i * RWKV_OUT_PIECES + skip + kk, 0),
                                            kk=kk))
             for kk in range(RWKV_OUT_PIECES)]
    return pl.pallas_call(
        _rwkv_out_kernel,
        grid=(nbatch, ntile),
        in_specs=pspec + pspec + [pl.BlockSpec((D_MODEL, D_MODEL), lambda b, i: (0, 0)),
                                  pl.BlockSpec((1, D_MODEL), lambda b, i: (0, 0))],
        out_specs=pl.BlockSpec((tm, D_MODEL), dst),
        out_shape=jax.ShapeDtypeStruct((nbatch * ntile * tm, D_MODEL), F32),
        compiler_params=_cparams(2),
        name="rwkv_out",
    )(*([z] * RWKV_OUT_PIECES), *([h] * RWKV_OUT_PIECES), p["w_out"], p["final_gain"])


def _prompt_bucket():
    assert WINDOW == BLOCK
    rel = (np.arange(BLOCK)[:, None] - np.arange(BLOCK)[None, :]) % BLOCK
    return _t5_bucket_np(rel)


def _sample_bucket(keep, t_new, slot):
    t = (np.arange(SUBLANES) % t_new)[:, None]
    j = np.arange(SAMPLE_KEYS)[None, :]
    own = j - keep - slot * t_new
    rel = np.where(j < keep, keep + t - j, t - own)
    ok = (rel >= 0) & (rel < WINDOW) & ((j < keep) | ((own >= 0) & (own < t_new)))
    return np.where(ok, _t5_bucket_np(rel), -1).astype(np.int32)


def kernel(x_prompt, x_sample, cache_win_k, cache_win_v, state_wkv, state_shift, meta_tokens, rel_bias_table, norm_gain, final_gain, attn_w_in, attn_sinks, attn_w_out, rwkv_mu, rwkv_w_in, rwkv_w0, rwkv_w1, rwkv_w2, rwkv_a0, rwkv_a1, rwkv_a2, rwkv_k_k, rwkv_k_a, rwkv_r_k, rwkv_ln_gamma, rwkv_ln_beta, rwkv_w_out):
    nb, seq, _ = x_prompt.shape
    ns, t_new, _ = x_sample.shape
    keep = cache_win_k.shape[2]
    lp = seq + BLOCK
    nblk = lp // BLOCK
    row = lambda x: x.reshape(1, D_MODEL)

    w_in0 = attn_w_in[0].astype(BF16)
    w_out0 = attn_w_out[0].astype(BF16)
    gain0 = row(norm_gain[0])
    sinks = attn_sinks[0]
    rp = dict(gain=row(norm_gain[1]), mu=rwkv_mu[0], w_in=rwkv_w_in[0].astype(BF16),
              w0=row(rwkv_w0[0]), w1=rwkv_w1[0].astype(BF16), w2=rwkv_w2[0].astype(BF16),
              a0=row(rwkv_a0[0]), a1=rwkv_a1[0].astype(BF16), a2=rwkv_a2[0].astype(BF16),
              k_k=row(rwkv_k_k[0]), k_a=row(rwkv_k_a[0]), r_k=row(rwkv_r_k[0]), ln_g=row(rwkv_ln_gamma[0]),
              ln_b=row(rwkv_ln_beta[0]), w_out=rwkv_w_out[0].astype(BF16),
              final_gain=row(final_gain))

    bias_p, *bias_s = _bias_call(rel_bias_table, [_prompt_bucket()]
                                 + [_sample_bucket(keep, t_new, slot) for slot in range(2)])

    head = jnp.concatenate([jnp.zeros((PAD, D_MODEL), F32), meta_tokens.astype(F32)], axis=0)
    xp = x_prompt.reshape(nb * seq, D_MODEL)
    q, kv, g = _attn_proj_call(xp, head, gain0, w_in0, BF16, nb, lp // ATTN_PROJ_ROWS,
                               ATTN_PROJ_ROWS // BLOCK, BLOCK)
    h1 = _attn_prompt_call(sinks, q, kv, g, head, xp, bias_p, w_out0, nb, nblk)
    kv3 = kv.reshape(nb, lp, 2 * A_KV_WIDTH)[:, lp - WINDOW:, :]
    win_k_p = kv3[:, :, :A_KV_WIDTH].reshape(1, nb, WINDOW, A_KV_HEADS, A_HEAD_DIM)
    win_v_p = kv3[:, :, A_KV_WIDTH:].reshape(1, nb, WINDOW, A_KV_HEADS, A_HEAD_DIM)

    shift0 = jnp.zeros((nb, 1, D_MODEL), F32)
    r, k, v, g1, ld, a, xlast = _rwkv_proj_call(h1, shift0, rp, nb, lp // RWKV_PROJ_ROWS, RWKV_PROJ_ROWS, None)
    z, st = _wkv_chunk_call(r, k, v, a, ld, g1, rp, nb, lp)
    y_prompt = _rwkv_out_call(z, h1, rp, nb, seq // RWKV_OUT_ROWS, RWKV_OUT_ROWS,
                              lp * RWKV_OUT_PIECES // RWKV_OUT_ROWS, BLOCK * RWKV_OUT_PIECES // RWKV_OUT_ROWS)
    y_prompt = y_prompt.reshape(nb, seq, D_MODEL)
    st = st.reshape(nb, D_MODEL // LANES, 2, R_HEAD_DIM, 2, R_HEAD_DIM)
    st = jnp.stack([st[:, :, 0, :, 0, :], st[:, :, 1, :, 1, :]], axis=2)
    wkv_p = jnp.swapaxes(st, -1, -2).reshape(1, nb, R_HEADS, R_HEAD_DIM, R_HEAD_DIM)
    shift_p = xlast.reshape(1, nb, D_MODEL)

    xs = x_sample.reshape(ns * t_new, D_MODEL)
    qs, kvs, gs = _attn_proj_call(xs, None, gain0, w_in0, F32, 1, 1, 1, ns * t_new)
    ck = jnp.swapaxes(cache_win_k[0].reshape(ns, keep, A_KV_WIDTH), 1, 2)
    cv = jnp.swapaxes(cache_win_v[0].reshape(ns, keep, A_KV_WIDTH), 1, 2)
    h1s, nk, nv = _attn_sample_call(sinks, qs, kvs, gs, xs, ck, cv, bias_s[0], bias_s[1], w_out0, t_new)
    win_k_s = jnp.swapaxes(nk, 1, 2).reshape(1, ns, keep, A_KV_HEADS, A_HEAD_DIM)
    win_v_s = jnp.swapaxes(nv, 1, 2).reshape(1, ns, keep, A_KV_HEADS, A_HEAD_DIM)

    shift_rows = jnp.repeat(state_shift[0], t_new, axis=0)
    tms = 256
    rs, ks, vs, g1s, lds, as_, xns = _rwkv_proj_call(h1s, shift_rows, rp, 1, ns * t_new // tms, tms, t_new)
    zs, st_s = _wkv_lanes_call(rs, ks, vs, as_, lds, g1s, rp, jnp.transpose(state_wkv[0], (1, 2, 3, 0)), t_new)
    y_sample = _rwkv_out_call(zs, h1s, rp, 1, 1, ns * t_new, RWKV_OUT_PIECES, 0)
    y_sample = y_sample.reshape(ns, t_new, D_MODEL)
    wkv_s = jnp.transpose(st_s, (3, 0, 1, 2))[None]
    shift_s = xns.reshape(ns, t_new, D_MODEL)[:, t_new - 1][None]

    return (y_prompt, y_sample, win_k_p, win_v_p, wkv_p, shift_p, win_k_s, win_v_s, wkv_s, shift_s)
```

```python
import functools
import math

import numpy as np
import jax
import jax.numpy as jnp
from jax import lax
from jax.experimental import pallas as pl
from jax.experimental.pallas import tpu as pltpu

F32 = jnp.float32
BF16 = jnp.bfloat16

D_MODEL = 1024
N_META = 16
RMS_EPS = 1e-6
A_HEADS = 16
A_KV_HEADS = 4
A_HEAD_DIM = 64
A_WIDTH = A_HEADS * A_HEAD_DIM
A_KV_WIDTH = A_KV_HEADS * A_HEAD_DIM
WINDOW = 128
BLOCK = 128
N_BUCKETS = 32
MAX_DISTANCE = 128
R_HEAD_DIM = 64
R_HEADS = D_MODEL // R_HEAD_DIM
GN_EPS = 64e-5

LANES = 128
SUBLANES = 8
PAD = BLOCK - N_META
NEG = -1e30
WKV_CHUNK = 64
WKV_CHUNKS_PER_STEP = 3
ATTN_BLOCKS_PER_STEP = 3
ATTN_PROJ_ROWS = 384
RWKV_PROJ_ROWS = 528
RWKV_OUT_ROWS = 1024
VMEM_LIMIT = 56 * 1024 * 1024


def _cparams(n_axes):
    return pltpu.CompilerParams(dimension_semantics=("arbitrary",) * n_axes,
                                vmem_limit_bytes=VMEM_LIMIT)


def _rmsnorm(x, gain):
    return x * lax.rsqrt(jnp.mean(x * x, axis=-1, keepdims=True) + RMS_EPS) * gain


def _sigmoid(x):
    return 1.0 / (1.0 + jnp.exp(-x))


def _iota(shape, dim):
    return lax.broadcasted_iota(jnp.int32, shape, dim)


def _t5_bucket_np(rel):
    n = np.maximum(rel, 0)
    max_exact = N_BUCKETS // 2
    nf = np.maximum(n, max_exact).astype(np.float32)
    scale = np.float32(math.log(MAX_DISTANCE / max_exact))
    large = max_exact + (np.log(nf / np.float32(max_exact)) / scale
                         * np.float32(N_BUCKETS - max_exact)).astype(np.int32)
    large = np.minimum(large, N_BUCKETS - 1)
    return np.where(n < max_exact, n, large).astype(np.int32)


def _bias_kernel(table_ref, *refs):
    h = pl.program_id(0)
    n = len(refs) // 2
    for bucket_ref, out_ref in zip(refs[:n], refs[n:]):
        bk = bucket_ref[...]
        acc = jnp.full(bk.shape, NEG, F32)
        for b in range(N_BUCKETS):
            acc = jnp.where(bk == b, table_ref[b, h], acc)
        out_ref[0] = acc


def _bias_call(table, buckets_np):
    return pl.pallas_call(
        _bias_kernel,
        grid=(A_HEADS,),
        in_specs=[pl.BlockSpec(memory_space=pltpu.SMEM)]
                 + [pl.BlockSpec(bk.shape, lambda h: (0, 0)) for bk in buckets_np],
        out_specs=[pl.BlockSpec((1,) + bk.shape, lambda h: (h, 0, 0)) for bk in buckets_np],
        out_shape=[jax.ShapeDtypeStruct((A_HEADS,) + bk.shape, F32) for bk in buckets_np],
        compiler_params=_cparams(1),
        name="bias_expand",
    )(table, *(jnp.asarray(bk) for bk in buckets_np))


def _attn_proj_kernel(head_ref, *refs, n_piece):
    x_refs, (gain_ref, w_ref, q_ref, kv_ref, g_ref) = refs[:n_piece], refs[n_piece:]
    first = x_refs[0][...]
    if head_ref is not None:
        first = jnp.where(pl.program_id(1) == 0, head_ref[...], first)
    x = jnp.concatenate([first] + [ref[...] for ref in x_refs[1:]], axis=0)
    xn = _rmsnorm(x, gain_ref[...])
    proj = jnp.dot(xn.astype(BF16), w_ref[...], preferred_element_type=F32)
    q_ref[...] = (proj[:, :A_WIDTH] * (A_HEAD_DIM ** -0.5)).astype(q_ref.dtype)
    kv_ref[...] = proj[:, A_WIDTH:A_WIDTH + 2 * A_KV_WIDTH]
    g_ref[...] = proj[:, A_WIDTH + 2 * A_KV_WIDTH:]


def _attn_proj_call(x2d, head, gain, w_bf16, q_dtype, nbatch, ntile, n_piece, piece):
    tm = n_piece * piece
    wcols = w_bf16.shape[1]
    per_seq = x2d.shape[0] // (nbatch * piece)
    lead = 0 if head is None else 1
    dst = lambda b, i: (b * ntile + i, 0)
    xspec = [pl.BlockSpec((piece, D_MODEL),
                          functools.partial(lambda b, i, kk: (b * per_seq + jnp.maximum(i * n_piece + kk - lead, 0), 0),
                                            kk=kk))
             for kk in range(n_piece)]
    kern = functools.partial(_attn_proj_kernel, n_piece=n_piece)
    operands = [x2d] * n_piece + [gain, w_bf16]
    if head is None:
        kern = functools.partial(kern, None)
        head_spec = []
    else:
        head_spec = [pl.BlockSpec((piece, D_MODEL), lambda b, i: (0, 0))]
        operands = [head] + operands
    n = nbatch * ntile * tm
    return pl.pallas_call(
        kern,
        grid=(nbatch, ntile),
        in_specs=head_spec + xspec + [pl.BlockSpec((1, D_MODEL), lambda b, i: (0, 0)),
                                      pl.BlockSpec((D_MODEL, wcols), lambda b, i: (0, 0))],
        out_specs=[pl.BlockSpec((tm, A_WIDTH), dst),
                   pl.BlockSpec((tm, 2 * A_KV_WIDTH), dst),
                   pl.BlockSpec((tm, A_WIDTH), dst)],
        out_shape=[jax.ShapeDtypeStruct((n, A_WIDTH), q_dtype),
                   jax.ShapeDtypeStruct((n, 2 * A_KV_WIDTH), F32),
                   jax.ShapeDtypeStruct((n, A_WIDTH), F32)],
        compiler_params=_cparams(2),
        name="attn_proj",
    )(*operands)


def _padded_kv_tiles(kv, c):
    lo = _iota((1, LANES), 1) < A_HEAD_DIM
    j = c // 2
    out = []
    for base in (0, A_KV_WIDTH):
        t = kv[:, base + j * LANES: base + (j + 1) * LANES]
        tr = pltpu.roll(t, A_HEAD_DIM, 1)
        if c % 2 == 0:
            even, odd = jnp.where(lo, t, 0.0), jnp.where(lo, 0.0, tr)
        else:
            even, odd = jnp.where(lo, tr, 0.0), jnp.where(lo, 0.0, t)
        out += [even.astype(BF16), odd.astype(BF16)]
    return out


def _mm_nt(a, b):
    return lax.dot_general(a, b, (((1,), (1,)), ((), ())), preferred_element_type=F32)


def _attn_prompt_kernel(sinks_ref, q_ref, kvc_ref, kvp_ref, g_ref, head_ref, *refs):
    nb = ATTN_BLOCKS_PER_STEP
    x_refs, (bias_ref, wout_ref, out_ref, og_ref) = refs[:nb], refs[nb:]
    i = pl.program_id(1)
    stack = 2 * BLOCK
    row, col = _iota((stack, BLOCK), 0) & (BLOCK - 1), _iota((stack, BLOCK), 1)
    upper = _iota((stack, 1), 0) >= BLOCK
    own = col <= row
    chains = [(c, idx) for c in range(A_KV_HEADS) for idx in range(2)]
    n = range(len(chains))
    cur = [_padded_kv_tiles(kvp_ref[...], c) for c in range(A_KV_HEADS)]
    for j in range(nb):
        rows = slice(j * BLOCK, (j + 1) * BLOCK)
        kvalid = (i * nb + j - 1 + own.astype(jnp.int32)) * BLOCK + col >= PAD
        prev, cur = cur, [_padded_kv_tiles(kvc_ref[rows, :], c) for c in range(A_KV_HEADS)]
        s, sink = [], []
        for c, idx in chains:
            q2 = q_ref[rows, 2 * c * LANES:(2 * c + 2) * LANES]
            q2 = jnp.concatenate([q2[:, :LANES], q2[:, LANES:]], axis=0)
            sc = jnp.where(own, _mm_nt(q2, cur[c][idx]), _mm_nt(q2, prev[c][idx]))
            bias = jnp.concatenate([bias_ref[4 * c + idx], bias_ref[4 * c + 2 + idx]], axis=0)
            s.append(jnp.where(kvalid, sc + bias, NEG))
            sink.append(jnp.where(upper, sinks_ref[4 * c + 2 + idx], sinks_ref[4 * c + idx]))
        m = [jnp.maximum(jnp.max(s[t], axis=1, keepdims=True), sink[t]) for t in n]
        p = [jnp.exp(s[t] - m[t]) for t in n]
        den = [jnp.sum(p[t], axis=1, keepdims=True) + jnp.exp(sink[t] - m[t]) for t in n]
        o = []
        for t, (c, idx) in enumerate(chains):
            pv = (jnp.dot(jnp.where(own, p[t], 0.0).astype(BF16), cur[c][2 + idx], preferred_element_type=F32)
                  + jnp.dot(jnp.where(own, 0.0, p[t]).astype(BF16), prev[c][2 + idx], preferred_element_type=F32))
            o.append(pv * (1.0 / den[t]))
        for c in range(A_KV_HEADS):
            both = o[2 * c] + o[2 * c + 1]
            for half in range(2):
                sl = slice((2 * c + half) * LANES, (2 * c + half + 1) * LANES)
                gt = g_ref[rows, sl]
                og_ref[rows, sl] = (both[half * BLOCK:(half + 1) * BLOCK] * (gt * _sigmoid(gt))).astype(BF16)
    resid = jnp.concatenate([jnp.where(i == 0, head_ref[...], x_refs[0][...])]
                            + [ref[...] for ref in x_refs[1:]], axis=0)
    out_ref[...] = resid + jnp.dot(og_ref[...], wout_ref[...], preferred_element_type=F32)


def _attn_prompt_call(sinks, q, kv, g, head, x2d, bias, wout_bf16, nbatch, nblk):
    n = q.shape[0]
    nb = ATTN_BLOCKS_PER_STEP
    nstep = nblk // nb
    rows = nb * BLOCK
    row = lambda b, i: (b * nstep + i, 0)
    prev = lambda b, i: (b * nblk + jnp.maximum(i * nb - 1, 0), 0)
    xrow = [functools.partial(lambda b, i, j: (b * (nblk - 1) + jnp.maximum(i * nb + j - 1, 0), 0), j=j)
            for j in range(nb)]
    return pl.pallas_call(
        _attn_prompt_kernel,
        grid=(nbatch, nstep),
        in_specs=[pl.BlockSpec(memory_space=pltpu.SMEM),
                  pl.BlockSpec((rows, A_WIDTH), row),
                  pl.BlockSpec((rows, 2 * A_KV_WIDTH), row),
                  pl.BlockSpec((BLOCK, 2 * A_KV_WIDTH), prev),
                  pl.BlockSpec((rows, A_WIDTH), row),
                  pl.BlockSpec((BLOCK, D_MODEL), lambda b, i: (0, 0))]
                 + [pl.BlockSpec((BLOCK, D_MODEL), xrow[j]) for j in range(nb)]
                 + [pl.BlockSpec((A_HEADS, BLOCK, BLOCK), lambda b, i: (0, 0, 0)),
                    pl.BlockSpec((A_WIDTH, D_MODEL), lambda b, i: (0, 0))],
        out_specs=pl.BlockSpec((rows, D_MODEL), row),
        out_shape=jax.ShapeDtypeStruct((n, D_MODEL), F32),
        scratch_shapes=[pltpu.VMEM((rows, A_WIDTH), BF16)],
        compiler_params=_cparams(2),
        name="attn_prompt",
    )(sinks, q, kv, kv, g, head, *([x2d] * nb), bias, wout_bf16)


SAMPLE_SB = 8
SAMPLE_KEYS = 2 * BLOCK


def _attn_sample_kernel(sinks_ref, q_ref, kvn_ref, g_ref, h_ref, ck_ref, cv_ref, bias0_ref, bias1_ref,
                        wout_ref, out_ref, nk_ref, nv_ref, og_ref, *, t_new):
    keep = ck_ref.shape[2]
    lo = _iota((1, LANES), 1) < A_HEAD_DIM
    stack = 8 * SUBLANES
    own = (_iota((stack, 1), 0) & (SUBLANES - 1)) // t_new
    piece = _iota((stack, 1), 0) // SUBLANES
    bias_refs = (bias0_ref, bias1_ref)
    n_tile = A_KV_WIDTH // LANES
    pair_ids = range(SAMPLE_SB * t_new // SUBLANES)

    bias_c, bias_n, sink = [], [], []
    for j in range(n_tile):
        heads = slice(8 * j, 8 * j + 8)
        bias_c.append(bias0_ref[heads, :, :keep].reshape(stack, keep))
        bias_n.append([ref[heads, :, keep:keep + SUBLANES].reshape(stack, SUBLANES) for ref in bias_refs])
        col = jnp.zeros((stack, 1), F32)
        for gq in range(8):
            col = jnp.where(piece == gq, sinks_ref[8 * j + gq], col)
        sink.append(col)

    chains = [(p, j, s) for p in pair_ids for j in range(n_tile) for s in range(SUBLANES // t_new)]
    qs = {}
    for p in pair_ids:
        rows = slice(p * SUBLANES, (p + 1) * SUBLANES)
        for j in range(n_tile):
            parts = []
            for gq in range(8):
                t = q_ref[rows, (4 * j + gq // 2) * LANES:(4 * j + gq // 2 + 1) * LANES]
                want_lo = gq < 4
                if (gq % 2 == 0) != want_lo:
                    t = pltpu.roll(t, A_HEAD_DIM, 1)
                parts.append(jnp.where(lo, t, 0.0) if want_lo else jnp.where(lo, 0.0, t))
            qs[p, j] = jnp.concatenate(parts, axis=0).astype(BF16)

    def kv_tiles(p, j, s, base):
        seq = p * (SUBLANES // t_new) + s
        cache = (ck_ref if base == 0 else cv_ref)[seq][j * LANES:(j + 1) * LANES, :].astype(BF16)
        new = kvn_ref[p * SUBLANES:(p + 1) * SUBLANES, base + j * LANES:base + (j + 1) * LANES].astype(BF16)
        return cache, new

    sc, sn = [], []
    for p, j, s in chains:
        kc, kn = kv_tiles(p, j, s, 0)
        sc.append(jnp.dot(qs[p, j], kc, preferred_element_type=F32) + bias_c[j])
        sn.append(_mm_nt(qs[p, j], kn) + bias_n[j][s])
    mx = [jnp.maximum(jnp.maximum(jnp.max(sc[i], axis=1, keepdims=True),
                                  jnp.max(sn[i], axis=1, keepdims=True)), sink[chains[i][1]])
          for i in range(len(chains))]
    pc = [jnp.exp(sc[i] - mx[i]) for i in range(len(chains))]
    pn = [jnp.exp(sn[i] - mx[i]) for i in range(len(chains))]
    den = [jnp.sum(pc[i], axis=1, keepdims=True) + jnp.sum(pn[i], axis=1, keepdims=True)
           + jnp.exp(sink[chains[i][1]] - mx[i]) for i in range(len(chains))]
    outs = {}
    for i, (p, j, s) in enumerate(chains):
        vc, vn = kv_tiles(p, j, s, A_KV_WIDTH)
        o = (_mm_nt(pc[i].astype(BF16), vc)
             + jnp.dot(pn[i].astype(BF16), vn, preferred_element_type=F32)) * (1.0 / den[i])
        outs[p, j] = o if s == 0 else jnp.where(own == s, o, outs[p, j])
    for p in pair_ids:
        rows = slice(p * SUBLANES, (p + 1) * SUBLANES)
        for j in range(n_tile):
            o = outs[p, j]
            for gg in range(4):
                even = o[2 * gg * SUBLANES:(2 * gg + 1) * SUBLANES]
                odd = o[(2 * gg + 1) * SUBLANES:(2 * gg + 2) * SUBLANES]
                if gg < 2:
                    tile = jnp.where(lo, even, pltpu.roll(odd, A_HEAD_DIM, 1))
                else:
                    tile = jnp.where(lo, pltpu.roll(even, A_HEAD_DIM, 1), odd)
                sl = slice((4 * j + gg) * LANES, (4 * j + gg + 1) * LANES)
                gt = g_ref[rows, sl]
                og_ref[rows, sl] = tile * (gt * _sigmoid(gt))
    lane_pos = _iota((SUBLANES, keep), 1)
    old = _iota((1, keep), 1) < keep - t_new
    for p in pair_ids:
        new8 = kvn_ref[p * SUBLANES:(p + 1) * SUBLANES, :]
        for s in range(SUBLANES // t_new):
            seq = p * (SUBLANES // t_new) + s
            sel = (lane_pos == _iota((SUBLANES, keep), 0) + (keep - t_new - s * t_new)).astype(BF16)
            placed = _mm2(_split(new8), sel, TN)
            for ref_in, ref_out, base in ((ck_ref, nk_ref, 0), (cv_ref, nv_ref, A_KV_WIDTH)):
                shifted = pltpu.roll(ref_in[seq], keep - t_new, 1)
                ref_out[seq] = jnp.where(old, shifted, placed[base:base + A_KV_WIDTH])
    out_ref[...] = h_ref[...] + jnp.dot(og_ref[...].astype(BF16), wout_ref[...],
                                        preferred_element_type=F32)


def _attn_sample_call(sinks, q, kv, g, h, cache_k, cache_v, bias0, bias1, wout_bf16, t_new):
    nseq, keep = cache_k.shape[0], cache_k.shape[2]
    rows = SAMPLE_SB * t_new
    row = lambda i: (i, 0)
    cspec = pl.BlockSpec((SAMPLE_SB, A_KV_WIDTH, keep), lambda i: (i, 0, 0))
    bspec = pl.BlockSpec((A_HEADS, SUBLANES, SAMPLE_KEYS), lambda i: (0, 0, 0))
    return pl.pallas_call(
        functools.partial(_attn_sample_kernel, t_new=t_new),
        grid=(nseq // SAMPLE_SB,),
        in_specs=[pl.BlockSpec(memory_space=pltpu.SMEM),
                  pl.BlockSpec((rows, A_WIDTH), row),
                  pl.BlockSpec((rows, 2 * A_KV_WIDTH), row),
                  pl.BlockSpec((rows, A_WIDTH), row),
                  pl.BlockSpec((rows, D_MODEL), row),
                  cspec, cspec, bspec, bspec,
                  pl.BlockSpec((A_WIDTH, D_MODEL), lambda i: (0, 0))],
        out_specs=[pl.BlockSpec((rows, D_MODEL), row), cspec, cspec],
        out_shape=[jax.ShapeDtypeStruct((nseq * t_new, D_MODEL), F32),
                   jax.ShapeDtypeStruct(cache_k.shape, F32),
                   jax.ShapeDtypeStruct(cache_v.shape, F32)],
        scratch_shapes=[pltpu.VMEM((rows, A_WIDTH), F32)],
        compiler_params=_cparams(1),
        name="attn_sample",
    )(sinks, q, kv, g, h, cache_k, cache_v, bias0, bias1, wout_bf16)


def _rwkv_proj_kernel(h_ref, shift_ref, gain_ref, mu_ref, win_ref, w0_ref, w1_ref, w2_ref,
                      a0_ref, a1_ref, a2_ref,
                      r_ref, k_ref, v_ref, g_ref, ld_ref, a_ref, xn_ref, *scratch, seq_len):
    xn = _rmsnorm(h_ref[...], gain_ref[...])
    tm = xn.shape[0]
    rolled = pltpu.roll(xn, 1, 0)
    row = _iota((tm, 1), 0)
    if seq_len is None:
        carry_ref, = scratch

        @pl.when(pl.program_id(1) == 0)
        def _():
            carry_ref[...] = shift_ref[0]

        xprev = jnp.where(row == 0, carry_ref[...], rolled)
        carry_ref[...] = xn[tm - 1:tm, :]
        xn_ref[0] = xn[tm - 1:tm, :]
    else:
        xprev = jnp.where(row % seq_len == 0, shift_ref[...], rolled)
        xn_ref[...] = xn
    dx = xprev - xn

    def mix(c):
        return (xn + dx * mu_ref[c:c + 1, :]).astype(BF16)

    for c, o_ref in enumerate((r_ref, k_ref, v_ref, g_ref)):
        o_ref[...] = jnp.dot(mix(c), win_ref[c], preferred_element_type=F32)
    lw = jnp.tanh(jnp.dot(mix(4), w1_ref[...], preferred_element_type=F32))
    z = w0_ref[...] + jnp.dot(lw.astype(BF16), w2_ref[...], preferred_element_type=F32)
    ld_ref[...] = -math.exp(-0.5) * _sigmoid(z)
    la = jnp.dot(mix(5), a1_ref[...], preferred_element_type=F32)
    a_ref[...] = _sigmoid(a0_ref[...] + jnp.dot(la.astype(BF16), a2_ref[...],
                                                preferred_element_type=F32))


def _rwkv_proj_call(h, shift, p, nbatch, ntile, tm, seq_len):
    n = h.shape[0]
    row = lambda b, i: (b * ntile + i, 0)
    full2 = lambda b, i: (0, 0)
    if seq_len is None:
        shift_spec = pl.BlockSpec((1, 1, D_MODEL), lambda b, i: (b, 0, 0))
        xn_spec = pl.BlockSpec((1, 1, D_MODEL), lambda b, i: (b, 0, 0))
        xn_shape = jax.ShapeDtypeStruct((nbatch, 1, D_MODEL), F32)
        scratch = [pltpu.VMEM((1, D_MODEL), F32)]
    else:
        shift_spec = pl.BlockSpec((tm, D_MODEL), row)
        xn_spec = pl.BlockSpec((tm, D_MODEL), row)
        xn_shape = jax.ShapeDtypeStruct((n, D_MODEL), F32)
        scratch = []
    lora = p["w1"].shape[1]
    big = jax.ShapeDtypeStruct((n, D_MODEL), F32)
    return pl.pallas_call(
        functools.partial(_rwkv_proj_kernel, seq_len=seq_len),
        grid=(nbatch, ntile),
        in_specs=[pl.BlockSpec((tm, D_MODEL), row),
                  shift_spec,
                  pl.BlockSpec((1, D_MODEL), full2),
                  pl.BlockSpec(p["mu"].shape, full2),
                  pl.BlockSpec(p["w_in"].shape, lambda b, i: (0, 0, 0)),
                  pl.BlockSpec((1, D_MODEL), full2),
                  pl.BlockSpec((D_MODEL, lora), full2),
                  pl.BlockSpec((lora, D_MODEL), full2),
                  pl.BlockSpec((1, D_MODEL), full2),
                  pl.BlockSpec((D_MODEL, lora), full2),
                  pl.BlockSpec((lora, D_MODEL), full2)],
        out_specs=[pl.BlockSpec((tm, D_MODEL), row)] * 6 + [xn_spec],
        out_shape=[big] * 6 + [xn_shape],
        scratch_shapes=scratch,
        compiler_params=_cparams(2),
        name="rwkv_proj",
    )(h, shift, p["gain"], p["mu"], p["w_in"], p["w0"], p["w1"], p["w2"], p["a0"], p["a1"], p["a2"])


NN = (((1,), (0,)), ((), ()))
NT = (((1,), (1,)), ((), ()))
TN = (((0,), (0,)), ((), ()))


def _split(x):
    hi = x.astype(BF16)
    return hi, (x - hi.astype(F32)).astype(BF16)


def _mm(a, b, dn):
    return lax.dot_general(a, b, dn, preferred_element_type=F32)


def _mm2(a, b, dn):
    ah, al = a
    if dn == TN:
        both = _mm(jnp.concatenate([ah, al], axis=1), b, dn)
        m = ah.shape[1]
    else:
        both = _mm(jnp.concatenate([ah, al], axis=0), b, dn)
        m = ah.shape[0]
    return both[:m] + both[m:]


def _mm2_top(x_hi, x_lo_top, b, dn):
    c = x_lo_top.shape[0]
    both = _mm(jnp.concatenate([x_hi, x_lo_top], axis=0), b, dn)
    return both[:c] + both[2 * c:], both[c:2 * c]


def _seg_sums(xs, ones_bf16, exact):
    rows = xs[0].shape[0]
    stacked = jnp.concatenate(xs, axis=0)
    if exact:
        hi, lo = _split(stacked)
        out = _mm(lo, ones_bf16, NN) + _mm(hi, ones_bf16, NN)
    else:
        out = _mm(stacked.astype(BF16), ones_bf16, NN)
    return [out[i * rows:(i + 1) * rows] for i in range(len(xs))]


def _lane_lo():
    return _iota((1, LANES), 1) < R_HEAD_DIM


def _same_half():
    return (_iota((LANES, LANES), 0) < R_HEAD_DIM) == (_iota((LANES, LANES), 1) < R_HEAD_DIM)


def _ones_blk():
    return _same_half().astype(BF16)


def _bd(z):
    lo = _lane_lo()
    return jnp.concatenate([jnp.where(lo, z, 0.0), jnp.where(lo, 0.0, z)], axis=0)


def _bd_swap(z):
    lo = _lane_lo()
    return jnp.concatenate([jnp.where(lo, 0.0, z), jnp.where(lo, z, 0.0)], axis=0)


def _wkv_batch_stage(r, k, v, a, ld, kkg, kag, fillers=(), early=None):
    c = r[0].shape[0]
    pairs = range(len(r))
    half = R_HEAD_DIM
    lo = _lane_lo()
    t_row, t_col = _iota((c, LANES), 0), _iota((c, LANES), 1) & (half - 1)
    strict, incl = t_row > t_col, t_row >= t_col
    tri = (_iota((c, c), 0) >= _iota((c, c), 1)).astype(BF16)
    ones_blk = _ones_blk()
    eye_pair = (_iota((c, LANES), 0) == (_iota((c, LANES), 1) & (half - 1))).astype(F32)

    kkx = [k[j] * kkg[j] for j in pairs]
    ssq = _seg_sums([kkx[j] * kkx[j] for j in pairs], ones_blk, exact=True)
    ld_hi, ld_lo = _split(jnp.concatenate(ld, axis=1))
    cs_all = _mm(tri, ld_lo, NN) + _mm(tri, ld_hi, NN)
    cs = [cs_all[:, j * LANES:(j + 1) * LANES] for j in pairs]
    tot = [cs[j][c - 1:c, :] for j in pairs]
    x_hi, at_lo, kh_all, bke, vbd_s, lk_a, lk_r, lb_a, lb_r = ([] for _ in range(9))
    for j in pairs:
        kk = kkx[j] / jnp.maximum(jnp.sqrt(ssq[j]), 1e-12)
        kh = k[j] * (1.0 + (a[j] - 1.0) * kag[j])
        bv = kk * a[j]
        e_neg = jnp.exp(-cs[j])
        e_end = jnp.exp(tot[j] - cs[j])
        at, rt = -kk * jnp.exp(cs[j] - ld[j]), r[j] * jnp.exp(cs[j])
        at_hi, at_l = _split(at)
        x_hi.append(jnp.concatenate([at_hi, rt.astype(BF16)], axis=0))
        at_lo.append(at_l)
        kh_all.append(kh)
        y_hi = jnp.concatenate([(bv * e_neg).astype(BF16), (kh * e_neg).astype(BF16)], axis=0)
        bke.append(jnp.concatenate([bv * e_end, kh * e_end], axis=0))
        vbd_s.append(_bd(v[j].astype(BF16)))
        xa, xb = jnp.where(lo, x_hi[j], 0.0), jnp.where(lo, 0.0, x_hi[j])
        both = _mm(jnp.concatenate([xa, xb, jnp.where(lo, at_l, 0.0), jnp.where(lo, 0.0, at_l)], axis=0),
                   y_hi, NT)
        ga_a = pltpu.roll(both[:c] + both[4 * c:5 * c], half, 1)
        ga_r = pltpu.roll(both[c:2 * c], half, 1)
        gb_a = both[2 * c:3 * c] + both[5 * c:]
        gb_r = both[3 * c:4 * c]
        lk_a.append(jnp.where(strict, jnp.where(lo, ga_a, gb_a), 0.0))
        lk_r.append(jnp.where(incl, jnp.where(lo, ga_r, gb_r), 0.0))
        lb_a.append(jnp.where(strict, jnp.where(lo, gb_a, ga_a), 0.0))
        lb_r.append(jnp.where(incl, jnp.where(lo, gb_r, ga_r), 0.0))
    fillers = list(fillers)
    if early is not None:
        early.update(x_hi=x_hi, at_lo=at_lo)
    pw = lb_a
    acc = [eye_pair + pw[j] for j in pairs]
    pw = [_mm(pw[j].astype(BF16), _bd(pw[j].astype(BF16)), NN) for j in pairs]
    from_v = []
    for j in pairs:
        lk_hi, lk_lo = _split(lk_a[j])
        from_v.append(_mm2_top(jnp.concatenate([lk_hi, lk_r[j].astype(BF16)], axis=0), lk_lo, vbd_s[j], NN))
    for _ in range(int(math.log2(c)) - 2):
        both = [_mm(jnp.concatenate([pw[j], acc[j]], axis=0).astype(BF16), _bd(pw[j].astype(BF16)), NN)
                for j in pairs]
        if fillers:
            fillers.pop(0)()
        pw = [both[j][:c] for j in pairs]
        acc = [acc[j] + both[j][c:] for j in pairs]
    tinv = [acc[j] + _mm(acc[j].astype(BF16), _bd(pw[j].astype(BF16)), NN) for j in pairs]
    for fill in fillers:
        fill()
    return dict(x_hi=x_hi, at_lo=at_lo, kh=kh_all, lb_r=lb_r, tinv=tinv, from_v=from_v, bke=bke,
                tot=tot, ones_blk=ones_blk)


def _wkv_finish(stage, from_state_a, from_state_r):
    pairs = range(len(from_state_a))
    u = [_mm2(_split(stage["tinv"][j]),
              _bd_swap((from_state_a[j] + stage["from_v"][j][0]).astype(BF16)), NN) for j in pairs]
    y = [from_state_r[j] + stage["from_v"][j][1]
         + _mm(stage["lb_r"][j].astype(BF16), _bd_swap(u[j].astype(BF16)), NN) for j in pairs]
    return u, y


def _wkv_gate(y, r, kh, v, g, rk, lng, lnb, ones_blk):
    n = len(y)
    pairs = range(n)
    inv_n = 1.0 / R_HEAD_DIM
    sums = _seg_sums([r[j] * kh[j] * rk[j] for j in pairs] + list(y), ones_blk, exact=False)
    rkk, mean = sums[:n], [s * inv_n for s in sums[n:]]
    d = [y[j] - mean[j] for j in pairs]
    var = [s * inv_n for s in _seg_sums([d[j] * d[j] for j in pairs], ones_blk, exact=False)]
    return [((d[j] * lax.rsqrt(var[j] + GN_EPS) * lng[j] + lnb[j] + rkk[j] * v[j])
             * (g[j] * _sigmoid(g[j]))).astype(BF16) for j in pairs]


def _wkv_chunk_kernel(r_ref, k_ref, v_ref, a_ref, ld_ref, g_ref, kkg_ref, kag_ref, rk_ref, lng_ref, lnb_ref,
                      z_ref, sout_ref, st_ref):
    c = WKV_CHUNK
    n_sub = r_ref.shape[0] // c
    npair = r_ref.shape[1] // LANES
    pairs = range(npair)
    items = [(s, j) for s in range(n_sub) for j in pairs]

    @pl.when(pl.program_id(2) == 0)
    def _():
        st_ref[...] = jnp.zeros_like(st_ref)

    def tile(ref, it):
        s, j = it
        return ref[s * c:(s + 1) * c, j * LANES:(j + 1) * LANES]

    def par(ref):
        return [ref[:, j * LANES:(j + 1) * LANES] for _, j in items]

    r, k, v, a, ld, g = ([tile(ref, it) for it in items] for ref in (r_ref, k_ref, v_ref, a_ref, ld_ref, g_ref))
    st = [st_ref[j] for j in pairs]
    from_state = {}
    stage = {}

    def state_products(group):
        def run():
            for j in group:
                from_state[j] = _mm2_top(stage["x_hi"][j], stage["at_lo"][j], st[j].astype(BF16), NN)
        return run

    n_fill = 4
    groups = [list(pairs)[i::n_fill] for i in range(n_fill)]
    stage.update(_wkv_batch_stage(r, k, v, a, ld, par(kkg_ref), par(kag_ref),
                                  fillers=[state_products(grp) for grp in groups if grp], early=stage))
    same_half = _same_half()
    eye_full = _iota((LANES, LANES), 0) == _iota((LANES, LANES), 1)
    ys = []
    for s in range(n_sub):
        idx = [s * npair + j for j in pairs]
        if s > 0:
            for j in pairs:
                from_state[j] = _mm2_top(stage["x_hi"][idx[j]], stage["at_lo"][idx[j]], st[j].astype(BF16), NN)
        sub = {key: [stage[key][i] for i in idx] for key in ("tinv", "from_v", "lb_r")}
        us, ys_s = _wkv_finish(sub, [from_state[j][0] for j in pairs], [from_state[j][1] for j in pairs])
        upd = [_mm2(_split(stage["bke"][idx[j]]), jnp.concatenate([us[j], v[idx[j]]], axis=0).astype(BF16), TN)
               for j in pairs]
        ys += ys_s
        for j in pairs:
            w_col = jnp.sum(jnp.where(eye_full, jnp.exp(stage["tot"][idx[j]]), 0.0), axis=1, keepdims=True)
            st[j] = w_col * st[j] + jnp.where(same_half, upd[j], 0.0)
    zs = _wkv_gate(ys, r, stage["kh"], v, g, par(rk_ref), par(lng_ref), par(lnb_ref), stage["ones_blk"])
    for i, (s, j) in enumerate(items):
        z_ref[s * c:(s + 1) * c, j * LANES:(j + 1) * LANES] = zs[i]
    for j in pairs:
        st_ref[j] = st[j]
        sout_ref[0, j] = st[j]


def _wkv_chunk_call(r, k, v, a, ld, g, p, nbatch, seq):
    rows = WKV_CHUNK * WKV_CHUNKS_PER_STEP
    nstep = seq // rows
    npair = D_MODEL // LANES
    tile = pl.BlockSpec((rows, D_MODEL), lambda b, j, t: (b * nstep + t, 0))
    par = pl.BlockSpec((1, D_MODEL), lambda b, j, t: (0, 0))
    return pl.pallas_call(
        _wkv_chunk_kernel,
        grid=(nbatch, 1, nstep),
        in_specs=[tile] * 6 + [par] * 5,
        out_specs=[tile, pl.BlockSpec((1, npair, LANES, LANES), lambda b, j, t: (b, 0, 0, 0))],
        out_shape=[jax.ShapeDtypeStruct((nbatch * seq, D_MODEL), BF16),
                   jax.ShapeDtypeStruct((nbatch, npair, LANES, LANES), F32)],
        scratch_shapes=[pltpu.VMEM((npair, LANES, LANES), F32)],
        compiler_params=_cparams(3),
        name="wkv_chunk",
    )(r, k, v, a, ld, g, p["k_k"], p["k_a"], p["r_k"], p["ln_g"], p["ln_b"])


WKV_LANES_UNROLL = 4


def _wkv_lanes_kernel(r_ref, k_ref, v_ref, a_ref, ld_ref, g_ref, kkg_ref, kag_ref, rk_ref, lng_ref, lnb_ref,
                      s_ref, z_ref, sout_ref, prep_ref, y_ref, *, seq_len):
    n = R_HEAD_DIM
    nseq = s_ref.shape[3]
    eye = _iota((LANES, LANES), 0) == _iota((LANES, LANES), 1)

    def column(ref):
        return jnp.sum(jnp.where(eye, ref[...], 0.0), axis=1, keepdims=True)

    kkg, kag, rk, lng, lnb = (column(ref) for ref in (kkg_ref, kag_ref, rk_ref, lng_ref, lnb_ref))

    def token(ref, t):
        return ref[pl.ds(t, nseq, stride=seq_len), :].T

    bonus = []
    for t in range(seq_len):
        r, k, v, a = (token(ref, t) for ref in (r_ref, k_ref, v_ref, a_ref))
        w = jnp.exp(token(ld_ref, t))
        kkx = k * kkg
        kh = k * (1.0 + (a - 1.0) * kag)
        rkk = r * kh * rk
        tiles = []
        for hh in range(2):
            rows = slice(hh * n, (hh + 1) * n)
            nrm = jnp.sqrt(jnp.sum(kkx[rows] * kkx[rows], axis=0, keepdims=True))
            kk = kkx[rows] / jnp.maximum(nrm, 1e-12)
            for q, val in enumerate((w[rows], -kk, kk * a[rows], kh[rows], r[rows], v[rows])):
                prep_ref[q, t, hh] = val
            tiles.append(jnp.sum(rkk[rows], axis=0, keepdims=True) * v[rows])
        bonus.append(tiles)

    for hh in range(2):
        def advance(i, carry, hh=hh):
            for u in range(WKV_LANES_UNROLL):
                vi = i * WKV_LANES_UNROLL + u
                slab = s_ref[hh, vi]
                for t in range(seq_len):
                    w, av, bv, kh, r = (prep_ref[q, t, hh] for q in range(5))
                    vrow = prep_ref[5, t, hh, pl.ds(vi, 1), :]
                    sa = jnp.sum(slab * av, axis=0, keepdims=True)
                    slab = slab * w + sa * bv + vrow * kh
                    y_ref[t, hh, pl.ds(vi, 1), :] = jnp.sum(slab * r, axis=0, keepdims=True)
                sout_ref[hh, vi] = slab
            return carry

        lax.fori_loop(0, n // WKV_LANES_UNROLL, advance, 0)

    for t in range(seq_len):
        parts = []
        for hh in range(2):
            rows = slice(hh * n, (hh + 1) * n)
            y = y_ref[t, hh]
            d = y - jnp.mean(y, axis=0, keepdims=True)
            var = jnp.mean(d * d, axis=0, keepdims=True)
            parts.append(d * lax.rsqrt(var + GN_EPS) * lng[rows] + lnb[rows] + bonus[t][hh])
        g = token(g_ref, t)
        z = jnp.concatenate(parts, axis=0) * (g * _sigmoid(g))
        z_ref[pl.ds(t, nseq, stride=seq_len), :] = z.T


def _wkv_lanes_call(r, k, v, a, ld, g, p, state_hvkb, seq_len):
    n = r.shape[0]
    nseq = state_hvkb.shape[3]
    tile = pl.BlockSpec((n, LANES), lambda j: (0, j))
    par = pl.BlockSpec((1, LANES), lambda j: (0, j))
    sspec = pl.BlockSpec((2, R_HEAD_DIM, R_HEAD_DIM, nseq), lambda j: (j, 0, 0, 0))
    return pl.pallas_call(
        functools.partial(_wkv_lanes_kernel, seq_len=seq_len),
        grid=(D_MODEL // LANES,),
        in_specs=[tile] * 6 + [par] * 5 + [sspec],
        out_specs=[tile, sspec],
        out_shape=[jax.ShapeDtypeStruct((n, D_MODEL), F32),
                   jax.ShapeDtypeStruct(state_hvkb.shape, F32)],
        scratch_shapes=[pltpu.VMEM((6, seq_len, 2, R_HEAD_DIM, nseq), F32),
                        pltpu.VMEM((seq_len, 2, R_HEAD_DIM, nseq), F32)],
        compiler_params=_cparams(1),
        name="wkv_lanes",
    )(r, k, v, a, ld, g, p["k_k"], p["k_a"], p["r_k"], p["ln_g"], p["ln_b"], state_hvkb)


RWKV_OUT_PIECES = 8


def _rwkv_out_kernel(*refs):
    n = RWKV_OUT_PIECES
    z_refs, h_refs, (wout_ref, fg_ref, out_ref) = refs[:n], refs[n:2 * n], refs[2 * n:]
    z = jnp.concatenate([ref[...].astype(BF16) for ref in z_refs], axis=0)
    h = jnp.concatenate([ref[...] for ref in h_refs], axis=0)
    h2 = h + jnp.dot(z, wout_ref[...], preferred_element_type=F32)
    out_ref[...] = _rmsnorm(h2, fg_ref[...])


def _rwkv_out_call(z, h, p, nbatch, ntile, tm, pieces_per_batch, skip):
    piece = tm // RWKV_OUT_PIECES
    dst = lambda b, i: (b * ntile + i, 0)
    pspec = [pl.BlockSpec((piece, D_MODEL),
                          functools.partial(lambda b, i, kk: (b * pieces_per_batch + i * RWKV_OUT_PIECES + skip + kk, 0),
                                            kk=kk))
             for kk in range(RWKV_OUT_PIECES)]
    return pl.pallas_call(
        _rwkv_out_kernel,
        grid=(nbatch, ntile),
        in_specs=pspec + pspec + [pl.BlockSpec((D_MODEL, D_MODEL), lambda b, i: (0, 0)),
                                  pl.BlockSpec((1, D_MODEL), lambda b, i: (0, 0))],
        out_specs=pl.BlockSpec((tm, D_MODEL), dst),
        out_shape=jax.ShapeDtypeStruct((nbatch * ntile * tm, D_MODEL), F32),
        compiler_params=_cparams(2),
        name="rwkv_out",
    )(*([z] * RWKV_OUT_PIECES), *([h] * RWKV_OUT_PIECES), p["w_out"], p["final_gain"])


def _prompt_bucket():
    assert WINDOW == BLOCK
    rel = (np.arange(BLOCK)[:, None] - np.arange(BLOCK)[None, :]) % BLOCK
    return _t5_bucket_np(rel)


def _sample_bucket(keep, t_new, slot):
    t = (np.arange(SUBLANES) % t_new)[:, None]
    j = np.arange(SAMPLE_KEYS)[None, :]
    own = j - keep - slot * t_new
    rel = np.where(j < keep, keep + t - j, t - own)
    ok = (rel >= 0) & (rel < WINDOW) & ((j < keep) | ((own >= 0) & (own < t_new)))
    return np.where(ok, _t5_bucket_np(rel), -1).astype(np.int32)


def kernel(x_prompt, x_sample, cache_win_k, cache_win_v, state_wkv, state_shift, meta_tokens, rel_bias_table, norm_gain, final_gain, attn_w_in, attn_sinks, attn_w_out, rwkv_mu, rwkv_w_in, rwkv_w0, rwkv_w1, rwkv_w2, rwkv_a0, rwkv_a1, rwkv_a2, rwkv_k_k, rwkv_k_a, rwkv_r_k, rwkv_ln_gamma, rwkv_ln_beta, rwkv_w_out):
    nb, seq, _ = x_prompt.shape
    ns, t_new, _ = x_sample.shape
    keep = cache_win_k.shape[2]
    lp = seq + BLOCK
    nblk = lp // BLOCK
    row = lambda x: x.reshape(1, D_MODEL)

    w_in0 = attn_w_in[0].astype(BF16)
    w_out0 = attn_w_out[0].astype(BF16)
    gain0 = row(norm_gain[0])
    sinks = attn_sinks[0]
    rp = dict(gain=row(norm_gain[1]), mu=rwkv_mu[0], w_in=rwkv_w_in[0].astype(BF16),
              w0=row(rwkv_w0[0]), w1=rwkv_w1[0].astype(BF16), w2=rwkv_w2[0].astype(BF16),
              a0=row(rwkv_a0[0]), a1=rwkv_a1[0].astype(BF16), a2=rwkv_a2[0].astype(BF16),
              k_k=row(rwkv_k_k[0]), k_a=row(rwkv_k_a[0]), r_k=row(rwkv_r_k[0]), ln_g=row(rwkv_ln_gamma[0]),
              ln_b=row(rwkv_ln_beta[0]), w_out=rwkv_w_out[0].astype(BF16),
              final_gain=row(final_gain))

    bias_p, *bias_s = _bias_call(rel_bias_table, [_prompt_bucket()]
                                 + [_sample_bucket(keep, t_new, slot) for slot in range(2)])

    head = jnp.concatenate([jnp.zeros((PAD, D_MODEL), F32), meta_tokens.astype(F32)], axis=0)
    xp = x_prompt.reshape(nb * seq, D_MODEL)
    q, kv, g = _attn_proj_call(xp, head, gain0, w_in0, BF16, nb, lp // ATTN_PROJ_ROWS,
                               ATTN_PROJ_ROWS // BLOCK, BLOCK)
    h1 = _attn_prompt_call(sinks, q, kv, g, head, xp, bias_p, w_out0, nb, nblk)
    kv3 = kv.reshape(nb, lp, 2 * A_KV_WIDTH)[:, lp - WINDOW:, :]
    win_k_p = kv3[:, :, :A_KV_WIDTH].reshape(1, nb, WINDOW, A_KV_HEADS, A_HEAD_DIM)
    win_v_p = kv3[:, :, A_KV_WIDTH:].reshape(1, nb, WINDOW, A_KV_HEADS, A_HEAD_DIM)

    shift0 = jnp.zeros((nb, 1, D_MODEL), F32)
    r, k, v, g1, ld, a, xlast = _rwkv_proj_call(h1, shift0, rp, nb, lp // RWKV_PROJ_ROWS, RWKV_PROJ_ROWS, None)
    z, st = _wkv_chunk_call(r, k, v, a, ld, g1, rp, nb, lp)
    y_prompt = _rwkv_out_call(z, h1, rp, nb, seq // RWKV_OUT_ROWS, RWKV_OUT_ROWS,
                              lp * RWKV_OUT_PIECES // RWKV_OUT_ROWS, BLOCK * RWKV_OUT_PIECES // RWKV_OUT_ROWS)
    y_prompt = y_prompt.reshape(nb, seq, D_MODEL)
    st = st.reshape(nb, D_MODEL // LANES, 2, R_HEAD_DIM, 2, R_HEAD_DIM)
    st = jnp.stack([st[:, :, 0, :, 0, :], st[:, :, 1, :, 1, :]], axis=2)
    wkv_p = jnp.swapaxes(st, -1, -2).reshape(1, nb, R_HEADS, R_HEAD_DIM, R_HEAD_DIM)
    shift_p = xlast.reshape(1, nb, D_MODEL)

    xs = x_sample.reshape(ns * t_new, D_MODEL)
    qs, kvs, gs = _attn_proj_call(xs, None, gain0, w_in0, F32, 1, 1, 1, ns * t_new)
    ck = jnp.swapaxes(cache_win_k[0].reshape(ns, keep, A_KV_WIDTH), 1, 2)
    cv = jnp.swapaxes(cache_win_v[0].reshape(ns, keep, A_KV_WIDTH), 1, 2)
    h1s, nk, nv = _attn_sample_call(sinks, qs, kvs, gs, xs, ck, cv, bias_s[0], bias_s[1], w_out0, t_new)
    win_k_s = jnp.swapaxes(nk, 1, 2).reshape(1, ns, keep, A_KV_HEADS, A_HEAD_DIM)
    win_v_s = jnp.swapaxes(nv, 1, 2).reshape(1, ns, keep, A_KV_HEADS, A_HEAD_DIM)

    shift_rows = jnp.repeat(state_shift[0], t_new, axis=0)
    tms = 256
    rs, ks, vs, g1s, lds, as_, xns = _rwkv_proj_call(h1s, shift_rows, rp, 1, ns * t_new // tms, tms, t_new)
    zs, st_s = _wkv_lanes_call(rs, ks, vs, as_, lds, g1s, rp, jnp.transpose(state_wkv[0], (1, 2, 3, 0)), t_new)
    y_sample = _rwkv_out_call(zs, h1s, rp, 1, 1, ns * t_new, RWKV_OUT_PIECES, 0)
    y_sample = y_sample.reshape(ns, t_new, D_MODEL)
    wkv_s = jnp.transpose(st_s, (3, 0, 1, 2))[None]
    shift_s = xns.reshape(ns, t_new, D_MODEL)[:, t_new - 1][None]

    return (y_prompt, y_sample, win_k_p, win_v_p, wkv_p, shift_p, win_k_s, win_v_s, wkv_s, shift_s)
```

```python
import functools
import math

import numpy as np
import jax
import jax.numpy as jnp
from jax import lax
from jax.experimental import pallas as pl
from jax.experimental.pallas import tpu as pltpu

F32 = jnp.float32
BF16 = jnp.bfloat16

D_MODEL = 1024
N_META = 16
RMS_EPS = 1e-6
A_HEADS = 16
A_KV_HEADS = 4
A_HEAD_DIM = 64
A_WIDTH = A_HEADS * A_HEAD_DIM
A_KV_WIDTH = A_KV_HEADS * A_HEAD_DIM
WINDOW = 128
BLOCK = 128
N_BUCKETS = 32
MAX_DISTANCE = 128
R_HEAD_DIM = 64
R_HEADS = D_MODEL // R_HEAD_DIM
GN_EPS = 64e-5

LANES = 128
SUBLANES = 8
PAD = BLOCK - N_META
NEG = -1e30
WKV_CHUNK = 64
WKV_CHUNKS_PER_STEP = 3
ATTN_BLOCKS_PER_STEP = 3
ATTN_PROJ_ROWS = 384
RWKV_PROJ_ROWS = 528
RWKV_OUT_ROWS = 1024
VMEM_LIMIT = 56 * 1024 * 1024


def _cparams(n_axes):
    return pltpu.CompilerParams(dimension_semantics=("arbitrary",) * n_axes,
                                vmem_limit_bytes=VMEM_LIMIT)


def _rmsnorm(x, gain):
    return x * lax.rsqrt(jnp.mean(x * x, axis=-1, keepdims=True) + RMS_EPS) * gain


def _sigmoid(x):
    return 1.0 / (1.0 + jnp.exp(-x))


def _iota(shape, dim):
    return lax.broadcasted_iota(jnp.int32, shape, dim)


def _t5_bucket_np(rel):
    n = np.maximum(rel, 0)
    max_exact = N_BUCKETS // 2
    nf = np.maximum(n, max_exact).astype(np.float32)
    scale = np.float32(math.log(MAX_DISTANCE / max_exact))
    large = max_exact + (np.log(nf / np.float32(max_exact)) / scale
                         * np.float32(N_BUCKETS - max_exact)).astype(np.int32)
    large = np.minimum(large, N_BUCKETS - 1)
    return np.where(n < max_exact, n, large).astype(np.int32)


def _bias_kernel(table_ref, *refs):
    h = pl.program_id(0)
    n = len(refs) // 2
    for bucket_ref, out_ref in zip(refs[:n], refs[n:]):
        bk = bucket_ref[...]
        acc = jnp.full(bk.shape, NEG, F32)
        for b in range(N_BUCKETS):
            acc = jnp.where(bk == b, table_ref[b, h], acc)
        out_ref[0] = acc


def _bias_call(table, buckets_np):
    return pl.pallas_call(
        _bias_kernel,
        grid=(A_HEADS,),
        in_specs=[pl.BlockSpec(memory_space=pltpu.SMEM)]
                 + [pl.BlockSpec(bk.shape, lambda h: (0, 0)) for bk in buckets_np],
        out_specs=[pl.BlockSpec((1,) + bk.shape, lambda h: (h, 0, 0)) for bk in buckets_np],
        out_shape=[jax.ShapeDtypeStruct((A_HEADS,) + bk.shape, F32) for bk in buckets_np],
        compiler_params=_cparams(1),
        name="bias_expand",
    )(table, *(jnp.asarray(bk) for bk in buckets_np))


def _attn_proj_kernel(head_ref, *refs, n_piece):
    x_refs, (gain_ref, w_ref, q_ref, kv_ref, g_ref) = refs[:n_piece], refs[n_piece:]
    first = x_refs[0][...]
    if head_ref is not None:
        first = jnp.where(pl.program_id(1) == 0, head_ref[...], first)
    x = jnp.concatenate([first] + [ref[...] for ref in x_refs[1:]], axis=0)
    xn = _rmsnorm(x, gain_ref[...])
    proj = jnp.dot(xn.astype(BF16), w_ref[...], preferred_element_type=F32)
    q_ref[...] = (proj[:, :A_WIDTH] * (A_HEAD_DIM ** -0.5)).astype(q_ref.dtype)
    kv_ref[...] = proj[:, A_WIDTH:A_WIDTH + 2 * A_KV_WIDTH]
    g_ref[...] = proj[:, A_WIDTH + 2 * A_KV_WIDTH:]


def _attn_proj_call(x2d, head, gain, w_bf16, q_dtype, nbatch, ntile, n_piece, piece):
    tm = n_piece * piece
    wcols = w_bf16.shape[1]
    per_seq = x2d.shape[0] // (nbatch * piece)
    lead = 0 if head is None else 1
    dst = lambda b, i: (b * ntile + i, 0)
    xspec = [pl.BlockSpec((piece, D_MODEL),
                          functools.partial(lambda b, i, kk: (b * per_seq + jnp.maximum(i * n_piece + kk - lead, 0), 0),
                                            kk=kk))
             for kk in range(n_piece)]
    kern = functools.partial(_attn_proj_kernel, n_piece=n_piece)
    operands = [x2d] * n_piece + [gain, w_bf16]
    if head is None:
        kern = functools.partial(kern, None)
        head_spec = []
    else:
        head_spec = [pl.BlockSpec((piece, D_MODEL), lambda b, i: (0, 0))]
        operands = [head] + operands
    n = nbatch * ntile * tm
    return pl.pallas_call(
        kern,
        grid=(nbatch, ntile),
        in_specs=head_spec + xspec + [pl.BlockSpec((1, D_MODEL), lambda b, i: (0, 0)),
                                      pl.BlockSpec((D_MODEL, wcols), lambda b, i: (0, 0))],
        out_specs=[pl.BlockSpec((tm, A_WIDTH), dst),
                   pl.BlockSpec((tm, 2 * A_KV_WIDTH), dst),
                   pl.BlockSpec((tm, A_WIDTH), dst)],
        out_shape=[jax.ShapeDtypeStruct((n, A_WIDTH), q_dtype),
                   jax.ShapeDtypeStruct((n, 2 * A_KV_WIDTH), F32),
                   jax.ShapeDtypeStruct((n, A_WIDTH), F32)],
        compiler_params=_cparams(2),
        name="attn_proj",
    )(*operands)


def _padded_kv_tiles(kv, c):
    lo = _iota((1, LANES), 1) < A_HEAD_DIM
    j = c // 2
    out = []
    for base in (0, A_KV_WIDTH):
        t = kv[:, base + j * LANES: base + (j + 1) * LANES]
        tr = pltpu.roll(t, A_HEAD_DIM, 1)
        if c % 2 == 0:
            even, odd = jnp.where(lo, t, 0.0), jnp.where(lo, 0.0, tr)
        else:
            even, odd = jnp.where(lo, tr, 0.0), jnp.where(lo, 0.0, t)
        out += [even.astype(BF16), odd.astype(BF16)]
    return out


def _mm_nt(a, b):
    return lax.dot_general(a, b, (((1,), (1,)), ((), ())), preferred_element_type=F32)


def _attn_prompt_kernel(sinks_ref, q_ref, kvc_ref, kvp_ref, g_ref, head_ref, *refs):
    nb = ATTN_BLOCKS_PER_STEP
    x_refs, (bias_ref, wout_ref, out_ref, og_ref) = refs[:nb], refs[nb:]
    i = pl.program_id(1)
    stack = 2 * BLOCK
    row, col = _iota((stack, BLOCK), 0) & (BLOCK - 1), _iota((stack, BLOCK), 1)
    upper = _iota((stack, 1), 0) >= BLOCK
    own = col <= row
    chains = [(c, idx) for c in range(A_KV_HEADS) for idx in range(2)]
    n = range(len(chains))
    cur = [_padded_kv_tiles(kvp_ref[...], c) for c in range(A_KV_HEADS)]
    for j in range(nb):
        rows = slice(j * BLOCK, (j + 1) * BLOCK)
        kvalid = (i * nb + j - 1 + own.astype(jnp.int32)) * BLOCK + col >= PAD
        prev, cur = cur, [_padded_kv_tiles(kvc_ref[rows, :], c) for c in range(A_KV_HEADS)]
        s, sink = [], []
        for c, idx in chains:
            q2 = q_ref[rows, 2 * c * LANES:(2 * c + 2) * LANES]
            q2 = jnp.concatenate([q2[:, :LANES], q2[:, LANES:]], axis=0)
            sc = jnp.where(own, _mm_nt(q2, cur[c][idx]), _mm_nt(q2, prev[c][idx]))
            bias = jnp.concatenate([bias_ref[4 * c + idx], bias_ref[4 * c + 2 + idx]], axis=0)
            s.append(jnp.where(kvalid, sc + bias, NEG))
            sink.append(jnp.where(upper, sinks_ref[4 * c + 2 + idx], sinks_ref[4 * c + idx]))
        m = [jnp.maximum(jnp.max(s[t], axis=1, keepdims=True), sink[t]) for t in n]
        p = [jnp.exp(s[t] - m[t]) for t in n]
        den = [jnp.sum(p[t], axis=1, keepdims=True) + jnp.exp(sink[t] - m[t]) for t in n]
        o = []
        for t, (c, idx) in enumerate(chains):
            pv = (jnp.dot(jnp.where(own, p[t], 0.0).astype(BF16), cur[c][2 + idx], preferred_element_type=F32)
                  + jnp.dot(jnp.where(own, 0.0, p[t]).astype(BF16), prev[c][2 + idx], preferred_element_type=F32))
            o.append(pv * (1.0 / den[t]))
        for c in range(A_KV_HEADS):
            both = o[2 * c] + o[2 * c + 1]
            for half in range(2):
                sl = slice((2 * c + half) * LANES, (2 * c + half + 1) * LANES)
                gt = g_ref[rows, sl]
                og_ref[rows, sl] = (both[half * BLOCK:(half + 1) * BLOCK] * (gt * _sigmoid(gt))).astype(BF16)
    resid = jnp.concatenate([jnp.where(i == 0, head_ref[...], x_refs[0][...])]
                            + [ref[...] for ref in x_refs[1:]], axis=0)
    out_ref[...] = resid + jnp.dot(og_ref[...], wout_ref[...], preferred_element_type=F32)


def _attn_prompt_call(sinks, q, kv, g, head, x2d, bias, wout_bf16, nbatch, nblk):
    n = q.shape[0]
    nb = ATTN_BLOCKS_PER_STEP
    nstep = nblk // nb
    rows = nb * BLOCK
    row = lambda b, i: (b * nstep + i, 0)
    prev = lambda b, i: (b * nblk + jnp.maximum(i * nb - 1, 0), 0)
    xrow = [functools.partial(lambda b, i, j: (b * (nblk - 1) + jnp.maximum(i * nb + j - 1, 0), 0), j=j)
            for j in range(nb)]
    return pl.pallas_call(
        _attn_prompt_kernel,
        grid=(nbatch, nstep),
        in_specs=[pl.BlockSpec(memory_space=pltpu.SMEM),
                  pl.BlockSpec((rows, A_WIDTH), row),
                  pl.BlockSpec((rows, 2 * A_KV_WIDTH), row),
                  pl.BlockSpec((BLOCK, 2 * A_KV_WIDTH), prev),
                  pl.BlockSpec((rows, A_WIDTH), row),
                  pl.BlockSpec((BLOCK, D_MODEL), lambda b, i: (0, 0))]
                 + [pl.BlockSpec((BLOCK, D_MODEL), xrow[j]) for j in range(nb)]
                 + [pl.BlockSpec((A_HEADS, BLOCK, BLOCK), lambda b, i: (0, 0, 0)),
                    pl.BlockSpec((A_WIDTH, D_MODEL), lambda b, i: (0, 0))],
        out_specs=pl.BlockSpec((rows, D_MODEL), row),
        out_shape=jax.ShapeDtypeStruct((n, D_MODEL), F32),
        scratch_shapes=[pltpu.VMEM((rows, A_WIDTH), BF16)],
        compiler_params=_cparams(2),
        name="attn_prompt",
    )(sinks, q, kv, kv, g, head, *([x2d] * nb), bias, wout_bf16)


SAMPLE_SB = 8
SAMPLE_KEYS = 2 * BLOCK


def _attn_sample_kernel(sinks_ref, q_ref, kvn_ref, g_ref, h_ref, ck_ref, cv_ref, bias0_ref, bias1_ref,
                        wout_ref, out_ref, nk_ref, nv_ref, og_ref, *, t_new):
    keep = ck_ref.shape[2]
    lo = _iota((1, LANES), 1) < A_HEAD_DIM
    stack = 8 * SUBLANES
    own = (_iota((stack, 1), 0) & (SUBLANES - 1)) // t_new
    piece = _iota((stack, 1), 0) // SUBLANES
    bias_refs = (bias0_ref, bias1_ref)
    n_tile = A_KV_WIDTH // LANES
    pair_ids = range(SAMPLE_SB * t_new // SUBLANES)

    bias_c, bias_n, sink = [], [], []
    for j in range(n_tile):
        heads = slice(8 * j, 8 * j + 8)
        bias_c.append(bias0_ref[heads, :, :keep].reshape(stack, keep))
        bias_n.append([ref[heads, :, keep:keep + SUBLANES].reshape(stack, SUBLANES) for ref in bias_refs])
        col = jnp.zeros((stack, 1), F32)
        for gq in range(8):
            col = jnp.where(piece == gq, sinks_ref[8 * j + gq], col)
        sink.append(col)

    chains = [(p, j, s) for p in pair_ids for j in range(n_tile) for s in range(SUBLANES // t_new)]
    qs = {}
    for p in pair_ids:
        rows = slice(p * SUBLANES, (p + 1) * SUBLANES)
        for j in range(n_tile):
            parts = []
            for gq in range(8):
                t = q_ref[rows, (4 * j + gq // 2) * LANES:(4 * j + gq // 2 + 1) * LANES]
                want_lo = gq < 4
                if (gq % 2 == 0) != want_lo:
                    t = pltpu.roll(t, A_HEAD_DIM, 1)
                parts.append(jnp.where(lo, t, 0.0) if want_lo else jnp.where(lo, 0.0, t))
            qs[p, j] = jnp.concatenate(parts, axis=0).astype(BF16)

    def kv_tiles(p, j, s, base):
        seq = p * (SUBLANES // t_new) + s
        cache = (ck_ref if base == 0 else cv_ref)[seq][j * LANES:(j + 1) * LANES, :].astype(BF16)
        new = kvn_ref[p * SUBLANES:(p + 1) * SUBLANES, base + j * LANES:base + (j + 1) * LANES].astype(BF16)
        return cache, new

    sc, sn = [], []
    for p, j, s in chains:
        kc, kn = kv_tiles(p, j, s, 0)
        sc.append(jnp.dot(qs[p, j], kc, preferred_element_type=F32) + bias_c[j])
        sn.append(_mm_nt(qs[p, j], kn) + bias_n[j][s])
    mx = [jnp.maximum(jnp.maximum(jnp.max(sc[i], axis=1, keepdims=True),
                                  jnp.max(sn[i], axis=1, keepdims=True)), sink[chains[i][1]])
          for i in range(len(chains))]
    pc = [jnp.exp(sc[i] - mx[i]) for i in range(len(chains))]
    pn = [jnp.exp(sn[i] - mx[i]) for i in range(len(chains))]
    den = [jnp.sum(pc[i], axis=1, keepdims=True) + jnp.sum(pn[i], axis=1, keepdims=True)
           + jnp.exp(sink[chains[i][1]] - mx[i]) for i in range(len(chains))]
    outs = {}
    for i, (p, j, s) in enumerate(chains):
        vc, vn = kv_tiles(p, j, s, A_KV_WIDTH)
        o = (_mm_nt(pc[i].astype(BF16), vc)
             + jnp.dot(pn[i].astype(BF16), vn, preferred_element_type=F32)) * (1.0 / den[i])
        outs[p, j] = o if s == 0 else jnp.where(own == s, o, outs[p, j])
    for p in pair_ids:
        rows = slice(p * SUBLANES, (p + 1) * SUBLANES)
        for j in range(n_tile):
            o = outs[p, j]
            for gg in range(4):
                even = o[2 * gg * SUBLANES:(2 * gg + 1) * SUBLANES]
                odd = o[(2 * gg + 1) * SUBLANES:(2 * gg + 2) * SUBLANES]
                if gg < 2:
                    tile = jnp.where(lo, even, pltpu.roll(odd, A_HEAD_DIM, 1))
                else:
                    tile = jnp.where(lo, pltpu.roll(even, A_HEAD_DIM, 1), odd)
                sl = slice((4 * j + gg) * LANES, (4 * j + gg + 1) * LANES)
                gt = g_ref[rows, sl]
                og_ref[rows, sl] = tile * (gt * _sigmoid(gt))
    lane_pos = _iota((SUBLANES, keep), 1)
    old = _iota((1, keep), 1) < keep - t_new
    for p in pair_ids:
        new8 = kvn_ref[p * SUBLANES:(p + 1) * SUBLANES, :]
        for s in range(SUBLANES // t_new):
            seq = p * (SUBLANES // t_new) + s
            sel = (lane_pos == _iota((SUBLANES, keep), 0) + (keep - t_new - s * t_new)).astype(BF16)
            placed = _mm2(_split(new8), sel, TN)
            for ref_in, ref_out, base in ((ck_ref, nk_ref, 0), (cv_ref, nv_ref, A_KV_WIDTH)):
                shifted = pltpu.roll(ref_in[seq], keep - t_new, 1)
                ref_out[seq] = jnp.where(old, shifted, placed[base:base + A_KV_WIDTH])
    out_ref[...] = h_ref[...] + jnp.dot(og_ref[...].astype(BF16), wout_ref[...],
                                        preferred_element_type=F32)


def _attn_sample_call(sinks, q, kv, g, h, cache_k, cache_v, bias0, bias1, wout_bf16, t_new):
    nseq, keep = cache_k.shape[0], cache_k.shape[2]
    rows = SAMPLE_SB * t_new
    row = lambda i: (i, 0)
    cspec = pl.BlockSpec((SAMPLE_SB, A_KV_WIDTH, keep), lambda i: (i, 0, 0))
    bspec = pl.BlockSpec((A_HEADS, SUBLANES, SAMPLE_KEYS), lambda i: (0, 0, 0))
    return pl.pallas_call(
        functools.partial(_attn_sample_kernel, t_new=t_new),
        grid=(nseq // SAMPLE_SB,),
        in_specs=[pl.BlockSpec(memory_space=pltpu.SMEM),
                  pl.BlockSpec((rows, A_WIDTH), row),
                  pl.BlockSpec((rows, 2 * A_KV_WIDTH), row),
                  pl.BlockSpec((rows, A_WIDTH), row),
                  pl.BlockSpec((rows, D_MODEL), row),
                  cspec, cspec, bspec, bspec,
                  pl.BlockSpec((A_WIDTH, D_MODEL), lambda i: (0, 0))],
        out_specs=[pl.BlockSpec((rows, D_MODEL), row), cspec, cspec],
        out_shape=[jax.ShapeDtypeStruct((nseq * t_new, D_MODEL), F32),
                   jax.ShapeDtypeStruct(cache_k.shape, F32),
                   jax.ShapeDtypeStruct(cache_v.shape, F32)],
        scratch_shapes=[pltpu.VMEM((rows, A_WIDTH), F32)],
        compiler_params=_cparams(1),
        name="attn_sample",
    )(sinks, q, kv, g, h, cache_k, cache_v, bias0, bias1, wout_bf16)


def _rwkv_proj_kernel(h_ref, shift_ref, gain_ref, mu_ref, win_ref, w0_ref, w1_ref, w2_ref,
                      a0_ref, a1_ref, a2_ref,
                      r_ref, k_ref, v_ref, g_ref, ld_ref, a_ref, xn_ref, *scratch, seq_len):
    xn = _rmsnorm(h_ref[...], gain_ref[...])
    tm = xn.shape[0]
    rolled = pltpu.roll(xn, 1, 0)
    row = _iota((tm, 1), 0)
    if seq_len is None:
        carry_ref, = scratch

        @pl.when(pl.program_id(1) == 0)
        def _():
            carry_ref[...] = shift_ref[0]

        xprev = jnp.where(row == 0, carry_ref[...], rolled)
        carry_ref[...] = xn[tm - 1:tm, :]
        xn_ref[0] = xn[tm - 1:tm, :]
    else:
        xprev = jnp.where(row % seq_len == 0, shift_ref[...], rolled)
        xn_ref[...] = xn
    dx = xprev - xn

    def mix(c):
        return (xn + dx * mu_ref[c:c + 1, :]).astype(BF16)

    for c, o_ref in enumerate((r_ref, k_ref, v_ref, g_ref)):
        o_ref[...] = jnp.dot(mix(c), win_ref[c], preferred_element_type=F32)
    lw = jnp.tanh(jnp.dot(mix(4), w1_ref[...], preferred_element_type=F32))
    z = w0_ref[...] + jnp.dot(lw.astype(BF16), w2_ref[...], preferred_element_type=F32)
    ld_ref[...] = -math.exp(-0.5) * _sigmoid(z)
    la = jnp.dot(mix(5), a1_ref[...], preferred_element_type=F32)
    a_ref[...] = _sigmoid(a0_ref[...] + jnp.dot(la.astype(BF16), a2_ref[...],
                                                preferred_element_type=F32))


def _rwkv_proj_call(h, shift, p, nbatch, ntile, tm, seq_len):
    n = h.shape[0]
    row = lambda b, i: (b * ntile + i, 0)
    full2 = lambda b, i: (0, 0)
    if seq_len is None:
        shift_spec = pl.BlockSpec((1, 1, D_MODEL), lambda b, i: (b, 0, 0))
        xn_spec = pl.BlockSpec((1, 1, D_MODEL), lambda b, i: (b, 0, 0))
        xn_shape = jax.ShapeDtypeStruct((nbatch, 1, D_MODEL), F32)
        scratch = [pltpu.VMEM((1, D_MODEL), F32)]
    else:
        shift_spec = pl.BlockSpec((tm, D_MODEL), row)
        xn_spec = pl.BlockSpec((tm, D_MODEL), row)
        xn_shape = jax.ShapeDtypeStruct((n, D_MODEL), F32)
        scratch = []
    lora = p["w1"].shape[1]
    big = jax.ShapeDtypeStruct((n, D_MODEL), F32)
    return pl.pallas_call(
        functools.partial(_rwkv_proj_kernel, seq_len=seq_len),
        grid=(nbatch, ntile),
        in_specs=[pl.BlockSpec((tm, D_MODEL), row),
                  shift_spec,
                  pl.BlockSpec((1, D_MODEL), full2),
                  pl.BlockSpec(p["mu"].shape, full2),
                  pl.BlockSpec(p["w_in"].shape, lambda b, i: (0, 0, 0)),
                  pl.BlockSpec((1, D_MODEL), full2),
                  pl.BlockSpec((D_MODEL, lora), full2),
                  pl.BlockSpec((lora, D_MODEL), full2),
                  pl.BlockSpec((1, D_MODEL), full2),
                  pl.BlockSpec((D_MODEL, lora), full2),
                  pl.BlockSpec((lora, D_MODEL), full2)],
        out_specs=[pl.BlockSpec((tm, D_MODEL), row)] * 6 + [xn_spec],
        out_shape=[big] * 6 + [xn_shape],
        scratch_shapes=scratch,
        compiler_params=_cparams(2),
        name="rwkv_proj",
    )(h, shift, p["gain"], p["mu"], p["w_in"], p["w0"], p["w1"], p["w2"], p["a0"], p["a1"], p["a2"])


NN = (((1,), (0,)), ((), ()))
NT = (((1,), (1,)), ((), ()))
TN = (((0,), (0,)), ((), ()))


def _split(x):
    hi = x.astype(BF16)
    return hi, (x - hi.astype(F32)).astype(BF16)


def _mm(a, b, dn):
    return lax.dot_general(a, b, dn, preferred_element_type=F32)


def _mm2(a, b, dn):
    ah, al = a
    if dn == TN:
        both = _mm(jnp.concatenate([ah, al], axis=1), b, dn)
        m = ah.shape[1]
    else:
        both = _mm(jnp.concatenate([ah, al], axis=0), b, dn)
        m = ah.shape[0]
    return both[:m] + both[m:]


def _seg_sums(xs, ones_bf16, exact):
    rows = xs[0].shape[0]
    stacked = jnp.concatenate(xs, axis=0)
    if exact:
        hi, lo = _split(stacked)
        out = _mm(lo, ones_bf16, NN) + _mm(hi, ones_bf16, NN)
    else:
        out = _mm(stacked.astype(BF16), ones_bf16, NN)
    return [out[i * rows:(i + 1) * rows] for i in range(len(xs))]


def _lane_lo():
    return _iota((1, LANES), 1) < R_HEAD_DIM


def _same_half():
    return (_iota((LANES, LANES), 0) < R_HEAD_DIM) == (_iota((LANES, LANES), 1) < R_HEAD_DIM)


def _ones_blk():
    return _same_half().astype(BF16)


def _bd(z):
    lo = _lane_lo()
    return jnp.concatenate([jnp.where(lo, z, 0.0), jnp.where(lo, 0.0, z)], axis=0)


def _bd_swap(z):
    lo = _lane_lo()
    return jnp.concatenate([jnp.where(lo, 0.0, z), jnp.where(lo, z, 0.0)], axis=0)


def _wkv_batch_stage(r, k, v, a, ld, kkg, kag, fillers=(), early=None):
    c = r[0].shape[0]
    pairs = range(len(r))
    half = R_HEAD_DIM
    lo = _lane_lo()
    t_row, t_col = _iota((c, LANES), 0), _iota((c, LANES), 1) & (half - 1)
    strict, incl = t_row > t_col, t_row >= t_col
    tri = (_iota((c, c), 0) >= _iota((c, c), 1)).astype(BF16)
    ones_blk = _ones_blk()
    eye_pair = (_iota((c, LANES), 0) == (_iota((c, LANES), 1) & (half - 1))).astype(F32)

    kkx = [k[j] * kkg[j] for j in pairs]
    ssq = _seg_sums([kkx[j] * kkx[j] for j in pairs], ones_blk, exact=True)
    ld_hi, ld_lo = _split(jnp.concatenate(ld, axis=1))
    cs_all = _mm(tri, ld_lo, NN) + _mm(tri, ld_hi, NN)
    cs = [cs_all[:, j * LANES:(j + 1) * LANES] for j in pairs]
    tot = [cs[j][c - 1:c, :] for j in pairs]
    x_hi, kh_all, bke, vbd_s, lk_a, lk_r, lb_a, lb_r = ([] for _ in range(8))
    for j in pairs:
        kk = kkx[j] / jnp.maximum(jnp.sqrt(ssq[j]), 1e-12)
        kh = k[j] * (1.0 + (a[j] - 1.0) * kag[j])
        bv = kk * a[j]
        e_neg = jnp.exp(-cs[j])
        e_end = jnp.exp(tot[j] - cs[j])
        at, rt = -kk * jnp.exp(cs[j] - ld[j]), r[j] * jnp.exp(cs[j])
        x_hi.append(jnp.concatenate([at.astype(BF16), rt.astype(BF16)], axis=0))
        kh_all.append(kh)
        y_hi = jnp.concatenate([(bv * e_neg).astype(BF16), (kh * e_neg).astype(BF16)], axis=0)
        bke.append(jnp.concatenate([(bv * e_end).astype(BF16), (kh * e_end).astype(BF16)], axis=0))
        vbd_s.append(_bd(v[j].astype(BF16)))
        both = _mm(jnp.concatenate([jnp.where(lo, x_hi[j], 0.0), jnp.where(lo, 0.0, x_hi[j])], axis=0), y_hi, NT)
        ga_a = pltpu.roll(both[:c], half, 1)
        ga_r = pltpu.roll(both[c:2 * c], half, 1)
        gb_a = both[2 * c:3 * c]
        gb_r = both[3 * c:4 * c]
        lk_a.append(jnp.where(strict, jnp.where(lo, ga_a, gb_a), 0.0))
        lk_r.append(jnp.where(incl, jnp.where(lo, ga_r, gb_r), 0.0))
        lb_a.append(jnp.where(strict, jnp.where(lo, gb_a, ga_a), 0.0))
        lb_r.append(jnp.where(incl, jnp.where(lo, gb_r, ga_r), 0.0))
    fillers = list(fillers)
    if early is not None:
        early.update(x_hi=x_hi)
    pw = lb_a
    acc = [eye_pair + pw[j] for j in pairs]
    pw = [_mm(pw[j].astype(BF16), _bd(pw[j].astype(BF16)), NN) for j in pairs]
    from_v = []
    for j in pairs:
        both = _mm(jnp.concatenate([lk_a[j].astype(BF16), lk_r[j].astype(BF16)], axis=0), vbd_s[j], NN)
        from_v.append((both[:c], both[c:]))
    for _ in range(int(math.log2(c)) - 2):
        both = [_mm(jnp.concatenate([pw[j], acc[j]], axis=0).astype(BF16), _bd(pw[j].astype(BF16)), NN)
                for j in pairs]
        if fillers:
            fillers.pop(0)()
        pw = [both[j][:c] for j in pairs]
        acc = [acc[j] + both[j][c:] for j in pairs]
    tinv = [acc[j] + _mm(acc[j].astype(BF16), _bd(pw[j].astype(BF16)), NN) for j in pairs]
    for fill in fillers:
        fill()
    return dict(x_hi=x_hi, kh=kh_all, lb_r=lb_r, tinv=tinv, from_v=from_v, bke=bke,
                tot=tot, ones_blk=ones_blk)


def _wkv_finish(stage, from_state_a, from_state_r):
    pairs = range(len(from_state_a))
    u = [_mm(stage["tinv"][j].astype(BF16),
             _bd_swap((from_state_a[j] + stage["from_v"][j][0]).astype(BF16)), NN) for j in pairs]
    y = [from_state_r[j] + stage["from_v"][j][1]
         + _mm(stage["lb_r"][j].astype(BF16), _bd_swap(u[j].astype(BF16)), NN) for j in pairs]
    return u, y


def _wkv_gate(y, r, kh, v, g, rk, lng, lnb, ones_blk):
    n = len(y)
    pairs = range(n)
    inv_n = 1.0 / R_HEAD_DIM
    sums = _seg_sums([r[j] * kh[j] * rk[j] for j in pairs] + list(y), ones_blk, exact=False)
    rkk, mean = sums[:n], [s * inv_n for s in sums[n:]]
    d = [y[j] - mean[j] for j in pairs]
    var = [s * inv_n for s in _seg_sums([d[j] * d[j] for j in pairs], ones_blk, exact=False)]
    return [((d[j] * lax.rsqrt(var[j] + GN_EPS) * lng[j] + lnb[j] + rkk[j] * v[j])
             * (g[j] * _sigmoid(g[j]))).astype(BF16) for j in pairs]


def _wkv_chunk_kernel(r_ref, k_ref, v_ref, a_ref, ld_ref, g_ref, kkg_ref, kag_ref, rk_ref, lng_ref, lnb_ref,
                      z_ref, sout_ref, st_ref):
    c = WKV_CHUNK
    n_sub = r_ref.shape[0] // c
    npair = r_ref.shape[1] // LANES
    pairs = range(npair)
    items = [(s, j) for s in range(n_sub) for j in pairs]

    @pl.when(pl.program_id(2) == 0)
    def _():
        st_ref[...] = jnp.zeros_like(st_ref)

    def tile(ref, it):
        s, j = it
        return ref[s * c:(s + 1) * c, j * LANES:(j + 1) * LANES]

    def par(ref):
        return [ref[:, j * LANES:(j + 1) * LANES] for _, j in items]

    r, k, v, a, ld, g = ([tile(ref, it) for it in items] for ref in (r_ref, k_ref, v_ref, a_ref, ld_ref, g_ref))
    st = [st_ref[j] for j in pairs]
    from_state = {}
    stage = {}

    def state_products(group):
        def run():
            for j in group:
                from_state[j] = _mm(stage["x_hi"][j], st[j].astype(BF16), NN)
        return run

    n_fill = 4
    groups = [list(pairs)[i::n_fill] for i in range(n_fill)]
    stage.update(_wkv_batch_stage(r, k, v, a, ld, par(kkg_ref), par(kag_ref),
                                  fillers=[state_products(grp) for grp in groups if grp], early=stage))
    same_half = _same_half()
    eye_full = _iota((LANES, LANES), 0) == _iota((LANES, LANES), 1)
    ys = []
    for s in range(n_sub):
        idx = [s * npair + j for j in pairs]
        if s > 0:
            for j in pairs:
                from_state[j] = _mm(stage["x_hi"][idx[j]], st[j].astype(BF16), NN)
        sub = {key: [stage[key][i] for i in idx] for key in ("tinv", "from_v", "lb_r")}
        us, ys_s = _wkv_finish(sub, [from_state[j][:c] for j in pairs], [from_state[j][c:] for j in pairs])
        upd = [_mm(stage["bke"][idx[j]], jnp.concatenate([us[j], v[idx[j]]], axis=0).astype(BF16), TN)
               for j in pairs]
        ys += ys_s
        for j in pairs:
            w_col = jnp.sum(jnp.where(eye_full, jnp.exp(stage["tot"][idx[j]]), 0.0), axis=1, keepdims=True)
            st[j] = w_col * st[j] + jnp.where(same_half, upd[j], 0.0)
    zs = _wkv_gate(ys, r, stage["kh"], v, g, par(rk_ref), par(lng_ref), par(lnb_ref), stage["ones_blk"])
    for i, (s, j) in enumerate(items):
        z_ref[s * c:(s + 1) * c, j * LANES:(j + 1) * LANES] = zs[i]
    for j in pairs:
        st_ref[j] = st[j]
        sout_ref[0, j] = st[j]


def _wkv_chunk_call(r, k, v, a, ld, g, p, nbatch, seq):
    rows = WKV_CHUNK * WKV_CHUNKS_PER_STEP
    nstep = seq // rows
    npair = D_MODEL // LANES
    tile = pl.BlockSpec((rows, D_MODEL), lambda b, j, t: (b * nstep + t, 0))
    par = pl.BlockSpec((1, D_MODEL), lambda b, j, t: (0, 0))
    return pl.pallas_call(
        _wkv_chunk_kernel,
        grid=(nbatch, 1, nstep),
        in_specs=[tile] * 6 + [par] * 5,
        out_specs=[tile, pl.BlockSpec((1, npair, LANES, LANES), lambda b, j, t: (b, 0, 0, 0))],
        out_shape=[jax.ShapeDtypeStruct((nbatch * seq, D_MODEL), BF16),
                   jax.ShapeDtypeStruct((nbatch, npair, LANES, LANES), F32)],
        scratch_shapes=[pltpu.VMEM((npair, LANES, LANES), F32)],
        compiler_params=_cparams(3),
        name="wkv_chunk",
    )(r, k, v, a, ld, g, p["k_k"], p["k_a"], p["r_k"], p["ln_g"], p["ln_b"])


WKV_LANES_UNROLL = 4


def _wkv_lanes_kernel(r_ref, k_ref, v_ref, a_ref, ld_ref, g_ref, kkg_ref, kag_ref, rk_ref, lng_ref, lnb_ref,
                      s_ref, z_ref, sout_ref, prep_ref, y_ref, *, seq_len):
    n = R_HEAD_DIM
    nseq = s_ref.shape[3]
    eye = _iota((LANES, LANES), 0) == _iota((LANES, LANES), 1)

    def column(ref):
        return jnp.sum(jnp.where(eye, ref[...], 0.0), axis=1, keepdims=True)

    kkg, kag, rk, lng, lnb = (column(ref) for ref in (kkg_ref, kag_ref, rk_ref, lng_ref, lnb_ref))

    def token(ref, t):
        return ref[pl.ds(t, nseq, stride=seq_len), :].T

    bonus = []
    for t in range(seq_len):
        r, k, v, a = (token(ref, t) for ref in (r_ref, k_ref, v_ref, a_ref))
        w = jnp.exp(token(ld_ref, t))
        kkx = k * kkg
        kh = k * (1.0 + (a - 1.0) * kag)
        rkk = r * kh * rk
        tiles = []
        for hh in range(2):
            rows = slice(hh * n, (hh + 1) * n)
            nrm = jnp.sqrt(jnp.sum(kkx[rows] * kkx[rows], axis=0, keepdims=True))
            kk = kkx[rows] / jnp.maximum(nrm, 1e-12)
            for q, val in enumerate((w[rows], -kk, kk * a[rows], kh[rows], r[rows], v[rows])):
                prep_ref[q, t, hh] = val
            tiles.append(jnp.sum(rkk[rows], axis=0, keepdims=True) * v[rows])
        bonus.append(tiles)

    for hh in range(2):
        def advance(i, carry, hh=hh):
            for u in range(WKV_LANES_UNROLL):
                vi = i * WKV_LANES_UNROLL + u
                slab = s_ref[hh, vi]
                for t in range(seq_len):
                    w, av, bv, kh, r = (prep_ref[q, t, hh] for q in range(5))
                    vrow = prep_ref[5, t, hh, pl.ds(vi, 1), :]
                    sa = jnp.sum(slab * av, axis=0, keepdims=True)
                    slab = slab * w + sa * bv + vrow * kh
                    y_ref[t, hh, pl.ds(vi, 1), :] = jnp.sum(slab * r, axis=0, keepdims=True)
                sout_ref[hh, vi] = slab
            return carry

        lax.fori_loop(0, n // WKV_LANES_UNROLL, advance, 0)

    for t in range(seq_len):
        parts = []
        for hh in range(2):
            rows = slice(hh * n, (hh + 1) * n)
            y = y_ref[t, hh]
            d = y - jnp.mean(y, axis=0, keepdims=True)
            var = jnp.mean(d * d, axis=0, keepdims=True)
            parts.append(d * lax.rsqrt(var + GN_EPS) * lng[rows] + lnb[rows] + bonus[t][hh])
        g = token(g_ref, t)
        z = jnp.concatenate(parts, axis=0) * (g * _sigmoid(g))
        z_ref[pl.ds(t, nseq, stride=seq_len), :] = z.T


def _wkv_lanes_call(r, k, v, a, ld, g, p, state_hvkb, seq_len):
    n = r.shape[0]
    nseq = state_hvkb.shape[3]
    tile = pl.BlockSpec((n, LANES), lambda j: (0, j))
    par = pl.BlockSpec((1, LANES), lambda j: (0, j))
    sspec = pl.BlockSpec((2, R_HEAD_DIM, R_HEAD_DIM, nseq), lambda j: (j, 0, 0, 0))
    return pl.pallas_call(
        functools.partial(_wkv_lanes_kernel, seq_len=seq_len),
        grid=(D_MODEL // LANES,),
        in_specs=[tile] * 6 + [par] * 5 + [sspec],
        out_specs=[tile, sspec],
        out_shape=[jax.ShapeDtypeStruct((n, D_MODEL), F32),
                   jax.ShapeDtypeStruct(state_hvkb.shape, F32)],
        scratch_shapes=[pltpu.VMEM((6, seq_len, 2, R_HEAD_DIM, nseq), F32),
                        pltpu.VMEM((seq_len, 2, R_HEAD_DIM, nseq), F32)],
        compiler_params=_cparams(1),
        name="wkv_lanes",
    )(r, k, v, a, ld, g, p["k_k"], p["k_a"], p["r_k"], p["ln_g"], p["ln_b"], state_hvkb)


RWKV_OUT_PIECES = 8


def _rwkv_out_kernel(*refs):
    n = RWKV_OUT_PIECES
    z_refs, h_refs, (wout_ref, fg_ref, out_ref) = refs[:n], refs[n:2 * n], refs[2 * n:]
    z = jnp.concatenate([ref[...].astype(BF16) for ref in z_refs], axis=0)
    h = jnp.concatenate([ref[...] for ref in h_refs], axis=0)
    h2 = h + jnp.dot(z, wout_ref[...], preferred_element_type=F32)
    out_ref[...] = _rmsnorm(h2, fg_ref[...])


def _rwkv_out_call(z, h, p, nbatch, ntile, tm, pieces_per_batch, skip):
    piece = tm // RWKV_OUT_PIECES
    dst = lambda b, i: (b * ntile + i, 0)
    pspec = [pl.BlockSpec((piece, D_MODEL),
                          functools.partial(lambda b, i, kk: (b * pieces_per_batch + i * RWKV_OUT_PIECES + skip + kk, 0),
                                            kk=kk))
             for kk in range(RWKV_OUT_PIECES)]
    return pl.pallas_call(
        _rwkv_out_kernel,
        grid=(nbatch, ntile),
        in_specs=pspec + pspec + [pl.BlockSpec((D_MODEL, D_MODEL), lambda b, i: (0, 0)),
                                  pl.BlockSpec((1, D_MODEL), lambda b, i: (0, 0))],
        out_specs=pl.BlockSpec((tm, D_MODEL), dst),
        out_shape=jax.ShapeDtypeStruct((nbatch * ntile * tm, D_MODEL), F32),
        compiler_params=_cparams(2),
        name="rwkv_out",
    )(*([z] * RWKV_OUT_PIECES), *([h] * RWKV_OUT_PIECES), p["w_out"], p["final_gain"])


def _prompt_bucket():
    assert WINDOW == BLOCK
    rel = (np.arange(BLOCK)[:, None] - np.arange(BLOCK)[None, :]) % BLOCK
    return _t5_bucket_np(rel)


def _sample_bucket(keep, t_new, slot):
    t = (np.arange(SUBLANES) % t_new)[:, None]
    j = np.arange(SAMPLE_KEYS)[None, :]
    own = j - keep - slot * t_new
    rel = np.where(j < keep, keep + t - j, t - own)
    ok = (rel >= 0) & (rel < WINDOW) & ((j < keep) | ((own >= 0) & (own < t_new)))
    return np.where(ok, _t5_bucket_np(rel), -1).astype(np.int32)


def kernel(x_prompt, x_sample, cache_win_k, cache_win_v, state_wkv, state_shift, meta_tokens, rel_bias_table, norm_gain, final_gain, attn_w_in, attn_sinks, attn_w_out, rwkv_mu, rwkv_w_in, rwkv_w0, rwkv_w1, rwkv_w2, rwkv_a0, rwkv_a1, rwkv_a2, rwkv_k_k, rwkv_k_a, rwkv_r_k, rwkv_ln_gamma, rwkv_ln_beta, rwkv_w_out):
    nb, seq, _ = x_prompt.shape
    ns, t_new, _ = x_sample.shape
    keep = cache_win_k.shape[2]
    lp = seq + BLOCK
    nblk = lp // BLOCK
    row = lambda x: x.reshape(1, D_MODEL)

    w_in0 = attn_w_in[0].astype(BF16)
    w_out0 = attn_w_out[0].astype(BF16)
    gain0 = row(norm_gain[0])
    sinks = attn_sinks[0]
    rp = dict(gain=row(norm_gain[1]), mu=rwkv_mu[0], w_in=rwkv_w_in[0].astype(BF16),
              w0=row(rwkv_w0[0]), w1=rwkv_w1[0].astype(BF16), w2=rwkv_w2[0].astype(BF16),
              a0=row(rwkv_a0[0]), a1=rwkv_a1[0].astype(BF16), a2=rwkv_a2[0].astype(BF16),
              k_k=row(rwkv_k_k[0]), k_a=row(rwkv_k_a[0]), r_k=row(rwkv_r_k[0]), ln_g=row(rwkv_ln_gamma[0]),
              ln_b=row(rwkv_ln_beta[0]), w_out=rwkv_w_out[0].astype(BF16),
              final_gain=row(final_gain))

    bias_p, *bias_s = _bias_call(rel_bias_table, [_prompt_bucket()]
                                 + [_sample_bucket(keep, t_new, slot) for slot in range(2)])

    head = jnp.concatenate([jnp.zeros((PAD, D_MODEL), F32), meta_tokens.astype(F32)], axis=0)
    xp = x_prompt.reshape(nb * seq, D_MODEL)
    q, kv, g = _attn_proj_call(xp, head, gain0, w_in0, BF16, nb, lp // ATTN_PROJ_ROWS,
                               ATTN_PROJ_ROWS // BLOCK, BLOCK)
    h1 = _attn_prompt_call(sinks, q, kv, g, head, xp, bias_p, w_out0, nb, nblk)
    kv3 = kv.reshape(nb, lp, 2 * A_KV_WIDTH)[:, lp - WINDOW:, :]
    win_k_p = kv3[:, :, :A_KV_WIDTH].reshape(1, nb, WINDOW, A_KV_HEADS, A_HEAD_DIM)
    win_v_p = kv3[:, :, A_KV_WIDTH:].reshape(1, nb, WINDOW, A_KV_HEADS, A_HEAD_DIM)

    shift0 = jnp.zeros((nb, 1, D_MODEL), F32)
    r, k, v, g1, ld, a, xlast = _rwkv_proj_call(h1, shift0, rp, nb, lp // RWKV_PROJ_ROWS, RWKV_PROJ_ROWS, None)
    z, st = _wkv_chunk_call(r, k, v, a, ld, g1, rp, nb, lp)
    y_prompt = _rwkv_out_call(z, h1, rp, nb, seq // RWKV_OUT_ROWS, RWKV_OUT_ROWS,
                              lp * RWKV_OUT_PIECES // RWKV_OUT_ROWS, BLOCK * RWKV_OUT_PIECES // RWKV_OUT_ROWS)
    y_prompt = y_prompt.reshape(nb, seq, D_MODEL)
    st = st.reshape(nb, D_MODEL // LANES, 2, R_HEAD_DIM, 2, R_HEAD_DIM)
    st = jnp.stack([st[:, :, 0, :, 0, :], st[:, :, 1, :, 1, :]], axis=2)
    wkv_p = jnp.swapaxes(st, -1, -2).reshape(1, nb, R_HEADS, R_HEAD_DIM, R_HEAD_DIM)
    shift_p = xlast.reshape(1, nb, D_MODEL)

    xs = x_sample.reshape(ns * t_new, D_MODEL)
    qs, kvs, gs = _attn_proj_call(xs, None, gain0, w_in0, F32, 1, 1, 1, ns * t_new)
    ck = jnp.swapaxes(cache_win_k[0].reshape(ns, keep, A_KV_WIDTH), 1, 2)
    cv = jnp.swapaxes(cache_win_v[0].reshape(ns, keep, A_KV_WIDTH), 1, 2)
    h1s, nk, nv = _attn_sample_call(sinks, qs, kvs, gs, xs, ck, cv, bias_s[0], bias_s[1], w_out0, t_new)
    win_k_s = jnp.swapaxes(nk, 1, 2).reshape(1, ns, keep, A_KV_HEADS, A_HEAD_DIM)
    win_v_s = jnp.swapaxes(nv, 1, 2).reshape(1, ns, keep, A_KV_HEADS, A_HEAD_DIM)

    shift_rows = jnp.repeat(state_shift[0], t_new, axis=0)
    tms = 256
    rs, ks, vs, g1s, lds, as_, xns = _rwkv_proj_call(h1s, shift_rows, rp, 1, ns * t_new // tms, tms, t_new)
    zs, st_s = _wkv_lanes_call(rs, ks, vs, as_, lds, g1s, rp, jnp.transpose(state_wkv[0], (1, 2, 3, 0)), t_new)
    y_sample = _rwkv_out_call(zs, h1s, rp, 1, 1, ns * t_new, RWKV_OUT_PIECES, 0)
    y_sample = y_sample.reshape(ns, t_new, D_MODEL)
    wkv_s = jnp.transpose(st_s, (3, 0, 1, 2))[None]
    shift_s = xns.reshape(ns, t_new, D_MODEL)[:, t_new - 1][None]

    return (y_prompt, y_sample, win_k_p, win_v_p, wkv_p, shift_p, win_k_s, win_v_s, wkv_s, shift_s)
```

```python
import functools
import math

import numpy as np
import jax
import jax.numpy as jnp
from jax import lax
from jax.experimental import pallas as pl
from jax.experimental.pallas import tpu as pltpu

F32 = jnp.float32
BF16 = jnp.bfloat16

D_MODEL = 1024
N_META = 16
RMS_EPS = 1e-6
A_HEADS = 16
A_KV_HEADS = 4
A_HEAD_DIM = 64
A_WIDTH = A_HEADS * A_HEAD_DIM
A_KV_WIDTH = A_KV_HEADS * A_HEAD_DIM
WINDOW = 128
BLOCK = 128
N_BUCKETS = 32
MAX_DISTANCE = 128
R_HEAD_DIM = 64
R_HEADS = D_MODEL // R_HEAD_DIM
GN_EPS = 64e-5

LANES = 128
SUBLANES = 8
PAD = BLOCK - N_META
NEG = -1e30
WKV_CHUNK = 64
WKV_CHUNKS_PER_STEP = 3
ATTN_BLOCKS_PER_STEP = 3
ATTN_PROJ_ROWS = 384
RWKV_PROJ_ROWS = 528
RWKV_OUT_ROWS = 1024
VMEM_LIMIT = 56 * 1024 * 1024


def _cparams(n_axes):
    return pltpu.CompilerParams(dimension_semantics=("arbitrary",) * n_axes,
                                vmem_limit_bytes=VMEM_LIMIT)


def _rmsnorm(x, gain):
    return x * lax.rsqrt(jnp.mean(x * x, axis=-1, keepdims=True) + RMS_EPS) * gain


def _sigmoid(x):
    return 1.0 / (1.0 + jnp.exp(-x))


def _iota(shape, dim):
    return lax.broadcasted_iota(jnp.int32, shape, dim)


def _t5_bucket_np(rel):
    n = np.maximum(rel, 0)
    max_exact = N_BUCKETS // 2
    nf = np.maximum(n, max_exact).astype(np.float32)
    scale = np.float32(math.log(MAX_DISTANCE / max_exact))
    large = max_exact + (np.log(nf / np.float32(max_exact)) / scale
                         * np.float32(N_BUCKETS - max_exact)).astype(np.int32)
    large = np.minimum(large, N_BUCKETS - 1)
    return np.where(n < max_exact, n, large).astype(np.int32)


def _bias_kernel(table_ref, *refs):
    h = pl.program_id(0)
    n = len(refs) // 2
    for bucket_ref, out_ref in zip(refs[:n], refs[n:]):
        bk = bucket_ref[...]
        acc = jnp.full(bk.shape, NEG, F32)
        for b in range(N_BUCKETS):
            acc = jnp.where(bk == b, table_ref[b, h], acc)
        out_ref[0] = acc


def _bias_call(table, buckets_np):
    return pl.pallas_call(
        _bias_kernel,
        grid=(A_HEADS,),
        in_specs=[pl.BlockSpec(memory_space=pltpu.SMEM)]
                 + [pl.BlockSpec(bk.shape, lambda h: (0, 0)) for bk in buckets_np],
        out_specs=[pl.BlockSpec((1,) + bk.shape, lambda h: (h, 0, 0)) for bk in buckets_np],
        out_shape=[jax.ShapeDtypeStruct((A_HEADS,) + bk.shape, F32) for bk in buckets_np],
        compiler_params=_cparams(1),
        name="bias_expand",
    )(table, *(jnp.asarray(bk) for bk in buckets_np))


def _attn_proj_kernel(head_ref, *refs, n_piece):
    x_refs, (gain_ref, w_ref, q_ref, kv_ref, g_ref) = refs[:n_piece], refs[n_piece:]
    first = x_refs[0][...]
    if head_ref is not None:
        first = jnp.where(pl.program_id(1) == 0, head_ref[...], first)
    x = jnp.concatenate([first] + [ref[...] for ref in x_refs[1:]], axis=0)
    xn = _rmsnorm(x, gain_ref[...])
    proj = jnp.dot(xn.astype(BF16), w_ref[...], preferred_element_type=F32)
    q_ref[...] = (proj[:, :A_WIDTH] * (A_HEAD_DIM ** -0.5)).astype(q_ref.dtype)
    kv_ref[...] = proj[:, A_WIDTH:A_WIDTH + 2 * A_KV_WIDTH]
    g_ref[...] = proj[:, A_WIDTH + 2 * A_KV_WIDTH:]


def _attn_proj_call(x2d, head, gain, w_bf16, q_dtype, nbatch, ntile, n_piece, piece):
    tm = n_piece * piece
    wcols = w_bf16.shape[1]
    per_seq = x2d.shape[0] // (nbatch * piece)
    lead = 0 if head is None else 1
    dst = lambda b, i: (b * ntile + i, 0)
    xspec = [pl.BlockSpec((piece, D_MODEL),
                          functools.partial(lambda b, i, kk: (b * per_seq + jnp.maximum(i * n_piece + kk - lead, 0), 0),
                                            kk=kk))
             for kk in range(n_piece)]
    kern = functools.partial(_attn_proj_kernel, n_piece=n_piece)
    operands = [x2d] * n_piece + [gain, w_bf16]
    if head is None:
        kern = functools.partial(kern, None)
        head_spec = []
    else:
        head_spec = [pl.BlockSpec((piece, D_MODEL), lambda b, i: (0, 0))]
        operands = [head] + operands
    n = nbatch * ntile * tm
    return pl.pallas_call(
        kern,
        grid=(nbatch, ntile),
        in_specs=head_spec + xspec + [pl.BlockSpec((1, D_MODEL), lambda b, i: (0, 0)),
                                      pl.BlockSpec((D_MODEL, wcols), lambda b, i: (0, 0))],
        out_specs=[pl.BlockSpec((tm, A_WIDTH), dst),
                   pl.BlockSpec((tm, 2 * A_KV_WIDTH), dst),
                   pl.BlockSpec((tm, A_WIDTH), dst)],
        out_shape=[jax.ShapeDtypeStruct((n, A_WIDTH), q_dtype),
                   jax.ShapeDtypeStruct((n, 2 * A_KV_WIDTH), F32),
                   jax.ShapeDtypeStruct((n, A_WIDTH), F32)],
        compiler_params=_cparams(2),
        name="attn_proj",
    )(*operands)


def _padded_kv_tiles(kv, c):
    lo = _iota((1, LANES), 1) < A_HEAD_DIM
    j = c // 2
    out = []
    for base in (0, A_KV_WIDTH):
        t = kv[:, base + j * LANES: base + (j + 1) * LANES]
        tr = pltpu.roll(t, A_HEAD_DIM, 1)
        if c % 2 == 0:
            even, odd = jnp.where(lo, t, 0.0), jnp.where(lo, 0.0, tr)
        else:
            even, odd = jnp.where(lo, tr, 0.0), jnp.where(lo, 0.0, t)
        out += [even.astype(BF16), odd.astype(BF16)]
    return out


def _mm_nt(a, b):
    return lax.dot_general(a, b, (((1,), (1,)), ((), ())), preferred_element_type=F32)


def _attn_prompt_kernel(sinks_ref, q_ref, kvc_ref, kvp_ref, g_ref, head_ref, *refs):
    nb = ATTN_BLOCKS_PER_STEP
    x_refs, (bias_ref, wout_ref, out_ref, og_ref) = refs[:nb], refs[nb:]
    i = pl.program_id(1)
    stack = 2 * BLOCK
    row, col = _iota((stack, BLOCK), 0) & (BLOCK - 1), _iota((stack, BLOCK), 1)
    upper = _iota((stack, 1), 0) >= BLOCK
    own = col <= row
    chains = [(c, idx) for c in range(A_KV_HEADS) for idx in range(2)]
    n = range(len(chains))
    cur = [_padded_kv_tiles(kvp_ref[...], c) for c in range(A_KV_HEADS)]
    for j in range(nb):
        rows = slice(j * BLOCK, (j + 1) * BLOCK)
        kvalid = (i * nb + j - 1 + own.astype(jnp.int32)) * BLOCK + col >= PAD
        prev, cur = cur, [_padded_kv_tiles(kvc_ref[rows, :], c) for c in range(A_KV_HEADS)]
        s, sink = [], []
        for c, idx in chains:
            q2 = q_ref[rows, 2 * c * LANES:(2 * c + 2) * LANES]
            q2 = jnp.concatenate([q2[:, :LANES], q2[:, LANES:]], axis=0)
            sc = jnp.where(own, _mm_nt(q2, cur[c][idx]), _mm_nt(q2, prev[c][idx]))
            bias = jnp.concatenate([bias_ref[4 * c + idx], bias_ref[4 * c + 2 + idx]], axis=0)
            s.append(jnp.where(kvalid, sc + bias, NEG))
            sink.append(jnp.where(upper, sinks_ref[4 * c + 2 + idx], sinks_ref[4 * c + idx]))
        m = [jnp.maximum(jnp.max(s[t], axis=1, keepdims=True), sink[t]) for t in n]
        p = [jnp.exp(s[t] - m[t]) for t in n]
        den = [jnp.sum(p[t], axis=1, keepdims=True) + jnp.exp(sink[t] - m[t]) for t in n]
        o = []
        for t, (c, idx) in enumerate(chains):
            pv = (jnp.dot(jnp.where(own, p[t], 0.0).astype(BF16), cur[c][2 + idx], preferred_element_type=F32)
                  + jnp.dot(jnp.where(own, 0.0, p[t]).astype(BF16), prev[c][2 + idx], preferred_element_type=F32))
            o.append(pv * (1.0 / den[t]))
        for c in range(A_KV_HEADS):
            both = o[2 * c] + o[2 * c + 1]
            for half in range(2):
                sl = slice((2 * c + half) * LANES, (2 * c + half + 1) * LANES)
                gt = g_ref[rows, sl]
                og_ref[rows, sl] = (both[half * BLOCK:(half + 1) * BLOCK] * (gt * _sigmoid(gt))).astype(BF16)
    resid = jnp.concatenate([jnp.where(i == 0, head_ref[...], x_refs[0][...])]
                            + [ref[...] for ref in x_refs[1:]], axis=0)
    out_ref[...] = resid + jnp.dot(og_ref[...], wout_ref[...], preferred_element_type=F32)


def _attn_prompt_call(sinks, q, kv, g, head, x2d, bias, wout_bf16, nbatch, nblk):
    n = q.shape[0]
    nb = ATTN_BLOCKS_PER_STEP
    nstep = nblk // nb
    rows = nb * BLOCK
    row = lambda b, i: (b * nstep + i, 0)
    prev = lambda b, i: (b * nblk + jnp.maximum(i * nb - 1, 0), 0)
    xrow = [functools.partial(lambda b, i, j: (b * (nblk - 1) + jnp.maximum(i * nb + j - 1, 0), 0), j=j)
            for j in range(nb)]
    return pl.pallas_call(
        _attn_prompt_kernel,
        grid=(nbatch, nstep),
        in_specs=[pl.BlockSpec(memory_space=pltpu.SMEM),
                  pl.BlockSpec((rows, A_WIDTH), row),
                  pl.BlockSpec((rows, 2 * A_KV_WIDTH), row),
                  pl.BlockSpec((BLOCK, 2 * A_KV_WIDTH), prev),
                  pl.BlockSpec((rows, A_WIDTH), row),
                  pl.BlockSpec((BLOCK, D_MODEL), lambda b, i: (0, 0))]
                 + [pl.BlockSpec((BLOCK, D_MODEL), xrow[j]) for j in range(nb)]
                 + [pl.BlockSpec((A_HEADS, BLOCK, BLOCK), lambda b, i: (0, 0, 0)),
                    pl.BlockSpec((A_WIDTH, D_MODEL), lambda b, i: (0, 0))],
        out_specs=pl.BlockSpec((rows, D_MODEL), row),
        out_shape=jax.ShapeDtypeStruct((n, D_MODEL), F32),
        scratch_shapes=[pltpu.VMEM((rows, A_WIDTH), BF16)],
        compiler_params=_cparams(2),
        name="attn_prompt",
    )(sinks, q, kv, kv, g, head, *([x2d] * nb), bias, wout_bf16)


SAMPLE_SB = 8
SAMPLE_KEYS = 2 * BLOCK


def _attn_sample_kernel(sinks_ref, q_ref, kvn_ref, g_ref, h_ref, ck_ref, cv_ref, bias0_ref, bias1_ref,
                        wout_ref, out_ref, nk_ref, nv_ref, og_ref, *, t_new):
    keep = ck_ref.shape[2]
    lo = _iota((1, LANES), 1) < A_HEAD_DIM
    stack = 8 * SUBLANES
    own = (_iota((stack, 1), 0) & (SUBLANES - 1)) // t_new
    piece = _iota((stack, 1), 0) // SUBLANES
    bias_refs = (bias0_ref, bias1_ref)
    n_tile = A_KV_WIDTH // LANES
    pair_ids = range(SAMPLE_SB * t_new // SUBLANES)

    bias_c, bias_n, sink = [], [], []
    for j in range(n_tile):
        heads = slice(8 * j, 8 * j + 8)
        bias_c.append(bias0_ref[heads, :, :keep].reshape(stack, keep))
        bias_n.append([ref[heads, :, keep:keep + SUBLANES].reshape(stack, SUBLANES) for ref in bias_refs])
        col = jnp.zeros((stack, 1), F32)
        for gq in range(8):
            col = jnp.where(piece == gq, sinks_ref[8 * j + gq], col)
        sink.append(col)

    chains = [(p, j, s) for p in pair_ids for j in range(n_tile) for s in range(SUBLANES // t_new)]
    qs = {}
    for p in pair_ids:
        rows = slice(p * SUBLANES, (p + 1) * SUBLANES)
        for j in range(n_tile):
            parts = []
            for gq in range(8):
                t = q_ref[rows, (4 * j + gq // 2) * LANES:(4 * j + gq // 2 + 1) * LANES]
                want_lo = gq < 4
                if (gq % 2 == 0) != want_lo:
                    t = pltpu.roll(t, A_HEAD_DIM, 1)
                parts.append(jnp.where(lo, t, 0.0) if want_lo else jnp.where(lo, 0.0, t))
            qs[p, j] = jnp.concatenate(parts, axis=0).astype(BF16)

    def kv_tiles(p, j, s, base):
        seq = p * (SUBLANES // t_new) + s
        cache = (ck_ref if base == 0 else cv_ref)[seq][j * LANES:(j + 1) * LANES, :].astype(BF16)
        new = kvn_ref[p * SUBLANES:(p + 1) * SUBLANES, base + j * LANES:base + (j + 1) * LANES].astype(BF16)
        return cache, new

    sc, sn = [], []
    for p, j, s in chains:
        kc, kn = kv_tiles(p, j, s, 0)
        sc.append(jnp.dot(qs[p, j], kc, preferred_element_type=F32) + bias_c[j])
        sn.append(_mm_nt(qs[p, j], kn) + bias_n[j][s])
    mx = [jnp.maximum(jnp.maximum(jnp.max(sc[i], axis=1, keepdims=True),
                                  jnp.max(sn[i], axis=1, keepdims=True)), sink[chains[i][1]])
          for i in range(len(chains))]
    pc = [jnp.exp(sc[i] - mx[i]) for i in range(len(chains))]
    pn = [jnp.exp(sn[i] - mx[i]) for i in range(len(chains))]
    den = [jnp.sum(pc[i], axis=1, keepdims=True) + jnp.sum(pn[i], axis=1, keepdims=True)
           + jnp.exp(sink[chains[i][1]] - mx[i]) for i in range(len(chains))]
    outs = {}
    for i, (p, j, s) in enumerate(chains):
        vc, vn = kv_tiles(p, j, s, A_KV_WIDTH)
        o = (_mm_nt(pc[i].astype(BF16), vc)
             + jnp.dot(pn[i].astype(BF16), vn, preferred_element_type=F32)) * (1.0 / den[i])
        outs[p, j] = o if s == 0 else jnp.where(own == s, o, outs[p, j])
    for p in pair_ids:
        rows = slice(p * SUBLANES, (p + 1) * SUBLANES)
        for j in range(n_tile):
            o = outs[p, j]
            for gg in range(4):
                even = o[2 * gg * SUBLANES:(2 * gg + 1) * SUBLANES]
                odd = o[(2 * gg + 1) * SUBLANES:(2 * gg + 2) * SUBLANES]
                if gg < 2:
                    tile = jnp.where(lo, even, pltpu.roll(odd, A_HEAD_DIM, 1))
                else:
                    tile = jnp.where(lo, pltpu.roll(even, A_HEAD_DIM, 1), odd)
                sl = slice((4 * j + gg) * LANES, (4 * j + gg + 1) * LANES)
                gt = g_ref[rows, sl]
                og_ref[rows, sl] = tile * (gt * _sigmoid(gt))
    lane_pos = _iota((SUBLANES, keep), 1)
    old = _iota((1, keep), 1) < keep - t_new
    for p in pair_ids:
        new8 = kvn_ref[p * SUBLANES:(p + 1) * SUBLANES, :]
        for s in range(SUBLANES // t_new):
            seq = p * (SUBLANES // t_new) + s
            sel = (lane_pos == _iota((SUBLANES, keep), 0) + (keep - t_new - s * t_new)).astype(BF16)
            placed = _mm2(_split(new8), sel, TN)
            for ref_in, ref_out, base in ((ck_ref, nk_ref, 0), (cv_ref, nv_ref, A_KV_WIDTH)):
                shifted = pltpu.roll(ref_in[seq], keep - t_new, 1)
                ref_out[seq] = jnp.where(old, shifted, placed[base:base + A_KV_WIDTH])
    out_ref[...] = h_ref[...] + jnp.dot(og_ref[...].astype(BF16), wout_ref[...],
                                        preferred_element_type=F32)


def _attn_sample_call(sinks, q, kv, g, h, cache_k, cache_v, bias0, bias1, wout_bf16, t_new):
    nseq, keep = cache_k.shape[0], cache_k.shape[2]
    rows = SAMPLE_SB * t_new
    row = lambda i: (i, 0)
    cspec = pl.BlockSpec((SAMPLE_SB, A_KV_WIDTH, keep), lambda i: (i, 0, 0))
    bspec = pl.BlockSpec((A_HEADS, SUBLANES, SAMPLE_KEYS), lambda i: (0, 0, 0))
    return pl.pallas_call(
        functools.partial(_attn_sample_kernel, t_new=t_new),
        grid=(nseq // SAMPLE_SB,),
        in_specs=[pl.BlockSpec(memory_space=pltpu.SMEM),
                  pl.BlockSpec((rows, A_WIDTH), row),
                  pl.BlockSpec((rows, 2 * A_KV_WIDTH), row),
                  pl.BlockSpec((rows, A_WIDTH), row),
                  pl.BlockSpec((rows, D_MODEL), row),
                  cspec, cspec, bspec, bspec,
                  pl.BlockSpec((A_WIDTH, D_MODEL), lambda i: (0, 0))],
        out_specs=[pl.BlockSpec((rows, D_MODEL), row), cspec, cspec],
        out_shape=[jax.ShapeDtypeStruct((nseq * t_new, D_MODEL), F32),
                   jax.ShapeDtypeStruct(cache_k.shape, F32),
                   jax.ShapeDtypeStruct(cache_v.shape, F32)],
        scratch_shapes=[pltpu.VMEM((rows, A_WIDTH), F32)],
        compiler_params=_cparams(1),
        name="attn_sample",
    )(sinks, q, kv, g, h, cache_k, cache_v, bias0, bias1, wout_bf16)


def _rwkv_proj_kernel(h_ref, shift_ref, gain_ref, mu_ref, win_ref, w0_ref, w1_ref, w2_ref,
                      a0_ref, a1_ref, a2_ref,
                      r_ref, k_ref, v_ref, g_ref, ld_ref, a_ref, xn_ref, *scratch, seq_len):
    xn = _rmsnorm(h_ref[...], gain_ref[...])
    tm = xn.shape[0]
    rolled = pltpu.roll(xn, 1, 0)
    row = _iota((tm, 1), 0)
    if seq_len is None:
        carry_ref, = scratch

        @pl.when(pl.program_id(1) == 0)
        def _():
            carry_ref[...] = shift_ref[0]

        xprev = jnp.where(row == 0, carry_ref[...], rolled)
        carry_ref[...] = xn[tm - 1:tm, :]
        xn_ref[0] = xn[tm - 1:tm, :]
    else:
        xprev = jnp.where(row % seq_len == 0, shift_ref[...], rolled)
        xn_ref[...] = xn
    dx = xprev - xn

    def mix(c):
        return (xn + dx * mu_ref[c:c + 1, :]).astype(BF16)

    for c, o_ref in enumerate((r_ref, k_ref, v_ref, g_ref)):
        o_ref[...] = jnp.dot(mix(c), win_ref[c], preferred_element_type=F32)
    lw = jnp.tanh(jnp.dot(mix(4), w1_ref[...], preferred_element_type=F32))
    z = w0_ref[...] + jnp.dot(lw.astype(BF16), w2_ref[...], preferred_element_type=F32)
    ld_ref[...] = -math.exp(-0.5) * _sigmoid(z)
    la = jnp.dot(mix(5), a1_ref[...], preferred_element_type=F32)
    a_ref[...] = _sigmoid(a0_ref[...] + jnp.dot(la.astype(BF16), a2_ref[...],
                                                preferred_element_type=F32))


def _rwkv_proj_call(h, shift, p, nbatch, ntile, tm, seq_len):
    n = h.shape[0]
    row = lambda b, i: (b * ntile + i, 0)
    full2 = lambda b, i: (0, 0)
    if seq_len is None:
        shift_spec = pl.BlockSpec((1, 1, D_MODEL), lambda b, i: (b, 0, 0))
        xn_spec = pl.BlockSpec((1, 1, D_MODEL), lambda b, i: (b, 0, 0))
        xn_shape = jax.ShapeDtypeStruct((nbatch, 1, D_MODEL), F32)
        scratch = [pltpu.VMEM((1, D_MODEL), F32)]
    else:
        shift_spec = pl.BlockSpec((tm, D_MODEL), row)
        xn_spec = pl.BlockSpec((tm, D_MODEL), row)
        xn_shape = jax.ShapeDtypeStruct((n, D_MODEL), F32)
        scratch = []
    lora = p["w1"].shape[1]
    big = jax.ShapeDtypeStruct((n, D_MODEL), F32)
    return pl.pallas_call(
        functools.partial(_rwkv_proj_kernel, seq_len=seq_len),
        grid=(nbatch, ntile),
        in_specs=[pl.BlockSpec((tm, D_MODEL), row),
                  shift_spec,
                  pl.BlockSpec((1, D_MODEL), full2),
                  pl.BlockSpec(p["mu"].shape, full2),
                  pl.BlockSpec(p["w_in"].shape, lambda b, i: (0, 0, 0)),
                  pl.BlockSpec((1, D_MODEL), full2),
                  pl.BlockSpec((D_MODEL, lora), full2),
                  pl.BlockSpec((lora, D_MODEL), full2),
                  pl.BlockSpec((1, D_MODEL), full2),
                  pl.BlockSpec((D_MODEL, lora), full2),
                  pl.BlockSpec((lora, D_MODEL), full2)],
        out_specs=[pl.BlockSpec((tm, D_MODEL), row)] * 6 + [xn_spec],
        out_shape=[big] * 6 + [xn_shape],
        scratch_shapes=scratch,
        compiler_params=_cparams(2),
        name="rwkv_proj",
    )(h, shift, p["gain"], p["mu"], p["w_in"], p["w0"], p["w1"], p["w2"], p["a0"], p["a1"], p["a2"])


NN = (((1,), (0,)), ((), ()))
NT = (((1,), (1,)), ((), ()))
TN = (((0,), (0,)), ((), ()))


def _split(x):
    hi = x.astype(BF16)
    return hi, (x - hi.astype(F32)).astype(BF16)


def _mm(a, b, dn):
    return lax.dot_general(a, b, dn, preferred_element_type=F32)


def _mm2(a, b, dn):
    ah, al = a
    if dn == TN:
        both = _mm(jnp.concatenate([ah, al], axis=1), b, dn)
        m = ah.shape[1]
    else:
        both = _mm(jnp.concatenate([ah, al], axis=0), b, dn)
        m = ah.shape[0]
    return both[:m] + both[m:]


def _seg_sums(xs, ones_bf16, exact):
    rows = xs[0].shape[0]
    stacked = jnp.concatenate(xs, axis=0)
    if exact:
        hi, lo = _split(stacked)
        out = _mm(lo, ones_bf16, NN) + _mm(hi, ones_bf16, NN)
    else:
        out = _mm(stacked.astype(BF16), ones_bf16, NN)
    return [out[i * rows:(i + 1) * rows] for i in range(len(xs))]


def _lane_lo():
    return _iota((1, LANES), 1) < R_HEAD_DIM


def _same_half():
    return (_iota((LANES, LANES), 0) < R_HEAD_DIM) == (_iota((LANES, LANES), 1) < R_HEAD_DIM)


def _ones_blk():
    return _same_half().astype(BF16)


def _bd(z):
    lo = _lane_lo()
    return jnp.concatenate([jnp.where(lo, z, 0.0), jnp.where(lo, 0.0, z)], axis=0)


def _bd_swap(z):
    lo = _lane_lo()
    return jnp.concatenate([jnp.where(lo, 0.0, z), jnp.where(lo, z, 0.0)], axis=0)


def _wkv_batch_stage(r, k, v, a, ld, kkg, kag, fillers=(), early=None):
    c = r[0].shape[0]
    pairs = range(len(r))
    half = R_HEAD_DIM
    lo = _lane_lo()
    t_row, t_col = _iota((c, LANES), 0), _iota((c, LANES), 1) & (half - 1)
    strict, incl = t_row > t_col, t_row >= t_col
    tri = (_iota((c, c), 0) >= _iota((c, c), 1)).astype(BF16)
    ones_blk = _ones_blk()
    eye_pair = (_iota((c, LANES), 0) == (_iota((c, LANES), 1) & (half - 1))).astype(F32)

    kkx = [k[j] * kkg[j] for j in pairs]
    ssq = _seg_sums([kkx[j] * kkx[j] for j in pairs], ones_blk, exact=True)
    ld_hi, ld_lo = _split(jnp.concatenate(ld, axis=1))
    cs_all = _mm(tri, ld_lo, NN) + _mm(tri, ld_hi, NN)
    cs = [cs_all[:, j * LANES:(j + 1) * LANES] for j in pairs]
    tot = [cs[j][c - 1:c, :] for j in pairs]
    x_hi, kh_all, bke, vbd_s, lk_a, lk_r, lb_a, lb_r = ([] for _ in range(8))
    for j in pairs:
        kk = kkx[j] * lax.rsqrt(jnp.maximum(ssq[j], 1e-24))
        kh = k[j] * (1.0 + (a[j] - 1.0) * kag[j])
        bv = kk * a[j]
        e_neg = jnp.exp(-cs[j])
        e_end = e_neg * jnp.exp(tot[j])
        at, rt = -kk * jnp.exp(cs[j] - ld[j]), r[j] * jnp.exp(cs[j])
        x_hi.append(jnp.concatenate([at.astype(BF16), rt.astype(BF16)], axis=0))
        kh_all.append(kh)
        y_hi = jnp.concatenate([(bv * e_neg).astype(BF16), (kh * e_neg).astype(BF16)], axis=0)
        bke.append(jnp.concatenate([(bv * e_end).astype(BF16), (kh * e_end).astype(BF16)], axis=0))
        vbd_s.append(_bd(v[j].astype(BF16)))
        both = _mm(jnp.concatenate([jnp.where(lo, x_hi[j], 0.0), jnp.where(lo, 0.0, x_hi[j])], axis=0), y_hi, NT)
        ga_a = pltpu.roll(both[:c], half, 1)
        ga_r = pltpu.roll(both[c:2 * c], half, 1)
        gb_a = both[2 * c:3 * c]
        gb_r = both[3 * c:4 * c]
        lk_a.append(jnp.where(strict, jnp.where(lo, ga_a, gb_a), 0.0))
        lk_r.append(jnp.where(incl, jnp.where(lo, ga_r, gb_r), 0.0))
        lb_a.append(jnp.where(strict, jnp.where(lo, gb_a, ga_a), 0.0))
        lb_r.append(jnp.where(incl, jnp.where(lo, gb_r, ga_r), 0.0))
    fillers = list(fillers)
    if early is not None:
        early.update(x_hi=x_hi)
    pw = lb_a
    acc = [eye_pair + pw[j] for j in pairs]
    pw = [_mm(pw[j].astype(BF16), _bd(pw[j].astype(BF16)), NN) for j in pairs]
    from_v = []
    for j in pairs:
        both = _mm(jnp.concatenate([lk_a[j].astype(BF16), lk_r[j].astype(BF16)], axis=0), vbd_s[j], NN)
        from_v.append((both[:c], both[c:]))
    for _ in range(int(math.log2(c)) - 2):
        both = [_mm(jnp.concatenate([pw[j], acc[j]], axis=0).astype(BF16), _bd(pw[j].astype(BF16)), NN)
                for j in pairs]
        if fillers:
            fillers.pop(0)()
        pw = [both[j][:c] for j in pairs]
        acc = [acc[j] + both[j][c:] for j in pairs]
    tinv = [acc[j] + _mm(acc[j].astype(BF16), _bd(pw[j].astype(BF16)), NN) for j in pairs]
    for fill in fillers:
        fill()
    return dict(x_hi=x_hi, kh=kh_all, lb_r=lb_r, tinv=tinv, from_v=from_v, bke=bke,
                tot=tot, ones_blk=ones_blk)


def _wkv_finish(stage, from_state_a, from_state_r):
    pairs = range(len(from_state_a))
    u = [_mm(stage["tinv"][j].astype(BF16),
             _bd_swap((from_state_a[j] + stage["from_v"][j][0]).astype(BF16)), NN) for j in pairs]
    y = [from_state_r[j] + stage["from_v"][j][1]
         + _mm(stage["lb_r"][j].astype(BF16), _bd_swap(u[j].astype(BF16)), NN) for j in pairs]
    return u, y


def _wkv_gate(y, r, kh, v, g, rk, lng, lnb, ones_blk):
    n = len(y)
    pairs = range(n)
    inv_n = 1.0 / R_HEAD_DIM
    sums = _seg_sums([r[j] * kh[j] * rk[j] for j in pairs] + list(y), ones_blk, exact=False)
    rkk, mean = sums[:n], [s * inv_n for s in sums[n:]]
    d = [y[j] - mean[j] for j in pairs]
    var = [s * inv_n for s in _seg_sums([d[j] * d[j] for j in pairs], ones_blk, exact=False)]
    return [((d[j] * lax.rsqrt(var[j] + GN_EPS) * lng[j] + lnb[j] + rkk[j] * v[j])
             * (g[j] * _sigmoid(g[j]))).astype(BF16) for j in pairs]


def _wkv_chunk_kernel(r_ref, k_ref, v_ref, a_ref, ld_ref, g_ref, kkg_ref, kag_ref, rk_ref, lng_ref, lnb_ref,
                      z_ref, sout_ref, st_ref):
    c = WKV_CHUNK
    n_sub = r_ref.shape[0] // c
    npair = r_ref.shape[1] // LANES
    pairs = range(npair)
    items = [(s, j) for s in range(n_sub) for j in pairs]

    @pl.when(pl.program_id(2) == 0)
    def _():
        st_ref[...] = jnp.zeros_like(st_ref)

    def tile(ref, it):
        s, j = it
        return ref[s * c:(s + 1) * c, j * LANES:(j + 1) * LANES]

    def par(ref):
        return [ref[:, j * LANES:(j + 1) * LANES] for _, j in items]

    r, k, v, a, ld, g = ([tile(ref, it) for it in items] for ref in (r_ref, k_ref, v_ref, a_ref, ld_ref, g_ref))
    st = [st_ref[j] for j in pairs]
    from_state = {}
    stage = {}

    def state_products(group):
        def run():
            for j in group:
                from_state[j] = _mm(stage["x_hi"][j], st[j].astype(BF16), NN)
        return run

    n_fill = 4
    groups = [list(pairs)[i::n_fill] for i in range(n_fill)]
    stage.update(_wkv_batch_stage(r, k, v, a, ld, par(kkg_ref), par(kag_ref),
                                  fillers=[state_products(grp) for grp in groups if grp], early=stage))
    same_half = _same_half()
    eye_full = _iota((LANES, LANES), 0) == _iota((LANES, LANES), 1)
    ys = []
    for s in range(n_sub):
        idx = [s * npair + j for j in pairs]
        if s > 0:
            for j in pairs:
                from_state[j] = _mm(stage["x_hi"][idx[j]], st[j].astype(BF16), NN)
        sub = {key: [stage[key][i] for i in idx] for key in ("tinv", "from_v", "lb_r")}
        us, ys_s = _wkv_finish(sub, [from_state[j][:c] for j in pairs], [from_state[j][c:] for j in pairs])
        upd = [_mm(stage["bke"][idx[j]], jnp.concatenate([us[j], v[idx[j]]], axis=0).astype(BF16), TN)
               for j in pairs]
        ys += ys_s
        for j in pairs:
            w_col = jnp.sum(jnp.where(eye_full, jnp.exp(stage["tot"][idx[j]]), 0.0), axis=1, keepdims=True)
            st[j] = w_col * st[j] + jnp.where(same_half, upd[j], 0.0)
    zs = _wkv_gate(ys, r, stage["kh"], v, g, par(rk_ref), par(lng_ref), par(lnb_ref), stage["ones_blk"])
    for i, (s, j) in enumerate(items):
        z_ref[s * c:(s + 1) * c, j * LANES:(j + 1) * LANES] = zs[i]
    for j in pairs:
        st_ref[j] = st[j]
        sout_ref[0, j] = st[j]


def _wkv_chunk_call(r, k, v, a, ld, g, p, nbatch, seq):
    rows = WKV_CHUNK * WKV_CHUNKS_PER_STEP
    nstep = seq // rows
    npair = D_MODEL // LANES
    tile = pl.BlockSpec((rows, D_MODEL), lambda b, j, t: (b * nstep + t, 0))
    par = pl.BlockSpec((1, D_MODEL), lambda b, j, t: (0, 0))
    return pl.pallas_call(
        _wkv_chunk_kernel,
        grid=(nbatch, 1, nstep),
        in_specs=[tile] * 6 + [par] * 5,
        out_specs=[tile, pl.BlockSpec((1, npair, LANES, LANES), lambda b, j, t: (b, 0, 0, 0))],
        out_shape=[jax.ShapeDtypeStruct((nbatch * seq, D_MODEL), BF16),
                   jax.ShapeDtypeStruct((nbatch, npair, LANES, LANES), F32)],
        scratch_shapes=[pltpu.VMEM((npair, LANES, LANES), F32)],
        compiler_params=_cparams(3),
        name="wkv_chunk",
    )(r, k, v, a, ld, g, p["k_k"], p["k_a"], p["r_k"], p["ln_g"], p["ln_b"])


WKV_LANES_UNROLL = 4


def _wkv_lanes_kernel(r_ref, k_ref, v_ref, a_ref, ld_ref, g_ref, kkg_ref, kag_ref, rk_ref, lng_ref, lnb_ref,
                      s_ref, z_ref, sout_ref, prep_ref, y_ref, *, seq_len):
    n = R_HEAD_DIM
    nseq = s_ref.shape[3]
    eye = _iota((LANES, LANES), 0) == _iota((LANES, LANES), 1)

    def column(ref):
        return jnp.sum(jnp.where(eye, ref[...], 0.0), axis=1, keepdims=True)

    kkg, kag, rk, lng, lnb = (column(ref) for ref in (kkg_ref, kag_ref, rk_ref, lng_ref, lnb_ref))

    def token(ref, t):
        return ref[pl.ds(t, nseq, stride=seq_len), :].T

    bonus = []
    for t in range(seq_len):
        r, k, v, a = (token(ref, t) for ref in (r_ref, k_ref, v_ref, a_ref))
        w = jnp.exp(token(ld_ref, t))
        kkx = k * kkg
        kh = k * (1.0 + (a - 1.0) * kag)
        rkk = r * kh * rk
        tiles = []
        for hh in range(2):
            rows = slice(hh * n, (hh + 1) * n)
            nrm = jnp.sqrt(jnp.sum(kkx[rows] * kkx[rows], axis=0, keepdims=True))
            kk = kkx[rows] / jnp.maximum(nrm, 1e-12)
            for q, val in enumerate((w[rows], -kk, kk * a[rows], kh[rows], r[rows], v[rows])):
                prep_ref[q, t, hh] = val
            tiles.append(jnp.sum(rkk[rows], axis=0, keepdims=True) * v[rows])
        bonus.append(tiles)

    for hh in range(2):
        def advance(i, carry, hh=hh):
            for u in range(WKV_LANES_UNROLL):
                vi = i * WKV_LANES_UNROLL + u
                slab = s_ref[hh, vi]
                for t in range(seq_len):
                    w, av, bv, kh, r = (prep_ref[q, t, hh] for q in range(5))
                    vrow = prep_ref[5, t, hh, pl.ds(vi, 1), :]
                    sa = jnp.sum(slab * av, axis=0, keepdims=True)
                    slab = slab * w + sa * bv + vrow * kh
                    y_ref[t, hh, pl.ds(vi, 1), :] = jnp.sum(slab * r, axis=0, keepdims=True)
                sout_ref[hh, vi] = slab
            return carry

        lax.fori_loop(0, n // WKV_LANES_UNROLL, advance, 0)

    for t in range(seq_len):
        parts = []
        for hh in range(2):
            rows = slice(hh * n, (hh + 1) * n)
            y = y_ref[t, hh]
            d = y - jnp.mean(y, axis=0, keepdims=True)
            var = jnp.mean(d * d, axis=0, keepdims=True)
            parts.append(d * lax.rsqrt(var + GN_EPS) * lng[rows] + lnb[rows] + bonus[t][hh])
        g = token(g_ref, t)
        z = jnp.concatenate(parts, axis=0) * (g * _sigmoid(g))
        z_ref[pl.ds(t, nseq, stride=seq_len), :] = z.T


def _wkv_lanes_call(r, k, v, a, ld, g, p, state_hvkb, seq_len):
    n = r.shape[0]
    nseq = state_hvkb.shape[3]
    tile = pl.BlockSpec((n, LANES), lambda j: (0, j))
    par = pl.BlockSpec((1, LANES), lambda j: (0, j))
    sspec = pl.BlockSpec((2, R_HEAD_DIM, R_HEAD_DIM, nseq), lambda j: (j, 0, 0, 0))
    return pl.pallas_call(
        functools.partial(_wkv_lanes_kernel, seq_len=seq_len),
        grid=(D_MODEL // LANES,),
        in_specs=[tile] * 6 + [par] * 5 + [sspec],
        out_specs=[tile, sspec],
        out_shape=[jax.ShapeDtypeStruct((n, D_MODEL), F32),
                   jax.ShapeDtypeStruct(state_hvkb.shape, F32)],
        scratch_shapes=[pltpu.VMEM((6, seq_len, 2, R_HEAD_DIM, nseq), F32),
                        pltpu.VMEM((seq_len, 2, R_HEAD_DIM, nseq), F32)],
        compiler_params=_cparams(1),
        name="wkv_lanes",
    )(r, k, v, a, ld, g, p["k_k"], p["k_a"], p["r_k"], p["ln_g"], p["ln_b"], state_hvkb)


RWKV_OUT_PIECES = 8


def _rwkv_out_kernel(*refs):
    n = RWKV_OUT_PIECES
    z_refs, h_refs, (wout_ref, fg_ref, out_ref) = refs[:n], refs[n:2 * n], refs[2 * n:]
    z = jnp.concatenate([ref[...].astype(BF16) for ref in z_refs], axis=0)
    h = jnp.concatenate([ref[...] for ref in h_refs], axis=0)
    h2 = h + jnp.dot(z, wout_ref[...], preferred_element_type=F32)
    out_ref[...] = _rmsnorm(h2, fg_ref[...])


def _rwkv_out_call(z, h, p, nbatch, ntile, tm, pieces_per_batch, skip):
    piece = tm // RWKV_OUT_PIECES
    dst = lambda b, i: (b * ntile + i, 0)
    pspec = [pl.BlockSpec((piece, D_MODEL),
                          functools.partial(lambda b, i, kk: (b * pieces_per_batch + i * RWKV_OUT_PIECES + skip + kk, 0),
                                            kk=kk))
             for kk in range(RWKV_OUT_PIECES)]
    return pl.pallas_call(
        _rwkv_out_kernel,
        grid=(nbatch, ntile),
        in_specs=pspec + pspec + [pl.BlockSpec((D_MODEL, D_MODEL), lambda b, i: (0, 0)),
                                  pl.BlockSpec((1, D_MODEL), lambda b, i: (0, 0))],
        out_specs=pl.BlockSpec((tm, D_MODEL), dst),
        out_shape=jax.ShapeDtypeStruct((nbatch * ntile * tm, D_MODEL), F32),
        compiler_params=_cparams(2),
        name="rwkv_out",
    )(*([z] * RWKV_OUT_PIECES), *([h] * RWKV_OUT_PIECES), p["w_out"], p["final_gain"])


def _prompt_bucket():
    assert WINDOW == BLOCK
    rel = (np.arange(BLOCK)[:, None] - np.arange(BLOCK)[None, :]) % BLOCK
    return _t5_bucket_np(rel)


def _sample_bucket(keep, t_new, slot):
    t = (np.arange(SUBLANES) % t_new)[:, None]
    j = np.arange(SAMPLE_KEYS)[None, :]
    own = j - keep - slot * t_new
    rel = np.where(j < keep, keep + t - j, t - own)
    ok = (rel >= 0) & (rel < WINDOW) & ((j < keep) | ((own >= 0) & (own < t_new)))
    return np.where(ok, _t5_bucket_np(rel), -1).astype(np.int32)


def kernel(x_prompt, x_sample, cache_win_k, cache_win_v, state_wkv, state_shift, meta_tokens, rel_bias_table, norm_gain, final_gain, attn_w_in, attn_sinks, attn_w_out, rwkv_mu, rwkv_w_in, rwkv_w0, rwkv_w1, rwkv_w2, rwkv_a0, rwkv_a1, rwkv_a2, rwkv_k_k, rwkv_k_a, rwkv_r_k, rwkv_ln_gamma, rwkv_ln_beta, rwkv_w_out):
    nb, seq, _ = x_prompt.shape
    ns, t_new, _ = x_sample.shape
    keep = cache_win_k.shape[2]
    lp = seq + BLOCK
    nblk = lp // BLOCK
    row = lambda x: x.reshape(1, D_MODEL)

    w_in0 = attn_w_in[0].astype(BF16)
    w_out0 = attn_w_out[0].astype(BF16)
    gain0 = row(norm_gain[0])
    sinks = attn_sinks[0]
    rp = dict(gain=row(norm_gain[1]), mu=rwkv_mu[0], w_in=rwkv_w_in[0].astype(BF16),
              w0=row(rwkv_w0[0]), w1=rwkv_w1[0].astype(BF16), w2=rwkv_w2[0].astype(BF16),
              a0=row(rwkv_a0[0]), a1=rwkv_a1[0].astype(BF16), a2=rwkv_a2[0].astype(BF16),
              k_k=row(rwkv_k_k[0]), k_a=row(rwkv_k_a[0]), r_k=row(rwkv_r_k[0]), ln_g=row(rwkv_ln_gamma[0]),
              ln_b=row(rwkv_ln_beta[0]), w_out=rwkv_w_out[0].astype(BF16),
              final_gain=row(final_gain))

    bias_p, *bias_s = _bias_call(rel_bias_table, [_prompt_bucket()]
                                 + [_sample_bucket(keep, t_new, slot) for slot in range(2)])

    head = jnp.concatenate([jnp.zeros((PAD, D_MODEL), F32), meta_tokens.astype(F32)], axis=0)
    xp = x_prompt.reshape(nb * seq, D_MODEL)
    q, kv, g = _attn_proj_call(xp, head, gain0, w_in0, BF16, nb, lp // ATTN_PROJ_ROWS,
                               ATTN_PROJ_ROWS // BLOCK, BLOCK)
    h1 = _attn_prompt_call(sinks, q, kv, g, head, xp, bias_p, w_out0, nb, nblk)
    kv3 = kv.reshape(nb, lp, 2 * A_KV_WIDTH)[:, lp - WINDOW:, :]
    win_k_p = kv3[:, :, :A_KV_WIDTH].reshape(1, nb, WINDOW, A_KV_HEADS, A_HEAD_DIM)
    win_v_p = kv3[:, :, A_KV_WIDTH:].reshape(1, nb, WINDOW, A_KV_HEADS, A_HEAD_DIM)

    shift0 = jnp.zeros((nb, 1, D_MODEL), F32)
    r, k, v, g1, ld, a, xlast = _rwkv_proj_call(h1, shift0, rp, nb, lp // RWKV_PROJ_ROWS, RWKV_PROJ_ROWS, None)
    z, st = _wkv_chunk_call(r, k, v, a, ld, g1, rp, nb, lp)
    y_prompt = _rwkv_out_call(z, h1, rp, nb, seq // RWKV_OUT_ROWS, RWKV_OUT_ROWS,
                              lp * RWKV_OUT_PIECES // RWKV_OUT_ROWS, BLOCK * RWKV_OUT_PIECES // RWKV_OUT_ROWS)
    y_prompt = y_prompt.reshape(nb, seq, D_MODEL)
    st = st.reshape(nb, D_MODEL // LANES, 2, R_HEAD_DIM, 2, R_HEAD_DIM)
    st = jnp.stack([st[:, :, 0, :, 0, :], st[:, :, 1, :, 1, :]], axis=2)
    wkv_p = jnp.swapaxes(st, -1, -2).reshape(1, nb, R_HEADS, R_HEAD_DIM, R_HEAD_DIM)
    shift_p = xlast.reshape(1, nb, D_MODEL)

    xs = x_sample.reshape(ns * t_new, D_MODEL)
    qs, kvs, gs = _attn_proj_call(xs, None, gain0, w_in0, F32, 1, 1, 1, ns * t_new)
    ck = jnp.swapaxes(cache_win_k[0].reshape(ns, keep, A_KV_WIDTH), 1, 2)
    cv = jnp.swapaxes(cache_win_v[0].reshape(ns, keep, A_KV_WIDTH), 1, 2)
    h1s, nk, nv = _attn_sample_call(sinks, qs, kvs, gs, xs, ck, cv, bias_s[0], bias_s[1], w_out0, t_new)
    win_k_s = jnp.swapaxes(nk, 1, 2).reshape(1, ns, keep, A_KV_HEADS, A_HEAD_DIM)
    win_v_s = jnp.swapaxes(nv, 1, 2).reshape(1, ns, keep, A_KV_HEADS, A_HEAD_DIM)

    shift_rows = jnp.repeat(state_shift[0], t_new, axis=0)
    tms = 256
    rs, ks, vs, g1s, lds, as_, xns = _rwkv_proj_call(h1s, shift_rows, rp, 1, ns * t_new // tms, tms, t_new)
    zs, st_s = _wkv_lanes_call(rs, ks, vs, as_, lds, g1s, rp, jnp.transpose(state_wkv[0], (1, 2, 3, 0)), t_new)
    y_sample = _rwkv_out_call(zs, h1s, rp, 1, 1, ns * t_new, RWKV_OUT_PIECES, 0)
    y_sample = y_sample.reshape(ns, t_new, D_MODEL)
    wkv_s = jnp.transpose(st_s, (3, 0, 1, 2))[None]
    shift_s = xns.reshape(ns, t_new, D_MODEL)[:, t_new - 1][None]

    return (y_prompt, y_sample, win_k_p, win_v_p, wkv_p, shift_p, win_k_s, win_v_s, wkv_s, shift_s)
```

```python
import functools
import math

import numpy as np
import jax
import jax.numpy as jnp
from jax import lax
from jax.experimental import pallas as pl
from jax.experimental.pallas import tpu as pltpu

F32 = jnp.float32
BF16 = jnp.bfloat16

D_MODEL = 1024
N_META = 16
RMS_EPS = 1e-6
A_HEADS = 16
A_KV_HEADS = 4
A_HEAD_DIM = 64
A_WIDTH = A_HEADS * A_HEAD_DIM
A_KV_WIDTH = A_KV_HEADS * A_HEAD_DIM
WINDOW = 128
BLOCK = 128
N_BUCKETS = 32
MAX_DISTANCE = 128
R_HEAD_DIM = 64
R_HEADS = D_MODEL // R_HEAD_DIM
GN_EPS = 64e-5

LANES = 128
SUBLANES = 8
PAD = BLOCK - N_META
NEG = -1e30
WKV_CHUNK = 64
WKV_CHUNKS_PER_STEP = 3
ATTN_BLOCKS_PER_STEP = 3
ATTN_PROJ_ROWS = 384
RWKV_PROJ_ROWS = 528
RWKV_OUT_ROWS = 1024
VMEM_LIMIT = 56 * 1024 * 1024


def _cparams(n_axes):
    return pltpu.CompilerParams(dimension_semantics=("arbitrary",) * n_axes,
                                vmem_limit_bytes=VMEM_LIMIT)


def _rmsnorm(x, gain):
    return x * lax.rsqrt(jnp.mean(x * x, axis=-1, keepdims=True) + RMS_EPS) * gain


def _sigmoid(x):
    return 1.0 / (1.0 + jnp.exp(-x))


def _iota(shape, dim):
    return lax.broadcasted_iota(jnp.int32, shape, dim)


def _t5_bucket_np(rel):
    n = np.maximum(rel, 0)
    max_exact = N_BUCKETS // 2
    nf = np.maximum(n, max_exact).astype(np.float32)
    scale = np.float32(math.log(MAX_DISTANCE / max_exact))
    large = max_exact + (np.log(nf / np.float32(max_exact)) / scale
                         * np.float32(N_BUCKETS - max_exact)).astype(np.int32)
    large = np.minimum(large, N_BUCKETS - 1)
    return np.where(n < max_exact, n, large).astype(np.int32)


def _bias_kernel(table_ref, *refs):
    h = pl.program_id(0)
    n = len(refs) // 2
    for bucket_ref, out_ref in zip(refs[:n], refs[n:]):
        bk = bucket_ref[...]
        acc = jnp.full(bk.shape, NEG, F32)
        for b in range(N_BUCKETS):
            acc = jnp.where(bk == b, table_ref[b, h], acc)
        out_ref[0] = acc


def _bias_call(table, buckets_np):
    return pl.pallas_call(
        _bias_kernel,
        grid=(A_HEADS,),
        in_specs=[pl.BlockSpec(memory_space=pltpu.SMEM)]
                 + [pl.BlockSpec(bk.shape, lambda h: (0, 0)) for bk in buckets_np],
        out_specs=[pl.BlockSpec((1,) + bk.shape, lambda h: (h, 0, 0)) for bk in buckets_np],
        out_shape=[jax.ShapeDtypeStruct((A_HEADS,) + bk.shape, F32) for bk in buckets_np],
        compiler_params=_cparams(1),
        name="bias_expand",
    )(table, *(jnp.asarray(bk) for bk in buckets_np))


def _attn_proj_kernel(head_ref, *refs, n_piece):
    x_refs, (gain_ref, w_ref, q_ref, kv_ref, g_ref) = refs[:n_piece], refs[n_piece:]
    first = x_refs[0][...]
    if head_ref is not None:
        first = jnp.where(pl.program_id(1) == 0, head_ref[...], first)
    x = jnp.concatenate([first] + [ref[...] for ref in x_refs[1:]], axis=0)
    xn = _rmsnorm(x, gain_ref[...])
    proj = jnp.dot(xn.astype(BF16), w_ref[...], preferred_element_type=F32)
    q_ref[...] = (proj[:, :A_WIDTH] * (A_HEAD_DIM ** -0.5)).astype(q_ref.dtype)
    kv_ref[...] = proj[:, A_WIDTH:A_WIDTH + 2 * A_KV_WIDTH]
    g_ref[...] = proj[:, A_WIDTH + 2 * A_KV_WIDTH:]


def _attn_proj_call(x2d, head, gain, w_bf16, q_dtype, nbatch, ntile, n_piece, piece):
    tm = n_piece * piece
    wcols = w_bf16.shape[1]
    per_seq = x2d.shape[0] // (nbatch * piece)
    lead = 0 if head is None else 1
    dst = lambda b, i: (b * ntile + i, 0)
    xspec = [pl.BlockSpec((piece, D_MODEL),
                          functools.partial(lambda b, i, kk: (b * per_seq + jnp.maximum(i * n_piece + kk - lead, 0), 0),
                                            kk=kk))
             for kk in range(n_piece)]
    kern = functools.partial(_attn_proj_kernel, n_piece=n_piece)
    operands = [x2d] * n_piece + [gain, w_bf16]
    if head is None:
        kern = functools.partial(kern, None)
        head_spec = []
    else:
        head_spec = [pl.BlockSpec((piece, D_MODEL), lambda b, i: (0, 0))]
        operands = [head] + operands
    n = nbatch * ntile * tm
    return pl.pallas_call(
        kern,
        grid=(nbatch, ntile),
        in_specs=head_spec + xspec + [pl.BlockSpec((1, D_MODEL), lambda b, i: (0, 0)),
                                      pl.BlockSpec((D_MODEL, wcols), lambda b, i: (0, 0))],
        out_specs=[pl.BlockSpec((tm, A_WIDTH), dst),
                   pl.BlockSpec((tm, 2 * A_KV_WIDTH), dst),
                   pl.BlockSpec((tm, A_WIDTH), dst)],
        out_shape=[jax.ShapeDtypeStruct((n, A_WIDTH), q_dtype),
                   jax.ShapeDtypeStruct((n, 2 * A_KV_WIDTH), F32),
                   jax.ShapeDtypeStruct((n, A_WIDTH), F32)],
        compiler_params=_cparams(2),
        name="attn_proj",
    )(*operands)


def _padded_kv_tiles(kv, c):
    lo = _iota((1, LANES), 1) < A_HEAD_DIM
    j = c // 2
    out = []
    for base in (0, A_KV_WIDTH):
        t = kv[:, base + j * LANES: base + (j + 1) * LANES]
        tr = pltpu.roll(t, A_HEAD_DIM, 1)
        if c % 2 == 0:
            even, odd = jnp.where(lo, t, 0.0), jnp.where(lo, 0.0, tr)
        else:
            even, odd = jnp.where(lo, tr, 0.0), jnp.where(lo, 0.0, t)
        out += [even.astype(BF16), odd.astype(BF16)]
    return out


def _mm_nt(a, b):
    return lax.dot_general(a, b, (((1,), (1,)), ((), ())), preferred_element_type=F32)


def _attn_prompt_kernel(sinks_ref, q_ref, kvc_ref, kvp_ref, g_ref, head_ref, *refs):
    nb = ATTN_BLOCKS_PER_STEP
    x_refs, (bias_ref, wout_ref, out_ref, og_ref) = refs[:nb], refs[nb:]
    i = pl.program_id(1)
    stack = 2 * BLOCK
    row, col = _iota((stack, BLOCK), 0) & (BLOCK - 1), _iota((stack, BLOCK), 1)
    upper = _iota((stack, 1), 0) >= BLOCK
    own = col <= row
    chains = [(c, idx) for c in range(A_KV_HEADS) for idx in range(2)]
    n = range(len(chains))
    cur = [_padded_kv_tiles(kvp_ref[...], c) for c in range(A_KV_HEADS)]
    for j in range(nb):
        rows = slice(j * BLOCK, (j + 1) * BLOCK)
        kvalid = (i * nb + j - 1 + own.astype(jnp.int32)) * BLOCK + col >= PAD
        prev, cur = cur, [_padded_kv_tiles(kvc_ref[rows, :], c) for c in range(A_KV_HEADS)]
        s, sink = [], []
        for c, idx in chains:
            q2 = q_ref[rows, 2 * c * LANES:(2 * c + 2) * LANES]
            q2 = jnp.concatenate([q2[:, :LANES], q2[:, LANES:]], axis=0)
            sc = jnp.where(own, _mm_nt(q2, cur[c][idx]), _mm_nt(q2, prev[c][idx]))
            bias = jnp.concatenate([bias_ref[4 * c + idx], bias_ref[4 * c + 2 + idx]], axis=0)
            s.append(jnp.where(kvalid, sc + bias, NEG))
            sink.append(jnp.where(upper, sinks_ref[4 * c + 2 + idx], sinks_ref[4 * c + idx]))
        m = [jnp.maximum(jnp.max(s[t], axis=1, keepdims=True), sink[t]) for t in n]
        p = [jnp.exp(s[t] - m[t]) for t in n]
        den = [jnp.sum(p[t], axis=1, keepdims=True) + jnp.exp(sink[t] - m[t]) for t in n]
        o = []
        for t, (c, idx) in enumerate(chains):
            pv = (jnp.dot(jnp.where(own, p[t], 0.0).astype(BF16), cur[c][2 + idx], preferred_element_type=F32)
                  + jnp.dot(jnp.where(own, 0.0, p[t]).astype(BF16), prev[c][2 + idx], preferred_element_type=F32))
            o.append(pv * (1.0 / den[t]))
        for c in range(A_KV_HEADS):
            both = o[2 * c] + o[2 * c + 1]
            for half in range(2):
                sl = slice((2 * c + half) * LANES, (2 * c + half + 1) * LANES)
                gt = g_ref[rows, sl]
                og_ref[rows, sl] = (both[half * BLOCK:(half + 1) * BLOCK] * (gt * _sigmoid(gt))).astype(BF16)
    resid = jnp.concatenate([jnp.where(i == 0, head_ref[...], x_refs[0][...])]
                            + [ref[...] for ref in x_refs[1:]], axis=0)
    out_ref[...] = resid + jnp.dot(og_ref[...], wout_ref[...], preferred_element_type=F32)


def _attn_prompt_call(sinks, q, kv, g, head, x2d, bias, wout_bf16, nbatch, nblk):
    n = q.shape[0]
    nb = ATTN_BLOCKS_PER_STEP
    nstep = nblk // nb
    rows = nb * BLOCK
    row = lambda b, i: (b * nstep + i, 0)
    prev = lambda b, i: (b * nblk + jnp.maximum(i * nb - 1, 0), 0)
    xrow = [functools.partial(lambda b, i, j: (b * (nblk - 1) + jnp.maximum(i * nb + j - 1, 0), 0), j=j)
            for j in range(nb)]
    return pl.pallas_call(
        _attn_prompt_kernel,
        grid=(nbatch, nstep),
        in_specs=[pl.BlockSpec(memory_space=pltpu.SMEM),
                  pl.BlockSpec((rows, A_WIDTH), row),
                  pl.BlockSpec((rows, 2 * A_KV_WIDTH), row),
                  pl.BlockSpec((BLOCK, 2 * A_KV_WIDTH), prev),
                  pl.BlockSpec((rows, A_WIDTH), row),
                  pl.BlockSpec((BLOCK, D_MODEL), lambda b, i: (0, 0))]
                 + [pl.BlockSpec((BLOCK, D_MODEL), xrow[j]) for j in range(nb)]
                 + [pl.BlockSpec((A_HEADS, BLOCK, BLOCK), lambda b, i: (0, 0, 0)),
                    pl.BlockSpec((A_WIDTH, D_MODEL), lambda b, i: (0, 0))],
        out_specs=pl.BlockSpec((rows, D_MODEL), row),
        out_shape=jax.ShapeDtypeStruct((n, D_MODEL), F32),
        scratch_shapes=[pltpu.VMEM((rows, A_WIDTH), BF16)],
        compiler_params=_cparams(2),
        name="attn_prompt",
    )(sinks, q, kv, kv, g, head, *([x2d] * nb), bias, wout_bf16)


SAMPLE_SB = 8
SAMPLE_KEYS = 2 * BLOCK


def _attn_sample_kernel(sinks_ref, q_ref, kvn_ref, g_ref, h_ref, ck_ref, cv_ref, bias0_ref, bias1_ref,
                        wout_ref, out_ref, nk_ref, nv_ref, og_ref, *, t_new):
    keep = ck_ref.shape[2]
    lo = _iota((1, LANES), 1) < A_HEAD_DIM
    stack = 8 * SUBLANES
    own = (_iota((stack, 1), 0) & (SUBLANES - 1)) // t_new
    piece = _iota((stack, 1), 0) // SUBLANES
    bias_refs = (bias0_ref, bias1_ref)
    n_tile = A_KV_WIDTH // LANES
    pair_ids = range(SAMPLE_SB * t_new // SUBLANES)

    bias_c, bias_n, sink = [], [], []
    for j in range(n_tile):
        heads = slice(8 * j, 8 * j + 8)
        bias_c.append(bias0_ref[heads, :, :keep].reshape(stack, keep))
        bias_n.append([ref[heads, :, keep:keep + SUBLANES].reshape(stack, SUBLANES) for ref in bias_refs])
        col = jnp.zeros((stack, 1), F32)
        for gq in range(8):
            col = jnp.where(piece == gq, sinks_ref[8 * j + gq], col)
        sink.append(col)

    chains = [(p, j, s) for p in pair_ids for j in range(n_tile) for s in range(SUBLANES // t_new)]
    qs = {}
    for p in pair_ids:
        rows = slice(p * SUBLANES, (p + 1) * SUBLANES)
        for j in range(n_tile):
            parts = []
            for gq in range(8):
                t = q_ref[rows, (4 * j + gq // 2) * LANES:(4 * j + gq // 2 + 1) * LANES]
                want_lo = gq < 4
                if (gq % 2 == 0) != want_lo:
                    t = pltpu.roll(t, A_HEAD_DIM, 1)
                parts.append(jnp.where(lo, t, 0.0) if want_lo else jnp.where(lo, 0.0, t))
            qs[p, j] = jnp.concatenate(parts, axis=0).astype(BF16)

    def kv_tiles(p, j, s, base):
        seq = p * (SUBLANES // t_new) + s
        cache = (ck_ref if base == 0 else cv_ref)[seq][j * LANES:(j + 1) * LANES, :].astype(BF16)
        new = kvn_ref[p * SUBLANES:(p + 1) * SUBLANES, base + j * LANES:base + (j + 1) * LANES].astype(BF16)
        return cache, new

    sc, sn = [], []
    for p, j, s in chains:
        kc, kn = kv_tiles(p, j, s, 0)
        sc.append(jnp.dot(qs[p, j], kc, preferred_element_type=F32) + bias_c[j])
        sn.append(_mm_nt(qs[p, j], kn) + bias_n[j][s])
    mx = [jnp.maximum(jnp.maximum(jnp.max(sc[i], axis=1, keepdims=True),
                                  jnp.max(sn[i], axis=1, keepdims=True)), sink[chains[i][1]])
          for i in range(len(chains))]
    pc = [jnp.exp(sc[i] - mx[i]) for i in range(len(chains))]
    pn = [jnp.exp(sn[i] - mx[i]) for i in range(len(chains))]
    den = [jnp.sum(pc[i], axis=1, keepdims=True) + jnp.sum(pn[i], axis=1, keepdims=True)
           + jnp.exp(sink[chains[i][1]] - mx[i]) for i in range(len(chains))]
    outs = {}
    for i, (p, j, s) in enumerate(chains):
        vc, vn = kv_tiles(p, j, s, A_KV_WIDTH)
        o = (_mm_nt(pc[i].astype(BF16), vc)
             + jnp.dot(pn[i].astype(BF16), vn, preferred_element_type=F32)) * (1.0 / den[i])
        outs[p, j] = o if s == 0 else jnp.where(own == s, o, outs[p, j])
    for p in pair_ids:
        rows = slice(p * SUBLANES, (p + 1) * SUBLANES)
        for j in range(n_tile):
            o = outs[p, j]
            for gg in range(4):
                even = o[2 * gg * SUBLANES:(2 * gg + 1) * SUBLANES]
                odd = o[(2 * gg + 1) * SUBLANES:(2 * gg + 2) * SUBLANES]
                if gg < 2:
                    tile = jnp.where(lo, even, pltpu.roll(odd, A_HEAD_DIM, 1))
                else:
                    tile = jnp.where(lo, pltpu.roll(even, A_HEAD_DIM, 1), odd)
                sl = slice((4 * j + gg) * LANES, (4 * j + gg + 1) * LANES)
                gt = g_ref[rows, sl]
                og_ref[rows, sl] = tile * (gt * _sigmoid(gt))
    lane_pos = _iota((SUBLANES, keep), 1)
    old = _iota((1, keep), 1) < keep - t_new
    for p in pair_ids:
        new8 = kvn_ref[p * SUBLANES:(p + 1) * SUBLANES, :]
        for s in range(SUBLANES // t_new):
            seq = p * (SUBLANES // t_new) + s
            sel = (lane_pos == _iota((SUBLANES, keep), 0) + (keep - t_new - s * t_new)).astype(BF16)
            placed = _mm2(_split(new8), sel, TN)
            for ref_in, ref_out, base in ((ck_ref, nk_ref, 0), (cv_ref, nv_ref, A_KV_WIDTH)):
                shifted = pltpu.roll(ref_in[seq], keep - t_new, 1)
                ref_out[seq] = jnp.where(old, shifted, placed[base:base + A_KV_WIDTH])
    out_ref[...] = h_ref[...] + jnp.dot(og_ref[...].astype(BF16), wout_ref[...],
                                        preferred_element_type=F32)


def _attn_sample_call(sinks, q, kv, g, h, cache_k, cache_v, bias0, bias1, wout_bf16, t_new):
    nseq, keep = cache_k.shape[0], cache_k.shape[2]
    rows = SAMPLE_SB * t_new
    row = lambda i: (i, 0)
    cspec = pl.BlockSpec((SAMPLE_SB, A_KV_WIDTH, keep), lambda i: (i, 0, 0))
    bspec = pl.BlockSpec((A_HEADS, SUBLANES, SAMPLE_KEYS), lambda i: (0, 0, 0))
    return pl.pallas_call(
        functools.partial(_attn_sample_kernel, t_new=t_new),
        grid=(nseq // SAMPLE_SB,),
        in_specs=[pl.BlockSpec(memory_space=pltpu.SMEM),
                  pl.BlockSpec((rows, A_WIDTH), row),
                  pl.BlockSpec((rows, 2 * A_KV_WIDTH), row),
                  pl.BlockSpec((rows, A_WIDTH), row),
                  pl.BlockSpec((rows, D_MODEL), row),
                  cspec, cspec, bspec, bspec,
                  pl.BlockSpec((A_WIDTH, D_MODEL), lambda i: (0, 0))],
        out_specs=[pl.BlockSpec((rows, D_MODEL), row), cspec, cspec],
        out_shape=[jax.ShapeDtypeStruct((nseq * t_new, D_MODEL), F32),
                   jax.ShapeDtypeStruct(cache_k.shape, F32),
                   jax.ShapeDtypeStruct(cache_v.shape, F32)],
        scratch_shapes=[pltpu.VMEM((rows, A_WIDTH), F32)],
        compiler_params=_cparams(1),
        name="attn_sample",
    )(sinks, q, kv, g, h, cache_k, cache_v, bias0, bias1, wout_bf16)


def _rwkv_proj_kernel(h_ref, shift_ref, gain_ref, mu_ref, win_ref, w0_ref, w1_ref, w2_ref,
                      a0_ref, a1_ref, a2_ref,
                      r_ref, k_ref, v_ref, g_ref, ld_ref, a_ref, xn_ref, *scratch, seq_len):
    xn = _rmsnorm(h_ref[...], gain_ref[...])
    tm = xn.shape[0]
    rolled = pltpu.roll(xn, 1, 0)
    row = _iota((tm, 1), 0)
    if seq_len is None:
        carry_ref, = scratch

        @pl.when(pl.program_id(1) == 0)
        def _():
            carry_ref[...] = shift_ref[0]

        xprev = jnp.where(row == 0, carry_ref[...], rolled)
        carry_ref[...] = xn[tm - 1:tm, :]
        xn_ref[0] = xn[tm - 1:tm, :]
    else:
        xprev = jnp.where(row % seq_len == 0, shift_ref[...], rolled)
        xn_ref[...] = xn
    dx = xprev - xn

    def mix(c):
        return (xn + dx * mu_ref[c:c + 1, :]).astype(BF16)

    for c, o_ref in enumerate((r_ref, k_ref, v_ref, g_ref)):
        o_ref[...] = jnp.dot(mix(c), win_ref[c], preferred_element_type=F32)
    lw = jnp.tanh(jnp.dot(mix(4), w1_ref[...], preferred_element_type=F32))
    z = w0_ref[...] + jnp.dot(lw.astype(BF16), w2_ref[...], preferred_element_type=F32)
    ld_ref[...] = -math.exp(-0.5) * _sigmoid(z)
    la = jnp.dot(mix(5), a1_ref[...], preferred_element_type=F32)
    a_ref[...] = _sigmoid(a0_ref[...] + jnp.dot(la.astype(BF16), a2_ref[...],
                                                preferred_element_type=F32))


def _rwkv_proj_call(h, shift, p, nbatch, ntile, tm, seq_len):
    n = h.shape[0]
    row = lambda b, i: (b * ntile + i, 0)
    full2 = lambda b, i: (0, 0)
    if seq_len is None:
        shift_spec = pl.BlockSpec((1, 1, D_MODEL), lambda b, i: (b, 0, 0))
        xn_spec = pl.BlockSpec((1, 1, D_MODEL), lambda b, i: (b, 0, 0))
        xn_shape = jax.ShapeDtypeStruct((nbatch, 1, D_MODEL), F32)
        scratch = [pltpu.VMEM((1, D_MODEL), F32)]
    else:
        shift_spec = pl.BlockSpec((tm, D_MODEL), row)
        xn_spec = pl.BlockSpec((tm, D_MODEL), row)
        xn_shape = jax.ShapeDtypeStruct((n, D_MODEL), F32)
        scratch = []
    lora = p["w1"].shape[1]
    big = jax.ShapeDtypeStruct((n, D_MODEL), F32)
    return pl.pallas_call(
        functools.partial(_rwkv_proj_kernel, seq_len=seq_len),
        grid=(nbatch, ntile),
        in_specs=[pl.BlockSpec((tm, D_MODEL), row),
                  shift_spec,
                  pl.BlockSpec((1, D_MODEL), full2),
                  pl.BlockSpec(p["mu"].shape, full2),
                  pl.BlockSpec(p["w_in"].shape, lambda b, i: (0, 0, 0)),
                  pl.BlockSpec((1, D_MODEL), full2),
                  pl.BlockSpec((D_MODEL, lora), full2),
                  pl.BlockSpec((lora, D_MODEL), full2),
                  pl.BlockSpec((1, D_MODEL), full2),
                  pl.BlockSpec((D_MODEL, lora), full2),
                  pl.BlockSpec((lora, D_MODEL), full2)],
        out_specs=[pl.BlockSpec((tm, D_MODEL), row)] * 6 + [xn_spec],
        out_shape=[big] * 6 + [xn_shape],
        scratch_shapes=scratch,
        compiler_params=_cparams(2),
        name="rwkv_proj",
    )(h, shift, p["gain"], p["mu"], p["w_in"], p["w0"], p["w1"], p["w2"], p["a0"], p["a1"], p["a2"])


NN = (((1,), (0,)), ((), ()))
NT = (((1,), (1,)), ((), ()))
TN = (((0,), (0,)), ((), ()))


def _split(x):
    hi = x.astype(BF16)
    return hi, (x - hi.astype(F32)).astype(BF16)


def _mm(a, b, dn):
    return lax.dot_general(a, b, dn, preferred_element_type=F32)


def _mm2(a, b, dn):
    ah, al = a
    if dn == TN:
        both = _mm(jnp.concatenate([ah, al], axis=1), b, dn)
        m = ah.shape[1]
    else:
        both = _mm(jnp.concatenate([ah, al], axis=0), b, dn)
        m = ah.shape[0]
    return both[:m] + both[m:]


def _seg_sums(xs, ones_bf16, exact):
    rows = xs[0].shape[0]
    stacked = jnp.concatenate(xs, axis=0)
    if exact:
        hi, lo = _split(stacked)
        out = _mm(lo, ones_bf16, NN) + _mm(hi, ones_bf16, NN)
    else:
        out = _mm(stacked.astype(BF16), ones_bf16, NN)
    return [out[i * rows:(i + 1) * rows] for i in range(len(xs))]


def _lane_lo():
    return _iota((1, LANES), 1) < R_HEAD_DIM


def _same_half():
    return (_iota((LANES, LANES), 0) < R_HEAD_DIM) == (_iota((LANES, LANES), 1) < R_HEAD_DIM)


def _ones_blk():
    return _same_half().astype(BF16)


def _bd(z):
    lo = _lane_lo()
    return jnp.concatenate([jnp.where(lo, z, 0.0), jnp.where(lo, 0.0, z)], axis=0)


def _bd_swap(z):
    lo = _lane_lo()
    return jnp.concatenate([jnp.where(lo, 0.0, z), jnp.where(lo, z, 0.0)], axis=0)


def _wkv_batch_stage(r, k, v, a, ld, kkg, kag, fillers=(), early=None):
    c = r[0].shape[0]
    pairs = range(len(r))
    half = R_HEAD_DIM
    lo = _lane_lo()
    t_row, t_col = _iota((c, LANES), 0), _iota((c, LANES), 1) & (half - 1)
    strict, incl = t_row > t_col, t_row >= t_col
    tri = (_iota((c, c), 0) >= _iota((c, c), 1)).astype(BF16)
    ones_blk = _ones_blk()
    eye_pair = (_iota((c, LANES), 0) == (_iota((c, LANES), 1) & (half - 1))).astype(F32)

    kkx = [k[j] * kkg[j] for j in pairs]
    ssq = _seg_sums([kkx[j] * kkx[j] for j in pairs], ones_blk, exact=True)
    ld_hi, ld_lo = _split(jnp.concatenate(ld, axis=1))
    cs_all = _mm(tri, ld_lo, NN) + _mm(tri, ld_hi, NN)
    cs = [cs_all[:, j * LANES:(j + 1) * LANES] for j in pairs]
    tot = [cs[j][c - 1:c, :] for j in pairs]
    x_hi, kh_all, bke, vbd_s, lk_a, lk_r, lb_a, lb_r = ([] for _ in range(8))
    for j in pairs:
        kk = kkx[j] * lax.rsqrt(jnp.maximum(ssq[j], 1e-24))
        kh = k[j] * (1.0 + (a[j] - 1.0) * kag[j])
        bv = kk * a[j]
        e_neg = jnp.exp(-cs[j])
        e_end = e_neg * jnp.exp(tot[j])
        at, rt = -kk * jnp.exp(cs[j] - ld[j]), r[j] * jnp.exp(cs[j])
        x_hi.append(jnp.concatenate([at.astype(BF16), rt.astype(BF16)], axis=0))
        kh_all.append(kh)
        y_hi = jnp.concatenate([(bv * e_neg).astype(BF16), (kh * e_neg).astype(BF16)], axis=0)
        bke.append(jnp.concatenate([(bv * e_end).astype(BF16), (kh * e_end).astype(BF16)], axis=0))
        vbd_s.append(_bd(v[j].astype(BF16)))
        both = _mm(jnp.concatenate([jnp.where(lo, x_hi[j], 0.0), jnp.where(lo, 0.0, x_hi[j])], axis=0), y_hi, NT)
        ga_a = pltpu.roll(both[:c], half, 1)
        ga_r = pltpu.roll(both[c:2 * c], half, 1)
        gb_a = both[2 * c:3 * c]
        gb_r = both[3 * c:4 * c]
        lk_a.append(jnp.where(strict, jnp.where(lo, ga_a, gb_a), 0.0))
        lk_r.append(jnp.where(incl, jnp.where(lo, ga_r, gb_r), 0.0))
        lb_a.append(jnp.where(strict, jnp.where(lo, gb_a, ga_a), 0.0))
        lb_r.append(jnp.where(incl, jnp.where(lo, gb_r, ga_r), 0.0))
    fillers = list(fillers)
    if early is not None:
        early.update(x_hi=x_hi)
    pw = lb_a
    acc = [eye_pair + pw[j] for j in pairs]
    pw = [_mm(pw[j].astype(BF16), _bd(pw[j].astype(BF16)), NN) for j in pairs]
    from_v = []
    for j in pairs:
        both = _mm(jnp.concatenate([lk_a[j].astype(BF16), lk_r[j].astype(BF16)], axis=0), vbd_s[j], NN)
        from_v.append((both[:c], both[c:]))
    for _ in range(int(math.log2(c)) - 2):
        both = [_mm(jnp.concatenate([pw[j], acc[j]], axis=0).astype(BF16), _bd(pw[j].astype(BF16)), NN)
                for j in pairs]
        if fillers:
            fillers.pop(0)()
        pw = [both[j][:c] for j in pairs]
        acc = [acc[j] + both[j][c:] for j in pairs]
    tinv = [acc[j] + _mm(acc[j].astype(BF16), _bd(pw[j].astype(BF16)), NN) for j in pairs]
    for fill in fillers:
        fill()
    return dict(x_hi=x_hi, kh=kh_all, lb_r=lb_r, tinv=tinv, from_v=from_v, bke=bke,
                tot=tot, ones_blk=ones_blk)


def _wkv_finish(stage, from_state_a, from_state_r):
    pairs = range(len(from_state_a))
    u = [_mm(stage["tinv"][j].astype(BF16),
             _bd_swap((from_state_a[j] + stage["from_v"][j][0]).astype(BF16)), NN) for j in pairs]
    y = [from_state_r[j] + stage["from_v"][j][1]
         + _mm(stage["lb_r"][j].astype(BF16), _bd_swap(u[j].astype(BF16)), NN) for j in pairs]
    return u, y


def _wkv_gate(y, r, kh, v, g, rk, lng, lnb, ones_blk):
    n = len(y)
    pairs = range(n)
    inv_n = 1.0 / R_HEAD_DIM
    sums = _seg_sums([r[j] * kh[j] * rk[j] for j in pairs] + list(y), ones_blk, exact=False)
    rkk, mean = sums[:n], [s * inv_n for s in sums[n:]]
    d = [y[j] - mean[j] for j in pairs]
    var = [s * inv_n for s in _seg_sums([d[j] * d[j] for j in pairs], ones_blk, exact=False)]
    return [((d[j] * lax.rsqrt(var[j] + GN_EPS) * lng[j] + lnb[j] + rkk[j] * v[j])
             * (g[j] * _sigmoid(g[j]))).astype(BF16) for j in pairs]


def _wkv_chunk_kernel(r_ref, k_ref, v_ref, a_ref, ld_ref, g_ref, kkg_ref, kag_ref, rk_ref, lng_ref, lnb_ref,
                      z_ref, sout_ref, st_ref):
    c = WKV_CHUNK
    n_sub = r_ref.shape[0] // c
    npair = r_ref.shape[1] // LANES
    pairs = range(npair)
    items = [(s, j) for s in range(n_sub) for j in pairs]

    @pl.when(pl.program_id(2) == 0)
    def _():
        st_ref[...] = jnp.zeros_like(st_ref)

    def tile(ref, it):
        s, j = it
        return ref[s * c:(s + 1) * c, j * LANES:(j + 1) * LANES]

    def par(ref):
        return [ref[:, j * LANES:(j + 1) * LANES] for _, j in items]

    r, k, v, a, ld, g = ([tile(ref, it) for it in items] for ref in (r_ref, k_ref, v_ref, a_ref, ld_ref, g_ref))
    st = [st_ref[j] for j in pairs]
    from_state = {}
    stage = {}

    def state_products(group):
        def run():
            for j in group:
                from_state[j] = _mm(stage["x_hi"][j], st[j].astype(BF16), NN)
        return run

    n_fill = 4
    groups = [list(pairs)[i::n_fill] for i in range(n_fill)]
    stage.update(_wkv_batch_stage(r, k, v, a, ld, par(kkg_ref), par(kag_ref),
                                  fillers=[state_products(grp) for grp in groups if grp], early=stage))
    same_half = _same_half()
    eye_full = _iota((LANES, LANES), 0) == _iota((LANES, LANES), 1)
    ys = []
    for s in range(n_sub):
        idx = [s * npair + j for j in pairs]
        if s > 0:
            for j in pairs:
                from_state[j] = _mm(stage["x_hi"][idx[j]], st[j].astype(BF16), NN)
        sub = {key: [stage[key][i] for i in idx] for key in ("tinv", "from_v", "lb_r")}
        us, ys_s = _wkv_finish(sub, [from_state[j][:c] for j in pairs], [from_state[j][c:] for j in pairs])
        upd = [_mm(stage["bke"][idx[j]], jnp.concatenate([us[j], v[idx[j]]], axis=0).astype(BF16), TN)
               for j in pairs]
        ys += ys_s
        for j in pairs:
            w_col = jnp.sum(jnp.where(eye_full, jnp.exp(stage["tot"][idx[j]]), 0.0), axis=1, keepdims=True)
            st[j] = w_col * st[j] + jnp.where(same_half, upd[j], 0.0)
    zs = _wkv_gate(ys, r, stage["kh"], v, g, par(rk_ref), par(lng_ref), par(lnb_ref), stage["ones_blk"])
    for i, (s, j) in enumerate(items):
        z_ref[s * c:(s + 1) * c, j * LANES:(j + 1) * LANES] = zs[i]
    for j in pairs:
        st_ref[j] = st[j]
        sout_ref[0, j] = st[j]


def _wkv_chunk_call(r, k, v, a, ld, g, p, nbatch, seq):
    rows = WKV_CHUNK * WKV_CHUNKS_PER_STEP
    nstep = seq // rows
    npair = D_MODEL // LANES
    tile = pl.BlockSpec((rows, D_MODEL), lambda b, j, t: (b * nstep + t, 0))
    par = pl.BlockSpec((1, D_MODEL), lambda b, j, t: (0, 0))
    return pl.pallas_call(
        _wkv_chunk_kernel,
        grid=(nbatch, 1, nstep),
        in_specs=[tile] * 6 + [par] * 5,
        out_specs=[tile, pl.BlockSpec((1, npair, LANES, LANES), lambda b, j, t: (b, 0, 0, 0))],
        out_shape=[jax.ShapeDtypeStruct((nbatch * seq, D_MODEL), BF16),
                   jax.ShapeDtypeStruct((nbatch, npair, LANES, LANES), F32)],
        scratch_shapes=[pltpu.VMEM((npair, LANES, LANES), F32)],
        compiler_params=_cparams(3),
        name="wkv_chunk",
    )(r, k, v, a, ld, g, p["k_k"], p["k_a"], p["r_k"], p["ln_g"], p["ln_b"])


WKV_LANES_UNROLL = 16


def _wkv_lanes_kernel(r_ref, k_ref, v_ref, a_ref, ld_ref, g_ref, kkg_ref, kag_ref, rk_ref, lng_ref, lnb_ref,
                      s_ref, z_ref, sout_ref, prep_ref, y_ref, *, seq_len):
    n = R_HEAD_DIM
    nseq = s_ref.shape[3]
    eye = _iota((LANES, LANES), 0) == _iota((LANES, LANES), 1)

    def column(ref):
        return jnp.sum(jnp.where(eye, ref[...], 0.0), axis=1, keepdims=True)

    kkg, kag, rk, lng, lnb = (column(ref) for ref in (kkg_ref, kag_ref, rk_ref, lng_ref, lnb_ref))

    def token(ref, t):
        return ref[pl.ds(t, nseq, stride=seq_len), :].T

    bonus = []
    for t in range(seq_len):
        r, k, v, a = (token(ref, t) for ref in (r_ref, k_ref, v_ref, a_ref))
        w = jnp.exp(token(ld_ref, t))
        kkx = k * kkg
        kh = k * (1.0 + (a - 1.0) * kag)
        rkk = r * kh * rk
        tiles = []
        for hh in range(2):
            rows = slice(hh * n, (hh + 1) * n)
            nrm = jnp.sqrt(jnp.sum(kkx[rows] * kkx[rows], axis=0, keepdims=True))
            kk = kkx[rows] / jnp.maximum(nrm, 1e-12)
            for q, val in enumerate((w[rows], -kk, kk * a[rows], kh[rows], r[rows], v[rows])):
                prep_ref[q, t, hh] = val
            tiles.append(jnp.sum(rkk[rows], axis=0, keepdims=True) * v[rows])
        bonus.append(tiles)

    for hh in range(2):
        def advance(i, carry, hh=hh):
            for u in range(WKV_LANES_UNROLL):
                vi = i * WKV_LANES_UNROLL + u
                slab = s_ref[hh, vi]
                for t in range(seq_len):
                    w, av, bv, kh, r = (prep_ref[q, t, hh] for q in range(5))
                    vrow = prep_ref[5, t, hh, pl.ds(vi, 1), :]
                    sa = jnp.sum(slab * av, axis=0, keepdims=True)
                    slab = slab * w + sa * bv + vrow * kh
                    y_ref[t, hh, pl.ds(vi, 1), :] = jnp.sum(slab * r, axis=0, keepdims=True)
                sout_ref[hh, vi] = slab
            return carry

        lax.fori_loop(0, n // WKV_LANES_UNROLL, advance, 0)

    for t in range(seq_len):
        parts = []
        for hh in range(2):
            rows = slice(hh * n, (hh + 1) * n)
            y = y_ref[t, hh]
            d = y - jnp.mean(y, axis=0, keepdims=True)
            var = jnp.mean(d * d, axis=0, keepdims=True)
            parts.append(d * lax.rsqrt(var + GN_EPS) * lng[rows] + lnb[rows] + bonus[t][hh])
        g = token(g_ref, t)
        z = jnp.concatenate(parts, axis=0) * (g * _sigmoid(g))
        z_ref[pl.ds(t, nseq, stride=seq_len), :] = z.T


def _wkv_lanes_call(r, k, v, a, ld, g, p, state_hvkb, seq_len):
    n = r.shape[0]
    nseq = state_hvkb.shape[3]
    tile = pl.BlockSpec((n, LANES), lambda j: (0, j))
    par = pl.BlockSpec((1, LANES), lambda j: (0, j))
    sspec = pl.BlockSpec((2, R_HEAD_DIM, R_HEAD_DIM, nseq), lambda j: (j, 0, 0, 0))
    return pl.pallas_call(
        functools.partial(_wkv_lanes_kernel, seq_len=seq_len),
        grid=(D_MODEL // LANES,),
        in_specs=[tile] * 6 + [par] * 5 + [sspec],
        out_specs=[tile, sspec],
        out_shape=[jax.ShapeDtypeStruct((n, D_MODEL), F32),
                   jax.ShapeDtypeStruct(state_hvkb.shape, F32)],
        scratch_shapes=[pltpu.VMEM((6, seq_len, 2, R_HEAD_DIM, nseq), F32),
                        pltpu.VMEM((seq_len, 2, R_HEAD_DIM, nseq), F32)],
        compiler_params=_cparams(1),
        name="wkv_lanes",
    )(r, k, v, a, ld, g, p["k_k"], p["k_a"], p["r_k"], p["ln_g"], p["ln_b"], state_hvkb)


RWKV_OUT_PIECES = 8


def _rwkv_out_kernel(*refs):
    n = RWKV_OUT_PIECES
    z_refs, h_refs, (wout_ref, fg_ref, out_ref) = refs[:n], refs[n:2 * n], refs[2 * n:]
    z = jnp.concatenate([ref[...].astype(BF16) for ref in z_refs], axis=0)
    h = jnp.concatenate([ref[...] for ref in h_refs], axis=0)
    h2 = h + jnp.dot(z, wout_ref[...], preferred_element_type=F32)
    out_ref[...] = _rmsnorm(h2, fg_ref[...])


def _rwkv_out_call(z, h, p, nbatch, ntile, tm, pieces_per_batch, skip):
    piece = tm // RWKV_OUT_PIECES
    dst = lambda b, i: (b * ntile + i, 0)
    pspec = [pl.BlockSpec((piece, D_MODEL),
                          functools.partial(lambda b, i, kk: (b * pieces_per_batch + i * RWKV_OUT_PIECES + skip + kk, 0),
                                            kk=kk))
             for kk in range(RWKV_OUT_PIECES)]
    return pl.pallas_call(
        _rwkv_out_kernel,
        grid=(nbatch, ntile),
        in_specs=pspec + pspec + [pl.BlockSpec((D_MODEL, D_MODEL), lambda b, i: (0, 0)),
                                  pl.BlockSpec((1, D_MODEL), lambda b, i: (0, 0))],
        out_specs=pl.BlockSpec((tm, D_MODEL), dst),
        out_shape=jax.ShapeDtypeStruct((nbatch * ntile * tm, D_MODEL), F32),
        compiler_params=_cparams(2),
        name="rwkv_out",
    )(*([z] * RWKV_OUT_PIECES), *([h] * RWKV_OUT_PIECES), p["w_out"], p["final_gain"])


def _prompt_bucket():
    assert WINDOW == BLOCK
    rel = (np.arange(BLOCK)[:, None] - np.arange(BLOCK)[None, :]) % BLOCK
    return _t5_bucket_np(rel)


def _sample_bucket(keep, t_new, slot):
    t = (np.arange(SUBLANES) % t_new)[:, None]
    j = np.arange(SAMPLE_KEYS)[None, :]
    own = j - keep - slot * t_new
    rel = np.where(j < keep, keep + t - j, t - own)
    ok = (rel >= 0) & (rel < WINDOW) & ((j < keep) | ((own >= 0) & (own < t_new)))
    return np.where(ok, _t5_bucket_np(rel), -1).astype(np.int32)


def kernel(x_prompt, x_sample, cache_win_k, cache_win_v, state_wkv, state_shift, meta_tokens, rel_bias_table, norm_gain, final_gain, attn_w_in, attn_sinks, attn_w_out, rwkv_mu, rwkv_w_in, rwkv_w0, rwkv_w1, rwkv_w2, rwkv_a0, rwkv_a1, rwkv_a2, rwkv_k_k, rwkv_k_a, rwkv_r_k, rwkv_ln_gamma, rwkv_ln_beta, rwkv_w_out):
    nb, seq, _ = x_prompt.shape
    ns, t_new, _ = x_sample.shape
    keep = cache_win_k.shape[2]
    lp = seq + BLOCK
    nblk = lp // BLOCK
    row = lambda x: x.reshape(1, D_MODEL)

    w_in0 = attn_w_in[0].astype(BF16)
    w_out0 = attn_w_out[0].astype(BF16)
    gain0 = row(norm_gain[0])
    sinks = attn_sinks[0]
    rp = dict(gain=row(norm_gain[1]), mu=rwkv_mu[0], w_in=rwkv_w_in[0].astype(BF16),
              w0=row(rwkv_w0[0]), w1=rwkv_w1[0].astype(BF16), w2=rwkv_w2[0].astype(BF16),
              a0=row(rwkv_a0[0]), a1=rwkv_a1[0].astype(BF16), a2=rwkv_a2[0].astype(BF16),
              k_k=row(rwkv_k_k[0]), k_a=row(rwkv_k_a[0]), r_k=row(rwkv_r_k[0]), ln_g=row(rwkv_ln_gamma[0]),
              ln_b=row(rwkv_ln_beta[0]), w_out=rwkv_w_out[0].astype(BF16),
              final_gain=row(final_gain))

    bias_p, *bias_s = _bias_call(rel_bias_table, [_prompt_bucket()]
                                 + [_sample_bucket(keep, t_new, slot) for slot in range(2)])

    head = jnp.concatenate([jnp.zeros((PAD, D_MODEL), F32), meta_tokens.astype(F32)], axis=0)
    xp = x_prompt.reshape(nb * seq, D_MODEL)
    q, kv, g = _attn_proj_call(xp, head, gain0, w_in0, BF16, nb, lp // ATTN_PROJ_ROWS,
                               ATTN_PROJ_ROWS // BLOCK, BLOCK)
    h1 = _attn_prompt_call(sinks, q, kv, g, head, xp, bias_p, w_out0, nb, nblk)
    kv3 = kv.reshape(nb, lp, 2 * A_KV_WIDTH)[:, lp - WINDOW:, :]
    win_k_p = kv3[:, :, :A_KV_WIDTH].reshape(1, nb, WINDOW, A_KV_HEADS, A_HEAD_DIM)
    win_v_p = kv3[:, :, A_KV_WIDTH:].reshape(1, nb, WINDOW, A_KV_HEADS, A_HEAD_DIM)

    shift0 = jnp.zeros((nb, 1, D_MODEL), F32)
    r, k, v, g1, ld, a, xlast = _rwkv_proj_call(h1, shift0, rp, nb, lp // RWKV_PROJ_ROWS, RWKV_PROJ_ROWS, None)
    z, st = _wkv_chunk_call(r, k, v, a, ld, g1, rp, nb, lp)
    y_prompt = _rwkv_out_call(z, h1, rp, nb, seq // RWKV_OUT_ROWS, RWKV_OUT_ROWS,
                              lp * RWKV_OUT_PIECES // RWKV_OUT_ROWS, BLOCK * RWKV_OUT_PIECES // RWKV_OUT_ROWS)
    y_prompt = y_prompt.reshape(nb, seq, D_MODEL)
    st = st.reshape(nb, D_MODEL // LANES, 2, R_HEAD_DIM, 2, R_HEAD_DIM)
    st = jnp.stack([st[:, :, 0, :, 0, :], st[:, :, 1, :, 1, :]], axis=2)
    wkv_p = jnp.swapaxes(st, -1, -2).reshape(1, nb, R_HEADS, R_HEAD_DIM, R_HEAD_DIM)
    shift_p = xlast.reshape(1, nb, D_MODEL)

    xs = x_sample.reshape(ns * t_new, D_MODEL)
    qs, kvs, gs = _attn_proj_call(xs, None, gain0, w_in0, F32, 1, 1, 1, ns * t_new)
    ck = jnp.swapaxes(cache_win_k[0].reshape(ns, keep, A_KV_WIDTH), 1, 2)
    cv = jnp.swapaxes(cache_win_v[0].reshape(ns, keep, A_KV_WIDTH), 1, 2)
    h1s, nk, nv = _attn_sample_call(sinks, qs, kvs, gs, xs, ck, cv, bias_s[0], bias_s[1], w_out0, t_new)
    win_k_s = jnp.swapaxes(nk, 1, 2).reshape(1, ns, keep, A_KV_HEADS, A_HEAD_DIM)
    win_v_s = jnp.swapaxes(nv, 1, 2).reshape(1, ns, keep, A_KV_HEADS, A_HEAD_DIM)

    shift_rows = jnp.repeat(state_shift[0], t_new, axis=0)
    tms = 256
    rs, ks, vs, g1s, lds, as_, xns = _rwkv_proj_call(h1s, shift_rows, rp, 1, ns * t_new // tms, tms, t_new)
    zs, st_s = _wkv_lanes_call(rs, ks, vs, as_, lds, g1s, rp, jnp.transpose(state_wkv[0], (1, 2, 3, 0)), t_new)
    y_sample = _rwkv_out_call(zs, h1s, rp, 1, 1, ns * t_new, RWKV_OUT_PIECES, 0)
    y_sample = y_sample.reshape(ns, t_new, D_MODEL)
    wkv_s = jnp.transpose(st_s, (3, 0, 1, 2))[None]
    shift_s = xns.reshape(ns, t_new, D_MODEL)[:, t_new - 1][None]

    return (y_prompt, y_sample, win_k_p, win_v_p, wkv_p, shift_p, win_k_s, win_v_s, wkv_s, shift_s)
```

```python
import functools
import math

import numpy as np
import jax
import jax.numpy as jnp
from jax import lax
from jax.experimental import pallas as pl
from jax.experimental.pallas import tpu as pltpu

F32 = jnp.float32
BF16 = jnp.bfloat16

D_MODEL = 1024
N_META = 16
RMS_EPS = 1e-6
A_HEADS = 16
A_KV_HEADS = 4
A_HEAD_DIM = 64
A_WIDTH = A_HEADS * A_HEAD_DIM
A_KV_WIDTH = A_KV_HEADS * A_HEAD_DIM
WINDOW = 128
BLOCK = 128
N_BUCKETS = 32
MAX_DISTANCE = 128
R_HEAD_DIM = 64
R_HEADS = D_MODEL // R_HEAD_DIM
GN_EPS = 64e-5

LANES = 128
SUBLANES = 8
PAD = BLOCK - N_META
NEG = -1e30
WKV_CHUNK = 64
WKV_CHUNKS_PER_STEP = 3
ATTN_BLOCKS_PER_STEP = 3
ATTN_PROJ_ROWS = 384
RWKV_PROJ_ROWS = 528
RWKV_OUT_ROWS = 1024
VMEM_LIMIT = 56 * 1024 * 1024


def _cparams(n_axes):
    return pltpu.CompilerParams(dimension_semantics=("arbitrary",) * n_axes,
                                vmem_limit_bytes=VMEM_LIMIT)


def _rmsnorm(x, gain):
    return x * lax.rsqrt(jnp.mean(x * x, axis=-1, keepdims=True) + RMS_EPS) * gain


def _sigmoid(x):
    return 1.0 / (1.0 + jnp.exp(-x))


def _iota(shape, dim):
    return lax.broadcasted_iota(jnp.int32, shape, dim)


def _t5_bucket_np(rel):
    n = np.maximum(rel, 0)
    max_exact = N_BUCKETS // 2
    nf = np.maximum(n, max_exact).astype(np.float32)
    scale = np.float32(math.log(MAX_DISTANCE / max_exact))
    large = max_exact + (np.log(nf / np.float32(max_exact)) / scale
                         * np.float32(N_BUCKETS - max_exact)).astype(np.int32)
    large = np.minimum(large, N_BUCKETS - 1)
    return np.where(n < max_exact, n, large).astype(np.int32)


def _bias_kernel(table_ref, *refs):
    h = pl.program_id(0)
    n = len(refs) // 2
    for bucket_ref, out_ref in zip(refs[:n], refs[n:]):
        bk = bucket_ref[...]
        acc = jnp.full(bk.shape, NEG, F32)
        for b in range(N_BUCKETS):
            acc = jnp.where(bk == b, table_ref[b, h], acc)
        out_ref[0] = acc


def _bias_call(table, buckets_np):
    return pl.pallas_call(
        _bias_kernel,
        grid=(A_HEADS,),
        in_specs=[pl.BlockSpec(memory_space=pltpu.SMEM)]
                 + [pl.BlockSpec(bk.shape, lambda h: (0, 0)) for bk in buckets_np],
        out_specs=[pl.BlockSpec((1,) + bk.shape, lambda h: (h, 0, 0)) for bk in buckets_np],
        out_shape=[jax.ShapeDtypeStruct((A_HEADS,) + bk.shape, F32) for bk in buckets_np],
        compiler_params=_cparams(1),
        name="bias_expand",
    )(table, *(jnp.asarray(bk) for bk in buckets_np))


def _attn_proj_kernel(head_ref, *refs, n_piece):
    x_refs, (gain_ref, w_ref, q_ref, kv_ref, g_ref) = refs[:n_piece], refs[n_piece:]
    first = x_refs[0][...]
    if head_ref is not None:
        first = jnp.where(pl.program_id(1) == 0, head_ref[...], first)
    x = jnp.concatenate([first] + [ref[...] for ref in x_refs[1:]], axis=0)
    xn = _rmsnorm(x, gain_ref[...])
    proj = jnp.dot(xn.astype(BF16), w_ref[...], preferred_element_type=F32)
    q_ref[...] = (proj[:, :A_WIDTH] * (A_HEAD_DIM ** -0.5)).astype(q_ref.dtype)
    kv_ref[...] = proj[:, A_WIDTH:A_WIDTH + 2 * A_KV_WIDTH]
    g_ref[...] = proj[:, A_WIDTH + 2 * A_KV_WIDTH:]


def _attn_proj_call(x2d, head, gain, w_bf16, q_dtype, nbatch, ntile, n_piece, piece):
    tm = n_piece * piece
    wcols = w_bf16.shape[1]
    per_seq = x2d.shape[0] // (nbatch * piece)
    lead = 0 if head is None else 1
    dst = lambda b, i: (b * ntile + i, 0)
    xspec = [pl.BlockSpec((piece, D_MODEL),
                          functools.partial(lambda b, i, kk: (b * per_seq + jnp.maximum(i * n_piece + kk - lead, 0), 0),
                                            kk=kk))
             for kk in range(n_piece)]
    kern = functools.partial(_attn_proj_kernel, n_piece=n_piece)
    operands = [x2d] * n_piece + [gain, w_bf16]
    if head is None:
        kern = functools.partial(kern, None)
        head_spec = []
    else:
        head_spec = [pl.BlockSpec((piece, D_MODEL), lambda b, i: (0, 0))]
        operands = [head] + operands
    n = nbatch * ntile * tm
    return pl.pallas_call(
        kern,
        grid=(nbatch, ntile),
        in_specs=head_spec + xspec + [pl.BlockSpec((1, D_MODEL), lambda b, i: (0, 0)),
                                      pl.BlockSpec((D_MODEL, wcols), lambda b, i: (0, 0))],
        out_specs=[pl.BlockSpec((tm, A_WIDTH), dst),
                   pl.BlockSpec((tm, 2 * A_KV_WIDTH), dst),
                   pl.BlockSpec((tm, A_WIDTH), dst)],
        out_shape=[jax.ShapeDtypeStruct((n, A_WIDTH), q_dtype),
                   jax.ShapeDtypeStruct((n, 2 * A_KV_WIDTH), F32),
                   jax.ShapeDtypeStruct((n, A_WIDTH), F32)],
        compiler_params=_cparams(2),
        name="attn_proj",
    )(*operands)


def _padded_kv_tiles(kv, c):
    lo = _iota((1, LANES), 1) < A_HEAD_DIM
    j = c // 2
    out = []
    for base in (0, A_KV_WIDTH):
        t = kv[:, base + j * LANES: base + (j + 1) * LANES]
        tr = pltpu.roll(t, A_HEAD_DIM, 1)
        if c % 2 == 0:
            even, odd = jnp.where(lo, t, 0.0), jnp.where(lo, 0.0, tr)
        else:
            even, odd = jnp.where(lo, tr, 0.0), jnp.where(lo, 0.0, t)
        out += [even.astype(BF16), odd.astype(BF16)]
    return out


def _mm_nt(a, b):
    return lax.dot_general(a, b, (((1,), (1,)), ((), ())), preferred_element_type=F32)


def _attn_prompt_kernel(sinks_ref, q_ref, kvc_ref, kvp_ref, g_ref, head_ref, *refs):
    nb = ATTN_BLOCKS_PER_STEP
    x_refs, (bias_ref, wout_ref, out_ref, og_ref) = refs[:nb], refs[nb:]
    i = pl.program_id(1)
    stack = 2 * BLOCK
    row, col = _iota((stack, BLOCK), 0) & (BLOCK - 1), _iota((stack, BLOCK), 1)
    upper = _iota((stack, 1), 0) >= BLOCK
    own = col <= row
    chains = [(c, idx) for c in range(A_KV_HEADS) for idx in range(2)]
    n = range(len(chains))
    cur = [_padded_kv_tiles(kvp_ref[...], c) for c in range(A_KV_HEADS)]
    for j in range(nb):
        rows = slice(j * BLOCK, (j + 1) * BLOCK)
        kvalid = (i * nb + j - 1 + own.astype(jnp.int32)) * BLOCK + col >= PAD
        prev, cur = cur, [_padded_kv_tiles(kvc_ref[rows, :], c) for c in range(A_KV_HEADS)]
        s, sink = [], []
        for c, idx in chains:
            q2 = q_ref[rows, 2 * c * LANES:(2 * c + 2) * LANES]
            q2 = jnp.concatenate([q2[:, :LANES], q2[:, LANES:]], axis=0)
            sc = jnp.where(own, _mm_nt(q2, cur[c][idx]), _mm_nt(q2, prev[c][idx]))
            bias = jnp.concatenate([bias_ref[4 * c + idx], bias_ref[4 * c + 2 + idx]], axis=0)
            s.append(jnp.where(kvalid, sc + bias, NEG))
            sink.append(jnp.where(upper, sinks_ref[4 * c + 2 + idx], sinks_ref[4 * c + idx]))
        m = [jnp.maximum(jnp.max(s[t], axis=1, keepdims=True), sink[t]) for t in n]
        p = [jnp.exp(s[t] - m[t]) for t in n]
        den = [jnp.sum(p[t], axis=1, keepdims=True) + jnp.exp(sink[t] - m[t]) for t in n]
        o = []
        for t, (c, idx) in enumerate(chains):
            pv = (jnp.dot(jnp.where(own, p[t], 0.0).astype(BF16), cur[c][2 + idx], preferred_element_type=F32)
                  + jnp.dot(jnp.where(own, 0.0, p[t]).astype(BF16), prev[c][2 + idx], preferred_element_type=F32))
            o.append(pv * (1.0 / den[t]))
        for c in range(A_KV_HEADS):
            both = o[2 * c] + o[2 * c + 1]
            for half in range(2):
                sl = slice((2 * c + half) * LANES, (2 * c + half + 1) * LANES)
                gt = g_ref[rows, sl]
                og_ref[rows, sl] = (both[half * BLOCK:(half + 1) * BLOCK] * (gt * _sigmoid(gt))).astype(BF16)
    resid = jnp.concatenate([jnp.where(i == 0, head_ref[...], x_refs[0][...])]
                            + [ref[...] for ref in x_refs[1:]], axis=0)
    out_ref[...] = resid + jnp.dot(og_ref[...], wout_ref[...], preferred_element_type=F32)


def _attn_prompt_call(sinks, q, kv, g, head, x2d, bias, wout_bf16, nbatch, nblk):
    n = q.shape[0]
    nb = ATTN_BLOCKS_PER_STEP
    nstep = nblk // nb
    rows = nb * BLOCK
    row = lambda b, i: (b * nstep + i, 0)
    prev = lambda b, i: (b * nblk + jnp.maximum(i * nb - 1, 0), 0)
    xrow = [functools.partial(lambda b, i, j: (b * (nblk - 1) + jnp.maximum(i * nb + j - 1, 0), 0), j=j)
            for j in range(nb)]
    return pl.pallas_call(
        _attn_prompt_kernel,
        grid=(nbatch, nstep),
        in_specs=[pl.BlockSpec(memory_space=pltpu.SMEM),
                  pl.BlockSpec((rows, A_WIDTH), row),
                  pl.BlockSpec((rows, 2 * A_KV_WIDTH), row),
                  pl.BlockSpec((BLOCK, 2 * A_KV_WIDTH), prev),
                  pl.BlockSpec((rows, A_WIDTH), row),
                  pl.BlockSpec((BLOCK, D_MODEL), lambda b, i: (0, 0))]
                 + [pl.BlockSpec((BLOCK, D_MODEL), xrow[j]) for j in range(nb)]
                 + [pl.BlockSpec((A_HEADS, BLOCK, BLOCK), lambda b, i: (0, 0, 0)),
                    pl.BlockSpec((A_WIDTH, D_MODEL), lambda b, i: (0, 0))],
        out_specs=pl.BlockSpec((rows, D_MODEL), row),
        out_shape=jax.ShapeDtypeStruct((n, D_MODEL), F32),
        scratch_shapes=[pltpu.VMEM((rows, A_WIDTH), BF16)],
        compiler_params=_cparams(2),
        name="attn_prompt",
    )(sinks, q, kv, kv, g, head, *([x2d] * nb), bias, wout_bf16)


SAMPLE_SB = 8
SAMPLE_KEYS = 2 * BLOCK


def _attn_sample_kernel(sinks_ref, q_ref, kvn_ref, g_ref, h_ref, ck_ref, cv_ref, bias0_ref, bias1_ref,
                        wout_ref, out_ref, nk_ref, nv_ref, og_ref, *, t_new):
    keep = ck_ref.shape[2]
    lo = _iota((1, LANES), 1) < A_HEAD_DIM
    stack = 8 * SUBLANES
    own = (_iota((stack, 1), 0) & (SUBLANES - 1)) // t_new
    piece = _iota((stack, 1), 0) // SUBLANES
    bias_refs = (bias0_ref, bias1_ref)
    n_tile = A_KV_WIDTH // LANES
    pair_ids = range(SAMPLE_SB * t_new // SUBLANES)

    bias_c, bias_n, sink = [], [], []
    for j in range(n_tile):
        heads = slice(8 * j, 8 * j + 8)
        bias_c.append(bias0_ref[heads, :, :keep].reshape(stack, keep))
        bias_n.append([ref[heads, :, keep:keep + SUBLANES].reshape(stack, SUBLANES) for ref in bias_refs])
        col = jnp.zeros((stack, 1), F32)
        for gq in range(8):
            col = jnp.where(piece == gq, sinks_ref[8 * j + gq], col)
        sink.append(col)

    chains = [(p, j, s) for p in pair_ids for j in range(n_tile) for s in range(SUBLANES // t_new)]
    qs = {}
    for p in pair_ids:
        rows = slice(p * SUBLANES, (p + 1) * SUBLANES)
        for j in range(n_tile):
            parts = []
            for gq in range(8):
                t = q_ref[rows, (4 * j + gq // 2) * LANES:(4 * j + gq // 2 + 1) * LANES]
                want_lo = gq < 4
                if (gq % 2 == 0) != want_lo:
                    t = pltpu.roll(t, A_HEAD_DIM, 1)
                parts.append(jnp.where(lo, t, 0.0) if want_lo else jnp.where(lo, 0.0, t))
            qs[p, j] = jnp.concatenate(parts, axis=0).astype(BF16)

    def kv_tiles(p, j, s, base):
        seq = p * (SUBLANES // t_new) + s
        cache = (ck_ref if base == 0 else cv_ref)[seq][j * LANES:(j + 1) * LANES, :].astype(BF16)
        new = kvn_ref[p * SUBLANES:(p + 1) * SUBLANES, base + j * LANES:base + (j + 1) * LANES].astype(BF16)
        return cache, new

    sc, sn = [], []
    for p, j, s in chains:
        kc, kn = kv_tiles(p, j, s, 0)
        sc.append(jnp.dot(qs[p, j], kc, preferred_element_type=F32) + bias_c[j])
        sn.append(_mm_nt(qs[p, j], kn) + bias_n[j][s])
    mx = [jnp.maximum(jnp.maximum(jnp.max(sc[i], axis=1, keepdims=True),
                                  jnp.max(sn[i], axis=1, keepdims=True)), sink[chains[i][1]])
          for i in range(len(chains))]
    pc = [jnp.exp(sc[i] - mx[i]) for i in range(len(chains))]
    pn = [jnp.exp(sn[i] - mx[i]) for i in range(len(chains))]
    den = [jnp.sum(pc[i], axis=1, keepdims=True) + jnp.sum(pn[i], axis=1, keepdims=True)
           + jnp.exp(sink[chains[i][1]] - mx[i]) for i in range(len(chains))]
    outs = {}
    for i, (p, j, s) in enumerate(chains):
        vc, vn = kv_tiles(p, j, s, A_KV_WIDTH)
        o = (_mm_nt(pc[i].astype(BF16), vc)
             + jnp.dot(pn[i].astype(BF16), vn, preferred_element_type=F32)) * (1.0 / den[i])
        outs[p, j] = o if s == 0 else jnp.where(own == s, o, outs[p, j])
    for p in pair_ids:
        rows = slice(p * SUBLANES, (p + 1) * SUBLANES)
        for j in range(n_tile):
            o = outs[p, j]
            for gg in range(4):
                even = o[2 * gg * SUBLANES:(2 * gg + 1) * SUBLANES]
                odd = o[(2 * gg + 1) * SUBLANES:(2 * gg + 2) * SUBLANES]
                if gg < 2:
                    tile = jnp.where(lo, even, pltpu.roll(odd, A_HEAD_DIM, 1))
                else:
                    tile = jnp.where(lo, pltpu.roll(even, A_HEAD_DIM, 1), odd)
                sl = slice((4 * j + gg) * LANES, (4 * j + gg + 1) * LANES)
                gt = g_ref[rows, sl]
                og_ref[rows, sl] = tile * (gt * _sigmoid(gt))
    lane_pos = _iota((SUBLANES, keep), 1)
    old = _iota((1, keep), 1) < keep - t_new
    for p in pair_ids:
        new8 = kvn_ref[p * SUBLANES:(p + 1) * SUBLANES, :]
        for s in range(SUBLANES // t_new):
            seq = p * (SUBLANES // t_new) + s
            sel = (lane_pos == _iota((SUBLANES, keep), 0) + (keep - t_new - s * t_new)).astype(BF16)
            placed = _mm2(_split(new8), sel, TN)
            for ref_in, ref_out, base in ((ck_ref, nk_ref, 0), (cv_ref, nv_ref, A_KV_WIDTH)):
                shifted = pltpu.roll(ref_in[seq], keep - t_new, 1)
                ref_out[seq] = jnp.where(old, shifted, placed[base:base + A_KV_WIDTH])
    out_ref[...] = h_ref[...] + jnp.dot(og_ref[...].astype(BF16), wout_ref[...],
                                        preferred_element_type=F32)


def _attn_sample_call(sinks, q, kv, g, h, cache_k, cache_v, bias0, bias1, wout_bf16, t_new):
    nseq, keep = cache_k.shape[0], cache_k.shape[2]
    rows = SAMPLE_SB * t_new
    row = lambda i: (i, 0)
    cspec = pl.BlockSpec((SAMPLE_SB, A_KV_WIDTH, keep), lambda i: (i, 0, 0))
    bspec = pl.BlockSpec((A_HEADS, SUBLANES, SAMPLE_KEYS), lambda i: (0, 0, 0))
    return pl.pallas_call(
        functools.partial(_attn_sample_kernel, t_new=t_new),
        grid=(nseq // SAMPLE_SB,),
        in_specs=[pl.BlockSpec(memory_space=pltpu.SMEM),
                  pl.BlockSpec((rows, A_WIDTH), row),
                  pl.BlockSpec((rows, 2 * A_KV_WIDTH), row),
                  pl.BlockSpec((rows, A_WIDTH), row),
                  pl.BlockSpec((rows, D_MODEL), row),
                  cspec, cspec, bspec, bspec,
                  pl.BlockSpec((A_WIDTH, D_MODEL), lambda i: (0, 0))],
        out_specs=[pl.BlockSpec((rows, D_MODEL), row), cspec, cspec],
        out_shape=[jax.ShapeDtypeStruct((nseq * t_new, D_MODEL), F32),
                   jax.ShapeDtypeStruct(cache_k.shape, F32),
                   jax.ShapeDtypeStruct(cache_v.shape, F32)],
        scratch_shapes=[pltpu.VMEM((rows, A_WIDTH), F32)],
        compiler_params=_cparams(1),
        name="attn_sample",
    )(sinks, q, kv, g, h, cache_k, cache_v, bias0, bias1, wout_bf16)


def _rwkv_proj_kernel(h_ref, shift_ref, gain_ref, mu_ref, win_ref, w0_ref, w1_ref, w2_ref,
                      a0_ref, a1_ref, a2_ref,
                      r_ref, k_ref, v_ref, g_ref, ld_ref, a_ref, xn_ref, *scratch, seq_len):
    xn = _rmsnorm(h_ref[...], gain_ref[...])
    tm = xn.shape[0]
    rolled = pltpu.roll(xn, 1, 0)
    row = _iota((tm, 1), 0)
    if seq_len is None:
        carry_ref, = scratch

        @pl.when(pl.program_id(1) == 0)
        def _():
            carry_ref[...] = shift_ref[0]

        xprev = jnp.where(row == 0, carry_ref[...], rolled)
        carry_ref[...] = xn[tm - 1:tm, :]
        xn_ref[0] = xn[tm - 1:tm, :]
    else:
        xprev = jnp.where(row % seq_len == 0, shift_ref[...], rolled)
        xn_ref[...] = xn
    dx = xprev - xn

    def mix(c):
        return (xn + dx * mu_ref[c:c + 1, :]).astype(BF16)

    for c, o_ref in enumerate((r_ref, k_ref, v_ref, g_ref)):
        o_ref[...] = jnp.dot(mix(c), win_ref[c], preferred_element_type=F32)
    lw = jnp.tanh(jnp.dot(mix(4), w1_ref[...], preferred_element_type=F32))
    z = w0_ref[...] + jnp.dot(lw.astype(BF16), w2_ref[...], preferred_element_type=F32)
    ld_ref[...] = -math.exp(-0.5) * _sigmoid(z)
    la = jnp.dot(mix(5), a1_ref[...], preferred_element_type=F32)
    a_ref[...] = _sigmoid(a0_ref[...] + jnp.dot(la.astype(BF16), a2_ref[...],
                                                preferred_element_type=F32))


def _rwkv_proj_call(h, shift, p, nbatch, ntile, tm, seq_len):
    n = h.shape[0]
    row = lambda b, i: (b * ntile + i, 0)
    full2 = lambda b, i: (0, 0)
    if seq_len is None:
        shift_spec = pl.BlockSpec((1, 1, D_MODEL), lambda b, i: (b, 0, 0))
        xn_spec = pl.BlockSpec((1, 1, D_MODEL), lambda b, i: (b, 0, 0))
        xn_shape = jax.ShapeDtypeStruct((nbatch, 1, D_MODEL), F32)
        scratch = [pltpu.VMEM((1, D_MODEL), F32)]
    else:
        shift_spec = pl.BlockSpec((tm, D_MODEL), row)
        xn_spec = pl.BlockSpec((tm, D_MODEL), row)
        xn_shape = jax.ShapeDtypeStruct((n, D_MODEL), F32)
        scratch = []
    lora = p["w1"].shape[1]
    big = jax.ShapeDtypeStruct((n, D_MODEL), F32)
    return pl.pallas_call(
        functools.partial(_rwkv_proj_kernel, seq_len=seq_len),
        grid=(nbatch, ntile),
        in_specs=[pl.BlockSpec((tm, D_MODEL), row),
                  shift_spec,
                  pl.BlockSpec((1, D_MODEL), full2),
                  pl.BlockSpec(p["mu"].shape, full2),
                  pl.BlockSpec(p["w_in"].shape, lambda b, i: (0, 0, 0)),
                  pl.BlockSpec((1, D_MODEL), full2),
                  pl.BlockSpec((D_MODEL, lora), full2),
                  pl.BlockSpec((lora, D_MODEL), full2),
                  pl.BlockSpec((1, D_MODEL), full2),
                  pl.BlockSpec((D_MODEL, lora), full2),
                  pl.BlockSpec((lora, D_MODEL), full2)],
        out_specs=[pl.BlockSpec((tm, D_MODEL), row)] * 6 + [xn_spec],
        out_shape=[big] * 6 + [xn_shape],
        scratch_shapes=scratch,
        compiler_params=_cparams(2),
        name="rwkv_proj",
    )(h, shift, p["gain"], p["mu"], p["w_in"], p["w0"], p["w1"], p["w2"], p["a0"], p["a1"], p["a2"])


NN = (((1,), (0,)), ((), ()))
NT = (((1,), (1,)), ((), ()))
TN = (((0,), (0,)), ((), ()))


def _split(x):
    hi = x.astype(BF16)
    return hi, (x - hi.astype(F32)).astype(BF16)


def _mm(a, b, dn):
    return lax.dot_general(a, b, dn, preferred_element_type=F32)


def _mm2(a, b, dn):
    ah, al = a
    if dn == TN:
        both = _mm(jnp.concatenate([ah, al], axis=1), b, dn)
        m = ah.shape[1]
    else:
        both = _mm(jnp.concatenate([ah, al], axis=0), b, dn)
        m = ah.shape[0]
    return both[:m] + both[m:]


def _seg_sums(xs, ones_bf16, exact):
    rows = xs[0].shape[0]
    stacked = jnp.concatenate(xs, axis=0)
    if exact:
        hi, lo = _split(stacked)
        out = _mm(lo, ones_bf16, NN) + _mm(hi, ones_bf16, NN)
    else:
        out = _mm(stacked.astype(BF16), ones_bf16, NN)
    return [out[i * rows:(i + 1) * rows] for i in range(len(xs))]


def _lane_lo():
    return _iota((1, LANES), 1) < R_HEAD_DIM


def _same_half():
    return (_iota((LANES, LANES), 0) < R_HEAD_DIM) == (_iota((LANES, LANES), 1) < R_HEAD_DIM)


def _ones_blk():
    return _same_half().astype(BF16)


def _bd(z):
    lo = _lane_lo()
    return jnp.concatenate([jnp.where(lo, z, 0.0), jnp.where(lo, 0.0, z)], axis=0)


def _bd_swap(z):
    lo = _lane_lo()
    return jnp.concatenate([jnp.where(lo, 0.0, z), jnp.where(lo, z, 0.0)], axis=0)


def _wkv_batch_stage(r, k, v, a, ld, kkg, kag, fillers=(), early=None):
    c = r[0].shape[0]
    pairs = range(len(r))
    half = R_HEAD_DIM
    lo = _lane_lo()
    t_row, t_col = _iota((c, LANES), 0), _iota((c, LANES), 1) & (half - 1)
    strict, incl = t_row > t_col, t_row >= t_col
    tri = (_iota((c, c), 0) >= _iota((c, c), 1)).astype(BF16)
    ones_blk = _ones_blk()
    eye_pair = (_iota((c, LANES), 0) == (_iota((c, LANES), 1) & (half - 1))).astype(F32)

    kkx = [k[j] * kkg[j] for j in pairs]
    ssq = _seg_sums([kkx[j] * kkx[j] for j in pairs], ones_blk, exact=True)
    ld_hi, ld_lo = _split(jnp.concatenate(ld, axis=1))
    cs_all = _mm(tri, ld_lo, NN) + _mm(tri, ld_hi, NN)
    cs = [cs_all[:, j * LANES:(j + 1) * LANES] for j in pairs]
    tot = [cs[j][c - 1:c, :] for j in pairs]
    x_hi, kh_all, bke, vbd_s, lk_a, lk_r, lb_a, lb_r = ([] for _ in range(8))
    for j in pairs:
        kk = kkx[j] * lax.rsqrt(jnp.maximum(ssq[j], 1e-24))
        kh = k[j] * (1.0 + (a[j] - 1.0) * kag[j])
        bv = kk * a[j]
        e_neg = jnp.exp(-cs[j])
        e_end = e_neg * jnp.exp(tot[j])
        at, rt = -kk * jnp.exp(cs[j] - ld[j]), r[j] * jnp.exp(cs[j])
        x_hi.append(jnp.concatenate([at.astype(BF16), rt.astype(BF16)], axis=0))
        kh_all.append(kh)
        y_hi = jnp.concatenate([(bv * e_neg).astype(BF16), (kh * e_neg).astype(BF16)], axis=0)
        bke.append(jnp.concatenate([(bv * e_end).astype(BF16), (kh * e_end).astype(BF16)], axis=0))
        vbd_s.append(_bd(v[j].astype(BF16)))
        both = _mm(jnp.concatenate([jnp.where(lo, x_hi[j], 0.0), jnp.where(lo, 0.0, x_hi[j])], axis=0), y_hi, NT)
        ga_a = pltpu.roll(both[:c], half, 1)
        ga_r = pltpu.roll(both[c:2 * c], half, 1)
        gb_a = both[2 * c:3 * c]
        gb_r = both[3 * c:4 * c]
        lk_a.append(jnp.where(strict, jnp.where(lo, ga_a, gb_a), 0.0))
        lk_r.append(jnp.where(incl, jnp.where(lo, ga_r, gb_r), 0.0))
        lb_a.append(jnp.where(strict, jnp.where(lo, gb_a, ga_a), 0.0))
        lb_r.append(jnp.where(incl, jnp.where(lo, gb_r, ga_r), 0.0))
    fillers = list(fillers)
    if early is not None:
        early.update(x_hi=x_hi)
    pw = lb_a
    acc = [eye_pair + pw[j] for j in pairs]
    pw = [_mm(pw[j].astype(BF16), _bd(pw[j].astype(BF16)), NN) for j in pairs]
    from_v = []
    for j in pairs:
        both = _mm(jnp.concatenate([lk_a[j].astype(BF16), lk_r[j].astype(BF16)], axis=0), vbd_s[j], NN)
        from_v.append((both[:c], both[c:]))
    for _ in range(int(math.log2(c)) - 2):
        both = [_mm(jnp.concatenate([pw[j], acc[j]], axis=0).astype(BF16), _bd(pw[j].astype(BF16)), NN)
                for j in pairs]
        if fillers:
            fillers.pop(0)()
        pw = [both[j][:c] for j in pairs]
        acc = [acc[j] + both[j][c:] for j in pairs]
    tinv = [acc[j] + _mm(acc[j].astype(BF16), _bd(pw[j].astype(BF16)), NN) for j in pairs]
    for fill in fillers:
        fill()
    return dict(x_hi=x_hi, kh=kh_all, lb_r=lb_r, tinv=tinv, from_v=from_v, bke=bke,
                tot=tot, ones_blk=ones_blk)


def _wkv_u(stage, from_state_a):
    return [_mm(stage["tinv"][j].astype(BF16),
                _bd_swap((from_state_a[j] + stage["from_v"][j][0]).astype(BF16)), NN)
            for j in range(len(from_state_a))]


def _wkv_y(stage, from_state_r, u):
    return [from_state_r[j] + stage["from_v"][j][1]
            + _mm(stage["lb_r"][j].astype(BF16), _bd_swap(u[j].astype(BF16)), NN) for j in range(len(u))]


def _wkv_gate(y, r, kh, v, g, rk, lng, lnb, ones_blk):
    n = len(y)
    pairs = range(n)
    inv_n = 1.0 / R_HEAD_DIM
    sums = _seg_sums([r[j] * kh[j] * rk[j] for j in pairs] + list(y), ones_blk, exact=False)
    rkk, mean = sums[:n], [s * inv_n for s in sums[n:]]
    d = [y[j] - mean[j] for j in pairs]
    var = [s * inv_n for s in _seg_sums([d[j] * d[j] for j in pairs], ones_blk, exact=False)]
    return [((d[j] * lax.rsqrt(var[j] + GN_EPS) * lng[j] + lnb[j] + rkk[j] * v[j])
             * (g[j] * _sigmoid(g[j]))).astype(BF16) for j in pairs]


def _wkv_chunk_kernel(r_ref, k_ref, v_ref, a_ref, ld_ref, g_ref, kkg_ref, kag_ref, rk_ref, lng_ref, lnb_ref,
                      z_ref, sout_ref, st_ref):
    c = WKV_CHUNK
    n_sub = r_ref.shape[0] // c
    npair = r_ref.shape[1] // LANES
    pairs = range(npair)
    items = [(s, j) for s in range(n_sub) for j in pairs]

    @pl.when(pl.program_id(2) == 0)
    def _():
        st_ref[...] = jnp.zeros_like(st_ref)

    def tile(ref, it):
        s, j = it
        return ref[s * c:(s + 1) * c, j * LANES:(j + 1) * LANES]

    def par(ref):
        return [ref[:, j * LANES:(j + 1) * LANES] for _, j in items]

    r, k, v, a, ld, g = ([tile(ref, it) for it in items] for ref in (r_ref, k_ref, v_ref, a_ref, ld_ref, g_ref))
    st = [st_ref[j] for j in pairs]
    from_state = {}
    stage = {}

    def state_products(group):
        def run():
            for j in group:
                from_state[j] = _mm(stage["x_hi"][j], st[j].astype(BF16), NN)
        return run

    n_fill = 4
    groups = [list(pairs)[i::n_fill] for i in range(n_fill)]
    stage.update(_wkv_batch_stage(r, k, v, a, ld, par(kkg_ref), par(kag_ref),
                                  fillers=[state_products(grp) for grp in groups if grp], early=stage))
    same_half = _same_half()
    eye_full = _iota((LANES, LANES), 0) == _iota((LANES, LANES), 1)
    ys = []
    pending = None
    for s in range(n_sub):
        idx = [s * npair + j for j in pairs]
        if s > 0:
            for j in pairs:
                from_state[j] = _mm(stage["x_hi"][idx[j]], st[j].astype(BF16), NN)
            ys += _wkv_y(*pending)
        sub = {key: [stage[key][i] for i in idx] for key in ("tinv", "from_v", "lb_r")}
        us = _wkv_u(sub, [from_state[j][:c] for j in pairs])
        upd = [_mm(stage["bke"][idx[j]], jnp.concatenate([us[j], v[idx[j]]], axis=0).astype(BF16), TN)
               for j in pairs]
        pending = (sub, [from_state[j][c:] for j in pairs], us)
        for j in pairs:
            w_col = jnp.sum(jnp.where(eye_full, jnp.exp(stage["tot"][idx[j]]), 0.0), axis=1, keepdims=True)
            st[j] = w_col * st[j] + jnp.where(same_half, upd[j], 0.0)
    ys += _wkv_y(*pending)
    zs = _wkv_gate(ys, r, stage["kh"], v, g, par(rk_ref), par(lng_ref), par(lnb_ref), stage["ones_blk"])
    for i, (s, j) in enumerate(items):
        z_ref[s * c:(s + 1) * c, j * LANES:(j + 1) * LANES] = zs[i]
    for j in pairs:
        st_ref[j] = st[j]
        sout_ref[0, j] = st[j]


def _wkv_chunk_call(r, k, v, a, ld, g, p, nbatch, seq):
    rows = WKV_CHUNK * WKV_CHUNKS_PER_STEP
    nstep = seq // rows
    npair = D_MODEL // LANES
    tile = pl.BlockSpec((rows, D_MODEL), lambda b, j, t: (b * nstep + t, 0))
    par = pl.BlockSpec((1, D_MODEL), lambda b, j, t: (0, 0))
    return pl.pallas_call(
        _wkv_chunk_kernel,
        grid=(nbatch, 1, nstep),
        in_specs=[tile] * 6 + [par] * 5,
        out_specs=[tile, pl.BlockSpec((1, npair, LANES, LANES), lambda b, j, t: (b, 0, 0, 0))],
        out_shape=[jax.ShapeDtypeStruct((nbatch * seq, D_MODEL), BF16),
                   jax.ShapeDtypeStruct((nbatch, npair, LANES, LANES), F32)],
        scratch_shapes=[pltpu.VMEM((npair, LANES, LANES), F32)],
        compiler_params=_cparams(3),
        name="wkv_chunk",
    )(r, k, v, a, ld, g, p["k_k"], p["k_a"], p["r_k"], p["ln_g"], p["ln_b"])


WKV_LANES_UNROLL = 16


def _wkv_lanes_kernel(r_ref, k_ref, v_ref, a_ref, ld_ref, g_ref, kkg_ref, kag_ref, rk_ref, lng_ref, lnb_ref,
                      s_ref, z_ref, sout_ref, prep_ref, y_ref, *, seq_len):
    n = R_HEAD_DIM
    nseq = s_ref.shape[3]
    eye = _iota((LANES, LANES), 0) == _iota((LANES, LANES), 1)

    def column(ref):
        return jnp.sum(jnp.where(eye, ref[...], 0.0), axis=1, keepdims=True)

    kkg, kag, rk, lng, lnb = (column(ref) for ref in (kkg_ref, kag_ref, rk_ref, lng_ref, lnb_ref))

    def token(ref, t):
        return ref[pl.ds(t, nseq, stride=seq_len), :].T

    bonus = []
    for t in range(seq_len):
        r, k, v, a = (token(ref, t) for ref in (r_ref, k_ref, v_ref, a_ref))
        w = jnp.exp(token(ld_ref, t))
        kkx = k * kkg
        kh = k * (1.0 + (a - 1.0) * kag)
        rkk = r * kh * rk
        tiles = []
        for hh in range(2):
            rows = slice(hh * n, (hh + 1) * n)
            nrm = jnp.sqrt(jnp.sum(kkx[rows] * kkx[rows], axis=0, keepdims=True))
            kk = kkx[rows] / jnp.maximum(nrm, 1e-12)
            for q, val in enumerate((w[rows], -kk, kk * a[rows], kh[rows], r[rows], v[rows])):
                prep_ref[q, t, hh] = val
            tiles.append(jnp.sum(rkk[rows], axis=0, keepdims=True) * v[rows])
        bonus.append(tiles)

    for hh in range(2):
        def advance(i, carry, hh=hh):
            for u in range(WKV_LANES_UNROLL):
                vi = i * WKV_LANES_UNROLL + u
                slab = s_ref[hh, vi]
                for t in range(seq_len):
                    w, av, bv, kh, r = (prep_ref[q, t, hh] for q in range(5))
                    vrow = prep_ref[5, t, hh, pl.ds(vi, 1), :]
                    sa = jnp.sum(slab * av, axis=0, keepdims=True)
                    slab = slab * w + sa * bv + vrow * kh
                    y_ref[t, hh, pl.ds(vi, 1), :] = jnp.sum(slab * r, axis=0, keepdims=True)
                sout_ref[hh, vi] = slab
            return carry

        lax.fori_loop(0, n // WKV_LANES_UNROLL, advance, 0)

    for t in range(seq_len):
        parts = []
        for hh in range(2):
            rows = slice(hh * n, (hh + 1) * n)
            y = y_ref[t, hh]
            d = y - jnp.mean(y, axis=0, keepdims=True)
            var = jnp.mean(d * d, axis=0, keepdims=True)
            parts.append(d * lax.rsqrt(var + GN_EPS) * lng[rows] + lnb[rows] + bonus[t][hh])
        g = token(g_ref, t)
        z = jnp.concatenate(parts, axis=0) * (g * _sigmoid(g))
        z_ref[pl.ds(t, nseq, stride=seq_len), :] = z.T


def _wkv_lanes_call(r, k, v, a, ld, g, p, state_hvkb, seq_len):
    n = r.shape[0]
    nseq = state_hvkb.shape[3]
    tile = pl.BlockSpec((n, LANES), lambda j: (0, j))
    par = pl.BlockSpec((1, LANES), lambda j: (0, j))
    sspec = pl.BlockSpec((2, R_HEAD_DIM, R_HEAD_DIM, nseq), lambda j: (j, 0, 0, 0))
    return pl.pallas_call(
        functools.partial(_wkv_lanes_kernel, seq_len=seq_len),
        grid=(D_MODEL // LANES,),
        in_specs=[tile] * 6 + [par] * 5 + [sspec],
        out_specs=[tile, sspec],
        out_shape=[jax.ShapeDtypeStruct((n, D_MODEL), F32),
                   jax.ShapeDtypeStruct(state_hvkb.shape, F32)],
        scratch_shapes=[pltpu.VMEM((6, seq_len, 2, R_HEAD_DIM, nseq), F32),
                        pltpu.VMEM((seq_len, 2, R_HEAD_DIM, nseq), F32)],
        compiler_params=_cparams(1),
        name="wkv_lanes",
    )(r, k, v, a, ld, g, p["k_k"], p["k_a"], p["r_k"], p["ln_g"], p["ln_b"], state_hvkb)


RWKV_OUT_PIECES = 8


def _rwkv_out_kernel(*refs):
    n = RWKV_OUT_PIECES
    z_refs, h_refs, (wout_ref, fg_ref, out_ref) = refs[:n], refs[n:2 * n], refs[2 * n:]
    z = jnp.concatenate([ref[...].astype(BF16) for ref in z_refs], axis=0)
    h = jnp.concatenate([ref[...] for ref in h_refs], axis=0)
    h2 = h + jnp.dot(z, wout_ref[...], preferred_element_type=F32)
    out_ref[...] = _rmsnorm(h2, fg_ref[...])


def _rwkv_out_call(z, h, p, nbatch, ntile, tm, pieces_per_batch, skip):
    piece = tm // RWKV_OUT_PIECES
    dst = lambda b, i: (b * ntile + i, 0)
    pspec = [pl.BlockSpec((piece, D_MODEL),
                          functools.partial(lambda b, i, kk: (b * pieces_per_batch + i * RWKV_OUT_PIECES + skip + kk, 0),
                                            kk=kk))
             for kk in range(RWKV_OUT_PIECES)]
    return pl.pallas_call(
        _rwkv_out_kernel,
        grid=(nbatch, ntile),
        in_specs=pspec + pspec + [pl.BlockSpec((D_MODEL, D_MODEL), lambda b, i: (0, 0)),
                                  pl.BlockSpec((1, D_MODEL), lambda b, i: (0, 0))],
        out_specs=pl.BlockSpec((tm, D_MODEL), dst),
        out_shape=jax.ShapeDtypeStruct((nbatch * ntile * tm, D_MODEL), F32),
        compiler_params=_cparams(2),
        name="rwkv_out",
    )(*([z] * RWKV_OUT_PIECES), *([h] * RWKV_OUT_PIECES), p["w_out"], p["final_gain"])


def _prompt_bucket():
    assert WINDOW == BLOCK
    rel = (np.arange(BLOCK)[:, None] - np.arange(BLOCK)[None, :]) % BLOCK
    return _t5_bucket_np(rel)


def _sample_bucket(keep, t_new, slot):
    t = (np.arange(SUBLANES) % t_new)[:, None]
    j = np.arange(SAMPLE_KEYS)[None, :]
    own = j - keep - slot * t_new
    rel = np.where(j < keep, keep + t - j, t - own)
    ok = (rel >= 0) & (rel < WINDOW) & ((j < keep) | ((own >= 0) & (own < t_new)))
    return np.where(ok, _t5_bucket_np(rel), -1).astype(np.int32)


def kernel(x_prompt, x_sample, cache_win_k, cache_win_v, state_wkv, state_shift, meta_tokens, rel_bias_table, norm_gain, final_gain, attn_w_in, attn_sinks, attn_w_out, rwkv_mu, rwkv_w_in, rwkv_w0, rwkv_w1, rwkv_w2, rwkv_a0, rwkv_a1, rwkv_a2, rwkv_k_k, rwkv_k_a, rwkv_r_k, rwkv_ln_gamma, rwkv_ln_beta, rwkv_w_out):
    nb, seq, _ = x_prompt.shape
    ns, t_new, _ = x_sample.shape
    keep = cache_win_k.shape[2]
    lp = seq + BLOCK
    nblk = lp // BLOCK
    row = lambda x: x.reshape(1, D_MODEL)

    w_in0 = attn_w_in[0].astype(BF16)
    w_out0 = attn_w_out[0].astype(BF16)
    gain0 = row(norm_gain[0])
    sinks = attn_sinks[0]
    rp = dict(gain=row(norm_gain[1]), mu=rwkv_mu[0], w_in=rwkv_w_in[0].astype(BF16),
              w0=row(rwkv_w0[0]), w1=rwkv_w1[0].astype(BF16), w2=rwkv_w2[0].astype(BF16),
              a0=row(rwkv_a0[0]), a1=rwkv_a1[0].astype(BF16), a2=rwkv_a2[0].astype(BF16),
              k_k=row(rwkv_k_k[0]), k_a=row(rwkv_k_a[0]), r_k=row(rwkv_r_k[0]), ln_g=row(rwkv_ln_gamma[0]),
              ln_b=row(rwkv_ln_beta[0]), w_out=rwkv_w_out[0].astype(BF16),
              final_gain=row(final_gain))

    bias_p, *bias_s = _bias_call(rel_bias_table, [_prompt_bucket()]
                                 + [_sample_bucket(keep, t_new, slot) for slot in range(2)])

    head = jnp.concatenate([jnp.zeros((PAD, D_MODEL), F32), meta_tokens.astype(F32)], axis=0)
    xp = x_prompt.reshape(nb * seq, D_MODEL)
    q, kv, g = _attn_proj_call(xp, head, gain0, w_in0, BF16, nb, lp // ATTN_PROJ_ROWS,
                               ATTN_PROJ_ROWS // BLOCK, BLOCK)
    h1 = _attn_prompt_call(sinks, q, kv, g, head, xp, bias_p, w_out0, nb, nblk)
    kv3 = kv.reshape(nb, lp, 2 * A_KV_WIDTH)[:, lp - WINDOW:, :]
    win_k_p = kv3[:, :, :A_KV_WIDTH].reshape(1, nb, WINDOW, A_KV_HEADS, A_HEAD_DIM)
    win_v_p = kv3[:, :, A_KV_WIDTH:].reshape(1, nb, WINDOW, A_KV_HEADS, A_HEAD_DIM)

    shift0 = jnp.zeros((nb, 1, D_MODEL), F32)
    r, k, v, g1, ld, a, xlast = _rwkv_proj_call(h1, shift0, rp, nb, lp // RWKV_PROJ_ROWS, RWKV_PROJ_ROWS, None)
    z, st = _wkv_chunk_call(r, k, v, a, ld, g1, rp, nb, lp)
    y_prompt = _rwkv_out_call(z, h1, rp, nb, seq // RWKV_OUT_ROWS, RWKV_OUT_ROWS,
                              lp * RWKV_OUT_PIECES // RWKV_OUT_ROWS, BLOCK * RWKV_OUT_PIECES // RWKV_OUT_ROWS)
    y_prompt = y_prompt.reshape(nb, seq, D_MODEL)
    st = st.reshape(nb, D_MODEL // LANES, 2, R_HEAD_DIM, 2, R_HEAD_DIM)
    st = jnp.stack([st[:, :, 0, :, 0, :], st[:, :, 1, :, 1, :]], axis=2)
    wkv_p = jnp.swapaxes(st, -1, -2).reshape(1, nb, R_HEADS, R_HEAD_DIM, R_HEAD_DIM)
    shift_p = xlast.reshape(1, nb, D_MODEL)

    xs = x_sample.reshape(ns * t_new, D_MODEL)
    qs, kvs, gs = _attn_proj_call(xs, None, gain0, w_in0, F32, 1, 1, 1, ns * t_new)
    ck = jnp.swapaxes(cache_win_k[0].reshape(ns, keep, A_KV_WIDTH), 1, 2)
    cv = jnp.swapaxes(cache_win_v[0].reshape(ns, keep, A_KV_WIDTH), 1, 2)
    h1s, nk, nv = _attn_sample_call(sinks, qs, kvs, gs, xs, ck, cv, bias_s[0], bias_s[1], w_out0, t_new)
    win_k_s = jnp.swapaxes(nk, 1, 2).reshape(1, ns, keep, A_KV_HEADS, A_HEAD_DIM)
    win_v_s = jnp.swapaxes(nv, 1, 2).reshape(1, ns, keep, A_KV_HEADS, A_HEAD_DIM)

    shift_rows = jnp.repeat(state_shift[0], t_new, axis=0)
    tms = 256
    rs, ks, vs, g1s, lds, as_, xns = _rwkv_proj_call(h1s, shift_rows, rp, 1, ns * t_new // tms, tms, t_new)
    zs, st_s = _wkv_lanes_call(rs, ks, vs, as_, lds, g1s, rp, jnp.transpose(state_wkv[0], (1, 2, 3, 0)), t_new)
    y_sample = _rwkv_out_call(zs, h1s, rp, 1, 1, ns * t_new, RWKV_OUT_PIECES, 0)
    y_sample = y_sample.reshape(ns, t_new, D_MODEL)
    wkv_s = jnp.transpose(st_s, (3, 0, 1, 2))[None]
    shift_s = xns.reshape(ns, t_new, D_MODEL)[:, t_new - 1][None]

    return (y_prompt, y_sample, win_k_p, win_v_p, wkv_p, shift_p, win_k_s, win_v_s, wkv_s, shift_s)
```

```python
import functools
import math

import numpy as np
import jax
import jax.numpy as jnp
from jax import lax
from jax.experimental import pallas as pl
from jax.experimental.pallas import tpu as pltpu

F32 = jnp.float32
BF16 = jnp.bfloat16

D_MODEL = 1024
N_META = 16
RMS_EPS = 1e-6
A_HEADS = 16
A_KV_HEADS = 4
A_HEAD_DIM = 64
A_WIDTH = A_HEADS * A_HEAD_DIM
A_KV_WIDTH = A_KV_HEADS * A_HEAD_DIM
WINDOW = 128
BLOCK = 128
N_BUCKETS = 32
MAX_DISTANCE = 128
R_HEAD_DIM = 64
R_HEADS = D_MODEL // R_HEAD_DIM
GN_EPS = 64e-5

LANES = 128
SUBLANES = 8
PAD = BLOCK - N_META
NEG = -1e30
WKV_CHUNK = 64
WKV_CHUNKS_PER_STEP = 3
ATTN_BLOCKS_PER_STEP = 3
ATTN_PROJ_ROWS = 384
RWKV_PROJ_ROWS = 528
RWKV_OUT_ROWS = 1024
VMEM_LIMIT = 56 * 1024 * 1024


def _cparams(n_axes):
    return pltpu.CompilerParams(dimension_semantics=("arbitrary",) * n_axes,
                                vmem_limit_bytes=VMEM_LIMIT)


def _rmsnorm(x, gain):
    return x * lax.rsqrt(jnp.mean(x * x, axis=-1, keepdims=True) + RMS_EPS) * gain


def _sigmoid(x):
    return 1.0 / (1.0 + jnp.exp(-x))


def _iota(shape, dim):
    return lax.broadcasted_iota(jnp.int32, shape, dim)


def _t5_bucket_np(rel):
    n = np.maximum(rel, 0)
    max_exact = N_BUCKETS // 2
    nf = np.maximum(n, max_exact).astype(np.float32)
    scale = np.float32(math.log(MAX_DISTANCE / max_exact))
    large = max_exact + (np.log(nf / np.float32(max_exact)) / scale
                         * np.float32(N_BUCKETS - max_exact)).astype(np.int32)
    large = np.minimum(large, N_BUCKETS - 1)
    return np.where(n < max_exact, n, large).astype(np.int32)


def _bias_kernel(table_ref, *refs):
    h = pl.program_id(0)
    n = len(refs) // 2
    for bucket_ref, out_ref in zip(refs[:n], refs[n:]):
        bk = bucket_ref[...]
        acc = jnp.full(bk.shape, NEG, F32)
        for b in range(N_BUCKETS):
            acc = jnp.where(bk == b, table_ref[b, h], acc)
        out_ref[0] = acc


def _bias_call(table, buckets_np):
    return pl.pallas_call(
        _bias_kernel,
        grid=(A_HEADS,),
        in_specs=[pl.BlockSpec(memory_space=pltpu.SMEM)]
                 + [pl.BlockSpec(bk.shape, lambda h: (0, 0)) for bk in buckets_np],
        out_specs=[pl.BlockSpec((1,) + bk.shape, lambda h: (h, 0, 0)) for bk in buckets_np],
        out_shape=[jax.ShapeDtypeStruct((A_HEADS,) + bk.shape, F32) for bk in buckets_np],
        compiler_params=_cparams(1),
        name="bias_expand",
    )(table, *(jnp.asarray(bk) for bk in buckets_np))


def _attn_proj_kernel(head_ref, *refs, n_piece):
    x_refs, (gain_ref, w_ref, q_ref, kv_ref, g_ref) = refs[:n_piece], refs[n_piece:]
    first = x_refs[0][...]
    if head_ref is not None:
        first = jnp.where(pl.program_id(1) == 0, head_ref[...], first)
    x = jnp.concatenate([first] + [ref[...] for ref in x_refs[1:]], axis=0)
    xn = _rmsnorm(x, gain_ref[...])
    proj = jnp.dot(xn.astype(BF16), w_ref[...], preferred_element_type=F32)
    q_ref[...] = (proj[:, :A_WIDTH] * (A_HEAD_DIM ** -0.5)).astype(q_ref.dtype)
    kv_ref[...] = proj[:, A_WIDTH:A_WIDTH + 2 * A_KV_WIDTH]
    g_ref[...] = proj[:, A_WIDTH + 2 * A_KV_WIDTH:]


def _attn_proj_call(x2d, head, gain, w_bf16, q_dtype, nbatch, ntile, n_piece, piece):
    tm = n_piece * piece
    wcols = w_bf16.shape[1]
    per_seq = x2d.shape[0] // (nbatch * piece)
    lead = 0 if head is None else 1
    dst = lambda b, i: (b * ntile + i, 0)
    xspec = [pl.BlockSpec((piece, D_MODEL),
                          functools.partial(lambda b, i, kk: (b * per_seq + jnp.maximum(i * n_piece + kk - lead, 0), 0),
                                            kk=kk))
             for kk in range(n_piece)]
    kern = functools.partial(_attn_proj_kernel, n_piece=n_piece)
    operands = [x2d] * n_piece + [gain, w_bf16]
    if head is None:
        kern = functools.partial(kern, None)
        head_spec = []
    else:
        head_spec = [pl.BlockSpec((piece, D_MODEL), lambda b, i: (0, 0))]
        operands = [head] + operands
    n = nbatch * ntile * tm
    return pl.pallas_call(
        kern,
        grid=(nbatch, ntile),
        in_specs=head_spec + xspec + [pl.BlockSpec((1, D_MODEL), lambda b, i: (0, 0)),
                                      pl.BlockSpec((D_MODEL, wcols), lambda b, i: (0, 0))],
        out_specs=[pl.BlockSpec((tm, A_WIDTH), dst),
                   pl.BlockSpec((tm, 2 * A_KV_WIDTH), dst),
                   pl.BlockSpec((tm, A_WIDTH), dst)],
        out_shape=[jax.ShapeDtypeStruct((n, A_WIDTH), q_dtype),
                   jax.ShapeDtypeStruct((n, 2 * A_KV_WIDTH), F32),
                   jax.ShapeDtypeStruct((n, A_WIDTH), F32)],
        compiler_params=_cparams(2),
        name="attn_proj",
    )(*operands)


def _padded_kv_tiles(kv, c):
    lo = _iota((1, LANES), 1) < A_HEAD_DIM
    j = c // 2
    out = []
    for base in (0, A_KV_WIDTH):
        t = kv[:, base + j * LANES: base + (j + 1) * LANES]
        tr = pltpu.roll(t, A_HEAD_DIM, 1)
        if c % 2 == 0:
            even, odd = jnp.where(lo, t, 0.0), jnp.where(lo, 0.0, tr)
        else:
            even, odd = jnp.where(lo, tr, 0.0), jnp.where(lo, 0.0, t)
        out += [even.astype(BF16), odd.astype(BF16)]
    return out


def _mm_nt(a, b):
    return lax.dot_general(a, b, (((1,), (1,)), ((), ())), preferred_element_type=F32)


def _attn_prompt_kernel(sinks_ref, q_ref, kvc_ref, kvp_ref, g_ref, head_ref, *refs):
    nb = ATTN_BLOCKS_PER_STEP
    x_refs, (bias_in_ref, wout_ref, out_ref, og_ref, bias_ref) = refs[:nb], refs[nb:]
    i = pl.program_id(1)
    stack = 2 * BLOCK
    row, col = _iota((stack, BLOCK), 0) & (BLOCK - 1), _iota((stack, BLOCK), 1)
    upper = _iota((stack, 1), 0) >= BLOCK
    own = col <= row

    @pl.when((pl.program_id(0) == 0) & (i == 0))
    def _():
        col1 = _iota((BLOCK, BLOCK), 1)
        own1, real = col1 <= _iota((BLOCK, BLOCK), 0), col1 >= PAD
        for h in range(A_HEADS):
            b = bias_in_ref[h]
            late = jnp.where(real, b, NEG)
            bias_ref[h, 0:BLOCK, :] = jnp.where(own1, late, NEG)
            bias_ref[h, BLOCK:2 * BLOCK, :] = jnp.where(own1, b, late)
            bias_ref[h, 2 * BLOCK:, :] = b

    chains = [(c, idx) for c in range(A_KV_HEADS) for idx in range(2)]
    n = range(len(chains))
    cur = [_padded_kv_tiles(kvp_ref[...], c) for c in range(A_KV_HEADS)]
    for j in range(nb):
        rows = slice(j * BLOCK, (j + 1) * BLOCK)
        brow = pl.ds(pl.multiple_of(jnp.minimum(i * nb + j, 2) * BLOCK, BLOCK), BLOCK)
        prev, cur = cur, [_padded_kv_tiles(kvc_ref[rows, :], c) for c in range(A_KV_HEADS)]
        s, sink = [], []
        for c, idx in chains:
            q2 = q_ref[rows, 2 * c * LANES:(2 * c + 2) * LANES]
            q2 = jnp.concatenate([q2[:, :LANES], q2[:, LANES:]], axis=0)
            sc = jnp.where(own, _mm_nt(q2, cur[c][idx]), _mm_nt(q2, prev[c][idx]))
            bias = jnp.concatenate([bias_ref[4 * c + idx, brow, :], bias_ref[4 * c + 2 + idx, brow, :]], axis=0)
            s.append(sc + bias)
            sink.append(jnp.where(upper, sinks_ref[4 * c + 2 + idx], sinks_ref[4 * c + idx]))
        m = [jnp.maximum(jnp.max(s[t], axis=1, keepdims=True), sink[t]) for t in n]
        p = [jnp.exp(s[t] - m[t]) for t in n]
        den = [jnp.sum(p[t], axis=1, keepdims=True) + jnp.exp(sink[t] - m[t]) for t in n]
        o = []
        for t, (c, idx) in enumerate(chains):
            pv = (jnp.dot(jnp.where(own, p[t], 0.0).astype(BF16), cur[c][2 + idx], preferred_element_type=F32)
                  + jnp.dot(jnp.where(own, 0.0, p[t]).astype(BF16), prev[c][2 + idx], preferred_element_type=F32))
            o.append(pv * (1.0 / den[t]))
        for c in range(A_KV_HEADS):
            both = o[2 * c] + o[2 * c + 1]
            for half in range(2):
                sl = slice((2 * c + half) * LANES, (2 * c + half + 1) * LANES)
                gt = g_ref[rows, sl]
                og_ref[rows, sl] = (both[half * BLOCK:(half + 1) * BLOCK] * (gt * _sigmoid(gt))).astype(BF16)
    resid = jnp.concatenate([jnp.where(i == 0, head_ref[...], x_refs[0][...])]
                            + [ref[...] for ref in x_refs[1:]], axis=0)
    out_ref[...] = resid + jnp.dot(og_ref[...], wout_ref[...], preferred_element_type=F32)


def _attn_prompt_call(sinks, q, kv, g, head, x2d, bias, wout_bf16, nbatch, nblk):
    n = q.shape[0]
    nb = ATTN_BLOCKS_PER_STEP
    nstep = nblk // nb
    rows = nb * BLOCK
    row = lambda b, i: (b * nstep + i, 0)
    prev = lambda b, i: (b * nblk + jnp.maximum(i * nb - 1, 0), 0)
    xrow = [functools.partial(lambda b, i, j: (b * (nblk - 1) + jnp.maximum(i * nb + j - 1, 0), 0), j=j)
            for j in range(nb)]
    return pl.pallas_call(
        _attn_prompt_kernel,
        grid=(nbatch, nstep),
        in_specs=[pl.BlockSpec(memory_space=pltpu.SMEM),
                  pl.BlockSpec((rows, A_WIDTH), row),
                  pl.BlockSpec((rows, 2 * A_KV_WIDTH), row),
                  pl.BlockSpec((BLOCK, 2 * A_KV_WIDTH), prev),
                  pl.BlockSpec((rows, A_WIDTH), row),
                  pl.BlockSpec((BLOCK, D_MODEL), lambda b, i: (0, 0))]
                 + [pl.BlockSpec((BLOCK, D_MODEL), xrow[j]) for j in range(nb)]
                 + [pl.BlockSpec((A_HEADS, BLOCK, BLOCK), lambda b, i: (0, 0, 0)),
                    pl.BlockSpec((A_WIDTH, D_MODEL), lambda b, i: (0, 0))],
        out_specs=pl.BlockSpec((rows, D_MODEL), row),
        out_shape=jax.ShapeDtypeStruct((n, D_MODEL), F32),
        scratch_shapes=[pltpu.VMEM((rows, A_WIDTH), BF16), pltpu.VMEM((A_HEADS, 3 * BLOCK, BLOCK), F32)],
        compiler_params=_cparams(2),
        name="attn_prompt",
    )(sinks, q, kv, kv, g, head, *([x2d] * nb), bias, wout_bf16)


SAMPLE_SB = 8
SAMPLE_KEYS = 2 * BLOCK


def _attn_sample_kernel(sinks_ref, q_ref, kvn_ref, g_ref, h_ref, ck_ref, cv_ref, bias0_ref, bias1_ref,
                        wout_ref, out_ref, nk_ref, nv_ref, og_ref, *, t_new):
    keep = ck_ref.shape[2]
    lo = _iota((1, LANES), 1) < A_HEAD_DIM
    stack = 8 * SUBLANES
    own = (_iota((stack, 1), 0) & (SUBLANES - 1)) // t_new
    piece = _iota((stack, 1), 0) // SUBLANES
    bias_refs = (bias0_ref, bias1_ref)
    n_tile = A_KV_WIDTH // LANES
    pair_ids = range(SAMPLE_SB * t_new // SUBLANES)

    bias_c, bias_n, sink = [], [], []
    for j in range(n_tile):
        heads = slice(8 * j, 8 * j + 8)
        bias_c.append(bias0_ref[heads, :, :keep].reshape(stack, keep))
        bias_n.append([ref[heads, :, keep:keep + SUBLANES].reshape(stack, SUBLANES) for ref in bias_refs])
        col = jnp.zeros((stack, 1), F32)
        for gq in range(8):
            col = jnp.where(piece == gq, sinks_ref[8 * j + gq], col)
        sink.append(col)

    chains = [(p, j, s) for p in pair_ids for j in range(n_tile) for s in range(SUBLANES // t_new)]
    qs = {}
    for p in pair_ids:
        rows = slice(p * SUBLANES, (p + 1) * SUBLANES)
        for j in range(n_tile):
            parts = []
            for gq in range(8):
                t = q_ref[rows, (4 * j + gq // 2) * LANES:(4 * j + gq // 2 + 1) * LANES]
                want_lo = gq < 4
                if (gq % 2 == 0) != want_lo:
                    t = pltpu.roll(t, A_HEAD_DIM, 1)
                parts.append(jnp.where(lo, t, 0.0) if want_lo else jnp.where(lo, 0.0, t))
            qs[p, j] = jnp.concatenate(parts, axis=0).astype(BF16)

    def kv_tiles(p, j, s, base):
        seq = p * (SUBLANES // t_new) + s
        cache = (ck_ref if base == 0 else cv_ref)[seq][j * LANES:(j + 1) * LANES, :].astype(BF16)
        new = kvn_ref[p * SUBLANES:(p + 1) * SUBLANES, base + j * LANES:base + (j + 1) * LANES].astype(BF16)
        return cache, new

    sc, sn = [], []
    for p, j, s in chains:
        kc, kn = kv_tiles(p, j, s, 0)
        sc.append(jnp.dot(qs[p, j], kc, preferred_element_type=F32) + bias_c[j])
        sn.append(_mm_nt(qs[p, j], kn) + bias_n[j][s])
    mx = [jnp.maximum(jnp.maximum(jnp.max(sc[i], axis=1, keepdims=True),
                                  jnp.max(sn[i], axis=1, keepdims=True)), sink[chains[i][1]])
          for i in range(len(chains))]
    pc = [jnp.exp(sc[i] - mx[i]) for i in range(len(chains))]
    pn = [jnp.exp(sn[i] - mx[i]) for i in range(len(chains))]
    den = [jnp.sum(pc[i], axis=1, keepdims=True) + jnp.sum(pn[i], axis=1, keepdims=True)
           + jnp.exp(sink[chains[i][1]] - mx[i]) for i in range(len(chains))]
    outs = {}
    for i, (p, j, s) in enumerate(chains):
        vc, vn = kv_tiles(p, j, s, A_KV_WIDTH)
        o = (_mm_nt(pc[i].astype(BF16), vc)
             + jnp.dot(pn[i].astype(BF16), vn, preferred_element_type=F32)) * (1.0 / den[i])
        outs[p, j] = o if s == 0 else jnp.where(own == s, o, outs[p, j])
    for p in pair_ids:
        rows = slice(p * SUBLANES, (p + 1) * SUBLANES)
        for j in range(n_tile):
            o = outs[p, j]
            for gg in range(4):
                even = o[2 * gg * SUBLANES:(2 * gg + 1) * SUBLANES]
                odd = o[(2 * gg + 1) * SUBLANES:(2 * gg + 2) * SUBLANES]
                if gg < 2:
                    tile = jnp.where(lo, even, pltpu.roll(odd, A_HEAD_DIM, 1))
                else:
                    tile = jnp.where(lo, pltpu.roll(even, A_HEAD_DIM, 1), odd)
                sl = slice((4 * j + gg) * LANES, (4 * j + gg + 1) * LANES)
                gt = g_ref[rows, sl]
                og_ref[rows, sl] = tile * (gt * _sigmoid(gt))
    lane_pos = _iota((SUBLANES, keep), 1)
    old = _iota((1, keep), 1) < keep - t_new
    for p in pair_ids:
        new8 = kvn_ref[p * SUBLANES:(p + 1) * SUBLANES, :]
        for s in range(SUBLANES // t_new):
            seq = p * (SUBLANES // t_new) + s
            sel = (lane_pos == _iota((SUBLANES, keep), 0) + (keep - t_new - s * t_new)).astype(BF16)
            placed = _mm2(_split(new8), sel, TN)
            for ref_in, ref_out, base in ((ck_ref, nk_ref, 0), (cv_ref, nv_ref, A_KV_WIDTH)):
                shifted = pltpu.roll(ref_in[seq], keep - t_new, 1)
                ref_out[seq] = jnp.where(old, shifted, placed[base:base + A_KV_WIDTH])
    out_ref[...] = h_ref[...] + jnp.dot(og_ref[...].astype(BF16), wout_ref[...],
                                        preferred_element_type=F32)


def _attn_sample_call(sinks, q, kv, g, h, cache_k, cache_v, bias0, bias1, wout_bf16, t_new):
    nseq, keep = cache_k.shape[0], cache_k.shape[2]
    rows = SAMPLE_SB * t_new
    row = lambda i: (i, 0)
    cspec = pl.BlockSpec((SAMPLE_SB, A_KV_WIDTH, keep), lambda i: (i, 0, 0))
    bspec = pl.BlockSpec((A_HEADS, SUBLANES, SAMPLE_KEYS), lambda i: (0, 0, 0))
    return pl.pallas_call(
        functools.partial(_attn_sample_kernel, t_new=t_new),
        grid=(nseq // SAMPLE_SB,),
        in_specs=[pl.BlockSpec(memory_space=pltpu.SMEM),
                  pl.BlockSpec((rows, A_WIDTH), row),
                  pl.BlockSpec((rows, 2 * A_KV_WIDTH), row),
                  pl.BlockSpec((rows, A_WIDTH), row),
                  pl.BlockSpec((rows, D_MODEL), row),
                  cspec, cspec, bspec, bspec,
                  pl.BlockSpec((A_WIDTH, D_MODEL), lambda i: (0, 0))],
        out_specs=[pl.BlockSpec((rows, D_MODEL), row), cspec, cspec],
        out_shape=[jax.ShapeDtypeStruct((nseq * t_new, D_MODEL), F32),
                   jax.ShapeDtypeStruct(cache_k.shape, F32),
                   jax.ShapeDtypeStruct(cache_v.shape, F32)],
        scratch_shapes=[pltpu.VMEM((rows, A_WIDTH), F32)],
        compiler_params=_cparams(1),
        name="attn_sample",
    )(sinks, q, kv, g, h, cache_k, cache_v, bias0, bias1, wout_bf16)


def _rwkv_proj_kernel(h_ref, shift_ref, gain_ref, mu_ref, win_ref, w0_ref, w1_ref, w2_ref,
                      a0_ref, a1_ref, a2_ref,
                      r_ref, k_ref, v_ref, g_ref, ld_ref, a_ref, xn_ref, *scratch, seq_len):
    xn = _rmsnorm(h_ref[...], gain_ref[...])
    tm = xn.shape[0]
    rolled = pltpu.roll(xn, 1, 0)
    row = _iota((tm, 1), 0)
    if seq_len is None:
        carry_ref, = scratch

        @pl.when(pl.program_id(1) == 0)
        def _():
            carry_ref[...] = shift_ref[0]

        xprev = jnp.where(row == 0, carry_ref[...], rolled)
        carry_ref[...] = xn[tm - 1:tm, :]
        xn_ref[0] = xn[tm - 1:tm, :]
    else:
        xprev = jnp.where(row % seq_len == 0, shift_ref[...], rolled)
        xn_ref[...] = xn
    dx = xprev - xn

    def mix(c):
        return (xn + dx * mu_ref[c:c + 1, :]).astype(BF16)

    for c, o_ref in enumerate((r_ref, k_ref, v_ref, g_ref)):
        o_ref[...] = jnp.dot(mix(c), win_ref[c], preferred_element_type=F32)
    lw = jnp.tanh(jnp.dot(mix(4), w1_ref[...], preferred_element_type=F32))
    z = w0_ref[...] + jnp.dot(lw.astype(BF16), w2_ref[...], preferred_element_type=F32)
    ld_ref[...] = -math.exp(-0.5) * _sigmoid(z)
    la = jnp.dot(mix(5), a1_ref[...], preferred_element_type=F32)
    a_ref[...] = _sigmoid(a0_ref[...] + jnp.dot(la.astype(BF16), a2_ref[...],
                                                preferred_element_type=F32))


def _rwkv_proj_call(h, shift, p, nbatch, ntile, tm, seq_len):
    n = h.shape[0]
    row = lambda b, i: (b * ntile + i, 0)
    full2 = lambda b, i: (0, 0)
    if seq_len is None:
        shift_spec = pl.BlockSpec((1, 1, D_MODEL), lambda b, i: (b, 0, 0))
        xn_spec = pl.BlockSpec((1, 1, D_MODEL), lambda b, i: (b, 0, 0))
        xn_shape = jax.ShapeDtypeStruct((nbatch, 1, D_MODEL), F32)
        scratch = [pltpu.VMEM((1, D_MODEL), F32)]
    else:
        shift_spec = pl.BlockSpec((tm, D_MODEL), row)
        xn_spec = pl.BlockSpec((tm, D_MODEL), row)
        xn_shape = jax.ShapeDtypeStruct((n, D_MODEL), F32)
        scratch = []
    lora = p["w1"].shape[1]
    big = jax.ShapeDtypeStruct((n, D_MODEL), F32)
    return pl.pallas_call(
        functools.partial(_rwkv_proj_kernel, seq_len=seq_len),
        grid=(nbatch, ntile),
        in_specs=[pl.BlockSpec((tm, D_MODEL), row),
                  shift_spec,
                  pl.BlockSpec((1, D_MODEL), full2),
                  pl.BlockSpec(p["mu"].shape, full2),
                  pl.BlockSpec(p["w_in"].shape, lambda b, i: (0, 0, 0)),
                  pl.BlockSpec((1, D_MODEL), full2),
                  pl.BlockSpec((D_MODEL, lora), full2),
                  pl.BlockSpec((lora, D_MODEL), full2),
                  pl.BlockSpec((1, D_MODEL), full2),
                  pl.BlockSpec((D_MODEL, lora), full2),
                  pl.BlockSpec((lora, D_MODEL), full2)],
        out_specs=[pl.BlockSpec((tm, D_MODEL), row)] * 6 + [xn_spec],
        out_shape=[big] * 6 + [xn_shape],
        scratch_shapes=scratch,
        compiler_params=_cparams(2),
        name="rwkv_proj",
    )(h, shift, p["gain"], p["mu"], p["w_in"], p["w0"], p["w1"], p["w2"], p["a0"], p["a1"], p["a2"])


NN = (((1,), (0,)), ((), ()))
NT = (((1,), (1,)), ((), ()))
TN = (((0,), (0,)), ((), ()))


def _split(x):
    hi = x.astype(BF16)
    return hi, (x - hi.astype(F32)).astype(BF16)


def _mm(a, b, dn):
    return lax.dot_general(a, b, dn, preferred_element_type=F32)


def _mm2(a, b, dn):
    ah, al = a
    if dn == TN:
        both = _mm(jnp.concatenate([ah, al], axis=1), b, dn)
        m = ah.shape[1]
    else:
        both = _mm(jnp.concatenate([ah, al], axis=0), b, dn)
        m = ah.shape[0]
    return both[:m] + both[m:]


def _seg_sums(xs, ones_bf16, exact):
    rows = xs[0].shape[0]
    stacked = jnp.concatenate(xs, axis=0)
    if exact:
        hi, lo = _split(stacked)
        out = _mm(lo, ones_bf16, NN) + _mm(hi, ones_bf16, NN)
    else:
        out = _mm(stacked.astype(BF16), ones_bf16, NN)
    return [out[i * rows:(i + 1) * rows] for i in range(len(xs))]


def _lane_lo():
    return _iota((1, LANES), 1) < R_HEAD_DIM


def _same_half():
    return (_iota((LANES, LANES), 0) < R_HEAD_DIM) == (_iota((LANES, LANES), 1) < R_HEAD_DIM)


def _ones_blk():
    return _same_half().astype(BF16)


def _bd(z):
    lo = _lane_lo()
    return jnp.concatenate([jnp.where(lo, z, 0.0), jnp.where(lo, 0.0, z)], axis=0)


def _bd_swap(z):
    lo = _lane_lo()
    return jnp.concatenate([jnp.where(lo, 0.0, z), jnp.where(lo, z, 0.0)], axis=0)


def _wkv_batch_stage(r, k, v, a, ld, kkg, kag, fillers=(), early=None):
    c = r[0].shape[0]
    pairs = range(len(r))
    half = R_HEAD_DIM
    lo = _lane_lo()
    t_row, t_col = _iota((c, LANES), 0), _iota((c, LANES), 1) & (half - 1)
    strict, incl = t_row > t_col, t_row >= t_col
    tri = (_iota((c, c), 0) >= _iota((c, c), 1)).astype(BF16)
    ones_blk = _ones_blk()
    eye_pair = (_iota((c, LANES), 0) == (_iota((c, LANES), 1) & (half - 1))).astype(F32)

    kkx = [k[j] * kkg[j] for j in pairs]
    ssq = _seg_sums([kkx[j] * kkx[j] for j in pairs], ones_blk, exact=True)
    ld_hi, ld_lo = _split(jnp.concatenate(ld, axis=1))
    cs_all = _mm(tri, ld_lo, NN) + _mm(tri, ld_hi, NN)
    cs = [cs_all[:, j * LANES:(j + 1) * LANES] for j in pairs]
    tot = [cs[j][c - 1:c, :] for j in pairs]
    x_hi, kh_all, bke, vbd_s, lk_a, lk_r, lb_a, lb_r = ([] for _ in range(8))
    for j in pairs:
        kk = kkx[j] * lax.rsqrt(jnp.maximum(ssq[j], 1e-24))
        kh = k[j] * (1.0 + (a[j] - 1.0) * kag[j])
        bv = kk * a[j]
        e_neg = jnp.exp(-cs[j])
        e_end = e_neg * jnp.exp(tot[j])
        at, rt = -kk * jnp.exp(cs[j] - ld[j]), r[j] * jnp.exp(cs[j])
        x_hi.append(jnp.concatenate([at.astype(BF16), rt.astype(BF16)], axis=0))
        kh_all.append(kh)
        y_hi = jnp.concatenate([(bv * e_neg).astype(BF16), (kh * e_neg).astype(BF16)], axis=0)
        bke.append(jnp.concatenate([(bv * e_end).astype(BF16), (kh * e_end).astype(BF16)], axis=0))
        vbd_s.append(_bd(v[j].astype(BF16)))
        both = _mm(jnp.concatenate([jnp.where(lo, x_hi[j], 0.0), jnp.where(lo, 0.0, x_hi[j])], axis=0), y_hi, NT)
        ga_a = pltpu.roll(both[:c], half, 1)
        ga_r = pltpu.roll(both[c:2 * c], half, 1)
        gb_a = both[2 * c:3 * c]
        gb_r = both[3 * c:4 * c]
        lk_a.append(jnp.where(strict, jnp.where(lo, ga_a, gb_a), 0.0))
        lk_r.append(jnp.where(incl, jnp.where(lo, ga_r, gb_r), 0.0))
        lb_a.append(jnp.where(strict, jnp.where(lo, gb_a, ga_a), 0.0))
        lb_r.append(jnp.where(incl, jnp.where(lo, gb_r, ga_r), 0.0))
    fillers = list(fillers)
    if early is not None:
        early.update(x_hi=x_hi)
    pw = lb_a
    acc = [eye_pair + pw[j] for j in pairs]
    pw = [_mm(pw[j].astype(BF16), _bd(pw[j].astype(BF16)), NN) for j in pairs]
    from_v = []
    for j in pairs:
        both = _mm(jnp.concatenate([lk_a[j].astype(BF16), lk_r[j].astype(BF16)], axis=0), vbd_s[j], NN)
        from_v.append((both[:c], both[c:]))
    for _ in range(int(math.log2(c)) - 2):
        both = [_mm(jnp.concatenate([pw[j], acc[j]], axis=0).astype(BF16), _bd(pw[j].astype(BF16)), NN)
                for j in pairs]
        if fillers:
            fillers.pop(0)()
        pw = [both[j][:c] for j in pairs]
        acc = [acc[j] + both[j][c:] for j in pairs]
    tinv = [acc[j] + _mm(acc[j].astype(BF16), _bd(pw[j].astype(BF16)), NN) for j in pairs]
    for fill in fillers:
        fill()
    return dict(x_hi=x_hi, kh=kh_all, lb_r=lb_r, tinv=tinv, from_v=from_v, bke=bke,
                tot=tot, ones_blk=ones_blk)


def _wkv_finish(stage, from_state_a, from_state_r):
    pairs = range(len(from_state_a))
    u = [_mm(stage["tinv"][j].astype(BF16),
             _bd_swap((from_state_a[j] + stage["from_v"][j][0]).astype(BF16)), NN) for j in pairs]
    y = [from_state_r[j] + stage["from_v"][j][1]
         + _mm(stage["lb_r"][j].astype(BF16), _bd_swap(u[j].astype(BF16)), NN) for j in pairs]
    return u, y


def _wkv_gate(y, r, kh, v, g, rk, lng, lnb, ones_blk):
    n = len(y)
    pairs = range(n)
    inv_n = 1.0 / R_HEAD_DIM
    sums = _seg_sums([r[j] * kh[j] * rk[j] for j in pairs] + list(y), ones_blk, exact=False)
    rkk, mean = sums[:n], [s * inv_n for s in sums[n:]]
    d = [y[j] - mean[j] for j in pairs]
    var = [s * inv_n for s in _seg_sums([d[j] * d[j] for j in pairs], ones_blk, exact=False)]
    return [((d[j] * lax.rsqrt(var[j] + GN_EPS) * lng[j] + lnb[j] + rkk[j] * v[j])
             * (g[j] * _sigmoid(g[j]))).astype(BF16) for j in pairs]


def _wkv_chunk_kernel(r_ref, k_ref, v_ref, a_ref, ld_ref, g_ref, kkg_ref, kag_ref, rk_ref, lng_ref, lnb_ref,
                      z_ref, sout_ref, st_ref):
    c = WKV_CHUNK
    n_sub = r_ref.shape[0] // c
    npair = r_ref.shape[1] // LANES
    pairs = range(npair)
    items = [(s, j) for s in range(n_sub) for j in pairs]

    @pl.when(pl.program_id(2) == 0)
    def _():
        st_ref[...] = jnp.zeros_like(st_ref)

    def tile(ref, it):
        s, j = it
        return ref[s * c:(s + 1) * c, j * LANES:(j + 1) * LANES]

    def par(ref):
        return [ref[:, j * LANES:(j + 1) * LANES] for _, j in items]

    r, k, v, a, ld, g = ([tile(ref, it) for it in items] for ref in (r_ref, k_ref, v_ref, a_ref, ld_ref, g_ref))
    st = [st_ref[j] for j in pairs]
    from_state = {}
    stage = {}

    def state_products(group):
        def run():
            for j in group:
                from_state[j] = _mm(stage["x_hi"][j], st[j].astype(BF16), NN)
        return run

    n_fill = 4
    groups = [list(pairs)[i::n_fill] for i in range(n_fill)]
    stage.update(_wkv_batch_stage(r, k, v, a, ld, par(kkg_ref), par(kag_ref),
                                  fillers=[state_products(grp) for grp in groups if grp], early=stage))
    same_half = _same_half()
    eye_full = _iota((LANES, LANES), 0) == _iota((LANES, LANES), 1)
    ys = []
    for s in range(n_sub):
        idx = [s * npair + j for j in pairs]
        if s > 0:
            for j in pairs:
                from_state[j] = _mm(stage["x_hi"][idx[j]], st[j].astype(BF16), NN)
        sub = {key: [stage[key][i] for i in idx] for key in ("tinv", "from_v", "lb_r")}
        us, ys_s = _wkv_finish(sub, [from_state[j][:c] for j in pairs], [from_state[j][c:] for j in pairs])
        upd = [_mm(stage["bke"][idx[j]], jnp.concatenate([us[j], v[idx[j]]], axis=0).astype(BF16), TN)
               for j in pairs]
        ys += ys_s
        for j in pairs:
            w_col = jnp.sum(jnp.where(eye_full, jnp.exp(stage["tot"][idx[j]]), 0.0), axis=1, keepdims=True)
            st[j] = w_col * st[j] + jnp.where(same_half, upd[j], 0.0)
    zs = _wkv_gate(ys, r, stage["kh"], v, g, par(rk_ref), par(lng_ref), par(lnb_ref), stage["ones_blk"])
    for i, (s, j) in enumerate(items):
        z_ref[s * c:(s + 1) * c, j * LANES:(j + 1) * LANES] = zs[i]
    for j in pairs:
        st_ref[j] = st[j]
        sout_ref[0, j] = st[j]


def _wkv_chunk_call(r, k, v, a, ld, g, p, nbatch, seq):
    rows = WKV_CHUNK * WKV_CHUNKS_PER_STEP
    nstep = seq // rows
    npair = D_MODEL // LANES
    tile = pl.BlockSpec((rows, D_MODEL), lambda b, j, t: (b * nstep + t, 0))
    par = pl.BlockSpec((1, D_MODEL), lambda b, j, t: (0, 0))
    return pl.pallas_call(
        _wkv_chunk_kernel,
        grid=(nbatch, 1, nstep),
        in_specs=[tile] * 6 + [par] * 5,
        out_specs=[tile, pl.BlockSpec((1, npair, LANES, LANES), lambda b, j, t: (b, 0, 0, 0))],
        out_shape=[jax.ShapeDtypeStruct((nbatch * seq, D_MODEL), BF16),
                   jax.ShapeDtypeStruct((nbatch, npair, LANES, LANES), F32)],
        scratch_shapes=[pltpu.VMEM((npair, LANES, LANES), F32)],
        compiler_params=_cparams(3),
        name="wkv_chunk",
    )(r, k, v, a, ld, g, p["k_k"], p["k_a"], p["r_k"], p["ln_g"], p["ln_b"])


WKV_LANES_UNROLL = 16


def _wkv_lanes_kernel(r_ref, k_ref, v_ref, a_ref, ld_ref, g_ref, kkg_ref, kag_ref, rk_ref, lng_ref, lnb_ref,
                      s_ref, z_ref, sout_ref, prep_ref, y_ref, *, seq_len):
    n = R_HEAD_DIM
    nseq = s_ref.shape[3]
    eye = _iota((LANES, LANES), 0) == _iota((LANES, LANES), 1)

    def column(ref):
        return jnp.sum(jnp.where(eye, ref[...], 0.0), axis=1, keepdims=True)

    kkg, kag, rk, lng, lnb = (column(ref) for ref in (kkg_ref, kag_ref, rk_ref, lng_ref, lnb_ref))

    def token(ref, t):
        return ref[pl.ds(t, nseq, stride=seq_len), :].T

    bonus = []
    for t in range(seq_len):
        r, k, v, a = (token(ref, t) for ref in (r_ref, k_ref, v_ref, a_ref))
        w = jnp.exp(token(ld_ref, t))
        kkx = k * kkg
        kh = k * (1.0 + (a - 1.0) * kag)
        rkk = r * kh * rk
        tiles = []
        for hh in range(2):
            rows = slice(hh * n, (hh + 1) * n)
            nrm = jnp.sqrt(jnp.sum(kkx[rows] * kkx[rows], axis=0, keepdims=True))
            kk = kkx[rows] / jnp.maximum(nrm, 1e-12)
            for q, val in enumerate((w[rows], -kk, kk * a[rows], kh[rows], r[rows], v[rows])):
                prep_ref[q, t, hh] = val
            tiles.append(jnp.sum(rkk[rows], axis=0, keepdims=True) * v[rows])
        bonus.append(tiles)

    for hh in range(2):
        def advance(i, carry, hh=hh):
            for u in range(WKV_LANES_UNROLL):
                vi = i * WKV_LANES_UNROLL + u
                slab = s_ref[hh, vi]
                for t in range(seq_len):
                    w, av, bv, kh, r = (prep_ref[q, t, hh] for q in range(5))
                    vrow = prep_ref[5, t, hh, pl.ds(vi, 1), :]
                    sa = jnp.sum(slab * av, axis=0, keepdims=True)
                    slab = slab * w + sa * bv + vrow * kh
                    y_ref[t, hh, pl.ds(vi, 1), :] = jnp.sum(slab * r, axis=0, keepdims=True)
                sout_ref[hh, vi] = slab
            return carry

        lax.fori_loop(0, n // WKV_LANES_UNROLL, advance, 0)

    for t in range(seq_len):
        parts = []
        for hh in range(2):
            rows = slice(hh * n, (hh + 1) * n)
            y = y_ref[t, hh]
            d = y - jnp.mean(y, axis=0, keepdims=True)
            var = jnp.mean(d * d, axis=0, keepdims=True)
            parts.append(d * lax.rsqrt(var + GN_EPS) * lng[rows] + lnb[rows] + bonus[t][hh])
        g = token(g_ref, t)
        z = jnp.concatenate(parts, axis=0) * (g * _sigmoid(g))
        z_ref[pl.ds(t, nseq, stride=seq_len), :] = z.T


def _wkv_lanes_call(r, k, v, a, ld, g, p, state_hvkb, seq_len):
    n = r.shape[0]
    nseq = state_hvkb.shape[3]
    tile = pl.BlockSpec((n, LANES), lambda j: (0, j))
    par = pl.BlockSpec((1, LANES), lambda j: (0, j))
    sspec = pl.BlockSpec((2, R_HEAD_DIM, R_HEAD_DIM, nseq), lambda j: (j, 0, 0, 0))
    return pl.pallas_call(
        functools.partial(_wkv_lanes_kernel, seq_len=seq_len),
        grid=(D_MODEL // LANES,),
        in_specs=[tile] * 6 + [par] * 5 + [sspec],
        out_specs=[tile, sspec],
        out_shape=[jax.ShapeDtypeStruct((n, D_MODEL), F32),
                   jax.ShapeDtypeStruct(state_hvkb.shape, F32)],
        scratch_shapes=[pltpu.VMEM((6, seq_len, 2, R_HEAD_DIM, nseq), F32),
                        pltpu.VMEM((seq_len, 2, R_HEAD_DIM, nseq), F32)],
        compiler_params=_cparams(1),
        name="wkv_lanes",
    )(r, k, v, a, ld, g, p["k_k"], p["k_a"], p["r_k"], p["ln_g"], p["ln_b"], state_hvkb)


RWKV_OUT_PIECES = 8


def _rwkv_out_kernel(*refs):
    n = RWKV_OUT_PIECES
    z_refs, h_refs, (wout_ref, fg_ref, out_ref) = refs[:n], refs[n:2 * n], refs[2 * n:]
    z = jnp.concatenate([ref[...].astype(BF16) for ref in z_refs], axis=0)
    h = jnp.concatenate([ref[...] for ref in h_refs], axis=0)
    h2 = h + jnp.dot(z, wout_ref[...], preferred_element_type=F32)
    out_ref[...] = _rmsnorm(h2, fg_ref[...])


def _rwkv_out_call(z, h, p, nbatch, ntile, tm, pieces_per_batch, skip):
    piece = tm // RWKV_OUT_PIECES
    dst = lambda b, i: (b * ntile + i, 0)
    pspec = [pl.BlockSpec((piece, D_MODEL),
                          functools.partial(lambda b, i, kk: (b * pieces_per_batch + i * RWKV_OUT_PIECES + skip + kk, 0),
                                            kk=kk))
             for kk in range(RWKV_OUT_PIECES)]
    return pl.pallas_call(
        _rwkv_out_kernel,
        grid=(nbatch, ntile),
        in_specs=pspec + pspec + [pl.BlockSpec((D_MODEL, D_MODEL), lambda b, i: (0, 0)),
                                  pl.BlockSpec((1, D_MODEL), lambda b, i: (0, 0))],
        out_specs=pl.BlockSpec((tm, D_MODEL), dst),
        out_shape=jax.ShapeDtypeStruct((nbatch * ntile * tm, D_MODEL), F32),
        compiler_params=_cparams(2),
        name="rwkv_out",
    )(*([z] * RWKV_OUT_PIECES), *([h] * RWKV_OUT_PIECES), p["w_out"], p["final_gain"])


def _prompt_bucket():
    assert WINDOW == BLOCK
    rel = (np.arange(BLOCK)[:, None] - np.arange(BLOCK)[None, :]) % BLOCK
    return _t5_bucket_np(rel)


def _sample_bucket(keep, t_new, slot):
    t = (np.arange(SUBLANES) % t_new)[:, None]
    j = np.arange(SAMPLE_KEYS)[None, :]
    own = j - keep - slot * t_new
    rel = np.where(j < keep, keep + t - j, t - own)
    ok = (rel >= 0) & (rel < WINDOW) & ((j < keep) | ((own >= 0) & (own < t_new)))
    return np.where(ok, _t5_bucket_np(rel), -1).astype(np.int32)


def kernel(x_prompt, x_sample, cache_win_k, cache_win_v, state_wkv, state_shift, meta_tokens, rel_bias_table, norm_gain, final_gain, attn_w_in, attn_sinks, attn_w_out, rwkv_mu, rwkv_w_in, rwkv_w0, rwkv_w1, rwkv_w2, rwkv_a0, rwkv_a1, rwkv_a2, rwkv_k_k, rwkv_k_a, rwkv_r_k, rwkv_ln_gamma, rwkv_ln_beta, rwkv_w_out):
    nb, seq, _ = x_prompt.shape
    ns, t_new, _ = x_sample.shape
    keep = cache_win_k.shape[2]
    lp = seq + BLOCK
    nblk = lp // BLOCK
    row = lambda x: x.reshape(1, D_MODEL)

    w_in0 = attn_w_in[0].astype(BF16)
    w_out0 = attn_w_out[0].astype(BF16)
    gain0 = row(norm_gain[0])
    sinks = attn_sinks[0]
    rp = dict(gain=row(norm_gain[1]), mu=rwkv_mu[0], w_in=rwkv_w_in[0].astype(BF16),
              w0=row(rwkv_w0[0]), w1=rwkv_w1[0].astype(BF16), w2=rwkv_w2[0].astype(BF16),
              a0=row(rwkv_a0[0]), a1=rwkv_a1[0].astype(BF16), a2=rwkv_a2[0].astype(BF16),
              k_k=row(rwkv_k_k[0]), k_a=row(rwkv_k_a[0]), r_k=row(rwkv_r_k[0]), ln_g=row(rwkv_ln_gamma[0]),
              ln_b=row(rwkv_ln_beta[0]), w_out=rwkv_w_out[0].astype(BF16),
              final_gain=row(final_gain))

    bias_p, *bias_s = _bias_call(rel_bias_table, [_prompt_bucket()]
                                 + [_sample_bucket(keep, t_new, slot) for slot in range(2)])

    head = jnp.concatenate([jnp.zeros((PAD, D_MODEL), F32), meta_tokens.astype(F32)], axis=0)
    xp = x_prompt.reshape(nb * seq, D_MODEL)
    q, kv, g = _attn_proj_call(xp, head, gain0, w_in0, BF16, nb, lp // ATTN_PROJ_ROWS,
                               ATTN_PROJ_ROWS // BLOCK, BLOCK)
    h1 = _attn_prompt_call(sinks, q, kv, g, head, xp, bias_p, w_out0, nb, nblk)
    kv3 = kv.reshape(nb, lp, 2 * A_KV_WIDTH)[:, lp - WINDOW:, :]
    win_k_p = kv3[:, :, :A_KV_WIDTH].reshape(1, nb, WINDOW, A_KV_HEADS, A_HEAD_DIM)
    win_v_p = kv3[:, :, A_KV_WIDTH:].reshape(1, nb, WINDOW, A_KV_HEADS, A_HEAD_DIM)

    shift0 = jnp.zeros((nb, 1, D_MODEL), F32)
    r, k, v, g1, ld, a, xlast = _rwkv_proj_call(h1, shift0, rp, nb, lp // RWKV_PROJ_ROWS, RWKV_PROJ_ROWS, None)
    z, st = _wkv_chunk_call(r, k, v, a, ld, g1, rp, nb, lp)
    y_prompt = _rwkv_out_call(z, h1, rp, nb, seq // RWKV_OUT_ROWS, RWKV_OUT_ROWS,
                              lp * RWKV_OUT_PIECES // RWKV_OUT_ROWS, BLOCK * RWKV_OUT_PIECES // RWKV_OUT_ROWS)
    y_prompt = y_prompt.reshape(nb, seq, D_MODEL)
    st = st.reshape(nb, D_MODEL // LANES, 2, R_HEAD_DIM, 2, R_HEAD_DIM)
    st = jnp.stack([st[:, :, 0, :, 0, :], st[:, :, 1, :, 1, :]], axis=2)
    wkv_p = jnp.swapaxes(st, -1, -2).reshape(1, nb, R_HEADS, R_HEAD_DIM, R_HEAD_DIM)
    shift_p = xlast.reshape(1, nb, D_MODEL)

    xs = x_sample.reshape(ns * t_new, D_MODEL)
    qs, kvs, gs = _attn_proj_call(xs, None, gain0, w_in0, F32, 1, 1, 1, ns * t_new)
    ck = jnp.swapaxes(cache_win_k[0].reshape(ns, keep, A_KV_WIDTH), 1, 2)
    cv = jnp.swapaxes(cache_win_v[0].reshape(ns, keep, A_KV_WIDTH), 1, 2)
    h1s, nk, nv = _attn_sample_call(sinks, qs, kvs, gs, xs, ck, cv, bias_s[0], bias_s[1], w_out0, t_new)
    win_k_s = jnp.swapaxes(nk, 1, 2).reshape(1, ns, keep, A_KV_HEADS, A_HEAD_DIM)
    win_v_s = jnp.swapaxes(nv, 1, 2).reshape(1, ns, keep, A_KV_HEADS, A_HEAD_DIM)

    shift_rows = jnp.repeat(state_shift[0], t_new, axis=0)
    tms = 256
    rs, ks, vs, g1s, lds, as_, xns = _rwkv_proj_call(h1s, shift_rows, rp, 1, ns * t_new // tms, tms, t_new)
    zs, st_s = _wkv_lanes_call(rs, ks, vs, as_, lds, g1s, rp, jnp.transpose(state_wkv[0], (1, 2, 3, 0)), t_new)
    y_sample = _rwkv_out_call(zs, h1s, rp, 1, 1, ns * t_new, RWKV_OUT_PIECES, 0)
    y_sample = y_sample.reshape(ns, t_new, D_MODEL)
    wkv_s = jnp.transpose(st_s, (3, 0, 1, 2))[None]
    shift_s = xns.reshape(ns, t_new, D_MODEL)[:, t_new - 1][None]

    return (y_prompt, y_sample, win_k_p, win_v_p, wkv_p, shift_p, win_k_s, win_v_s, wkv_s, shift_s)
```

```python
import functools
import math

import numpy as np
import jax
import jax.numpy as jnp
from jax import lax
from jax.experimental import pallas as pl
from jax.experimental.pallas import tpu as pltpu

F32 = jnp.float32
BF16 = jnp.bfloat16

D_MODEL = 1024
N_META = 16
RMS_EPS = 1e-6
A_HEADS = 16
A_KV_HEADS = 4
A_HEAD_DIM = 64
A_WIDTH = A_HEADS * A_HEAD_DIM
A_KV_WIDTH = A_KV_HEADS * A_HEAD_DIM
WINDOW = 128
BLOCK = 128
N_BUCKETS = 32
MAX_DISTANCE = 128
R_HEAD_DIM = 64
R_HEADS = D_MODEL // R_HEAD_DIM
GN_EPS = 64e-5

LANES = 128
SUBLANES = 8
PAD = BLOCK - N_META
NEG = -1e30
WKV_CHUNK = 64
WKV_CHUNKS_PER_STEP = 3
ATTN_BLOCKS_PER_STEP = 3
ATTN_PROJ_ROWS = 384
RWKV_PROJ_ROWS = 528
RWKV_OUT_ROWS = 1024
VMEM_LIMIT = 56 * 1024 * 1024


def _cparams(n_axes):
    return pltpu.CompilerParams(dimension_semantics=("arbitrary",) * n_axes,
                                vmem_limit_bytes=VMEM_LIMIT)


def _rmsnorm(x, gain):
    return x * lax.rsqrt(jnp.mean(x * x, axis=-1, keepdims=True) + RMS_EPS) * gain


def _sigmoid(x):
    return 1.0 / (1.0 + jnp.exp(-x))


def _iota(shape, dim):
    return lax.broadcasted_iota(jnp.int32, shape, dim)


def _t5_bucket_np(rel):
    n = np.maximum(rel, 0)
    max_exact = N_BUCKETS // 2
    nf = np.maximum(n, max_exact).astype(np.float32)
    scale = np.float32(math.log(MAX_DISTANCE / max_exact))
    large = max_exact + (np.log(nf / np.float32(max_exact)) / scale
                         * np.float32(N_BUCKETS - max_exact)).astype(np.int32)
    large = np.minimum(large, N_BUCKETS - 1)
    return np.where(n < max_exact, n, large).astype(np.int32)


def _bias_kernel(table_ref, *refs):
    h = pl.program_id(0)
    n = len(refs) // 2
    for bucket_ref, out_ref in zip(refs[:n], refs[n:]):
        bk = bucket_ref[...]
        acc = jnp.full(bk.shape, NEG, F32)
        for b in range(N_BUCKETS):
            acc = jnp.where(bk == b, table_ref[b, h], acc)
        out_ref[0] = acc


def _bias_call(table, buckets_np):
    return pl.pallas_call(
        _bias_kernel,
        grid=(A_HEADS,),
        in_specs=[pl.BlockSpec(memory_space=pltpu.SMEM)]
                 + [pl.BlockSpec(bk.shape, lambda h: (0, 0)) for bk in buckets_np],
        out_specs=[pl.BlockSpec((1,) + bk.shape, lambda h: (h, 0, 0)) for bk in buckets_np],
        out_shape=[jax.ShapeDtypeStruct((A_HEADS,) + bk.shape, F32) for bk in buckets_np],
        compiler_params=_cparams(1),
        name="bias_expand",
    )(table, *(jnp.asarray(bk) for bk in buckets_np))


def _attn_proj_kernel(head_ref, *refs, n_piece):
    x_refs, (gain_ref, w_ref, q_ref, kv_ref, g_ref, kvb_ref) = refs[:n_piece], refs[n_piece:]
    first = x_refs[0][...]
    if head_ref is not None:
        first = jnp.where(pl.program_id(1) == 0, head_ref[...], first)
    x = jnp.concatenate([first] + [ref[...] for ref in x_refs[1:]], axis=0)
    xn = _rmsnorm(x, gain_ref[...])
    proj = jnp.dot(xn.astype(BF16), w_ref[...], preferred_element_type=F32)
    q_ref[...] = (proj[:, :A_WIDTH] * (A_HEAD_DIM ** -0.5)).astype(q_ref.dtype)
    kv = proj[:, A_WIDTH:A_WIDTH + 2 * A_KV_WIDTH]
    kv_ref[...] = kv
    g_ref[...] = proj[:, A_WIDTH + 2 * A_KV_WIDTH:]
    swapped = [pltpu.roll(kv[:, t * LANES:(t + 1) * LANES], A_HEAD_DIM, 1) for t in range(2 * A_KV_WIDTH // LANES)]
    kvb_ref[...] = jnp.concatenate([kv] + swapped, axis=1).astype(BF16)


def _attn_proj_call(x2d, head, gain, w_bf16, q_dtype, nbatch, ntile, n_piece, piece):
    tm = n_piece * piece
    wcols = w_bf16.shape[1]
    per_seq = x2d.shape[0] // (nbatch * piece)
    lead = 0 if head is None else 1
    dst = lambda b, i: (b * ntile + i, 0)
    xspec = [pl.BlockSpec((piece, D_MODEL),
                          functools.partial(lambda b, i, kk: (b * per_seq + jnp.maximum(i * n_piece + kk - lead, 0), 0),
                                            kk=kk))
             for kk in range(n_piece)]
    kern = functools.partial(_attn_proj_kernel, n_piece=n_piece)
    operands = [x2d] * n_piece + [gain, w_bf16]
    if head is None:
        kern = functools.partial(kern, None)
        head_spec = []
    else:
        head_spec = [pl.BlockSpec((piece, D_MODEL), lambda b, i: (0, 0))]
        operands = [head] + operands
    n = nbatch * ntile * tm
    return pl.pallas_call(
        kern,
        grid=(nbatch, ntile),
        in_specs=head_spec + xspec + [pl.BlockSpec((1, D_MODEL), lambda b, i: (0, 0)),
                                      pl.BlockSpec((D_MODEL, wcols), lambda b, i: (0, 0))],
        out_specs=[pl.BlockSpec((tm, A_WIDTH), dst),
                   pl.BlockSpec((tm, 2 * A_KV_WIDTH), dst),
                   pl.BlockSpec((tm, A_WIDTH), dst),
                   pl.BlockSpec((tm, 4 * A_KV_WIDTH), dst)],
        out_shape=[jax.ShapeDtypeStruct((n, A_WIDTH), q_dtype),
                   jax.ShapeDtypeStruct((n, 2 * A_KV_WIDTH), F32),
                   jax.ShapeDtypeStruct((n, A_WIDTH), F32),
                   jax.ShapeDtypeStruct((n, 4 * A_KV_WIDTH), BF16)],
        compiler_params=_cparams(2),
        name="attn_proj",
    )(*operands)


def _padded_kv_tiles(kvb, c):
    lo = _iota((1, LANES), 1) < A_HEAD_DIM
    j = c // 2
    zero = jnp.zeros((), BF16)
    out = []
    for base in (0, A_KV_WIDTH):
        t = kvb[:, base + j * LANES: base + (j + 1) * LANES]
        tr = kvb[:, 2 * A_KV_WIDTH + base + j * LANES: 2 * A_KV_WIDTH + base + (j + 1) * LANES]
        if c % 2 == 0:
            out += [jnp.where(lo, t, zero), jnp.where(lo, zero, tr)]
        else:
            out += [jnp.where(lo, tr, zero), jnp.where(lo, zero, t)]
    return out


def _mm_nt(a, b):
    return lax.dot_general(a, b, (((1,), (1,)), ((), ())), preferred_element_type=F32)


def _attn_prompt_kernel(sinks_ref, q_ref, kvc_ref, kvp_ref, g_ref, head_ref, *refs):
    nb = ATTN_BLOCKS_PER_STEP
    x_refs, (bias_in_ref, wout_ref, out_ref, og_ref, bias_ref) = refs[:nb], refs[nb:]
    i = pl.program_id(1)
    stack = 2 * BLOCK
    row, col = _iota((stack, BLOCK), 0) & (BLOCK - 1), _iota((stack, BLOCK), 1)
    upper = _iota((stack, 1), 0) >= BLOCK
    own = col <= row

    @pl.when((pl.program_id(0) == 0) & (i == 0))
    def _():
        col1 = _iota((BLOCK, BLOCK), 1)
        own1, real = col1 <= _iota((BLOCK, BLOCK), 0), col1 >= PAD
        for h in range(A_HEADS):
            b = bias_in_ref[h]
            late = jnp.where(real, b, NEG)
            bias_ref[h, 0:BLOCK, :] = jnp.where(own1, late, NEG)
            bias_ref[h, BLOCK:2 * BLOCK, :] = jnp.where(own1, b, late)
            bias_ref[h, 2 * BLOCK:, :] = b

    chains = [(c, idx) for c in range(A_KV_HEADS) for idx in range(2)]
    n = range(len(chains))
    cur = [_padded_kv_tiles(kvp_ref[...], c) for c in range(A_KV_HEADS)]
    for j in range(nb):
        rows = slice(j * BLOCK, (j + 1) * BLOCK)
        brow = pl.ds(pl.multiple_of(jnp.minimum(i * nb + j, 2) * BLOCK, BLOCK), BLOCK)
        prev, cur = cur, [_padded_kv_tiles(kvc_ref[rows, :], c) for c in range(A_KV_HEADS)]
        s, sink = [], []
        for c, idx in chains:
            q2 = q_ref[rows, 2 * c * LANES:(2 * c + 2) * LANES]
            q2 = jnp.concatenate([q2[:, :LANES], q2[:, LANES:]], axis=0)
            sc = jnp.where(own, _mm_nt(q2, cur[c][idx]), _mm_nt(q2, prev[c][idx]))
            bias = jnp.concatenate([bias_ref[4 * c + idx, brow, :], bias_ref[4 * c + 2 + idx, brow, :]], axis=0)
            s.append(sc + bias)
            sink.append(jnp.where(upper, sinks_ref[4 * c + 2 + idx], sinks_ref[4 * c + idx]))
        m = [jnp.maximum(jnp.max(s[t], axis=1, keepdims=True), sink[t]) for t in n]
        p = [jnp.exp(s[t] - m[t]) for t in n]
        den = [jnp.sum(p[t], axis=1, keepdims=True) + jnp.exp(sink[t] - m[t]) for t in n]
        o = []
        for t, (c, idx) in enumerate(chains):
            pv = (jnp.dot(jnp.where(own, p[t], 0.0).astype(BF16), cur[c][2 + idx], preferred_element_type=F32)
                  + jnp.dot(jnp.where(own, 0.0, p[t]).astype(BF16), prev[c][2 + idx], preferred_element_type=F32))
            o.append(pv * (1.0 / den[t]))
        for c in range(A_KV_HEADS):
            both = o[2 * c] + o[2 * c + 1]
            for half in range(2):
                sl = slice((2 * c + half) * LANES, (2 * c + half + 1) * LANES)
                gt = g_ref[rows, sl]
                og_ref[rows, sl] = (both[half * BLOCK:(half + 1) * BLOCK] * (gt * _sigmoid(gt))).astype(BF16)
    resid = jnp.concatenate([jnp.where(i == 0, head_ref[...], x_refs[0][...])]
                            + [ref[...] for ref in x_refs[1:]], axis=0)
    out_ref[...] = resid + jnp.dot(og_ref[...], wout_ref[...], preferred_element_type=F32)


def _attn_prompt_call(sinks, q, kv, g, head, x2d, bias, wout_bf16, nbatch, nblk):
    n = q.shape[0]
    nb = ATTN_BLOCKS_PER_STEP
    nstep = nblk // nb
    rows = nb * BLOCK
    row = lambda b, i: (b * nstep + i, 0)
    prev = lambda b, i: (b * nblk + jnp.maximum(i * nb - 1, 0), 0)
    xrow = [functools.partial(lambda b, i, j: (b * (nblk - 1) + jnp.maximum(i * nb + j - 1, 0), 0), j=j)
            for j in range(nb)]
    return pl.pallas_call(
        _attn_prompt_kernel,
        grid=(nbatch, nstep),
        in_specs=[pl.BlockSpec(memory_space=pltpu.SMEM),
                  pl.BlockSpec((rows, A_WIDTH), row),
                  pl.BlockSpec((rows, 4 * A_KV_WIDTH), row),
                  pl.BlockSpec((BLOCK, 4 * A_KV_WIDTH), prev),
                  pl.BlockSpec((rows, A_WIDTH), row),
                  pl.BlockSpec((BLOCK, D_MODEL), lambda b, i: (0, 0))]
                 + [pl.BlockSpec((BLOCK, D_MODEL), xrow[j]) for j in range(nb)]
                 + [pl.BlockSpec((A_HEADS, BLOCK, BLOCK), lambda b, i: (0, 0, 0)),
                    pl.BlockSpec((A_WIDTH, D_MODEL), lambda b, i: (0, 0))],
        out_specs=pl.BlockSpec((rows, D_MODEL), row),
        out_shape=jax.ShapeDtypeStruct((n, D_MODEL), F32),
        scratch_shapes=[pltpu.VMEM((rows, A_WIDTH), BF16), pltpu.VMEM((A_HEADS, 3 * BLOCK, BLOCK), F32)],
        compiler_params=_cparams(2),
        name="attn_prompt",
    )(sinks, q, kv, kv, g, head, *([x2d] * nb), bias, wout_bf16)


SAMPLE_SB = 8
SAMPLE_KEYS = 2 * BLOCK


def _attn_sample_kernel(sinks_ref, q_ref, kvn_ref, g_ref, h_ref, ck_ref, cv_ref, bias0_ref, bias1_ref,
                        wout_ref, out_ref, nk_ref, nv_ref, og_ref, *, t_new):
    keep = ck_ref.shape[2]
    lo = _iota((1, LANES), 1) < A_HEAD_DIM
    stack = 8 * SUBLANES
    own = (_iota((stack, 1), 0) & (SUBLANES - 1)) // t_new
    piece = _iota((stack, 1), 0) // SUBLANES
    bias_refs = (bias0_ref, bias1_ref)
    n_tile = A_KV_WIDTH // LANES
    pair_ids = range(SAMPLE_SB * t_new // SUBLANES)

    bias_c, bias_n, sink = [], [], []
    for j in range(n_tile):
        heads = slice(8 * j, 8 * j + 8)
        bias_c.append(bias0_ref[heads, :, :keep].reshape(stack, keep))
        bias_n.append([ref[heads, :, keep:keep + SUBLANES].reshape(stack, SUBLANES) for ref in bias_refs])
        col = jnp.zeros((stack, 1), F32)
        for gq in range(8):
            col = jnp.where(piece == gq, sinks_ref[8 * j + gq], col)
        sink.append(col)

    chains = [(p, j, s) for p in pair_ids for j in range(n_tile) for s in range(SUBLANES // t_new)]
    qs = {}
    for p in pair_ids:
        rows = slice(p * SUBLANES, (p + 1) * SUBLANES)
        for j in range(n_tile):
            parts = []
            for gq in range(8):
                t = q_ref[rows, (4 * j + gq // 2) * LANES:(4 * j + gq // 2 + 1) * LANES]
                want_lo = gq < 4
                if (gq % 2 == 0) != want_lo:
                    t = pltpu.roll(t, A_HEAD_DIM, 1)
                parts.append(jnp.where(lo, t, 0.0) if want_lo else jnp.where(lo, 0.0, t))
            qs[p, j] = jnp.concatenate(parts, axis=0).astype(BF16)

    def kv_tiles(p, j, s, base):
        seq = p * (SUBLANES // t_new) + s
        cache = (ck_ref if base == 0 else cv_ref)[seq][j * LANES:(j + 1) * LANES, :].astype(BF16)
        new = kvn_ref[p * SUBLANES:(p + 1) * SUBLANES, base + j * LANES:base + (j + 1) * LANES].astype(BF16)
        return cache, new

    sc, sn = [], []
    for p, j, s in chains:
        kc, kn = kv_tiles(p, j, s, 0)
        sc.append(jnp.dot(qs[p, j], kc, preferred_element_type=F32) + bias_c[j])
        sn.append(_mm_nt(qs[p, j], kn) + bias_n[j][s])
    mx = [jnp.maximum(jnp.maximum(jnp.max(sc[i], axis=1, keepdims=True),
                                  jnp.max(sn[i], axis=1, keepdims=True)), sink[chains[i][1]])
          for i in range(len(chains))]
    pc = [jnp.exp(sc[i] - mx[i]) for i in range(len(chains))]
    pn = [jnp.exp(sn[i] - mx[i]) for i in range(len(chains))]
    den = [jnp.sum(pc[i], axis=1, keepdims=True) + jnp.sum(pn[i], axis=1, keepdims=True)
           + jnp.exp(sink[chains[i][1]] - mx[i]) for i in range(len(chains))]
    outs = {}
    for i, (p, j, s) in enumerate(chains):
        vc, vn = kv_tiles(p, j, s, A_KV_WIDTH)
        o = (_mm_nt(pc[i].astype(BF16), vc)
             + jnp.dot(pn[i].astype(BF16), vn, preferred_element_type=F32)) * (1.0 / den[i])
        outs[p, j] = o if s == 0 else jnp.where(own == s, o, outs[p, j])
    for p in pair_ids:
        rows = slice(p * SUBLANES, (p + 1) * SUBLANES)
        for j in range(n_tile):
            o = outs[p, j]
            for gg in range(4):
                even = o[2 * gg * SUBLANES:(2 * gg + 1) * SUBLANES]
                odd = o[(2 * gg + 1) * SUBLANES:(2 * gg + 2) * SUBLANES]
                if gg < 2:
                    tile = jnp.where(lo, even, pltpu.roll(odd, A_HEAD_DIM, 1))
                else:
                    tile = jnp.where(lo, pltpu.roll(even, A_HEAD_DIM, 1), odd)
                sl = slice((4 * j + gg) * LANES, (4 * j + gg + 1) * LANES)
                gt = g_ref[rows, sl]
                og_ref[rows, sl] = tile * (gt * _sigmoid(gt))
    lane_pos = _iota((SUBLANES, keep), 1)
    old = _iota((1, keep), 1) < keep - t_new
    for p in pair_ids:
        new8 = kvn_ref[p * SUBLANES:(p + 1) * SUBLANES, :]
        for s in range(SUBLANES // t_new):
            seq = p * (SUBLANES // t_new) + s
            sel = (lane_pos == _iota((SUBLANES, keep), 0) + (keep - t_new - s * t_new)).astype(BF16)
            placed = _mm2(_split(new8), sel, TN)
            for ref_in, ref_out, base in ((ck_ref, nk_ref, 0), (cv_ref, nv_ref, A_KV_WIDTH)):
                shifted = pltpu.roll(ref_in[seq], keep - t_new, 1)
                ref_out[seq] = jnp.where(old, shifted, placed[base:base + A_KV_WIDTH])
    out_ref[...] = h_ref[...] + jnp.dot(og_ref[...].astype(BF16), wout_ref[...],
                                        preferred_element_type=F32)


def _attn_sample_call(sinks, q, kv, g, h, cache_k, cache_v, bias0, bias1, wout_bf16, t_new):
    nseq, keep = cache_k.shape[0], cache_k.shape[2]
    rows = SAMPLE_SB * t_new
    row = lambda i: (i, 0)
    cspec = pl.BlockSpec((SAMPLE_SB, A_KV_WIDTH, keep), lambda i: (i, 0, 0))
    bspec = pl.BlockSpec((A_HEADS, SUBLANES, SAMPLE_KEYS), lambda i: (0, 0, 0))
    return pl.pallas_call(
        functools.partial(_attn_sample_kernel, t_new=t_new),
        grid=(nseq // SAMPLE_SB,),
        in_specs=[pl.BlockSpec(memory_space=pltpu.SMEM),
                  pl.BlockSpec((rows, A_WIDTH), row),
                  pl.BlockSpec((rows, 2 * A_KV_WIDTH), row),
                  pl.BlockSpec((rows, A_WIDTH), row),
                  pl.BlockSpec((rows, D_MODEL), row),
                  cspec, cspec, bspec, bspec,
                  pl.BlockSpec((A_WIDTH, D_MODEL), lambda i: (0, 0))],
        out_specs=[pl.BlockSpec((rows, D_MODEL), row), cspec, cspec],
        out_shape=[jax.ShapeDtypeStruct((nseq * t_new, D_MODEL), F32),
                   jax.ShapeDtypeStruct(cache_k.shape, F32),
                   jax.ShapeDtypeStruct(cache_v.shape, F32)],
        scratch_shapes=[pltpu.VMEM((rows, A_WIDTH), F32)],
        compiler_params=_cparams(1),
        name="attn_sample",
    )(sinks, q, kv, g, h, cache_k, cache_v, bias0, bias1, wout_bf16)


def _rwkv_proj_kernel(h_ref, shift_ref, gain_ref, mu_ref, win_ref, w0_ref, w1_ref, w2_ref,
                      a0_ref, a1_ref, a2_ref,
                      r_ref, k_ref, v_ref, g_ref, ld_ref, a_ref, xn_ref, *scratch, seq_len):
    xn = _rmsnorm(h_ref[...], gain_ref[...])
    tm = xn.shape[0]
    rolled = pltpu.roll(xn, 1, 0)
    row = _iota((tm, 1), 0)
    if seq_len is None:
        carry_ref, = scratch

        @pl.when(pl.program_id(1) == 0)
        def _():
            carry_ref[...] = shift_ref[0]

        xprev = jnp.where(row == 0, carry_ref[...], rolled)
        carry_ref[...] = xn[tm - 1:tm, :]
        xn_ref[0] = xn[tm - 1:tm, :]
    else:
        xprev = jnp.where(row % seq_len == 0, shift_ref[...], rolled)
        xn_ref[...] = xn
    dx = xprev - xn

    def mix(c):
        return (xn + dx * mu_ref[c:c + 1, :]).astype(BF16)

    for c, o_ref in enumerate((r_ref, k_ref, v_ref, g_ref)):
        o_ref[...] = jnp.dot(mix(c), win_ref[c], preferred_element_type=F32)
    lw = jnp.tanh(jnp.dot(mix(4), w1_ref[...], preferred_element_type=F32))
    z = w0_ref[...] + jnp.dot(lw.astype(BF16), w2_ref[...], preferred_element_type=F32)
    ld_ref[...] = -math.exp(-0.5) * _sigmoid(z)
    la = jnp.dot(mix(5), a1_ref[...], preferred_element_type=F32)
    a_ref[...] = _sigmoid(a0_ref[...] + jnp.dot(la.astype(BF16), a2_ref[...],
                                                preferred_element_type=F32))


def _rwkv_proj_call(h, shift, p, nbatch, ntile, tm, seq_len):
    n = h.shape[0]
    row = lambda b, i: (b * ntile + i, 0)
    full2 = lambda b, i: (0, 0)
    if seq_len is None:
        shift_spec = pl.BlockSpec((1, 1, D_MODEL), lambda b, i: (b, 0, 0))
        xn_spec = pl.BlockSpec((1, 1, D_MODEL), lambda b, i: (b, 0, 0))
        xn_shape = jax.ShapeDtypeStruct((nbatch, 1, D_MODEL), F32)
        scratch = [pltpu.VMEM((1, D_MODEL), F32)]
    else:
        shift_spec = pl.BlockSpec((tm, D_MODEL), row)
        xn_spec = pl.BlockSpec((tm, D_MODEL), row)
        xn_shape = jax.ShapeDtypeStruct((n, D_MODEL), F32)
        scratch = []
    lora = p["w1"].shape[1]
    big = jax.ShapeDtypeStruct((n, D_MODEL), F32)
    return pl.pallas_call(
        functools.partial(_rwkv_proj_kernel, seq_len=seq_len),
        grid=(nbatch, ntile),
        in_specs=[pl.BlockSpec((tm, D_MODEL), row),
                  shift_spec,
                  pl.BlockSpec((1, D_MODEL), full2),
                  pl.BlockSpec(p["mu"].shape, full2),
                  pl.BlockSpec(p["w_in"].shape, lambda b, i: (0, 0, 0)),
                  pl.BlockSpec((1, D_MODEL), full2),
                  pl.BlockSpec((D_MODEL, lora), full2),
                  pl.BlockSpec((lora, D_MODEL), full2),
                  pl.BlockSpec((1, D_MODEL), full2),
                  pl.BlockSpec((D_MODEL, lora), full2),
                  pl.BlockSpec((lora, D_MODEL), full2)],
        out_specs=[pl.BlockSpec((tm, D_MODEL), row)] * 6 + [xn_spec],
        out_shape=[big] * 6 + [xn_shape],
        scratch_shapes=scratch,
        compiler_params=_cparams(2),
        name="rwkv_proj",
    )(h, shift, p["gain"], p["mu"], p["w_in"], p["w0"], p["w1"], p["w2"], p["a0"], p["a1"], p["a2"])


NN = (((1,), (0,)), ((), ()))
NT = (((1,), (1,)), ((), ()))
TN = (((0,), (0,)), ((), ()))


def _split(x):
    hi = x.astype(BF16)
    return hi, (x - hi.astype(F32)).astype(BF16)


def _mm(a, b, dn):
    return lax.dot_general(a, b, dn, preferred_element_type=F32)


def _mm2(a, b, dn):
    ah, al = a
    if dn == TN:
        both = _mm(jnp.concatenate([ah, al], axis=1), b, dn)
        m = ah.shape[1]
    else:
        both = _mm(jnp.concatenate([ah, al], axis=0), b, dn)
        m = ah.shape[0]
    return both[:m] + both[m:]


def _seg_sums(xs, ones_bf16, exact):
    rows = xs[0].shape[0]
    stacked = jnp.concatenate(xs, axis=0)
    if exact:
        hi, lo = _split(stacked)
        out = _mm(lo, ones_bf16, NN) + _mm(hi, ones_bf16, NN)
    else:
        out = _mm(stacked.astype(BF16), ones_bf16, NN)
    return [out[i * rows:(i + 1) * rows] for i in range(len(xs))]


def _lane_lo():
    return _iota((1, LANES), 1) < R_HEAD_DIM


def _same_half():
    return (_iota((LANES, LANES), 0) < R_HEAD_DIM) == (_iota((LANES, LANES), 1) < R_HEAD_DIM)


def _ones_blk():
    return _same_half().astype(BF16)


def _bd(z):
    lo = _lane_lo()
    return jnp.concatenate([jnp.where(lo, z, 0.0), jnp.where(lo, 0.0, z)], axis=0)


def _bd_swap(z):
    lo = _lane_lo()
    return jnp.concatenate([jnp.where(lo, 0.0, z), jnp.where(lo, z, 0.0)], axis=0)


def _wkv_batch_stage(r, k, v, a, ld, kkg, kag, fillers=(), early=None):
    c = r[0].shape[0]
    pairs = range(len(r))
    half = R_HEAD_DIM
    lo = _lane_lo()
    t_row, t_col = _iota((c, LANES), 0), _iota((c, LANES), 1) & (half - 1)
    strict, incl = t_row > t_col, t_row >= t_col
    tri = (_iota((c, c), 0) >= _iota((c, c), 1)).astype(BF16)
    ones_blk = _ones_blk()
    eye_pair = (_iota((c, LANES), 0) == (_iota((c, LANES), 1) & (half - 1))).astype(F32)

    kkx = [k[j] * kkg[j] for j in pairs]
    ssq = _seg_sums([kkx[j] * kkx[j] for j in pairs], ones_blk, exact=True)
    ld_hi, ld_lo = _split(jnp.concatenate(ld, axis=1))
    cs_all = _mm(tri, ld_lo, NN) + _mm(tri, ld_hi, NN)
    cs = [cs_all[:, j * LANES:(j + 1) * LANES] for j in pairs]
    tot = [cs[j][c - 1:c, :] for j in pairs]
    x_hi, kh_all, bke, vbd_s, lk_a, lk_r, lb_a, lb_r = ([] for _ in range(8))
    for j in pairs:
        kk = kkx[j] * lax.rsqrt(jnp.maximum(ssq[j], 1e-24))
        kh = k[j] * (1.0 + (a[j] - 1.0) * kag[j])
        bv = kk * a[j]
        e_neg = jnp.exp(-cs[j])
        e_end = e_neg * jnp.exp(tot[j])
        at, rt = -kk * jnp.exp(cs[j] - ld[j]), r[j] * jnp.exp(cs[j])
        x_hi.append(jnp.concatenate([at.astype(BF16), rt.astype(BF16)], axis=0))
        kh_all.append(kh)
        y_hi = jnp.concatenate([(bv * e_neg).astype(BF16), (kh * e_neg).astype(BF16)], axis=0)
        bke.append(jnp.concatenate([(bv * e_end).astype(BF16), (kh * e_end).astype(BF16)], axis=0))
        vbd_s.append(_bd(v[j].astype(BF16)))
        both = _mm(jnp.concatenate([jnp.where(lo, x_hi[j], 0.0), jnp.where(lo, 0.0, x_hi[j])], axis=0), y_hi, NT)
        ga_a = pltpu.roll(both[:c], half, 1)
        ga_r = pltpu.roll(both[c:2 * c], half, 1)
        gb_a = both[2 * c:3 * c]
        gb_r = both[3 * c:4 * c]
        lk_a.append(jnp.where(strict, jnp.where(lo, ga_a, gb_a), 0.0))
        lk_r.append(jnp.where(incl, jnp.where(lo, ga_r, gb_r), 0.0))
        lb_a.append(jnp.where(strict, jnp.where(lo, gb_a, ga_a), 0.0))
        lb_r.append(jnp.where(incl, jnp.where(lo, gb_r, ga_r), 0.0))
    fillers = list(fillers)
    if early is not None:
        early.update(x_hi=x_hi)
    pw = lb_a
    acc = [eye_pair + pw[j] for j in pairs]
    pw = [_mm(pw[j].astype(BF16), _bd(pw[j].astype(BF16)), NN) for j in pairs]
    from_v = []
    for j in pairs:
        both = _mm(jnp.concatenate([lk_a[j].astype(BF16), lk_r[j].astype(BF16)], axis=0), vbd_s[j], NN)
        from_v.append((both[:c], both[c:]))
    for _ in range(int(math.log2(c)) - 2):
        both = [_mm(jnp.concatenate([pw[j], acc[j]], axis=0).astype(BF16), _bd(pw[j].astype(BF16)), NN)
                for j in pairs]
        if fillers:
            fillers.pop(0)()
        pw = [both[j][:c] for j in pairs]
        acc = [acc[j] + both[j][c:] for j in pairs]
    tinv = [acc[j] + _mm(acc[j].astype(BF16), _bd(pw[j].astype(BF16)), NN) for j in pairs]
    for fill in fillers:
        fill()
    return dict(x_hi=x_hi, kh=kh_all, lb_r=lb_r, tinv=tinv, from_v=from_v, bke=bke,
                tot=tot, ones_blk=ones_blk)


def _wkv_finish(stage, from_state_a, from_state_r):
    pairs = range(len(from_state_a))
    u = [_mm(stage["tinv"][j].astype(BF16),
             _bd_swap((from_state_a[j] + stage["from_v"][j][0]).astype(BF16)), NN) for j in pairs]
    y = [from_state_r[j] + stage["from_v"][j][1]
         + _mm(stage["lb_r"][j].astype(BF16), _bd_swap(u[j].astype(BF16)), NN) for j in pairs]
    return u, y


def _wkv_gate(y, r, kh, v, g, rk, lng, lnb, ones_blk):
    n = len(y)
    pairs = range(n)
    inv_n = 1.0 / R_HEAD_DIM
    sums = _seg_sums([r[j] * kh[j] * rk[j] for j in pairs] + list(y), ones_blk, exact=False)
    rkk, mean = sums[:n], [s * inv_n for s in sums[n:]]
    d = [y[j] - mean[j] for j in pairs]
    var = [s * inv_n for s in _seg_sums([d[j] * d[j] for j in pairs], ones_blk, exact=False)]
    return [((d[j] * lax.rsqrt(var[j] + GN_EPS) * lng[j] + lnb[j] + rkk[j] * v[j])
             * (g[j] * _sigmoid(g[j]))).astype(BF16) for j in pairs]


def _wkv_chunk_kernel(r_ref, k_ref, v_ref, a_ref, ld_ref, g_ref, kkg_ref, kag_ref, rk_ref, lng_ref, lnb_ref,
                      z_ref, sout_ref, st_ref):
    c = WKV_CHUNK
    n_sub = r_ref.shape[0] // c
    npair = r_ref.shape[1] // LANES
    pairs = range(npair)
    items = [(s, j) for s in range(n_sub) for j in pairs]

    @pl.when(pl.program_id(2) == 0)
    def _():
        st_ref[...] = jnp.zeros_like(st_ref)

    def tile(ref, it):
        s, j = it
        return ref[s * c:(s + 1) * c, j * LANES:(j + 1) * LANES]

    def par(ref):
        return [ref[:, j * LANES:(j + 1) * LANES] for _, j in items]

    r, k, v, a, ld, g = ([tile(ref, it) for it in items] for ref in (r_ref, k_ref, v_ref, a_ref, ld_ref, g_ref))
    st = [st_ref[j] for j in pairs]
    from_state = {}
    stage = {}

    def state_products(group):
        def run():
            for j in group:
                from_state[j] = _mm(stage["x_hi"][j], st[j].astype(BF16), NN)
        return run

    n_fill = 4
    groups = [list(pairs)[i::n_fill] for i in range(n_fill)]
    stage.update(_wkv_batch_stage(r, k, v, a, ld, par(kkg_ref), par(kag_ref),
                                  fillers=[state_products(grp) for grp in groups if grp], early=stage))
    same_half = _same_half()
    eye_full = _iota((LANES, LANES), 0) == _iota((LANES, LANES), 1)
    ys = []
    for s in range(n_sub):
        idx = [s * npair + j for j in pairs]
        if s > 0:
            for j in pairs:
                from_state[j] = _mm(stage["x_hi"][idx[j]], st[j].astype(BF16), NN)
        sub = {key: [stage[key][i] for i in idx] for key in ("tinv", "from_v", "lb_r")}
        us, ys_s = _wkv_finish(sub, [from_state[j][:c] for j in pairs], [from_state[j][c:] for j in pairs])
        upd = [_mm(stage["bke"][idx[j]], jnp.concatenate([us[j], v[idx[j]]], axis=0).astype(BF16), TN)
               for j in pairs]
        ys += ys_s
        for j in pairs:
            w_col = jnp.sum(jnp.where(eye_full, jnp.exp(stage["tot"][idx[j]]), 0.0), axis=1, keepdims=True)
            st[j] = w_col * st[j] + jnp.where(same_half, upd[j], 0.0)
    zs = _wkv_gate(ys, r, stage["kh"], v, g, par(rk_ref), par(lng_ref), par(lnb_ref), stage["ones_blk"])
    for i, (s, j) in enumerate(items):
        z_ref[s * c:(s + 1) * c, j * LANES:(j + 1) * LANES] = zs[i]
    for j in pairs:
        st_ref[j] = st[j]
        sout_ref[0, j] = st[j]


def _wkv_chunk_call(r, k, v, a, ld, g, p, nbatch, seq):
    rows = WKV_CHUNK * WKV_CHUNKS_PER_STEP
    nstep = seq // rows
    npair = D_MODEL // LANES
    tile = pl.BlockSpec((rows, D_MODEL), lambda b, j, t: (b * nstep + t, 0))
    par = pl.BlockSpec((1, D_MODEL), lambda b, j, t: (0, 0))
    return pl.pallas_call(
        _wkv_chunk_kernel,
        grid=(nbatch, 1, nstep),
        in_specs=[tile] * 6 + [par] * 5,
        out_specs=[tile, pl.BlockSpec((1, npair, LANES, LANES), lambda b, j, t: (b, 0, 0, 0))],
        out_shape=[jax.ShapeDtypeStruct((nbatch * seq, D_MODEL), BF16),
                   jax.ShapeDtypeStruct((nbatch, npair, LANES, LANES), F32)],
        scratch_shapes=[pltpu.VMEM((npair, LANES, LANES), F32)],
        compiler_params=_cparams(3),
        name="wkv_chunk",
    )(r, k, v, a, ld, g, p["k_k"], p["k_a"], p["r_k"], p["ln_g"], p["ln_b"])


WKV_LANES_UNROLL = 16


def _wkv_lanes_kernel(r_ref, k_ref, v_ref, a_ref, ld_ref, g_ref, kkg_ref, kag_ref, rk_ref, lng_ref, lnb_ref,
                      s_ref, z_ref, sout_ref, prep_ref, y_ref, *, seq_len):
    n = R_HEAD_DIM
    nseq = s_ref.shape[3]
    eye = _iota((LANES, LANES), 0) == _iota((LANES, LANES), 1)

    def column(ref):
        return jnp.sum(jnp.where(eye, ref[...], 0.0), axis=1, keepdims=True)

    kkg, kag, rk, lng, lnb = (column(ref) for ref in (kkg_ref, kag_ref, rk_ref, lng_ref, lnb_ref))

    def token(ref, t):
        return ref[pl.ds(t, nseq, stride=seq_len), :].T

    bonus = []
    for t in range(seq_len):
        r, k, v, a = (token(ref, t) for ref in (r_ref, k_ref, v_ref, a_ref))
        w = jnp.exp(token(ld_ref, t))
        kkx = k * kkg
        kh = k * (1.0 + (a - 1.0) * kag)
        rkk = r * kh * rk
        tiles = []
        for hh in range(2):
            rows = slice(hh * n, (hh + 1) * n)
            nrm = jnp.sqrt(jnp.sum(kkx[rows] * kkx[rows], axis=0, keepdims=True))
            kk = kkx[rows] / jnp.maximum(nrm, 1e-12)
            for q, val in enumerate((w[rows], -kk, kk * a[rows], kh[rows], r[rows], v[rows])):
                prep_ref[q, t, hh] = val
            tiles.append(jnp.sum(rkk[rows], axis=0, keepdims=True) * v[rows])
        bonus.append(tiles)

    for hh in range(2):
        def advance(i, carry, hh=hh):
            for u in range(WKV_LANES_UNROLL):
                vi = i * WKV_LANES_UNROLL + u
                slab = s_ref[hh, vi]
                for t in range(seq_len):
                    w, av, bv, kh, r = (prep_ref[q, t, hh] for q in range(5))
                    vrow = prep_ref[5, t, hh, pl.ds(vi, 1), :]
                    sa = jnp.sum(slab * av, axis=0, keepdims=True)
                    slab = slab * w + sa * bv + vrow * kh
                    y_ref[t, hh, pl.ds(vi, 1), :] = jnp.sum(slab * r, axis=0, keepdims=True)
                sout_ref[hh, vi] = slab
            return carry

        lax.fori_loop(0, n // WKV_LANES_UNROLL, advance, 0)

    for t in range(seq_len):
        parts = []
        for hh in range(2):
            rows = slice(hh * n, (hh + 1) * n)
            y = y_ref[t, hh]
            d = y - jnp.mean(y, axis=0, keepdims=True)
            var = jnp.mean(d * d, axis=0, keepdims=True)
            parts.append(d * lax.rsqrt(var + GN_EPS) * lng[rows] + lnb[rows] + bonus[t][hh])
        g = token(g_ref, t)
        z = jnp.concatenate(parts, axis=0) * (g * _sigmoid(g))
        z_ref[pl.ds(t, nseq, stride=seq_len), :] = z.T


def _wkv_lanes_call(r, k, v, a, ld, g, p, state_hvkb, seq_len):
    n = r.shape[0]
    nseq = state_hvkb.shape[3]
    tile = pl.BlockSpec((n, LANES), lambda j: (0, j))
    par = pl.BlockSpec((1, LANES), lambda j: (0, j))
    sspec = pl.BlockSpec((2, R_HEAD_DIM, R_HEAD_DIM, nseq), lambda j: (j, 0, 0, 0))
    return pl.pallas_call(
        functools.partial(_wkv_lanes_kernel, seq_len=seq_len),
        grid=(D_MODEL // LANES,),
        in_specs=[tile] * 6 + [par] * 5 + [sspec],
        out_specs=[tile, sspec],
        out_shape=[jax.ShapeDtypeStruct((n, D_MODEL), F32),
                   jax.ShapeDtypeStruct(state_hvkb.shape, F32)],
        scratch_shapes=[pltpu.VMEM((6, seq_len, 2, R_HEAD_DIM, nseq), F32),
                        pltpu.VMEM((seq_len, 2, R_HEAD_DIM, nseq), F32)],
        compiler_params=_cparams(1),
        name="wkv_lanes",
    )(r, k, v, a, ld, g, p["k_k"], p["k_a"], p["r_k"], p["ln_g"], p["ln_b"], state_hvkb)


RWKV_OUT_PIECES = 8


def _rwkv_out_kernel(*refs):
    n = RWKV_OUT_PIECES
    z_refs, h_refs, (wout_ref, fg_ref, out_ref) = refs[:n], refs[n:2 * n], refs[2 * n:]
    z = jnp.concatenate([ref[...].astype(BF16) for ref in z_refs], axis=0)
    h = jnp.concatenate([ref[...] for ref in h_refs], axis=0)
    h2 = h + jnp.dot(z, wout_ref[...], preferred_element_type=F32)
    out_ref[...] = _rmsnorm(h2, fg_ref[...])


def _rwkv_out_call(z, h, p, nbatch, ntile, tm, pieces_per_batch, skip):
    piece = tm // RWKV_OUT_PIECES
    dst = lambda b, i: (b * ntile + i, 0)
    pspec = [pl.BlockSpec((piece, D_MODEL),
                          functools.partial(lambda b, i, kk: (b * pieces_per_batch + i * RWKV_OUT_PIECES + skip + kk, 0),
                                            kk=kk))
             for kk in range(RWKV_OUT_PIECES)]
    return pl.pallas_call(
        _rwkv_out_kernel,
        grid=(nbatch, ntile),
        in_specs=pspec + pspec + [pl.BlockSpec((D_MODEL, D_MODEL), lambda b, i: (0, 0)),
                                  pl.BlockSpec((1, D_MODEL), lambda b, i: (0, 0))],
        out_specs=pl.BlockSpec((tm, D_MODEL), dst),
        out_shape=jax.ShapeDtypeStruct((nbatch * ntile * tm, D_MODEL), F32),
        compiler_params=_cparams(2),
        name="rwkv_out",
    )(*([z] * RWKV_OUT_PIECES), *([h] * RWKV_OUT_PIECES), p["w_out"], p["final_gain"])


def _prompt_bucket():
    assert WINDOW == BLOCK
    rel = (np.arange(BLOCK)[:, None] - np.arange(BLOCK)[None, :]) % BLOCK
    return _t5_bucket_np(rel)


def _sample_bucket(keep, t_new, slot):
    t = (np.arange(SUBLANES) % t_new)[:, None]
    j = np.arange(SAMPLE_KEYS)[None, :]
    own = j - keep - slot * t_new
    rel = np.where(j < keep, keep + t - j, t - own)
    ok = (rel >= 0) & (rel < WINDOW) & ((j < keep) | ((own >= 0) & (own < t_new)))
    return np.where(ok, _t5_bucket_np(rel), -1).astype(np.int32)


def kernel(x_prompt, x_sample, cache_win_k, cache_win_v, state_wkv, state_shift, meta_tokens, rel_bias_table, norm_gain, final_gain, attn_w_in, attn_sinks, attn_w_out, rwkv_mu, rwkv_w_in, rwkv_w0, rwkv_w1, rwkv_w2, rwkv_a0, rwkv_a1, rwkv_a2, rwkv_k_k, rwkv_k_a, rwkv_r_k, rwkv_ln_gamma, rwkv_ln_beta, rwkv_w_out):
    nb, seq, _ = x_prompt.shape
    ns, t_new, _ = x_sample.shape
    keep = cache_win_k.shape[2]
    lp = seq + BLOCK
    nblk = lp // BLOCK
    row = lambda x: x.reshape(1, D_MODEL)

    w_in0 = attn_w_in[0].astype(BF16)
    w_out0 = attn_w_out[0].astype(BF16)
    gain0 = row(norm_gain[0])
    sinks = attn_sinks[0]
    rp = dict(gain=row(norm_gain[1]), mu=rwkv_mu[0], w_in=rwkv_w_in[0].astype(BF16),
              w0=row(rwkv_w0[0]), w1=rwkv_w1[0].astype(BF16), w2=rwkv_w2[0].astype(BF16),
              a0=row(rwkv_a0[0]), a1=rwkv_a1[0].astype(BF16), a2=rwkv_a2[0].astype(BF16),
              k_k=row(rwkv_k_k[0]), k_a=row(rwkv_k_a[0]), r_k=row(rwkv_r_k[0]), ln_g=row(rwkv_ln_gamma[0]),
              ln_b=row(rwkv_ln_beta[0]), w_out=rwkv_w_out[0].astype(BF16),
              final_gain=row(final_gain))

    bias_p, *bias_s = _bias_call(rel_bias_table, [_prompt_bucket()]
                                 + [_sample_bucket(keep, t_new, slot) for slot in range(2)])

    head = jnp.concatenate([jnp.zeros((PAD, D_MODEL), F32), meta_tokens.astype(F32)], axis=0)
    xp = x_prompt.reshape(nb * seq, D_MODEL)
    q, kv, g, kvb = _attn_proj_call(xp, head, gain0, w_in0, BF16, nb, lp // ATTN_PROJ_ROWS,
                                    ATTN_PROJ_ROWS // BLOCK, BLOCK)
    h1 = _attn_prompt_call(sinks, q, kvb, g, head, xp, bias_p, w_out0, nb, nblk)
    kv3 = kv.reshape(nb, lp, 2 * A_KV_WIDTH)[:, lp - WINDOW:, :]
    win_k_p = kv3[:, :, :A_KV_WIDTH].reshape(1, nb, WINDOW, A_KV_HEADS, A_HEAD_DIM)
    win_v_p = kv3[:, :, A_KV_WIDTH:].reshape(1, nb, WINDOW, A_KV_HEADS, A_HEAD_DIM)

    shift0 = jnp.zeros((nb, 1, D_MODEL), F32)
    r, k, v, g1, ld, a, xlast = _rwkv_proj_call(h1, shift0, rp, nb, lp // RWKV_PROJ_ROWS, RWKV_PROJ_ROWS, None)
    z, st = _wkv_chunk_call(r, k, v, a, ld, g1, rp, nb, lp)
    y_prompt = _rwkv_out_call(z, h1, rp, nb, seq // RWKV_OUT_ROWS, RWKV_OUT_ROWS,
                              lp * RWKV_OUT_PIECES // RWKV_OUT_ROWS, BLOCK * RWKV_OUT_PIECES // RWKV_OUT_ROWS)
    y_prompt = y_prompt.reshape(nb, seq, D_MODEL)
    st = st.reshape(nb, D_MODEL // LANES, 2, R_HEAD_DIM, 2, R_HEAD_DIM)
    st = jnp.stack([st[:, :, 0, :, 0, :], st[:, :, 1, :, 1, :]], axis=2)
    wkv_p = jnp.swapaxes(st, -1, -2).reshape(1, nb, R_HEADS, R_HEAD_DIM, R_HEAD_DIM)
    shift_p = xlast.reshape(1, nb, D_MODEL)

    xs = x_sample.reshape(ns * t_new, D_MODEL)
    qs, kvs, gs, _ = _attn_proj_call(xs, None, gain0, w_in0, F32, 1, 1, 1, ns * t_new)
    ck = jnp.swapaxes(cache_win_k[0].reshape(ns, keep, A_KV_WIDTH), 1, 2)
    cv = jnp.swapaxes(cache_win_v[0].reshape(ns, keep, A_KV_WIDTH), 1, 2)
    h1s, nk, nv = _attn_sample_call(sinks, qs, kvs, gs, xs, ck, cv, bias_s[0], bias_s[1], w_out0, t_new)
    win_k_s = jnp.swapaxes(nk, 1, 2).reshape(1, ns, keep, A_KV_HEADS, A_HEAD_DIM)
    win_v_s = jnp.swapaxes(nv, 1, 2).reshape(1, ns, keep, A_KV_HEADS, A_HEAD_DIM)

    shift_rows = jnp.repeat(state_shift[0], t_new, axis=0)
    tms = 256
    rs, ks, vs, g1s, lds, as_, xns = _rwkv_proj_call(h1s, shift_rows, rp, 1, ns * t_new // tms, tms, t_new)
    zs, st_s = _wkv_lanes_call(rs, ks, vs, as_, lds, g1s, rp, jnp.transpose(state_wkv[0], (1, 2, 3, 0)), t_new)
    y_sample = _rwkv_out_call(zs, h1s, rp, 1, 1, ns * t_new, RWKV_OUT_PIECES, 0)
    y_sample = y_sample.reshape(ns, t_new, D_MODEL)
    wkv_s = jnp.transpose(st_s, (3, 0, 1, 2))[None]
    shift_s = xns.reshape(ns, t_new, D_MODEL)[:, t_new - 1][None]

    return (y_prompt, y_sample, win_k_p, win_v_p, wkv_p, shift_p, win_k_s, win_v_s, wkv_s, shift_s)
```
